```python
import math
import jax, jax.numpy as jnp
from jax import lax
import numpy as np

D_MODEL = 1024
BATCH = 8
SEQ = 2048
DEPTH = 1

ATT_HEADS = 8
HEAD_DIM = 64
ATT_WIDTH = ATT_HEADS * HEAD_DIM
SSM_WIDTH = D_MODEL // 2
D_MIX = ATT_WIDTH + SSM_WIDTH
SSM_CH = 16
SSM_GROUPS = SSM_WIDTH // SSM_CH
SSM_STATE = 64
DT_MIN = 1e-3
DT_MAX = 1e-1
ROPE_THETA = 500000.0
ROT_DIM = HEAD_DIM // 4
DILATED_CONFIGS = ((128, 1), (512, 4), (2048, 16))
ATT_BLOCK = 128
N_EXPERTS = 256
TOP_K = 8
N_EXPERT_GROUPS = 8
TOPK_GROUPS = 4
EXPERT_FF = D_MODEL // 4
SHARED_FF = EXPERT_FF
ROUTED_SCALE = 2.5
MOE_BLOCK = 128
DEEPNORM_ALPHA = (2 * DEPTH) ** 0.25
DEEPNORM_BETA = (8 * DEPTH) ** -0.25
LN_EPS = 1e-5
RMS_EPS = 1e-6

kernel_name = 'hybrid_dilated_attn_s5_moe_deepnorm'


def layer_norm(x, g, b):
    xf = x.astype(jnp.float32)
    mu = xf.mean(-1, keepdims=True)
    var = jnp.square(xf - mu).mean(-1, keepdims=True)
    return ((xf - mu) * lax.rsqrt(var + LN_EPS) * g.astype(jnp.float32) + b.astype(jnp.float32)).astype(x.dtype)


def rms_norm(x, g):
    xf = x.astype(jnp.float32)
    return (xf * lax.rsqrt(jnp.square(xf).mean(-1, keepdims=True) + RMS_EPS) * g.astype(jnp.float32)).astype(x.dtype)


def rope_tables(seq_len):
    half = ROT_DIM // 2
    inv_freq = jnp.power(jnp.float32(ROPE_THETA), -jnp.arange(half, dtype=jnp.float32) / half)
    ang = jnp.arange(seq_len, dtype=jnp.float32)[:, None] * inv_freq[None, :]
    return jnp.cos(ang), jnp.sin(ang)


def partial_rope(t, cos, sin):
    half = ROT_DIM // 2
    tf = t.astype(jnp.float32)
    c = cos[None, :, None, :]
    s = sin[None, :, None, :]
    x1 = tf[..., :half]
    x2 = tf[..., half:ROT_DIM]
    out = jnp.concatenate([x1 * c - x2 * s, x2 * c + x1 * s, tf[..., ROT_DIM:]], axis=-1)
    return out.astype(t.dtype)


def dilated_branch(q, k, v, window, dilation):
    bsz, seq, heads, hd = q.shape
    w_sub = window // dilation
    n_sub = seq // dilation
    nb = -(-n_sub // ATT_BLOCK)
    pad = nb * ATT_BLOCK - n_sub

    def blocks(t):
        t = t.reshape(bsz, n_sub, dilation, heads, hd).transpose(0, 2, 3, 1, 4)
        t = jnp.pad(t, ((0, 0), (0, 0), (0, 0), (0, pad), (0, 0)))
        return t.reshape(bsz, dilation, heads, nb, ATT_BLOCK, hd)

    def with_prev(t):
        prev = jnp.pad(t, ((0, 0), (0, 0), (0, 0), (1, 0), (0, 0), (0, 0)))[:, :, :, :-1]
        return jnp.concatenate([prev, t], axis=4)

    qb = blocks(q)
    kk = with_prev(blocks(k))
    vv = with_prev(blocks(v))
    s = jnp.einsum('bdhnqe,bdhnke->bdhnqk', qb, kk, preferred_element_type=jnp.float32) * (hd ** -0.5)
    i = jnp.arange(ATT_BLOCK)[:, None]
    j = jnp.arange(2 * ATT_BLOCK)[None, :]
    dist = i + ATT_BLOCK - j
    band = (dist >= 0) & (dist <= w_sub)
    has_prev = (jnp.arange(nb) > 0)[:, None, None] | (j >= ATT_BLOCK)[None]
    valid = band[None] & has_prev
    s = jnp.where(valid, s, -jnp.inf)
    m = s.max(-1, keepdims=True)
    p = jnp.exp(s - m)
    den = p.sum(-1, keepdims=True)
    o = jnp.einsum('bdhnqk,bdhnke->bdhnqe', p, vv.astype(jnp.float32)) / den
    lse = (m + jnp.log(den))[..., 0]
    o = o.reshape(bsz, dilation, heads, nb * ATT_BLOCK, hd)[:, :, :, :n_sub]
    o = o.transpose(0, 3, 1, 2, 4).reshape(bsz, seq, heads, hd)
    lse = lse.reshape(bsz, dilation, heads, nb * ATT_BLOCK)[..., :n_sub]
    lse = lse.transpose(0, 3, 1, 2).reshape(bsz, seq, heads)
    return o, lse


def dilated_attention(q, k, v):
    outs = []
    lses = []
    for window, dilation in DILATED_CONFIGS:
        o, lse = dilated_branch(q, k, v, window, dilation)
        outs.append(o)
        lses.append(lse)
    w = jax.nn.softmax(jnp.stack(lses), axis=0)
    o = jnp.einsum('cblh,cblhe->blhe', w, jnp.stack(outs))
    return o.astype(q.dtype)


def s5_mixer(u, lam_re, lam_im, log_step, b_re, b_im, c_re, c_im, d_skip, w_glu, b_glu):
    bsz, seq, _ = u.shape
    f32 = jnp.float32
    uf = u.astype(f32).reshape(bsz, seq, SSM_GROUPS, SSM_CH)
    lam = lax.complex(lam_re.astype(f32), lam_im.astype(f32))
    step = jnp.exp(log_step.astype(f32))[:, None]
    lam_bar = jnp.exp(lam * step)
    bmat = lax.complex(b_re.astype(f32), b_im.astype(f32))
    b_bar = ((lam_bar - 1.0) / lam)[..., None] * bmat
    bu = jnp.einsum('gpc,blgc->blgp', b_bar, uf.astype(jnp.complex64))
    a = jnp.broadcast_to(lam_bar, bu.shape)

    def combine(left, right):
        a1, b1 = left
        a2, b2 = right
        return a1 * a2, a2 * b1 + b2

    _, states = lax.associative_scan(combine, (a, bu), axis=1)
    cmat = lax.complex(c_re.astype(f32), c_im.astype(f32))
    y = jnp.einsum('gcp,blgp->blgc', cmat, states).real + d_skip.astype(f32) * uf
    y = jax.nn.gelu(y.reshape(bsz, seq, SSM_WIDTH)).astype(u.dtype)
    return y * jax.nn.sigmoid(y @ w_glu + b_glu)


def hybrid_mixer(h, cos, sin, w_in, att_norm_g, lam_re, lam_im, log_step, b_re, b_im, c_re, c_im,
                 d_skip, w_glu, b_glu, ssm_norm_g, w_out):
    bsz, seq, _ = h.shape
    proj = h @ w_in
    q, k, v, u = jnp.split(proj, [ATT_WIDTH, 2 * ATT_WIDTH, 3 * ATT_WIDTH], axis=-1)
    q = partial_rope(q.reshape(bsz, seq, ATT_HEADS, HEAD_DIM), cos, sin)
    k = partial_rope(k.reshape(bsz, seq, ATT_HEADS, HEAD_DIM), cos, sin)
    v = v.reshape(bsz, seq, ATT_HEADS, HEAD_DIM)
    o_att = dilated_attention(q, k, v).reshape(bsz, seq, ATT_WIDTH)
    o_ssm = s5_mixer(u, lam_re, lam_im, log_step, b_re, b_im, c_re, c_im, d_skip, w_glu, b_glu)
    y = jnp.concatenate([rms_norm(o_att, att_norm_g), rms_norm(o_ssm, ssm_norm_g)], axis=-1)
    return y @ w_out


def routed_experts(h, top_e, gate, w_gu, w_down):
    n_tok, d = h.shape
    nk = n_tok * TOP_K
    flat_e = top_e.reshape(-1).astype(jnp.int32)
    flat_tok = jnp.repeat(jnp.arange(n_tok, dtype=jnp.int32), TOP_K)
    flat_w = gate.reshape(-1)
    order = jnp.argsort(flat_e)
    e_s = flat_e[order]
    tok_s = flat_tok[order]
    w_s = flat_w[order]
    counts = jnp.bincount(flat_e, length=N_EXPERTS).astype(jnp.int32)
    start = jnp.cumsum(counts) - counts
    padded = (counts + MOE_BLOCK - 1) // MOE_BLOCK * MOE_BLOCK
    pad_end = jnp.cumsum(padded)
    pad_start = pad_end - padded
    dest = pad_start[e_s] + (jnp.arange(nk, dtype=jnp.int32) - start[e_s])
    total = -(-(nk + N_EXPERTS * MOE_BLOCK) // MOE_BLOCK) * MOE_BLOCK
    n_blocks = total // MOE_BLOCK
    tok_buf = jnp.full((total,), n_tok, jnp.int32).at[dest].set(tok_s)
    w_buf = jnp.zeros((total,), h.dtype).at[dest].set(w_s)
    block_e = jnp.minimum(jnp.searchsorted(pad_end, jnp.arange(n_blocks, dtype=jnp.int32) * MOE_BLOCK, side='right'),
                          N_EXPERTS - 1).astype(jnp.int32)
    h_pad = jnp.concatenate([h, jnp.zeros((1, d), h.dtype)], axis=0)

    def one_block(args):
        tok, wgt, e = args
        xb = h_pad[tok]
        g, up = jnp.split(xb @ w_gu[e], 2, axis=-1)
        return ((jax.nn.silu(g) * up) @ w_down[e]) * wgt[:, None]

    y = lax.map(one_block, (tok_buf.reshape(n_blocks, MOE_BLOCK), w_buf.reshape(n_blocks, MOE_BLOCK), block_e))
    y = y.reshape(total, d)
    return jax.ops.segment_sum(y, tok_buf, num_segments=n_tok + 1)[:n_tok]


def moe_ffn(h, router_w, router_bias, w_gu, w_down, shared_w_gu, shared_w_down):
    n_tok = h.shape[0]
    scores = jax.nn.sigmoid(h.astype(jnp.float32) @ router_w.astype(jnp.float32))
    choice = scores + router_bias.astype(jnp.float32)
    grp_score = lax.top_k(choice.reshape(n_tok, N_EXPERT_GROUPS, -1), 2)[0].sum(-1)
    _, top_grp = lax.top_k(grp_score, TOPK_GROUPS)
    grp_mask = jnp.any(top_grp[..., None] == jnp.arange(N_EXPERT_GROUPS)[None, None, :], axis=1)
    masked = jnp.where(jnp.repeat(grp_mask, N_EXPERTS // N_EXPERT_GROUPS, axis=-1), choice, -jnp.inf)
    _, top_e = lax.top_k(masked, TOP_K)
    gate = jnp.take_along_axis(scores, top_e, axis=-1)
    gate = ROUTED_SCALE * gate / (gate.sum(-1, keepdims=True) + 1e-20)
    routed = routed_experts(h, top_e, gate.astype(h.dtype), w_gu, w_down)
    g, up = jnp.split(h @ shared_w_gu, 2, axis=-1)
    shared = (jax.nn.silu(g) * up) @ shared_w_down
    return routed + shared


def setup_inputs(seed: int = 0) -> dict:
    key = jax.random.key(seed)
    ks = jax.random.split(key, 24)
    f32 = jnp.float32

    def nrm(k, shape, scale):
        return scale * jax.random.normal(k, shape, f32)

    G, P, CH = SSM_GROUPS, SSM_STATE, SSM_CH
    x = jax.random.normal(ks[0], (BATCH, SEQ, D_MODEL), f32)
    w_in = nrm(ks[1], (DEPTH, D_MODEL, 3 * ATT_WIDTH + SSM_WIDTH), D_MODEL ** -0.5)
    att_norm_g = 1.0 + nrm(ks[2], (DEPTH, ATT_WIDTH), 0.02)
    lam_re = -0.5 * jnp.exp(nrm(ks[3], (DEPTH, G, P), 0.02))
    lam_im = jnp.broadcast_to(math.pi * jnp.arange(P, dtype=f32), (DEPTH, G, P))
    log_step = jax.random.uniform(ks[4], (DEPTH, G), f32, math.log(DT_MIN), math.log(DT_MAX))
    b_re = nrm(ks[5], (DEPTH, G, P, CH), (2 * CH) ** -0.5)
    b_im = nrm(ks[6], (DEPTH, G, P, CH), (2 * CH) ** -0.5)
    c_re = nrm(ks[7], (DEPTH, G, CH, P), 0.5)
    c_im = nrm(ks[8], (DEPTH, G, CH, P), 0.5)
    d_skip = nrm(ks[9], (DEPTH, G, CH), 1.0)
    w_glu = nrm(ks[10], (DEPTH, SSM_WIDTH, SSM_WIDTH), SSM_WIDTH ** -0.5)
    b_glu = nrm(ks[11], (DEPTH, SSM_WIDTH), 0.02)
    ssm_norm_g = 1.0 + nrm(ks[12], (DEPTH, SSM_WIDTH), 0.02)
    w_out = nrm(ks[13], (DEPTH, D_MIX, D_MODEL), DEEPNORM_BETA * D_MIX ** -0.5)
    ln1_g = 1.0 + nrm(ks[14], (DEPTH, D_MODEL), 0.02)
    ln1_b = nrm(ks[15], (DEPTH, D_MODEL), 0.02)
    router_w = nrm(ks[16], (DEPTH, D_MODEL, N_EXPERTS), D_MODEL ** -0.5)
    router_bias = nrm(ks[17], (DEPTH, N_EXPERTS), 0.01)
    w_gu = nrm(ks[18], (DEPTH, N_EXPERTS, D_MODEL, 2 * EXPERT_FF), D_MODEL ** -0.5)
    w_down = nrm(ks[19], (DEPTH, N_EXPERTS, EXPERT_FF, D_MODEL), DEEPNORM_BETA * EXPERT_FF ** -0.5)
    shared_w_gu = nrm(ks[20], (DEPTH, D_MODEL, 2 * SHARED_FF), D_MODEL ** -0.5)
    shared_w_down = nrm(ks[21], (DEPTH, SHARED_FF, D_MODEL), DEEPNORM_BETA * SHARED_FF ** -0.5)
    ln2_g = 1.0 + nrm(ks[22], (DEPTH, D_MODEL), 0.02)
    ln2_b = nrm(ks[23], (DEPTH, D_MODEL), 0.02)
    return {'x': x, 'w_in': w_in, 'att_norm_g': att_norm_g, 'lam_re': lam_re, 'lam_im': lam_im,
            'log_step': log_step, 'b_re': b_re, 'b_im': b_im, 'c_re': c_re, 'c_im': c_im, 'd_skip': d_skip,
            'w_glu': w_glu, 'b_glu': b_glu, 'ssm_norm_g': ssm_norm_g, 'w_out': w_out, 'ln1_g': ln1_g,
            'ln1_b': ln1_b, 'router_w': router_w, 'router_bias': router_bias, 'w_gu': w_gu, 'w_down': w_down,
            'shared_w_gu': shared_w_gu, 'shared_w_down': shared_w_down, 'ln2_g': ln2_g, 'ln2_b': ln2_b}


def reference(x, w_in, att_norm_g, lam_re, lam_im, log_step, b_re, b_im, c_re, c_im, d_skip, w_glu, b_glu,
              ssm_norm_g, w_out, ln1_g, ln1_b, router_w, router_bias, w_gu, w_down, shared_w_gu,
              shared_w_down, ln2_g, ln2_b):
    bsz, seq, d = x.shape
    cos, sin = rope_tables(seq)
    h = x
    for i in range(DEPTH):
        mix = hybrid_mixer(h, cos, sin, w_in[i], att_norm_g[i], lam_re[i], lam_im[i], log_step[i], b_re[i],
                           b_im[i], c_re[i], c_im[i], d_skip[i], w_glu[i], b_glu[i], ssm_norm_g[i], w_out[i])
        h = layer_norm(DEEPNORM_ALPHA * h + mix, ln1_g[i], ln1_b[i])
        ffn = moe_ffn(h.reshape(bsz * seq, d), router_w[i], router_bias[i], w_gu[i], w_down[i],
                      shared_w_gu[i], shared_w_down[i]).reshape(bsz, seq, d)
        h = layer_norm(DEEPNORM_ALPHA * h + ffn, ln2_g[i], ln2_b[i])
    return h
```

```python
import functools
import math

import jax
import jax.numpy as jnp
from jax import lax
from jax.experimental import pallas as pl
from jax.experimental.pallas import tpu as pltpu

F32 = jnp.float32
BF16 = jnp.bfloat16

ATT_HEADS = 8
HEAD_DIM = 64
ATT_WIDTH = ATT_HEADS * HEAD_DIM
SSM_CH = 16
SSM_STATE = 64
ROPE_THETA = 500000.0
ROT_DIM = HEAD_DIM // 4
DILATIONS = (1, 4, 16)
ATT_BLOCK = 128
N_EXPERTS = 256
TOP_K = 8
N_EXPERT_GROUPS = 8
TOPK_GROUPS = 4
ROUTED_SCALE = 2.5
LN_EPS = 1e-5
RMS_EPS = 1e-6

LANES = 128
SUBLANES = 8
EXPERT_ROWS = 128
NEG_INF = float("-inf")


def _cparams(sem, vmem_mb):
    return pltpu.CompilerParams(dimension_semantics=sem, vmem_limit_bytes=vmem_mb * 1024 * 1024)


def _inproj_kernel(x_ref, w_ref, cos_ref, sa_ref, sb_ref, o_ref, *, n_rot_cols):
    xb = x_ref[...].astype(BF16)
    cosf = cos_ref[...]
    sa = sa_ref[...]
    sb = sb_ref[...]
    width = o_ref.shape[1]
    chunk = 512
    for c in range(width // chunk):
        r = jnp.dot(xb, w_ref[:, c * chunk:(c + 1) * chunk], preferred_element_type=F32)
        if c * chunk < n_rot_cols:
            parts = []
            for s in range(chunk // LANES):
                t = r[:, s * LANES:(s + 1) * LANES]
                parts.append(t * cosf + pltpu.roll(t, LANES - ROT_DIM // 2, 1) * sa
                             + pltpu.roll(t, ROT_DIM // 2, 1) * sb)
            r = jnp.concatenate(parts, axis=1)
        o_ref[:, c * chunk:(c + 1) * chunk] = r


def _rope_lane_tables(seq):
    half = ROT_DIM // 2
    inv_freq = jnp.power(jnp.float32(ROPE_THETA), -jnp.arange(half, dtype=F32) / half)
    ang = jnp.arange(seq, dtype=F32)[:, None] * inv_freq[None, :]
    cos, sin = jnp.cos(ang), jnp.sin(ang)
    rest = HEAD_DIM - ROT_DIM
    cos_h = jnp.concatenate([cos, cos, jnp.ones((seq, rest), F32)], axis=1)
    sa_h = jnp.concatenate([-sin, jnp.zeros((seq, half + rest), F32)], axis=1)
    sb_h = jnp.concatenate([jnp.zeros((seq, half), F32), sin, jnp.zeros((seq, rest), F32)], axis=1)
    rep = LANES // HEAD_DIM
    return tuple(jnp.tile(t, (1, rep)) for t in (cos_h, sa_h, sb_h))


def _inproj(x2d, w_in_bf, seq):
    n_tok, d = x2d.shape
    width = w_in_bf.shape[1]
    tm = 512
    cosf, sa, sb = _rope_lane_tables(seq)
    tab_spec = pl.BlockSpec((tm, LANES), lambda i: (i % (seq // tm), 0))
    return pl.pallas_call(
        functools.partial(_inproj_kernel, n_rot_cols=2 * ATT_WIDTH),
        out_shape=jax.ShapeDtypeStruct((n_tok, width), F32),
        grid=(n_tok // tm,),
        in_specs=[pl.BlockSpec((tm, d), lambda i: (i, 0)),
                  pl.BlockSpec((d, width), lambda i: (0, 0)),
                  tab_spec, tab_spec, tab_spec],
        out_specs=pl.BlockSpec((tm, width), lambda i: (i, 0)),
        compiler_params=_cparams(("parallel",), 48),
        name="inproj",
    )(x2d, w_in_bf, cosf, sa, sb)


def _attn_kernel(q_ref, k_ref, v_ref, o_ref, ob_ref, lb_ref, *, seq):
    blk = ATT_BLOCK
    lane = lax.broadcasted_iota(jnp.int32, (1, LANES), 1)
    head0 = lane < HEAD_DIM
    qi = lax.broadcasted_iota(jnp.int32, (blk, 2 * blk), 0)
    kj = lax.broadcasted_iota(jnp.int32, (blk, 2 * blk), 1)
    dist = qi + blk - kj
    band = (dist >= 0) & (dist <= blk)
    is_cur = kj >= blk
    causal = (lax.broadcasted_iota(jnp.int32, (blk, blk), 0)
              >= lax.broadcasted_iota(jnp.int32, (blk, blk), 1))
    scale = HEAD_DIM ** -0.5

    def rows(start, d):
        if d == 1:
            return pl.ds(start, blk)
        return pl.ds(start, blk, stride=d)

    def one_block(c, d, start_q, start_p, has_prev, with_prev):
        sl_q = rows(start_q, d)
        q = q_ref[sl_q, :]
        kk = k_ref[sl_q, :]
        vv = v_ref[sl_q, :]
        if with_prev:
            sl_p = rows(start_p, d)
            kk = jnp.concatenate([k_ref[sl_p, :], kk], axis=0)
            vv = jnp.concatenate([v_ref[sl_p, :], vv], axis=0)
            valid = band & (is_cur | has_prev)
        else:
            valid = causal
        kk = kk.astype(BF16)
        vv = vv.astype(BF16)
        outs = []
        lses = []
        for h in range(LANES // HEAD_DIM):
            hm = head0 if h == 0 else jnp.logical_not(head0)
            qh = jnp.where(hm, q, 0.0).astype(BF16)
            s = lax.dot_general(qh, kk, (((1,), (1,)), ((), ())), preferred_element_type=F32) * scale
            s = jnp.where(valid, s, NEG_INF)
            m = jnp.max(s, axis=-1, keepdims=True)
            p = jnp.exp(s - m)
            den = jnp.sum(p, axis=-1, keepdims=True)
            outs.append(jnp.dot(p.astype(BF16), vv, preferred_element_type=F32) / den)
            lses.append(m + jnp.log(den))
        ob_ref[c, sl_q, :] = jnp.where(head0, outs[0], outs[1])
        lb_ref[c, sl_q, :] = jnp.where(head0, lses[0], lses[1])

    for c, d in enumerate(DILATIONS):
        n_sub = seq // d
        nb = n_sub // blk
        n_iter = nb * d
        if nb == 1:
            def body1(r, carry, c=c, d=d):
                one_block(c, d, r, r, None, False)
                return carry
            lax.fori_loop(0, n_iter, body1, 0)
        else:
            def body(it, carry, c=c, d=d, nb=nb):
                n = it // d
                r = it - n * d
                start_q = r + n * (d * blk)
                has_prev = n > 0
                start_p = jnp.where(has_prev, start_q - d * blk, start_q)
                one_block(c, d, start_q, start_p, has_prev, True)
                return carry
            lax.fori_loop(0, n_iter, body, 0)

    rc = 256
    def merge(i, carry):
        sl = pl.ds(pl.multiple_of(i * rc, rc), rc)
        l0 = lb_ref[0, sl, :]
        l1 = lb_ref[1, sl, :]
        l2 = lb_ref[2, sl, :]
        mx = jnp.maximum(jnp.maximum(l0, l1), l2)
        e0 = jnp.exp(l0 - mx)
        e1 = jnp.exp(l1 - mx)
        e2 = jnp.exp(l2 - mx)
        tot = e0 + e1 + e2
        o_ref[sl, :] = ((e0 / tot) * ob_ref[0, sl, :] + (e1 / tot) * ob_ref[1, sl, :]
                        + (e2 / tot) * ob_ref[2, sl, :])
        return carry
    lax.fori_loop(0, seq // rc, merge, 0)


def _attention(proj, bsz, seq):
    n_tok = proj.shape[0]
    pairs = ATT_WIDTH // LANES
    assert seq % (ATT_BLOCK * max(DILATIONS)) == 0
    blk = (seq, LANES)
    return pl.pallas_call(
        functools.partial(_attn_kernel, seq=seq),
        out_shape=jax.ShapeDtypeStruct((n_tok, ATT_WIDTH), F32),
        grid=(bsz, pairs),
        in_specs=[pl.BlockSpec(blk, lambda b, h: (b, h)),
                  pl.BlockSpec(blk, lambda b, h: (b, pairs + h)),
                  pl.BlockSpec(blk, lambda b, h: (b, 2 * pairs + h))],
        out_specs=pl.BlockSpec(blk, lambda b, h: (b, h)),
        scratch_shapes=[pltpu.VMEM((len(DILATIONS), seq, LANES), F32),
                        pltpu.VMEM((len(DILATIONS), seq, LANES), F32)],
        compiler_params=_cparams(("parallel", "parallel"), 32),
        name="attn",
    )(proj, proj, proj)


def _s5_kernel(u_ref, bm_ref, lam_ref, cm_ref, dk_ref, o_ref, us_ref, st_ref, ys_ref, carry_ref, *, tc):
    bsz = u_ref.shape[0]
    half = st_ref.shape[1] // 2
    rows = tc * bsz
    mm_rows = 512

    @pl.when(pl.program_id(1) == 0)
    def _():
        carry_ref[...] = jnp.zeros_like(carry_ref)

    for b in range(bsz):
        us_ref[pl.ds(b, tc, stride=bsz), :] = u_ref[b]

    bm = bm_ref[0]
    for r0 in range(0, rows, mm_rows):
        st_ref[r0:r0 + mm_rows, :] = jnp.dot(us_ref[r0:r0 + mm_rows, :].astype(BF16), bm,
                                             preferred_element_type=F32)

    lam = lam_ref[0]
    lam_re = lam[:, :half]
    lam_im = lam[:, half:]

    def step(t, carry):
        xr, xi = carry
        sl = pl.ds(pl.multiple_of(t * bsz, bsz), bsz)
        nr = lam_re * xr - lam_im * xi + st_ref[sl, :half]
        ni = lam_re * xi + lam_im * xr + st_ref[sl, half:]
        st_ref[sl, :half] = nr
        st_ref[sl, half:] = ni
        return nr, ni

    xr, xi = lax.fori_loop(0, tc, step, (carry_ref[:, :half], carry_ref[:, half:]), unroll=4)
    carry_ref[:, :half] = xr
    carry_ref[:, half:] = xi

    cm = cm_ref[0]
    for r0 in range(0, rows, mm_rows):
        ys_ref[r0:r0 + mm_rows, :] = jnp.dot(st_ref[r0:r0 + mm_rows, :].astype(BF16), cm,
                                             preferred_element_type=F32)
    dk = dk_ref[...]
    for b in range(bsz):
        o_ref[b] = ys_ref[pl.ds(b, tc, stride=bsz), :] + dk * u_ref[b]


def _s5_params(lam_re, lam_im, log_step, b_re, b_im, c_re, c_im, bsz):
    groups = lam_re.shape[0]
    gpc = LANES // SSM_CH
    n_chunks = groups // gpc
    lam = lax.complex(lam_re.astype(F32), lam_im.astype(F32))
    step = jnp.exp(log_step.astype(F32))[:, None]
    lam_bar = jnp.exp(lam * step)
    bmat = lax.complex(b_re.astype(F32), b_im.astype(F32))
    b_bar = ((lam_bar - 1.0) / lam)[..., None] * bmat
    eye = jnp.eye(gpc, dtype=F32)

    def block_diag_in(t):
        t = t.reshape(n_chunks, gpc, SSM_STATE, SSM_CH)
        return jnp.einsum('ngpc,gh->ngchp', t, eye).reshape(n_chunks, gpc * SSM_CH, gpc * SSM_STATE)

    def block_diag_out(t):
        t = t.reshape(n_chunks, gpc, SSM_CH, SSM_STATE)
        return jnp.einsum('ngcp,gh->ngphc', t, eye).reshape(n_chunks, gpc * SSM_STATE, gpc * SSM_CH)

    bm = jnp.concatenate([block_diag_in(b_bar.real), block_diag_in(b_bar.imag)], axis=2).astype(BF16)
    cm = jnp.concatenate([block_diag_out(c_re.astype(F32)), block_diag_out(-c_im.astype(F32))],
                         axis=1).astype(BF16)
    lam_row = jnp.concatenate([lam_bar.real.reshape(n_chunks, gpc * SSM_STATE),
                               lam_bar.imag.reshape(n_chunks, gpc * SSM_STATE)], axis=1)
    lam_t = jnp.broadcast_to(lam_row[:, None, :], (n_chunks, bsz, 2 * gpc * SSM_STATE))
    return bm, lam_t, cm, n_chunks


def _s5(proj3, u_col0, lam_re, lam_im, log_step, b_re, b_im, c_re, c_im, d_skip):
    bsz, seq, _ = proj3.shape
    assert bsz == SUBLANES
    bm, lam_t, cm, n_chunks = _s5_params(lam_re, lam_im, log_step, b_re, b_im, c_re, c_im, bsz)
    width = n_chunks * LANES
    tc = 256
    st_cols = bm.shape[2]
    ublk0 = u_col0 // LANES
    return pl.pallas_call(
        functools.partial(_s5_kernel, tc=tc),
        out_shape=jax.ShapeDtypeStruct((bsz, seq, width), F32),
        grid=(n_chunks, seq // tc),
        in_specs=[pl.BlockSpec((bsz, tc, LANES), lambda c, t: (0, t, ublk0 + c)),
                  pl.BlockSpec((1, LANES, st_cols), lambda c, t: (c, 0, 0)),
                  pl.BlockSpec((1, bsz, st_cols), lambda c, t: (c, 0, 0)),
                  pl.BlockSpec((1, st_cols, LANES), lambda c, t: (c, 0, 0)),
                  pl.BlockSpec((1, LANES), lambda c, t: (0, c))],
        out_specs=pl.BlockSpec((bsz, tc, LANES), lambda c, t: (0, t, c)),
        scratch_shapes=[pltpu.VMEM((tc * bsz, LANES), F32),
                        pltpu.VMEM((tc * bsz, st_cols), F32),
                        pltpu.VMEM((tc * bsz, LANES), F32),
                        pltpu.VMEM((bsz, st_cols), F32)],
        compiler_params=_cparams(("arbitrary", "arbitrary"), 40),
        name="s5",
    )(proj3, bm, lam_t, cm, d_skip.reshape(1, width).astype(F32))


def _layer_norm(v, g, b):
    mu = jnp.mean(v, axis=-1, keepdims=True)
    var = jnp.mean(jnp.square(v - mu), axis=-1, keepdims=True)
    return (v - mu) * lax.rsqrt(var + LN_EPS) * g + b


def _rms_norm(v, g):
    return v * lax.rsqrt(jnp.mean(jnp.square(v), axis=-1, keepdims=True) + RMS_EPS) * g


def _mixout_kernel(att_ref, ssm_ref, x_ref, wglu_ref, bglu_ref, ag_ref, sg_ref, wout_ref, g_ref, b_ref,
                   o_ref, *, alpha):
    y = jax.nn.gelu(ssm_ref[...])
    z = jnp.dot(y.astype(BF16), wglu_ref[...], preferred_element_type=F32) + bglu_ref[...]
    o_ssm = y * jax.nn.sigmoid(z)
    a = _rms_norm(att_ref[...], ag_ref[...]).astype(BF16)
    s = _rms_norm(o_ssm, sg_ref[...]).astype(BF16)
    wa = att_ref.shape[1]
    mix = (jnp.dot(a, wout_ref[:wa, :], preferred_element_type=F32)
           + jnp.dot(s, wout_ref[wa:, :], preferred_element_type=F32))
    o_ref[...] = _layer_norm(alpha * x_ref[...] + mix, g_ref[...], b_ref[...])


def _mixout(o_att, y_ssm, x2d, w_glu, b_glu, att_g, ssm_g, w_out, ln_g, ln_b, alpha):
    n_tok, d = x2d.shape
    wa = o_att.shape[1]
    ws = y_ssm.shape[1]
    tm = 256
    row = lambda w: pl.BlockSpec((tm, w), lambda i: (i, 0))
    full = lambda a: pl.BlockSpec(a.shape, lambda i: (0,) * a.ndim)
    args = (o_att, y_ssm, x2d, w_glu.astype(BF16), b_glu.reshape(1, ws), att_g.reshape(1, wa),
            ssm_g.reshape(1, ws), w_out.astype(BF16), ln_g.reshape(1, d), ln_b.reshape(1, d))
    return pl.pallas_call(
        functools.partial(_mixout_kernel, alpha=alpha),
        out_shape=jax.ShapeDtypeStruct((n_tok, d), F32),
        grid=(n_tok // tm,),
        in_specs=[row(wa), row(ws), row(d)] + [full(a) for a in args[3:]],
        out_specs=row(d),
        compiler_params=_cparams(("parallel",), 32),
        name="mixout",
    )(*args)


def _split_bf16(v):
    hi = v.astype(BF16)
    lo = (v - hi.astype(F32)).astype(BF16)
    return hi, lo


def _router_kernel(h_ref, wt_ref, bias_ref, e_ref, g_ref, r_ref, cnt_ref, run_ref):
    tm = h_ref.shape[0]
    n_exp = wt_ref.shape[0]
    gsz = n_exp // N_EXPERT_GROUPS

    @pl.when(pl.program_id(0) == 0)
    def _():
        run_ref[...] = jnp.zeros_like(run_ref)

    w_hi, w_lo = _split_bf16(wt_ref[...])
    h_hi, h_lo = _split_bf16(h_ref[...])
    nt = (((1,), (1,)), ((), ()))
    logits = (lax.dot_general(w_hi, h_hi, nt, preferred_element_type=F32)
              + lax.dot_general(w_hi, h_lo, nt, preferred_element_type=F32)
              + lax.dot_general(w_lo, h_hi, nt, preferred_element_type=F32))
    scores = jax.nn.sigmoid(logits)
    choice = scores + bias_ref[:, 0:1]

    gio = lax.broadcasted_iota(jnp.int32, (gsz, tm), 0).astype(F32)
    gscore = []
    for g in range(N_EXPERT_GROUPS):
        cg = choice[g * gsz:(g + 1) * gsz, :]
        m1 = jnp.max(cg, axis=0, keepdims=True)
        i1 = jnp.min(jnp.where(cg == m1, gio, float(gsz)), axis=0, keepdims=True)
        m2 = jnp.max(jnp.where(gio == i1, NEG_INF, cg), axis=0, keepdims=True)
        gscore.append(m1 + m2)
    masked = []
    for g in range(N_EXPERT_GROUPS):
        beat = jnp.zeros((1, tm), F32)
        for o in range(N_EXPERT_GROUPS):
            if o == g:
                continue
            wins = (gscore[o] >= gscore[g]) if o < g else (gscore[o] > gscore[g])
            beat = beat + jnp.where(wins, 1.0, 0.0)
        keep = beat < float(TOPK_GROUPS)
        masked.append(jnp.where(keep, choice[g * gsz:(g + 1) * gsz, :], NEG_INF))
    cur = jnp.concatenate(masked, axis=0)

    eio = lax.broadcasted_iota(jnp.int32, (n_exp, tm), 0).astype(F32)
    idxs = []
    gates = []
    onehot = jnp.zeros((n_exp, tm), F32)
    for _ in range(TOP_K):
        m = jnp.max(cur, axis=0, keepdims=True)
        idx = jnp.min(jnp.where(cur == m, eio, float(n_exp)), axis=0, keepdims=True)
        hit = eio == idx
        idxs.append(idx)
        gates.append(jnp.sum(jnp.where(hit, scores, 0.0), axis=0, keepdims=True))
        cur = jnp.where(hit, NEG_INF, cur)
        onehot = onehot + jnp.where(hit, 1.0, 0.0)
    gate = jnp.concatenate(gates, axis=0)
    gate = ROUTED_SCALE * gate / (jnp.sum(gate, axis=0, keepdims=True) + 1e-20)

    si = lax.broadcasted_iota(jnp.int32, (tm, tm), 0)
    ti = lax.broadcasted_iota(jnp.int32, (tm, tm), 1)
    upper = jnp.where(si < ti, 1.0, 0.0).astype(BF16)
    before = jnp.dot(onehot.astype(BF16), upper, preferred_element_type=F32) + run_ref[:, 0:1]
    ranks = [jnp.sum(jnp.where(eio == idx, before, 0.0), axis=0, keepdims=True) for idx in idxs]

    e_ref[...] = jnp.concatenate(idxs, axis=0).astype(jnp.int32)
    g_ref[...] = gate
    r_ref[...] = jnp.concatenate(ranks, axis=0).astype(jnp.int32)
    run_ref[...] = run_ref[...] + jnp.sum(onehot, axis=1, keepdims=True)
    cnt_ref[...] = run_ref[...]


def _router(h, router_w, router_bias):
    n_tok, d = h.shape
    n_exp = router_w.shape[1]
    tm = 256
    wt = router_w.astype(F32).T
    bias = jnp.broadcast_to(router_bias.astype(F32)[:, None], (n_exp, LANES))
    tok = pl.BlockSpec((TOP_K, tm), lambda i: (0, i))
    return pl.pallas_call(
        _router_kernel,
        out_shape=(jax.ShapeDtypeStruct((TOP_K, n_tok), jnp.int32),
                   jax.ShapeDtypeStruct((TOP_K, n_tok), F32),
                   jax.ShapeDtypeStruct((TOP_K, n_tok), jnp.int32),
                   jax.ShapeDtypeStruct((n_exp, LANES), F32)),
        grid=(n_tok // tm,),
        in_specs=[pl.BlockSpec((tm, d), lambda i: (i, 0)),
                  pl.BlockSpec((n_exp, d), lambda i: (0, 0)),
                  pl.BlockSpec((n_exp, LANES), lambda i: (0, 0))],
        out_specs=(tok, tok, tok, pl.BlockSpec((n_exp, LANES), lambda i: (0, 0))),
        scratch_shapes=[pltpu.VMEM((n_exp, LANES), F32)],
        compiler_params=_cparams(("arbitrary",), 32),
        name="router",
    )(h, wt, bias)


def _dispatch_kernel(dest_ref, h_ref, xs_ref, sem):
    tm = h_ref.shape[0]

    def issue(t, carry):
        for k in range(TOP_K):
            pltpu.make_async_copy(h_ref.at[pl.ds(t, 1)], xs_ref.at[pl.ds(dest_ref[k, t], 1)], sem).start()
        return carry
    lax.fori_loop(0, tm, issue, 0)

    def drain(t, carry):
        for k in range(TOP_K):
            pltpu.make_async_copy(h_ref.at[pl.ds(0, 1)], xs_ref.at[pl.ds(0, 1)], sem).wait()
        return carry
    lax.fori_loop(0, tm, drain, 0)


def _dispatch(h, dest):
    n_tok, d = h.shape
    tm = 256
    return pl.pallas_call(
        _dispatch_kernel,
        out_shape=jax.ShapeDtypeStruct((n_tok * TOP_K, d), h.dtype),
        grid=(n_tok // tm,),
        in_specs=[pl.BlockSpec((TOP_K, tm), lambda i: (0, i), memory_space=pltpu.SMEM),
                  pl.BlockSpec((tm, d), lambda i: (i, 0))],
        out_specs=pl.BlockSpec(memory_space=pl.ANY),
        scratch_shapes=[pltpu.SemaphoreType.DMA(())],
        compiler_params=_cparams(("arbitrary",), 32),
        name="dispatch",
    )(dest, h)


def _experts_kernel(vblk_ref, vexp_ref, vlo_ref, vhi_ref, nvis_ref, x_ref, wgu_ref, wdn_ref, y_ref):
    v = pl.program_id(0)
    prev = jnp.maximum(v - 1, 0)
    first = (v == 0) | (vblk_ref[v] != vblk_ref[prev])

    @pl.when(first)
    def _():
        y_ref[...] = jnp.zeros_like(y_ref)

    @pl.when(v < nvis_ref[0])
    def _():
        ff = wdn_ref.shape[1]
        xb = x_ref[...].astype(BF16)
        gu = jnp.dot(xb, wgu_ref[0].astype(BF16), preferred_element_type=F32)
        act = (jax.nn.silu(gu[:, :ff]) * gu[:, ff:]).astype(BF16)
        y = jnp.dot(act, wdn_ref[0].astype(BF16), preferred_element_type=F32)
        row = lax.broadcasted_iota(jnp.int32, (y.shape[0], 1), 0)
        mine = (row >= vlo_ref[v]) & (row < vhi_ref[v])
        y_ref[...] += jnp.where(mine, y, 0.0)


def _visit_list(counts, n_rows):
    n_exp = counts.shape[0]
    n_blocks = n_rows // EXPERT_ROWS
    n_vis_max = n_blocks + n_exp
    ends = jnp.cumsum(counts)
    starts = ends - counts
    fb = starts // EXPERT_ROWS
    lb = jnp.maximum(ends - 1, 0) // EXPERT_ROWS
    nv = jnp.where(counts > 0, lb - fb + 1, 0)
    vend = jnp.cumsum(nv)
    voff = vend - nv
    n_vis = vend[-1]
    v = jnp.arange(n_vis_max, dtype=jnp.int32)
    vc = jnp.minimum(v, n_vis - 1)
    e = jnp.searchsorted(vend, vc, side='right').astype(jnp.int32)
    blk = fb[e] + (vc - voff[e])
    lo = jnp.maximum(starts[e], blk * EXPERT_ROWS) - blk * EXPERT_ROWS
    hi = jnp.minimum(ends[e], (blk + 1) * EXPERT_ROWS) - blk * EXPERT_ROWS
    live = v < n_vis
    lo = jnp.where(live, lo, 0)
    hi = jnp.where(live, hi, 0)
    i32 = lambda a: a.astype(jnp.int32)
    return i32(blk), i32(e), i32(lo), i32(hi), i32(n_vis).reshape(1), n_vis_max


def _experts(xs, counts, w_gu, w_down):
    n_rows, d = xs.shape
    ff2 = w_gu.shape[2]
    ff = w_down.shape[1]
    vblk, vexp, vlo, vhi, nvis, n_vis_max = _visit_list(counts, n_rows)
    grid_spec = pltpu.PrefetchScalarGridSpec(
        num_scalar_prefetch=5,
        grid=(n_vis_max,),
        in_specs=[pl.BlockSpec((EXPERT_ROWS, d), lambda v, b, e, lo, hi, n: (b[v], 0)),
                  pl.BlockSpec((1, d, ff2), lambda v, b, e, lo, hi, n: (e[v], 0, 0)),
                  pl.BlockSpec((1, ff, d), lambda v, b, e, lo, hi, n: (e[v], 0, 0))],
        out_specs=pl.BlockSpec((EXPERT_ROWS, d), lambda v, b, e, lo, hi, n: (b[v], 0)),
    )
    return pl.pallas_call(
        _experts_kernel,
        out_shape=jax.ShapeDtypeStruct((n_rows, d), F32),
        grid_spec=grid_spec,
        compiler_params=_cparams(("arbitrary",), 32),
        name="experts",
    )(vblk, vexp, vlo, vhi, nvis, xs, w_gu, w_down)


def _combine_kernel(dest_ref, gate_ref, h_ref, ys_ref, wgu_ref, wdn_ref, g_ref, b_ref, o_ref, buf_ref, sem,
                    *, alpha):
    tm = h_ref.shape[0]

    def issue(t, carry):
        for k in range(TOP_K):
            pltpu.make_async_copy(ys_ref.at[pl.ds(dest_ref[k, t], 1)], buf_ref.at[k, pl.ds(t, 1)], sem).start()
        return carry
    lax.fori_loop(0, tm, issue, 0)

    h = h_ref[...]
    ff = wdn_ref.shape[0]
    gu = jnp.dot(h.astype(BF16), wgu_ref[...], preferred_element_type=F32)
    act = (jax.nn.silu(gu[:, :ff]) * gu[:, ff:]).astype(BF16)
    acc = alpha * h + jnp.dot(act, wdn_ref[...], preferred_element_type=F32)

    def drain(t, carry):
        for k in range(TOP_K):
            pltpu.make_async_copy(ys_ref.at[pl.ds(0, 1)], buf_ref.at[0, pl.ds(0, 1)], sem).wait()
        return carry
    lax.fori_loop(0, tm, drain, 0)

    gate = gate_ref[...]
    for k in range(TOP_K):
        acc = acc + gate[:, k:k + 1] * buf_ref[k]
    o_ref[...] = _layer_norm(acc, g_ref[...], b_ref[...])


def _combine(h, ys, dest, gate_t, shared_w_gu, shared_w_down, ln_g, ln_b, alpha):
    n_tok, d = h.shape
    tm = 256
    full = lambda a: pl.BlockSpec(a.shape, lambda i: (0,) * a.ndim)
    wgu = shared_w_gu.astype(BF16)
    wdn = shared_w_down.astype(BF16)
    g2 = ln_g.reshape(1, d)
    b2 = ln_b.reshape(1, d)
    return pl.pallas_call(
        functools.partial(_combine_kernel, alpha=alpha),
        out_shape=jax.ShapeDtypeStruct((n_tok, d), F32),
        grid=(n_tok // tm,),
        in_specs=[pl.BlockSpec((TOP_K, tm), lambda i: (0, i), memory_space=pltpu.SMEM),
                  pl.BlockSpec((tm, TOP_K), lambda i: (i, 0)),
                  pl.BlockSpec((tm, d), lambda i: (i, 0)),
                  pl.BlockSpec(memory_space=pl.ANY),
                  full(wgu), full(wdn), full(g2), full(b2)],
        out_specs=pl.BlockSpec((tm, d), lambda i: (i, 0)),
        scratch_shapes=[pltpu.VMEM((TOP_K, tm, d), F32), pltpu.SemaphoreType.DMA(())],
        compiler_params=_cparams(("arbitrary",), 40),
        name="combine",
    )(dest, gate_t, h, ys, wgu, wdn, g2, b2)


def _moe(h, router_w, router_bias, w_gu, w_down, shared_w_gu, shared_w_down, ln_g, ln_b, alpha):
    top_e, gate, rank, cnt = _router(h, router_w, router_bias)
    counts = cnt[:, 0].astype(jnp.int32)
    starts = jnp.cumsum(counts) - counts
    dest = starts[top_e] + rank
    xs = _dispatch(h, dest)
    ys = _experts(xs, counts, w_gu, w_down)
    return _combine(h, ys, dest, gate.T, shared_w_gu, shared_w_down, ln_g, ln_b, alpha)


def kernel(x, w_in, att_norm_g, lam_re, lam_im, log_step, b_re, b_im, c_re, c_im, d_skip, w_glu, b_glu,
           ssm_norm_g, w_out, ln1_g, ln1_b, router_w, router_bias, w_gu, w_down, shared_w_gu,
           shared_w_down, ln2_g, ln2_b):
    bsz, seq, d = x.shape
    depth = w_in.shape[0]
    alpha = (2 * depth) ** 0.25
    h = x.reshape(bsz * seq, d)
    for i in range(depth):
        proj = _inproj(h, w_in[i].astype(BF16), seq)
        o_att = _attention(proj, bsz, seq)
        y_ssm = _s5(proj.reshape(bsz, seq, -1), 3 * ATT_WIDTH, lam_re[i], lam_im[i], log_step[i],
                    b_re[i], b_im[i], c_re[i], c_im[i], d_skip[i])
        h = _mixout(o_att, y_ssm.reshape(bsz * seq, -1), h, w_glu[i], b_glu[i], att_norm_g[i],
                    ssm_norm_g[i], w_out[i], ln1_g[i], ln1_b[i], alpha)
        h = _moe(h, router_w[i], router_bias[i], w_gu[i], w_down[i], shared_w_gu[i], shared_w_down[i],
                 ln2_g[i], ln2_b[i], alpha)
    return h.reshape(bsz, seq, d)
```

```python
import functools
import math

import jax
import jax.numpy as jnp
from jax import lax
from jax.experimental import pallas as pl
from jax.experimental.pallas import tpu as pltpu

F32 = jnp.float32
BF16 = jnp.bfloat16

ATT_HEADS = 8
HEAD_DIM = 64
ATT_WIDTH = ATT_HEADS * HEAD_DIM
SSM_CH = 16
SSM_STATE = 64
ROPE_THETA = 500000.0
ROT_DIM = HEAD_DIM // 4
DILATIONS = (1, 4, 16)
ATT_BLOCK = 128
ATT_GROUP = 4
N_EXPERTS = 256
TOP_K = 8
N_EXPERT_GROUPS = 8
TOPK_GROUPS = 4
ROUTED_SCALE = 2.5
LN_EPS = 1e-5
RMS_EPS = 1e-6

LANES = 128
SUBLANES = 8
EXPERT_ROWS = 128
NEG_INF = float("-inf")


def _cparams(sem, vmem_mb):
    return pltpu.CompilerParams(dimension_semantics=sem, vmem_limit_bytes=vmem_mb * 1024 * 1024)


def _inproj_kernel(x_ref, w_ref, cos_ref, sa_ref, sb_ref, o_ref, *, n_rot_cols):
    xb = x_ref[...].astype(BF16)
    cosf = cos_ref[...]
    sa = sa_ref[...]
    sb = sb_ref[...]
    width = o_ref.shape[1]
    chunk = 512
    for c in range(width // chunk):
        r = jnp.dot(xb, w_ref[:, c * chunk:(c + 1) * chunk], preferred_element_type=F32)
        if c * chunk < n_rot_cols:
            parts = []
            for s in range(chunk // LANES):
                t = r[:, s * LANES:(s + 1) * LANES]
                parts.append(t * cosf + pltpu.roll(t, LANES - ROT_DIM // 2, 1) * sa
                             + pltpu.roll(t, ROT_DIM // 2, 1) * sb)
            r = jnp.concatenate(parts, axis=1)
        o_ref[:, c * chunk:(c + 1) * chunk] = r


def _rope_lane_tables(seq):
    half = ROT_DIM // 2
    inv_freq = jnp.power(jnp.float32(ROPE_THETA), -jnp.arange(half, dtype=F32) / half)
    ang = jnp.arange(seq, dtype=F32)[:, None] * inv_freq[None, :]
    cos, sin = jnp.cos(ang), jnp.sin(ang)
    rest = HEAD_DIM - ROT_DIM
    cos_h = jnp.concatenate([cos, cos, jnp.ones((seq, rest), F32)], axis=1)
    sa_h = jnp.concatenate([-sin, jnp.zeros((seq, half + rest), F32)], axis=1)
    sb_h = jnp.concatenate([jnp.zeros((seq, half), F32), sin, jnp.zeros((seq, rest), F32)], axis=1)
    rep = LANES // HEAD_DIM
    return tuple(jnp.tile(t, (1, rep)) for t in (cos_h, sa_h, sb_h))


def _inproj(x2d, w_in_bf, seq):
    n_tok, d = x2d.shape
    width = w_in_bf.shape[1]
    tm = 512
    cosf, sa, sb = _rope_lane_tables(seq)
    tab_spec = pl.BlockSpec((tm, LANES), lambda i: (i % (seq // tm), 0))
    return pl.pallas_call(
        functools.partial(_inproj_kernel, n_rot_cols=2 * ATT_WIDTH),
        out_shape=jax.ShapeDtypeStruct((n_tok, width), F32),
        grid=(n_tok // tm,),
        in_specs=[pl.BlockSpec((tm, d), lambda i: (i, 0)),
                  pl.BlockSpec((d, width), lambda i: (0, 0)),
                  tab_spec, tab_spec, tab_spec],
        out_specs=pl.BlockSpec((tm, width), lambda i: (i, 0)),
        compiler_params=_cparams(("parallel",), 48),
        name="inproj",
    )(x2d, w_in_bf, cosf, sa, sb)


def _attn_kernel(q_ref, k_ref, v_ref, o_ref, ob_ref, lb_ref, band_ref, causal_ref, *, seq):
    blk = ATT_BLOCK
    lane = lax.broadcasted_iota(jnp.int32, (1, LANES), 1)
    head0 = lane < HEAD_DIM
    scale = HEAD_DIM ** -0.5

    qi = lax.broadcasted_iota(jnp.int32, (blk, 2 * blk), 0)
    kj = lax.broadcasted_iota(jnp.int32, (blk, 2 * blk), 1)
    dist = qi + blk - kj
    band_ref[...] = jnp.where((dist >= 0) & (dist <= blk), 0.0, NEG_INF)
    causal_ref[...] = jnp.where(lax.broadcasted_iota(jnp.int32, (blk, blk), 0)
                                >= lax.broadcasted_iota(jnp.int32, (blk, blk), 1), 0.0, NEG_INF)

    def rows(start, d):
        if d == 1:
            return pl.ds(start, blk)
        return pl.ds(start, blk, stride=d)

    def one_block(c, d, start_q, with_prev):
        sl_q = rows(start_q, d)
        q = q_ref[sl_q, :] * scale
        kk = k_ref[sl_q, :]
        vv = v_ref[sl_q, :]
        if with_prev:
            sl_p = rows(start_q - d * blk, d)
            kk = jnp.concatenate([k_ref[sl_p, :], kk], axis=0)
            vv = jnp.concatenate([v_ref[sl_p, :], vv], axis=0)
            bias_ref = band_ref
        else:
            bias_ref = causal_ref
        kk = kk.astype(BF16)
        vv = vv.astype(BF16)
        outs = []
        lses = []
        for h in range(LANES // HEAD_DIM):
            hm = head0 if h == 0 else jnp.logical_not(head0)
            qh = jnp.where(hm, q, 0.0).astype(BF16)
            s = lax.dot_general(qh, kk, (((1,), (1,)), ((), ())), preferred_element_type=F32)
            s = s + bias_ref[...]
            m = jnp.max(s, axis=-1, keepdims=True)
            p = jnp.exp(s - m)
            den = jnp.sum(p, axis=-1, keepdims=True)
            outs.append(jnp.dot(p.astype(BF16), vv, preferred_element_type=F32) / den)
            lses.append(m + jnp.log(den))
        ob_ref[c, sl_q, :] = jnp.where(head0, outs[0], outs[1])
        lb_ref[c, sl_q, :] = jnp.where(head0, lses[0], lses[1])

    def run_blocks(n_blocks, group, fn):
        assert n_blocks % group == 0
        if n_blocks == group:
            for g in range(group):
                fn(g)
            return
        def body(it, carry):
            for g in range(group):
                fn(it * group + g)
            return carry
        lax.fori_loop(0, n_blocks // group, body, 0)

    for c, d in enumerate(DILATIONS):
        nb = seq // d // blk
        run_blocks(d, min(d, ATT_GROUP), lambda r, c=c, d=d: one_block(c, d, r, False))
        rest = (nb - 1) * d
        if rest:
            def later(idx, c=c, d=d):
                n = idx // d
                r = idx - n * d
                one_block(c, d, r + (n + 1) * (d * blk), True)
            group = max(g for g in range(1, ATT_GROUP + 1) if rest % g == 0)
            run_blocks(rest, group, later)

    rc = 256
    def merge(i, carry):
        sl = pl.ds(pl.multiple_of(i * rc, rc), rc)
        l0 = lb_ref[0, sl, :]
        l1 = lb_ref[1, sl, :]
        l2 = lb_ref[2, sl, :]
        mx = jnp.maximum(jnp.maximum(l0, l1), l2)
        e0 = jnp.exp(l0 - mx)
        e1 = jnp.exp(l1 - mx)
        e2 = jnp.exp(l2 - mx)
        tot = e0 + e1 + e2
        o_ref[sl, :] = ((e0 / tot) * ob_ref[0, sl, :] + (e1 / tot) * ob_ref[1, sl, :]
                        + (e2 / tot) * ob_ref[2, sl, :])
        return carry
    lax.fori_loop(0, seq // rc, merge, 0)


def _attention(proj, bsz, seq):
    n_tok = proj.shape[0]
    pairs = ATT_WIDTH // LANES
    assert seq % (ATT_BLOCK * max(DILATIONS)) == 0
    blk = (seq, LANES)
    return pl.pallas_call(
        functools.partial(_attn_kernel, seq=seq),
        out_shape=jax.ShapeDtypeStruct((n_tok, ATT_WIDTH), F32),
        grid=(bsz, pairs),
        in_specs=[pl.BlockSpec(blk, lambda b, h: (b, h)),
                  pl.BlockSpec(blk, lambda b, h: (b, pairs + h)),
                  pl.BlockSpec(blk, lambda b, h: (b, 2 * pairs + h))],
        out_specs=pl.BlockSpec(blk, lambda b, h: (b, h)),
        scratch_shapes=[pltpu.VMEM((len(DILATIONS), seq, LANES), F32),
                        pltpu.VMEM((len(DILATIONS), seq, LANES), F32),
                        pltpu.VMEM((ATT_BLOCK, 2 * ATT_BLOCK), F32),
                        pltpu.VMEM((ATT_BLOCK, ATT_BLOCK), F32)],
        compiler_params=_cparams(("parallel", "parallel"), 32),
        name="attn",
    )(proj, proj, proj)


def _s5_kernel(u_ref, bm_ref, lam_ref, cm_ref, dk_ref, o_ref, us_ref, st_ref, ys_ref, carry_ref, *, tc):
    bsz = u_ref.shape[0]
    half = st_ref.shape[1] // 2
    rows = tc * bsz
    mm_rows = 512

    @pl.when(pl.program_id(1) == 0)
    def _():
        carry_ref[...] = jnp.zeros_like(carry_ref)

    for b in range(bsz):
        us_ref[pl.ds(b, tc, stride=bsz), :] = u_ref[b]

    bm = bm_ref[0]
    for r0 in range(0, rows, mm_rows):
        st_ref[r0:r0 + mm_rows, :] = jnp.dot(us_ref[r0:r0 + mm_rows, :].astype(BF16), bm,
                                             preferred_element_type=F32)

    lam = lam_ref[0]
    lam_re = lam[:, :half]
    lam_im = lam[:, half:]

    def step(t, carry):
        xr, xi = carry
        sl = pl.ds(pl.multiple_of(t * bsz, bsz), bsz)
        nr = lam_re * xr - lam_im * xi + st_ref[sl, :half]
        ni = lam_re * xi + lam_im * xr + st_ref[sl, half:]
        st_ref[sl, :half] = nr
        st_ref[sl, half:] = ni
        return nr, ni

    xr, xi = lax.fori_loop(0, tc, step, (carry_ref[:, :half], carry_ref[:, half:]), unroll=4)
    carry_ref[:, :half] = xr
    carry_ref[:, half:] = xi

    cm = cm_ref[0]
    for r0 in range(0, rows, mm_rows):
        ys_ref[r0:r0 + mm_rows, :] = jnp.dot(st_ref[r0:r0 + mm_rows, :].astype(BF16), cm,
                                             preferred_element_type=F32)
    dk = dk_ref[...]
    for b in range(bsz):
        o_ref[b] = ys_ref[pl.ds(b, tc, stride=bsz), :] + dk * u_ref[b]


def _s5_params(lam_re, lam_im, log_step, b_re, b_im, c_re, c_im, bsz):
    groups = lam_re.shape[0]
    gpc = LANES // SSM_CH
    n_chunks = groups // gpc
    lam = lax.complex(lam_re.astype(F32), lam_im.astype(F32))
    step = jnp.exp(log_step.astype(F32))[:, None]
    lam_bar = jnp.exp(lam * step)
    bmat = lax.complex(b_re.astype(F32), b_im.astype(F32))
    b_bar = ((lam_bar - 1.0) / lam)[..., None] * bmat
    eye = jnp.eye(gpc, dtype=F32)

    def block_diag_in(t):
        t = t.reshape(n_chunks, gpc, SSM_STATE, SSM_CH)
        return jnp.einsum('ngpc,gh->ngchp', t, eye).reshape(n_chunks, gpc * SSM_CH, gpc * SSM_STATE)

    def block_diag_out(t):
        t = t.reshape(n_chunks, gpc, SSM_CH, SSM_STATE)
        return jnp.einsum('ngcp,gh->ngphc', t, eye).reshape(n_chunks, gpc * SSM_STATE, gpc * SSM_CH)

    bm = jnp.concatenate([block_diag_in(b_bar.real), block_diag_in(b_bar.imag)], axis=2).astype(BF16)
    cm = jnp.concatenate([block_diag_out(c_re.astype(F32)), block_diag_out(-c_im.astype(F32))],
                         axis=1).astype(BF16)
    lam_row = jnp.concatenate([lam_bar.real.reshape(n_chunks, gpc * SSM_STATE),
                               lam_bar.imag.reshape(n_chunks, gpc * SSM_STATE)], axis=1)
    lam_t = jnp.broadcast_to(lam_row[:, None, :], (n_chunks, bsz, 2 * gpc * SSM_STATE))
    return bm, lam_t, cm, n_chunks


def _s5(proj3, u_col0, lam_re, lam_im, log_step, b_re, b_im, c_re, c_im, d_skip):
    bsz, seq, _ = proj3.shape
    assert bsz == SUBLANES
    bm, lam_t, cm, n_chunks = _s5_params(lam_re, lam_im, log_step, b_re, b_im, c_re, c_im, bsz)
    width = n_chunks * LANES
    tc = 256
    st_cols = bm.shape[2]
    ublk0 = u_col0 // LANES
    return pl.pallas_call(
        functools.partial(_s5_kernel, tc=tc),
        out_shape=jax.ShapeDtypeStruct((bsz, seq, width), F32),
        grid=(n_chunks, seq // tc),
        in_specs=[pl.BlockSpec((bsz, tc, LANES), lambda c, t: (0, t, ublk0 + c)),
                  pl.BlockSpec((1, LANES, st_cols), lambda c, t: (c, 0, 0)),
                  pl.BlockSpec((1, bsz, st_cols), lambda c, t: (c, 0, 0)),
                  pl.BlockSpec((1, st_cols, LANES), lambda c, t: (c, 0, 0)),
                  pl.BlockSpec((1, LANES), lambda c, t: (0, c))],
        out_specs=pl.BlockSpec((bsz, tc, LANES), lambda c, t: (0, t, c)),
        scratch_shapes=[pltpu.VMEM((tc * bsz, LANES), F32),
                        pltpu.VMEM((tc * bsz, st_cols), F32),
                        pltpu.VMEM((tc * bsz, LANES), F32),
                        pltpu.VMEM((bsz, st_cols), F32)],
        compiler_params=_cparams(("arbitrary", "arbitrary"), 40),
        name="s5",
    )(proj3, bm, lam_t, cm, d_skip.reshape(1, width).astype(F32))


def _layer_norm(v, g, b):
    mu = jnp.mean(v, axis=-1, keepdims=True)
    var = jnp.mean(jnp.square(v - mu), axis=-1, keepdims=True)
    return (v - mu) * lax.rsqrt(var + LN_EPS) * g + b


def _rms_norm(v, g):
    return v * lax.rsqrt(jnp.mean(jnp.square(v), axis=-1, keepdims=True) + RMS_EPS) * g


def _mixout_kernel(att_ref, ssm_ref, x_ref, wglu_ref, bglu_ref, ag_ref, sg_ref, wout_ref, g_ref, b_ref,
                   o_ref, *, alpha):
    y = jax.nn.gelu(ssm_ref[...])
    z = jnp.dot(y.astype(BF16), wglu_ref[...], preferred_element_type=F32) + bglu_ref[...]
    o_ssm = y * jax.nn.sigmoid(z)
    a = _rms_norm(att_ref[...], ag_ref[...]).astype(BF16)
    s = _rms_norm(o_ssm, sg_ref[...]).astype(BF16)
    wa = att_ref.shape[1]
    mix = (jnp.dot(a, wout_ref[:wa, :], preferred_element_type=F32)
           + jnp.dot(s, wout_ref[wa:, :], preferred_element_type=F32))
    o_ref[...] = _layer_norm(alpha * x_ref[...] + mix, g_ref[...], b_ref[...])


def _mixout(o_att, y_ssm, x2d, w_glu, b_glu, att_g, ssm_g, w_out, ln_g, ln_b, alpha):
    n_tok, d = x2d.shape
    wa = o_att.shape[1]
    ws = y_ssm.shape[1]
    tm = 256
    row = lambda w: pl.BlockSpec((tm, w), lambda i: (i, 0))
    full = lambda a: pl.BlockSpec(a.shape, lambda i: (0,) * a.ndim)
    args = (o_att, y_ssm, x2d, w_glu.astype(BF16), b_glu.reshape(1, ws), att_g.reshape(1, wa),
            ssm_g.reshape(1, ws), w_out.astype(BF16), ln_g.reshape(1, d), ln_b.reshape(1, d))
    return pl.pallas_call(
        functools.partial(_mixout_kernel, alpha=alpha),
        out_shape=jax.ShapeDtypeStruct((n_tok, d), F32),
        grid=(n_tok // tm,),
        in_specs=[row(wa), row(ws), row(d)] + [full(a) for a in args[3:]],
        out_specs=row(d),
        compiler_params=_cparams(("parallel",), 32),
        name="mixout",
    )(*args)


def _split_bf16(v):
    hi = v.astype(BF16)
    lo = (v - hi.astype(F32)).astype(BF16)
    return hi, lo


def _router_kernel(h_ref, wt_ref, bias_ref, e_ref, g_ref, r_ref, cnt_ref, run_ref):
    tm = h_ref.shape[0]
    n_exp = wt_ref.shape[0]
    gsz = n_exp // N_EXPERT_GROUPS

    @pl.when(pl.program_id(0) == 0)
    def _():
        run_ref[...] = jnp.zeros_like(run_ref)

    w_hi, w_lo = _split_bf16(wt_ref[...])
    h_hi, h_lo = _split_bf16(h_ref[...])
    nt = (((1,), (1,)), ((), ()))
    logits = (lax.dot_general(w_hi, h_hi, nt, preferred_element_type=F32)
              + lax.dot_general(w_hi, h_lo, nt, preferred_element_type=F32)
              + lax.dot_general(w_lo, h_hi, nt, preferred_element_type=F32))
    scores = jax.nn.sigmoid(logits)
    choice = scores + bias_ref[:, 0:1]

    gio = lax.broadcasted_iota(jnp.int32, (gsz, tm), 0).astype(F32)
    gscore = []
    for g in range(N_EXPERT_GROUPS):
        cg = choice[g * gsz:(g + 1) * gsz, :]
        m1 = jnp.max(cg, axis=0, keepdims=True)
        i1 = jnp.min(jnp.where(cg == m1, gio, float(gsz)), axis=0, keepdims=True)
        m2 = jnp.max(jnp.where(gio == i1, NEG_INF, cg), axis=0, keepdims=True)
        gscore.append(m1 + m2)
    masked = []
    for g in range(N_EXPERT_GROUPS):
        beat = jnp.zeros((1, tm), F32)
        for o in range(N_EXPERT_GROUPS):
            if o == g:
                continue
            wins = (gscore[o] >= gscore[g]) if o < g else (gscore[o] > gscore[g])
            beat = beat + jnp.where(wins, 1.0, 0.0)
        keep = beat < float(TOPK_GROUPS)
        masked.append(jnp.where(keep, choice[g * gsz:(g + 1) * gsz, :], NEG_INF))
    cur = jnp.concatenate(masked, axis=0)

    eio = lax.broadcasted_iota(jnp.int32, (n_exp, tm), 0).astype(F32)
    idxs = []
    gates = []
    onehot = jnp.zeros((n_exp, tm), F32)
    for _ in range(TOP_K):
        m = jnp.max(cur, axis=0, keepdims=True)
        idx = jnp.min(jnp.where(cur == m, eio, float(n_exp)), axis=0, keepdims=True)
        hit = eio == idx
        idxs.append(idx)
        gates.append(jnp.sum(jnp.where(hit, scores, 0.0), axis=0, keepdims=True))
        cur = jnp.where(hit, NEG_INF, cur)
        onehot = onehot + jnp.where(hit, 1.0, 0.0)
    gate = jnp.concatenate(gates, axis=0)
    gate = ROUTED_SCALE * gate / (jnp.sum(gate, axis=0, keepdims=True) + 1e-20)

    si = lax.broadcasted_iota(jnp.int32, (tm, tm), 0)
    ti = lax.broadcasted_iota(jnp.int32, (tm, tm), 1)
    upper = jnp.where(si < ti, 1.0, 0.0).astype(BF16)
    before = jnp.dot(onehot.astype(BF16), upper, preferred_element_type=F32) + run_ref[:, 0:1]
    ranks = [jnp.sum(jnp.where(eio == idx, before, 0.0), axis=0, keepdims=True) for idx in idxs]

    e_ref[...] = jnp.concatenate(idxs, axis=0).astype(jnp.int32)
    g_ref[...] = gate
    r_ref[...] = jnp.concatenate(ranks, axis=0).astype(jnp.int32)
    run_ref[...] = run_ref[...] + jnp.sum(onehot, axis=1, keepdims=True)
    cnt_ref[...] = run_ref[...]


def _router(h, router_w, router_bias):
    n_tok, d = h.shape
    n_exp = router_w.shape[1]
    tm = 256
    wt = router_w.astype(F32).T
    bias = jnp.broadcast_to(router_bias.astype(F32)[:, None], (n_exp, LANES))
    tok = pl.BlockSpec((TOP_K, tm), lambda i: (0, i))
    return pl.pallas_call(
        _router_kernel,
        out_shape=(jax.ShapeDtypeStruct((TOP_K, n_tok), jnp.int32),
                   jax.ShapeDtypeStruct((TOP_K, n_tok), F32),
                   jax.ShapeDtypeStruct((TOP_K, n_tok), jnp.int32),
                   jax.ShapeDtypeStruct((n_exp, LANES), F32)),
        grid=(n_tok // tm,),
        in_specs=[pl.BlockSpec((tm, d), lambda i: (i, 0)),
                  pl.BlockSpec((n_exp, d), lambda i: (0, 0)),
                  pl.BlockSpec((n_exp, LANES), lambda i: (0, 0))],
        out_specs=(tok, tok, tok, pl.BlockSpec((n_exp, LANES), lambda i: (0, 0))),
        scratch_shapes=[pltpu.VMEM((n_exp, LANES), F32)],
        compiler_params=_cparams(("arbitrary",), 32),
        name="router",
    )(h, wt, bias)


def _dest_kernel(e_ref, r_ref, st_ref, d_ref):
    n_exp = st_ref.shape[0]
    tm = e_ref.shape[1]
    eio = lax.broadcasted_iota(jnp.int32, (n_exp, tm), 0)
    start = st_ref[:, 0:1]
    rows = [jnp.sum(jnp.where(eio == e_ref[k:k + 1, :], start, 0.0), axis=0, keepdims=True)
            for k in range(TOP_K)]
    d_ref[...] = jnp.concatenate(rows, axis=0).astype(jnp.int32) + r_ref[...]


def _dest(top_e, rank, starts):
    n_tok = top_e.shape[1]
    n_exp = starts.shape[0]
    tm = 512
    st = jnp.broadcast_to(starts.astype(F32)[:, None], (n_exp, LANES))
    tok = pl.BlockSpec((TOP_K, tm), lambda i: (0, i))
    return pl.pallas_call(
        _dest_kernel,
        out_shape=jax.ShapeDtypeStruct((TOP_K, n_tok), jnp.int32),
        grid=(n_tok // tm,),
        in_specs=[tok, tok, pl.BlockSpec((n_exp, LANES), lambda i: (0, 0))],
        out_specs=tok,
        compiler_params=_cparams(("parallel",), 32),
        name="dest",
    )(top_e, rank, st)


def _dispatch_kernel(dest_ref, h_ref, xs_ref, sem):
    tm = h_ref.shape[0]

    def issue(t, carry):
        for k in range(TOP_K):
            pltpu.make_async_copy(h_ref.at[pl.ds(t, 1)], xs_ref.at[pl.ds(dest_ref[k, t], 1)],
                                  sem).start(priority=k % 2)
        return carry
    lax.fori_loop(0, tm, issue, 0, unroll=2)

    def drain(t, carry):
        for k in range(TOP_K):
            pltpu.make_async_copy(h_ref.at[pl.ds(0, 1)], xs_ref.at[pl.ds(0, 1)], sem).wait()
        return carry
    lax.fori_loop(0, tm, drain, 0)


def _dispatch(h, dest):
    n_tok, d = h.shape
    tm = 256
    return pl.pallas_call(
        _dispatch_kernel,
        out_shape=jax.ShapeDtypeStruct((n_tok * TOP_K, d), h.dtype),
        grid=(n_tok // tm,),
        in_specs=[pl.BlockSpec((TOP_K, tm), lambda i: (0, i), memory_space=pltpu.SMEM),
                  pl.BlockSpec((tm, d), lambda i: (i, 0))],
        out_specs=pl.BlockSpec(memory_space=pl.ANY),
        scratch_shapes=[pltpu.SemaphoreType.DMA(())],
        compiler_params=_cparams(("arbitrary",), 32),
        name="dispatch",
    )(dest, h)


def _experts_kernel(vblk_ref, vexp_ref, vlo_ref, vhi_ref, nvis_ref, x_ref, wgu_ref, wdn_ref, y_ref,
                    wgu_bf, wdn_bf):
    v = pl.program_id(0)
    prev = jnp.maximum(v - 1, 0)
    first = (v == 0) | (vblk_ref[v] != vblk_ref[prev])

    @pl.when(first)
    def _():
        y_ref[...] = jnp.zeros_like(y_ref)

    @pl.when((v == 0) | (vexp_ref[v] != vexp_ref[prev]))
    def _():
        wgu_bf[...] = wgu_ref[0].astype(BF16)
        wdn_bf[...] = wdn_ref[0].astype(BF16)

    @pl.when(v < nvis_ref[0])
    def _():
        ff = wdn_ref.shape[1]
        xb = x_ref[...].astype(BF16)
        gu = jnp.dot(xb, wgu_bf[...], preferred_element_type=F32)
        act = (jax.nn.silu(gu[:, :ff]) * gu[:, ff:]).astype(BF16)
        y = jnp.dot(act, wdn_bf[...], preferred_element_type=F32)
        row = lax.broadcasted_iota(jnp.int32, (y.shape[0], 1), 0)
        mine = (row >= vlo_ref[v]) & (row < vhi_ref[v])
        y_ref[...] += jnp.where(mine, y, 0.0)


def _visit_list(counts, n_rows):
    n_exp = counts.shape[0]
    n_blocks = n_rows // EXPERT_ROWS
    n_vis_max = n_blocks + n_exp
    ends = jnp.cumsum(counts)
    starts = ends - counts
    fb = starts // EXPERT_ROWS
    lb = jnp.maximum(ends - 1, 0) // EXPERT_ROWS
    nv = jnp.where(counts > 0, lb - fb + 1, 0)
    vend = jnp.cumsum(nv)
    voff = vend - nv
    n_vis = vend[-1]
    v = jnp.arange(n_vis_max, dtype=jnp.int32)
    vc = jnp.minimum(v, n_vis - 1)
    e = jnp.sum(vend[None, :] <= vc[:, None], axis=1).astype(jnp.int32)
    pick = e[:, None] == jnp.arange(n_exp, dtype=jnp.int32)[None, :]
    take = lambda tab: jnp.sum(jnp.where(pick, tab[None, :], 0), axis=1)
    blk = take(fb) + (vc - take(voff))
    lo = jnp.maximum(take(starts), blk * EXPERT_ROWS) - blk * EXPERT_ROWS
    hi = jnp.minimum(take(ends), (blk + 1) * EXPERT_ROWS) - blk * EXPERT_ROWS
    live = v < n_vis
    lo = jnp.where(live, lo, 0)
    hi = jnp.where(live, hi, 0)
    i32 = lambda a: a.astype(jnp.int32)
    return i32(blk), i32(e), i32(lo), i32(hi), i32(n_vis).reshape(1), n_vis_max


def _experts(xs, counts, w_gu, w_down):
    n_rows, d = xs.shape
    ff2 = w_gu.shape[2]
    ff = w_down.shape[1]
    vblk, vexp, vlo, vhi, nvis, n_vis_max = _visit_list(counts, n_rows)
    grid_spec = pltpu.PrefetchScalarGridSpec(
        num_scalar_prefetch=5,
        grid=(n_vis_max,),
        in_specs=[pl.BlockSpec((EXPERT_ROWS, d), lambda v, b, e, lo, hi, n: (b[v], 0)),
                  pl.BlockSpec((1, d, ff2), lambda v, b, e, lo, hi, n: (e[v], 0, 0)),
                  pl.BlockSpec((1, ff, d), lambda v, b, e, lo, hi, n: (e[v], 0, 0))],
        out_specs=pl.BlockSpec((EXPERT_ROWS, d), lambda v, b, e, lo, hi, n: (b[v], 0)),
        scratch_shapes=[pltpu.VMEM((d, ff2), BF16), pltpu.VMEM((ff, d), BF16)],
    )
    return pl.pallas_call(
        _experts_kernel,
        out_shape=jax.ShapeDtypeStruct((n_rows, d), F32),
        grid_spec=grid_spec,
        compiler_params=_cparams(("arbitrary",), 32),
        name="experts",
    )(vblk, vexp, vlo, vhi, nvis, xs, w_gu, w_down)


def _combine_kernel(dest_ref, gate_ref, h_ref, ys_ref, wgu_ref, wdn_ref, g_ref, b_ref, o_ref, buf_ref, sem,
                    *, alpha):
    tm = h_ref.shape[0]

    def issue(t, carry):
        for k in range(TOP_K):
            pltpu.make_async_copy(ys_ref.at[pl.ds(dest_ref[k, t], 1)], buf_ref.at[k, pl.ds(t, 1)],
                                  sem).start(priority=k % 2)
        return carry
    lax.fori_loop(0, tm, issue, 0, unroll=2)

    h = h_ref[...]
    ff = wdn_ref.shape[0]
    gu = jnp.dot(h.astype(BF16), wgu_ref[...], preferred_element_type=F32)
    act = (jax.nn.silu(gu[:, :ff]) * gu[:, ff:]).astype(BF16)
    acc = alpha * h + jnp.dot(act, wdn_ref[...], preferred_element_type=F32)

    def drain(t, carry):
        for k in range(TOP_K):
            pltpu.make_async_copy(ys_ref.at[pl.ds(0, 1)], buf_ref.at[0, pl.ds(0, 1)], sem).wait()
        return carry
    lax.fori_loop(0, tm, drain, 0)

    gate = gate_ref[...]
    for k in range(TOP_K):
        acc = acc + gate[:, k:k + 1] * buf_ref[k]
    o_ref[...] = _layer_norm(acc, g_ref[...], b_ref[...])


def _combine(h, ys, dest, gate_t, shared_w_gu, shared_w_down, ln_g, ln_b, alpha):
    n_tok, d = h.shape
    tm = 256
    full = lambda a: pl.BlockSpec(a.shape, lambda i: (0,) * a.ndim)
    wgu = shared_w_gu.astype(BF16)
    wdn = shared_w_down.astype(BF16)
    g2 = ln_g.reshape(1, d)
    b2 = ln_b.reshape(1, d)
    return pl.pallas_call(
        functools.partial(_combine_kernel, alpha=alpha),
        out_shape=jax.ShapeDtypeStruct((n_tok, d), F32),
        grid=(n_tok // tm,),
        in_specs=[pl.BlockSpec((TOP_K, tm), lambda i: (0, i), memory_space=pltpu.SMEM),
                  pl.BlockSpec((tm, TOP_K), lambda i: (i, 0)),
                  pl.BlockSpec((tm, d), lambda i: (i, 0)),
                  pl.BlockSpec(memory_space=pl.ANY),
                  full(wgu), full(wdn), full(g2), full(b2)],
        out_specs=pl.BlockSpec((tm, d), lambda i: (i, 0)),
        scratch_shapes=[pltpu.VMEM((TOP_K, tm, d), F32), pltpu.SemaphoreType.DMA(())],
        compiler_params=_cparams(("arbitrary",), 40),
        name="combine",
    )(dest, gate_t, h, ys, wgu, wdn, g2, b2)


def _moe(h, router_w, router_bias, w_gu, w_down, shared_w_gu, shared_w_down, ln_g, ln_b, alpha):
    top_e, gate, rank, cnt = _router(h, router_w, router_bias)
    counts = cnt[:, 0].astype(jnp.int32)
    starts = jnp.cumsum(counts) - counts
    dest = _dest(top_e, rank, starts)
    xs = _dispatch(h, dest)
    ys = _experts(xs, counts, w_gu, w_down)
    return _combine(h, ys, dest, gate.T, shared_w_gu, shared_w_down, ln_g, ln_b, alpha)


def kernel(x, w_in, att_norm_g, lam_re, lam_im, log_step, b_re, b_im, c_re, c_im, d_skip, w_glu, b_glu,
           ssm_norm_g, w_out, ln1_g, ln1_b, router_w, router_bias, w_gu, w_down, shared_w_gu,
           shared_w_down, ln2_g, ln2_b):
    bsz, seq, d = x.shape
    depth = w_in.shape[0]
    alpha = (2 * depth) ** 0.25
    h = x.reshape(bsz * seq, d)
    for i in range(depth):
        proj = _inproj(h, w_in[i].astype(BF16), seq)
        o_att = _attention(proj, bsz, seq)
        y_ssm = _s5(proj.reshape(bsz, seq, -1), 3 * ATT_WIDTH, lam_re[i], lam_im[i], log_step[i],
                    b_re[i], b_im[i], c_re[i], c_im[i], d_skip[i])
        h = _mixout(o_att, y_ssm.reshape(bsz * seq, -1), h, w_glu[i], b_glu[i], att_norm_g[i],
                    ssm_norm_g[i], w_out[i], ln1_g[i], ln1_b[i], alpha)
        h = _moe(h, router_w[i], router_bias[i], w_gu[i], w_down[i], shared_w_gu[i], shared_w_down[i],
                 ln2_g[i], ln2_b[i], alpha)
    return h.reshape(bsz, seq, d)
```

```python
import functools
import math

import jax
import jax.numpy as jnp
from jax import lax
from jax.experimental import pallas as pl
from jax.experimental.pallas import tpu as pltpu

F32 = jnp.float32
BF16 = jnp.bfloat16

ATT_HEADS = 8
HEAD_DIM = 64
ATT_WIDTH = ATT_HEADS * HEAD_DIM
SSM_CH = 16
SSM_STATE = 64
ROPE_THETA = 500000.0
ROT_DIM = HEAD_DIM // 4
DILATIONS = (1, 4, 16)
ATT_BLOCK = 128
ATT_GROUP = 4
N_EXPERTS = 256
TOP_K = 8
N_EXPERT_GROUPS = 8
TOPK_GROUPS = 4
ROUTED_SCALE = 2.5
LN_EPS = 1e-5
RMS_EPS = 1e-6

LANES = 128
SUBLANES = 8
EXPERT_ROWS = 128
EXPERT_RING = 4
NEG_INF = float("-inf")


def _cparams(sem, vmem_mb):
    return pltpu.CompilerParams(dimension_semantics=sem, vmem_limit_bytes=vmem_mb * 1024 * 1024)


def _inproj_kernel(x_ref, w_ref, cos_ref, sa_ref, sb_ref, o_ref, *, n_rot_cols):
    xb = x_ref[...].astype(BF16)
    cosf = cos_ref[...]
    sa = sa_ref[...]
    sb = sb_ref[...]
    width = o_ref.shape[1]
    chunk = 512
    for c in range(width // chunk):
        r = jnp.dot(xb, w_ref[:, c * chunk:(c + 1) * chunk], preferred_element_type=F32)
        if c * chunk < n_rot_cols:
            parts = []
            for s in range(chunk // LANES):
                t = r[:, s * LANES:(s + 1) * LANES]
                parts.append(t * cosf + pltpu.roll(t, LANES - ROT_DIM // 2, 1) * sa
                             + pltpu.roll(t, ROT_DIM // 2, 1) * sb)
            r = jnp.concatenate(parts, axis=1)
        o_ref[:, c * chunk:(c + 1) * chunk] = r


def _rope_lane_tables(seq):
    half = ROT_DIM // 2
    inv_freq = jnp.power(jnp.float32(ROPE_THETA), -jnp.arange(half, dtype=F32) / half)
    ang = jnp.arange(seq, dtype=F32)[:, None] * inv_freq[None, :]
    cos, sin = jnp.cos(ang), jnp.sin(ang)
    rest = HEAD_DIM - ROT_DIM
    cos_h = jnp.concatenate([cos, cos, jnp.ones((seq, rest), F32)], axis=1)
    sa_h = jnp.concatenate([-sin, jnp.zeros((seq, half + rest), F32)], axis=1)
    sb_h = jnp.concatenate([jnp.zeros((seq, half), F32), sin, jnp.zeros((seq, rest), F32)], axis=1)
    rep = LANES // HEAD_DIM
    return tuple(jnp.tile(t, (1, rep)) for t in (cos_h, sa_h, sb_h))


def _inproj(x2d, w_in_bf, seq):
    n_tok, d = x2d.shape
    width = w_in_bf.shape[1]
    tm = 512
    cosf, sa, sb = _rope_lane_tables(seq)
    tab_spec = pl.BlockSpec((tm, LANES), lambda i: (i % (seq // tm), 0))
    return pl.pallas_call(
        functools.partial(_inproj_kernel, n_rot_cols=2 * ATT_WIDTH),
        out_shape=jax.ShapeDtypeStruct((n_tok, width), F32),
        grid=(n_tok // tm,),
        in_specs=[pl.BlockSpec((tm, d), lambda i: (i, 0)),
                  pl.BlockSpec((d, width), lambda i: (0, 0)),
                  tab_spec, tab_spec, tab_spec],
        out_specs=pl.BlockSpec((tm, width), lambda i: (i, 0)),
        compiler_params=_cparams(("parallel",), 48),
        name="inproj",
    )(x2d, w_in_bf, cosf, sa, sb)


def _attn_kernel(q_ref, k_ref, v_ref, o_ref, ob_ref, lb_ref, band_ref, causal_ref, *, seq):
    blk = ATT_BLOCK
    lane = lax.broadcasted_iota(jnp.int32, (1, LANES), 1)
    head0 = lane < HEAD_DIM
    scale = HEAD_DIM ** -0.5

    qi = lax.broadcasted_iota(jnp.int32, (blk, 2 * blk), 0)
    kj = lax.broadcasted_iota(jnp.int32, (blk, 2 * blk), 1)
    dist = qi + blk - kj
    band_ref[...] = jnp.where((dist >= 0) & (dist <= blk), 0.0, NEG_INF)
    causal_ref[...] = jnp.where(lax.broadcasted_iota(jnp.int32, (blk, blk), 0)
                                >= lax.broadcasted_iota(jnp.int32, (blk, blk), 1), 0.0, NEG_INF)

    def rows(start, d):
        if d == 1:
            return pl.ds(start, blk)
        return pl.ds(start, blk, stride=d)

    def one_block(c, d, start_q, with_prev):
        sl_q = rows(start_q, d)
        q = q_ref[sl_q, :] * scale
        kk = k_ref[sl_q, :]
        vv = v_ref[sl_q, :]
        if with_prev:
            sl_p = rows(start_q - d * blk, d)
            kk = jnp.concatenate([k_ref[sl_p, :], kk], axis=0)
            vv = jnp.concatenate([v_ref[sl_p, :], vv], axis=0)
            bias_ref = band_ref
        else:
            bias_ref = causal_ref
        kk = kk.astype(BF16)
        vv = vv.astype(BF16)
        outs = []
        lses = []
        for h in range(LANES // HEAD_DIM):
            hm = head0 if h == 0 else jnp.logical_not(head0)
            qh = jnp.where(hm, q, 0.0).astype(BF16)
            s = lax.dot_general(qh, kk, (((1,), (1,)), ((), ())), preferred_element_type=F32)
            s = s + bias_ref[...]
            m = jnp.max(s, axis=-1, keepdims=True)
            p = jnp.exp(s - m)
            den = jnp.sum(p, axis=-1, keepdims=True)
            outs.append(jnp.dot(p.astype(BF16), vv, preferred_element_type=F32) / den)
            lses.append(m + jnp.log(den))
        ob_ref[c, sl_q, :] = jnp.where(head0, outs[0], outs[1])
        lb_ref[c, sl_q, :] = jnp.where(head0, lses[0], lses[1])

    def run_blocks(n_blocks, group, fn):
        assert n_blocks % group == 0
        if n_blocks == group:
            for g in range(group):
                fn(g)
            return
        def body(it, carry):
            for g in range(group):
                fn(it * group + g)
            return carry
        lax.fori_loop(0, n_blocks // group, body, 0)

    for c, d in enumerate(DILATIONS):
        nb = seq // d // blk
        run_blocks(d, min(d, ATT_GROUP), lambda r, c=c, d=d: one_block(c, d, r, False))
        rest = (nb - 1) * d
        if rest:
            def later(idx, c=c, d=d):
                n = idx // d
                r = idx - n * d
                one_block(c, d, r + (n + 1) * (d * blk), True)
            group = max(g for g in range(1, ATT_GROUP + 1) if rest % g == 0)
            run_blocks(rest, group, later)

    rc = 256
    def merge(i, carry):
        sl = pl.ds(pl.multiple_of(i * rc, rc), rc)
        l0 = lb_ref[0, sl, :]
        l1 = lb_ref[1, sl, :]
        l2 = lb_ref[2, sl, :]
        mx = jnp.maximum(jnp.maximum(l0, l1), l2)
        e0 = jnp.exp(l0 - mx)
        e1 = jnp.exp(l1 - mx)
        e2 = jnp.exp(l2 - mx)
        tot = e0 + e1 + e2
        o_ref[sl, :] = ((e0 / tot) * ob_ref[0, sl, :] + (e1 / tot) * ob_ref[1, sl, :]
                        + (e2 / tot) * ob_ref[2, sl, :])
        return carry
    lax.fori_loop(0, seq // rc, merge, 0)


def _attention(proj, bsz, seq):
    n_tok = proj.shape[0]
    pairs = ATT_WIDTH // LANES
    assert seq % (ATT_BLOCK * max(DILATIONS)) == 0
    blk = (seq, LANES)
    return pl.pallas_call(
        functools.partial(_attn_kernel, seq=seq),
        out_shape=jax.ShapeDtypeStruct((n_tok, ATT_WIDTH), F32),
        grid=(bsz, pairs),
        in_specs=[pl.BlockSpec(blk, lambda b, h: (b, h)),
                  pl.BlockSpec(blk, lambda b, h: (b, pairs + h)),
                  pl.BlockSpec(blk, lambda b, h: (b, 2 * pairs + h))],
        out_specs=pl.BlockSpec(blk, lambda b, h: (b, h)),
        scratch_shapes=[pltpu.VMEM((len(DILATIONS), seq, LANES), F32),
                        pltpu.VMEM((len(DILATIONS), seq, LANES), F32),
                        pltpu.VMEM((ATT_BLOCK, 2 * ATT_BLOCK), F32),
                        pltpu.VMEM((ATT_BLOCK, ATT_BLOCK), F32)],
        compiler_params=_cparams(("parallel", "parallel"), 32),
        name="attn",
    )(proj, proj, proj)


def _s5_kernel(u_ref, bm_ref, lam_ref, cm_ref, dk_ref, o_ref, us_ref, st_ref, ys_ref, carry_ref, *, tc):
    bsz = u_ref.shape[0]
    half = st_ref.shape[1] // 2
    rows = tc * bsz
    mm_rows = 512

    @pl.when(pl.program_id(1) == 0)
    def _():
        carry_ref[...] = jnp.zeros_like(carry_ref)

    for b in range(bsz):
        us_ref[pl.ds(b, tc, stride=bsz), :] = u_ref[b]

    bm = bm_ref[0]
    for r0 in range(0, rows, mm_rows):
        st_ref[r0:r0 + mm_rows, :] = jnp.dot(us_ref[r0:r0 + mm_rows, :].astype(BF16), bm,
                                             preferred_element_type=F32)

    lam = lam_ref[0]
    lam_re = lam[:, :half]
    lam_im = lam[:, half:]

    def step(t, carry):
        xr, xi = carry
        sl = pl.ds(pl.multiple_of(t * bsz, bsz), bsz)
        nr = lam_re * xr - lam_im * xi + st_ref[sl, :half]
        ni = lam_re * xi + lam_im * xr + st_ref[sl, half:]
        st_ref[sl, :half] = nr
        st_ref[sl, half:] = ni
        return nr, ni

    xr, xi = lax.fori_loop(0, tc, step, (carry_ref[:, :half], carry_ref[:, half:]), unroll=4)
    carry_ref[:, :half] = xr
    carry_ref[:, half:] = xi

    cm = cm_ref[0]
    for r0 in range(0, rows, mm_rows):
        ys_ref[r0:r0 + mm_rows, :] = jnp.dot(st_ref[r0:r0 + mm_rows, :].astype(BF16), cm,
                                             preferred_element_type=F32)
    dk = dk_ref[...]
    for b in range(bsz):
        o_ref[b] = ys_ref[pl.ds(b, tc, stride=bsz), :] + dk * u_ref[b]


def _s5_params(lam_re, lam_im, log_step, b_re, b_im, c_re, c_im, bsz):
    groups = lam_re.shape[0]
    gpc = LANES // SSM_CH
    n_chunks = groups // gpc
    lam = lax.complex(lam_re.astype(F32), lam_im.astype(F32))
    step = jnp.exp(log_step.astype(F32))[:, None]
    lam_bar = jnp.exp(lam * step)
    bmat = lax.complex(b_re.astype(F32), b_im.astype(F32))
    b_bar = ((lam_bar - 1.0) / lam)[..., None] * bmat
    eye = jnp.eye(gpc, dtype=F32)

    def block_diag_in(t):
        t = t.reshape(n_chunks, gpc, SSM_STATE, SSM_CH)
        return jnp.einsum('ngpc,gh->ngchp', t, eye).reshape(n_chunks, gpc * SSM_CH, gpc * SSM_STATE)

    def block_diag_out(t):
        t = t.reshape(n_chunks, gpc, SSM_CH, SSM_STATE)
        return jnp.einsum('ngcp,gh->ngphc', t, eye).reshape(n_chunks, gpc * SSM_STATE, gpc * SSM_CH)

    bm = jnp.concatenate([block_diag_in(b_bar.real), block_diag_in(b_bar.imag)], axis=2).astype(BF16)
    cm = jnp.concatenate([block_diag_out(c_re.astype(F32)), block_diag_out(-c_im.astype(F32))],
                         axis=1).astype(BF16)
    lam_row = jnp.concatenate([lam_bar.real.reshape(n_chunks, gpc * SSM_STATE),
                               lam_bar.imag.reshape(n_chunks, gpc * SSM_STATE)], axis=1)
    lam_t = jnp.broadcast_to(lam_row[:, None, :], (n_chunks, bsz, 2 * gpc * SSM_STATE))
    return bm, lam_t, cm, n_chunks


def _s5(proj3, u_col0, lam_re, lam_im, log_step, b_re, b_im, c_re, c_im, d_skip):
    bsz, seq, _ = proj3.shape
    assert bsz == SUBLANES
    bm, lam_t, cm, n_chunks = _s5_params(lam_re, lam_im, log_step, b_re, b_im, c_re, c_im, bsz)
    width = n_chunks * LANES
    tc = 256
    st_cols = bm.shape[2]
    ublk0 = u_col0 // LANES
    return pl.pallas_call(
        functools.partial(_s5_kernel, tc=tc),
        out_shape=jax.ShapeDtypeStruct((bsz, seq, width), F32),
        grid=(n_chunks, seq // tc),
        in_specs=[pl.BlockSpec((bsz, tc, LANES), lambda c, t: (0, t, ublk0 + c)),
                  pl.BlockSpec((1, LANES, st_cols), lambda c, t: (c, 0, 0)),
                  pl.BlockSpec((1, bsz, st_cols), lambda c, t: (c, 0, 0)),
                  pl.BlockSpec((1, st_cols, LANES), lambda c, t: (c, 0, 0)),
                  pl.BlockSpec((1, LANES), lambda c, t: (0, c))],
        out_specs=pl.BlockSpec((bsz, tc, LANES), lambda c, t: (0, t, c)),
        scratch_shapes=[pltpu.VMEM((tc * bsz, LANES), F32),
                        pltpu.VMEM((tc * bsz, st_cols), F32),
                        pltpu.VMEM((tc * bsz, LANES), F32),
                        pltpu.VMEM((bsz, st_cols), F32)],
        compiler_params=_cparams(("arbitrary", "arbitrary"), 40),
        name="s5",
    )(proj3, bm, lam_t, cm, d_skip.reshape(1, width).astype(F32))


def _layer_norm(v, g, b):
    mu = jnp.mean(v, axis=-1, keepdims=True)
    var = jnp.mean(jnp.square(v - mu), axis=-1, keepdims=True)
    return (v - mu) * lax.rsqrt(var + LN_EPS) * g + b


def _rms_norm(v, g):
    return v * lax.rsqrt(jnp.mean(jnp.square(v), axis=-1, keepdims=True) + RMS_EPS) * g


def _mixout_kernel(att_ref, ssm_ref, x_ref, wglu_ref, bglu_ref, ag_ref, sg_ref, wout_ref, g_ref, b_ref,
                   o_ref, *, alpha):
    y = jax.nn.gelu(ssm_ref[...])
    z = jnp.dot(y.astype(BF16), wglu_ref[...], preferred_element_type=F32) + bglu_ref[...]
    o_ssm = y * jax.nn.sigmoid(z)
    a = _rms_norm(att_ref[...], ag_ref[...]).astype(BF16)
    s = _rms_norm(o_ssm, sg_ref[...]).astype(BF16)
    wa = att_ref.shape[1]
    mix = (jnp.dot(a, wout_ref[:wa, :], preferred_element_type=F32)
           + jnp.dot(s, wout_ref[wa:, :], preferred_element_type=F32))
    o_ref[...] = _layer_norm(alpha * x_ref[...] + mix, g_ref[...], b_ref[...])


def _mixout(o_att, y_ssm, x2d, w_glu, b_glu, att_g, ssm_g, w_out, ln_g, ln_b, alpha):
    n_tok, d = x2d.shape
    wa = o_att.shape[1]
    ws = y_ssm.shape[1]
    tm = 256
    row = lambda w: pl.BlockSpec((tm, w), lambda i: (i, 0))
    full = lambda a: pl.BlockSpec(a.shape, lambda i: (0,) * a.ndim)
    args = (o_att, y_ssm, x2d, w_glu.astype(BF16), b_glu.reshape(1, ws), att_g.reshape(1, wa),
            ssm_g.reshape(1, ws), w_out.astype(BF16), ln_g.reshape(1, d), ln_b.reshape(1, d))
    return pl.pallas_call(
        functools.partial(_mixout_kernel, alpha=alpha),
        out_shape=jax.ShapeDtypeStruct((n_tok, d), F32),
        grid=(n_tok // tm,),
        in_specs=[row(wa), row(ws), row(d)] + [full(a) for a in args[3:]],
        out_specs=row(d),
        compiler_params=_cparams(("parallel",), 32),
        name="mixout",
    )(*args)


def _split_bf16(v):
    hi = v.astype(BF16)
    lo = (v - hi.astype(F32)).astype(BF16)
    return hi, lo


def _router_kernel(h_ref, wt_ref, bias_ref, e_ref, g_ref, r_ref, cnt_ref, run_ref):
    tm = h_ref.shape[0]
    n_exp = wt_ref.shape[0]
    gsz = n_exp // N_EXPERT_GROUPS

    @pl.when(pl.program_id(0) == 0)
    def _():
        run_ref[...] = jnp.zeros_like(run_ref)

    w_hi, w_lo = _split_bf16(wt_ref[...])
    h_hi, h_lo = _split_bf16(h_ref[...])
    nt = (((1,), (1,)), ((), ()))
    logits = (lax.dot_general(w_hi, h_hi, nt, preferred_element_type=F32)
              + lax.dot_general(w_hi, h_lo, nt, preferred_element_type=F32)
              + lax.dot_general(w_lo, h_hi, nt, preferred_element_type=F32))
    scores = jax.nn.sigmoid(logits)
    choice = scores + bias_ref[:, 0:1]

    gio = lax.broadcasted_iota(jnp.int32, (gsz, tm), 0).astype(F32)
    gscore = []
    for g in range(N_EXPERT_GROUPS):
        cg = choice[g * gsz:(g + 1) * gsz, :]
        m1 = jnp.max(cg, axis=0, keepdims=True)
        i1 = jnp.min(jnp.where(cg == m1, gio, float(gsz)), axis=0, keepdims=True)
        m2 = jnp.max(jnp.where(gio == i1, NEG_INF, cg), axis=0, keepdims=True)
        gscore.append(m1 + m2)
    masked = []
    for g in range(N_EXPERT_GROUPS):
        beat = jnp.zeros((1, tm), F32)
        for o in range(N_EXPERT_GROUPS):
            if o == g:
                continue
            wins = (gscore[o] >= gscore[g]) if o < g else (gscore[o] > gscore[g])
            beat = beat + jnp.where(wins, 1.0, 0.0)
        keep = beat < float(TOPK_GROUPS)
        masked.append(jnp.where(keep, choice[g * gsz:(g + 1) * gsz, :], NEG_INF))
    cur = jnp.concatenate(masked, axis=0)

    eio = lax.broadcasted_iota(jnp.int32, (n_exp, tm), 0).astype(F32)
    idxs = []
    gates = []
    onehot = jnp.zeros((n_exp, tm), F32)
    for _ in range(TOP_K):
        m = jnp.max(cur, axis=0, keepdims=True)
        idx = jnp.min(jnp.where(cur == m, eio, float(n_exp)), axis=0, keepdims=True)
        hit = eio == idx
        idxs.append(idx)
        gates.append(jnp.sum(jnp.where(hit, scores, 0.0), axis=0, keepdims=True))
        cur = jnp.where(hit, NEG_INF, cur)
        onehot = onehot + jnp.where(hit, 1.0, 0.0)
    gate = jnp.concatenate(gates, axis=0)
    gate = ROUTED_SCALE * gate / (jnp.sum(gate, axis=0, keepdims=True) + 1e-20)

    si = lax.broadcasted_iota(jnp.int32, (tm, tm), 0)
    ti = lax.broadcasted_iota(jnp.int32, (tm, tm), 1)
    upper = jnp.where(si < ti, 1.0, 0.0).astype(BF16)
    before = jnp.dot(onehot.astype(BF16), upper, preferred_element_type=F32) + run_ref[:, 0:1]
    ranks = [jnp.sum(jnp.where(eio == idx, before, 0.0), axis=0, keepdims=True) for idx in idxs]

    e_ref[...] = jnp.concatenate(idxs, axis=0).astype(jnp.int32)
    g_ref[...] = gate
    r_ref[...] = jnp.concatenate(ranks, axis=0).astype(jnp.int32)
    run_ref[...] = run_ref[...] + jnp.sum(onehot, axis=1, keepdims=True)
    cnt_ref[...] = run_ref[...]


def _router(h, router_w, router_bias):
    n_tok, d = h.shape
    n_exp = router_w.shape[1]
    tm = 256
    wt = router_w.astype(F32).T
    bias = jnp.broadcast_to(router_bias.astype(F32)[:, None], (n_exp, LANES))
    tok = pl.BlockSpec((TOP_K, tm), lambda i: (0, i))
    return pl.pallas_call(
        _router_kernel,
        out_shape=(jax.ShapeDtypeStruct((TOP_K, n_tok), jnp.int32),
                   jax.ShapeDtypeStruct((TOP_K, n_tok), F32),
                   jax.ShapeDtypeStruct((TOP_K, n_tok), jnp.int32),
                   jax.ShapeDtypeStruct((n_exp, LANES), F32)),
        grid=(n_tok // tm,),
        in_specs=[pl.BlockSpec((tm, d), lambda i: (i, 0)),
                  pl.BlockSpec((n_exp, d), lambda i: (0, 0)),
                  pl.BlockSpec((n_exp, LANES), lambda i: (0, 0))],
        out_specs=(tok, tok, tok, pl.BlockSpec((n_exp, LANES), lambda i: (0, 0))),
        scratch_shapes=[pltpu.VMEM((n_exp, LANES), F32)],
        compiler_params=_cparams(("arbitrary",), 32),
        name="router",
    )(h, wt, bias)


def _dest_kernel(e_ref, r_ref, st_ref, d_ref):
    n_exp = st_ref.shape[0]
    tm = e_ref.shape[1]
    eio = lax.broadcasted_iota(jnp.int32, (n_exp, tm), 0)
    start = st_ref[:, 0:1]
    rows = [jnp.sum(jnp.where(eio == e_ref[k:k + 1, :], start, 0.0), axis=0, keepdims=True)
            for k in range(TOP_K)]
    d_ref[...] = jnp.concatenate(rows, axis=0).astype(jnp.int32) + r_ref[...]


def _dest(top_e, rank, starts):
    n_tok = top_e.shape[1]
    n_exp = starts.shape[0]
    tm = 512
    st = jnp.broadcast_to(starts.astype(F32)[:, None], (n_exp, LANES))
    tok = pl.BlockSpec((TOP_K, tm), lambda i: (0, i))
    return pl.pallas_call(
        _dest_kernel,
        out_shape=jax.ShapeDtypeStruct((TOP_K, n_tok), jnp.int32),
        grid=(n_tok // tm,),
        in_specs=[tok, tok, pl.BlockSpec((n_exp, LANES), lambda i: (0, 0))],
        out_specs=tok,
        compiler_params=_cparams(("parallel",), 32),
        name="dest",
    )(top_e, rank, st)


def _dispatch_kernel(dest_ref, h_ref, xs_ref, sem):
    tm = h_ref.shape[0]

    def issue(t, carry):
        for k in range(TOP_K):
            pltpu.make_async_copy(h_ref.at[pl.ds(t, 1)], xs_ref.at[pl.ds(dest_ref[k, t], 1)],
                                  sem).start(priority=k % 2)
        return carry
    lax.fori_loop(0, tm, issue, 0, unroll=2)

    def drain(t, carry):
        for k in range(TOP_K):
            pltpu.make_async_copy(h_ref.at[pl.ds(0, 1)], xs_ref.at[pl.ds(0, 1)], sem).wait()
        return carry
    lax.fori_loop(0, tm, drain, 0)


def _dispatch(h, dest):
    n_tok, d = h.shape
    tm = 256
    return pl.pallas_call(
        _dispatch_kernel,
        out_shape=jax.ShapeDtypeStruct((n_tok * TOP_K, d), h.dtype),
        grid=(n_tok // tm,),
        in_specs=[pl.BlockSpec((TOP_K, tm), lambda i: (0, i), memory_space=pltpu.SMEM),
                  pl.BlockSpec((tm, d), lambda i: (i, 0))],
        out_specs=pl.BlockSpec(memory_space=pl.ANY),
        scratch_shapes=[pltpu.SemaphoreType.DMA(())],
        compiler_params=_cparams(("arbitrary",), 32),
        name="dispatch",
    )(dest, h)


def _experts_kernel(vblk_ref, vlo_ref, vhi_ref, voff_ref, vend_ref, nvis_ref,
                    xs_ref, wgu_ref, wdn_ref, ys_ref,
                    xbuf, acc, wgu_bf, wdn_bf, xsem, ysem):
    e = pl.program_id(0)
    n_vis = nvis_ref[0]
    rows = xbuf.shape[1]
    ff = wdn_bf.shape[0]

    def x_copy(v):
        slot = v % EXPERT_RING
        return pltpu.make_async_copy(xs_ref.at[pl.ds(pl.multiple_of(vblk_ref[v] * rows, rows), rows)],
                                     xbuf.at[slot], xsem.at[slot])

    def y_copy(blk):
        slot = blk % 2
        return pltpu.make_async_copy(acc.at[slot], ys_ref.at[pl.ds(pl.multiple_of(blk * rows, rows), rows)],
                                     ysem.at[slot])

    @pl.when(e == 0)
    def _():
        for i in range(EXPERT_RING):
            @pl.when(i < n_vis)
            def _():
                x_copy(i).start()

    wgu_bf[...] = wgu_ref[0].astype(BF16)
    wdn_bf[...] = wdn_ref[0].astype(BF16)

    def visit(v, carry):
        blk = vblk_ref[v]
        slot = blk % 2

        @pl.when((v == 0) | (blk != vblk_ref[jnp.maximum(v - 1, 0)]))
        def _():
            @pl.when(blk >= 1)
            def _():
                y_copy(blk - 1).start()

            @pl.when(blk >= 2)
            def _():
                y_copy(blk - 2).wait()
            acc[slot] = jnp.zeros(acc.shape[1:], F32)

        x_copy(v).wait()
        xb = xbuf[v % EXPERT_RING].astype(BF16)
        gu = jnp.dot(xb, wgu_bf[...], preferred_element_type=F32)
        act = (jax.nn.silu(gu[:, :ff]) * gu[:, ff:]).astype(BF16)
        y = jnp.dot(act, wdn_bf[...], preferred_element_type=F32)
        row = lax.broadcasted_iota(jnp.int32, (rows, 1), 0)
        mine = (row >= vlo_ref[v]) & (row < vhi_ref[v])
        acc[slot] += jnp.where(mine, y, 0.0)

        @pl.when(v + EXPERT_RING < n_vis)
        def _():
            x_copy(v + EXPERT_RING).start()

        @pl.when(v == n_vis - 1)
        def _():
            y_copy(blk).start()

            @pl.when(blk >= 1)
            def _():
                y_copy(blk - 1).wait()
            y_copy(blk).wait()
        return carry

    lax.fori_loop(voff_ref[e], vend_ref[e], visit, 0)


def _visit_list(counts, n_rows):
    n_exp = counts.shape[0]
    n_blocks = n_rows // EXPERT_ROWS
    n_vis_max = n_blocks + n_exp
    ends = jnp.cumsum(counts)
    starts = ends - counts
    fb = starts // EXPERT_ROWS
    lb = jnp.maximum(ends - 1, 0) // EXPERT_ROWS
    nv = jnp.where(counts > 0, lb - fb + 1, 0)
    vend = jnp.cumsum(nv)
    voff = vend - nv
    n_vis = vend[-1]
    v = jnp.arange(n_vis_max, dtype=jnp.int32)
    vc = jnp.minimum(v, n_vis - 1)
    e = jnp.sum(vend[None, :] <= vc[:, None], axis=1).astype(jnp.int32)
    pick = e[:, None] == jnp.arange(n_exp, dtype=jnp.int32)[None, :]
    take = lambda tab: jnp.sum(jnp.where(pick, tab[None, :], 0), axis=1)
    blk = take(fb) + (vc - take(voff))
    lo = jnp.maximum(take(starts), blk * EXPERT_ROWS) - blk * EXPERT_ROWS
    hi = jnp.minimum(take(ends), (blk + 1) * EXPERT_ROWS) - blk * EXPERT_ROWS
    i32 = lambda a: a.astype(jnp.int32)
    return i32(blk), i32(lo), i32(hi), i32(voff), i32(vend), i32(n_vis).reshape(1)


def _experts(xs, counts, w_gu, w_down):
    n_rows, d = xs.shape
    n_exp, _, ff2 = w_gu.shape
    ff = w_down.shape[1]
    assert n_rows % EXPERT_ROWS == 0
    prefetch = _visit_list(counts, n_rows)
    grid_spec = pltpu.PrefetchScalarGridSpec(
        num_scalar_prefetch=len(prefetch),
        grid=(n_exp,),
        in_specs=[pl.BlockSpec(memory_space=pl.ANY),
                  pl.BlockSpec((1, d, ff2), lambda e, *_: (e, 0, 0)),
                  pl.BlockSpec((1, ff, d), lambda e, *_: (e, 0, 0))],
        out_specs=pl.BlockSpec(memory_space=pl.ANY),
        scratch_shapes=[pltpu.VMEM((EXPERT_RING, EXPERT_ROWS, d), F32),
                        pltpu.VMEM((2, EXPERT_ROWS, d), F32),
                        pltpu.VMEM((d, ff2), BF16), pltpu.VMEM((ff, d), BF16),
                        pltpu.SemaphoreType.DMA((EXPERT_RING,)),
                        pltpu.SemaphoreType.DMA((2,))],
    )
    return pl.pallas_call(
        _experts_kernel,
        out_shape=jax.ShapeDtypeStruct((n_rows, d), F32),
        grid_spec=grid_spec,
        compiler_params=_cparams(("arbitrary",), 32),
        name="experts",
    )(*prefetch, xs, w_gu, w_down)


def _combine_kernel(dest_ref, dnext_ref, gate_ref, h_ref, ys_ref, wgu_ref, wdn_ref, g_ref, b_ref, o_ref,
                    buf_ref, sem, *, alpha):
    tm = h_ref.shape[0]
    i = pl.program_id(0)
    cur = i % 2

    def gather(d_ref, slot):
        def issue(t, carry):
            for k in range(TOP_K):
                pltpu.make_async_copy(ys_ref.at[pl.ds(d_ref[k, t], 1)], buf_ref.at[slot, k, pl.ds(t, 1)],
                                      sem.at[slot]).start(priority=k % 2)
            return carry
        lax.fori_loop(0, tm, issue, 0, unroll=2)

    @pl.when(i == 0)
    def _():
        gather(dest_ref, 0)

    @pl.when(i + 1 < pl.num_programs(0))
    def _():
        gather(dnext_ref, 1 - cur)

    h = h_ref[...]
    ff = wdn_ref.shape[0]
    gu = jnp.dot(h.astype(BF16), wgu_ref[...], preferred_element_type=F32)
    act = (jax.nn.silu(gu[:, :ff]) * gu[:, ff:]).astype(BF16)
    acc = alpha * h + jnp.dot(act, wdn_ref[...], preferred_element_type=F32)

    def drain(t, carry):
        for k in range(TOP_K):
            pltpu.make_async_copy(ys_ref.at[pl.ds(0, 1)], buf_ref.at[cur, 0, pl.ds(0, 1)], sem.at[cur]).wait()
        return carry
    lax.fori_loop(0, tm, drain, 0)

    gate = gate_ref[...]
    for k in range(TOP_K):
        acc = acc + gate[:, k:k + 1] * buf_ref[cur, k]
    o_ref[...] = _layer_norm(acc, g_ref[...], b_ref[...])


def _combine(h, ys, dest, gate_t, shared_w_gu, shared_w_down, ln_g, ln_b, alpha):
    n_tok, d = h.shape
    tm = 256
    n_tiles = n_tok // tm
    full = lambda a: pl.BlockSpec(a.shape, lambda i: (0,) * a.ndim)
    wgu = shared_w_gu.astype(BF16)
    wdn = shared_w_down.astype(BF16)
    g2 = ln_g.reshape(1, d)
    b2 = ln_b.reshape(1, d)
    return pl.pallas_call(
        functools.partial(_combine_kernel, alpha=alpha),
        out_shape=jax.ShapeDtypeStruct((n_tok, d), F32),
        grid=(n_tiles,),
        in_specs=[pl.BlockSpec((TOP_K, tm), lambda i: (0, i), memory_space=pltpu.SMEM),
                  pl.BlockSpec((TOP_K, tm), lambda i: (0, jnp.minimum(i + 1, n_tiles - 1)),
                               memory_space=pltpu.SMEM),
                  pl.BlockSpec((tm, TOP_K), lambda i: (i, 0)),
                  pl.BlockSpec((tm, d), lambda i: (i, 0)),
                  pl.BlockSpec(memory_space=pl.ANY),
                  full(wgu), full(wdn), full(g2), full(b2)],
        out_specs=pl.BlockSpec((tm, d), lambda i: (i, 0)),
        scratch_shapes=[pltpu.VMEM((2, TOP_K, tm, d), F32), pltpu.SemaphoreType.DMA((2,))],
        compiler_params=_cparams(("arbitrary",), 48),
        name="combine",
    )(dest, dest, gate_t, h, ys, wgu, wdn, g2, b2)


def _moe(h, router_w, router_bias, w_gu, w_down, shared_w_gu, shared_w_down, ln_g, ln_b, alpha):
    top_e, gate, rank, cnt = _router(h, router_w, router_bias)
    counts = cnt[:, 0].astype(jnp.int32)
    starts = jnp.cumsum(counts) - counts
    dest = _dest(top_e, rank, starts)
    xs = _dispatch(h, dest)
    ys = _experts(xs, counts, w_gu, w_down)
    return _combine(h, ys, dest, gate.T, shared_w_gu, shared_w_down, ln_g, ln_b, alpha)


def kernel(x, w_in, att_norm_g, lam_re, lam_im, log_step, b_re, b_im, c_re, c_im, d_skip, w_glu, b_glu,
           ssm_norm_g, w_out, ln1_g, ln1_b, router_w, router_bias, w_gu, w_down, shared_w_gu,
           shared_w_down, ln2_g, ln2_b):
    bsz, seq, d = x.shape
    depth = w_in.shape[0]
    alpha = (2 * depth) ** 0.25
    h = x.reshape(bsz * seq, d)
    for i in range(depth):
        proj = _inproj(h, w_in[i].astype(BF16), seq)
        o_att = _attention(proj, bsz, seq)
        y_ssm = _s5(proj.reshape(bsz, seq, -1), 3 * ATT_WIDTH, lam_re[i], lam_im[i], log_step[i],
                    b_re[i], b_im[i], c_re[i], c_im[i], d_skip[i])
        h = _mixout(o_att, y_ssm.reshape(bsz * seq, -1), h, w_glu[i], b_glu[i], att_norm_g[i],
                    ssm_norm_g[i], w_out[i], ln1_g[i], ln1_b[i], alpha)
        h = _moe(h, router_w[i], router_bias[i], w_gu[i], w_down[i], shared_w_gu[i], shared_w_down[i],
                 ln2_g[i], ln2_b[i], alpha)
    return h.reshape(bsz, seq, d)
```

```python
import functools
import math

import jax
import jax.numpy as jnp
from jax import lax
from jax.experimental import pallas as pl
from jax.experimental.pallas import tpu as pltpu

F32 = jnp.float32
BF16 = jnp.bfloat16

ATT_HEADS = 8
HEAD_DIM = 64
ATT_WIDTH = ATT_HEADS * HEAD_DIM
SSM_CH = 16
SSM_STATE = 64
ROPE_THETA = 500000.0
ROT_DIM = HEAD_DIM // 4
DILATIONS = (1, 4, 16)
ATT_BLOCK = 128
ATT_GROUP = 4
N_EXPERTS = 256
TOP_K = 8
N_EXPERT_GROUPS = 8
TOPK_GROUPS = 4
ROUTED_SCALE = 2.5
LN_EPS = 1e-5
RMS_EPS = 1e-6

LANES = 128
SUBLANES = 8
EXPERT_ROWS = 128
EXPERT_RING = 4
ROW_TILE = 8
NEG_INF = float("-inf")


def _cparams(sem, vmem_mb):
    return pltpu.CompilerParams(dimension_semantics=sem, vmem_limit_bytes=vmem_mb * 1024 * 1024)


def _inproj_kernel(x_ref, w_ref, cos_ref, sa_ref, sb_ref, o_ref, *, n_rot_cols):
    xb = x_ref[...].astype(BF16)
    cosf = cos_ref[...]
    sa = sa_ref[...]
    sb = sb_ref[...]
    width = o_ref.shape[1]
    chunk = 512
    for c in range(width // chunk):
        r = jnp.dot(xb, w_ref[:, c * chunk:(c + 1) * chunk], preferred_element_type=F32)
        if c * chunk < n_rot_cols:
            parts = []
            for s in range(chunk // LANES):
                t = r[:, s * LANES:(s + 1) * LANES]
                parts.append(t * cosf + pltpu.roll(t, LANES - ROT_DIM // 2, 1) * sa
                             + pltpu.roll(t, ROT_DIM // 2, 1) * sb)
            r = jnp.concatenate(parts, axis=1)
        o_ref[:, c * chunk:(c + 1) * chunk] = r


def _rope_lane_tables(seq):
    half = ROT_DIM // 2
    inv_freq = jnp.power(jnp.float32(ROPE_THETA), -jnp.arange(half, dtype=F32) / half)
    ang = jnp.arange(seq, dtype=F32)[:, None] * inv_freq[None, :]
    cos, sin = jnp.cos(ang), jnp.sin(ang)
    rest = HEAD_DIM - ROT_DIM
    cos_h = jnp.concatenate([cos, cos, jnp.ones((seq, rest), F32)], axis=1)
    sa_h = jnp.concatenate([-sin, jnp.zeros((seq, half + rest), F32)], axis=1)
    sb_h = jnp.concatenate([jnp.zeros((seq, half), F32), sin, jnp.zeros((seq, rest), F32)], axis=1)
    rep = LANES // HEAD_DIM
    return tuple(jnp.tile(t, (1, rep)) for t in (cos_h, sa_h, sb_h))


def _inproj(x2d, w_in_bf, seq):
    n_tok, d = x2d.shape
    width = w_in_bf.shape[1]
    tm = 512
    cosf, sa, sb = _rope_lane_tables(seq)
    tab_spec = pl.BlockSpec((tm, LANES), lambda i: (i % (seq // tm), 0))
    return pl.pallas_call(
        functools.partial(_inproj_kernel, n_rot_cols=2 * ATT_WIDTH),
        out_shape=jax.ShapeDtypeStruct((n_tok, width), F32),
        grid=(n_tok // tm,),
        in_specs=[pl.BlockSpec((tm, d), lambda i: (i, 0)),
                  pl.BlockSpec((d, width), lambda i: (0, 0)),
                  tab_spec, tab_spec, tab_spec],
        out_specs=pl.BlockSpec((tm, width), lambda i: (i, 0)),
        compiler_params=_cparams(("parallel",), 48),
        name="inproj",
    )(x2d, w_in_bf, cosf, sa, sb)


def _attn_kernel(q_ref, k_ref, v_ref, o_ref, ob_ref, lb_ref, band_ref, causal_ref, *, seq):
    blk = ATT_BLOCK
    lane = lax.broadcasted_iota(jnp.int32, (1, LANES), 1)
    head0 = lane < HEAD_DIM
    scale = HEAD_DIM ** -0.5

    qi = lax.broadcasted_iota(jnp.int32, (blk, 2 * blk), 0)
    kj = lax.broadcasted_iota(jnp.int32, (blk, 2 * blk), 1)
    dist = qi + blk - kj
    band_ref[...] = jnp.where((dist >= 0) & (dist <= blk), 0.0, NEG_INF)
    causal_ref[...] = jnp.where(lax.broadcasted_iota(jnp.int32, (blk, blk), 0)
                                >= lax.broadcasted_iota(jnp.int32, (blk, blk), 1), 0.0, NEG_INF)

    def rows(start, d):
        if d == 1:
            return pl.ds(start, blk)
        return pl.ds(start, blk, stride=d)

    def one_block(c, d, start_q, with_prev):
        sl_q = rows(start_q, d)
        q = q_ref[sl_q, :] * scale
        kk = k_ref[sl_q, :]
        vv = v_ref[sl_q, :]
        if with_prev:
            sl_p = rows(start_q - d * blk, d)
            kk = jnp.concatenate([k_ref[sl_p, :], kk], axis=0)
            vv = jnp.concatenate([v_ref[sl_p, :], vv], axis=0)
            bias_ref = band_ref
        else:
            bias_ref = causal_ref
        kk = kk.astype(BF16)
        vv = vv.astype(BF16)
        outs = []
        lses = []
        for h in range(LANES // HEAD_DIM):
            hm = head0 if h == 0 else jnp.logical_not(head0)
            qh = jnp.where(hm, q, 0.0).astype(BF16)
            s = lax.dot_general(qh, kk, (((1,), (1,)), ((), ())), preferred_element_type=F32)
            s = s + bias_ref[...]
            m = jnp.max(s, axis=-1, keepdims=True)
            p = jnp.exp(s - m)
            den = jnp.sum(p, axis=-1, keepdims=True)
            outs.append(jnp.dot(p.astype(BF16), vv, preferred_element_type=F32) / den)
            lses.append(m + jnp.log(den))
        ob_ref[c, sl_q, :] = jnp.where(head0, outs[0], outs[1])
        lb_ref[c, sl_q, :] = jnp.where(head0, lses[0], lses[1])

    def run_blocks(n_blocks, group, fn):
        assert n_blocks % group == 0
        if n_blocks == group:
            for g in range(group):
                fn(g)
            return
        def body(it, carry):
            for g in range(group):
                fn(it * group + g)
            return carry
        lax.fori_loop(0, n_blocks // group, body, 0)

    for c, d in enumerate(DILATIONS):
        nb = seq // d // blk
        run_blocks(d, min(d, ATT_GROUP), lambda r, c=c, d=d: one_block(c, d, r, False))
        rest = (nb - 1) * d
        if rest:
            def later(idx, c=c, d=d):
                n = idx // d
                r = idx - n * d
                one_block(c, d, r + (n + 1) * (d * blk), True)
            group = max(g for g in range(1, ATT_GROUP + 1) if rest % g == 0)
            run_blocks(rest, group, later)

    rc = 256
    def merge(i, carry):
        sl = pl.ds(pl.multiple_of(i * rc, rc), rc)
        l0 = lb_ref[0, sl, :]
        l1 = lb_ref[1, sl, :]
        l2 = lb_ref[2, sl, :]
        mx = jnp.maximum(jnp.maximum(l0, l1), l2)
        e0 = jnp.exp(l0 - mx)
        e1 = jnp.exp(l1 - mx)
        e2 = jnp.exp(l2 - mx)
        tot = e0 + e1 + e2
        o_ref[sl, :] = ((e0 / tot) * ob_ref[0, sl, :] + (e1 / tot) * ob_ref[1, sl, :]
                        + (e2 / tot) * ob_ref[2, sl, :])
        return carry
    lax.fori_loop(0, seq // rc, merge, 0)


def _attention(proj, bsz, seq):
    n_tok = proj.shape[0]
    pairs = ATT_WIDTH // LANES
    assert seq % (ATT_BLOCK * max(DILATIONS)) == 0
    blk = (seq, LANES)
    return pl.pallas_call(
        functools.partial(_attn_kernel, seq=seq),
        out_shape=jax.ShapeDtypeStruct((n_tok, ATT_WIDTH), F32),
        grid=(bsz, pairs),
        in_specs=[pl.BlockSpec(blk, lambda b, h: (b, h)),
                  pl.BlockSpec(blk, lambda b, h: (b, pairs + h)),
                  pl.BlockSpec(blk, lambda b, h: (b, 2 * pairs + h))],
        out_specs=pl.BlockSpec(blk, lambda b, h: (b, h)),
        scratch_shapes=[pltpu.VMEM((len(DILATIONS), seq, LANES), F32),
                        pltpu.VMEM((len(DILATIONS), seq, LANES), F32),
                        pltpu.VMEM((ATT_BLOCK, 2 * ATT_BLOCK), F32),
                        pltpu.VMEM((ATT_BLOCK, ATT_BLOCK), F32)],
        compiler_params=_cparams(("parallel", "parallel"), 32),
        name="attn",
    )(proj, proj, proj)


def _s5_kernel(u_ref, bm_ref, lam_ref, cm_ref, dk_ref, o_ref, us_ref, st_ref, ys_ref, carry_ref, *, tc):
    bsz = u_ref.shape[0]
    half = st_ref.shape[1] // 2
    rows = tc * bsz
    mm_rows = 512

    @pl.when(pl.program_id(1) == 0)
    def _():
        carry_ref[...] = jnp.zeros_like(carry_ref)

    for b in range(bsz):
        us_ref[pl.ds(b, tc, stride=bsz), :] = u_ref[b]

    bm = bm_ref[0]
    for r0 in range(0, rows, mm_rows):
        st_ref[r0:r0 + mm_rows, :] = jnp.dot(us_ref[r0:r0 + mm_rows, :].astype(BF16), bm,
                                             preferred_element_type=F32)

    lam = lam_ref[0]
    lam_re = lam[:, :half]
    lam_im = lam[:, half:]

    def step(t, carry):
        xr, xi = carry
        sl = pl.ds(pl.multiple_of(t * bsz, bsz), bsz)
        nr = lam_re * xr - lam_im * xi + st_ref[sl, :half]
        ni = lam_re * xi + lam_im * xr + st_ref[sl, half:]
        st_ref[sl, :half] = nr
        st_ref[sl, half:] = ni
        return nr, ni

    xr, xi = lax.fori_loop(0, tc, step, (carry_ref[:, :half], carry_ref[:, half:]), unroll=4)
    carry_ref[:, :half] = xr
    carry_ref[:, half:] = xi

    cm = cm_ref[0]
    for r0 in range(0, rows, mm_rows):
        ys_ref[r0:r0 + mm_rows, :] = jnp.dot(st_ref[r0:r0 + mm_rows, :].astype(BF16), cm,
                                             preferred_element_type=F32)
    dk = dk_ref[...]
    for b in range(bsz):
        o_ref[b] = ys_ref[pl.ds(b, tc, stride=bsz), :] + dk * u_ref[b]


def _s5_params(lam_re, lam_im, log_step, b_re, b_im, c_re, c_im, bsz):
    groups = lam_re.shape[0]
    gpc = LANES // SSM_CH
    n_chunks = groups // gpc
    lam = lax.complex(lam_re.astype(F32), lam_im.astype(F32))
    step = jnp.exp(log_step.astype(F32))[:, None]
    lam_bar = jnp.exp(lam * step)
    bmat = lax.complex(b_re.astype(F32), b_im.astype(F32))
    b_bar = ((lam_bar - 1.0) / lam)[..., None] * bmat
    eye = jnp.eye(gpc, dtype=F32)

    def block_diag_in(t):
        t = t.reshape(n_chunks, gpc, SSM_STATE, SSM_CH)
        return jnp.einsum('ngpc,gh->ngchp', t, eye).reshape(n_chunks, gpc * SSM_CH, gpc * SSM_STATE)

    def block_diag_out(t):
        t = t.reshape(n_chunks, gpc, SSM_CH, SSM_STATE)
        return jnp.einsum('ngcp,gh->ngphc', t, eye).reshape(n_chunks, gpc * SSM_STATE, gpc * SSM_CH)

    bm = jnp.concatenate([block_diag_in(b_bar.real), block_diag_in(b_bar.imag)], axis=2).astype(BF16)
    cm = jnp.concatenate([block_diag_out(c_re.astype(F32)), block_diag_out(-c_im.astype(F32))],
                         axis=1).astype(BF16)
    lam_row = jnp.concatenate([lam_bar.real.reshape(n_chunks, gpc * SSM_STATE),
                               lam_bar.imag.reshape(n_chunks, gpc * SSM_STATE)], axis=1)
    lam_t = jnp.broadcast_to(lam_row[:, None, :], (n_chunks, bsz, 2 * gpc * SSM_STATE))
    return bm, lam_t, cm, n_chunks


def _s5(proj3, u_col0, lam_re, lam_im, log_step, b_re, b_im, c_re, c_im, d_skip):
    bsz, seq, _ = proj3.shape
    assert bsz == SUBLANES
    bm, lam_t, cm, n_chunks = _s5_params(lam_re, lam_im, log_step, b_re, b_im, c_re, c_im, bsz)
    width = n_chunks * LANES
    tc = 256
    st_cols = bm.shape[2]
    ublk0 = u_col0 // LANES
    return pl.pallas_call(
        functools.partial(_s5_kernel, tc=tc),
        out_shape=jax.ShapeDtypeStruct((bsz, seq, width), F32),
        grid=(n_chunks, seq // tc),
        in_specs=[pl.BlockSpec((bsz, tc, LANES), lambda c, t: (0, t, ublk0 + c)),
                  pl.BlockSpec((1, LANES, st_cols), lambda c, t: (c, 0, 0)),
                  pl.BlockSpec((1, bsz, st_cols), lambda c, t: (c, 0, 0)),
                  pl.BlockSpec((1, st_cols, LANES), lambda c, t: (c, 0, 0)),
                  pl.BlockSpec((1, LANES), lambda c, t: (0, c))],
        out_specs=pl.BlockSpec((bsz, tc, LANES), lambda c, t: (0, t, c)),
        scratch_shapes=[pltpu.VMEM((tc * bsz, LANES), F32),
                        pltpu.VMEM((tc * bsz, st_cols), F32),
                        pltpu.VMEM((tc * bsz, LANES), F32),
                        pltpu.VMEM((bsz, st_cols), F32)],
        compiler_params=_cparams(("arbitrary", "arbitrary"), 40),
        name="s5",
    )(proj3, bm, lam_t, cm, d_skip.reshape(1, width).astype(F32))


def _layer_norm(v, g, b):
    mu = jnp.mean(v, axis=-1, keepdims=True)
    var = jnp.mean(jnp.square(v - mu), axis=-1, keepdims=True)
    return (v - mu) * lax.rsqrt(var + LN_EPS) * g + b


def _rms_norm(v, g):
    return v * lax.rsqrt(jnp.mean(jnp.square(v), axis=-1, keepdims=True) + RMS_EPS) * g


def _mixout_kernel(att_ref, ssm_ref, x_ref, wglu_ref, bglu_ref, ag_ref, sg_ref, wout_ref, g_ref, b_ref,
                   o_ref, *, alpha):
    y = jax.nn.gelu(ssm_ref[...])
    z = jnp.dot(y.astype(BF16), wglu_ref[...], preferred_element_type=F32) + bglu_ref[...]
    o_ssm = y * jax.nn.sigmoid(z)
    a = _rms_norm(att_ref[...], ag_ref[...]).astype(BF16)
    s = _rms_norm(o_ssm, sg_ref[...]).astype(BF16)
    wa = att_ref.shape[1]
    mix = (jnp.dot(a, wout_ref[:wa, :], preferred_element_type=F32)
           + jnp.dot(s, wout_ref[wa:, :], preferred_element_type=F32))
    o_ref[...] = _layer_norm(alpha * x_ref[...] + mix, g_ref[...], b_ref[...])


def _mixout(o_att, y_ssm, x2d, w_glu, b_glu, att_g, ssm_g, w_out, ln_g, ln_b, alpha):
    n_tok, d = x2d.shape
    wa = o_att.shape[1]
    ws = y_ssm.shape[1]
    tm = 256
    row = lambda w: pl.BlockSpec((tm, w), lambda i: (i, 0))
    full = lambda a: pl.BlockSpec(a.shape, lambda i: (0,) * a.ndim)
    args = (o_att, y_ssm, x2d, w_glu.astype(BF16), b_glu.reshape(1, ws), att_g.reshape(1, wa),
            ssm_g.reshape(1, ws), w_out.astype(BF16), ln_g.reshape(1, d), ln_b.reshape(1, d))
    return pl.pallas_call(
        functools.partial(_mixout_kernel, alpha=alpha),
        out_shape=jax.ShapeDtypeStruct((n_tok, d), F32),
        grid=(n_tok // tm,),
        in_specs=[row(wa), row(ws), row(d)] + [full(a) for a in args[3:]],
        out_specs=row(d),
        compiler_params=_cparams(("parallel",), 32),
        name="mixout",
    )(*args)


def _split_bf16(v):
    hi = v.astype(BF16)
    lo = (v - hi.astype(F32)).astype(BF16)
    return hi, lo


def _router_kernel(h_ref, wt_ref, bias_ref, e_ref, g_ref, r_ref, cnt_ref, run_ref):
    tm = h_ref.shape[0]
    n_exp = wt_ref.shape[0]
    gsz = n_exp // N_EXPERT_GROUPS

    @pl.when(pl.program_id(0) == 0)
    def _():
        run_ref[...] = jnp.zeros_like(run_ref)

    w_hi, w_lo = _split_bf16(wt_ref[...])
    h_hi, h_lo = _split_bf16(h_ref[...])
    nt = (((1,), (1,)), ((), ()))
    logits = (lax.dot_general(w_hi, h_hi, nt, preferred_element_type=F32)
              + lax.dot_general(w_hi, h_lo, nt, preferred_element_type=F32)
              + lax.dot_general(w_lo, h_hi, nt, preferred_element_type=F32))
    scores = jax.nn.sigmoid(logits)
    choice = scores + bias_ref[:, 0:1]

    gio = lax.broadcasted_iota(jnp.int32, (gsz, tm), 0).astype(F32)
    gscore = []
    for g in range(N_EXPERT_GROUPS):
        cg = choice[g * gsz:(g + 1) * gsz, :]
        m1 = jnp.max(cg, axis=0, keepdims=True)
        i1 = jnp.min(jnp.where(cg == m1, gio, float(gsz)), axis=0, keepdims=True)
        m2 = jnp.max(jnp.where(gio == i1, NEG_INF, cg), axis=0, keepdims=True)
        gscore.append(m1 + m2)
    masked = []
    for g in range(N_EXPERT_GROUPS):
        beat = jnp.zeros((1, tm), F32)
        for o in range(N_EXPERT_GROUPS):
            if o == g:
                continue
            wins = (gscore[o] >= gscore[g]) if o < g else (gscore[o] > gscore[g])
            beat = beat + jnp.where(wins, 1.0, 0.0)
        keep = beat < float(TOPK_GROUPS)
        masked.append(jnp.where(keep, choice[g * gsz:(g + 1) * gsz, :], NEG_INF))
    cur = jnp.concatenate(masked, axis=0)

    eio = lax.broadcasted_iota(jnp.int32, (n_exp, tm), 0).astype(F32)
    idxs = []
    gates = []
    onehot = jnp.zeros((n_exp, tm), F32)
    for _ in range(TOP_K):
        m = jnp.max(cur, axis=0, keepdims=True)
        idx = jnp.min(jnp.where(cur == m, eio, float(n_exp)), axis=0, keepdims=True)
        hit = eio == idx
        idxs.append(idx)
        gates.append(jnp.sum(jnp.where(hit, scores, 0.0), axis=0, keepdims=True))
        cur = jnp.where(hit, NEG_INF, cur)
        onehot = onehot + jnp.where(hit, 1.0, 0.0)
    gate = jnp.concatenate(gates, axis=0)
    gate = ROUTED_SCALE * gate / (jnp.sum(gate, axis=0, keepdims=True) + 1e-20)

    si = lax.broadcasted_iota(jnp.int32, (tm, tm), 0)
    ti = lax.broadcasted_iota(jnp.int32, (tm, tm), 1)
    upper = jnp.where(si < ti, 1.0, 0.0).astype(BF16)
    before = jnp.dot(onehot.astype(BF16), upper, preferred_element_type=F32) + run_ref[:, 0:1]
    ranks = [jnp.sum(jnp.where(eio == idx, before, 0.0), axis=0, keepdims=True) for idx in idxs]

    e_ref[...] = jnp.concatenate(idxs, axis=0).astype(jnp.int32)
    g_ref[...] = gate
    r_ref[...] = jnp.concatenate(ranks, axis=0).astype(jnp.int32)
    run_ref[...] = run_ref[...] + jnp.sum(onehot, axis=1, keepdims=True)
    cnt_ref[...] = run_ref[...]


def _router(h, router_w, router_bias):
    n_tok, d = h.shape
    n_exp = router_w.shape[1]
    tm = 256
    wt = router_w.astype(F32).T
    bias = jnp.broadcast_to(router_bias.astype(F32)[:, None], (n_exp, LANES))
    tok = pl.BlockSpec((TOP_K, tm), lambda i: (0, i))
    return pl.pallas_call(
        _router_kernel,
        out_shape=(jax.ShapeDtypeStruct((TOP_K, n_tok), jnp.int32),
                   jax.ShapeDtypeStruct((TOP_K, n_tok), F32),
                   jax.ShapeDtypeStruct((TOP_K, n_tok), jnp.int32),
                   jax.ShapeDtypeStruct((n_exp, LANES), F32)),
        grid=(n_tok // tm,),
        in_specs=[pl.BlockSpec((tm, d), lambda i: (i, 0)),
                  pl.BlockSpec((n_exp, d), lambda i: (0, 0)),
                  pl.BlockSpec((n_exp, LANES), lambda i: (0, 0))],
        out_specs=(tok, tok, tok, pl.BlockSpec((n_exp, LANES), lambda i: (0, 0))),
        scratch_shapes=[pltpu.VMEM((n_exp, LANES), F32)],
        compiler_params=_cparams(("arbitrary",), 32),
        name="router",
    )(h, wt, bias)


def _dest_kernel(e_ref, r_ref, st_ref, d_ref):
    n_exp = st_ref.shape[0]
    tm = e_ref.shape[1]
    eio = lax.broadcasted_iota(jnp.int32, (n_exp, tm), 0)
    start = st_ref[:, 0:1]
    rows = [jnp.sum(jnp.where(eio == e_ref[k:k + 1, :], start, 0.0), axis=0, keepdims=True)
            for k in range(TOP_K)]
    d_ref[...] = jnp.concatenate(rows, axis=0).astype(jnp.int32) + r_ref[...]


def _dest(top_e, rank, starts):
    n_tok = top_e.shape[1]
    n_exp = starts.shape[0]
    tm = 512
    st = jnp.broadcast_to(starts.astype(F32)[:, None], (n_exp, LANES))
    tok = pl.BlockSpec((TOP_K, tm), lambda i: (0, i))
    return pl.pallas_call(
        _dest_kernel,
        out_shape=jax.ShapeDtypeStruct((TOP_K, n_tok), jnp.int32),
        grid=(n_tok // tm,),
        in_specs=[tok, tok, pl.BlockSpec((n_exp, LANES), lambda i: (0, 0))],
        out_specs=tok,
        compiler_params=_cparams(("parallel",), 32),
        name="dest",
    )(top_e, rank, st)


def _to_row_tiles(dst_ref, slot, val):
    rows = val.shape[0]
    for j in range(ROW_TILE):
        dst_ref[slot, pl.ds(j, rows, stride=ROW_TILE), :] = val[:, j * LANES:(j + 1) * LANES]


def _from_row_tiles(src_ref, idx, rows):
    return jnp.concatenate([src_ref[(*idx, pl.ds(j, rows, stride=ROW_TILE), slice(None))]
                            for j in range(ROW_TILE)], axis=1)


def _row_tile(r):
    return pl.ds(pl.multiple_of(r * ROW_TILE, ROW_TILE), ROW_TILE)


def _dispatch_kernel(dest_ref, h_ref, xs_ref, ht_ref, sem):
    tm = h_ref.shape[0]
    i = pl.program_id(0)
    cur = i % 2
    _to_row_tiles(ht_ref, cur, h_ref[...])

    def issue(t, carry):
        for k in range(TOP_K):
            pltpu.make_async_copy(ht_ref.at[cur, _row_tile(t)], xs_ref.at[_row_tile(dest_ref[k, t])],
                                  sem.at[cur]).start(priority=k % 2)
        return carry
    lax.fori_loop(0, tm, issue, 0, unroll=2)

    def drain(slot):
        def body(t, carry):
            for k in range(TOP_K):
                pltpu.make_async_copy(ht_ref.at[slot, _row_tile(0)], xs_ref.at[_row_tile(0)],
                                      sem.at[slot]).wait()
            return carry
        lax.fori_loop(0, tm, body, 0)

    @pl.when(i > 0)
    def _():
        drain(1 - cur)

    @pl.when(i == pl.num_programs(0) - 1)
    def _():
        drain(cur)


def _dispatch(h, dest):
    n_tok, d = h.shape
    assert d == ROW_TILE * LANES
    tm = 256
    return pl.pallas_call(
        _dispatch_kernel,
        out_shape=jax.ShapeDtypeStruct((n_tok * TOP_K * ROW_TILE, LANES), h.dtype),
        grid=(n_tok // tm,),
        in_specs=[pl.BlockSpec((TOP_K, tm), lambda i: (0, i), memory_space=pltpu.SMEM),
                  pl.BlockSpec((tm, d), lambda i: (i, 0))],
        out_specs=pl.BlockSpec(memory_space=pl.ANY),
        scratch_shapes=[pltpu.VMEM((2, tm * ROW_TILE, LANES), h.dtype), pltpu.SemaphoreType.DMA((2,))],
        compiler_params=_cparams(("arbitrary",), 32),
        name="dispatch",
    )(dest, h)


def _experts_kernel(vblk_ref, vlo_ref, vhi_ref, voff_ref, vend_ref, nvis_ref,
                    xs_ref, wgu_ref, wdn_ref, ys_ref,
                    xbuf, acc, wgu_bf, wdn_bf, xsem, ysem):
    e = pl.program_id(0)
    n_vis = nvis_ref[0]
    trows = xbuf.shape[1]
    rows = trows // ROW_TILE
    ff = wdn_bf.shape[0]

    def x_copy(v):
        slot = v % EXPERT_RING
        return pltpu.make_async_copy(xs_ref.at[pl.ds(pl.multiple_of(vblk_ref[v] * trows, trows), trows)],
                                     xbuf.at[slot], xsem.at[slot])

    def y_copy(blk):
        slot = blk % 2
        return pltpu.make_async_copy(acc.at[slot], ys_ref.at[pl.ds(pl.multiple_of(blk * trows, trows), trows)],
                                     ysem.at[slot])

    @pl.when(e == 0)
    def _():
        for i in range(EXPERT_RING):
            @pl.when(i < n_vis)
            def _():
                x_copy(i).start()

    wgu_bf[...] = wgu_ref[0].astype(BF16)
    wdn_bf[...] = wdn_ref[0].astype(BF16)

    def visit(v, carry):
        blk = vblk_ref[v]
        slot = blk % 2

        @pl.when((v == 0) | (blk != vblk_ref[jnp.maximum(v - 1, 0)]))
        def _():
            @pl.when(blk >= 1)
            def _():
                y_copy(blk - 1).start()

            @pl.when(blk >= 2)
            def _():
                y_copy(blk - 2).wait()
            acc[slot] = jnp.zeros(acc.shape[1:], F32)

        x_copy(v).wait()
        xb = _from_row_tiles(xbuf, (v % EXPERT_RING,), rows).astype(BF16)
        gu = jnp.dot(xb, wgu_bf[...], preferred_element_type=F32)
        act = (jax.nn.silu(gu[:, :ff]) * gu[:, ff:]).astype(BF16)
        y = jnp.dot(act, wdn_bf[...], preferred_element_type=F32)
        row = lax.broadcasted_iota(jnp.int32, (rows, 1), 0)
        mine = (row >= vlo_ref[v]) & (row < vhi_ref[v])
        y = jnp.where(mine, y, 0.0)
        for j in range(ROW_TILE):
            acc[slot, pl.ds(j, rows, stride=ROW_TILE), :] += y[:, j * LANES:(j + 1) * LANES]

        @pl.when(v + EXPERT_RING < n_vis)
        def _():
            x_copy(v + EXPERT_RING).start()

        @pl.when(v == n_vis - 1)
        def _():
            y_copy(blk).start()

            @pl.when(blk >= 1)
            def _():
                y_copy(blk - 1).wait()
            y_copy(blk).wait()
        return carry

    lax.fori_loop(voff_ref[e], vend_ref[e], visit, 0)


def _visit_list(counts, n_rows):
    n_exp = counts.shape[0]
    n_blocks = n_rows // EXPERT_ROWS
    n_vis_max = n_blocks + n_exp
    ends = jnp.cumsum(counts)
    starts = ends - counts
    fb = starts // EXPERT_ROWS
    lb = jnp.maximum(ends - 1, 0) // EXPERT_ROWS
    nv = jnp.where(counts > 0, lb - fb + 1, 0)
    vend = jnp.cumsum(nv)
    voff = vend - nv
    n_vis = vend[-1]
    v = jnp.arange(n_vis_max, dtype=jnp.int32)
    vc = jnp.minimum(v, n_vis - 1)
    e = jnp.sum(vend[None, :] <= vc[:, None], axis=1).astype(jnp.int32)
    pick = e[:, None] == jnp.arange(n_exp, dtype=jnp.int32)[None, :]
    take = lambda tab: jnp.sum(jnp.where(pick, tab[None, :], 0), axis=1)
    blk = take(fb) + (vc - take(voff))
    lo = jnp.maximum(take(starts), blk * EXPERT_ROWS) - blk * EXPERT_ROWS
    hi = jnp.minimum(take(ends), (blk + 1) * EXPERT_ROWS) - blk * EXPERT_ROWS
    i32 = lambda a: a.astype(jnp.int32)
    return i32(blk), i32(lo), i32(hi), i32(voff), i32(vend), i32(n_vis).reshape(1)


def _experts(xs, counts, w_gu, w_down):
    n_exp, d, ff2 = w_gu.shape
    ff = w_down.shape[1]
    n_rows = xs.shape[0] // ROW_TILE
    assert n_rows % EXPERT_ROWS == 0 and d == ROW_TILE * LANES
    prefetch = _visit_list(counts, n_rows)
    grid_spec = pltpu.PrefetchScalarGridSpec(
        num_scalar_prefetch=len(prefetch),
        grid=(n_exp,),
        in_specs=[pl.BlockSpec(memory_space=pl.ANY),
                  pl.BlockSpec((1, d, ff2), lambda e, *_: (e, 0, 0)),
                  pl.BlockSpec((1, ff, d), lambda e, *_: (e, 0, 0))],
        out_specs=pl.BlockSpec(memory_space=pl.ANY),
        scratch_shapes=[pltpu.VMEM((EXPERT_RING, EXPERT_ROWS * ROW_TILE, LANES), F32),
                        pltpu.VMEM((2, EXPERT_ROWS * ROW_TILE, LANES), F32),
                        pltpu.VMEM((d, ff2), BF16), pltpu.VMEM((ff, d), BF16),
                        pltpu.SemaphoreType.DMA((EXPERT_RING,)),
                        pltpu.SemaphoreType.DMA((2,))],
    )
    return pl.pallas_call(
        _experts_kernel,
        out_shape=jax.ShapeDtypeStruct(xs.shape, F32),
        grid_spec=grid_spec,
        compiler_params=_cparams(("arbitrary",), 32),
        name="experts",
    )(*prefetch, xs, w_gu, w_down)


def _combine_kernel(dest_ref, dnext_ref, gate_ref, h_ref, ys_ref, wgu_ref, wdn_ref, g_ref, b_ref, o_ref,
                    buf_ref, sem, *, alpha):
    tm = h_ref.shape[0]
    i = pl.program_id(0)
    cur = i % 2

    def gather(d_ref, slot):
        def issue(t, carry):
            for k in range(TOP_K):
                pltpu.make_async_copy(ys_ref.at[_row_tile(d_ref[k, t])], buf_ref.at[slot, k, _row_tile(t)],
                                      sem.at[slot]).start(priority=k % 2)
            return carry
        lax.fori_loop(0, tm, issue, 0, unroll=2)

    @pl.when(i == 0)
    def _():
        gather(dest_ref, 0)

    @pl.when(i + 1 < pl.num_programs(0))
    def _():
        gather(dnext_ref, 1 - cur)

    h = h_ref[...]
    ff = wdn_ref.shape[0]
    gu = jnp.dot(h.astype(BF16), wgu_ref[...], preferred_element_type=F32)
    act = (jax.nn.silu(gu[:, :ff]) * gu[:, ff:]).astype(BF16)
    acc = alpha * h + jnp.dot(act, wdn_ref[...], preferred_element_type=F32)

    def drain(t, carry):
        for k in range(TOP_K):
            pltpu.make_async_copy(ys_ref.at[_row_tile(0)], buf_ref.at[cur, 0, _row_tile(0)], sem.at[cur]).wait()
        return carry
    lax.fori_loop(0, tm, drain, 0)

    gate = gate_ref[...]
    parts = []
    for j in range(ROW_TILE):
        part = gate[:, 0:1] * buf_ref[cur, 0, pl.ds(j, tm, stride=ROW_TILE), :]
        for k in range(1, TOP_K):
            part = part + gate[:, k:k + 1] * buf_ref[cur, k, pl.ds(j, tm, stride=ROW_TILE), :]
        parts.append(part)
    acc = acc + jnp.concatenate(parts, axis=1)
    o_ref[...] = _layer_norm(acc, g_ref[...], b_ref[...])


def _combine(h, ys, dest, gate_t, shared_w_gu, shared_w_down, ln_g, ln_b, alpha):
    n_tok, d = h.shape
    tm = 256
    n_tiles = n_tok // tm
    full = lambda a: pl.BlockSpec(a.shape, lambda i: (0,) * a.ndim)
    wgu = shared_w_gu.astype(BF16)
    wdn = shared_w_down.astype(BF16)
    g2 = ln_g.reshape(1, d)
    b2 = ln_b.reshape(1, d)
    return pl.pallas_call(
        functools.partial(_combine_kernel, alpha=alpha),
        out_shape=jax.ShapeDtypeStruct((n_tok, d), F32),
        grid=(n_tiles,),
        in_specs=[pl.BlockSpec((TOP_K, tm), lambda i: (0, i), memory_space=pltpu.SMEM),
                  pl.BlockSpec((TOP_K, tm), lambda i: (0, jnp.minimum(i + 1, n_tiles - 1)),
                               memory_space=pltpu.SMEM),
                  pl.BlockSpec((tm, TOP_K), lambda i: (i, 0)),
                  pl.BlockSpec((tm, d), lambda i: (i, 0)),
                  pl.BlockSpec(memory_space=pl.ANY),
                  full(wgu), full(wdn), full(g2), full(b2)],
        out_specs=pl.BlockSpec((tm, d), lambda i: (i, 0)),
        scratch_shapes=[pltpu.VMEM((2, TOP_K, tm * ROW_TILE, LANES), F32), pltpu.SemaphoreType.DMA((2,))],
        compiler_params=_cparams(("arbitrary",), 48),
        name="combine",
    )(dest, dest, gate_t, h, ys, wgu, wdn, g2, b2)


def _moe(h, router_w, router_bias, w_gu, w_down, shared_w_gu, shared_w_down, ln_g, ln_b, alpha):
    top_e, gate, rank, cnt = _router(h, router_w, router_bias)
    counts = cnt[:, 0].astype(jnp.int32)
    starts = jnp.cumsum(counts) - counts
    dest = _dest(top_e, rank, starts)
    xs = _dispatch(h, dest)
    ys = _experts(xs, counts, w_gu, w_down)
    return _combine(h, ys, dest, gate.T, shared_w_gu, shared_w_down, ln_g, ln_b, alpha)


def kernel(x, w_in, att_norm_g, lam_re, lam_im, log_step, b_re, b_im, c_re, c_im, d_skip, w_glu, b_glu,
           ssm_norm_g, w_out, ln1_g, ln1_b, router_w, router_bias, w_gu, w_down, shared_w_gu,
           shared_w_down, ln2_g, ln2_b):
    bsz, seq, d = x.shape
    depth = w_in.shape[0]
    alpha = (2 * depth) ** 0.25
    h = x.reshape(bsz * seq, d)
    for i in range(depth):
        proj = _inproj(h, w_in[i].astype(BF16), seq)
        o_att = _attention(proj, bsz, seq)
        y_ssm = _s5(proj.reshape(bsz, seq, -1), 3 * ATT_WIDTH, lam_re[i], lam_im[i], log_step[i],
                    b_re[i], b_im[i], c_re[i], c_im[i], d_skip[i])
        h = _mixout(o_att, y_ssm.reshape(bsz * seq, -1), h, w_glu[i], b_glu[i], att_norm_g[i],
                    ssm_norm_g[i], w_out[i], ln1_g[i], ln1_b[i], alpha)
        h = _moe(h, router_w[i], router_bias[i], w_gu[i], w_down[i], shared_w_gu[i], shared_w_down[i],
                 ln2_g[i], ln2_b[i], alpha)
    return h.reshape(bsz, seq, d)
```

```python
import functools
import math

import jax
import jax.numpy as jnp
from jax import lax
from jax.experimental import pallas as pl
from jax.experimental.pallas import tpu as pltpu

F32 = jnp.float32
BF16 = jnp.bfloat16

ATT_HEADS = 8
HEAD_DIM = 64
ATT_WIDTH = ATT_HEADS * HEAD_DIM
SSM_CH = 16
SSM_STATE = 64
ROPE_THETA = 500000.0
ROT_DIM = HEAD_DIM // 4
DILATIONS = (1, 4, 16)
ATT_BLOCK = 128
ATT_GROUP = 4
N_EXPERTS = 256
TOP_K = 8
N_EXPERT_GROUPS = 8
TOPK_GROUPS = 4
ROUTED_SCALE = 2.5
LN_EPS = 1e-5
RMS_EPS = 1e-6

LANES = 128
SUBLANES = 8
EXPERT_ROWS = 128
EXPERT_RING = 4
ROW_TILE = 8
NEG_INF = float("-inf")


def _cparams(sem, vmem_mb):
    return pltpu.CompilerParams(dimension_semantics=sem, vmem_limit_bytes=vmem_mb * 1024 * 1024)


def _inproj_kernel(x_ref, w_ref, cos_ref, sa_ref, sb_ref, o_ref, *, n_rot_cols):
    xb = x_ref[...].astype(BF16)
    cosf = cos_ref[...]
    sa = sa_ref[...]
    sb = sb_ref[...]
    width = o_ref.shape[1]
    chunk = 512
    for c in range(width // chunk):
        r = jnp.dot(xb, w_ref[:, c * chunk:(c + 1) * chunk], preferred_element_type=F32)
        if c * chunk < n_rot_cols:
            parts = []
            for s in range(chunk // LANES):
                t = r[:, s * LANES:(s + 1) * LANES]
                parts.append(t * cosf + pltpu.roll(t, LANES - ROT_DIM // 2, 1) * sa
                             + pltpu.roll(t, ROT_DIM // 2, 1) * sb)
            r = jnp.concatenate(parts, axis=1)
        o_ref[:, c * chunk:(c + 1) * chunk] = r


def _rope_lane_tables(seq):
    half = ROT_DIM // 2
    inv_freq = jnp.power(jnp.float32(ROPE_THETA), -jnp.arange(half, dtype=F32) / half)
    ang = jnp.arange(seq, dtype=F32)[:, None] * inv_freq[None, :]
    cos, sin = jnp.cos(ang), jnp.sin(ang)
    rest = HEAD_DIM - ROT_DIM
    cos_h = jnp.concatenate([cos, cos, jnp.ones((seq, rest), F32)], axis=1)
    sa_h = jnp.concatenate([-sin, jnp.zeros((seq, half + rest), F32)], axis=1)
    sb_h = jnp.concatenate([jnp.zeros((seq, half), F32), sin, jnp.zeros((seq, rest), F32)], axis=1)
    rep = LANES // HEAD_DIM
    return tuple(jnp.tile(t, (1, rep)) for t in (cos_h, sa_h, sb_h))


def _inproj(x2d, w_in_bf, seq):
    n_tok, d = x2d.shape
    width = w_in_bf.shape[1]
    tm = 512
    cosf, sa, sb = _rope_lane_tables(seq)
    tab_spec = pl.BlockSpec((tm, LANES), lambda i: (i % (seq // tm), 0))
    return pl.pallas_call(
        functools.partial(_inproj_kernel, n_rot_cols=2 * ATT_WIDTH),
        out_shape=jax.ShapeDtypeStruct((n_tok, width), F32),
        grid=(n_tok // tm,),
        in_specs=[pl.BlockSpec((tm, d), lambda i: (i, 0)),
                  pl.BlockSpec((d, width), lambda i: (0, 0)),
                  tab_spec, tab_spec, tab_spec],
        out_specs=pl.BlockSpec((tm, width), lambda i: (i, 0)),
        compiler_params=_cparams(("parallel",), 48),
        name="inproj",
    )(x2d, w_in_bf, cosf, sa, sb)


def _attn_kernel(q_ref, k_ref, v_ref, o_ref, ob_ref, lb_ref, band_ref, causal_ref, *, seq):
    blk = ATT_BLOCK
    lane = lax.broadcasted_iota(jnp.int32, (1, LANES), 1)
    head0 = lane < HEAD_DIM
    scale = HEAD_DIM ** -0.5

    qi = lax.broadcasted_iota(jnp.int32, (blk, 2 * blk), 0)
    kj = lax.broadcasted_iota(jnp.int32, (blk, 2 * blk), 1)
    dist = qi + blk - kj
    band_ref[...] = jnp.where((dist >= 0) & (dist <= blk), 0.0, NEG_INF)
    causal_ref[...] = jnp.where(lax.broadcasted_iota(jnp.int32, (blk, blk), 0)
                                >= lax.broadcasted_iota(jnp.int32, (blk, blk), 1), 0.0, NEG_INF)

    def rows(start, d):
        if d == 1:
            return pl.ds(start, blk)
        return pl.ds(start, blk, stride=d)

    def one_block(c, d, start_q, with_prev):
        sl_q = rows(start_q, d)
        q = q_ref[sl_q, :] * scale
        kk = k_ref[sl_q, :]
        vv = v_ref[sl_q, :]
        if with_prev:
            sl_p = rows(start_q - d * blk, d)
            kk = jnp.concatenate([k_ref[sl_p, :], kk], axis=0)
            vv = jnp.concatenate([v_ref[sl_p, :], vv], axis=0)
            bias_ref = band_ref
        else:
            bias_ref = causal_ref
        kk = kk.astype(BF16)
        vv = vv.astype(BF16)
        outs = []
        lses = []
        for h in range(LANES // HEAD_DIM):
            hm = head0 if h == 0 else jnp.logical_not(head0)
            qh = jnp.where(hm, q, 0.0).astype(BF16)
            s = lax.dot_general(qh, kk, (((1,), (1,)), ((), ())), preferred_element_type=F32)
            s = s + bias_ref[...]
            m = jnp.max(s, axis=-1, keepdims=True)
            p = jnp.exp(s - m)
            den = jnp.sum(p, axis=-1, keepdims=True)
            outs.append(jnp.dot(p.astype(BF16), vv, preferred_element_type=F32) / den)
            lses.append(m + jnp.log(den))
        ob_ref[c, sl_q, :] = jnp.where(head0, outs[0], outs[1])
        lb_ref[c, sl_q, :] = jnp.where(head0, lses[0], lses[1])

    def run_blocks(n_blocks, group, fn):
        assert n_blocks % group == 0
        if n_blocks == group:
            for g in range(group):
                fn(g)
            return
        def body(it, carry):
            for g in range(group):
                fn(it * group + g)
            return carry
        lax.fori_loop(0, n_blocks // group, body, 0)

    for c, d in enumerate(DILATIONS):
        nb = seq // d // blk
        run_blocks(d, min(d, ATT_GROUP), lambda r, c=c, d=d: one_block(c, d, r, False))
        rest = (nb - 1) * d
        if rest:
            def later(idx, c=c, d=d):
                n = idx // d
                r = idx - n * d
                one_block(c, d, r + (n + 1) * (d * blk), True)
            group = max(g for g in range(1, ATT_GROUP + 1) if rest % g == 0)
            run_blocks(rest, group, later)

    rc = 256
    def merge(i, carry):
        sl = pl.ds(pl.multiple_of(i * rc, rc), rc)
        l0 = lb_ref[0, sl, :]
        l1 = lb_ref[1, sl, :]
        l2 = lb_ref[2, sl, :]
        mx = jnp.maximum(jnp.maximum(l0, l1), l2)
        e0 = jnp.exp(l0 - mx)
        e1 = jnp.exp(l1 - mx)
        e2 = jnp.exp(l2 - mx)
        tot = e0 + e1 + e2
        o_ref[sl, :] = ((e0 / tot) * ob_ref[0, sl, :] + (e1 / tot) * ob_ref[1, sl, :]
                        + (e2 / tot) * ob_ref[2, sl, :])
        return carry
    lax.fori_loop(0, seq // rc, merge, 0)


def _attention(proj, bsz, seq):
    n_tok = proj.shape[0]
    pairs = ATT_WIDTH // LANES
    assert seq % (ATT_BLOCK * max(DILATIONS)) == 0
    blk = (seq, LANES)
    return pl.pallas_call(
        functools.partial(_attn_kernel, seq=seq),
        out_shape=jax.ShapeDtypeStruct((n_tok, ATT_WIDTH), F32),
        grid=(bsz, pairs),
        in_specs=[pl.BlockSpec(blk, lambda b, h: (b, h)),
                  pl.BlockSpec(blk, lambda b, h: (b, pairs + h)),
                  pl.BlockSpec(blk, lambda b, h: (b, 2 * pairs + h))],
        out_specs=pl.BlockSpec(blk, lambda b, h: (b, h)),
        scratch_shapes=[pltpu.VMEM((len(DILATIONS), seq, LANES), F32),
                        pltpu.VMEM((len(DILATIONS), seq, LANES), F32),
                        pltpu.VMEM((ATT_BLOCK, 2 * ATT_BLOCK), F32),
                        pltpu.VMEM((ATT_BLOCK, ATT_BLOCK), F32)],
        compiler_params=_cparams(("parallel", "parallel"), 32),
        name="attn",
    )(proj, proj, proj)


def _s5_kernel(u_ref, bm_ref, lam_ref, cm_ref, dk_ref, o_ref, us_ref, st_ref, ys_ref, carry_ref, *, tc):
    bsz = u_ref.shape[0]
    half = st_ref.shape[1] // 2
    rows = tc * bsz
    mm_rows = 512

    @pl.when(pl.program_id(1) == 0)
    def _():
        carry_ref[...] = jnp.zeros_like(carry_ref)

    for b in range(bsz):
        us_ref[pl.ds(b, tc, stride=bsz), :] = u_ref[b]

    bm = bm_ref[0]
    for r0 in range(0, rows, mm_rows):
        st_ref[r0:r0 + mm_rows, :] = jnp.dot(us_ref[r0:r0 + mm_rows, :].astype(BF16), bm,
                                             preferred_element_type=F32)

    lam = lam_ref[0]
    lam_re = lam[:, :half]
    lam_im = lam[:, half:]

    def step(t, carry):
        xr, xi = carry
        sl = pl.ds(pl.multiple_of(t * bsz, bsz), bsz)
        nr = lam_re * xr - lam_im * xi + st_ref[sl, :half]
        ni = lam_re * xi + lam_im * xr + st_ref[sl, half:]
        st_ref[sl, :half] = nr
        st_ref[sl, half:] = ni
        return nr, ni

    xr, xi = lax.fori_loop(0, tc, step, (carry_ref[:, :half], carry_ref[:, half:]), unroll=4)
    carry_ref[:, :half] = xr
    carry_ref[:, half:] = xi

    cm = cm_ref[0]
    for r0 in range(0, rows, mm_rows):
        ys_ref[r0:r0 + mm_rows, :] = jnp.dot(st_ref[r0:r0 + mm_rows, :].astype(BF16), cm,
                                             preferred_element_type=F32)
    dk = dk_ref[...]
    for b in range(bsz):
        o_ref[b] = ys_ref[pl.ds(b, tc, stride=bsz), :] + dk * u_ref[b]


def _s5_params(lam_re, lam_im, log_step, b_re, b_im, c_re, c_im, bsz):
    groups = lam_re.shape[0]
    gpc = LANES // SSM_CH
    n_chunks = groups // gpc
    lam = lax.complex(lam_re.astype(F32), lam_im.astype(F32))
    step = jnp.exp(log_step.astype(F32))[:, None]
    lam_bar = jnp.exp(lam * step)
    bmat = lax.complex(b_re.astype(F32), b_im.astype(F32))
    b_bar = ((lam_bar - 1.0) / lam)[..., None] * bmat
    eye = jnp.eye(gpc, dtype=F32)

    def block_diag_in(t):
        t = t.reshape(n_chunks, gpc, SSM_STATE, SSM_CH)
        return jnp.einsum('ngpc,gh->ngchp', t, eye).reshape(n_chunks, gpc * SSM_CH, gpc * SSM_STATE)

    def block_diag_out(t):
        t = t.reshape(n_chunks, gpc, SSM_CH, SSM_STATE)
        return jnp.einsum('ngcp,gh->ngphc', t, eye).reshape(n_chunks, gpc * SSM_STATE, gpc * SSM_CH)

    bm = jnp.concatenate([block_diag_in(b_bar.real), block_diag_in(b_bar.imag)], axis=2).astype(BF16)
    cm = jnp.concatenate([block_diag_out(c_re.astype(F32)), block_diag_out(-c_im.astype(F32))],
                         axis=1).astype(BF16)
    lam_row = jnp.concatenate([lam_bar.real.reshape(n_chunks, gpc * SSM_STATE),
                               lam_bar.imag.reshape(n_chunks, gpc * SSM_STATE)], axis=1)
    lam_t = jnp.broadcast_to(lam_row[:, None, :], (n_chunks, bsz, 2 * gpc * SSM_STATE))
    return bm, lam_t, cm, n_chunks


def _s5(proj3, u_col0, lam_re, lam_im, log_step, b_re, b_im, c_re, c_im, d_skip):
    bsz, seq, _ = proj3.shape
    assert bsz == SUBLANES
    bm, lam_t, cm, n_chunks = _s5_params(lam_re, lam_im, log_step, b_re, b_im, c_re, c_im, bsz)
    width = n_chunks * LANES
    tc = 256
    st_cols = bm.shape[2]
    ublk0 = u_col0 // LANES
    return pl.pallas_call(
        functools.partial(_s5_kernel, tc=tc),
        out_shape=jax.ShapeDtypeStruct((bsz, seq, width), F32),
        grid=(n_chunks, seq // tc),
        in_specs=[pl.BlockSpec((bsz, tc, LANES), lambda c, t: (0, t, ublk0 + c)),
                  pl.BlockSpec((1, LANES, st_cols), lambda c, t: (c, 0, 0)),
                  pl.BlockSpec((1, bsz, st_cols), lambda c, t: (c, 0, 0)),
                  pl.BlockSpec((1, st_cols, LANES), lambda c, t: (c, 0, 0)),
                  pl.BlockSpec((1, LANES), lambda c, t: (0, c))],
        out_specs=pl.BlockSpec((bsz, tc, LANES), lambda c, t: (0, t, c)),
        scratch_shapes=[pltpu.VMEM((tc * bsz, LANES), F32),
                        pltpu.VMEM((tc * bsz, st_cols), F32),
                        pltpu.VMEM((tc * bsz, LANES), F32),
                        pltpu.VMEM((bsz, st_cols), F32)],
        compiler_params=_cparams(("arbitrary", "arbitrary"), 40),
        name="s5",
    )(proj3, bm, lam_t, cm, d_skip.reshape(1, width).astype(F32))


def _layer_norm(v, g, b):
    mu = jnp.mean(v, axis=-1, keepdims=True)
    var = jnp.mean(jnp.square(v - mu), axis=-1, keepdims=True)
    return (v - mu) * lax.rsqrt(var + LN_EPS) * g + b


def _rms_norm(v, g):
    return v * lax.rsqrt(jnp.mean(jnp.square(v), axis=-1, keepdims=True) + RMS_EPS) * g


def _mixout_kernel(att_ref, ssm_ref, x_ref, wglu_ref, bglu_ref, ag_ref, sg_ref, wout_ref, g_ref, b_ref,
                   o_ref, *, alpha):
    y = jax.nn.gelu(ssm_ref[...])
    z = jnp.dot(y.astype(BF16), wglu_ref[...], preferred_element_type=F32) + bglu_ref[...]
    o_ssm = y * jax.nn.sigmoid(z)
    a = _rms_norm(att_ref[...], ag_ref[...]).astype(BF16)
    s = _rms_norm(o_ssm, sg_ref[...]).astype(BF16)
    wa = att_ref.shape[1]
    mix = (jnp.dot(a, wout_ref[:wa, :], preferred_element_type=F32)
           + jnp.dot(s, wout_ref[wa:, :], preferred_element_type=F32))
    o_ref[...] = _layer_norm(alpha * x_ref[...] + mix, g_ref[...], b_ref[...])


def _mixout(o_att, y_ssm, x2d, w_glu, b_glu, att_g, ssm_g, w_out, ln_g, ln_b, alpha):
    n_tok, d = x2d.shape
    wa = o_att.shape[1]
    ws = y_ssm.shape[1]
    tm = 256
    row = lambda w: pl.BlockSpec((tm, w), lambda i: (i, 0))
    full = lambda a: pl.BlockSpec(a.shape, lambda i: (0,) * a.ndim)
    args = (o_att, y_ssm, x2d, w_glu.astype(BF16), b_glu.reshape(1, ws), att_g.reshape(1, wa),
            ssm_g.reshape(1, ws), w_out.astype(BF16), ln_g.reshape(1, d), ln_b.reshape(1, d))
    return pl.pallas_call(
        functools.partial(_mixout_kernel, alpha=alpha),
        out_shape=jax.ShapeDtypeStruct((n_tok, d), F32),
        grid=(n_tok // tm,),
        in_specs=[row(wa), row(ws), row(d)] + [full(a) for a in args[3:]],
        out_specs=row(d),
        compiler_params=_cparams(("parallel",), 32),
        name="mixout",
    )(*args)


def _split_bf16(v):
    hi = v.astype(BF16)
    lo = (v - hi.astype(F32)).astype(BF16)
    return hi, lo


def _router_kernel(h_ref, wt_ref, bias_ref, e_ref, g_ref, r_ref, cnt_ref, run_ref):
    tm = h_ref.shape[0]
    n_exp = wt_ref.shape[0]
    gsz = n_exp // N_EXPERT_GROUPS

    @pl.when(pl.program_id(0) == 0)
    def _():
        run_ref[...] = jnp.zeros_like(run_ref)

    w_hi, w_lo = _split_bf16(wt_ref[...])
    h_hi, h_lo = _split_bf16(h_ref[...])
    nt = (((1,), (1,)), ((), ()))
    logits = (lax.dot_general(w_hi, h_hi, nt, preferred_element_type=F32)
              + lax.dot_general(w_hi, h_lo, nt, preferred_element_type=F32)
              + lax.dot_general(w_lo, h_hi, nt, preferred_element_type=F32))
    scores = jax.nn.sigmoid(logits)
    choice = scores + bias_ref[:, 0:1]

    gio = lax.broadcasted_iota(jnp.int32, (gsz, tm), 0).astype(F32)
    gscore = []
    for g in range(N_EXPERT_GROUPS):
        cg = choice[g * gsz:(g + 1) * gsz, :]
        m1 = jnp.max(cg, axis=0, keepdims=True)
        i1 = jnp.min(jnp.where(cg == m1, gio, float(gsz)), axis=0, keepdims=True)
        m2 = jnp.max(jnp.where(gio == i1, NEG_INF, cg), axis=0, keepdims=True)
        gscore.append(m1 + m2)
    masked = []
    for g in range(N_EXPERT_GROUPS):
        beat = jnp.zeros((1, tm), F32)
        for o in range(N_EXPERT_GROUPS):
            if o == g:
                continue
            wins = (gscore[o] >= gscore[g]) if o < g else (gscore[o] > gscore[g])
            beat = beat + jnp.where(wins, 1.0, 0.0)
        keep = beat < float(TOPK_GROUPS)
        masked.append(jnp.where(keep, choice[g * gsz:(g + 1) * gsz, :], NEG_INF))
    cur = jnp.concatenate(masked, axis=0)

    eio = lax.broadcasted_iota(jnp.int32, (n_exp, tm), 0).astype(F32)
    idxs = []
    gates = []
    onehot = jnp.zeros((n_exp, tm), F32)
    for _ in range(TOP_K):
        m = jnp.max(cur, axis=0, keepdims=True)
        idx = jnp.min(jnp.where(cur == m, eio, float(n_exp)), axis=0, keepdims=True)
        hit = eio == idx
        idxs.append(idx)
        gates.append(jnp.sum(jnp.where(hit, scores, 0.0), axis=0, keepdims=True))
        cur = jnp.where(hit, NEG_INF, cur)
        onehot = onehot + jnp.where(hit, 1.0, 0.0)
    gate = jnp.concatenate(gates, axis=0)
    gate = ROUTED_SCALE * gate / (jnp.sum(gate, axis=0, keepdims=True) + 1e-20)

    si = lax.broadcasted_iota(jnp.int32, (tm, tm), 0)
    ti = lax.broadcasted_iota(jnp.int32, (tm, tm), 1)
    upper = jnp.where(si < ti, 1.0, 0.0).astype(BF16)
    before = jnp.dot(onehot.astype(BF16), upper, preferred_element_type=F32) + run_ref[:, 0:1]
    ranks = [jnp.sum(jnp.where(eio == idx, before, 0.0), axis=0, keepdims=True) for idx in idxs]

    e_ref[...] = jnp.concatenate(idxs, axis=0).astype(jnp.int32)
    g_ref[...] = gate
    r_ref[...] = jnp.concatenate(ranks, axis=0).astype(jnp.int32)
    run_ref[...] = run_ref[...] + jnp.sum(onehot, axis=1, keepdims=True)
    cnt_ref[...] = run_ref[...]


def _router(h, router_w, router_bias):
    n_tok, d = h.shape
    n_exp = router_w.shape[1]
    tm = 256
    wt = router_w.astype(F32).T
    bias = jnp.broadcast_to(router_bias.astype(F32)[:, None], (n_exp, LANES))
    tok = pl.BlockSpec((TOP_K, tm), lambda i: (0, i))
    return pl.pallas_call(
        _router_kernel,
        out_shape=(jax.ShapeDtypeStruct((TOP_K, n_tok), jnp.int32),
                   jax.ShapeDtypeStruct((TOP_K, n_tok), F32),
                   jax.ShapeDtypeStruct((TOP_K, n_tok), jnp.int32),
                   jax.ShapeDtypeStruct((n_exp, LANES), F32)),
        grid=(n_tok // tm,),
        in_specs=[pl.BlockSpec((tm, d), lambda i: (i, 0)),
                  pl.BlockSpec((n_exp, d), lambda i: (0, 0)),
                  pl.BlockSpec((n_exp, LANES), lambda i: (0, 0))],
        out_specs=(tok, tok, tok, pl.BlockSpec((n_exp, LANES), lambda i: (0, 0))),
        scratch_shapes=[pltpu.VMEM((n_exp, LANES), F32)],
        compiler_params=_cparams(("arbitrary",), 32),
        name="router",
    )(h, wt, bias)


def _dest_kernel(e_ref, r_ref, st_ref, d_ref):
    n_exp = st_ref.shape[0]
    tm = e_ref.shape[1]
    eio = lax.broadcasted_iota(jnp.int32, (n_exp, tm), 0)
    start = st_ref[:, 0:1]
    rows = [jnp.sum(jnp.where(eio == e_ref[k:k + 1, :], start, 0.0), axis=0, keepdims=True)
            for k in range(TOP_K)]
    d_ref[...] = jnp.concatenate(rows, axis=0).astype(jnp.int32) + r_ref[...]


def _dest(top_e, rank, starts):
    n_tok = top_e.shape[1]
    n_exp = starts.shape[0]
    tm = 512
    st = jnp.broadcast_to(starts.astype(F32)[:, None], (n_exp, LANES))
    tok = pl.BlockSpec((TOP_K, tm), lambda i: (0, i))
    return pl.pallas_call(
        _dest_kernel,
        out_shape=jax.ShapeDtypeStruct((TOP_K, n_tok), jnp.int32),
        grid=(n_tok // tm,),
        in_specs=[tok, tok, pl.BlockSpec((n_exp, LANES), lambda i: (0, 0))],
        out_specs=tok,
        compiler_params=_cparams(("parallel",), 32),
        name="dest",
    )(top_e, rank, st)


def _to_row_tiles(dst_ref, slot, val):
    rows = val.shape[0]
    for j in range(ROW_TILE):
        dst_ref[slot, pl.ds(j, rows, stride=ROW_TILE), :] = val[:, j * LANES:(j + 1) * LANES]


def _from_row_tiles(src_ref, idx, rows):
    return jnp.concatenate([src_ref[(*idx, pl.ds(j, rows, stride=ROW_TILE), slice(None))]
                            for j in range(ROW_TILE)], axis=1)


def _row_tile(r):
    return pl.ds(pl.multiple_of(r * ROW_TILE, ROW_TILE), ROW_TILE)


def _dispatch_kernel(pstart_ref, cnt_ref, dest_ref, h_ref, xs_ref, ht_ref, zero_ref, sem, zsem):
    tm = h_ref.shape[0]
    i = pl.program_id(0)
    n_steps = pl.num_programs(0)
    cur = i % 2
    _to_row_tiles(ht_ref, cur, h_ref[...])

    def issue(t, carry):
        for k in range(TOP_K):
            pltpu.make_async_copy(ht_ref.at[cur, _row_tile(t)], xs_ref.at[_row_tile(dest_ref[k, t])],
                                  sem.at[cur]).start(priority=k % 2)
        return carry
    lax.fori_loop(0, tm, issue, 0, unroll=2)

    n_exp = cnt_ref.shape[0]
    per_step = -(-n_exp // n_steps)
    zero_ref[...] = jnp.zeros_like(zero_ref)

    def pad_rows(e):
        cnt = cnt_ref[e]
        padded = (cnt + EXPERT_ROWS - 1) // EXPERT_ROWS * EXPERT_ROWS
        return pstart_ref[e] + cnt, pstart_ref[e] + padded

    for j in range(per_step):
        e = jnp.minimum(i * per_step + j, n_exp - 1)
        lo, hi = pad_rows(e)
        hi = jnp.where(i * per_step + j < n_exp, hi, lo)

        def fill(r, carry):
            pltpu.make_async_copy(zero_ref, xs_ref.at[_row_tile(r)], zsem).start()
            return carry
        lax.fori_loop(lo, hi, fill, 0)

        def fill_done(r, carry):
            pltpu.make_async_copy(zero_ref, xs_ref.at[_row_tile(0)], zsem).wait()
            return carry
        lax.fori_loop(lo, hi, fill_done, 0)

    def drain(slot):
        def body(t, carry):
            for k in range(TOP_K):
                pltpu.make_async_copy(ht_ref.at[slot, _row_tile(0)], xs_ref.at[_row_tile(0)],
                                      sem.at[slot]).wait()
            return carry
        lax.fori_loop(0, tm, body, 0)

    @pl.when(i > 0)
    def _():
        drain(1 - cur)

    @pl.when(i == n_steps - 1)
    def _():
        drain(cur)


def _dispatch(h, dest, pad_start, counts):
    n_tok, d = h.shape
    n_exp = counts.shape[0]
    assert d == ROW_TILE * LANES
    tm = 256
    n_rows = n_tok * TOP_K + n_exp * EXPERT_ROWS
    grid_spec = pltpu.PrefetchScalarGridSpec(
        num_scalar_prefetch=2,
        grid=(n_tok // tm,),
        in_specs=[pl.BlockSpec((TOP_K, tm), lambda i, *_: (0, i), memory_space=pltpu.SMEM),
                  pl.BlockSpec((tm, d), lambda i, *_: (i, 0))],
        out_specs=pl.BlockSpec(memory_space=pl.ANY),
        scratch_shapes=[pltpu.VMEM((2, tm * ROW_TILE, LANES), h.dtype),
                        pltpu.VMEM((ROW_TILE, LANES), h.dtype),
                        pltpu.SemaphoreType.DMA((2,)), pltpu.SemaphoreType.DMA(())],
    )
    return pl.pallas_call(
        _dispatch_kernel,
        out_shape=jax.ShapeDtypeStruct((n_rows * ROW_TILE, LANES), h.dtype),
        grid_spec=grid_spec,
        compiler_params=_cparams(("arbitrary",), 32),
        name="dispatch",
    )(pad_start, counts, dest, h)


def _experts_kernel(bstart_ref, bend_ref, nblk_ref, xs_ref, wgu_ref, wdn_ref, ys_ref,
                    xbuf, ybuf, act_ref, wgu_bf, wdn_bf, xsem, ysem):
    e = pl.program_id(0)
    n_blk = nblk_ref[0]
    trows = xbuf.shape[1]
    rows = trows // ROW_TILE
    ff = wdn_bf.shape[0]
    b0 = bstart_ref[e]
    b1 = bend_ref[e]

    def block_rows(b):
        return pl.ds(pl.multiple_of(b * trows, trows), trows)

    def x_copy(b):
        slot = b % EXPERT_RING
        return pltpu.make_async_copy(xs_ref.at[block_rows(b)], xbuf.at[slot], xsem.at[slot])

    def y_copy(b):
        slot = b % 2
        return pltpu.make_async_copy(ybuf.at[slot], ys_ref.at[block_rows(b)], ysem.at[slot])

    @pl.when(e == 0)
    def _():
        for i in range(EXPERT_RING):
            @pl.when(i < n_blk)
            def _():
                x_copy(i).start()

    wgu_bf[...] = wgu_ref[0].astype(BF16)
    wdn_bf[...] = wdn_ref[0].astype(BF16)

    def up(blocks):
        for b in blocks:
            x_copy(b).wait()
        for b in blocks:
            xb = _from_row_tiles(xbuf, (b % EXPERT_RING,), rows).astype(BF16)
            gu = jnp.dot(xb, wgu_bf[...], preferred_element_type=F32)
            act_ref[b - b0] = (jax.nn.silu(gu[:, :ff]) * gu[:, ff:]).astype(BF16)
        for b in blocks:
            @pl.when(b + EXPERT_RING < n_blk)
            def _():
                x_copy(b + EXPERT_RING).start()

    def up_pair(p, carry):
        up((b0 + 2 * p, b0 + 2 * p + 1))
        return carry
    n_mine = b1 - b0
    lax.fori_loop(0, n_mine // 2, up_pair, 0)

    @pl.when(n_mine % 2 == 1)
    def _():
        up((b1 - 1,))

    def down(blocks):
        for b in blocks:
            @pl.when(b >= 2)
            def _():
                y_copy(b - 2).wait()
        for b in blocks:
            y = jnp.dot(act_ref[b - b0], wdn_bf[...], preferred_element_type=F32)
            for j in range(ROW_TILE):
                ybuf[b % 2, pl.ds(j, rows, stride=ROW_TILE), :] = y[:, j * LANES:(j + 1) * LANES]
        for b in blocks:
            y_copy(b).start()
        last = blocks[-1]

        @pl.when(last == n_blk - 1)
        def _():
            @pl.when(last >= 1)
            def _():
                y_copy(last - 1).wait()
            y_copy(last).wait()

    def down_pair(p, carry):
        down((b0 + 2 * p, b0 + 2 * p + 1))
        return carry
    lax.fori_loop(0, n_mine // 2, down_pair, 0)

    @pl.when(n_mine % 2 == 1)
    def _():
        down((b1 - 1,))


def _expert_blocks(counts):
    blocks = (counts + EXPERT_ROWS - 1) // EXPERT_ROWS
    bend = jnp.cumsum(blocks)
    bstart = bend - blocks
    i32 = lambda a: a.astype(jnp.int32)
    return i32(bstart), i32(bend), i32(bend[-1]).reshape(1), i32(bstart * EXPERT_ROWS)


def _experts(xs, bstart, bend, n_blk, w_gu, w_down):
    n_exp, d, ff2 = w_gu.shape
    ff = w_down.shape[1]
    n_rows = xs.shape[0] // ROW_TILE
    assert n_rows % EXPERT_ROWS == 0 and d == ROW_TILE * LANES
    max_blocks = (n_rows - n_exp * EXPERT_ROWS) // TOP_K // EXPERT_ROWS + 1
    grid_spec = pltpu.PrefetchScalarGridSpec(
        num_scalar_prefetch=3,
        grid=(n_exp,),
        in_specs=[pl.BlockSpec(memory_space=pl.ANY),
                  pl.BlockSpec((1, d, ff2), lambda e, *_: (e, 0, 0)),
                  pl.BlockSpec((1, ff, d), lambda e, *_: (e, 0, 0))],
        out_specs=pl.BlockSpec(memory_space=pl.ANY),
        scratch_shapes=[pltpu.VMEM((EXPERT_RING, EXPERT_ROWS * ROW_TILE, LANES), F32),
                        pltpu.VMEM((2, EXPERT_ROWS * ROW_TILE, LANES), F32),
                        pltpu.VMEM((max_blocks, EXPERT_ROWS, ff), BF16),
                        pltpu.VMEM((d, ff2), BF16), pltpu.VMEM((ff, d), BF16),
                        pltpu.SemaphoreType.DMA((EXPERT_RING,)),
                        pltpu.SemaphoreType.DMA((2,))],
    )
    return pl.pallas_call(
        _experts_kernel,
        out_shape=jax.ShapeDtypeStruct(xs.shape, F32),
        grid_spec=grid_spec,
        compiler_params=_cparams(("arbitrary",), 40),
        name="experts",
    )(bstart, bend, n_blk, xs, w_gu, w_down)


def _combine_kernel(dest_ref, dnext_ref, gate_ref, h_ref, ys_ref, wgu_ref, wdn_ref, g_ref, b_ref, o_ref,
                    buf_ref, sem, *, alpha):
    tm = h_ref.shape[0]
    i = pl.program_id(0)
    cur = i % 2

    def gather(d_ref, slot):
        def issue(t, carry):
            for k in range(TOP_K):
                pltpu.make_async_copy(ys_ref.at[_row_tile(d_ref[k, t])], buf_ref.at[slot, k, _row_tile(t)],
                                      sem.at[slot]).start(priority=k % 2)
            return carry
        lax.fori_loop(0, tm, issue, 0, unroll=2)

    @pl.when(i == 0)
    def _():
        gather(dest_ref, 0)

    @pl.when(i + 1 < pl.num_programs(0))
    def _():
        gather(dnext_ref, 1 - cur)

    h = h_ref[...]
    ff = wdn_ref.shape[0]
    gu = jnp.dot(h.astype(BF16), wgu_ref[...], preferred_element_type=F32)
    act = (jax.nn.silu(gu[:, :ff]) * gu[:, ff:]).astype(BF16)
    acc = alpha * h + jnp.dot(act, wdn_ref[...], preferred_element_type=F32)

    def drain(t, carry):
        for k in range(TOP_K):
            pltpu.make_async_copy(ys_ref.at[_row_tile(0)], buf_ref.at[cur, 0, _row_tile(0)], sem.at[cur]).wait()
        return carry
    lax.fori_loop(0, tm, drain, 0)

    gate = gate_ref[...]
    parts = []
    for j in range(ROW_TILE):
        part = gate[:, 0:1] * buf_ref[cur, 0, pl.ds(j, tm, stride=ROW_TILE), :]
        for k in range(1, TOP_K):
            part = part + gate[:, k:k + 1] * buf_ref[cur, k, pl.ds(j, tm, stride=ROW_TILE), :]
        parts.append(part)
    acc = acc + jnp.concatenate(parts, axis=1)
    o_ref[...] = _layer_norm(acc, g_ref[...], b_ref[...])


def _combine(h, ys, dest, gate_t, shared_w_gu, shared_w_down, ln_g, ln_b, alpha):
    n_tok, d = h.shape
    tm = 256
    n_tiles = n_tok // tm
    full = lambda a: pl.BlockSpec(a.shape, lambda i: (0,) * a.ndim)
    wgu = shared_w_gu.astype(BF16)
    wdn = shared_w_down.astype(BF16)
    g2 = ln_g.reshape(1, d)
    b2 = ln_b.reshape(1, d)
    return pl.pallas_call(
        functools.partial(_combine_kernel, alpha=alpha),
        out_shape=jax.ShapeDtypeStruct((n_tok, d), F32),
        grid=(n_tiles,),
        in_specs=[pl.BlockSpec((TOP_K, tm), lambda i: (0, i), memory_space=pltpu.SMEM),
                  pl.BlockSpec((TOP_K, tm), lambda i: (0, jnp.minimum(i + 1, n_tiles - 1)),
                               memory_space=pltpu.SMEM),
                  pl.BlockSpec((tm, TOP_K), lambda i: (i, 0)),
                  pl.BlockSpec((tm, d), lambda i: (i, 0)),
                  pl.BlockSpec(memory_space=pl.ANY),
                  full(wgu), full(wdn), full(g2), full(b2)],
        out_specs=pl.BlockSpec((tm, d), lambda i: (i, 0)),
        scratch_shapes=[pltpu.VMEM((2, TOP_K, tm * ROW_TILE, LANES), F32), pltpu.SemaphoreType.DMA((2,))],
        compiler_params=_cparams(("arbitrary",), 48),
        name="combine",
    )(dest, dest, gate_t, h, ys, wgu, wdn, g2, b2)


def _moe(h, router_w, router_bias, w_gu, w_down, shared_w_gu, shared_w_down, ln_g, ln_b, alpha):
    top_e, gate, rank, cnt = _router(h, router_w, router_bias)
    counts = cnt[:, 0].astype(jnp.int32)
    bstart, bend, n_blk, pad_start = _expert_blocks(counts)
    dest = _dest(top_e, rank, pad_start)
    xs = _dispatch(h, dest, pad_start, counts)
    ys = _experts(xs, bstart, bend, n_blk, w_gu, w_down)
    return _combine(h, ys, dest, gate.T, shared_w_gu, shared_w_down, ln_g, ln_b, alpha)


def kernel(x, w_in, att_norm_g, lam_re, lam_im, log_step, b_re, b_im, c_re, c_im, d_skip, w_glu, b_glu,
           ssm_norm_g, w_out, ln1_g, ln1_b, router_w, router_bias, w_gu, w_down, shared_w_gu,
           shared_w_down, ln2_g, ln2_b):
    bsz, seq, d = x.shape
    depth = w_in.shape[0]
    alpha = (2 * depth) ** 0.25
    h = x.reshape(bsz * seq, d)
    for i in range(depth):
        proj = _inproj(h, w_in[i].astype(BF16), seq)
        o_att = _attention(proj, bsz, seq)
        y_ssm = _s5(proj.reshape(bsz, seq, -1), 3 * ATT_WIDTH, lam_re[i], lam_im[i], log_step[i],
                    b_re[i], b_im[i], c_re[i], c_im[i], d_skip[i])
        h = _mixout(o_att, y_ssm.reshape(bsz * seq, -1), h, w_glu[i], b_glu[i], att_norm_g[i],
                    ssm_norm_g[i], w_out[i], ln1_g[i], ln1_b[i], alpha)
        h = _moe(h, router_w[i], router_bias[i], w_gu[i], w_down[i], shared_w_gu[i], shared_w_down[i],
                 ln2_g[i], ln2_b[i], alpha)
    return h.reshape(bsz, seq, d)
```

```python
import functools
import math

import jax
import jax.numpy as jnp
from jax import lax
from jax.experimental import pallas as pl
from jax.experimental.pallas import tpu as pltpu

F32 = jnp.float32
BF16 = jnp.bfloat16

ATT_HEADS = 8
HEAD_DIM = 64
ATT_WIDTH = ATT_HEADS * HEAD_DIM
SSM_CH = 16
SSM_STATE = 64
ROPE_THETA = 500000.0
ROT_DIM = HEAD_DIM // 4
DILATIONS = (1, 4, 16)
ATT_BLOCK = 128
ATT_GROUP = 4
N_EXPERTS = 256
TOP_K = 8
N_EXPERT_GROUPS = 8
TOPK_GROUPS = 4
ROUTED_SCALE = 2.5
LN_EPS = 1e-5
RMS_EPS = 1e-6

LANES = 128
SUBLANES = 8
EXPERT_ROWS = 128
EXPERT_RING = 8
ROW_TILE = 8
NEG_INF = float("-inf")


def _cparams(sem, vmem_mb):
    return pltpu.CompilerParams(dimension_semantics=sem, vmem_limit_bytes=vmem_mb * 1024 * 1024)


def _inproj_kernel(x_ref, w_ref, cos_ref, sa_ref, sb_ref, o_ref, *, n_rot_cols):
    xb = x_ref[...].astype(BF16)
    cosf = cos_ref[...]
    sa = sa_ref[...]
    sb = sb_ref[...]
    width = o_ref.shape[1]
    chunk = 512
    for c in range(width // chunk):
        r = jnp.dot(xb, w_ref[:, c * chunk:(c + 1) * chunk], preferred_element_type=F32)
        if c * chunk < n_rot_cols:
            parts = []
            for s in range(chunk // LANES):
                t = r[:, s * LANES:(s + 1) * LANES]
                parts.append(t * cosf + pltpu.roll(t, LANES - ROT_DIM // 2, 1) * sa
                             + pltpu.roll(t, ROT_DIM // 2, 1) * sb)
            r = jnp.concatenate(parts, axis=1)
        o_ref[:, c * chunk:(c + 1) * chunk] = r


def _rope_lane_tables(seq):
    half = ROT_DIM // 2
    inv_freq = jnp.power(jnp.float32(ROPE_THETA), -jnp.arange(half, dtype=F32) / half)
    ang = jnp.arange(seq, dtype=F32)[:, None] * inv_freq[None, :]
    cos, sin = jnp.cos(ang), jnp.sin(ang)
    rest = HEAD_DIM - ROT_DIM
    cos_h = jnp.concatenate([cos, cos, jnp.ones((seq, rest), F32)], axis=1)
    sa_h = jnp.concatenate([-sin, jnp.zeros((seq, half + rest), F32)], axis=1)
    sb_h = jnp.concatenate([jnp.zeros((seq, half), F32), sin, jnp.zeros((seq, rest), F32)], axis=1)
    rep = LANES // HEAD_DIM
    return tuple(jnp.tile(t, (1, rep)) for t in (cos_h, sa_h, sb_h))


def _inproj(x2d, w_in_bf, seq):
    n_tok, d = x2d.shape
    width = w_in_bf.shape[1]
    tm = 512
    cosf, sa, sb = _rope_lane_tables(seq)
    tab_spec = pl.BlockSpec((tm, LANES), lambda i: (i % (seq // tm), 0))
    return pl.pallas_call(
        functools.partial(_inproj_kernel, n_rot_cols=2 * ATT_WIDTH),
        out_shape=jax.ShapeDtypeStruct((n_tok, width), F32),
        grid=(n_tok // tm,),
        in_specs=[pl.BlockSpec((tm, d), lambda i: (i, 0)),
                  pl.BlockSpec((d, width), lambda i: (0, 0)),
                  tab_spec, tab_spec, tab_spec],
        out_specs=pl.BlockSpec((tm, width), lambda i: (i, 0)),
        compiler_params=_cparams(("parallel",), 48),
        name="inproj",
    )(x2d, w_in_bf, cosf, sa, sb)


def _attn_kernel(q_ref, k_ref, v_ref, o_ref, ob_ref, lb_ref, band_ref, causal_ref, *, seq):
    blk = ATT_BLOCK
    lane = lax.broadcasted_iota(jnp.int32, (1, LANES), 1)
    head0 = lane < HEAD_DIM
    scale = HEAD_DIM ** -0.5

    qi = lax.broadcasted_iota(jnp.int32, (blk, 2 * blk), 0)
    kj = lax.broadcasted_iota(jnp.int32, (blk, 2 * blk), 1)
    dist = qi + blk - kj
    band_ref[...] = jnp.where((dist >= 0) & (dist <= blk), 0.0, NEG_INF)
    causal_ref[...] = jnp.where(lax.broadcasted_iota(jnp.int32, (blk, blk), 0)
                                >= lax.broadcasted_iota(jnp.int32, (blk, blk), 1), 0.0, NEG_INF)

    def rows(start, d):
        if d == 1:
            return pl.ds(start, blk)
        return pl.ds(start, blk, stride=d)

    def one_block(c, d, start_q, with_prev):
        sl_q = rows(start_q, d)
        q = q_ref[sl_q, :] * scale
        kk = k_ref[sl_q, :]
        vv = v_ref[sl_q, :]
        if with_prev:
            sl_p = rows(start_q - d * blk, d)
            kk = jnp.concatenate([k_ref[sl_p, :], kk], axis=0)
            vv = jnp.concatenate([v_ref[sl_p, :], vv], axis=0)
            bias_ref = band_ref
        else:
            bias_ref = causal_ref
        kk = kk.astype(BF16)
        vv = vv.astype(BF16)
        outs = []
        lses = []
        for h in range(LANES // HEAD_DIM):
            hm = head0 if h == 0 else jnp.logical_not(head0)
            qh = jnp.where(hm, q, 0.0).astype(BF16)
            s = lax.dot_general(qh, kk, (((1,), (1,)), ((), ())), preferred_element_type=F32)
            s = s + bias_ref[...]
            m = jnp.max(s, axis=-1, keepdims=True)
            p = jnp.exp(s - m)
            den = jnp.sum(p, axis=-1, keepdims=True)
            outs.append(jnp.dot(p.astype(BF16), vv, preferred_element_type=F32) / den)
            lses.append(m + jnp.log(den))
        ob_ref[c, sl_q, :] = jnp.where(head0, outs[0], outs[1])
        lb_ref[c, sl_q, :] = jnp.where(head0, lses[0], lses[1])

    def run_blocks(n_blocks, group, fn):
        assert n_blocks % group == 0
        if n_blocks == group:
            for g in range(group):
                fn(g)
            return
        def body(it, carry):
            for g in range(group):
                fn(it * group + g)
            return carry
        lax.fori_loop(0, n_blocks // group, body, 0)

    for c, d in enumerate(DILATIONS):
        nb = seq // d // blk
        run_blocks(d, min(d, ATT_GROUP), lambda r, c=c, d=d: one_block(c, d, r, False))
        rest = (nb - 1) * d
        if rest:
            def later(idx, c=c, d=d):
                n = idx // d
                r = idx - n * d
                one_block(c, d, r + (n + 1) * (d * blk), True)
            group = max(g for g in range(1, ATT_GROUP + 1) if rest % g == 0)
            run_blocks(rest, group, later)

    rc = 256
    def merge(i, carry):
        sl = pl.ds(pl.multiple_of(i * rc, rc), rc)
        l0 = lb_ref[0, sl, :]
        l1 = lb_ref[1, sl, :]
        l2 = lb_ref[2, sl, :]
        mx = jnp.maximum(jnp.maximum(l0, l1), l2)
        e0 = jnp.exp(l0 - mx)
        e1 = jnp.exp(l1 - mx)
        e2 = jnp.exp(l2 - mx)
        tot = e0 + e1 + e2
        o_ref[sl, :] = ((e0 / tot) * ob_ref[0, sl, :] + (e1 / tot) * ob_ref[1, sl, :]
                        + (e2 / tot) * ob_ref[2, sl, :])
        return carry
    lax.fori_loop(0, seq // rc, merge, 0)


def _attention(proj, bsz, seq):
    n_tok = proj.shape[0]
    pairs = ATT_WIDTH // LANES
    assert seq % (ATT_BLOCK * max(DILATIONS)) == 0
    blk = (seq, LANES)
    return pl.pallas_call(
        functools.partial(_attn_kernel, seq=seq),
        out_shape=jax.ShapeDtypeStruct((n_tok, ATT_WIDTH), F32),
        grid=(bsz, pairs),
        in_specs=[pl.BlockSpec(blk, lambda b, h: (b, h)),
                  pl.BlockSpec(blk, lambda b, h: (b, pairs + h)),
                  pl.BlockSpec(blk, lambda b, h: (b, 2 * pairs + h))],
        out_specs=pl.BlockSpec(blk, lambda b, h: (b, h)),
        scratch_shapes=[pltpu.VMEM((len(DILATIONS), seq, LANES), F32),
                        pltpu.VMEM((len(DILATIONS), seq, LANES), F32),
                        pltpu.VMEM((ATT_BLOCK, 2 * ATT_BLOCK), F32),
                        pltpu.VMEM((ATT_BLOCK, ATT_BLOCK), F32)],
        compiler_params=_cparams(("parallel", "parallel"), 32),
        name="attn",
    )(proj, proj, proj)


def _s5_kernel(u_ref, bm_ref, lam_ref, cm_ref, dk_ref, o_ref, us_ref, st_ref, ys_ref, carry_ref, *, tc):
    bsz = u_ref.shape[0]
    half = st_ref.shape[1] // 2
    rows = tc * bsz
    mm_rows = 512

    @pl.when(pl.program_id(1) == 0)
    def _():
        carry_ref[...] = jnp.zeros_like(carry_ref)

    for b in range(bsz):
        us_ref[pl.ds(b, tc, stride=bsz), :] = u_ref[b]

    bm = bm_ref[0]
    for r0 in range(0, rows, mm_rows):
        st_ref[r0:r0 + mm_rows, :] = jnp.dot(us_ref[r0:r0 + mm_rows, :].astype(BF16), bm,
                                             preferred_element_type=F32)

    lam = lam_ref[0]
    lam_re = lam[:, :half]
    lam_im = lam[:, half:]

    def step(t, carry):
        xr, xi = carry
        sl = pl.ds(pl.multiple_of(t * bsz, bsz), bsz)
        nr = lam_re * xr - lam_im * xi + st_ref[sl, :half]
        ni = lam_re * xi + lam_im * xr + st_ref[sl, half:]
        st_ref[sl, :half] = nr
        st_ref[sl, half:] = ni
        return nr, ni

    xr, xi = lax.fori_loop(0, tc, step, (carry_ref[:, :half], carry_ref[:, half:]), unroll=4)
    carry_ref[:, :half] = xr
    carry_ref[:, half:] = xi

    cm = cm_ref[0]
    for r0 in range(0, rows, mm_rows):
        ys_ref[r0:r0 + mm_rows, :] = jnp.dot(st_ref[r0:r0 + mm_rows, :].astype(BF16), cm,
                                             preferred_element_type=F32)
    dk = dk_ref[...]
    for b in range(bsz):
        o_ref[b] = ys_ref[pl.ds(b, tc, stride=bsz), :] + dk * u_ref[b]


def _s5_params(lam_re, lam_im, log_step, b_re, b_im, c_re, c_im, bsz):
    groups = lam_re.shape[0]
    gpc = LANES // SSM_CH
    n_chunks = groups // gpc
    lam = lax.complex(lam_re.astype(F32), lam_im.astype(F32))
    step = jnp.exp(log_step.astype(F32))[:, None]
    lam_bar = jnp.exp(lam * step)
    bmat = lax.complex(b_re.astype(F32), b_im.astype(F32))
    b_bar = ((lam_bar - 1.0) / lam)[..., None] * bmat
    eye = jnp.eye(gpc, dtype=F32)

    def block_diag_in(t):
        t = t.reshape(n_chunks, gpc, SSM_STATE, SSM_CH)
        return jnp.einsum('ngpc,gh->ngchp', t, eye).reshape(n_chunks, gpc * SSM_CH, gpc * SSM_STATE)

    def block_diag_out(t):
        t = t.reshape(n_chunks, gpc, SSM_CH, SSM_STATE)
        return jnp.einsum('ngcp,gh->ngphc', t, eye).reshape(n_chunks, gpc * SSM_STATE, gpc * SSM_CH)

    bm = jnp.concatenate([block_diag_in(b_bar.real), block_diag_in(b_bar.imag)], axis=2).astype(BF16)
    cm = jnp.concatenate([block_diag_out(c_re.astype(F32)), block_diag_out(-c_im.astype(F32))],
                         axis=1).astype(BF16)
    lam_row = jnp.concatenate([lam_bar.real.reshape(n_chunks, gpc * SSM_STATE),
                               lam_bar.imag.reshape(n_chunks, gpc * SSM_STATE)], axis=1)
    lam_t = jnp.broadcast_to(lam_row[:, None, :], (n_chunks, bsz, 2 * gpc * SSM_STATE))
    return bm, lam_t, cm, n_chunks


def _s5(proj3, u_col0, lam_re, lam_im, log_step, b_re, b_im, c_re, c_im, d_skip):
    bsz, seq, _ = proj3.shape
    assert bsz == SUBLANES
    bm, lam_t, cm, n_chunks = _s5_params(lam_re, lam_im, log_step, b_re, b_im, c_re, c_im, bsz)
    width = n_chunks * LANES
    tc = 256
    st_cols = bm.shape[2]
    ublk0 = u_col0 // LANES
    return pl.pallas_call(
        functools.partial(_s5_kernel, tc=tc),
        out_shape=jax.ShapeDtypeStruct((bsz, seq, width), F32),
        grid=(n_chunks, seq // tc),
        in_specs=[pl.BlockSpec((bsz, tc, LANES), lambda c, t: (0, t, ublk0 + c)),
                  pl.BlockSpec((1, LANES, st_cols), lambda c, t: (c, 0, 0)),
                  pl.BlockSpec((1, bsz, st_cols), lambda c, t: (c, 0, 0)),
                  pl.BlockSpec((1, st_cols, LANES), lambda c, t: (c, 0, 0)),
                  pl.BlockSpec((1, LANES), lambda c, t: (0, c))],
        out_specs=pl.BlockSpec((bsz, tc, LANES), lambda c, t: (0, t, c)),
        scratch_shapes=[pltpu.VMEM((tc * bsz, LANES), F32),
                        pltpu.VMEM((tc * bsz, st_cols), F32),
                        pltpu.VMEM((tc * bsz, LANES), F32),
                        pltpu.VMEM((bsz, st_cols), F32)],
        compiler_params=_cparams(("arbitrary", "arbitrary"), 40),
        name="s5",
    )(proj3, bm, lam_t, cm, d_skip.reshape(1, width).astype(F32))


def _layer_norm(v, g, b):
    mu = jnp.mean(v, axis=-1, keepdims=True)
    var = jnp.mean(jnp.square(v - mu), axis=-1, keepdims=True)
    return (v - mu) * lax.rsqrt(var + LN_EPS) * g + b


def _rms_norm(v, g):
    return v * lax.rsqrt(jnp.mean(jnp.square(v), axis=-1, keepdims=True) + RMS_EPS) * g


def _mixout_kernel(att_ref, ssm_ref, x_ref, wglu_ref, bglu_ref, ag_ref, sg_ref, wout_ref, g_ref, b_ref,
                   o_ref, *, alpha):
    y = jax.nn.gelu(ssm_ref[...])
    z = jnp.dot(y.astype(BF16), wglu_ref[...], preferred_element_type=F32) + bglu_ref[...]
    o_ssm = y * jax.nn.sigmoid(z)
    a = _rms_norm(att_ref[...], ag_ref[...]).astype(BF16)
    s = _rms_norm(o_ssm, sg_ref[...]).astype(BF16)
    wa = att_ref.shape[1]
    mix = (jnp.dot(a, wout_ref[:wa, :], preferred_element_type=F32)
           + jnp.dot(s, wout_ref[wa:, :], preferred_element_type=F32))
    o_ref[...] = _layer_norm(alpha * x_ref[...] + mix, g_ref[...], b_ref[...])


def _mixout(o_att, y_ssm, x2d, w_glu, b_glu, att_g, ssm_g, w_out, ln_g, ln_b, alpha):
    n_tok, d = x2d.shape
    wa = o_att.shape[1]
    ws = y_ssm.shape[1]
    tm = 256
    row = lambda w: pl.BlockSpec((tm, w), lambda i: (i, 0))
    full = lambda a: pl.BlockSpec(a.shape, lambda i: (0,) * a.ndim)
    args = (o_att, y_ssm, x2d, w_glu.astype(BF16), b_glu.reshape(1, ws), att_g.reshape(1, wa),
            ssm_g.reshape(1, ws), w_out.astype(BF16), ln_g.reshape(1, d), ln_b.reshape(1, d))
    return pl.pallas_call(
        functools.partial(_mixout_kernel, alpha=alpha),
        out_shape=jax.ShapeDtypeStruct((n_tok, d), F32),
        grid=(n_tok // tm,),
        in_specs=[row(wa), row(ws), row(d)] + [full(a) for a in args[3:]],
        out_specs=row(d),
        compiler_params=_cparams(("parallel",), 32),
        name="mixout",
    )(*args)


def _split_bf16(v):
    hi = v.astype(BF16)
    lo = (v - hi.astype(F32)).astype(BF16)
    return hi, lo


def _router_kernel(h_ref, wt_ref, bias_ref, e_ref, g_ref, r_ref, cnt_ref, run_ref):
    tm = h_ref.shape[0]
    n_exp = wt_ref.shape[0]
    gsz = n_exp // N_EXPERT_GROUPS

    @pl.when(pl.program_id(0) == 0)
    def _():
        run_ref[...] = jnp.zeros_like(run_ref)

    w_hi, w_lo = _split_bf16(wt_ref[...])
    h_hi, h_lo = _split_bf16(h_ref[...])
    nt = (((1,), (1,)), ((), ()))
    logits = (lax.dot_general(w_hi, h_hi, nt, preferred_element_type=F32)
              + lax.dot_general(w_hi, h_lo, nt, preferred_element_type=F32)
              + lax.dot_general(w_lo, h_hi, nt, preferred_element_type=F32))
    scores = jax.nn.sigmoid(logits)
    choice = scores + bias_ref[:, 0:1]

    gio = lax.broadcasted_iota(jnp.int32, (gsz, tm), 0).astype(F32)
    gscore = []
    for g in range(N_EXPERT_GROUPS):
        cg = choice[g * gsz:(g + 1) * gsz, :]
        m1 = jnp.max(cg, axis=0, keepdims=True)
        i1 = jnp.min(jnp.where(cg == m1, gio, float(gsz)), axis=0, keepdims=True)
        m2 = jnp.max(jnp.where(gio == i1, NEG_INF, cg), axis=0, keepdims=True)
        gscore.append(m1 + m2)
    masked = []
    for g in range(N_EXPERT_GROUPS):
        beat = jnp.zeros((1, tm), F32)
        for o in range(N_EXPERT_GROUPS):
            if o == g:
                continue
            wins = (gscore[o] >= gscore[g]) if o < g else (gscore[o] > gscore[g])
            beat = beat + jnp.where(wins, 1.0, 0.0)
        keep = beat < float(TOPK_GROUPS)
        masked.append(jnp.where(keep, choice[g * gsz:(g + 1) * gsz, :], NEG_INF))
    cur = jnp.concatenate(masked, axis=0)

    eio = lax.broadcasted_iota(jnp.int32, (n_exp, tm), 0).astype(F32)
    idxs = []
    gates = []
    onehot = jnp.zeros((n_exp, tm), F32)
    for _ in range(TOP_K):
        m = jnp.max(cur, axis=0, keepdims=True)
        idx = jnp.min(jnp.where(cur == m, eio, float(n_exp)), axis=0, keepdims=True)
        hit = eio == idx
        idxs.append(idx)
        gates.append(jnp.sum(jnp.where(hit, scores, 0.0), axis=0, keepdims=True))
        cur = jnp.where(hit, NEG_INF, cur)
        onehot = onehot + jnp.where(hit, 1.0, 0.0)
    gate = jnp.concatenate(gates, axis=0)
    gate = ROUTED_SCALE * gate / (jnp.sum(gate, axis=0, keepdims=True) + 1e-20)

    si = lax.broadcasted_iota(jnp.int32, (tm, tm), 0)
    ti = lax.broadcasted_iota(jnp.int32, (tm, tm), 1)
    upper = jnp.where(si < ti, 1.0, 0.0).astype(BF16)
    before = jnp.dot(onehot.astype(BF16), upper, preferred_element_type=F32) + run_ref[:, 0:1]
    ranks = [jnp.sum(jnp.where(eio == idx, before, 0.0), axis=0, keepdims=True) for idx in idxs]

    e_ref[...] = jnp.concatenate(idxs, axis=0).astype(jnp.int32)
    g_ref[...] = gate
    r_ref[...] = jnp.concatenate(ranks, axis=0).astype(jnp.int32)
    run_ref[...] = run_ref[...] + jnp.sum(onehot, axis=1, keepdims=True)
    cnt_ref[...] = run_ref[...]


def _router(h, router_w, router_bias):
    n_tok, d = h.shape
    n_exp = router_w.shape[1]
    tm = 256
    wt = router_w.astype(F32).T
    bias = jnp.broadcast_to(router_bias.astype(F32)[:, None], (n_exp, LANES))
    tok = pl.BlockSpec((TOP_K, tm), lambda i: (0, i))
    return pl.pallas_call(
        _router_kernel,
        out_shape=(jax.ShapeDtypeStruct((TOP_K, n_tok), jnp.int32),
                   jax.ShapeDtypeStruct((TOP_K, n_tok), F32),
                   jax.ShapeDtypeStruct((TOP_K, n_tok), jnp.int32),
                   jax.ShapeDtypeStruct((n_exp, LANES), F32)),
        grid=(n_tok // tm,),
        in_specs=[pl.BlockSpec((tm, d), lambda i: (i, 0)),
                  pl.BlockSpec((n_exp, d), lambda i: (0, 0)),
                  pl.BlockSpec((n_exp, LANES), lambda i: (0, 0))],
        out_specs=(tok, tok, tok, pl.BlockSpec((n_exp, LANES), lambda i: (0, 0))),
        scratch_shapes=[pltpu.VMEM((n_exp, LANES), F32)],
        compiler_params=_cparams(("arbitrary",), 32),
        name="router",
    )(h, wt, bias)


def _dest_kernel(e_ref, r_ref, st_ref, d_ref):
    n_exp = st_ref.shape[0]
    tm = e_ref.shape[1]
    eio = lax.broadcasted_iota(jnp.int32, (n_exp, tm), 0)
    start = st_ref[:, 0:1]
    rows = [jnp.sum(jnp.where(eio == e_ref[k:k + 1, :], start, 0.0), axis=0, keepdims=True)
            for k in range(TOP_K)]
    d_ref[...] = jnp.concatenate(rows, axis=0).astype(jnp.int32) + r_ref[...]


def _dest(top_e, rank, starts):
    n_tok = top_e.shape[1]
    n_exp = starts.shape[0]
    tm = 512
    st = jnp.broadcast_to(starts.astype(F32)[:, None], (n_exp, LANES))
    tok = pl.BlockSpec((TOP_K, tm), lambda i: (0, i))
    return pl.pallas_call(
        _dest_kernel,
        out_shape=jax.ShapeDtypeStruct((TOP_K, n_tok), jnp.int32),
        grid=(n_tok // tm,),
        in_specs=[tok, tok, pl.BlockSpec((n_exp, LANES), lambda i: (0, 0))],
        out_specs=tok,
        compiler_params=_cparams(("parallel",), 32),
        name="dest",
    )(top_e, rank, st)


def _to_row_tiles(dst_ref, slot, val):
    rows = val.shape[0]
    for j in range(ROW_TILE):
        dst_ref[slot, pl.ds(j, rows, stride=ROW_TILE), :] = val[:, j * LANES:(j + 1) * LANES]


def _from_row_tiles(src_ref, idx, rows):
    return jnp.concatenate([src_ref[(*idx, pl.ds(j, rows, stride=ROW_TILE), slice(None))]
                            for j in range(ROW_TILE)], axis=1)


def _row_tile(r):
    return pl.ds(pl.multiple_of(r * ROW_TILE, ROW_TILE), ROW_TILE)


def _dispatch_kernel(pstart_ref, cnt_ref, dest_ref, h_ref, xs_ref, ht_ref, zero_ref, sem, zsem):
    tm = h_ref.shape[0]
    i = pl.program_id(0)
    n_steps = pl.num_programs(0)
    cur = i % 2
    _to_row_tiles(ht_ref, cur, h_ref[...])

    def issue(t, carry):
        for k in range(TOP_K):
            pltpu.make_async_copy(ht_ref.at[cur, _row_tile(t)], xs_ref.at[_row_tile(dest_ref[k, t])],
                                  sem.at[cur]).start(priority=k % 2)
        return carry
    lax.fori_loop(0, tm, issue, 0, unroll=2)

    n_exp = cnt_ref.shape[0]
    per_step = -(-n_exp // n_steps)
    zero_ref[...] = jnp.zeros_like(zero_ref)

    def pad_rows(e):
        cnt = cnt_ref[e]
        padded = (cnt + EXPERT_ROWS - 1) // EXPERT_ROWS * EXPERT_ROWS
        return pstart_ref[e] + cnt, pstart_ref[e] + padded

    def fill(r, carry):
        pltpu.make_async_copy(zero_ref, xs_ref.at[_row_tile(r)], zsem).start()
        return carry

    def fill_done(r, carry):
        pltpu.make_async_copy(zero_ref, xs_ref.at[_row_tile(0)], zsem).wait()
        return carry

    spans = []
    for j in range(per_step):
        e = jnp.minimum(i * per_step + j, n_exp - 1)
        lo, hi = pad_rows(e)
        spans.append((lo, jnp.where(i * per_step + j < n_exp, hi, lo)))
    for lo, hi in spans:
        lax.fori_loop(lo, hi, fill, 0)

    def drain(slot):
        def body(t, carry):
            for k in range(TOP_K):
                pltpu.make_async_copy(ht_ref.at[slot, _row_tile(0)], xs_ref.at[_row_tile(0)],
                                      sem.at[slot]).wait()
            return carry
        lax.fori_loop(0, tm, body, 0)

    @pl.when(i > 0)
    def _():
        drain(1 - cur)

    @pl.when(i == n_steps - 1)
    def _():
        drain(cur)

    for lo, hi in spans:
        lax.fori_loop(lo, hi, fill_done, 0)


def _dispatch(h, dest, pad_start, counts):
    n_tok, d = h.shape
    n_exp = counts.shape[0]
    assert d == ROW_TILE * LANES
    tm = 256
    n_rows = n_tok * TOP_K + n_exp * EXPERT_ROWS
    grid_spec = pltpu.PrefetchScalarGridSpec(
        num_scalar_prefetch=2,
        grid=(n_tok // tm,),
        in_specs=[pl.BlockSpec((TOP_K, tm), lambda i, *_: (0, i), memory_space=pltpu.SMEM),
                  pl.BlockSpec((tm, d), lambda i, *_: (i, 0))],
        out_specs=pl.BlockSpec(memory_space=pl.ANY),
        scratch_shapes=[pltpu.VMEM((2, tm * ROW_TILE, LANES), h.dtype),
                        pltpu.VMEM((ROW_TILE, LANES), h.dtype),
                        pltpu.SemaphoreType.DMA((2,)), pltpu.SemaphoreType.DMA(())],
    )
    return pl.pallas_call(
        _dispatch_kernel,
        out_shape=jax.ShapeDtypeStruct((n_rows * ROW_TILE, LANES), h.dtype),
        grid_spec=grid_spec,
        compiler_params=_cparams(("arbitrary",), 32),
        name="dispatch",
    )(pad_start, counts, dest, h)


def _experts_kernel(bstart_ref, bend_ref, nblk_ref, xs_ref, wgu_ref, wdn_ref, ys_ref,
                    xbuf, ybuf, act_ref, wgu_bf, wdn_bf, xsem, ysem):
    e = pl.program_id(0)
    n_blk = nblk_ref[0]
    trows = xbuf.shape[1]
    rows = trows // ROW_TILE
    ff = wdn_bf.shape[0]
    b0 = bstart_ref[e]
    b1 = bend_ref[e]

    def block_rows(b):
        return pl.ds(pl.multiple_of(b * trows, trows), trows)

    def x_copy(b):
        slot = b % EXPERT_RING
        return pltpu.make_async_copy(xs_ref.at[block_rows(b)], xbuf.at[slot], xsem.at[slot])

    def y_copy(b):
        slot = b % 2
        return pltpu.make_async_copy(ybuf.at[slot], ys_ref.at[block_rows(b)], ysem.at[slot])

    @pl.when(e == 0)
    def _():
        for i in range(EXPERT_RING):
            @pl.when(i < n_blk)
            def _():
                x_copy(i).start(priority=1)

    wgu_bf[...] = wgu_ref[0].astype(BF16)
    wdn_bf[...] = wdn_ref[0].astype(BF16)

    def up(blocks):
        for b in blocks:
            x_copy(b).wait()
        for b in blocks:
            xb = _from_row_tiles(xbuf, (b % EXPERT_RING,), rows).astype(BF16)
            gu = jnp.dot(xb, wgu_bf[...], preferred_element_type=F32)
            act_ref[b - b0] = (jax.nn.silu(gu[:, :ff]) * gu[:, ff:]).astype(BF16)
        for b in blocks:
            @pl.when(b + EXPERT_RING < n_blk)
            def _():
                x_copy(b + EXPERT_RING).start(priority=1)

    def up_pair(p, carry):
        up((b0 + 2 * p, b0 + 2 * p + 1))
        return carry
    n_mine = b1 - b0
    lax.fori_loop(0, n_mine // 2, up_pair, 0)

    @pl.when(n_mine % 2 == 1)
    def _():
        up((b1 - 1,))

    def down(blocks):
        for b in blocks:
            @pl.when(b >= 2)
            def _():
                y_copy(b - 2).wait()
        for b in blocks:
            y = jnp.dot(act_ref[b - b0], wdn_bf[...], preferred_element_type=F32)
            for j in range(ROW_TILE):
                ybuf[b % 2, pl.ds(j, rows, stride=ROW_TILE), :] = y[:, j * LANES:(j + 1) * LANES]
        for b in blocks:
            y_copy(b).start(priority=1)
        last = blocks[-1]

        @pl.when(last == n_blk - 1)
        def _():
            @pl.when(last >= 1)
            def _():
                y_copy(last - 1).wait()
            y_copy(last).wait()

    def down_pair(p, carry):
        down((b0 + 2 * p, b0 + 2 * p + 1))
        return carry
    lax.fori_loop(0, n_mine // 2, down_pair, 0)

    @pl.when(n_mine % 2 == 1)
    def _():
        down((b1 - 1,))


def _expert_blocks(counts):
    blocks = (counts + EXPERT_ROWS - 1) // EXPERT_ROWS
    bend = jnp.cumsum(blocks)
    bstart = bend - blocks
    i32 = lambda a: a.astype(jnp.int32)
    return i32(bstart), i32(bend), i32(bend[-1]).reshape(1), i32(bstart * EXPERT_ROWS)


def _experts(xs, bstart, bend, n_blk, w_gu, w_down):
    n_exp, d, ff2 = w_gu.shape
    ff = w_down.shape[1]
    n_rows = xs.shape[0] // ROW_TILE
    assert n_rows % EXPERT_ROWS == 0 and d == ROW_TILE * LANES
    max_blocks = (n_rows - n_exp * EXPERT_ROWS) // TOP_K // EXPERT_ROWS + 1
    grid_spec = pltpu.PrefetchScalarGridSpec(
        num_scalar_prefetch=3,
        grid=(n_exp,),
        in_specs=[pl.BlockSpec(memory_space=pl.ANY),
                  pl.BlockSpec((1, d, ff2), lambda e, *_: (e, 0, 0)),
                  pl.BlockSpec((1, ff, d), lambda e, *_: (e, 0, 0))],
        out_specs=pl.BlockSpec(memory_space=pl.ANY),
        scratch_shapes=[pltpu.VMEM((EXPERT_RING, EXPERT_ROWS * ROW_TILE, LANES), F32),
                        pltpu.VMEM((2, EXPERT_ROWS * ROW_TILE, LANES), F32),
                        pltpu.VMEM((max_blocks, EXPERT_ROWS, ff), BF16),
                        pltpu.VMEM((d, ff2), BF16), pltpu.VMEM((ff, d), BF16),
                        pltpu.SemaphoreType.DMA((EXPERT_RING,)),
                        pltpu.SemaphoreType.DMA((2,))],
    )
    return pl.pallas_call(
        _experts_kernel,
        out_shape=jax.ShapeDtypeStruct(xs.shape, F32),
        grid_spec=grid_spec,
        compiler_params=_cparams(("arbitrary",), 40),
        name="experts",
    )(bstart, bend, n_blk, xs, w_gu, w_down)


def _combine_kernel(dest_ref, dnext_ref, gate_ref, h_ref, ys_ref, wgu_ref, wdn_ref, g_ref, b_ref, o_ref,
                    buf_ref, sem, *, alpha):
    tm = h_ref.shape[0]
    i = pl.program_id(0)
    cur = i % 2

    def gather(d_ref, slot):
        def issue(t, carry):
            for k in range(TOP_K):
                pltpu.make_async_copy(ys_ref.at[_row_tile(d_ref[k, t])], buf_ref.at[slot, k, _row_tile(t)],
                                      sem.at[slot]).start(priority=k % 2)
            return carry
        lax.fori_loop(0, tm, issue, 0, unroll=2)

    @pl.when(i == 0)
    def _():
        gather(dest_ref, 0)

    @pl.when(i + 1 < pl.num_programs(0))
    def _():
        gather(dnext_ref, 1 - cur)

    h = h_ref[...]
    ff = wdn_ref.shape[0]
    gu = jnp.dot(h.astype(BF16), wgu_ref[...], preferred_element_type=F32)
    act = (jax.nn.silu(gu[:, :ff]) * gu[:, ff:]).astype(BF16)
    acc = alpha * h + jnp.dot(act, wdn_ref[...], preferred_element_type=F32)

    def drain(t, carry):
        for k in range(TOP_K):
            pltpu.make_async_copy(ys_ref.at[_row_tile(0)], buf_ref.at[cur, 0, _row_tile(0)], sem.at[cur]).wait()
        return carry
    lax.fori_loop(0, tm, drain, 0)

    gate = gate_ref[...]
    parts = []
    for j in range(ROW_TILE):
        part = gate[:, 0:1] * buf_ref[cur, 0, pl.ds(j, tm, stride=ROW_TILE), :]
        for k in range(1, TOP_K):
            part = part + gate[:, k:k + 1] * buf_ref[cur, k, pl.ds(j, tm, stride=ROW_TILE), :]
        parts.append(part)
    acc = acc + jnp.concatenate(parts, axis=1)
    o_ref[...] = _layer_norm(acc, g_ref[...], b_ref[...])


def _combine(h, ys, dest, gate_t, shared_w_gu, shared_w_down, ln_g, ln_b, alpha):
    n_tok, d = h.shape
    tm = 256
    n_tiles = n_tok // tm
    full = lambda a: pl.BlockSpec(a.shape, lambda i: (0,) * a.ndim)
    wgu = shared_w_gu.astype(BF16)
    wdn = shared_w_down.astype(BF16)
    g2 = ln_g.reshape(1, d)
    b2 = ln_b.reshape(1, d)
    return pl.pallas_call(
        functools.partial(_combine_kernel, alpha=alpha),
        out_shape=jax.ShapeDtypeStruct((n_tok, d), F32),
        grid=(n_tiles,),
        in_specs=[pl.BlockSpec((TOP_K, tm), lambda i: (0, i), memory_space=pltpu.SMEM),
                  pl.BlockSpec((TOP_K, tm), lambda i: (0, jnp.minimum(i + 1, n_tiles - 1)),
                               memory_space=pltpu.SMEM),
                  pl.BlockSpec((tm, TOP_K), lambda i: (i, 0)),
                  pl.BlockSpec((tm, d), lambda i: (i, 0)),
                  pl.BlockSpec(memory_space=pl.ANY),
                  full(wgu), full(wdn), full(g2), full(b2)],
        out_specs=pl.BlockSpec((tm, d), lambda i: (i, 0)),
        scratch_shapes=[pltpu.VMEM((2, TOP_K, tm * ROW_TILE, LANES), F32), pltpu.SemaphoreType.DMA((2,))],
        compiler_params=_cparams(("arbitrary",), 48),
        name="combine",
    )(dest, dest, gate_t, h, ys, wgu, wdn, g2, b2)


def _moe(h, router_w, router_bias, w_gu, w_down, shared_w_gu, shared_w_down, ln_g, ln_b, alpha):
    top_e, gate, rank, cnt = _router(h, router_w, router_bias)
    counts = cnt[:, 0].astype(jnp.int32)
    bstart, bend, n_blk, pad_start = _expert_blocks(counts)
    dest = _dest(top_e, rank, pad_start)
    xs = _dispatch(h, dest, pad_start, counts)
    ys = _experts(xs, bstart, bend, n_blk, w_gu, w_down)
    return _combine(h, ys, dest, gate.T, shared_w_gu, shared_w_down, ln_g, ln_b, alpha)


def kernel(x, w_in, att_norm_g, lam_re, lam_im, log_step, b_re, b_im, c_re, c_im, d_skip, w_glu, b_glu,
           ssm_norm_g, w_out, ln1_g, ln1_b, router_w, router_bias, w_gu, w_down, shared_w_gu,
           shared_w_down, ln2_g, ln2_b):
    bsz, seq, d = x.shape
    depth = w_in.shape[0]
    alpha = (2 * depth) ** 0.25
    h = x.reshape(bsz * seq, d)
    for i in range(depth):
        proj = _inproj(h, w_in[i].astype(BF16), seq)
        o_att = _attention(proj, bsz, seq)
        y_ssm = _s5(proj.reshape(bsz, seq, -1), 3 * ATT_WIDTH, lam_re[i], lam_im[i], log_step[i],
                    b_re[i], b_im[i], c_re[i], c_im[i], d_skip[i])
        h = _mixout(o_att, y_ssm.reshape(bsz * seq, -1), h, w_glu[i], b_glu[i], att_norm_g[i],
                    ssm_norm_g[i], w_out[i], ln1_g[i], ln1_b[i], alpha)
        h = _moe(h, router_w[i], router_bias[i], w_gu[i], w_down[i], shared_w_gu[i], shared_w_down[i],
                 ln2_g[i], ln2_b[i], alpha)
    return h.reshape(bsz, seq, d)
```

```python
import functools
import math

import jax
import jax.numpy as jnp
from jax import lax
from jax.experimental import pallas as pl
from jax.experimental.pallas import tpu as pltpu

F32 = jnp.float32
BF16 = jnp.bfloat16
U32 = jnp.uint32

ATT_HEADS = 8
HEAD_DIM = 64
ATT_WIDTH = ATT_HEADS * HEAD_DIM
SSM_CH = 16
SSM_STATE = 64
ROPE_THETA = 500000.0
ROT_DIM = HEAD_DIM // 4
DILATIONS = (1, 4, 16)
ATT_BLOCK = 128
ATT_GROUP = 4
N_EXPERTS = 256
TOP_K = 8
N_EXPERT_GROUPS = 8
TOPK_GROUPS = 4
ROUTED_SCALE = 2.5
LN_EPS = 1e-5
RMS_EPS = 1e-6

LANES = 128
SUBLANES = 8
EXPERT_ROWS = 128
EXPERT_RING = 8
ROW_TILE = 4
NEG_INF = float("-inf")


def _cparams(sem, vmem_mb):
    return pltpu.CompilerParams(dimension_semantics=sem, vmem_limit_bytes=vmem_mb * 1024 * 1024)


def _inproj_kernel(x_ref, w_ref, cos_ref, sa_ref, sb_ref, o_ref, *, n_rot_cols):
    xb = x_ref[...].astype(BF16)
    cosf = cos_ref[...]
    sa = sa_ref[...]
    sb = sb_ref[...]
    width = o_ref.shape[1]
    chunk = 512
    for c in range(width // chunk):
        r = jnp.dot(xb, w_ref[:, c * chunk:(c + 1) * chunk], preferred_element_type=F32)
        if c * chunk < n_rot_cols:
            parts = []
            for s in range(chunk // LANES):
                t = r[:, s * LANES:(s + 1) * LANES]
                parts.append(t * cosf + pltpu.roll(t, LANES - ROT_DIM // 2, 1) * sa
                             + pltpu.roll(t, ROT_DIM // 2, 1) * sb)
            r = jnp.concatenate(parts, axis=1)
        o_ref[:, c * chunk:(c + 1) * chunk] = r


def _rope_lane_tables(seq):
    half = ROT_DIM // 2
    inv_freq = jnp.power(jnp.float32(ROPE_THETA), -jnp.arange(half, dtype=F32) / half)
    ang = jnp.arange(seq, dtype=F32)[:, None] * inv_freq[None, :]
    cos, sin = jnp.cos(ang), jnp.sin(ang)
    rest = HEAD_DIM - ROT_DIM
    cos_h = jnp.concatenate([cos, cos, jnp.ones((seq, rest), F32)], axis=1)
    sa_h = jnp.concatenate([-sin, jnp.zeros((seq, half + rest), F32)], axis=1)
    sb_h = jnp.concatenate([jnp.zeros((seq, half), F32), sin, jnp.zeros((seq, rest), F32)], axis=1)
    rep = LANES // HEAD_DIM
    return tuple(jnp.tile(t, (1, rep)) for t in (cos_h, sa_h, sb_h))


def _inproj(x2d, w_in_bf, seq):
    n_tok, d = x2d.shape
    width = w_in_bf.shape[1]
    tm = 512
    cosf, sa, sb = _rope_lane_tables(seq)
    tab_spec = pl.BlockSpec((tm, LANES), lambda i: (i % (seq // tm), 0))
    return pl.pallas_call(
        functools.partial(_inproj_kernel, n_rot_cols=2 * ATT_WIDTH),
        out_shape=jax.ShapeDtypeStruct((n_tok, width), F32),
        grid=(n_tok // tm,),
        in_specs=[pl.BlockSpec((tm, d), lambda i: (i, 0)),
                  pl.BlockSpec((d, width), lambda i: (0, 0)),
                  tab_spec, tab_spec, tab_spec],
        out_specs=pl.BlockSpec((tm, width), lambda i: (i, 0)),
        compiler_params=_cparams(("parallel",), 48),
        name="inproj",
    )(x2d, w_in_bf, cosf, sa, sb)


def _attn_kernel(q_ref, k_ref, v_ref, o_ref, ob_ref, lb_ref, band_ref, causal_ref, *, seq):
    blk = ATT_BLOCK
    lane = lax.broadcasted_iota(jnp.int32, (1, LANES), 1)
    head0 = lane < HEAD_DIM
    scale = HEAD_DIM ** -0.5

    qi = lax.broadcasted_iota(jnp.int32, (blk, 2 * blk), 0)
    kj = lax.broadcasted_iota(jnp.int32, (blk, 2 * blk), 1)
    dist = qi + blk - kj
    band_ref[...] = jnp.where((dist >= 0) & (dist <= blk), 0.0, NEG_INF)
    causal_ref[...] = jnp.where(lax.broadcasted_iota(jnp.int32, (blk, blk), 0)
                                >= lax.broadcasted_iota(jnp.int32, (blk, blk), 1), 0.0, NEG_INF)

    def rows(start, d):
        if d == 1:
            return pl.ds(start, blk)
        return pl.ds(start, blk, stride=d)

    def one_block(c, d, start_q, with_prev):
        sl_q = rows(start_q, d)
        q = q_ref[sl_q, :] * scale
        kk = k_ref[sl_q, :]
        vv = v_ref[sl_q, :]
        if with_prev:
            sl_p = rows(start_q - d * blk, d)
            kk = jnp.concatenate([k_ref[sl_p, :], kk], axis=0)
            vv = jnp.concatenate([v_ref[sl_p, :], vv], axis=0)
            bias_ref = band_ref
        else:
            bias_ref = causal_ref
        kk = kk.astype(BF16)
        vv = vv.astype(BF16)
        outs = []
        lses = []
        for h in range(LANES // HEAD_DIM):
            hm = head0 if h == 0 else jnp.logical_not(head0)
            qh = jnp.where(hm, q, 0.0).astype(BF16)
            s = lax.dot_general(qh, kk, (((1,), (1,)), ((), ())), preferred_element_type=F32)
            s = s + bias_ref[...]
            m = jnp.max(s, axis=-1, keepdims=True)
            p = jnp.exp(s - m)
            den = jnp.sum(p, axis=-1, keepdims=True)
            outs.append(jnp.dot(p.astype(BF16), vv, preferred_element_type=F32) / den)
            lses.append(m + jnp.log(den))
        ob_ref[c, sl_q, :] = jnp.where(head0, outs[0], outs[1])
        lb_ref[c, sl_q, :] = jnp.where(head0, lses[0], lses[1])

    def run_blocks(n_blocks, group, fn):
        assert n_blocks % group == 0
        if n_blocks == group:
            for g in range(group):
                fn(g)
            return
        def body(it, carry):
            for g in range(group):
                fn(it * group + g)
            return carry
        lax.fori_loop(0, n_blocks // group, body, 0)

    for c, d in enumerate(DILATIONS):
        nb = seq // d // blk
        run_blocks(d, min(d, ATT_GROUP), lambda r, c=c, d=d: one_block(c, d, r, False))
        rest = (nb - 1) * d
        if rest:
            def later(idx, c=c, d=d):
                n = idx // d
                r = idx - n * d
                one_block(c, d, r + (n + 1) * (d * blk), True)
            group = max(g for g in range(1, ATT_GROUP + 1) if rest % g == 0)
            run_blocks(rest, group, later)

    rc = 256
    def merge(i, carry):
        sl = pl.ds(pl.multiple_of(i * rc, rc), rc)
        l0 = lb_ref[0, sl, :]
        l1 = lb_ref[1, sl, :]
        l2 = lb_ref[2, sl, :]
        mx = jnp.maximum(jnp.maximum(l0, l1), l2)
        e0 = jnp.exp(l0 - mx)
        e1 = jnp.exp(l1 - mx)
        e2 = jnp.exp(l2 - mx)
        tot = e0 + e1 + e2
        o_ref[sl, :] = ((e0 / tot) * ob_ref[0, sl, :] + (e1 / tot) * ob_ref[1, sl, :]
                        + (e2 / tot) * ob_ref[2, sl, :])
        return carry
    lax.fori_loop(0, seq // rc, merge, 0)


def _attention(proj, bsz, seq):
    n_tok = proj.shape[0]
    pairs = ATT_WIDTH // LANES
    assert seq % (ATT_BLOCK * max(DILATIONS)) == 0
    blk = (seq, LANES)
    return pl.pallas_call(
        functools.partial(_attn_kernel, seq=seq),
        out_shape=jax.ShapeDtypeStruct((n_tok, ATT_WIDTH), F32),
        grid=(bsz, pairs),
        in_specs=[pl.BlockSpec(blk, lambda b, h: (b, h)),
                  pl.BlockSpec(blk, lambda b, h: (b, pairs + h)),
                  pl.BlockSpec(blk, lambda b, h: (b, 2 * pairs + h))],
        out_specs=pl.BlockSpec(blk, lambda b, h: (b, h)),
        scratch_shapes=[pltpu.VMEM((len(DILATIONS), seq, LANES), F32),
                        pltpu.VMEM((len(DILATIONS), seq, LANES), F32),
                        pltpu.VMEM((ATT_BLOCK, 2 * ATT_BLOCK), F32),
                        pltpu.VMEM((ATT_BLOCK, ATT_BLOCK), F32)],
        compiler_params=_cparams(("parallel", "parallel"), 32),
        name="attn",
    )(proj, proj, proj)


def _s5_kernel(u_ref, bm_ref, lam_ref, cm_ref, dk_ref, o_ref, us_ref, st_ref, ys_ref, carry_ref, *, tc):
    bsz = u_ref.shape[0]
    half = st_ref.shape[1] // 2
    rows = tc * bsz
    mm_rows = 512

    @pl.when(pl.program_id(1) == 0)
    def _():
        carry_ref[...] = jnp.zeros_like(carry_ref)

    for b in range(bsz):
        us_ref[pl.ds(b, tc, stride=bsz), :] = u_ref[b]

    bm = bm_ref[0]
    for r0 in range(0, rows, mm_rows):
        st_ref[r0:r0 + mm_rows, :] = jnp.dot(us_ref[r0:r0 + mm_rows, :].astype(BF16), bm,
                                             preferred_element_type=F32)

    lam = lam_ref[0]
    lam_re = lam[:, :half]
    lam_im = lam[:, half:]

    def step(t, carry):
        xr, xi = carry
        sl = pl.ds(pl.multiple_of(t * bsz, bsz), bsz)
        nr = lam_re * xr - lam_im * xi + st_ref[sl, :half]
        ni = lam_re * xi + lam_im * xr + st_ref[sl, half:]
        st_ref[sl, :half] = nr
        st_ref[sl, half:] = ni
        return nr, ni

    xr, xi = lax.fori_loop(0, tc, step, (carry_ref[:, :half], carry_ref[:, half:]), unroll=4)
    carry_ref[:, :half] = xr
    carry_ref[:, half:] = xi

    cm = cm_ref[0]
    for r0 in range(0, rows, mm_rows):
        ys_ref[r0:r0 + mm_rows, :] = jnp.dot(st_ref[r0:r0 + mm_rows, :].astype(BF16), cm,
                                             preferred_element_type=F32)
    dk = dk_ref[...]
    for b in range(bsz):
        o_ref[b] = ys_ref[pl.ds(b, tc, stride=bsz), :] + dk * u_ref[b]


def _s5_params(lam_re, lam_im, log_step, b_re, b_im, c_re, c_im, bsz):
    groups = lam_re.shape[0]
    gpc = LANES // SSM_CH
    n_chunks = groups // gpc
    lam = lax.complex(lam_re.astype(F32), lam_im.astype(F32))
    step = jnp.exp(log_step.astype(F32))[:, None]
    lam_bar = jnp.exp(lam * step)
    bmat = lax.complex(b_re.astype(F32), b_im.astype(F32))
    b_bar = ((lam_bar - 1.0) / lam)[..., None] * bmat
    eye = jnp.eye(gpc, dtype=F32)

    def block_diag_in(t):
        t = t.reshape(n_chunks, gpc, SSM_STATE, SSM_CH)
        return jnp.einsum('ngpc,gh->ngchp', t, eye).reshape(n_chunks, gpc * SSM_CH, gpc * SSM_STATE)

    def block_diag_out(t):
        t = t.reshape(n_chunks, gpc, SSM_CH, SSM_STATE)
        return jnp.einsum('ngcp,gh->ngphc', t, eye).reshape(n_chunks, gpc * SSM_STATE, gpc * SSM_CH)

    bm = jnp.concatenate([block_diag_in(b_bar.real), block_diag_in(b_bar.imag)], axis=2).astype(BF16)
    cm = jnp.concatenate([block_diag_out(c_re.astype(F32)), block_diag_out(-c_im.astype(F32))],
                         axis=1).astype(BF16)
    lam_row = jnp.concatenate([lam_bar.real.reshape(n_chunks, gpc * SSM_STATE),
                               lam_bar.imag.reshape(n_chunks, gpc * SSM_STATE)], axis=1)
    lam_t = jnp.broadcast_to(lam_row[:, None, :], (n_chunks, bsz, 2 * gpc * SSM_STATE))
    return bm, lam_t, cm, n_chunks


def _s5(proj3, u_col0, lam_re, lam_im, log_step, b_re, b_im, c_re, c_im, d_skip):
    bsz, seq, _ = proj3.shape
    assert bsz == SUBLANES
    bm, lam_t, cm, n_chunks = _s5_params(lam_re, lam_im, log_step, b_re, b_im, c_re, c_im, bsz)
    width = n_chunks * LANES
    tc = 256
    st_cols = bm.shape[2]
    ublk0 = u_col0 // LANES
    return pl.pallas_call(
        functools.partial(_s5_kernel, tc=tc),
        out_shape=jax.ShapeDtypeStruct((bsz, seq, width), F32),
        grid=(n_chunks, seq // tc),
        in_specs=[pl.BlockSpec((bsz, tc, LANES), lambda c, t: (0, t, ublk0 + c)),
                  pl.BlockSpec((1, LANES, st_cols), lambda c, t: (c, 0, 0)),
                  pl.BlockSpec((1, bsz, st_cols), lambda c, t: (c, 0, 0)),
                  pl.BlockSpec((1, st_cols, LANES), lambda c, t: (c, 0, 0)),
                  pl.BlockSpec((1, LANES), lambda c, t: (0, c))],
        out_specs=pl.BlockSpec((bsz, tc, LANES), lambda c, t: (0, t, c)),
        scratch_shapes=[pltpu.VMEM((tc * bsz, LANES), F32),
                        pltpu.VMEM((tc * bsz, st_cols), F32),
                        pltpu.VMEM((tc * bsz, LANES), F32),
                        pltpu.VMEM((bsz, st_cols), F32)],
        compiler_params=_cparams(("arbitrary", "arbitrary"), 40),
        name="s5",
    )(proj3, bm, lam_t, cm, d_skip.reshape(1, width).astype(F32))


def _layer_norm(v, g, b):
    mu = jnp.mean(v, axis=-1, keepdims=True)
    var = jnp.mean(jnp.square(v - mu), axis=-1, keepdims=True)
    return (v - mu) * lax.rsqrt(var + LN_EPS) * g + b


def _rms_norm(v, g):
    return v * lax.rsqrt(jnp.mean(jnp.square(v), axis=-1, keepdims=True) + RMS_EPS) * g


def _mixout_kernel(att_ref, ssm_ref, x_ref, wglu_ref, bglu_ref, ag_ref, sg_ref, wout_ref, g_ref, b_ref,
                   o_ref, *, alpha):
    y = jax.nn.gelu(ssm_ref[...])
    z = jnp.dot(y.astype(BF16), wglu_ref[...], preferred_element_type=F32) + bglu_ref[...]
    o_ssm = y * jax.nn.sigmoid(z)
    a = _rms_norm(att_ref[...], ag_ref[...]).astype(BF16)
    s = _rms_norm(o_ssm, sg_ref[...]).astype(BF16)
    wa = att_ref.shape[1]
    mix = (jnp.dot(a, wout_ref[:wa, :], preferred_element_type=F32)
           + jnp.dot(s, wout_ref[wa:, :], preferred_element_type=F32))
    o_ref[...] = _layer_norm(alpha * x_ref[...] + mix, g_ref[...], b_ref[...])


def _mixout(o_att, y_ssm, x2d, w_glu, b_glu, att_g, ssm_g, w_out, ln_g, ln_b, alpha):
    n_tok, d = x2d.shape
    wa = o_att.shape[1]
    ws = y_ssm.shape[1]
    tm = 256
    row = lambda w: pl.BlockSpec((tm, w), lambda i: (i, 0))
    full = lambda a: pl.BlockSpec(a.shape, lambda i: (0,) * a.ndim)
    args = (o_att, y_ssm, x2d, w_glu.astype(BF16), b_glu.reshape(1, ws), att_g.reshape(1, wa),
            ssm_g.reshape(1, ws), w_out.astype(BF16), ln_g.reshape(1, d), ln_b.reshape(1, d))
    return pl.pallas_call(
        functools.partial(_mixout_kernel, alpha=alpha),
        out_shape=jax.ShapeDtypeStruct((n_tok, d), F32),
        grid=(n_tok // tm,),
        in_specs=[row(wa), row(ws), row(d)] + [full(a) for a in args[3:]],
        out_specs=row(d),
        compiler_params=_cparams(("parallel",), 32),
        name="mixout",
    )(*args)


def _split_bf16(v):
    hi = v.astype(BF16)
    lo = (v - hi.astype(F32)).astype(BF16)
    return hi, lo


def _router_kernel(h_ref, wt_ref, bias_ref, e_ref, g_ref, r_ref, cnt_ref, run_ref):
    tm = h_ref.shape[0]
    n_exp = wt_ref.shape[0]
    gsz = n_exp // N_EXPERT_GROUPS

    @pl.when(pl.program_id(0) == 0)
    def _():
        run_ref[...] = jnp.zeros_like(run_ref)

    w_hi, w_lo = _split_bf16(wt_ref[...])
    h_hi, h_lo = _split_bf16(h_ref[...])
    nt = (((1,), (1,)), ((), ()))
    logits = (lax.dot_general(w_hi, h_hi, nt, preferred_element_type=F32)
              + lax.dot_general(w_hi, h_lo, nt, preferred_element_type=F32)
              + lax.dot_general(w_lo, h_hi, nt, preferred_element_type=F32))
    scores = jax.nn.sigmoid(logits)
    choice = scores + bias_ref[:, 0:1]

    gio = lax.broadcasted_iota(jnp.int32, (gsz, tm), 0).astype(F32)
    gscore = []
    for g in range(N_EXPERT_GROUPS):
        cg = choice[g * gsz:(g + 1) * gsz, :]
        m1 = jnp.max(cg, axis=0, keepdims=True)
        i1 = jnp.min(jnp.where(cg == m1, gio, float(gsz)), axis=0, keepdims=True)
        m2 = jnp.max(jnp.where(gio == i1, NEG_INF, cg), axis=0, keepdims=True)
        gscore.append(m1 + m2)
    masked = []
    for g in range(N_EXPERT_GROUPS):
        beat = jnp.zeros((1, tm), F32)
        for o in range(N_EXPERT_GROUPS):
            if o == g:
                continue
            wins = (gscore[o] >= gscore[g]) if o < g else (gscore[o] > gscore[g])
            beat = beat + jnp.where(wins, 1.0, 0.0)
        keep = beat < float(TOPK_GROUPS)
        masked.append(jnp.where(keep, choice[g * gsz:(g + 1) * gsz, :], NEG_INF))
    cur = jnp.concatenate(masked, axis=0)

    eio = lax.broadcasted_iota(jnp.int32, (n_exp, tm), 0).astype(F32)
    idxs = []
    gates = []
    onehot = jnp.zeros((n_exp, tm), F32)
    for _ in range(TOP_K):
        m = jnp.max(cur, axis=0, keepdims=True)
        idx = jnp.min(jnp.where(cur == m, eio, float(n_exp)), axis=0, keepdims=True)
        hit = eio == idx
        idxs.append(idx)
        gates.append(jnp.sum(jnp.where(hit, scores, 0.0), axis=0, keepdims=True))
        cur = jnp.where(hit, NEG_INF, cur)
        onehot = onehot + jnp.where(hit, 1.0, 0.0)
    gate = jnp.concatenate(gates, axis=0)
    gate = ROUTED_SCALE * gate / (jnp.sum(gate, axis=0, keepdims=True) + 1e-20)

    si = lax.broadcasted_iota(jnp.int32, (tm, tm), 0)
    ti = lax.broadcasted_iota(jnp.int32, (tm, tm), 1)
    upper = jnp.where(si < ti, 1.0, 0.0).astype(BF16)
    before = jnp.dot(onehot.astype(BF16), upper, preferred_element_type=F32) + run_ref[:, 0:1]
    ranks = [jnp.sum(jnp.where(eio == idx, before, 0.0), axis=0, keepdims=True) for idx in idxs]

    e_ref[...] = jnp.concatenate(idxs, axis=0).astype(jnp.int32)
    g_ref[...] = gate
    r_ref[...] = jnp.concatenate(ranks, axis=0).astype(jnp.int32)
    run_ref[...] = run_ref[...] + jnp.sum(onehot, axis=1, keepdims=True)
    cnt_ref[...] = run_ref[...]


def _router(h, router_w, router_bias):
    n_tok, d = h.shape
    n_exp = router_w.shape[1]
    tm = 256
    wt = router_w.astype(F32).T
    bias = jnp.broadcast_to(router_bias.astype(F32)[:, None], (n_exp, LANES))
    tok = pl.BlockSpec((TOP_K, tm), lambda i: (0, i))
    return pl.pallas_call(
        _router_kernel,
        out_shape=(jax.ShapeDtypeStruct((TOP_K, n_tok), jnp.int32),
                   jax.ShapeDtypeStruct((TOP_K, n_tok), F32),
                   jax.ShapeDtypeStruct((TOP_K, n_tok), jnp.int32),
                   jax.ShapeDtypeStruct((n_exp, LANES), F32)),
        grid=(n_tok // tm,),
        in_specs=[pl.BlockSpec((tm, d), lambda i: (i, 0)),
                  pl.BlockSpec((n_exp, d), lambda i: (0, 0)),
                  pl.BlockSpec((n_exp, LANES), lambda i: (0, 0))],
        out_specs=(tok, tok, tok, pl.BlockSpec((n_exp, LANES), lambda i: (0, 0))),
        scratch_shapes=[pltpu.VMEM((n_exp, LANES), F32)],
        compiler_params=_cparams(("arbitrary",), 32),
        name="router",
    )(h, wt, bias)


def _dest_kernel(e_ref, r_ref, st_ref, d_ref):
    n_exp = st_ref.shape[0]
    tm = e_ref.shape[1]
    eio = lax.broadcasted_iota(jnp.int32, (n_exp, tm), 0)
    start = st_ref[:, 0:1]
    rows = [jnp.sum(jnp.where(eio == e_ref[k:k + 1, :], start, 0.0), axis=0, keepdims=True)
            for k in range(TOP_K)]
    d_ref[...] = jnp.concatenate(rows, axis=0).astype(jnp.int32) + r_ref[...]


def _dest(top_e, rank, starts):
    n_tok = top_e.shape[1]
    n_exp = starts.shape[0]
    tm = 512
    st = jnp.broadcast_to(starts.astype(F32)[:, None], (n_exp, LANES))
    tok = pl.BlockSpec((TOP_K, tm), lambda i: (0, i))
    return pl.pallas_call(
        _dest_kernel,
        out_shape=jax.ShapeDtypeStruct((TOP_K, n_tok), jnp.int32),
        grid=(n_tok // tm,),
        in_specs=[tok, tok, pl.BlockSpec((n_exp, LANES), lambda i: (0, 0))],
        out_specs=tok,
        compiler_params=_cparams(("parallel",), 32),
        name="dest",
    )(top_e, rank, st)


def _pack_bf16_pairs(val):
    half = val.shape[1] // 2
    lo = pltpu.bitcast(val[:, :half].astype(BF16).astype(F32), U32)
    hi = pltpu.bitcast(val[:, half:].astype(BF16).astype(F32), U32)
    return (lo >> 16) | (hi & jnp.uint32(0xFFFF0000))


def _unpack_bf16_pairs(words):
    lo = pltpu.bitcast(words << 16, F32)
    hi = pltpu.bitcast(words & jnp.uint32(0xFFFF0000), F32)
    return jnp.concatenate([lo, hi], axis=1)


def _to_row_tiles(dst_ref, slot, val):
    rows = val.shape[0]
    words = _pack_bf16_pairs(val)
    for j in range(ROW_TILE):
        dst_ref[slot, pl.ds(j, rows, stride=ROW_TILE), :] = words[:, j * LANES:(j + 1) * LANES]


def _row_tile_words(src_ref, idx, rows):
    return jnp.concatenate([src_ref[(*idx, pl.ds(j, rows, stride=ROW_TILE), slice(None))]
                            for j in range(ROW_TILE)], axis=1)


def _row_tile(r):
    return pl.ds(pl.multiple_of(r * ROW_TILE, ROW_TILE), ROW_TILE)


def _dispatch_kernel(dest_ref, h_ref, xs_ref, ht_ref, sem):
    tm = h_ref.shape[0]
    i = pl.program_id(0)
    cur = i % 2
    _to_row_tiles(ht_ref, cur, h_ref[...])

    def issue(t, carry):
        for k in range(TOP_K):
            pltpu.make_async_copy(ht_ref.at[cur, _row_tile(t)], xs_ref.at[_row_tile(dest_ref[k, t])],
                                  sem.at[cur]).start(priority=k % 2)
        return carry
    lax.fori_loop(0, tm, issue, 0, unroll=2)

    def drain(slot):
        def body(t, carry):
            for k in range(TOP_K):
                pltpu.make_async_copy(ht_ref.at[slot, _row_tile(0)], xs_ref.at[_row_tile(0)],
                                      sem.at[slot]).wait()
            return carry
        lax.fori_loop(0, tm, body, 0)

    @pl.when(i > 0)
    def _():
        drain(1 - cur)

    @pl.when(i == pl.num_programs(0) - 1)
    def _():
        drain(cur)


def _dispatch(h, dest, n_exp):
    n_tok, d = h.shape
    assert d == 2 * ROW_TILE * LANES
    tm = 256
    n_rows = n_tok * TOP_K + n_exp * EXPERT_ROWS
    return pl.pallas_call(
        _dispatch_kernel,
        out_shape=jax.ShapeDtypeStruct((n_rows * ROW_TILE, LANES), U32),
        grid=(n_tok // tm,),
        in_specs=[pl.BlockSpec((TOP_K, tm), lambda i: (0, i), memory_space=pltpu.SMEM),
                  pl.BlockSpec((tm, d), lambda i: (i, 0))],
        out_specs=pl.BlockSpec(memory_space=pl.ANY),
        scratch_shapes=[pltpu.VMEM((2, tm * ROW_TILE, LANES), U32), pltpu.SemaphoreType.DMA((2,))],
        compiler_params=_cparams(("arbitrary",), 32),
        name="dispatch",
    )(dest, h)


def _experts_kernel(bstart_ref, bend_ref, cnt_ref, nblk_ref, xs_ref, wgu_ref, wdn_ref, ys_ref,
                    xbuf, ybuf, act_ref, wgu_bf, wdn_bf, xsem, ysem):
    e = pl.program_id(0)
    n_blk = nblk_ref[0]
    trows = xbuf.shape[1]
    rows = trows // ROW_TILE
    ff = wdn_bf.shape[0]
    b0 = bstart_ref[e]
    b1 = bend_ref[e]

    def block_rows(b):
        return pl.ds(pl.multiple_of(b * trows, trows), trows)

    def x_copy(b):
        slot = b % EXPERT_RING
        return pltpu.make_async_copy(xs_ref.at[block_rows(b)], xbuf.at[slot], xsem.at[slot])

    def y_copy(b):
        slot = b % 2
        return pltpu.make_async_copy(ybuf.at[slot], ys_ref.at[block_rows(b)], ysem.at[slot])

    @pl.when(e == 0)
    def _():
        for i in range(EXPERT_RING):
            @pl.when(i < n_blk)
            def _():
                x_copy(i).start(priority=1)

    wgu_bf[...] = wgu_ref[0].astype(BF16)
    wdn_bf[...] = wdn_ref[0].astype(BF16)

    def up(blocks):
        for b in blocks:
            x_copy(b).wait()
        for b in blocks:
            words = _row_tile_words(xbuf, (b % EXPERT_RING,), rows)
            row = lax.broadcasted_iota(jnp.int32, (rows, 1), 0)
            words = jnp.where(row < cnt_ref[e] - (b - b0) * rows, words, jnp.uint32(0))
            xb = _unpack_bf16_pairs(words).astype(BF16)
            gu = jnp.dot(xb, wgu_bf[...], preferred_element_type=F32)
            act_ref[b - b0] = (jax.nn.silu(gu[:, :ff]) * gu[:, ff:]).astype(BF16)
        for b in blocks:
            @pl.when(b + EXPERT_RING < n_blk)
            def _():
                x_copy(b + EXPERT_RING).start(priority=1)

    def up_pair(p, carry):
        up((b0 + 2 * p, b0 + 2 * p + 1))
        return carry
    n_mine = b1 - b0
    lax.fori_loop(0, n_mine // 2, up_pair, 0)

    @pl.when(n_mine % 2 == 1)
    def _():
        up((b1 - 1,))

    def down(blocks):
        for b in blocks:
            @pl.when(b >= 2)
            def _():
                y_copy(b - 2).wait()
        for b in blocks:
            _to_row_tiles(ybuf, b % 2, jnp.dot(act_ref[b - b0], wdn_bf[...], preferred_element_type=F32))
        for b in blocks:
            y_copy(b).start(priority=1)
        last = blocks[-1]

        @pl.when(last == n_blk - 1)
        def _():
            @pl.when(last >= 1)
            def _():
                y_copy(last - 1).wait()
            y_copy(last).wait()

    def down_pair(p, carry):
        down((b0 + 2 * p, b0 + 2 * p + 1))
        return carry
    lax.fori_loop(0, n_mine // 2, down_pair, 0)

    @pl.when(n_mine % 2 == 1)
    def _():
        down((b1 - 1,))


def _expert_blocks(counts):
    blocks = (counts + EXPERT_ROWS - 1) // EXPERT_ROWS
    bend = jnp.cumsum(blocks)
    bstart = bend - blocks
    i32 = lambda a: a.astype(jnp.int32)
    return i32(bstart), i32(bend), i32(bend[-1]).reshape(1), i32(bstart * EXPERT_ROWS)


def _experts(xs, bstart, bend, counts, n_blk, w_gu, w_down):
    n_exp, d, ff2 = w_gu.shape
    ff = w_down.shape[1]
    n_rows = xs.shape[0] // ROW_TILE
    assert n_rows % EXPERT_ROWS == 0 and d == 2 * ROW_TILE * LANES
    max_blocks = (n_rows - n_exp * EXPERT_ROWS) // TOP_K // EXPERT_ROWS + 1
    grid_spec = pltpu.PrefetchScalarGridSpec(
        num_scalar_prefetch=4,
        grid=(n_exp,),
        in_specs=[pl.BlockSpec(memory_space=pl.ANY),
                  pl.BlockSpec((1, d, ff2), lambda e, *_: (e, 0, 0)),
                  pl.BlockSpec((1, ff, d), lambda e, *_: (e, 0, 0))],
        out_specs=pl.BlockSpec(memory_space=pl.ANY),
        scratch_shapes=[pltpu.VMEM((EXPERT_RING, EXPERT_ROWS * ROW_TILE, LANES), U32),
                        pltpu.VMEM((2, EXPERT_ROWS * ROW_TILE, LANES), U32),
                        pltpu.VMEM((max_blocks, EXPERT_ROWS, ff), BF16),
                        pltpu.VMEM((d, ff2), BF16), pltpu.VMEM((ff, d), BF16),
                        pltpu.SemaphoreType.DMA((EXPERT_RING,)),
                        pltpu.SemaphoreType.DMA((2,))],
    )
    return pl.pallas_call(
        _experts_kernel,
        out_shape=jax.ShapeDtypeStruct(xs.shape, U32),
        grid_spec=grid_spec,
        compiler_params=_cparams(("arbitrary",), 40),
        name="experts",
    )(bstart, bend, counts, n_blk, xs, w_gu, w_down)


def _combine_kernel(dest_ref, dnext_ref, gate_ref, h_ref, ys_ref, wgu_ref, wdn_ref, g_ref, b_ref, o_ref,
                    buf_ref, sem, *, alpha):
    tm = h_ref.shape[0]
    i = pl.program_id(0)
    cur = i % 2

    def gather(d_ref, slot):
        def issue(t, carry):
            for k in range(TOP_K):
                pltpu.make_async_copy(ys_ref.at[_row_tile(d_ref[k, t])], buf_ref.at[slot, k, _row_tile(t)],
                                      sem.at[slot]).start(priority=k % 2)
            return carry
        lax.fori_loop(0, tm, issue, 0, unroll=2)

    @pl.when(i == 0)
    def _():
        gather(dest_ref, 0)

    @pl.when(i + 1 < pl.num_programs(0))
    def _():
        gather(dnext_ref, 1 - cur)

    h = h_ref[...]
    ff = wdn_ref.shape[0]
    gu = jnp.dot(h.astype(BF16), wgu_ref[...], preferred_element_type=F32)
    act = (jax.nn.silu(gu[:, :ff]) * gu[:, ff:]).astype(BF16)
    acc = alpha * h + jnp.dot(act, wdn_ref[...], preferred_element_type=F32)

    def drain(t, carry):
        for k in range(TOP_K):
            pltpu.make_async_copy(ys_ref.at[_row_tile(0)], buf_ref.at[cur, 0, _row_tile(0)], sem.at[cur]).wait()
        return carry
    lax.fori_loop(0, tm, drain, 0)

    gate = gate_ref[...]
    for k in range(TOP_K):
        acc = acc + gate[:, k:k + 1] * _unpack_bf16_pairs(_row_tile_words(buf_ref, (cur, k), tm))
    o_ref[...] = _layer_norm(acc, g_ref[...], b_ref[...])


def _combine(h, ys, dest, gate_t, shared_w_gu, shared_w_down, ln_g, ln_b, alpha):
    n_tok, d = h.shape
    tm = 256
    n_tiles = n_tok // tm
    full = lambda a: pl.BlockSpec(a.shape, lambda i: (0,) * a.ndim)
    wgu = shared_w_gu.astype(BF16)
    wdn = shared_w_down.astype(BF16)
    g2 = ln_g.reshape(1, d)
    b2 = ln_b.reshape(1, d)
    return pl.pallas_call(
        functools.partial(_combine_kernel, alpha=alpha),
        out_shape=jax.ShapeDtypeStruct((n_tok, d), F32),
        grid=(n_tiles,),
        in_specs=[pl.BlockSpec((TOP_K, tm), lambda i: (0, i), memory_space=pltpu.SMEM),
                  pl.BlockSpec((TOP_K, tm), lambda i: (0, jnp.minimum(i + 1, n_tiles - 1)),
                               memory_space=pltpu.SMEM),
                  pl.BlockSpec((tm, TOP_K), lambda i: (i, 0)),
                  pl.BlockSpec((tm, d), lambda i: (i, 0)),
                  pl.BlockSpec(memory_space=pl.ANY),
                  full(wgu), full(wdn), full(g2), full(b2)],
        out_specs=pl.BlockSpec((tm, d), lambda i: (i, 0)),
        scratch_shapes=[pltpu.VMEM((2, TOP_K, tm * ROW_TILE, LANES), U32), pltpu.SemaphoreType.DMA((2,))],
        compiler_params=_cparams(("arbitrary",), 48),
        name="combine",
    )(dest, dest, gate_t, h, ys, wgu, wdn, g2, b2)


def _moe(h, router_w, router_bias, w_gu, w_down, shared_w_gu, shared_w_down, ln_g, ln_b, alpha):
    top_e, gate, rank, cnt = _router(h, router_w, router_bias)
    counts = cnt[:, 0].astype(jnp.int32)
    bstart, bend, n_blk, pad_start = _expert_blocks(counts)
    dest = _dest(top_e, rank, pad_start)
    xs = _dispatch(h, dest, counts.shape[0])
    ys = _experts(xs, bstart, bend, counts, n_blk, w_gu, w_down)
    return _combine(h, ys, dest, gate.T, shared_w_gu, shared_w_down, ln_g, ln_b, alpha)


def kernel(x, w_in, att_norm_g, lam_re, lam_im, log_step, b_re, b_im, c_re, c_im, d_skip, w_glu, b_glu,
           ssm_norm_g, w_out, ln1_g, ln1_b, router_w, router_bias, w_gu, w_down, shared_w_gu,
           shared_w_down, ln2_g, ln2_b):
    bsz, seq, d = x.shape
    depth = w_in.shape[0]
    alpha = (2 * depth) ** 0.25
    h = x.reshape(bsz * seq, d)
    for i in range(depth):
        proj = _inproj(h, w_in[i].astype(BF16), seq)
        o_att = _attention(proj, bsz, seq)
        y_ssm = _s5(proj.reshape(bsz, seq, -1), 3 * ATT_WIDTH, lam_re[i], lam_im[i], log_step[i],
                    b_re[i], b_im[i], c_re[i], c_im[i], d_skip[i])
        h = _mixout(o_att, y_ssm.reshape(bsz * seq, -1), h, w_glu[i], b_glu[i], att_norm_g[i],
                    ssm_norm_g[i], w_out[i], ln1_g[i], ln1_b[i], alpha)
        h = _moe(h, router_w[i], router_bias[i], w_gu[i], w_down[i], shared_w_gu[i], shared_w_down[i],
                 ln2_g[i], ln2_b[i], alpha)
    return h.reshape(bsz, seq, d)
```

```python
import functools
import math

import jax
import jax.numpy as jnp
from jax import lax
from jax.experimental import pallas as pl
from jax.experimental.pallas import tpu as pltpu

F32 = jnp.float32
BF16 = jnp.bfloat16
U32 = jnp.uint32

ATT_HEADS = 8
HEAD_DIM = 64
ATT_WIDTH = ATT_HEADS * HEAD_DIM
SSM_CH = 16
SSM_STATE = 64
ROPE_THETA = 500000.0
ROT_DIM = HEAD_DIM // 4
DILATIONS = (1, 4, 16)
ATT_BLOCK = 128
ATT_GROUP = 4
N_EXPERTS = 256
TOP_K = 8
N_EXPERT_GROUPS = 8
TOPK_GROUPS = 4
ROUTED_SCALE = 2.5
LN_EPS = 1e-5
RMS_EPS = 1e-6

LANES = 128
SUBLANES = 8
EXPERT_ROWS = 128
EXPERT_RING = 8
EXPERT_OUT_RING = 8
ROW_TILE = 4
NEG_INF = float("-inf")


def _cparams(sem, vmem_mb):
    return pltpu.CompilerParams(dimension_semantics=sem, vmem_limit_bytes=vmem_mb * 1024 * 1024)


def _inproj_kernel(x_ref, w_ref, cos_ref, sa_ref, sb_ref, o_ref, *, n_rot_cols):
    xb = x_ref[...].astype(BF16)
    cosf = cos_ref[...]
    sa = sa_ref[...]
    sb = sb_ref[...]
    width = o_ref.shape[1]
    chunk = 512
    for c in range(width // chunk):
        r = jnp.dot(xb, w_ref[:, c * chunk:(c + 1) * chunk], preferred_element_type=F32)
        if c * chunk < n_rot_cols:
            parts = []
            for s in range(chunk // LANES):
                t = r[:, s * LANES:(s + 1) * LANES]
                parts.append(t * cosf + pltpu.roll(t, LANES - ROT_DIM // 2, 1) * sa
                             + pltpu.roll(t, ROT_DIM // 2, 1) * sb)
            r = jnp.concatenate(parts, axis=1)
        o_ref[:, c * chunk:(c + 1) * chunk] = r


def _rope_lane_tables(seq):
    half = ROT_DIM // 2
    inv_freq = jnp.power(jnp.float32(ROPE_THETA), -jnp.arange(half, dtype=F32) / half)
    ang = jnp.arange(seq, dtype=F32)[:, None] * inv_freq[None, :]
    cos, sin = jnp.cos(ang), jnp.sin(ang)
    rest = HEAD_DIM - ROT_DIM
    cos_h = jnp.concatenate([cos, cos, jnp.ones((seq, rest), F32)], axis=1)
    sa_h = jnp.concatenate([-sin, jnp.zeros((seq, half + rest), F32)], axis=1)
    sb_h = jnp.concatenate([jnp.zeros((seq, half), F32), sin, jnp.zeros((seq, rest), F32)], axis=1)
    rep = LANES // HEAD_DIM
    return tuple(jnp.tile(t, (1, rep)) for t in (cos_h, sa_h, sb_h))


def _inproj(x2d, w_in_bf, seq):
    n_tok, d = x2d.shape
    width = w_in_bf.shape[1]
    tm = 512
    cosf, sa, sb = _rope_lane_tables(seq)
    tab_spec = pl.BlockSpec((tm, LANES), lambda i: (i % (seq // tm), 0))
    return pl.pallas_call(
        functools.partial(_inproj_kernel, n_rot_cols=2 * ATT_WIDTH),
        out_shape=jax.ShapeDtypeStruct((n_tok, width), F32),
        grid=(n_tok // tm,),
        in_specs=[pl.BlockSpec((tm, d), lambda i: (i, 0)),
                  pl.BlockSpec((d, width), lambda i: (0, 0)),
                  tab_spec, tab_spec, tab_spec],
        out_specs=pl.BlockSpec((tm, width), lambda i: (i, 0)),
        compiler_params=_cparams(("parallel",), 48),
        name="inproj",
    )(x2d, w_in_bf, cosf, sa, sb)


def _attn_kernel(q_ref, k_ref, v_ref, o_ref, ob_ref, lb_ref, band_ref, causal_ref, *, seq):
    blk = ATT_BLOCK
    lane = lax.broadcasted_iota(jnp.int32, (1, LANES), 1)
    head0 = lane < HEAD_DIM
    scale = HEAD_DIM ** -0.5

    qi = lax.broadcasted_iota(jnp.int32, (blk, 2 * blk), 0)
    kj = lax.broadcasted_iota(jnp.int32, (blk, 2 * blk), 1)
    dist = qi + blk - kj
    band_ref[...] = jnp.where((dist >= 0) & (dist <= blk), 0.0, NEG_INF)
    causal_ref[...] = jnp.where(lax.broadcasted_iota(jnp.int32, (blk, blk), 0)
                                >= lax.broadcasted_iota(jnp.int32, (blk, blk), 1), 0.0, NEG_INF)

    def rows(start, d):
        if d == 1:
            return pl.ds(start, blk)
        return pl.ds(start, blk, stride=d)

    def one_block(c, d, start_q, with_prev):
        sl_q = rows(start_q, d)
        q = q_ref[sl_q, :] * scale
        kk = k_ref[sl_q, :]
        vv = v_ref[sl_q, :]
        if with_prev:
            sl_p = rows(start_q - d * blk, d)
            kk = jnp.concatenate([k_ref[sl_p, :], kk], axis=0)
            vv = jnp.concatenate([v_ref[sl_p, :], vv], axis=0)
            bias_ref = band_ref
        else:
            bias_ref = causal_ref
        kk = kk.astype(BF16)
        vv = vv.astype(BF16)
        outs = []
        lses = []
        for h in range(LANES // HEAD_DIM):
            hm = head0 if h == 0 else jnp.logical_not(head0)
            qh = jnp.where(hm, q, 0.0).astype(BF16)
            s = lax.dot_general(qh, kk, (((1,), (1,)), ((), ())), preferred_element_type=F32)
            s = s + bias_ref[...]
            m = jnp.max(s, axis=-1, keepdims=True)
            p = jnp.exp(s - m)
            den = jnp.sum(p, axis=-1, keepdims=True)
            outs.append(jnp.dot(p.astype(BF16), vv, preferred_element_type=F32) / den)
            lses.append(m + jnp.log(den))
        ob_ref[c, sl_q, :] = jnp.where(head0, outs[0], outs[1])
        lb_ref[c, sl_q, :] = jnp.where(head0, lses[0], lses[1])

    def run_blocks(n_blocks, group, fn):
        assert n_blocks % group == 0
        if n_blocks == group:
            for g in range(group):
                fn(g)
            return
        def body(it, carry):
            for g in range(group):
                fn(it * group + g)
            return carry
        lax.fori_loop(0, n_blocks // group, body, 0)

    for c, d in enumerate(DILATIONS):
        nb = seq // d // blk
        run_blocks(d, min(d, ATT_GROUP), lambda r, c=c, d=d: one_block(c, d, r, False))
        rest = (nb - 1) * d
        if rest:
            def later(idx, c=c, d=d):
                n = idx // d
                r = idx - n * d
                one_block(c, d, r + (n + 1) * (d * blk), True)
            group = max(g for g in range(1, ATT_GROUP + 1) if rest % g == 0)
            run_blocks(rest, group, later)

    rc = 256
    def merge(i, carry):
        sl = pl.ds(pl.multiple_of(i * rc, rc), rc)
        l0 = lb_ref[0, sl, :]
        l1 = lb_ref[1, sl, :]
        l2 = lb_ref[2, sl, :]
        mx = jnp.maximum(jnp.maximum(l0, l1), l2)
        e0 = jnp.exp(l0 - mx)
        e1 = jnp.exp(l1 - mx)
        e2 = jnp.exp(l2 - mx)
        tot = e0 + e1 + e2
        o_ref[sl, :] = ((e0 / tot) * ob_ref[0, sl, :] + (e1 / tot) * ob_ref[1, sl, :]
                        + (e2 / tot) * ob_ref[2, sl, :])
        return carry
    lax.fori_loop(0, seq // rc, merge, 0)


def _attention(proj, bsz, seq):
    n_tok = proj.shape[0]
    pairs = ATT_WIDTH // LANES
    assert seq % (ATT_BLOCK * max(DILATIONS)) == 0
    blk = (seq, LANES)
    return pl.pallas_call(
        functools.partial(_attn_kernel, seq=seq),
        out_shape=jax.ShapeDtypeStruct((n_tok, ATT_WIDTH), F32),
        grid=(bsz, pairs),
        in_specs=[pl.BlockSpec(blk, lambda b, h: (b, h)),
                  pl.BlockSpec(blk, lambda b, h: (b, pairs + h)),
                  pl.BlockSpec(blk, lambda b, h: (b, 2 * pairs + h))],
        out_specs=pl.BlockSpec(blk, lambda b, h: (b, h)),
        scratch_shapes=[pltpu.VMEM((len(DILATIONS), seq, LANES), F32),
                        pltpu.VMEM((len(DILATIONS), seq, LANES), F32),
                        pltpu.VMEM((ATT_BLOCK, 2 * ATT_BLOCK), F32),
                        pltpu.VMEM((ATT_BLOCK, ATT_BLOCK), F32)],
        compiler_params=_cparams(("parallel", "parallel"), 32),
        name="attn",
    )(proj, proj, proj)


def _s5_kernel(u_ref, bm_ref, lam_ref, cm_ref, dk_ref, o_ref, us_ref, st_ref, ys_ref, carry_ref, *, tc):
    bsz = u_ref.shape[0]
    half = st_ref.shape[1] // 2
    rows = tc * bsz
    mm_rows = 512

    @pl.when(pl.program_id(1) == 0)
    def _():
        carry_ref[...] = jnp.zeros_like(carry_ref)

    for b in range(bsz):
        us_ref[pl.ds(b, tc, stride=bsz), :] = u_ref[b]

    bm = bm_ref[0]
    for r0 in range(0, rows, mm_rows):
        st_ref[r0:r0 + mm_rows, :] = jnp.dot(us_ref[r0:r0 + mm_rows, :].astype(BF16), bm,
                                             preferred_element_type=F32)

    lam = lam_ref[0]
    lam_re = lam[:, :half]
    lam_im = lam[:, half:]

    def step(t, carry):
        xr, xi = carry
        sl = pl.ds(pl.multiple_of(t * bsz, bsz), bsz)
        nr = lam_re * xr - lam_im * xi + st_ref[sl, :half]
        ni = lam_re * xi + lam_im * xr + st_ref[sl, half:]
        st_ref[sl, :half] = nr
        st_ref[sl, half:] = ni
        return nr, ni

    xr, xi = lax.fori_loop(0, tc, step, (carry_ref[:, :half], carry_ref[:, half:]), unroll=4)
    carry_ref[:, :half] = xr
    carry_ref[:, half:] = xi

    cm = cm_ref[0]
    for r0 in range(0, rows, mm_rows):
        ys_ref[r0:r0 + mm_rows, :] = jnp.dot(st_ref[r0:r0 + mm_rows, :].astype(BF16), cm,
                                             preferred_element_type=F32)
    dk = dk_ref[...]
    for b in range(bsz):
        o_ref[b] = ys_ref[pl.ds(b, tc, stride=bsz), :] + dk * u_ref[b]


def _s5_params(lam_re, lam_im, log_step, b_re, b_im, c_re, c_im, bsz):
    groups = lam_re.shape[0]
    gpc = LANES // SSM_CH
    n_chunks = groups // gpc
    lam = lax.complex(lam_re.astype(F32), lam_im.astype(F32))
    step = jnp.exp(log_step.astype(F32))[:, None]
    lam_bar = jnp.exp(lam * step)
    bmat = lax.complex(b_re.astype(F32), b_im.astype(F32))
    b_bar = ((lam_bar - 1.0) / lam)[..., None] * bmat
    eye = jnp.eye(gpc, dtype=F32)

    def block_diag_in(t):
        t = t.reshape(n_chunks, gpc, SSM_STATE, SSM_CH)
        return jnp.einsum('ngpc,gh->ngchp', t, eye).reshape(n_chunks, gpc * SSM_CH, gpc * SSM_STATE)

    def block_diag_out(t):
        t = t.reshape(n_chunks, gpc, SSM_CH, SSM_STATE)
        return jnp.einsum('ngcp,gh->ngphc', t, eye).reshape(n_chunks, gpc * SSM_STATE, gpc * SSM_CH)

    bm = jnp.concatenate([block_diag_in(b_bar.real), block_diag_in(b_bar.imag)], axis=2).astype(BF16)
    cm = jnp.concatenate([block_diag_out(c_re.astype(F32)), block_diag_out(-c_im.astype(F32))],
                         axis=1).astype(BF16)
    lam_row = jnp.concatenate([lam_bar.real.reshape(n_chunks, gpc * SSM_STATE),
                               lam_bar.imag.reshape(n_chunks, gpc * SSM_STATE)], axis=1)
    lam_t = jnp.broadcast_to(lam_row[:, None, :], (n_chunks, bsz, 2 * gpc * SSM_STATE))
    return bm, lam_t, cm, n_chunks


def _s5(proj3, u_col0, lam_re, lam_im, log_step, b_re, b_im, c_re, c_im, d_skip):
    bsz, seq, _ = proj3.shape
    assert bsz == SUBLANES
    bm, lam_t, cm, n_chunks = _s5_params(lam_re, lam_im, log_step, b_re, b_im, c_re, c_im, bsz)
    width = n_chunks * LANES
    tc = 256
    st_cols = bm.shape[2]
    ublk0 = u_col0 // LANES
    return pl.pallas_call(
        functools.partial(_s5_kernel, tc=tc),
        out_shape=jax.ShapeDtypeStruct((bsz, seq, width), F32),
        grid=(n_chunks, seq // tc),
        in_specs=[pl.BlockSpec((bsz, tc, LANES), lambda c, t: (0, t, ublk0 + c)),
                  pl.BlockSpec((1, LANES, st_cols), lambda c, t: (c, 0, 0)),
                  pl.BlockSpec((1, bsz, st_cols), lambda c, t: (c, 0, 0)),
                  pl.BlockSpec((1, st_cols, LANES), lambda c, t: (c, 0, 0)),
                  pl.BlockSpec((1, LANES), lambda c, t: (0, c))],
        out_specs=pl.BlockSpec((bsz, tc, LANES), lambda c, t: (0, t, c)),
        scratch_shapes=[pltpu.VMEM((tc * bsz, LANES), F32),
                        pltpu.VMEM((tc * bsz, st_cols), F32),
                        pltpu.VMEM((tc * bsz, LANES), F32),
                        pltpu.VMEM((bsz, st_cols), F32)],
        compiler_params=_cparams(("arbitrary", "arbitrary"), 40),
        name="s5",
    )(proj3, bm, lam_t, cm, d_skip.reshape(1, width).astype(F32))


def _layer_norm(v, g, b):
    mu = jnp.mean(v, axis=-1, keepdims=True)
    var = jnp.mean(jnp.square(v - mu), axis=-1, keepdims=True)
    return (v - mu) * lax.rsqrt(var + LN_EPS) * g + b


def _rms_norm(v, g):
    return v * lax.rsqrt(jnp.mean(jnp.square(v), axis=-1, keepdims=True) + RMS_EPS) * g


def _mixout_kernel(att_ref, ssm_ref, x_ref, wglu_ref, bglu_ref, ag_ref, sg_ref, wout_ref, g_ref, b_ref,
                   o_ref, *, alpha):
    y = jax.nn.gelu(ssm_ref[...])
    z = jnp.dot(y.astype(BF16), wglu_ref[...], preferred_element_type=F32) + bglu_ref[...]
    o_ssm = y * jax.nn.sigmoid(z)
    a = _rms_norm(att_ref[...], ag_ref[...]).astype(BF16)
    s = _rms_norm(o_ssm, sg_ref[...]).astype(BF16)
    wa = att_ref.shape[1]
    mix = (jnp.dot(a, wout_ref[:wa, :], preferred_element_type=F32)
           + jnp.dot(s, wout_ref[wa:, :], preferred_element_type=F32))
    o_ref[...] = _layer_norm(alpha * x_ref[...] + mix, g_ref[...], b_ref[...])


def _mixout(o_att, y_ssm, x2d, w_glu, b_glu, att_g, ssm_g, w_out, ln_g, ln_b, alpha):
    n_tok, d = x2d.shape
    wa = o_att.shape[1]
    ws = y_ssm.shape[1]
    tm = 256
    row = lambda w: pl.BlockSpec((tm, w), lambda i: (i, 0))
    full = lambda a: pl.BlockSpec(a.shape, lambda i: (0,) * a.ndim)
    args = (o_att, y_ssm, x2d, w_glu.astype(BF16), b_glu.reshape(1, ws), att_g.reshape(1, wa),
            ssm_g.reshape(1, ws), w_out.astype(BF16), ln_g.reshape(1, d), ln_b.reshape(1, d))
    return pl.pallas_call(
        functools.partial(_mixout_kernel, alpha=alpha),
        out_shape=jax.ShapeDtypeStruct((n_tok, d), F32),
        grid=(n_tok // tm,),
        in_specs=[row(wa), row(ws), row(d)] + [full(a) for a in args[3:]],
        out_specs=row(d),
        compiler_params=_cparams(("parallel",), 32),
        name="mixout",
    )(*args)


def _split_bf16(v):
    hi = v.astype(BF16)
    lo = (v - hi.astype(F32)).astype(BF16)
    return hi, lo


def _router_kernel(h_ref, wt_ref, bias_ref, e_ref, g_ref, r_ref, cnt_ref, run_ref):
    tm = h_ref.shape[0]
    n_exp = wt_ref.shape[0]
    gsz = n_exp // N_EXPERT_GROUPS

    @pl.when(pl.program_id(0) == 0)
    def _():
        run_ref[...] = jnp.zeros_like(run_ref)

    w_hi, w_lo = _split_bf16(wt_ref[...])
    h_hi, h_lo = _split_bf16(h_ref[...])
    nt = (((1,), (1,)), ((), ()))
    logits = (lax.dot_general(w_hi, h_hi, nt, preferred_element_type=F32)
              + lax.dot_general(w_hi, h_lo, nt, preferred_element_type=F32)
              + lax.dot_general(w_lo, h_hi, nt, preferred_element_type=F32))
    scores = jax.nn.sigmoid(logits)
    choice = scores + bias_ref[:, 0:1]

    gio = lax.broadcasted_iota(jnp.int32, (gsz, tm), 0).astype(F32)
    gscore = []
    for g in range(N_EXPERT_GROUPS):
        cg = choice[g * gsz:(g + 1) * gsz, :]
        m1 = jnp.max(cg, axis=0, keepdims=True)
        i1 = jnp.min(jnp.where(cg == m1, gio, float(gsz)), axis=0, keepdims=True)
        m2 = jnp.max(jnp.where(gio == i1, NEG_INF, cg), axis=0, keepdims=True)
        gscore.append(m1 + m2)
    masked = []
    for g in range(N_EXPERT_GROUPS):
        beat = jnp.zeros((1, tm), F32)
        for o in range(N_EXPERT_GROUPS):
            if o == g:
                continue
            wins = (gscore[o] >= gscore[g]) if o < g else (gscore[o] > gscore[g])
            beat = beat + jnp.where(wins, 1.0, 0.0)
        keep = beat < float(TOPK_GROUPS)
        masked.append(jnp.where(keep, choice[g * gsz:(g + 1) * gsz, :], NEG_INF))
    cur = jnp.concatenate(masked, axis=0)

    eio = lax.broadcasted_iota(jnp.int32, (n_exp, tm), 0).astype(F32)
    idxs = []
    gates = []
    onehot = jnp.zeros((n_exp, tm), F32)
    for _ in range(TOP_K):
        m = jnp.max(cur, axis=0, keepdims=True)
        idx = jnp.min(jnp.where(cur == m, eio, float(n_exp)), axis=0, keepdims=True)
        hit = eio == idx
        idxs.append(idx)
        gates.append(jnp.sum(jnp.where(hit, scores, 0.0), axis=0, keepdims=True))
        cur = jnp.where(hit, NEG_INF, cur)
        onehot = onehot + jnp.where(hit, 1.0, 0.0)
    gate = jnp.concatenate(gates, axis=0)
    gate = ROUTED_SCALE * gate / (jnp.sum(gate, axis=0, keepdims=True) + 1e-20)

    si = lax.broadcasted_iota(jnp.int32, (tm, tm), 0)
    ti = lax.broadcasted_iota(jnp.int32, (tm, tm), 1)
    upper = jnp.where(si < ti, 1.0, 0.0).astype(BF16)
    before = jnp.dot(onehot.astype(BF16), upper, preferred_element_type=F32) + run_ref[:, 0:1]
    ranks = [jnp.sum(jnp.where(eio == idx, before, 0.0), axis=0, keepdims=True) for idx in idxs]

    e_ref[...] = jnp.concatenate(idxs, axis=0).astype(jnp.int32)
    g_ref[...] = gate
    r_ref[...] = jnp.concatenate(ranks, axis=0).astype(jnp.int32)
    run_ref[...] = run_ref[...] + jnp.sum(onehot, axis=1, keepdims=True)
    cnt_ref[...] = run_ref[...]


def _router(h, router_w, router_bias):
    n_tok, d = h.shape
    n_exp = router_w.shape[1]
    tm = 256
    wt = router_w.astype(F32).T
    bias = jnp.broadcast_to(router_bias.astype(F32)[:, None], (n_exp, LANES))
    tok = pl.BlockSpec((TOP_K, tm), lambda i: (0, i))
    return pl.pallas_call(
        _router_kernel,
        out_shape=(jax.ShapeDtypeStruct((TOP_K, n_tok), jnp.int32),
                   jax.ShapeDtypeStruct((TOP_K, n_tok), F32),
                   jax.ShapeDtypeStruct((TOP_K, n_tok), jnp.int32),
                   jax.ShapeDtypeStruct((n_exp, LANES), F32)),
        grid=(n_tok // tm,),
        in_specs=[pl.BlockSpec((tm, d), lambda i: (i, 0)),
                  pl.BlockSpec((n_exp, d), lambda i: (0, 0)),
                  pl.BlockSpec((n_exp, LANES), lambda i: (0, 0))],
        out_specs=(tok, tok, tok, pl.BlockSpec((n_exp, LANES), lambda i: (0, 0))),
        scratch_shapes=[pltpu.VMEM((n_exp, LANES), F32)],
        compiler_params=_cparams(("arbitrary",), 32),
        name="router",
    )(h, wt, bias)


def _dest_kernel(e_ref, r_ref, st_ref, d_ref):
    n_exp = st_ref.shape[0]
    tm = e_ref.shape[1]
    eio = lax.broadcasted_iota(jnp.int32, (n_exp, tm), 0)
    start = st_ref[:, 0:1]
    rows = [jnp.sum(jnp.where(eio == e_ref[k:k + 1, :], start, 0.0), axis=0, keepdims=True)
            for k in range(TOP_K)]
    d_ref[...] = jnp.concatenate(rows, axis=0).astype(jnp.int32) + r_ref[...]


def _dest(top_e, rank, starts):
    n_tok = top_e.shape[1]
    n_exp = starts.shape[0]
    tm = 512
    st = jnp.broadcast_to(starts.astype(F32)[:, None], (n_exp, LANES))
    tok = pl.BlockSpec((TOP_K, tm), lambda i: (0, i))
    return pl.pallas_call(
        _dest_kernel,
        out_shape=jax.ShapeDtypeStruct((TOP_K, n_tok), jnp.int32),
        grid=(n_tok // tm,),
        in_specs=[tok, tok, pl.BlockSpec((n_exp, LANES), lambda i: (0, 0))],
        out_specs=tok,
        compiler_params=_cparams(("parallel",), 32),
        name="dest",
    )(top_e, rank, st)


def _pack_bf16_pairs(val):
    half = val.shape[1] // 2
    lo = pltpu.bitcast(val[:, :half].astype(BF16).astype(F32), U32)
    hi = pltpu.bitcast(val[:, half:].astype(BF16).astype(F32), U32)
    return (lo >> 16) | (hi & jnp.uint32(0xFFFF0000))


def _unpack_bf16_pairs(words):
    lo = pltpu.bitcast(words << 16, F32)
    hi = pltpu.bitcast(words & jnp.uint32(0xFFFF0000), F32)
    return jnp.concatenate([lo, hi], axis=1)


def _to_row_tiles(dst_ref, slot, val):
    rows = val.shape[0]
    words = _pack_bf16_pairs(val)
    for j in range(ROW_TILE):
        dst_ref[slot, pl.ds(j, rows, stride=ROW_TILE), :] = words[:, j * LANES:(j + 1) * LANES]


def _row_tile_words(src_ref, idx, rows):
    return jnp.concatenate([src_ref[(*idx, pl.ds(j, rows, stride=ROW_TILE), slice(None))]
                            for j in range(ROW_TILE)], axis=1)


def _row_tile(r):
    return pl.ds(pl.multiple_of(r * ROW_TILE, ROW_TILE), ROW_TILE)


def _dispatch_kernel(dest_ref, h_ref, xs_ref, ht_ref, sem):
    tm = h_ref.shape[0]
    i = pl.program_id(0)
    cur = i % 2
    _to_row_tiles(ht_ref, cur, h_ref[...])

    def issue(t, carry):
        for k in range(TOP_K):
            pltpu.make_async_copy(ht_ref.at[cur, _row_tile(t)], xs_ref.at[_row_tile(dest_ref[k, t])],
                                  sem.at[cur]).start(priority=k % 2)
        return carry
    lax.fori_loop(0, tm, issue, 0, unroll=2)

    def drain(slot):
        def body(t, carry):
            for k in range(TOP_K):
                pltpu.make_async_copy(ht_ref.at[slot, _row_tile(0)], xs_ref.at[_row_tile(0)],
                                      sem.at[slot]).wait()
            return carry
        lax.fori_loop(0, tm, body, 0)

    @pl.when(i > 0)
    def _():
        drain(1 - cur)

    @pl.when(i == pl.num_programs(0) - 1)
    def _():
        drain(cur)


def _dispatch(h, dest, n_exp):
    n_tok, d = h.shape
    assert d == 2 * ROW_TILE * LANES
    tm = 256
    n_rows = n_tok * TOP_K + n_exp * EXPERT_ROWS
    return pl.pallas_call(
        _dispatch_kernel,
        out_shape=jax.ShapeDtypeStruct((n_rows * ROW_TILE, LANES), U32),
        grid=(n_tok // tm,),
        in_specs=[pl.BlockSpec((TOP_K, tm), lambda i: (0, i), memory_space=pltpu.SMEM),
                  pl.BlockSpec((tm, d), lambda i: (i, 0))],
        out_specs=pl.BlockSpec(memory_space=pl.ANY),
        scratch_shapes=[pltpu.VMEM((2, tm * ROW_TILE, LANES), U32), pltpu.SemaphoreType.DMA((2,))],
        compiler_params=_cparams(("arbitrary",), 32),
        name="dispatch",
    )(dest, h)


def _experts_kernel(bstart_ref, bend_ref, cnt_ref, nblk_ref, xs_ref, wgu_ref, wdn_ref, ys_ref,
                    xbuf, ybuf, act_ref, wgu_bf, wdn_bf, xsem, ysem):
    e = pl.program_id(0)
    n_blk = nblk_ref[0]
    trows = xbuf.shape[1]
    rows = trows // ROW_TILE
    ff = wdn_bf.shape[0]
    b0 = bstart_ref[e]
    b1 = bend_ref[e]

    def block_rows(b):
        return pl.ds(pl.multiple_of(b * trows, trows), trows)

    def x_copy(b):
        slot = b % EXPERT_RING
        return pltpu.make_async_copy(xs_ref.at[block_rows(b)], xbuf.at[slot], xsem.at[slot])

    def y_copy(b):
        slot = b % EXPERT_OUT_RING
        return pltpu.make_async_copy(ybuf.at[slot], ys_ref.at[block_rows(b)], ysem.at[slot])

    @pl.when(e == 0)
    def _():
        for i in range(EXPERT_RING):
            @pl.when(i < n_blk)
            def _():
                x_copy(i).start(priority=1)

    wgu_bf[...] = wgu_ref[0].astype(BF16)
    wdn_bf[...] = wdn_ref[0].astype(BF16)

    def up(blocks):
        for b in blocks:
            x_copy(b).wait()
        for b in blocks:
            words = _row_tile_words(xbuf, (b % EXPERT_RING,), rows)
            row = lax.broadcasted_iota(jnp.int32, (rows, 1), 0)
            words = jnp.where(row < cnt_ref[e] - (b - b0) * rows, words, jnp.uint32(0))
            xb = _unpack_bf16_pairs(words).astype(BF16)
            gu = jnp.dot(xb, wgu_bf[...], preferred_element_type=F32)
            act_ref[b - b0] = (jax.nn.silu(gu[:, :ff]) * gu[:, ff:]).astype(BF16)
        for b in blocks:
            @pl.when(b + EXPERT_RING < n_blk)
            def _():
                x_copy(b + EXPERT_RING).start(priority=1)

    def up_pair(p, carry):
        up((b0 + 2 * p, b0 + 2 * p + 1))
        return carry
    n_mine = b1 - b0
    lax.fori_loop(0, n_mine // 2, up_pair, 0)

    @pl.when(n_mine % 2 == 1)
    def _():
        up((b1 - 1,))

    def down(blocks):
        for b in blocks:
            @pl.when(b >= EXPERT_OUT_RING)
            def _():
                y_copy(b - EXPERT_OUT_RING).wait()
        for b in blocks:
            _to_row_tiles(ybuf, b % EXPERT_OUT_RING,
                          jnp.dot(act_ref[b - b0], wdn_bf[...], preferred_element_type=F32))
        for b in blocks:
            y_copy(b).start(priority=1)
        last = blocks[-1]

        @pl.when(last == n_blk - 1)
        def _():
            for i in range(EXPERT_OUT_RING):
                @pl.when(last >= i)
                def _():
                    y_copy(last - i).wait()

    def down_pair(p, carry):
        down((b0 + 2 * p, b0 + 2 * p + 1))
        return carry
    lax.fori_loop(0, n_mine // 2, down_pair, 0)

    @pl.when(n_mine % 2 == 1)
    def _():
        down((b1 - 1,))


def _expert_blocks(counts):
    blocks = (counts + EXPERT_ROWS - 1) // EXPERT_ROWS
    bend = jnp.cumsum(blocks)
    bstart = bend - blocks
    i32 = lambda a: a.astype(jnp.int32)
    return i32(bstart), i32(bend), i32(bend[-1]).reshape(1), i32(bstart * EXPERT_ROWS)


def _experts(xs, bstart, bend, counts, n_blk, w_gu, w_down):
    n_exp, d, ff2 = w_gu.shape
    ff = w_down.shape[1]
    n_rows = xs.shape[0] // ROW_TILE
    assert n_rows % EXPERT_ROWS == 0 and d == 2 * ROW_TILE * LANES
    max_blocks = (n_rows - n_exp * EXPERT_ROWS) // TOP_K // EXPERT_ROWS + 1
    grid_spec = pltpu.PrefetchScalarGridSpec(
        num_scalar_prefetch=4,
        grid=(n_exp,),
        in_specs=[pl.BlockSpec(memory_space=pl.ANY),
                  pl.BlockSpec((1, d, ff2), lambda e, *_: (e, 0, 0)),
                  pl.BlockSpec((1, ff, d), lambda e, *_: (e, 0, 0))],
        out_specs=pl.BlockSpec(memory_space=pl.ANY),
        scratch_shapes=[pltpu.VMEM((EXPERT_RING, EXPERT_ROWS * ROW_TILE, LANES), U32),
                        pltpu.VMEM((EXPERT_OUT_RING, EXPERT_ROWS * ROW_TILE, LANES), U32),
                        pltpu.VMEM((max_blocks, EXPERT_ROWS, ff), BF16),
                        pltpu.VMEM((d, ff2), BF16), pltpu.VMEM((ff, d), BF16),
                        pltpu.SemaphoreType.DMA((EXPERT_RING,)),
                        pltpu.SemaphoreType.DMA((EXPERT_OUT_RING,))],
    )
    return pl.pallas_call(
        _experts_kernel,
        out_shape=jax.ShapeDtypeStruct(xs.shape, U32),
        grid_spec=grid_spec,
        compiler_params=_cparams(("arbitrary",), 40),
        name="experts",
    )(bstart, bend, counts, n_blk, xs, w_gu, w_down)


def _combine_kernel(dest_ref, dnext_ref, gate_ref, h_ref, ys_ref, wgu_ref, wdn_ref, g_ref, b_ref, o_ref,
                    buf_ref, sem, *, alpha):
    tm = h_ref.shape[0]
    i = pl.program_id(0)
    cur = i % 2

    def gather(d_ref, slot):
        def issue(t, carry):
            for k in range(TOP_K):
                pltpu.make_async_copy(ys_ref.at[_row_tile(d_ref[k, t])], buf_ref.at[slot, k, _row_tile(t)],
                                      sem.at[slot]).start(priority=k % 2)
            return carry
        lax.fori_loop(0, tm, issue, 0, unroll=2)

    @pl.when(i == 0)
    def _():
        gather(dest_ref, 0)

    @pl.when(i + 1 < pl.num_programs(0))
    def _():
        gather(dnext_ref, 1 - cur)

    h = h_ref[...]
    ff = wdn_ref.shape[0]
    gu = jnp.dot(h.astype(BF16), wgu_ref[...], preferred_element_type=F32)
    act = (jax.nn.silu(gu[:, :ff]) * gu[:, ff:]).astype(BF16)
    acc = alpha * h + jnp.dot(act, wdn_ref[...], preferred_element_type=F32)

    def drain(t, carry):
        for k in range(TOP_K):
            pltpu.make_async_copy(ys_ref.at[_row_tile(0)], buf_ref.at[cur, 0, _row_tile(0)], sem.at[cur]).wait()
        return carry
    lax.fori_loop(0, tm, drain, 0)

    gate = gate_ref[...]
    for k in range(TOP_K):
        acc = acc + gate[:, k:k + 1] * _unpack_bf16_pairs(_row_tile_words(buf_ref, (cur, k), tm))
    o_ref[...] = _layer_norm(acc, g_ref[...], b_ref[...])


def _combine(h, ys, dest, gate_t, shared_w_gu, shared_w_down, ln_g, ln_b, alpha):
    n_tok, d = h.shape
    tm = 256
    n_tiles = n_tok // tm
    full = lambda a: pl.BlockSpec(a.shape, lambda i: (0,) * a.ndim)
    wgu = shared_w_gu.astype(BF16)
    wdn = shared_w_down.astype(BF16)
    g2 = ln_g.reshape(1, d)
    b2 = ln_b.reshape(1, d)
    return pl.pallas_call(
        functools.partial(_combine_kernel, alpha=alpha),
        out_shape=jax.ShapeDtypeStruct((n_tok, d), F32),
        grid=(n_tiles,),
        in_specs=[pl.BlockSpec((TOP_K, tm), lambda i: (0, i), memory_space=pltpu.SMEM),
                  pl.BlockSpec((TOP_K, tm), lambda i: (0, jnp.minimum(i + 1, n_tiles - 1)),
                               memory_space=pltpu.SMEM),
                  pl.BlockSpec((tm, TOP_K), lambda i: (i, 0)),
                  pl.BlockSpec((tm, d), lambda i: (i, 0)),
                  pl.BlockSpec(memory_space=pl.ANY),
                  full(wgu), full(wdn), full(g2), full(b2)],
        out_specs=pl.BlockSpec((tm, d), lambda i: (i, 0)),
        scratch_shapes=[pltpu.VMEM((2, TOP_K, tm * ROW_TILE, LANES), U32), pltpu.SemaphoreType.DMA((2,))],
        compiler_params=_cparams(("arbitrary",), 48),
        name="combine",
    )(dest, dest, gate_t, h, ys, wgu, wdn, g2, b2)


def _moe(h, router_w, router_bias, w_gu, w_down, shared_w_gu, shared_w_down, ln_g, ln_b, alpha):
    top_e, gate, rank, cnt = _router(h, router_w, router_bias)
    counts = cnt[:, 0].astype(jnp.int32)
    bstart, bend, n_blk, pad_start = _expert_blocks(counts)
    dest = _dest(top_e, rank, pad_start)
    xs = _dispatch(h, dest, counts.shape[0])
    ys = _experts(xs, bstart, bend, counts, n_blk, w_gu, w_down)
    return _combine(h, ys, dest, gate.T, shared_w_gu, shared_w_down, ln_g, ln_b, alpha)


def kernel(x, w_in, att_norm_g, lam_re, lam_im, log_step, b_re, b_im, c_re, c_im, d_skip, w_glu, b_glu,
           ssm_norm_g, w_out, ln1_g, ln1_b, router_w, router_bias, w_gu, w_down, shared_w_gu,
           shared_w_down, ln2_g, ln2_b):
    bsz, seq, d = x.shape
    depth = w_in.shape[0]
    alpha = (2 * depth) ** 0.25
    h = x.reshape(bsz * seq, d)
    for i in range(depth):
        proj = _inproj(h, w_in[i].astype(BF16), seq)
        o_att = _attention(proj, bsz, seq)
        y_ssm = _s5(proj.reshape(bsz, seq, -1), 3 * ATT_WIDTH, lam_re[i], lam_im[i], log_step[i],
                    b_re[i], b_im[i], c_re[i], c_im[i], d_skip[i])
        h = _mixout(o_att, y_ssm.reshape(bsz * seq, -1), h, w_glu[i], b_glu[i], att_norm_g[i],
                    ssm_norm_g[i], w_out[i], ln1_g[i], ln1_b[i], alpha)
        h = _moe(h, router_w[i], router_bias[i], w_gu[i], w_down[i], shared_w_gu[i], shared_w_down[i],
                 ln2_g[i], ln2_b[i], alpha)
    return h.reshape(bsz, seq, d)
```

```python
import functools
import math

import jax
import jax.numpy as jnp
from jax import lax
from jax.experimental import pallas as pl
from jax.experimental.pallas import tpu as pltpu

F32 = jnp.float32
BF16 = jnp.bfloat16
U32 = jnp.uint32

ATT_HEADS = 8
HEAD_DIM = 64
ATT_WIDTH = ATT_HEADS * HEAD_DIM
SSM_CH = 16
SSM_STATE = 64
ROPE_THETA = 500000.0
ROT_DIM = HEAD_DIM // 4
DILATIONS = (1, 4, 16)
ATT_BLOCK = 128
ATT_GROUP = 8
N_EXPERTS = 256
TOP_K = 8
N_EXPERT_GROUPS = 8
TOPK_GROUPS = 4
ROUTED_SCALE = 2.5
LN_EPS = 1e-5
RMS_EPS = 1e-6

LANES = 128
SUBLANES = 8
EXPERT_ROWS = 128
EXPERT_RING = 8
EXPERT_OUT_RING = 8
ROW_TILE = 4
NEG_INF = float("-inf")


def _cparams(sem, vmem_mb):
    return pltpu.CompilerParams(dimension_semantics=sem, vmem_limit_bytes=vmem_mb * 1024 * 1024)


def _inproj_kernel(x_ref, w_ref, cos_ref, sa_ref, sb_ref, o_ref, *, n_rot_cols):
    xb = x_ref[...].astype(BF16)
    cosf = cos_ref[...]
    sa = sa_ref[...]
    sb = sb_ref[...]
    width = o_ref.shape[1]
    chunk = 512
    for c in range(width // chunk):
        r = jnp.dot(xb, w_ref[:, c * chunk:(c + 1) * chunk], preferred_element_type=F32)
        if c * chunk < n_rot_cols:
            parts = []
            for s in range(chunk // LANES):
                t = r[:, s * LANES:(s + 1) * LANES]
                parts.append(t * cosf + pltpu.roll(t, LANES - ROT_DIM // 2, 1) * sa
                             + pltpu.roll(t, ROT_DIM // 2, 1) * sb)
            r = jnp.concatenate(parts, axis=1)
        o_ref[:, c * chunk:(c + 1) * chunk] = r


def _rope_lane_tables(seq):
    half = ROT_DIM // 2
    inv_freq = jnp.power(jnp.float32(ROPE_THETA), -jnp.arange(half, dtype=F32) / half)
    ang = jnp.arange(seq, dtype=F32)[:, None] * inv_freq[None, :]
    cos, sin = jnp.cos(ang), jnp.sin(ang)
    rest = HEAD_DIM - ROT_DIM
    cos_h = jnp.concatenate([cos, cos, jnp.ones((seq, rest), F32)], axis=1)
    sa_h = jnp.concatenate([-sin, jnp.zeros((seq, half + rest), F32)], axis=1)
    sb_h = jnp.concatenate([jnp.zeros((seq, half), F32), sin, jnp.zeros((seq, rest), F32)], axis=1)
    rep = LANES // HEAD_DIM
    return tuple(jnp.tile(t, (1, rep)) for t in (cos_h, sa_h, sb_h))


def _inproj(x2d, w_in_bf, seq):
    n_tok, d = x2d.shape
    width = w_in_bf.shape[1]
    tm = 512
    cosf, sa, sb = _rope_lane_tables(seq)
    tab_spec = pl.BlockSpec((tm, LANES), lambda i: (i % (seq // tm), 0))
    return pl.pallas_call(
        functools.partial(_inproj_kernel, n_rot_cols=2 * ATT_WIDTH),
        out_shape=jax.ShapeDtypeStruct((n_tok, width), F32),
        grid=(n_tok // tm,),
        in_specs=[pl.BlockSpec((tm, d), lambda i: (i, 0)),
                  pl.BlockSpec((d, width), lambda i: (0, 0)),
                  tab_spec, tab_spec, tab_spec],
        out_specs=pl.BlockSpec((tm, width), lambda i: (i, 0)),
        compiler_params=_cparams(("parallel",), 48),
        name="inproj",
    )(x2d, w_in_bf, cosf, sa, sb)


def _attn_kernel(q_ref, k_ref, v_ref, o_ref, qs_ref, ks_ref, vs_ref, tmp_ref, ob_ref, lb_ref, band_ref,
                 first_ref, *, seq):
    blk = ATT_BLOCK
    lane = lax.broadcasted_iota(jnp.int32, (1, LANES), 1)
    head0 = lane < HEAD_DIM
    scale = HEAD_DIM ** -0.5
    d1, d2 = DILATIONS[1], DILATIONS[2]
    assert DILATIONS[0] == 1 and d2 == d1 * d1
    seg = seq // d1
    sub = seg // d1

    qi = lax.broadcasted_iota(jnp.int32, (blk, 2 * blk), 0)
    kj = lax.broadcasted_iota(jnp.int32, (blk, 2 * blk), 1)
    dist = qi + blk - kj
    band_ref[...] = jnp.where((dist >= 0) & (dist <= blk), 0.0, NEG_INF)
    first_ref[...] = jnp.where((dist >= 0) & (kj >= blk), 0.0, NEG_INF)

    n_class = (1, d1, d2)
    class_len = (seq, seg, sub)
    base = [0]
    for c in range(len(DILATIONS)):
        base.append(base[c] + n_class[c] * (class_len[c] + blk))

    def kv_row0(c, g):
        return base[c] + g * (class_len[c] + blk)

    qs_ref[0] = (q_ref[...] * scale).astype(BF16)
    for a in range(d1):
        x = q_ref[pl.ds(a, seg, stride=d1), :] * scale
        tmp_ref[a * seg:(a + 1) * seg, :] = x
        qs_ref[1, a * seg:(a + 1) * seg, :] = x.astype(BF16)
    for g in range(d2):
        qs_ref[2, g * sub:(g + 1) * sub, :] = tmp_ref[pl.ds((g // d1) * seg + g % d1, sub, stride=d1),
                                                      :].astype(BF16)
    for src_ref, dst_ref in ((k_ref, ks_ref), (v_ref, vs_ref)):
        for c in range(len(DILATIONS)):
            for g in range(n_class[c]):
                dst_ref[kv_row0(c, g):kv_row0(c, g) + blk, :] = jnp.zeros((blk, LANES), BF16)
        dst_ref[kv_row0(0, 0) + blk:kv_row0(0, 0) + blk + seq, :] = src_ref[...].astype(BF16)
        for a in range(d1):
            x = src_ref[pl.ds(a, seg, stride=d1), :]
            tmp_ref[a * seg:(a + 1) * seg, :] = x
            dst_ref[kv_row0(1, a) + blk:kv_row0(1, a) + blk + seg, :] = x.astype(BF16)
        for g in range(d2):
            dst_ref[kv_row0(2, g) + blk:kv_row0(2, g) + blk + sub, :] = tmp_ref[
                pl.ds((g // d1) * seg + g % d1, sub, stride=d1), :].astype(BF16)

    def one_block(c, g, n, out_rows, bias_ref):
        q = qs_ref[c, pl.ds(aligned(g * class_len[c] + n * blk), blk), :]
        kv_rows = pl.ds(aligned(kv_row0(c, g) + n * blk), 2 * blk)
        kk = ks_ref[kv_rows, :]
        vv = vs_ref[kv_rows, :]
        outs = []
        lses = []
        for h in range(LANES // HEAD_DIM):
            hm = head0 if h == 0 else jnp.logical_not(head0)
            qh = jnp.where(hm, q, jnp.zeros_like(q))
            s = lax.dot_general(qh, kk, (((1,), (1,)), ((), ())), preferred_element_type=F32)
            s = s + bias_ref[...]
            m = jnp.max(s, axis=-1, keepdims=True)
            p = jnp.exp(s - m)
            den = jnp.sum(p, axis=-1, keepdims=True)
            outs.append(jnp.dot(p.astype(BF16), vv, preferred_element_type=F32) / den)
            lses.append(m + jnp.log(den))
        ob_ref[c, out_rows, :] = jnp.where(head0, outs[0], outs[1])
        lb_ref[c, out_rows, :] = jnp.where(head0, lses[0], lses[1])

    def run_blocks(n_blocks, fn):
        group = max(g for g in range(1, ATT_GROUP + 1) if n_blocks % g == 0)
        if n_blocks == group:
            for g in range(group):
                fn(g)
            return
        def body(it, carry):
            for g in range(group):
                fn(it * group + g)
            return carry
        lax.fori_loop(0, n_blocks // group, body, 0)

    def aligned(x):
        return x if isinstance(x, int) else pl.multiple_of(x, blk)

    one_block(0, 0, 0, pl.ds(0, blk), first_ref)
    run_blocks(seq // blk - 1,
               lambda i: one_block(0, 0, i + 1, pl.ds(aligned((i + 1) * blk), blk), band_ref))

    nb1 = seg // blk
    run_blocks(d1, lambda a: one_block(1, a, 0, pl.ds(a, blk, stride=d1), first_ref))
    def later1(i):
        a = i // (nb1 - 1)
        n = i - a * (nb1 - 1) + 1
        one_block(1, a, n, pl.ds(a + n * (d1 * blk), blk, stride=d1), band_ref)
    run_blocks(d1 * (nb1 - 1), later1)

    assert sub == blk
    def only2(g):
        a = g // d1
        one_block(2, g, 0, pl.ds(a + d1 * (g - a * d1), blk, stride=d2), first_ref)
    run_blocks(d2, only2)

    rc = 256
    def merge(i, carry):
        sl = pl.ds(pl.multiple_of(i * rc, rc), rc)
        l0 = lb_ref[0, sl, :]
        l1 = lb_ref[1, sl, :]
        l2 = lb_ref[2, sl, :]
        mx = jnp.maximum(jnp.maximum(l0, l1), l2)
        e0 = jnp.exp(l0 - mx)
        e1 = jnp.exp(l1 - mx)
        e2 = jnp.exp(l2 - mx)
        tot = e0 + e1 + e2
        o_ref[sl, :] = ((e0 / tot) * ob_ref[0, sl, :] + (e1 / tot) * ob_ref[1, sl, :]
                        + (e2 / tot) * ob_ref[2, sl, :])
        return carry
    lax.fori_loop(0, seq // rc, merge, 0)


def _attention(proj, bsz, seq):
    n_tok = proj.shape[0]
    pairs = ATT_WIDTH // LANES
    assert seq % (ATT_BLOCK * max(DILATIONS)) == 0
    kv_rows = sum(seq + d * ATT_BLOCK for d in DILATIONS)
    blk = (seq, LANES)
    return pl.pallas_call(
        functools.partial(_attn_kernel, seq=seq),
        out_shape=jax.ShapeDtypeStruct((n_tok, ATT_WIDTH), F32),
        grid=(bsz, pairs),
        in_specs=[pl.BlockSpec(blk, lambda b, h: (b, h)),
                  pl.BlockSpec(blk, lambda b, h: (b, pairs + h)),
                  pl.BlockSpec(blk, lambda b, h: (b, 2 * pairs + h))],
        out_specs=pl.BlockSpec(blk, lambda b, h: (b, h)),
        scratch_shapes=[pltpu.VMEM((len(DILATIONS), seq, LANES), BF16),
                        pltpu.VMEM((kv_rows, LANES), BF16),
                        pltpu.VMEM((kv_rows, LANES), BF16),
                        pltpu.VMEM((seq, LANES), F32),
                        pltpu.VMEM((len(DILATIONS), seq, LANES), F32),
                        pltpu.VMEM((len(DILATIONS), seq, LANES), F32),
                        pltpu.VMEM((ATT_BLOCK, 2 * ATT_BLOCK), F32),
                        pltpu.VMEM((ATT_BLOCK, 2 * ATT_BLOCK), F32)],
        compiler_params=_cparams(("parallel", "parallel"), 40),
        name="attn",
    )(proj, proj, proj)


def _s5_kernel(u_ref, bm_ref, lam_ref, cm_ref, dk_ref, o_ref, us_ref, st_ref, ys_ref, carry_ref, *, tc):
    bsz = u_ref.shape[0]
    half = st_ref.shape[1] // 2
    rows = tc * bsz
    mm_rows = 512

    @pl.when(pl.program_id(1) == 0)
    def _():
        carry_ref[...] = jnp.zeros_like(carry_ref)

    for b in range(bsz):
        us_ref[pl.ds(b, tc, stride=bsz), :] = u_ref[b]

    bm = bm_ref[0]
    for r0 in range(0, rows, mm_rows):
        st_ref[r0:r0 + mm_rows, :] = jnp.dot(us_ref[r0:r0 + mm_rows, :].astype(BF16), bm,
                                             preferred_element_type=F32)

    lam = lam_ref[0]
    lam_re = lam[:, :half]
    lam_im = lam[:, half:]

    def step(t, carry):
        xr, xi = carry
        sl = pl.ds(pl.multiple_of(t * bsz, bsz), bsz)
        nr = lam_re * xr - lam_im * xi + st_ref[sl, :half]
        ni = lam_re * xi + lam_im * xr + st_ref[sl, half:]
        st_ref[sl, :half] = nr
        st_ref[sl, half:] = ni
        return nr, ni

    xr, xi = lax.fori_loop(0, tc, step, (carry_ref[:, :half], carry_ref[:, half:]), unroll=4)
    carry_ref[:, :half] = xr
    carry_ref[:, half:] = xi

    cm = cm_ref[0]
    for r0 in range(0, rows, mm_rows):
        ys_ref[r0:r0 + mm_rows, :] = jnp.dot(st_ref[r0:r0 + mm_rows, :].astype(BF16), cm,
                                             preferred_element_type=F32)
    dk = dk_ref[...]
    for b in range(bsz):
        o_ref[b] = ys_ref[pl.ds(b, tc, stride=bsz), :] + dk * u_ref[b]


def _s5_params(lam_re, lam_im, log_step, b_re, b_im, c_re, c_im, bsz):
    groups = lam_re.shape[0]
    gpc = LANES // SSM_CH
    n_chunks = groups // gpc
    lam = lax.complex(lam_re.astype(F32), lam_im.astype(F32))
    step = jnp.exp(log_step.astype(F32))[:, None]
    lam_bar = jnp.exp(lam * step)
    bmat = lax.complex(b_re.astype(F32), b_im.astype(F32))
    b_bar = ((lam_bar - 1.0) / lam)[..., None] * bmat
    eye = jnp.eye(gpc, dtype=F32)

    def block_diag_in(t):
        t = t.reshape(n_chunks, gpc, SSM_STATE, SSM_CH)
        return jnp.einsum('ngpc,gh->ngchp', t, eye).reshape(n_chunks, gpc * SSM_CH, gpc * SSM_STATE)

    def block_diag_out(t):
        t = t.reshape(n_chunks, gpc, SSM_CH, SSM_STATE)
        return jnp.einsum('ngcp,gh->ngphc', t, eye).reshape(n_chunks, gpc * SSM_STATE, gpc * SSM_CH)

    bm = jnp.concatenate([block_diag_in(b_bar.real), block_diag_in(b_bar.imag)], axis=2).astype(BF16)
    cm = jnp.concatenate([block_diag_out(c_re.astype(F32)), block_diag_out(-c_im.astype(F32))],
                         axis=1).astype(BF16)
    lam_row = jnp.concatenate([lam_bar.real.reshape(n_chunks, gpc * SSM_STATE),
                               lam_bar.imag.reshape(n_chunks, gpc * SSM_STATE)], axis=1)
    lam_t = jnp.broadcast_to(lam_row[:, None, :], (n_chunks, bsz, 2 * gpc * SSM_STATE))
    return bm, lam_t, cm, n_chunks


def _s5(proj3, u_col0, lam_re, lam_im, log_step, b_re, b_im, c_re, c_im, d_skip):
    bsz, seq, _ = proj3.shape
    assert bsz == SUBLANES
    bm, lam_t, cm, n_chunks = _s5_params(lam_re, lam_im, log_step, b_re, b_im, c_re, c_im, bsz)
    width = n_chunks * LANES
    tc = 256
    st_cols = bm.shape[2]
    ublk0 = u_col0 // LANES
    return pl.pallas_call(
        functools.partial(_s5_kernel, tc=tc),
        out_shape=jax.ShapeDtypeStruct((bsz, seq, width), F32),
        grid=(n_chunks, seq // tc),
        in_specs=[pl.BlockSpec((bsz, tc, LANES), lambda c, t: (0, t, ublk0 + c)),
                  pl.BlockSpec((1, LANES, st_cols), lambda c, t: (c, 0, 0)),
                  pl.BlockSpec((1, bsz, st_cols), lambda c, t: (c, 0, 0)),
                  pl.BlockSpec((1, st_cols, LANES), lambda c, t: (c, 0, 0)),
                  pl.BlockSpec((1, LANES), lambda c, t: (0, c))],
        out_specs=pl.BlockSpec((bsz, tc, LANES), lambda c, t: (0, t, c)),
        scratch_shapes=[pltpu.VMEM((tc * bsz, LANES), F32),
                        pltpu.VMEM((tc * bsz, st_cols), F32),
                        pltpu.VMEM((tc * bsz, LANES), F32),
                        pltpu.VMEM((bsz, st_cols), F32)],
        compiler_params=_cparams(("arbitrary", "arbitrary"), 40),
        name="s5",
    )(proj3, bm, lam_t, cm, d_skip.reshape(1, width).astype(F32))


def _layer_norm(v, g, b):
    mu = jnp.mean(v, axis=-1, keepdims=True)
    var = jnp.mean(jnp.square(v - mu), axis=-1, keepdims=True)
    return (v - mu) * lax.rsqrt(var + LN_EPS) * g + b


def _rms_norm(v, g):
    return v * lax.rsqrt(jnp.mean(jnp.square(v), axis=-1, keepdims=True) + RMS_EPS) * g


def _mixout_kernel(att_ref, ssm_ref, x_ref, wglu_ref, bglu_ref, ag_ref, sg_ref, wout_ref, g_ref, b_ref,
                   o_ref, *, alpha):
    y = jax.nn.gelu(ssm_ref[...])
    z = jnp.dot(y.astype(BF16), wglu_ref[...], preferred_element_type=F32) + bglu_ref[...]
    o_ssm = y * jax.nn.sigmoid(z)
    a = _rms_norm(att_ref[...], ag_ref[...]).astype(BF16)
    s = _rms_norm(o_ssm, sg_ref[...]).astype(BF16)
    wa = att_ref.shape[1]
    mix = (jnp.dot(a, wout_ref[:wa, :], preferred_element_type=F32)
           + jnp.dot(s, wout_ref[wa:, :], preferred_element_type=F32))
    o_ref[...] = _layer_norm(alpha * x_ref[...] + mix, g_ref[...], b_ref[...])


def _mixout(o_att, y_ssm, x2d, w_glu, b_glu, att_g, ssm_g, w_out, ln_g, ln_b, alpha):
    n_tok, d = x2d.shape
    wa = o_att.shape[1]
    ws = y_ssm.shape[1]
    tm = 256
    row = lambda w: pl.BlockSpec((tm, w), lambda i: (i, 0))
    full = lambda a: pl.BlockSpec(a.shape, lambda i: (0,) * a.ndim)
    args = (o_att, y_ssm, x2d, w_glu.astype(BF16), b_glu.reshape(1, ws), att_g.reshape(1, wa),
            ssm_g.reshape(1, ws), w_out.astype(BF16), ln_g.reshape(1, d), ln_b.reshape(1, d))
    return pl.pallas_call(
        functools.partial(_mixout_kernel, alpha=alpha),
        out_shape=jax.ShapeDtypeStruct((n_tok, d), F32),
        grid=(n_tok // tm,),
        in_specs=[row(wa), row(ws), row(d)] + [full(a) for a in args[3:]],
        out_specs=row(d),
        compiler_params=_cparams(("parallel",), 32),
        name="mixout",
    )(*args)


def _split_bf16(v):
    hi = v.astype(BF16)
    lo = (v - hi.astype(F32)).astype(BF16)
    return hi, lo


def _router_kernel(h_ref, wt_ref, bias_ref, e_ref, g_ref, r_ref, cnt_ref, run_ref):
    tm = h_ref.shape[0]
    n_exp = wt_ref.shape[0]
    gsz = n_exp // N_EXPERT_GROUPS

    @pl.when(pl.program_id(0) == 0)
    def _():
        run_ref[...] = jnp.zeros_like(run_ref)

    w_hi, w_lo = _split_bf16(wt_ref[...])
    h_hi, h_lo = _split_bf16(h_ref[...])
    nt = (((1,), (1,)), ((), ()))
    logits = (lax.dot_general(w_hi, h_hi, nt, preferred_element_type=F32)
              + lax.dot_general(w_hi, h_lo, nt, preferred_element_type=F32)
              + lax.dot_general(w_lo, h_hi, nt, preferred_element_type=F32))
    scores = jax.nn.sigmoid(logits)
    choice = scores + bias_ref[:, 0:1]

    gio = lax.broadcasted_iota(jnp.int32, (gsz, tm), 0).astype(F32)
    gscore = []
    for g in range(N_EXPERT_GROUPS):
        cg = choice[g * gsz:(g + 1) * gsz, :]
        m1 = jnp.max(cg, axis=0, keepdims=True)
        i1 = jnp.min(jnp.where(cg == m1, gio, float(gsz)), axis=0, keepdims=True)
        m2 = jnp.max(jnp.where(gio == i1, NEG_INF, cg), axis=0, keepdims=True)
        gscore.append(m1 + m2)
    masked = []
    for g in range(N_EXPERT_GROUPS):
        beat = jnp.zeros((1, tm), F32)
        for o in range(N_EXPERT_GROUPS):
            if o == g:
                continue
            wins = (gscore[o] >= gscore[g]) if o < g else (gscore[o] > gscore[g])
            beat = beat + jnp.where(wins, 1.0, 0.0)
        keep = beat < float(TOPK_GROUPS)
        masked.append(jnp.where(keep, choice[g * gsz:(g + 1) * gsz, :], NEG_INF))
    cur = jnp.concatenate(masked, axis=0)

    eio = lax.broadcasted_iota(jnp.int32, (n_exp, tm), 0).astype(F32)
    idxs = []
    gates = []
    onehot = jnp.zeros((n_exp, tm), F32)
    for _ in range(TOP_K):
        m = jnp.max(cur, axis=0, keepdims=True)
        idx = jnp.min(jnp.where(cur == m, eio, float(n_exp)), axis=0, keepdims=True)
        hit = eio == idx
        idxs.append(idx)
        gates.append(jnp.sum(jnp.where(hit, scores, 0.0), axis=0, keepdims=True))
        cur = jnp.where(hit, NEG_INF, cur)
        onehot = onehot + jnp.where(hit, 1.0, 0.0)
    gate = jnp.concatenate(gates, axis=0)
    gate = ROUTED_SCALE * gate / (jnp.sum(gate, axis=0, keepdims=True) + 1e-20)

    si = lax.broadcasted_iota(jnp.int32, (tm, tm), 0)
    ti = lax.broadcasted_iota(jnp.int32, (tm, tm), 1)
    upper = jnp.where(si < ti, 1.0, 0.0).astype(BF16)
    before = jnp.dot(onehot.astype(BF16), upper, preferred_element_type=F32) + run_ref[:, 0:1]
    ranks = [jnp.sum(jnp.where(eio == idx, before, 0.0), axis=0, keepdims=True) for idx in idxs]

    e_ref[...] = jnp.concatenate(idxs, axis=0).astype(jnp.int32)
    g_ref[...] = gate
    r_ref[...] = jnp.concatenate(ranks, axis=0).astype(jnp.int32)
    run_ref[...] = run_ref[...] + jnp.sum(onehot, axis=1, keepdims=True)
    cnt_ref[...] = run_ref[...]


def _router(h, router_w, router_bias):
    n_tok, d = h.shape
    n_exp = router_w.shape[1]
    tm = 256
    wt = router_w.astype(F32).T
    bias = jnp.broadcast_to(router_bias.astype(F32)[:, None], (n_exp, LANES))
    tok = pl.BlockSpec((TOP_K, tm), lambda i: (0, i))
    return pl.pallas_call(
        _router_kernel,
        out_shape=(jax.ShapeDtypeStruct((TOP_K, n_tok), jnp.int32),
                   jax.ShapeDtypeStruct((TOP_K, n_tok), F32),
                   jax.ShapeDtypeStruct((TOP_K, n_tok), jnp.int32),
                   jax.ShapeDtypeStruct((n_exp, LANES), F32)),
        grid=(n_tok // tm,),
        in_specs=[pl.BlockSpec((tm, d), lambda i: (i, 0)),
                  pl.BlockSpec((n_exp, d), lambda i: (0, 0)),
                  pl.BlockSpec((n_exp, LANES), lambda i: (0, 0))],
        out_specs=(tok, tok, tok, pl.BlockSpec((n_exp, LANES), lambda i: (0, 0))),
        scratch_shapes=[pltpu.VMEM((n_exp, LANES), F32)],
        compiler_params=_cparams(("arbitrary",), 32),
        name="router",
    )(h, wt, bias)


def _dest_kernel(e_ref, r_ref, st_ref, d_ref):
    n_exp = st_ref.shape[0]
    tm = e_ref.shape[1]
    eio = lax.broadcasted_iota(jnp.int32, (n_exp, tm), 0)
    start = st_ref[:, 0:1]
    rows = [jnp.sum(jnp.where(eio == e_ref[k:k + 1, :], start, 0.0), axis=0, keepdims=True)
            for k in range(TOP_K)]
    d_ref[...] = jnp.concatenate(rows, axis=0).astype(jnp.int32) + r_ref[...]


def _dest(top_e, rank, starts):
    n_tok = top_e.shape[1]
    n_exp = starts.shape[0]
    tm = 512
    st = jnp.broadcast_to(starts.astype(F32)[:, None], (n_exp, LANES))
    tok = pl.BlockSpec((TOP_K, tm), lambda i: (0, i))
    return pl.pallas_call(
        _dest_kernel,
        out_shape=jax.ShapeDtypeStruct((TOP_K, n_tok), jnp.int32),
        grid=(n_tok // tm,),
        in_specs=[tok, tok, pl.BlockSpec((n_exp, LANES), lambda i: (0, 0))],
        out_specs=tok,
        compiler_params=_cparams(("parallel",), 32),
        name="dest",
    )(top_e, rank, st)


def _pack_bf16_pairs(val):
    half = val.shape[1] // 2
    lo = pltpu.bitcast(val[:, :half].astype(BF16).astype(F32), U32)
    hi = pltpu.bitcast(val[:, half:].astype(BF16).astype(F32), U32)
    return (lo >> 16) | (hi & jnp.uint32(0xFFFF0000))


def _unpack_bf16_pairs(words):
    lo = pltpu.bitcast(words << 16, F32)
    hi = pltpu.bitcast(words & jnp.uint32(0xFFFF0000), F32)
    return jnp.concatenate([lo, hi], axis=1)


def _to_row_tiles(dst_ref, slot, val):
    rows = val.shape[0]
    words = _pack_bf16_pairs(val)
    for j in range(ROW_TILE):
        dst_ref[slot, pl.ds(j, rows, stride=ROW_TILE), :] = words[:, j * LANES:(j + 1) * LANES]


def _row_tile_words(src_ref, idx, rows):
    return jnp.concatenate([src_ref[(*idx, pl.ds(j, rows, stride=ROW_TILE), slice(None))]
                            for j in range(ROW_TILE)], axis=1)


def _row_tile(r):
    return pl.ds(pl.multiple_of(r * ROW_TILE, ROW_TILE), ROW_TILE)


def _dispatch_kernel(dest_ref, h_ref, xs_ref, ht_ref, sem):
    tm = h_ref.shape[0]
    i = pl.program_id(0)
    cur = i % 2
    _to_row_tiles(ht_ref, cur, h_ref[...])

    def issue(t, carry):
        for k in range(TOP_K):
            pltpu.make_async_copy(ht_ref.at[cur, _row_tile(t)], xs_ref.at[_row_tile(dest_ref[k, t])],
                                  sem.at[cur]).start(priority=k % 2)
        return carry
    lax.fori_loop(0, tm, issue, 0, unroll=2)

    def drain(slot):
        def body(t, carry):
            for k in range(TOP_K):
                pltpu.make_async_copy(ht_ref.at[slot, _row_tile(0)], xs_ref.at[_row_tile(0)],
                                      sem.at[slot]).wait()
            return carry
        lax.fori_loop(0, tm, body, 0)

    @pl.when(i > 0)
    def _():
        drain(1 - cur)

    @pl.when(i == pl.num_programs(0) - 1)
    def _():
        drain(cur)


def _dispatch(h, dest, n_exp):
    n_tok, d = h.shape
    assert d == 2 * ROW_TILE * LANES
    tm = 256
    n_rows = n_tok * TOP_K + n_exp * EXPERT_ROWS
    return pl.pallas_call(
        _dispatch_kernel,
        out_shape=jax.ShapeDtypeStruct((n_rows * ROW_TILE, LANES), U32),
        grid=(n_tok // tm,),
        in_specs=[pl.BlockSpec((TOP_K, tm), lambda i: (0, i), memory_space=pltpu.SMEM),
                  pl.BlockSpec((tm, d), lambda i: (i, 0))],
        out_specs=pl.BlockSpec(memory_space=pl.ANY),
        scratch_shapes=[pltpu.VMEM((2, tm * ROW_TILE, LANES), U32), pltpu.SemaphoreType.DMA((2,))],
        compiler_params=_cparams(("arbitrary",), 32),
        name="dispatch",
    )(dest, h)


def _experts_kernel(bstart_ref, bend_ref, cnt_ref, nblk_ref, xs_ref, wgu_ref, wdn_ref, ys_ref,
                    xbuf, ybuf, act_ref, wgu_bf, wdn_bf, xsem, ysem):
    e = pl.program_id(0)
    n_blk = nblk_ref[0]
    trows = xbuf.shape[1]
    rows = trows // ROW_TILE
    ff = wdn_bf.shape[0]
    b0 = bstart_ref[e]
    b1 = bend_ref[e]

    def block_rows(b):
        return pl.ds(pl.multiple_of(b * trows, trows), trows)

    def x_copy(b):
        slot = b % EXPERT_RING
        return pltpu.make_async_copy(xs_ref.at[block_rows(b)], xbuf.at[slot], xsem.at[slot])

    def y_copy(b):
        slot = b % EXPERT_OUT_RING
        return pltpu.make_async_copy(ybuf.at[slot], ys_ref.at[block_rows(b)], ysem.at[slot])

    @pl.when(e == 0)
    def _():
        for i in range(EXPERT_RING):
            @pl.when(i < n_blk)
            def _():
                x_copy(i).start(priority=1)

    wgu_bf[...] = wgu_ref[0].astype(BF16)
    wdn_bf[...] = wdn_ref[0].astype(BF16)

    def up(blocks):
        for b in blocks:
            x_copy(b).wait()
        for b in blocks:
            words = _row_tile_words(xbuf, (b % EXPERT_RING,), rows)
            row = lax.broadcasted_iota(jnp.int32, (rows, 1), 0)
            words = jnp.where(row < cnt_ref[e] - (b - b0) * rows, words, jnp.uint32(0))
            xb = _unpack_bf16_pairs(words).astype(BF16)
            gu = jnp.dot(xb, wgu_bf[...], preferred_element_type=F32)
            act_ref[b - b0] = (jax.nn.silu(gu[:, :ff]) * gu[:, ff:]).astype(BF16)
        for b in blocks:
            @pl.when(b + EXPERT_RING < n_blk)
            def _():
                x_copy(b + EXPERT_RING).start(priority=1)

    def up_pair(p, carry):
        up((b0 + 2 * p, b0 + 2 * p + 1))
        return carry
    n_mine = b1 - b0
    lax.fori_loop(0, n_mine // 2, up_pair, 0)

    @pl.when(n_mine % 2 == 1)
    def _():
        up((b1 - 1,))

    def down(blocks):
        for b in blocks:
            @pl.when(b >= EXPERT_OUT_RING)
            def _():
                y_copy(b - EXPERT_OUT_RING).wait()
        for b in blocks:
            _to_row_tiles(ybuf, b % EXPERT_OUT_RING,
                          jnp.dot(act_ref[b - b0], wdn_bf[...], preferred_element_type=F32))
        for b in blocks:
            y_copy(b).start(priority=1)
        last = blocks[-1]

        @pl.when(last == n_blk - 1)
        def _():
            for i in range(EXPERT_OUT_RING):
                @pl.when(last >= i)
                def _():
                    y_copy(last - i).wait()

    def down_pair(p, carry):
        down((b0 + 2 * p, b0 + 2 * p + 1))
        return carry
    lax.fori_loop(0, n_mine // 2, down_pair, 0)

    @pl.when(n_mine % 2 == 1)
    def _():
        down((b1 - 1,))


def _expert_blocks(counts):
    blocks = (counts + EXPERT_ROWS - 1) // EXPERT_ROWS
    bend = jnp.cumsum(blocks)
    bstart = bend - blocks
    i32 = lambda a: a.astype(jnp.int32)
    return i32(bstart), i32(bend), i32(bend[-1]).reshape(1), i32(bstart * EXPERT_ROWS)


def _experts(xs, bstart, bend, counts, n_blk, w_gu, w_down):
    n_exp, d, ff2 = w_gu.shape
    ff = w_down.shape[1]
    n_rows = xs.shape[0] // ROW_TILE
    assert n_rows % EXPERT_ROWS == 0 and d == 2 * ROW_TILE * LANES
    max_blocks = (n_rows - n_exp * EXPERT_ROWS) // TOP_K // EXPERT_ROWS + 1
    grid_spec = pltpu.PrefetchScalarGridSpec(
        num_scalar_prefetch=4,
        grid=(n_exp,),
        in_specs=[pl.BlockSpec(memory_space=pl.ANY),
                  pl.BlockSpec((1, d, ff2), lambda e, *_: (e, 0, 0)),
                  pl.BlockSpec((1, ff, d), lambda e, *_: (e, 0, 0))],
        out_specs=pl.BlockSpec(memory_space=pl.ANY),
        scratch_shapes=[pltpu.VMEM((EXPERT_RING, EXPERT_ROWS * ROW_TILE, LANES), U32),
                        pltpu.VMEM((EXPERT_OUT_RING, EXPERT_ROWS * ROW_TILE, LANES), U32),
                        pltpu.VMEM((max_blocks, EXPERT_ROWS, ff), BF16),
                        pltpu.VMEM((d, ff2), BF16), pltpu.VMEM((ff, d), BF16),
                        pltpu.SemaphoreType.DMA((EXPERT_RING,)),
                        pltpu.SemaphoreType.DMA((EXPERT_OUT_RING,))],
    )
    return pl.pallas_call(
        _experts_kernel,
        out_shape=jax.ShapeDtypeStruct(xs.shape, U32),
        grid_spec=grid_spec,
        compiler_params=_cparams(("arbitrary",), 40),
        name="experts",
    )(bstart, bend, counts, n_blk, xs, w_gu, w_down)


def _combine_kernel(dest_ref, dnext_ref, gate_ref, h_ref, ys_ref, wgu_ref, wdn_ref, g_ref, b_ref, o_ref,
                    buf_ref, sem, *, alpha):
    tm = h_ref.shape[0]
    i = pl.program_id(0)
    cur = i % 2

    def gather(d_ref, slot):
        def issue(t, carry):
            for k in range(TOP_K):
                pltpu.make_async_copy(ys_ref.at[_row_tile(d_ref[k, t])], buf_ref.at[slot, k, _row_tile(t)],
                                      sem.at[slot]).start(priority=k % 2)
            return carry
        lax.fori_loop(0, tm, issue, 0, unroll=2)

    @pl.when(i == 0)
    def _():
        gather(dest_ref, 0)

    @pl.when(i + 1 < pl.num_programs(0))
    def _():
        gather(dnext_ref, 1 - cur)

    h = h_ref[...]
    ff = wdn_ref.shape[0]
    gu = jnp.dot(h.astype(BF16), wgu_ref[...], preferred_element_type=F32)
    act = (jax.nn.silu(gu[:, :ff]) * gu[:, ff:]).astype(BF16)
    acc = alpha * h + jnp.dot(act, wdn_ref[...], preferred_element_type=F32)

    def drain(t, carry):
        for k in range(TOP_K):
            pltpu.make_async_copy(ys_ref.at[_row_tile(0)], buf_ref.at[cur, 0, _row_tile(0)], sem.at[cur]).wait()
        return carry
    lax.fori_loop(0, tm, drain, 0)

    gate = gate_ref[...]
    for k in range(TOP_K):
        acc = acc + gate[:, k:k + 1] * _unpack_bf16_pairs(_row_tile_words(buf_ref, (cur, k), tm))
    o_ref[...] = _layer_norm(acc, g_ref[...], b_ref[...])


def _combine(h, ys, dest, gate_t, shared_w_gu, shared_w_down, ln_g, ln_b, alpha):
    n_tok, d = h.shape
    tm = 256
    n_tiles = n_tok // tm
    full = lambda a: pl.BlockSpec(a.shape, lambda i: (0,) * a.ndim)
    wgu = shared_w_gu.astype(BF16)
    wdn = shared_w_down.astype(BF16)
    g2 = ln_g.reshape(1, d)
    b2 = ln_b.reshape(1, d)
    return pl.pallas_call(
        functools.partial(_combine_kernel, alpha=alpha),
        out_shape=jax.ShapeDtypeStruct((n_tok, d), F32),
        grid=(n_tiles,),
        in_specs=[pl.BlockSpec((TOP_K, tm), lambda i: (0, i), memory_space=pltpu.SMEM),
                  pl.BlockSpec((TOP_K, tm), lambda i: (0, jnp.minimum(i + 1, n_tiles - 1)),
                               memory_space=pltpu.SMEM),
                  pl.BlockSpec((tm, TOP_K), lambda i: (i, 0)),
                  pl.BlockSpec((tm, d), lambda i: (i, 0)),
                  pl.BlockSpec(memory_space=pl.ANY),
                  full(wgu), full(wdn), full(g2), full(b2)],
        out_specs=pl.BlockSpec((tm, d), lambda i: (i, 0)),
        scratch_shapes=[pltpu.VMEM((2, TOP_K, tm * ROW_TILE, LANES), U32), pltpu.SemaphoreType.DMA((2,))],
        compiler_params=_cparams(("arbitrary",), 48),
        name="combine",
    )(dest, dest, gate_t, h, ys, wgu, wdn, g2, b2)


def _moe(h, router_w, router_bias, w_gu, w_down, shared_w_gu, shared_w_down, ln_g, ln_b, alpha):
    top_e, gate, rank, cnt = _router(h, router_w, router_bias)
    counts = cnt[:, 0].astype(jnp.int32)
    bstart, bend, n_blk, pad_start = _expert_blocks(counts)
    dest = _dest(top_e, rank, pad_start)
    xs = _dispatch(h, dest, counts.shape[0])
    ys = _experts(xs, bstart, bend, counts, n_blk, w_gu, w_down)
    return _combine(h, ys, dest, gate.T, shared_w_gu, shared_w_down, ln_g, ln_b, alpha)


def kernel(x, w_in, att_norm_g, lam_re, lam_im, log_step, b_re, b_im, c_re, c_im, d_skip, w_glu, b_glu,
           ssm_norm_g, w_out, ln1_g, ln1_b, router_w, router_bias, w_gu, w_down, shared_w_gu,
           shared_w_down, ln2_g, ln2_b):
    bsz, seq, d = x.shape
    depth = w_in.shape[0]
    alpha = (2 * depth) ** 0.25
    h = x.reshape(bsz * seq, d)
    for i in range(depth):
        proj = _inproj(h, w_in[i].astype(BF16), seq)
        o_att = _attention(proj, bsz, seq)
        y_ssm = _s5(proj.reshape(bsz, seq, -1), 3 * ATT_WIDTH, lam_re[i], lam_im[i], log_step[i],
                    b_re[i], b_im[i], c_re[i], c_im[i], d_skip[i])
        h = _mixout(o_att, y_ssm.reshape(bsz * seq, -1), h, w_glu[i], b_glu[i], att_norm_g[i],
                    ssm_norm_g[i], w_out[i], ln1_g[i], ln1_b[i], alpha)
        h = _moe(h, router_w[i], router_bias[i], w_gu[i], w_down[i], shared_w_gu[i], shared_w_down[i],
                 ln2_g[i], ln2_b[i], alpha)
    return h.reshape(bsz, seq, d)
```

```python
import functools
import math

import jax
import jax.numpy as jnp
from jax import lax
from jax.experimental import pallas as pl
from jax.experimental.pallas import tpu as pltpu

F32 = jnp.float32
BF16 = jnp.bfloat16
U32 = jnp.uint32

ATT_HEADS = 8
HEAD_DIM = 64
ATT_WIDTH = ATT_HEADS * HEAD_DIM
SSM_CH = 16
SSM_STATE = 64
ROPE_THETA = 500000.0
ROT_DIM = HEAD_DIM // 4
DILATIONS = (1, 4, 16)
ATT_BLOCK = 128
ATT_GROUP = 8
N_EXPERTS = 256
TOP_K = 8
N_EXPERT_GROUPS = 8
TOPK_GROUPS = 4
ROUTED_SCALE = 2.5
LN_EPS = 1e-5
RMS_EPS = 1e-6

LANES = 128
SUBLANES = 8
EXPERT_ROWS = 128
MOE_TOKEN_TILE = 256
EXPERT_RING = 8
EXPERT_OUT_RING = 8
ROW_TILE = 4
NEG_INF = float("-inf")


def _cparams(sem, vmem_mb):
    return pltpu.CompilerParams(dimension_semantics=sem, vmem_limit_bytes=vmem_mb * 1024 * 1024)


def _inproj_kernel(x_ref, w_ref, cos_ref, sa_ref, sb_ref, o_ref, *, n_rot_cols):
    xb = x_ref[...].astype(BF16)
    cosf = cos_ref[...]
    sa = sa_ref[...]
    sb = sb_ref[...]
    width = o_ref.shape[1]
    chunk = 512
    for c in range(width // chunk):
        r = jnp.dot(xb, w_ref[:, c * chunk:(c + 1) * chunk], preferred_element_type=F32)
        if c * chunk < n_rot_cols:
            parts = []
            for s in range(chunk // LANES):
                t = r[:, s * LANES:(s + 1) * LANES]
                parts.append(t * cosf + pltpu.roll(t, LANES - ROT_DIM // 2, 1) * sa
                             + pltpu.roll(t, ROT_DIM // 2, 1) * sb)
            r = jnp.concatenate(parts, axis=1)
        o_ref[:, c * chunk:(c + 1) * chunk] = r


def _rope_lane_tables(seq):
    half = ROT_DIM // 2
    inv_freq = jnp.power(jnp.float32(ROPE_THETA), -jnp.arange(half, dtype=F32) / half)
    ang = jnp.arange(seq, dtype=F32)[:, None] * inv_freq[None, :]
    cos, sin = jnp.cos(ang), jnp.sin(ang)
    rest = HEAD_DIM - ROT_DIM
    cos_h = jnp.concatenate([cos, cos, jnp.ones((seq, rest), F32)], axis=1)
    sa_h = jnp.concatenate([-sin, jnp.zeros((seq, half + rest), F32)], axis=1)
    sb_h = jnp.concatenate([jnp.zeros((seq, half), F32), sin, jnp.zeros((seq, rest), F32)], axis=1)
    rep = LANES // HEAD_DIM
    return tuple(jnp.tile(t, (1, rep)) for t in (cos_h, sa_h, sb_h))


def _inproj(x2d, w_in_bf, seq):
    n_tok, d = x2d.shape
    width = w_in_bf.shape[1]
    tm = 512
    cosf, sa, sb = _rope_lane_tables(seq)
    tab_spec = pl.BlockSpec((tm, LANES), lambda i: (i % (seq // tm), 0))
    return pl.pallas_call(
        functools.partial(_inproj_kernel, n_rot_cols=2 * ATT_WIDTH),
        out_shape=jax.ShapeDtypeStruct((n_tok, width), F32),
        grid=(n_tok // tm,),
        in_specs=[pl.BlockSpec((tm, d), lambda i: (i, 0)),
                  pl.BlockSpec((d, width), lambda i: (0, 0)),
                  tab_spec, tab_spec, tab_spec],
        out_specs=pl.BlockSpec((tm, width), lambda i: (i, 0)),
        compiler_params=_cparams(("parallel",), 48),
        name="inproj",
    )(x2d, w_in_bf, cosf, sa, sb)


def _attn_kernel(q_ref, k_ref, v_ref, o_ref, qs_ref, ks_ref, vs_ref, tmp_ref, ob_ref, lb_ref, band_ref,
                 first_ref, *, seq):
    blk = ATT_BLOCK
    lane = lax.broadcasted_iota(jnp.int32, (1, LANES), 1)
    head0 = lane < HEAD_DIM
    scale = HEAD_DIM ** -0.5
    d1, d2 = DILATIONS[1], DILATIONS[2]
    assert DILATIONS[0] == 1 and d2 == d1 * d1
    seg = seq // d1
    sub = seg // d1

    qi = lax.broadcasted_iota(jnp.int32, (blk, 2 * blk), 0)
    kj = lax.broadcasted_iota(jnp.int32, (blk, 2 * blk), 1)
    dist = qi + blk - kj
    band_ref[...] = jnp.where((dist >= 0) & (dist <= blk), 0.0, NEG_INF)
    first_ref[...] = jnp.where((dist >= 0) & (kj >= blk), 0.0, NEG_INF)

    n_class = (1, d1, d2)
    class_len = (seq, seg, sub)
    base = [0]
    for c in range(len(DILATIONS)):
        base.append(base[c] + n_class[c] * (class_len[c] + blk))

    def kv_row0(c, g):
        return base[c] + g * (class_len[c] + blk)

    qs_ref[0] = (q_ref[...] * scale).astype(BF16)
    for a in range(d1):
        x = q_ref[pl.ds(a, seg, stride=d1), :] * scale
        tmp_ref[a * seg:(a + 1) * seg, :] = x
        qs_ref[1, a * seg:(a + 1) * seg, :] = x.astype(BF16)
    for g in range(d2):
        qs_ref[2, g * sub:(g + 1) * sub, :] = tmp_ref[pl.ds((g // d1) * seg + g % d1, sub, stride=d1),
                                                      :].astype(BF16)
    for src_ref, dst_ref in ((k_ref, ks_ref), (v_ref, vs_ref)):
        for c in range(len(DILATIONS)):
            for g in range(n_class[c]):
                dst_ref[kv_row0(c, g):kv_row0(c, g) + blk, :] = jnp.zeros((blk, LANES), BF16)
        dst_ref[kv_row0(0, 0) + blk:kv_row0(0, 0) + blk + seq, :] = src_ref[...].astype(BF16)
        for a in range(d1):
            x = src_ref[pl.ds(a, seg, stride=d1), :]
            tmp_ref[a * seg:(a + 1) * seg, :] = x
            dst_ref[kv_row0(1, a) + blk:kv_row0(1, a) + blk + seg, :] = x.astype(BF16)
        for g in range(d2):
            dst_ref[kv_row0(2, g) + blk:kv_row0(2, g) + blk + sub, :] = tmp_ref[
                pl.ds((g // d1) * seg + g % d1, sub, stride=d1), :].astype(BF16)

    def one_block(c, g, n, out_rows, bias_ref):
        q = qs_ref[c, pl.ds(aligned(g * class_len[c] + n * blk), blk), :]
        kv_rows = pl.ds(aligned(kv_row0(c, g) + n * blk), 2 * blk)
        kk = ks_ref[kv_rows, :]
        vv = vs_ref[kv_rows, :]
        outs = []
        lses = []
        for h in range(LANES // HEAD_DIM):
            hm = head0 if h == 0 else jnp.logical_not(head0)
            qh = jnp.where(hm, q, jnp.zeros_like(q))
            s = lax.dot_general(qh, kk, (((1,), (1,)), ((), ())), preferred_element_type=F32)
            s = s + bias_ref[...]
            m = jnp.max(s, axis=-1, keepdims=True)
            p = jnp.exp(s - m)
            den = jnp.sum(p, axis=-1, keepdims=True)
            outs.append(jnp.dot(p.astype(BF16), vv, preferred_element_type=F32) / den)
            lses.append(m + jnp.log(den))
        ob_ref[c, out_rows, :] = jnp.where(head0, outs[0], outs[1])
        lb_ref[c, out_rows, :] = jnp.where(head0, lses[0], lses[1])

    def run_blocks(n_blocks, fn):
        group = max(g for g in range(1, ATT_GROUP + 1) if n_blocks % g == 0)
        if n_blocks == group:
            for g in range(group):
                fn(g)
            return
        def body(it, carry):
            for g in range(group):
                fn(it * group + g)
            return carry
        lax.fori_loop(0, n_blocks // group, body, 0)

    def aligned(x):
        return x if isinstance(x, int) else pl.multiple_of(x, blk)

    one_block(0, 0, 0, pl.ds(0, blk), first_ref)
    run_blocks(seq // blk - 1,
               lambda i: one_block(0, 0, i + 1, pl.ds(aligned((i + 1) * blk), blk), band_ref))

    nb1 = seg // blk
    run_blocks(d1, lambda a: one_block(1, a, 0, pl.ds(a, blk, stride=d1), first_ref))
    def later1(i):
        a = i // (nb1 - 1)
        n = i - a * (nb1 - 1) + 1
        one_block(1, a, n, pl.ds(a + n * (d1 * blk), blk, stride=d1), band_ref)
    run_blocks(d1 * (nb1 - 1), later1)

    assert sub == blk
    def only2(g):
        a = g // d1
        one_block(2, g, 0, pl.ds(a + d1 * (g - a * d1), blk, stride=d2), first_ref)
    run_blocks(d2, only2)

    rc = 256
    def merge(i, carry):
        sl = pl.ds(pl.multiple_of(i * rc, rc), rc)
        l0 = lb_ref[0, sl, :]
        l1 = lb_ref[1, sl, :]
        l2 = lb_ref[2, sl, :]
        mx = jnp.maximum(jnp.maximum(l0, l1), l2)
        e0 = jnp.exp(l0 - mx)
        e1 = jnp.exp(l1 - mx)
        e2 = jnp.exp(l2 - mx)
        tot = e0 + e1 + e2
        o_ref[sl, :] = ((e0 / tot) * ob_ref[0, sl, :] + (e1 / tot) * ob_ref[1, sl, :]
                        + (e2 / tot) * ob_ref[2, sl, :])
        return carry
    lax.fori_loop(0, seq // rc, merge, 0)


def _attention(proj, bsz, seq):
    n_tok = proj.shape[0]
    pairs = ATT_WIDTH // LANES
    assert seq % (ATT_BLOCK * max(DILATIONS)) == 0
    kv_rows = sum(seq + d * ATT_BLOCK for d in DILATIONS)
    blk = (seq, LANES)
    return pl.pallas_call(
        functools.partial(_attn_kernel, seq=seq),
        out_shape=jax.ShapeDtypeStruct((n_tok, ATT_WIDTH), F32),
        grid=(bsz, pairs),
        in_specs=[pl.BlockSpec(blk, lambda b, h: (b, h)),
                  pl.BlockSpec(blk, lambda b, h: (b, pairs + h)),
                  pl.BlockSpec(blk, lambda b, h: (b, 2 * pairs + h))],
        out_specs=pl.BlockSpec(blk, lambda b, h: (b, h)),
        scratch_shapes=[pltpu.VMEM((len(DILATIONS), seq, LANES), BF16),
                        pltpu.VMEM((kv_rows, LANES), BF16),
                        pltpu.VMEM((kv_rows, LANES), BF16),
                        pltpu.VMEM((seq, LANES), F32),
                        pltpu.VMEM((len(DILATIONS), seq, LANES), F32),
                        pltpu.VMEM((len(DILATIONS), seq, LANES), F32),
                        pltpu.VMEM((ATT_BLOCK, 2 * ATT_BLOCK), F32),
                        pltpu.VMEM((ATT_BLOCK, 2 * ATT_BLOCK), F32)],
        compiler_params=_cparams(("parallel", "parallel"), 40),
        name="attn",
    )(proj, proj, proj)


def _s5_kernel(u_ref, bm_ref, lam_ref, cm_ref, dk_ref, o_ref, us_ref, st_ref, ys_ref, carry_ref, *, tc):
    bsz = u_ref.shape[0]
    half = st_ref.shape[1] // 2
    rows = tc * bsz
    mm_rows = 512

    @pl.when(pl.program_id(1) == 0)
    def _():
        carry_ref[...] = jnp.zeros_like(carry_ref)

    for b in range(bsz):
        us_ref[pl.ds(b, tc, stride=bsz), :] = u_ref[b]

    bm = bm_ref[0]
    for r0 in range(0, rows, mm_rows):
        st_ref[r0:r0 + mm_rows, :] = jnp.dot(us_ref[r0:r0 + mm_rows, :].astype(BF16), bm,
                                             preferred_element_type=F32)

    lam = lam_ref[0]
    lam_re = lam[:, :half]
    lam_im = lam[:, half:]

    def step(t, carry):
        xr, xi = carry
        sl = pl.ds(pl.multiple_of(t * bsz, bsz), bsz)
        nr = lam_re * xr - lam_im * xi + st_ref[sl, :half]
        ni = lam_re * xi + lam_im * xr + st_ref[sl, half:]
        st_ref[sl, :half] = nr
        st_ref[sl, half:] = ni
        return nr, ni

    xr, xi = lax.fori_loop(0, tc, step, (carry_ref[:, :half], carry_ref[:, half:]), unroll=4)
    carry_ref[:, :half] = xr
    carry_ref[:, half:] = xi

    cm = cm_ref[0]
    for r0 in range(0, rows, mm_rows):
        ys_ref[r0:r0 + mm_rows, :] = jnp.dot(st_ref[r0:r0 + mm_rows, :].astype(BF16), cm,
                                             preferred_element_type=F32)
    dk = dk_ref[...]
    for b in range(bsz):
        o_ref[b] = ys_ref[pl.ds(b, tc, stride=bsz), :] + dk * u_ref[b]


def _s5_params(lam_re, lam_im, log_step, b_re, b_im, c_re, c_im, bsz):
    groups = lam_re.shape[0]
    gpc = LANES // SSM_CH
    n_chunks = groups // gpc
    lam = lax.complex(lam_re.astype(F32), lam_im.astype(F32))
    step = jnp.exp(log_step.astype(F32))[:, None]
    lam_bar = jnp.exp(lam * step)
    bmat = lax.complex(b_re.astype(F32), b_im.astype(F32))
    b_bar = ((lam_bar - 1.0) / lam)[..., None] * bmat
    eye = jnp.eye(gpc, dtype=F32)

    def block_diag_in(t):
        t = t.reshape(n_chunks, gpc, SSM_STATE, SSM_CH)
        return jnp.einsum('ngpc,gh->ngchp', t, eye).reshape(n_chunks, gpc * SSM_CH, gpc * SSM_STATE)

    def block_diag_out(t):
        t = t.reshape(n_chunks, gpc, SSM_CH, SSM_STATE)
        return jnp.einsum('ngcp,gh->ngphc', t, eye).reshape(n_chunks, gpc * SSM_STATE, gpc * SSM_CH)

    bm = jnp.concatenate([block_diag_in(b_bar.real), block_diag_in(b_bar.imag)], axis=2).astype(BF16)
    cm = jnp.concatenate([block_diag_out(c_re.astype(F32)), block_diag_out(-c_im.astype(F32))],
                         axis=1).astype(BF16)
    lam_row = jnp.concatenate([lam_bar.real.reshape(n_chunks, gpc * SSM_STATE),
                               lam_bar.imag.reshape(n_chunks, gpc * SSM_STATE)], axis=1)
    lam_t = jnp.broadcast_to(lam_row[:, None, :], (n_chunks, bsz, 2 * gpc * SSM_STATE))
    return bm, lam_t, cm, n_chunks


def _s5(proj3, u_col0, lam_re, lam_im, log_step, b_re, b_im, c_re, c_im, d_skip):
    bsz, seq, _ = proj3.shape
    assert bsz == SUBLANES
    bm, lam_t, cm, n_chunks = _s5_params(lam_re, lam_im, log_step, b_re, b_im, c_re, c_im, bsz)
    width = n_chunks * LANES
    tc = 256
    st_cols = bm.shape[2]
    ublk0 = u_col0 // LANES
    return pl.pallas_call(
        functools.partial(_s5_kernel, tc=tc),
        out_shape=jax.ShapeDtypeStruct((bsz, seq, width), F32),
        grid=(n_chunks, seq // tc),
        in_specs=[pl.BlockSpec((bsz, tc, LANES), lambda c, t: (0, t, ublk0 + c)),
                  pl.BlockSpec((1, LANES, st_cols), lambda c, t: (c, 0, 0)),
                  pl.BlockSpec((1, bsz, st_cols), lambda c, t: (c, 0, 0)),
                  pl.BlockSpec((1, st_cols, LANES), lambda c, t: (c, 0, 0)),
                  pl.BlockSpec((1, LANES), lambda c, t: (0, c))],
        out_specs=pl.BlockSpec((bsz, tc, LANES), lambda c, t: (0, t, c)),
        scratch_shapes=[pltpu.VMEM((tc * bsz, LANES), F32),
                        pltpu.VMEM((tc * bsz, st_cols), F32),
                        pltpu.VMEM((tc * bsz, LANES), F32),
                        pltpu.VMEM((bsz, st_cols), F32)],
        compiler_params=_cparams(("arbitrary", "arbitrary"), 40),
        name="s5",
    )(proj3, bm, lam_t, cm, d_skip.reshape(1, width).astype(F32))


def _layer_norm(v, g, b):
    mu = jnp.mean(v, axis=-1, keepdims=True)
    var = jnp.mean(jnp.square(v - mu), axis=-1, keepdims=True)
    return (v - mu) * lax.rsqrt(var + LN_EPS) * g + b


def _rms_norm(v, g):
    return v * lax.rsqrt(jnp.mean(jnp.square(v), axis=-1, keepdims=True) + RMS_EPS) * g


def _mixout_kernel(att_ref, ssm_ref, x_ref, wglu_ref, bglu_ref, ag_ref, sg_ref, wout_ref, g_ref, b_ref,
                   o_ref, *, alpha):
    y = jax.nn.gelu(ssm_ref[...])
    z = jnp.dot(y.astype(BF16), wglu_ref[...], preferred_element_type=F32) + bglu_ref[...]
    o_ssm = y * jax.nn.sigmoid(z)
    a = _rms_norm(att_ref[...], ag_ref[...]).astype(BF16)
    s = _rms_norm(o_ssm, sg_ref[...]).astype(BF16)
    wa = att_ref.shape[1]
    mix = (jnp.dot(a, wout_ref[:wa, :], preferred_element_type=F32)
           + jnp.dot(s, wout_ref[wa:, :], preferred_element_type=F32))
    o_ref[...] = _layer_norm(alpha * x_ref[...] + mix, g_ref[...], b_ref[...])


def _mixout(o_att, y_ssm, x2d, w_glu, b_glu, att_g, ssm_g, w_out, ln_g, ln_b, alpha):
    n_tok, d = x2d.shape
    wa = o_att.shape[1]
    ws = y_ssm.shape[1]
    tm = 256
    row = lambda w: pl.BlockSpec((tm, w), lambda i: (i, 0))
    full = lambda a: pl.BlockSpec(a.shape, lambda i: (0,) * a.ndim)
    args = (o_att, y_ssm, x2d, w_glu.astype(BF16), b_glu.reshape(1, ws), att_g.reshape(1, wa),
            ssm_g.reshape(1, ws), w_out.astype(BF16), ln_g.reshape(1, d), ln_b.reshape(1, d))
    return pl.pallas_call(
        functools.partial(_mixout_kernel, alpha=alpha),
        out_shape=jax.ShapeDtypeStruct((n_tok, d), F32),
        grid=(n_tok // tm,),
        in_specs=[row(wa), row(ws), row(d)] + [full(a) for a in args[3:]],
        out_specs=row(d),
        compiler_params=_cparams(("parallel",), 32),
        name="mixout",
    )(*args)


def _split_bf16(v):
    hi = v.astype(BF16)
    lo = (v - hi.astype(F32)).astype(BF16)
    return hi, lo


def _router_kernel(h_ref, wt_ref, bias_ref, e_ref, g_ref, r_ref, cnt_ref, run_ref):
    tm = h_ref.shape[0]
    n_exp = wt_ref.shape[0]
    gsz = n_exp // N_EXPERT_GROUPS

    @pl.when(pl.program_id(0) == 0)
    def _():
        run_ref[...] = jnp.zeros_like(run_ref)

    w_hi, w_lo = _split_bf16(wt_ref[...])
    h_hi, h_lo = _split_bf16(h_ref[...])
    nt = (((1,), (1,)), ((), ()))
    logits = (lax.dot_general(w_hi, h_hi, nt, preferred_element_type=F32)
              + lax.dot_general(w_hi, h_lo, nt, preferred_element_type=F32)
              + lax.dot_general(w_lo, h_hi, nt, preferred_element_type=F32))
    scores = jax.nn.sigmoid(logits)
    choice = scores + bias_ref[:, 0:1]

    gio = lax.broadcasted_iota(jnp.int32, (gsz, tm), 0).astype(F32)
    gscore = []
    for g in range(N_EXPERT_GROUPS):
        cg = choice[g * gsz:(g + 1) * gsz, :]
        m1 = jnp.max(cg, axis=0, keepdims=True)
        i1 = jnp.min(jnp.where(cg == m1, gio, float(gsz)), axis=0, keepdims=True)
        m2 = jnp.max(jnp.where(gio == i1, NEG_INF, cg), axis=0, keepdims=True)
        gscore.append(m1 + m2)
    masked = []
    for g in range(N_EXPERT_GROUPS):
        beat = jnp.zeros((1, tm), F32)
        for o in range(N_EXPERT_GROUPS):
            if o == g:
                continue
            wins = (gscore[o] >= gscore[g]) if o < g else (gscore[o] > gscore[g])
            beat = beat + jnp.where(wins, 1.0, 0.0)
        keep = beat < float(TOPK_GROUPS)
        masked.append(jnp.where(keep, choice[g * gsz:(g + 1) * gsz, :], NEG_INF))
    cur = jnp.concatenate(masked, axis=0)

    eio = lax.broadcasted_iota(jnp.int32, (n_exp, tm), 0).astype(F32)
    idxs = []
    gates = []
    onehot = jnp.zeros((n_exp, tm), F32)
    for _ in range(TOP_K):
        m = jnp.max(cur, axis=0, keepdims=True)
        idx = jnp.min(jnp.where(cur == m, eio, float(n_exp)), axis=0, keepdims=True)
        hit = eio == idx
        idxs.append(idx)
        gates.append(jnp.sum(jnp.where(hit, scores, 0.0), axis=0, keepdims=True))
        cur = jnp.where(hit, NEG_INF, cur)
        onehot = onehot + jnp.where(hit, 1.0, 0.0)
    gate = jnp.concatenate(gates, axis=0)
    gate = ROUTED_SCALE * gate / (jnp.sum(gate, axis=0, keepdims=True) + 1e-20)

    si = lax.broadcasted_iota(jnp.int32, (tm, tm), 0)
    ti = lax.broadcasted_iota(jnp.int32, (tm, tm), 1)
    upper = jnp.where(si < ti, 1.0, 0.0).astype(BF16)
    before = jnp.dot(onehot.astype(BF16), upper, preferred_element_type=F32) + run_ref[:, 0:1]
    ranks = [jnp.sum(jnp.where(eio == idx, before, 0.0), axis=0, keepdims=True) for idx in idxs]

    e_ref[...] = jnp.concatenate(idxs, axis=0).astype(jnp.int32)
    g_ref[...] = gate
    r_ref[...] = jnp.concatenate(ranks, axis=0).astype(jnp.int32)
    run_ref[...] = run_ref[...] + jnp.sum(onehot, axis=1, keepdims=True)
    cnt_ref[...] = run_ref[...]


def _router(h, router_w, router_bias):
    n_tok, d = h.shape
    n_exp = router_w.shape[1]
    tm = 256
    wt = router_w.astype(F32).T
    bias = jnp.broadcast_to(router_bias.astype(F32)[:, None], (n_exp, LANES))
    tok = pl.BlockSpec((TOP_K, tm), lambda i: (0, i))
    return pl.pallas_call(
        _router_kernel,
        out_shape=(jax.ShapeDtypeStruct((TOP_K, n_tok), jnp.int32),
                   jax.ShapeDtypeStruct((TOP_K, n_tok), F32),
                   jax.ShapeDtypeStruct((TOP_K, n_tok), jnp.int32),
                   jax.ShapeDtypeStruct((n_exp, LANES), F32)),
        grid=(n_tok // tm,),
        in_specs=[pl.BlockSpec((tm, d), lambda i: (i, 0)),
                  pl.BlockSpec((n_exp, d), lambda i: (0, 0)),
                  pl.BlockSpec((n_exp, LANES), lambda i: (0, 0))],
        out_specs=(tok, tok, tok, pl.BlockSpec((n_exp, LANES), lambda i: (0, 0))),
        scratch_shapes=[pltpu.VMEM((n_exp, LANES), F32)],
        compiler_params=_cparams(("arbitrary",), 32),
        name="router",
    )(h, wt, bias)


def _dest_kernel(e_ref, r_ref, st_ref, d_ref):
    n_exp = st_ref.shape[0]
    tm = e_ref.shape[1]
    eio = lax.broadcasted_iota(jnp.int32, (n_exp, tm), 0)
    start = st_ref[:, 0:1]
    rows = [jnp.sum(jnp.where(eio == e_ref[k:k + 1, :], start, 0.0), axis=0, keepdims=True)
            for k in range(TOP_K)]
    d_ref[...] = jnp.concatenate(rows, axis=0).astype(jnp.int32) + r_ref[...]


def _dest(top_e, rank, starts):
    n_tok = top_e.shape[1]
    n_exp = starts.shape[0]
    tm = 512
    st = jnp.broadcast_to(starts.astype(F32)[:, None], (n_exp, LANES))
    tok = pl.BlockSpec((TOP_K, tm), lambda i: (0, i))
    return pl.pallas_call(
        _dest_kernel,
        out_shape=jax.ShapeDtypeStruct((TOP_K, n_tok), jnp.int32),
        grid=(n_tok // tm,),
        in_specs=[tok, tok, pl.BlockSpec((n_exp, LANES), lambda i: (0, 0))],
        out_specs=tok,
        compiler_params=_cparams(("parallel",), 32),
        name="dest",
    )(top_e, rank, st)


def _pack_bf16_pairs(val):
    half = val.shape[1] // 2
    lo = pltpu.bitcast(val[:, :half].astype(BF16).astype(F32), U32)
    hi = pltpu.bitcast(val[:, half:].astype(BF16).astype(F32), U32)
    return (lo >> 16) | (hi & jnp.uint32(0xFFFF0000))


def _unpack_bf16_pairs(words):
    lo = pltpu.bitcast(words << 16, F32)
    hi = pltpu.bitcast(words & jnp.uint32(0xFFFF0000), F32)
    return jnp.concatenate([lo, hi], axis=1)


def _to_row_tiles(dst_ref, slot, val):
    rows = val.shape[0]
    words = _pack_bf16_pairs(val)
    for j in range(ROW_TILE):
        dst_ref[slot, pl.ds(j, rows, stride=ROW_TILE), :] = words[:, j * LANES:(j + 1) * LANES]


def _row_tile_words(src_ref, idx, rows):
    return jnp.concatenate([src_ref[(*idx, pl.ds(j, rows, stride=ROW_TILE), slice(None))]
                            for j in range(ROW_TILE)], axis=1)


def _row_tile(r):
    return pl.ds(pl.multiple_of(r * ROW_TILE, ROW_TILE), ROW_TILE)


def _dispatch_kernel(dest_ref, h_ref, xs_ref, ht_ref, sem):
    tm = h_ref.shape[0]
    i = pl.program_id(0)
    cur = i % 2
    _to_row_tiles(ht_ref, cur, h_ref[...])

    def issue(t, carry):
        for k in range(TOP_K):
            pltpu.make_async_copy(ht_ref.at[cur, _row_tile(t)], xs_ref.at[_row_tile(dest_ref[k * tm + t])],
                                  sem.at[cur]).start(priority=k % 2)
        return carry
    lax.fori_loop(0, tm, issue, 0, unroll=2)

    def drain(slot):
        for k in range(TOP_K):
            pltpu.make_async_copy(ht_ref.at[slot], xs_ref.at[pl.ds(0, tm * ROW_TILE)], sem.at[slot]).wait()

    @pl.when(i > 0)
    def _():
        drain(1 - cur)

    @pl.when(i == pl.num_programs(0) - 1)
    def _():
        drain(cur)


def _dispatch(h, dest, n_exp):
    n_tok, d = h.shape
    assert d == 2 * ROW_TILE * LANES
    tm = MOE_TOKEN_TILE
    n_rows = n_tok * TOP_K + n_exp * EXPERT_ROWS
    return pl.pallas_call(
        _dispatch_kernel,
        out_shape=jax.ShapeDtypeStruct((n_rows * ROW_TILE, LANES), U32),
        grid=(n_tok // tm,),
        in_specs=[pl.BlockSpec((TOP_K * tm,), lambda i: (i,), memory_space=pltpu.SMEM),
                  pl.BlockSpec((tm, d), lambda i: (i, 0))],
        out_specs=pl.BlockSpec(memory_space=pl.ANY),
        scratch_shapes=[pltpu.VMEM((2, tm * ROW_TILE, LANES), U32), pltpu.SemaphoreType.DMA((2,))],
        compiler_params=_cparams(("arbitrary",), 32),
        name="dispatch",
    )(dest, h)


def _experts_kernel(bstart_ref, bend_ref, cnt_ref, nblk_ref, xs_ref, wgu_ref, wdn_ref, ys_ref,
                    xbuf, ybuf, act_ref, wgu_bf, wdn_bf, xsem, ysem):
    e = pl.program_id(0)
    n_blk = nblk_ref[0]
    trows = xbuf.shape[1]
    rows = trows // ROW_TILE
    ff = wdn_bf.shape[0]
    b0 = bstart_ref[e]
    b1 = bend_ref[e]

    def block_rows(b):
        return pl.ds(pl.multiple_of(b * trows, trows), trows)

    def x_copy(b):
        slot = b % EXPERT_RING
        return pltpu.make_async_copy(xs_ref.at[block_rows(b)], xbuf.at[slot], xsem.at[slot])

    def y_copy(b):
        slot = b % EXPERT_OUT_RING
        return pltpu.make_async_copy(ybuf.at[slot], ys_ref.at[block_rows(b)], ysem.at[slot])

    @pl.when(e == 0)
    def _():
        for i in range(EXPERT_RING):
            @pl.when(i < n_blk)
            def _():
                x_copy(i).start(priority=1)

    wgu_bf[...] = wgu_ref[0].astype(BF16)
    wdn_bf[...] = wdn_ref[0].astype(BF16)

    def up(blocks):
        for b in blocks:
            x_copy(b).wait()
        for b in blocks:
            words = _row_tile_words(xbuf, (b % EXPERT_RING,), rows)
            row = lax.broadcasted_iota(jnp.int32, (rows, 1), 0)
            words = jnp.where(row < cnt_ref[e] - (b - b0) * rows, words, jnp.uint32(0))
            xb = _unpack_bf16_pairs(words).astype(BF16)
            gu = jnp.dot(xb, wgu_bf[...], preferred_element_type=F32)
            act_ref[b - b0] = (jax.nn.silu(gu[:, :ff]) * gu[:, ff:]).astype(BF16)
        for b in blocks:
            @pl.when(b + EXPERT_RING < n_blk)
            def _():
                x_copy(b + EXPERT_RING).start(priority=1)

    def up_pair(p, carry):
        up((b0 + 2 * p, b0 + 2 * p + 1))
        return carry
    n_mine = b1 - b0
    lax.fori_loop(0, n_mine // 2, up_pair, 0)

    @pl.when(n_mine % 2 == 1)
    def _():
        up((b1 - 1,))

    def down(blocks):
        for b in blocks:
            @pl.when(b >= EXPERT_OUT_RING)
            def _():
                y_copy(b - EXPERT_OUT_RING).wait()
        for b in blocks:
            _to_row_tiles(ybuf, b % EXPERT_OUT_RING,
                          jnp.dot(act_ref[b - b0], wdn_bf[...], preferred_element_type=F32))
        for b in blocks:
            y_copy(b).start(priority=1)
        last = blocks[-1]

        @pl.when(last == n_blk - 1)
        def _():
            for i in range(EXPERT_OUT_RING):
                @pl.when(last >= i)
                def _():
                    y_copy(last - i).wait()

    def down_pair(p, carry):
        down((b0 + 2 * p, b0 + 2 * p + 1))
        return carry
    lax.fori_loop(0, n_mine // 2, down_pair, 0)

    @pl.when(n_mine % 2 == 1)
    def _():
        down((b1 - 1,))


def _expert_blocks(counts):
    blocks = (counts + EXPERT_ROWS - 1) // EXPERT_ROWS
    bend = jnp.cumsum(blocks)
    bstart = bend - blocks
    i32 = lambda a: a.astype(jnp.int32)
    return i32(bstart), i32(bend), i32(bend[-1]).reshape(1), i32(bstart * EXPERT_ROWS)


def _experts(xs, bstart, bend, counts, n_blk, w_gu, w_down):
    n_exp, d, ff2 = w_gu.shape
    ff = w_down.shape[1]
    n_rows = xs.shape[0] // ROW_TILE
    assert n_rows % EXPERT_ROWS == 0 and d == 2 * ROW_TILE * LANES
    max_blocks = (n_rows - n_exp * EXPERT_ROWS) // TOP_K // EXPERT_ROWS + 1
    grid_spec = pltpu.PrefetchScalarGridSpec(
        num_scalar_prefetch=4,
        grid=(n_exp,),
        in_specs=[pl.BlockSpec(memory_space=pl.ANY),
                  pl.BlockSpec((1, d, ff2), lambda e, *_: (e, 0, 0)),
                  pl.BlockSpec((1, ff, d), lambda e, *_: (e, 0, 0))],
        out_specs=pl.BlockSpec(memory_space=pl.ANY),
        scratch_shapes=[pltpu.VMEM((EXPERT_RING, EXPERT_ROWS * ROW_TILE, LANES), U32),
                        pltpu.VMEM((EXPERT_OUT_RING, EXPERT_ROWS * ROW_TILE, LANES), U32),
                        pltpu.VMEM((max_blocks, EXPERT_ROWS, ff), BF16),
                        pltpu.VMEM((d, ff2), BF16), pltpu.VMEM((ff, d), BF16),
                        pltpu.SemaphoreType.DMA((EXPERT_RING,)),
                        pltpu.SemaphoreType.DMA((EXPERT_OUT_RING,))],
    )
    return pl.pallas_call(
        _experts_kernel,
        out_shape=jax.ShapeDtypeStruct(xs.shape, U32),
        grid_spec=grid_spec,
        compiler_params=_cparams(("arbitrary",), 40),
        name="experts",
    )(bstart, bend, counts, n_blk, xs, w_gu, w_down)


def _combine_kernel(dest_ref, dnext_ref, gate_ref, h_ref, ys_ref, wgu_ref, wdn_ref, g_ref, b_ref, o_ref,
                    buf_ref, sem, *, alpha):
    tm = h_ref.shape[0]
    i = pl.program_id(0)
    cur = i % 2

    def gather(d_ref, slot):
        def issue(t, carry):
            for k in range(TOP_K):
                pltpu.make_async_copy(ys_ref.at[_row_tile(d_ref[k * tm + t])], buf_ref.at[slot, k, _row_tile(t)],
                                      sem.at[slot]).start(priority=k % 2)
            return carry
        lax.fori_loop(0, tm, issue, 0, unroll=2)

    @pl.when(i == 0)
    def _():
        gather(dest_ref, 0)

    @pl.when(i + 1 < pl.num_programs(0))
    def _():
        gather(dnext_ref, 1 - cur)

    h = h_ref[...]
    ff = wdn_ref.shape[0]
    gu = jnp.dot(h.astype(BF16), wgu_ref[...], preferred_element_type=F32)
    act = (jax.nn.silu(gu[:, :ff]) * gu[:, ff:]).astype(BF16)
    acc = alpha * h + jnp.dot(act, wdn_ref[...], preferred_element_type=F32)

    for k in range(TOP_K):
        pltpu.make_async_copy(ys_ref.at[pl.ds(0, tm * ROW_TILE)], buf_ref.at[cur, k], sem.at[cur]).wait()

    gate = gate_ref[...]
    for k in range(TOP_K):
        acc = acc + gate[:, k:k + 1] * _unpack_bf16_pairs(_row_tile_words(buf_ref, (cur, k), tm))
    o_ref[...] = _layer_norm(acc, g_ref[...], b_ref[...])


def _combine(h, ys, dest, gate_t, shared_w_gu, shared_w_down, ln_g, ln_b, alpha):
    n_tok, d = h.shape
    tm = MOE_TOKEN_TILE
    n_tiles = n_tok // tm
    full = lambda a: pl.BlockSpec(a.shape, lambda i: (0,) * a.ndim)
    wgu = shared_w_gu.astype(BF16)
    wdn = shared_w_down.astype(BF16)
    g2 = ln_g.reshape(1, d)
    b2 = ln_b.reshape(1, d)
    return pl.pallas_call(
        functools.partial(_combine_kernel, alpha=alpha),
        out_shape=jax.ShapeDtypeStruct((n_tok, d), F32),
        grid=(n_tiles,),
        in_specs=[pl.BlockSpec((TOP_K * tm,), lambda i: (i,), memory_space=pltpu.SMEM),
                  pl.BlockSpec((TOP_K * tm,), lambda i: (jnp.minimum(i + 1, n_tiles - 1),),
                               memory_space=pltpu.SMEM),
                  pl.BlockSpec((tm, TOP_K), lambda i: (i, 0)),
                  pl.BlockSpec((tm, d), lambda i: (i, 0)),
                  pl.BlockSpec(memory_space=pl.ANY),
                  full(wgu), full(wdn), full(g2), full(b2)],
        out_specs=pl.BlockSpec((tm, d), lambda i: (i, 0)),
        scratch_shapes=[pltpu.VMEM((2, TOP_K, tm * ROW_TILE, LANES), U32), pltpu.SemaphoreType.DMA((2,))],
        compiler_params=_cparams(("arbitrary",), 48),
        name="combine",
    )(dest, dest, gate_t, h, ys, wgu, wdn, g2, b2)


def _moe(h, router_w, router_bias, w_gu, w_down, shared_w_gu, shared_w_down, ln_g, ln_b, alpha):
    top_e, gate, rank, cnt = _router(h, router_w, router_bias)
    counts = cnt[:, 0].astype(jnp.int32)
    bstart, bend, n_blk, pad_start = _expert_blocks(counts)
    dest = _dest(top_e, rank, pad_start)
    dest_tiles = dest.reshape(TOP_K, -1, MOE_TOKEN_TILE).transpose(1, 0, 2).reshape(-1)
    xs = _dispatch(h, dest_tiles, counts.shape[0])
    ys = _experts(xs, bstart, bend, counts, n_blk, w_gu, w_down)
    return _combine(h, ys, dest_tiles, gate.T, shared_w_gu, shared_w_down, ln_g, ln_b, alpha)


def kernel(x, w_in, att_norm_g, lam_re, lam_im, log_step, b_re, b_im, c_re, c_im, d_skip, w_glu, b_glu,
           ssm_norm_g, w_out, ln1_g, ln1_b, router_w, router_bias, w_gu, w_down, shared_w_gu,
           shared_w_down, ln2_g, ln2_b):
    bsz, seq, d = x.shape
    depth = w_in.shape[0]
    alpha = (2 * depth) ** 0.25
    h = x.reshape(bsz * seq, d)
    for i in range(depth):
        proj = _inproj(h, w_in[i].astype(BF16), seq)
        o_att = _attention(proj, bsz, seq)
        y_ssm = _s5(proj.reshape(bsz, seq, -1), 3 * ATT_WIDTH, lam_re[i], lam_im[i], log_step[i],
                    b_re[i], b_im[i], c_re[i], c_im[i], d_skip[i])
        h = _mixout(o_att, y_ssm.reshape(bsz * seq, -1), h, w_glu[i], b_glu[i], att_norm_g[i],
                    ssm_norm_g[i], w_out[i], ln1_g[i], ln1_b[i], alpha)
        h = _moe(h, router_w[i], router_bias[i], w_gu[i], w_down[i], shared_w_gu[i], shared_w_down[i],
                 ln2_g[i], ln2_b[i], alpha)
    return h.reshape(bsz, seq, d)
```

```python
import functools
import math

import jax
import jax.numpy as jnp
from jax import lax
from jax.experimental import pallas as pl
from jax.experimental.pallas import tpu as pltpu

F32 = jnp.float32
BF16 = jnp.bfloat16
U32 = jnp.uint32

ATT_HEADS = 8
HEAD_DIM = 64
ATT_WIDTH = ATT_HEADS * HEAD_DIM
SSM_CH = 16
SSM_STATE = 64
ROPE_THETA = 500000.0
ROT_DIM = HEAD_DIM // 4
DILATIONS = (1, 4, 16)
ATT_BLOCK = 128
ATT_GROUP = 8
N_EXPERTS = 256
TOP_K = 8
N_EXPERT_GROUPS = 8
TOPK_GROUPS = 4
ROUTED_SCALE = 2.5
LN_EPS = 1e-5
RMS_EPS = 1e-6

LANES = 128
SUBLANES = 8
EXPERT_ROWS = 128
MOE_TOKEN_TILE = 512
EXPERT_RING = 8
EXPERT_OUT_RING = 8
ROW_TILE = 4
NEG_INF = float("-inf")


def _cparams(sem, vmem_mb):
    return pltpu.CompilerParams(dimension_semantics=sem, vmem_limit_bytes=vmem_mb * 1024 * 1024)


def _inproj_kernel(x_ref, w_ref, cos_ref, sa_ref, sb_ref, o_ref, *, n_rot_cols):
    xb = x_ref[...].astype(BF16)
    cosf = cos_ref[...]
    sa = sa_ref[...]
    sb = sb_ref[...]
    width = o_ref.shape[1]
    chunk = 512
    for c in range(width // chunk):
        r = jnp.dot(xb, w_ref[:, c * chunk:(c + 1) * chunk], preferred_element_type=F32)
        if c * chunk < n_rot_cols:
            parts = []
            for s in range(chunk // LANES):
                t = r[:, s * LANES:(s + 1) * LANES]
                parts.append(t * cosf + pltpu.roll(t, LANES - ROT_DIM // 2, 1) * sa
                             + pltpu.roll(t, ROT_DIM // 2, 1) * sb)
            r = jnp.concatenate(parts, axis=1)
        o_ref[:, c * chunk:(c + 1) * chunk] = r


def _rope_lane_tables(seq):
    half = ROT_DIM // 2
    inv_freq = jnp.power(jnp.float32(ROPE_THETA), -jnp.arange(half, dtype=F32) / half)
    ang = jnp.arange(seq, dtype=F32)[:, None] * inv_freq[None, :]
    cos, sin = jnp.cos(ang), jnp.sin(ang)
    rest = HEAD_DIM - ROT_DIM
    cos_h = jnp.concatenate([cos, cos, jnp.ones((seq, rest), F32)], axis=1)
    sa_h = jnp.concatenate([-sin, jnp.zeros((seq, half + rest), F32)], axis=1)
    sb_h = jnp.concatenate([jnp.zeros((seq, half), F32), sin, jnp.zeros((seq, rest), F32)], axis=1)
    rep = LANES // HEAD_DIM
    return tuple(jnp.tile(t, (1, rep)) for t in (cos_h, sa_h, sb_h))


def _inproj(x2d, w_in_bf, seq):
    n_tok, d = x2d.shape
    width = w_in_bf.shape[1]
    tm = 512
    cosf, sa, sb = _rope_lane_tables(seq)
    tab_spec = pl.BlockSpec((tm, LANES), lambda i: (i % (seq // tm), 0))
    return pl.pallas_call(
        functools.partial(_inproj_kernel, n_rot_cols=2 * ATT_WIDTH),
        out_shape=jax.ShapeDtypeStruct((n_tok, width), F32),
        grid=(n_tok // tm,),
        in_specs=[pl.BlockSpec((tm, d), lambda i: (i, 0)),
                  pl.BlockSpec((d, width), lambda i: (0, 0)),
                  tab_spec, tab_spec, tab_spec],
        out_specs=pl.BlockSpec((tm, width), lambda i: (i, 0)),
        compiler_params=_cparams(("parallel",), 48),
        name="inproj",
    )(x2d, w_in_bf, cosf, sa, sb)


def _attn_kernel(q_ref, k_ref, v_ref, o_ref, qs_ref, ks_ref, vs_ref, tmp_ref, ob_ref, lb_ref, band_ref,
                 first_ref, *, seq):
    blk = ATT_BLOCK
    lane = lax.broadcasted_iota(jnp.int32, (1, LANES), 1)
    head0 = lane < HEAD_DIM
    scale = HEAD_DIM ** -0.5
    d1, d2 = DILATIONS[1], DILATIONS[2]
    assert DILATIONS[0] == 1 and d2 == d1 * d1
    seg = seq // d1
    sub = seg // d1

    qi = lax.broadcasted_iota(jnp.int32, (blk, 2 * blk), 0)
    kj = lax.broadcasted_iota(jnp.int32, (blk, 2 * blk), 1)
    dist = qi + blk - kj
    band_ref[...] = jnp.where((dist >= 0) & (dist <= blk), 0.0, NEG_INF)
    first_ref[...] = jnp.where((dist >= 0) & (kj >= blk), 0.0, NEG_INF)

    n_class = (1, d1, d2)
    class_len = (seq, seg, sub)
    base = [0]
    for c in range(len(DILATIONS)):
        base.append(base[c] + n_class[c] * (class_len[c] + blk))

    def kv_row0(c, g):
        return base[c] + g * (class_len[c] + blk)

    qs_ref[0] = (q_ref[...] * scale).astype(BF16)
    for a in range(d1):
        x = q_ref[pl.ds(a, seg, stride=d1), :] * scale
        tmp_ref[a * seg:(a + 1) * seg, :] = x
        qs_ref[1, a * seg:(a + 1) * seg, :] = x.astype(BF16)
    for g in range(d2):
        qs_ref[2, g * sub:(g + 1) * sub, :] = tmp_ref[pl.ds((g // d1) * seg + g % d1, sub, stride=d1),
                                                      :].astype(BF16)
    for src_ref, dst_ref in ((k_ref, ks_ref), (v_ref, vs_ref)):
        for c in range(len(DILATIONS)):
            for g in range(n_class[c]):
                dst_ref[kv_row0(c, g):kv_row0(c, g) + blk, :] = jnp.zeros((blk, LANES), BF16)
        dst_ref[kv_row0(0, 0) + blk:kv_row0(0, 0) + blk + seq, :] = src_ref[...].astype(BF16)
        for a in range(d1):
            x = src_ref[pl.ds(a, seg, stride=d1), :]
            tmp_ref[a * seg:(a + 1) * seg, :] = x
            dst_ref[kv_row0(1, a) + blk:kv_row0(1, a) + blk + seg, :] = x.astype(BF16)
        for g in range(d2):
            dst_ref[kv_row0(2, g) + blk:kv_row0(2, g) + blk + sub, :] = tmp_ref[
                pl.ds((g // d1) * seg + g % d1, sub, stride=d1), :].astype(BF16)

    def one_block(c, g, n, out_rows, bias_ref):
        q = qs_ref[c, pl.ds(aligned(g * class_len[c] + n * blk), blk), :]
        kv_rows = pl.ds(aligned(kv_row0(c, g) + n * blk), 2 * blk)
        kk = ks_ref[kv_rows, :]
        vv = vs_ref[kv_rows, :]
        outs = []
        lses = []
        for h in range(LANES // HEAD_DIM):
            hm = head0 if h == 0 else jnp.logical_not(head0)
            qh = jnp.where(hm, q, jnp.zeros_like(q))
            s = lax.dot_general(qh, kk, (((1,), (1,)), ((), ())), preferred_element_type=F32)
            s = s + bias_ref[...]
            m = jnp.max(s, axis=-1, keepdims=True)
            p = jnp.exp(s - m)
            den = jnp.sum(p, axis=-1, keepdims=True)
            outs.append(jnp.dot(p.astype(BF16), vv, preferred_element_type=F32) / den)
            lses.append(m + jnp.log(den))
        ob_ref[c, out_rows, :] = jnp.where(head0, outs[0], outs[1])
        lb_ref[c, out_rows, :] = jnp.where(head0, lses[0], lses[1])

    def run_blocks(n_blocks, fn):
        group = max(g for g in range(1, ATT_GROUP + 1) if n_blocks % g == 0)
        if n_blocks == group:
            for g in range(group):
                fn(g)
            return
        def body(it, carry):
            for g in range(group):
                fn(it * group + g)
            return carry
        lax.fori_loop(0, n_blocks // group, body, 0)

    def aligned(x):
        return x if isinstance(x, int) else pl.multiple_of(x, blk)

    one_block(0, 0, 0, pl.ds(0, blk), first_ref)
    run_blocks(seq // blk - 1,
               lambda i: one_block(0, 0, i + 1, pl.ds(aligned((i + 1) * blk), blk), band_ref))

    nb1 = seg // blk
    run_blocks(d1, lambda a: one_block(1, a, 0, pl.ds(a, blk, stride=d1), first_ref))
    def later1(i):
        a = i // (nb1 - 1)
        n = i - a * (nb1 - 1) + 1
        one_block(1, a, n, pl.ds(a + n * (d1 * blk), blk, stride=d1), band_ref)
    run_blocks(d1 * (nb1 - 1), later1)

    assert sub == blk
    def only2(g):
        a = g // d1
        one_block(2, g, 0, pl.ds(a + d1 * (g - a * d1), blk, stride=d2), first_ref)
    run_blocks(d2, only2)

    rc = 256
    def merge(i, carry):
        sl = pl.ds(pl.multiple_of(i * rc, rc), rc)
        l0 = lb_ref[0, sl, :]
        l1 = lb_ref[1, sl, :]
        l2 = lb_ref[2, sl, :]
        mx = jnp.maximum(jnp.maximum(l0, l1), l2)
        e0 = jnp.exp(l0 - mx)
        e1 = jnp.exp(l1 - mx)
        e2 = jnp.exp(l2 - mx)
        tot = e0 + e1 + e2
        o_ref[sl, :] = ((e0 / tot) * ob_ref[0, sl, :] + (e1 / tot) * ob_ref[1, sl, :]
                        + (e2 / tot) * ob_ref[2, sl, :])
        return carry
    lax.fori_loop(0, seq // rc, merge, 0)


def _attention(proj, bsz, seq):
    n_tok = proj.shape[0]
    pairs = ATT_WIDTH // LANES
    assert seq % (ATT_BLOCK * max(DILATIONS)) == 0
    kv_rows = sum(seq + d * ATT_BLOCK for d in DILATIONS)
    blk = (seq, LANES)
    return pl.pallas_call(
        functools.partial(_attn_kernel, seq=seq),
        out_shape=jax.ShapeDtypeStruct((n_tok, ATT_WIDTH), F32),
        grid=(bsz, pairs),
        in_specs=[pl.BlockSpec(blk, lambda b, h: (b, h)),
                  pl.BlockSpec(blk, lambda b, h: (b, pairs + h)),
                  pl.BlockSpec(blk, lambda b, h: (b, 2 * pairs + h))],
        out_specs=pl.BlockSpec(blk, lambda b, h: (b, h)),
        scratch_shapes=[pltpu.VMEM((len(DILATIONS), seq, LANES), BF16),
                        pltpu.VMEM((kv_rows, LANES), BF16),
                        pltpu.VMEM((kv_rows, LANES), BF16),
                        pltpu.VMEM((seq, LANES), F32),
                        pltpu.VMEM((len(DILATIONS), seq, LANES), F32),
                        pltpu.VMEM((len(DILATIONS), seq, LANES), F32),
                        pltpu.VMEM((ATT_BLOCK, 2 * ATT_BLOCK), F32),
                        pltpu.VMEM((ATT_BLOCK, 2 * ATT_BLOCK), F32)],
        compiler_params=_cparams(("parallel", "parallel"), 40),
        name="attn",
    )(proj, proj, proj)


def _s5_kernel(u_ref, bm_ref, lam_ref, cm_ref, dk_ref, o_ref, us_ref, st_ref, ys_ref, carry_ref, *, tc):
    bsz = u_ref.shape[0]
    half = st_ref.shape[1] // 2
    rows = tc * bsz
    mm_rows = 512

    @pl.when(pl.program_id(1) == 0)
    def _():
        carry_ref[...] = jnp.zeros_like(carry_ref)

    for b in range(bsz):
        us_ref[pl.ds(b, tc, stride=bsz), :] = u_ref[b]

    bm = bm_ref[0]
    for r0 in range(0, rows, mm_rows):
        st_ref[r0:r0 + mm_rows, :] = jnp.dot(us_ref[r0:r0 + mm_rows, :].astype(BF16), bm,
                                             preferred_element_type=F32)

    lam = lam_ref[0]
    lam_re = lam[:, :half]
    lam_im = lam[:, half:]

    def step(t, carry):
        xr, xi = carry
        sl = pl.ds(pl.multiple_of(t * bsz, bsz), bsz)
        nr = lam_re * xr - lam_im * xi + st_ref[sl, :half]
        ni = lam_re * xi + lam_im * xr + st_ref[sl, half:]
        st_ref[sl, :half] = nr
        st_ref[sl, half:] = ni
        return nr, ni

    xr, xi = lax.fori_loop(0, tc, step, (carry_ref[:, :half], carry_ref[:, half:]), unroll=4)
    carry_ref[:, :half] = xr
    carry_ref[:, half:] = xi

    cm = cm_ref[0]
    for r0 in range(0, rows, mm_rows):
        ys_ref[r0:r0 + mm_rows, :] = jnp.dot(st_ref[r0:r0 + mm_rows, :].astype(BF16), cm,
                                             preferred_element_type=F32)
    dk = dk_ref[...]
    for b in range(bsz):
        o_ref[b] = ys_ref[pl.ds(b, tc, stride=bsz), :] + dk * u_ref[b]


def _s5_params(lam_re, lam_im, log_step, b_re, b_im, c_re, c_im, bsz):
    groups = lam_re.shape[0]
    gpc = LANES // SSM_CH
    n_chunks = groups // gpc
    lam = lax.complex(lam_re.astype(F32), lam_im.astype(F32))
    step = jnp.exp(log_step.astype(F32))[:, None]
    lam_bar = jnp.exp(lam * step)
    bmat = lax.complex(b_re.astype(F32), b_im.astype(F32))
    b_bar = ((lam_bar - 1.0) / lam)[..., None] * bmat
    eye = jnp.eye(gpc, dtype=F32)

    def block_diag_in(t):
        t = t.reshape(n_chunks, gpc, SSM_STATE, SSM_CH)
        return jnp.einsum('ngpc,gh->ngchp', t, eye).reshape(n_chunks, gpc * SSM_CH, gpc * SSM_STATE)

    def block_diag_out(t):
        t = t.reshape(n_chunks, gpc, SSM_CH, SSM_STATE)
        return jnp.einsum('ngcp,gh->ngphc', t, eye).reshape(n_chunks, gpc * SSM_STATE, gpc * SSM_CH)

    bm = jnp.concatenate([block_diag_in(b_bar.real), block_diag_in(b_bar.imag)], axis=2).astype(BF16)
    cm = jnp.concatenate([block_diag_out(c_re.astype(F32)), block_diag_out(-c_im.astype(F32))],
                         axis=1).astype(BF16)
    lam_row = jnp.concatenate([lam_bar.real.reshape(n_chunks, gpc * SSM_STATE),
                               lam_bar.imag.reshape(n_chunks, gpc * SSM_STATE)], axis=1)
    lam_t = jnp.broadcast_to(lam_row[:, None, :], (n_chunks, bsz, 2 * gpc * SSM_STATE))
    return bm, lam_t, cm, n_chunks


def _s5(proj3, u_col0, lam_re, lam_im, log_step, b_re, b_im, c_re, c_im, d_skip):
    bsz, seq, _ = proj3.shape
    assert bsz == SUBLANES
    bm, lam_t, cm, n_chunks = _s5_params(lam_re, lam_im, log_step, b_re, b_im, c_re, c_im, bsz)
    width = n_chunks * LANES
    tc = 256
    st_cols = bm.shape[2]
    ublk0 = u_col0 // LANES
    return pl.pallas_call(
        functools.partial(_s5_kernel, tc=tc),
        out_shape=jax.ShapeDtypeStruct((bsz, seq, width), F32),
        grid=(n_chunks, seq // tc),
        in_specs=[pl.BlockSpec((bsz, tc, LANES), lambda c, t: (0, t, ublk0 + c)),
                  pl.BlockSpec((1, LANES, st_cols), lambda c, t: (c, 0, 0)),
                  pl.BlockSpec((1, bsz, st_cols), lambda c, t: (c, 0, 0)),
                  pl.BlockSpec((1, st_cols, LANES), lambda c, t: (c, 0, 0)),
                  pl.BlockSpec((1, LANES), lambda c, t: (0, c))],
        out_specs=pl.BlockSpec((bsz, tc, LANES), lambda c, t: (0, t, c)),
        scratch_shapes=[pltpu.VMEM((tc * bsz, LANES), F32),
                        pltpu.VMEM((tc * bsz, st_cols), F32),
                        pltpu.VMEM((tc * bsz, LANES), F32),
                        pltpu.VMEM((bsz, st_cols), F32)],
        compiler_params=_cparams(("arbitrary", "arbitrary"), 40),
        name="s5",
    )(proj3, bm, lam_t, cm, d_skip.reshape(1, width).astype(F32))


def _layer_norm(v, g, b):
    mu = jnp.mean(v, axis=-1, keepdims=True)
    var = jnp.mean(jnp.square(v - mu), axis=-1, keepdims=True)
    return (v - mu) * lax.rsqrt(var + LN_EPS) * g + b


def _rms_norm(v, g):
    return v * lax.rsqrt(jnp.mean(jnp.square(v), axis=-1, keepdims=True) + RMS_EPS) * g


def _mixout_kernel(att_ref, ssm_ref, x_ref, wglu_ref, bglu_ref, ag_ref, sg_ref, wout_ref, g_ref, b_ref,
                   o_ref, *, alpha):
    y = jax.nn.gelu(ssm_ref[...])
    z = jnp.dot(y.astype(BF16), wglu_ref[...], preferred_element_type=F32) + bglu_ref[...]
    o_ssm = y * jax.nn.sigmoid(z)
    a = _rms_norm(att_ref[...], ag_ref[...]).astype(BF16)
    s = _rms_norm(o_ssm, sg_ref[...]).astype(BF16)
    wa = att_ref.shape[1]
    mix = (jnp.dot(a, wout_ref[:wa, :], preferred_element_type=F32)
           + jnp.dot(s, wout_ref[wa:, :], preferred_element_type=F32))
    o_ref[...] = _layer_norm(alpha * x_ref[...] + mix, g_ref[...], b_ref[...])


def _mixout(o_att, y_ssm, x2d, w_glu, b_glu, att_g, ssm_g, w_out, ln_g, ln_b, alpha):
    n_tok, d = x2d.shape
    wa = o_att.shape[1]
    ws = y_ssm.shape[1]
    tm = 256
    row = lambda w: pl.BlockSpec((tm, w), lambda i: (i, 0))
    full = lambda a: pl.BlockSpec(a.shape, lambda i: (0,) * a.ndim)
    args = (o_att, y_ssm, x2d, w_glu.astype(BF16), b_glu.reshape(1, ws), att_g.reshape(1, wa),
            ssm_g.reshape(1, ws), w_out.astype(BF16), ln_g.reshape(1, d), ln_b.reshape(1, d))
    return pl.pallas_call(
        functools.partial(_mixout_kernel, alpha=alpha),
        out_shape=jax.ShapeDtypeStruct((n_tok, d), F32),
        grid=(n_tok // tm,),
        in_specs=[row(wa), row(ws), row(d)] + [full(a) for a in args[3:]],
        out_specs=row(d),
        compiler_params=_cparams(("parallel",), 32),
        name="mixout",
    )(*args)


def _split_bf16(v):
    hi = v.astype(BF16)
    lo = (v - hi.astype(F32)).astype(BF16)
    return hi, lo


def _router_kernel(h_ref, wt_ref, bias_ref, e_ref, g_ref, r_ref, cnt_ref, run_ref):
    tm = h_ref.shape[0]
    n_exp = wt_ref.shape[0]
    gsz = n_exp // N_EXPERT_GROUPS

    @pl.when(pl.program_id(0) == 0)
    def _():
        run_ref[...] = jnp.zeros_like(run_ref)

    w_hi, w_lo = _split_bf16(wt_ref[...])
    h_hi, h_lo = _split_bf16(h_ref[...])
    nt = (((1,), (1,)), ((), ()))
    logits = (lax.dot_general(w_hi, h_hi, nt, preferred_element_type=F32)
              + lax.dot_general(w_hi, h_lo, nt, preferred_element_type=F32)
              + lax.dot_general(w_lo, h_hi, nt, preferred_element_type=F32))
    scores = jax.nn.sigmoid(logits)
    choice = scores + bias_ref[:, 0:1]

    gio = lax.broadcasted_iota(jnp.int32, (gsz, tm), 0).astype(F32)
    gscore = []
    for g in range(N_EXPERT_GROUPS):
        cg = choice[g * gsz:(g + 1) * gsz, :]
        m1 = jnp.max(cg, axis=0, keepdims=True)
        i1 = jnp.min(jnp.where(cg == m1, gio, float(gsz)), axis=0, keepdims=True)
        m2 = jnp.max(jnp.where(gio == i1, NEG_INF, cg), axis=0, keepdims=True)
        gscore.append(m1 + m2)
    masked = []
    for g in range(N_EXPERT_GROUPS):
        beat = jnp.zeros((1, tm), F32)
        for o in range(N_EXPERT_GROUPS):
            if o == g:
                continue
            wins = (gscore[o] >= gscore[g]) if o < g else (gscore[o] > gscore[g])
            beat = beat + jnp.where(wins, 1.0, 0.0)
        keep = beat < float(TOPK_GROUPS)
        masked.append(jnp.where(keep, choice[g * gsz:(g + 1) * gsz, :], NEG_INF))
    cur = jnp.concatenate(masked, axis=0)

    eio = lax.broadcasted_iota(jnp.int32, (n_exp, tm), 0).astype(F32)
    idxs = []
    gates = []
    onehot = jnp.zeros((n_exp, tm), F32)
    for _ in range(TOP_K):
        m = jnp.max(cur, axis=0, keepdims=True)
        idx = jnp.min(jnp.where(cur == m, eio, float(n_exp)), axis=0, keepdims=True)
        hit = eio == idx
        idxs.append(idx)
        gates.append(jnp.sum(jnp.where(hit, scores, 0.0), axis=0, keepdims=True))
        cur = jnp.where(hit, NEG_INF, cur)
        onehot = onehot + jnp.where(hit, 1.0, 0.0)
    gate = jnp.concatenate(gates, axis=0)
    gate = ROUTED_SCALE * gate / (jnp.sum(gate, axis=0, keepdims=True) + 1e-20)

    si = lax.broadcasted_iota(jnp.int32, (tm, tm), 0)
    ti = lax.broadcasted_iota(jnp.int32, (tm, tm), 1)
    upper = jnp.where(si < ti, 1.0, 0.0).astype(BF16)
    before = jnp.dot(onehot.astype(BF16), upper, preferred_element_type=F32) + run_ref[:, 0:1]
    ranks = [jnp.sum(jnp.where(eio == idx, before, 0.0), axis=0, keepdims=True) for idx in idxs]

    e_ref[...] = jnp.concatenate(idxs, axis=0).astype(jnp.int32)
    g_ref[...] = gate
    r_ref[...] = jnp.concatenate(ranks, axis=0).astype(jnp.int32)
    run_ref[...] = run_ref[...] + jnp.sum(onehot, axis=1, keepdims=True)
    cnt_ref[...] = run_ref[...]


def _router(h, router_w, router_bias):
    n_tok, d = h.shape
    n_exp = router_w.shape[1]
    tm = 256
    wt = router_w.astype(F32).T
    bias = jnp.broadcast_to(router_bias.astype(F32)[:, None], (n_exp, LANES))
    tok = pl.BlockSpec((TOP_K, tm), lambda i: (0, i))
    return pl.pallas_call(
        _router_kernel,
        out_shape=(jax.ShapeDtypeStruct((TOP_K, n_tok), jnp.int32),
                   jax.ShapeDtypeStruct((TOP_K, n_tok), F32),
                   jax.ShapeDtypeStruct((TOP_K, n_tok), jnp.int32),
                   jax.ShapeDtypeStruct((n_exp, LANES), F32)),
        grid=(n_tok // tm,),
        in_specs=[pl.BlockSpec((tm, d), lambda i: (i, 0)),
                  pl.BlockSpec((n_exp, d), lambda i: (0, 0)),
                  pl.BlockSpec((n_exp, LANES), lambda i: (0, 0))],
        out_specs=(tok, tok, tok, pl.BlockSpec((n_exp, LANES), lambda i: (0, 0))),
        scratch_shapes=[pltpu.VMEM((n_exp, LANES), F32)],
        compiler_params=_cparams(("arbitrary",), 32),
        name="router",
    )(h, wt, bias)


def _dest_kernel(e_ref, r_ref, st_ref, d_ref):
    n_exp = st_ref.shape[0]
    tm = e_ref.shape[1]
    eio = lax.broadcasted_iota(jnp.int32, (n_exp, tm), 0)
    start = st_ref[:, 0:1]
    rows = [jnp.sum(jnp.where(eio == e_ref[k:k + 1, :], start, 0.0), axis=0, keepdims=True)
            for k in range(TOP_K)]
    d_ref[...] = jnp.concatenate(rows, axis=0).astype(jnp.int32) + r_ref[...]


def _dest(top_e, rank, starts):
    n_tok = top_e.shape[1]
    n_exp = starts.shape[0]
    tm = 512
    st = jnp.broadcast_to(starts.astype(F32)[:, None], (n_exp, LANES))
    tok = pl.BlockSpec((TOP_K, tm), lambda i: (0, i))
    return pl.pallas_call(
        _dest_kernel,
        out_shape=jax.ShapeDtypeStruct((TOP_K, n_tok), jnp.int32),
        grid=(n_tok // tm,),
        in_specs=[tok, tok, pl.BlockSpec((n_exp, LANES), lambda i: (0, 0))],
        out_specs=tok,
        compiler_params=_cparams(("parallel",), 32),
        name="dest",
    )(top_e, rank, st)


def _pack_bf16_pairs(val):
    half = val.shape[1] // 2
    lo = pltpu.bitcast(val[:, :half].astype(BF16).astype(F32), U32)
    hi = pltpu.bitcast(val[:, half:].astype(BF16).astype(F32), U32)
    return (lo >> 16) | (hi & jnp.uint32(0xFFFF0000))


def _unpack_bf16_pairs(words):
    lo = pltpu.bitcast(words << 16, F32)
    hi = pltpu.bitcast(words & jnp.uint32(0xFFFF0000), F32)
    return jnp.concatenate([lo, hi], axis=1)


def _to_row_tiles(dst_ref, slot, val):
    rows = val.shape[0]
    words = _pack_bf16_pairs(val)
    for j in range(ROW_TILE):
        dst_ref[slot, pl.ds(j, rows, stride=ROW_TILE), :] = words[:, j * LANES:(j + 1) * LANES]


def _row_tile_words(src_ref, idx, rows):
    return jnp.concatenate([src_ref[(*idx, pl.ds(j, rows, stride=ROW_TILE), slice(None))]
                            for j in range(ROW_TILE)], axis=1)


def _row_tile(r):
    return pl.ds(pl.multiple_of(r * ROW_TILE, ROW_TILE), ROW_TILE)


def _dispatch_kernel(dest_ref, h_ref, xs_ref, ht_ref, sem):
    tm = h_ref.shape[0]
    i = pl.program_id(0)
    cur = i % 2
    _to_row_tiles(ht_ref, cur, h_ref[...])

    def issue(t, carry):
        for k in range(TOP_K):
            pltpu.make_async_copy(ht_ref.at[cur, _row_tile(t)], xs_ref.at[_row_tile(dest_ref[t * TOP_K + k])],
                                  sem.at[cur]).start(priority=k % 2)
        return carry
    lax.fori_loop(0, tm, issue, 0, unroll=2)

    def drain(slot):
        for k in range(TOP_K):
            pltpu.make_async_copy(ht_ref.at[slot], xs_ref.at[pl.ds(0, tm * ROW_TILE)], sem.at[slot]).wait()

    @pl.when(i > 0)
    def _():
        drain(1 - cur)

    @pl.when(i == pl.num_programs(0) - 1)
    def _():
        drain(cur)


def _dispatch(h, dest, n_exp):
    n_tok, d = h.shape
    assert d == 2 * ROW_TILE * LANES
    tm = MOE_TOKEN_TILE
    n_rows = n_tok * TOP_K + n_exp * EXPERT_ROWS
    return pl.pallas_call(
        _dispatch_kernel,
        out_shape=jax.ShapeDtypeStruct((n_rows * ROW_TILE, LANES), U32),
        grid=(n_tok // tm,),
        in_specs=[pl.BlockSpec((TOP_K * tm,), lambda i: (i,), memory_space=pltpu.SMEM),
                  pl.BlockSpec((tm, d), lambda i: (i, 0))],
        out_specs=pl.BlockSpec(memory_space=pl.ANY),
        scratch_shapes=[pltpu.VMEM((2, tm * ROW_TILE, LANES), U32), pltpu.SemaphoreType.DMA((2,))],
        compiler_params=_cparams(("arbitrary",), 32),
        name="dispatch",
    )(dest, h)


def _experts_kernel(bstart_ref, bend_ref, cnt_ref, nblk_ref, xs_ref, wgu_ref, wdn_ref, ys_ref,
                    xbuf, ybuf, act_ref, wgu_bf, wdn_bf, xsem, ysem):
    e = pl.program_id(0)
    n_blk = nblk_ref[0]
    trows = xbuf.shape[1]
    rows = trows // ROW_TILE
    ff = wdn_bf.shape[0]
    b0 = bstart_ref[e]
    b1 = bend_ref[e]

    def block_rows(b):
        return pl.ds(pl.multiple_of(b * trows, trows), trows)

    def x_copy(b):
        slot = b % EXPERT_RING
        return pltpu.make_async_copy(xs_ref.at[block_rows(b)], xbuf.at[slot], xsem.at[slot])

    def y_copy(b):
        slot = b % EXPERT_OUT_RING
        return pltpu.make_async_copy(ybuf.at[slot], ys_ref.at[block_rows(b)], ysem.at[slot])

    @pl.when(e == 0)
    def _():
        for i in range(EXPERT_RING):
            @pl.when(i < n_blk)
            def _():
                x_copy(i).start(priority=1)

    wgu_bf[...] = wgu_ref[0].astype(BF16)
    wdn_bf[...] = wdn_ref[0].astype(BF16)

    def up(blocks):
        for b in blocks:
            x_copy(b).wait()
        for b in blocks:
            words = _row_tile_words(xbuf, (b % EXPERT_RING,), rows)
            row = lax.broadcasted_iota(jnp.int32, (rows, 1), 0)
            words = jnp.where(row < cnt_ref[e] - (b - b0) * rows, words, jnp.uint32(0))
            xb = _unpack_bf16_pairs(words).astype(BF16)
            gu = jnp.dot(xb, wgu_bf[...], preferred_element_type=F32)
            act_ref[b - b0] = (jax.nn.silu(gu[:, :ff]) * gu[:, ff:]).astype(BF16)
        for b in blocks:
            @pl.when(b + EXPERT_RING < n_blk)
            def _():
                x_copy(b + EXPERT_RING).start(priority=1)

    def up_pair(p, carry):
        up((b0 + 2 * p, b0 + 2 * p + 1))
        return carry
    n_mine = b1 - b0
    lax.fori_loop(0, n_mine // 2, up_pair, 0)

    @pl.when(n_mine % 2 == 1)
    def _():
        up((b1 - 1,))

    def down(blocks):
        for b in blocks:
            @pl.when(b >= EXPERT_OUT_RING)
            def _():
                y_copy(b - EXPERT_OUT_RING).wait()
        for b in blocks:
            _to_row_tiles(ybuf, b % EXPERT_OUT_RING,
                          jnp.dot(act_ref[b - b0], wdn_bf[...], preferred_element_type=F32))
        for b in blocks:
            y_copy(b).start(priority=1)
        last = blocks[-1]

        @pl.when(last == n_blk - 1)
        def _():
            for i in range(EXPERT_OUT_RING):
                @pl.when(last >= i)
                def _():
                    y_copy(last - i).wait()

    def down_pair(p, carry):
        down((b0 + 2 * p, b0 + 2 * p + 1))
        return carry
    lax.fori_loop(0, n_mine // 2, down_pair, 0)

    @pl.when(n_mine % 2 == 1)
    def _():
        down((b1 - 1,))


def _expert_blocks(counts):
    blocks = (counts + EXPERT_ROWS - 1) // EXPERT_ROWS
    bend = jnp.cumsum(blocks)
    bstart = bend - blocks
    i32 = lambda a: a.astype(jnp.int32)
    return i32(bstart), i32(bend), i32(bend[-1]).reshape(1), i32(bstart * EXPERT_ROWS)


def _experts(xs, bstart, bend, counts, n_blk, w_gu, w_down):
    n_exp, d, ff2 = w_gu.shape
    ff = w_down.shape[1]
    n_rows = xs.shape[0] // ROW_TILE
    assert n_rows % EXPERT_ROWS == 0 and d == 2 * ROW_TILE * LANES
    max_blocks = (n_rows - n_exp * EXPERT_ROWS) // TOP_K // EXPERT_ROWS + 1
    grid_spec = pltpu.PrefetchScalarGridSpec(
        num_scalar_prefetch=4,
        grid=(n_exp,),
        in_specs=[pl.BlockSpec(memory_space=pl.ANY),
                  pl.BlockSpec((1, d, ff2), lambda e, *_: (e, 0, 0)),
                  pl.BlockSpec((1, ff, d), lambda e, *_: (e, 0, 0))],
        out_specs=pl.BlockSpec(memory_space=pl.ANY),
        scratch_shapes=[pltpu.VMEM((EXPERT_RING, EXPERT_ROWS * ROW_TILE, LANES), U32),
                        pltpu.VMEM((EXPERT_OUT_RING, EXPERT_ROWS * ROW_TILE, LANES), U32),
                        pltpu.VMEM((max_blocks, EXPERT_ROWS, ff), BF16),
                        pltpu.VMEM((d, ff2), BF16), pltpu.VMEM((ff, d), BF16),
                        pltpu.SemaphoreType.DMA((EXPERT_RING,)),
                        pltpu.SemaphoreType.DMA((EXPERT_OUT_RING,))],
    )
    return pl.pallas_call(
        _experts_kernel,
        out_shape=jax.ShapeDtypeStruct(xs.shape, U32),
        grid_spec=grid_spec,
        compiler_params=_cparams(("arbitrary",), 40),
        name="experts",
    )(bstart, bend, counts, n_blk, xs, w_gu, w_down)


def _combine_kernel(dest_ref, dnext_ref, gate_ref, h_ref, ys_ref, wgu_ref, wdn_ref, g_ref, b_ref, o_ref,
                    buf_ref, routed_ref, sem, *, alpha):
    tm = h_ref.shape[0]
    i = pl.program_id(0)
    cur = i % 2
    chunk = SUBLANES

    def issue(d_ref, slot, t):
        for k in range(TOP_K):
            pltpu.make_async_copy(ys_ref.at[_row_tile(d_ref[t * TOP_K + k])], buf_ref.at[slot, k, _row_tile(t)],
                                  sem.at[slot]).start(priority=k % 2)

    @pl.when(i == 0)
    def _():
        def first(t, carry):
            issue(dest_ref, 0, t)
            return carry
        lax.fori_loop(0, tm, first, 0, unroll=2)

    for k in range(TOP_K):
        pltpu.make_async_copy(ys_ref.at[pl.ds(0, tm * ROW_TILE)], buf_ref.at[cur, k], sem.at[cur]).wait()

    def weighted_sum(c):
        tok = pl.ds(pl.multiple_of(c * chunk, chunk), chunk)
        gate = gate_ref[tok, :]
        total = None
        for k in range(TOP_K):
            words = jnp.concatenate(
                [buf_ref[cur, k, pl.ds(pl.multiple_of(c * (chunk * ROW_TILE), chunk * ROW_TILE) + j, chunk,
                                       stride=ROW_TILE), :] for j in range(ROW_TILE)], axis=1)
            term = gate[:, k:k + 1] * _unpack_bf16_pairs(words)
            total = term if total is None else total + term
        routed_ref[tok, :] = total

    @pl.when(i + 1 < pl.num_programs(0))
    def _():
        def body(c, carry):
            for t in range(chunk):
                issue(dnext_ref, 1 - cur, c * chunk + t)
            weighted_sum(c)
            return carry
        lax.fori_loop(0, tm // chunk, body, 0)

    @pl.when(i + 1 == pl.num_programs(0))
    def _():
        def body(c, carry):
            weighted_sum(c)
            return carry
        lax.fori_loop(0, tm // chunk, body, 0)

    h = h_ref[...]
    ff = wdn_ref.shape[0]
    gu = jnp.dot(h.astype(BF16), wgu_ref[...], preferred_element_type=F32)
    act = (jax.nn.silu(gu[:, :ff]) * gu[:, ff:]).astype(BF16)
    acc = alpha * h + jnp.dot(act, wdn_ref[...], preferred_element_type=F32) + routed_ref[...]
    o_ref[...] = _layer_norm(acc, g_ref[...], b_ref[...])


def _combine(h, ys, dest, gate_t, shared_w_gu, shared_w_down, ln_g, ln_b, alpha):
    n_tok, d = h.shape
    tm = MOE_TOKEN_TILE
    n_tiles = n_tok // tm
    full = lambda a: pl.BlockSpec(a.shape, lambda i: (0,) * a.ndim)
    wgu = shared_w_gu.astype(BF16)
    wdn = shared_w_down.astype(BF16)
    g2 = ln_g.reshape(1, d)
    b2 = ln_b.reshape(1, d)
    return pl.pallas_call(
        functools.partial(_combine_kernel, alpha=alpha),
        out_shape=jax.ShapeDtypeStruct((n_tok, d), F32),
        grid=(n_tiles,),
        in_specs=[pl.BlockSpec((TOP_K * tm,), lambda i: (i,), memory_space=pltpu.SMEM),
                  pl.BlockSpec((TOP_K * tm,), lambda i: (jnp.minimum(i + 1, n_tiles - 1),),
                               memory_space=pltpu.SMEM),
                  pl.BlockSpec((tm, TOP_K), lambda i: (i, 0)),
                  pl.BlockSpec((tm, d), lambda i: (i, 0)),
                  pl.BlockSpec(memory_space=pl.ANY),
                  full(wgu), full(wdn), full(g2), full(b2)],
        out_specs=pl.BlockSpec((tm, d), lambda i: (i, 0)),
        scratch_shapes=[pltpu.VMEM((2, TOP_K, tm * ROW_TILE, LANES), U32), pltpu.VMEM((tm, d), F32),
                        pltpu.SemaphoreType.DMA((2,))],
        compiler_params=_cparams(("arbitrary",), 48),
        name="combine",
    )(dest, dest, gate_t, h, ys, wgu, wdn, g2, b2)


def _moe(h, router_w, router_bias, w_gu, w_down, shared_w_gu, shared_w_down, ln_g, ln_b, alpha):
    top_e, gate, rank, cnt = _router(h, router_w, router_bias)
    counts = cnt[:, 0].astype(jnp.int32)
    bstart, bend, n_blk, pad_start = _expert_blocks(counts)
    dest = _dest(top_e, rank, pad_start)
    dest_tiles = dest.T.reshape(-1)
    xs = _dispatch(h, dest_tiles, counts.shape[0])
    ys = _experts(xs, bstart, bend, counts, n_blk, w_gu, w_down)
    return _combine(h, ys, dest_tiles, gate.T, shared_w_gu, shared_w_down, ln_g, ln_b, alpha)


def kernel(x, w_in, att_norm_g, lam_re, lam_im, log_step, b_re, b_im, c_re, c_im, d_skip, w_glu, b_glu,
           ssm_norm_g, w_out, ln1_g, ln1_b, router_w, router_bias, w_gu, w_down, shared_w_gu,
           shared_w_down, ln2_g, ln2_b):
    bsz, seq, d = x.shape
    depth = w_in.shape[0]
    alpha = (2 * depth) ** 0.25
    h = x.reshape(bsz * seq, d)
    for i in range(depth):
        proj = _inproj(h, w_in[i].astype(BF16), seq)
        o_att = _attention(proj, bsz, seq)
        y_ssm = _s5(proj.reshape(bsz, seq, -1), 3 * ATT_WIDTH, lam_re[i], lam_im[i], log_step[i],
                    b_re[i], b_im[i], c_re[i], c_im[i], d_skip[i])
        h = _mixout(o_att, y_ssm.reshape(bsz * seq, -1), h, w_glu[i], b_glu[i], att_norm_g[i],
                    ssm_norm_g[i], w_out[i], ln1_g[i], ln1_b[i], alpha)
        h = _moe(h, router_w[i], router_bias[i], w_gu[i], w_down[i], shared_w_gu[i], shared_w_down[i],
                 ln2_g[i], ln2_b[i], alpha)
    return h.reshape(bsz, seq, d)
```

```python
import functools
import math

import jax
import jax.numpy as jnp
from jax import lax
from jax.experimental import pallas as pl
from jax.experimental.pallas import tpu as pltpu

F32 = jnp.float32
BF16 = jnp.bfloat16
U32 = jnp.uint32

ATT_HEADS = 8
HEAD_DIM = 64
ATT_WIDTH = ATT_HEADS * HEAD_DIM
SSM_CH = 16
SSM_STATE = 64
ROPE_THETA = 500000.0
ROT_DIM = HEAD_DIM // 4
DILATIONS = (1, 4, 16)
ATT_BLOCK = 128
ATT_GROUP = 16
N_EXPERTS = 256
TOP_K = 8
N_EXPERT_GROUPS = 8
TOPK_GROUPS = 4
ROUTED_SCALE = 2.5
LN_EPS = 1e-5
RMS_EPS = 1e-6

LANES = 128
SUBLANES = 8
EXPERT_ROWS = 128
MOE_TOKEN_TILE = 512
EXPERT_RING = 16
EXPERT_OUT_RING = 16
ROW_TILE = 4
NEG_INF = float("-inf")


def _cparams(sem, vmem_mb):
    return pltpu.CompilerParams(dimension_semantics=sem, vmem_limit_bytes=vmem_mb * 1024 * 1024)


def _inproj_kernel(x_ref, w_ref, cos_ref, sa_ref, sb_ref, o_ref, *, n_rot_cols):
    xb = x_ref[...].astype(BF16)
    cosf = cos_ref[...]
    sa = sa_ref[...]
    sb = sb_ref[...]
    width = o_ref.shape[1]
    chunk = 512
    for c in range(width // chunk):
        r = jnp.dot(xb, w_ref[:, c * chunk:(c + 1) * chunk], preferred_element_type=F32)
        if c * chunk < n_rot_cols:
            parts = []
            for s in range(chunk // LANES):
                t = r[:, s * LANES:(s + 1) * LANES]
                parts.append(t * cosf + pltpu.roll(t, LANES - ROT_DIM // 2, 1) * sa
                             + pltpu.roll(t, ROT_DIM // 2, 1) * sb)
            r = jnp.concatenate(parts, axis=1)
        o_ref[:, c * chunk:(c + 1) * chunk] = r


def _rope_lane_tables(seq):
    half = ROT_DIM // 2
    inv_freq = jnp.power(jnp.float32(ROPE_THETA), -jnp.arange(half, dtype=F32) / half)
    ang = jnp.arange(seq, dtype=F32)[:, None] * inv_freq[None, :]
    cos, sin = jnp.cos(ang), jnp.sin(ang)
    rest = HEAD_DIM - ROT_DIM
    cos_h = jnp.concatenate([cos, cos, jnp.ones((seq, rest), F32)], axis=1)
    sa_h = jnp.concatenate([-sin, jnp.zeros((seq, half + rest), F32)], axis=1)
    sb_h = jnp.concatenate([jnp.zeros((seq, half), F32), sin, jnp.zeros((seq, rest), F32)], axis=1)
    rep = LANES // HEAD_DIM
    return tuple(jnp.tile(t, (1, rep)) for t in (cos_h, sa_h, sb_h))


def _inproj(x2d, w_in_bf, seq):
    n_tok, d = x2d.shape
    width = w_in_bf.shape[1]
    tm = 512
    cosf, sa, sb = _rope_lane_tables(seq)
    tab_spec = pl.BlockSpec((tm, LANES), lambda i: (i % (seq // tm), 0))
    return pl.pallas_call(
        functools.partial(_inproj_kernel, n_rot_cols=2 * ATT_WIDTH),
        out_shape=jax.ShapeDtypeStruct((n_tok, width), F32),
        grid=(n_tok // tm,),
        in_specs=[pl.BlockSpec((tm, d), lambda i: (i, 0)),
                  pl.BlockSpec((d, width), lambda i: (0, 0)),
                  tab_spec, tab_spec, tab_spec],
        out_specs=pl.BlockSpec((tm, width), lambda i: (i, 0)),
        compiler_params=_cparams(("parallel",), 48),
        name="inproj",
    )(x2d, w_in_bf, cosf, sa, sb)


def _attn_kernel(q_ref, k_ref, v_ref, o_ref, qs_ref, ks_ref, vs_ref, tmp_ref, ob_ref, lb_ref, band_ref,
                 first_ref, *, seq):
    blk = ATT_BLOCK
    lane = lax.broadcasted_iota(jnp.int32, (1, LANES), 1)
    head0 = lane < HEAD_DIM
    scale = HEAD_DIM ** -0.5
    d1, d2 = DILATIONS[1], DILATIONS[2]
    assert DILATIONS[0] == 1 and d2 == d1 * d1
    seg = seq // d1
    sub = seg // d1

    qi = lax.broadcasted_iota(jnp.int32, (blk, 2 * blk), 0)
    kj = lax.broadcasted_iota(jnp.int32, (blk, 2 * blk), 1)
    dist = qi + blk - kj
    band_ref[...] = jnp.where((dist >= 0) & (dist <= blk), 0.0, NEG_INF)
    first_ref[...] = jnp.where((dist >= 0) & (kj >= blk), 0.0, NEG_INF)

    n_class = (1, d1, d2)
    class_len = (seq, seg, sub)
    base = [0]
    for c in range(len(DILATIONS)):
        base.append(base[c] + n_class[c] * (class_len[c] + blk))

    def kv_row0(c, g):
        return base[c] + g * (class_len[c] + blk)

    qs_ref[0] = (q_ref[...] * scale).astype(BF16)
    for a in range(d1):
        x = q_ref[pl.ds(a, seg, stride=d1), :] * scale
        tmp_ref[a * seg:(a + 1) * seg, :] = x
        qs_ref[1, a * seg:(a + 1) * seg, :] = x.astype(BF16)
    for g in range(d2):
        qs_ref[2, g * sub:(g + 1) * sub, :] = tmp_ref[pl.ds((g // d1) * seg + g % d1, sub, stride=d1),
                                                      :].astype(BF16)
    for src_ref, dst_ref in ((k_ref, ks_ref), (v_ref, vs_ref)):
        for c in range(len(DILATIONS)):
            for g in range(n_class[c]):
                dst_ref[kv_row0(c, g):kv_row0(c, g) + blk, :] = jnp.zeros((blk, LANES), BF16)
        dst_ref[kv_row0(0, 0) + blk:kv_row0(0, 0) + blk + seq, :] = src_ref[...].astype(BF16)
        for a in range(d1):
            x = src_ref[pl.ds(a, seg, stride=d1), :]
            tmp_ref[a * seg:(a + 1) * seg, :] = x
            dst_ref[kv_row0(1, a) + blk:kv_row0(1, a) + blk + seg, :] = x.astype(BF16)
        for g in range(d2):
            dst_ref[kv_row0(2, g) + blk:kv_row0(2, g) + blk + sub, :] = tmp_ref[
                pl.ds((g // d1) * seg + g % d1, sub, stride=d1), :].astype(BF16)

    def one_block(c, g, n, out_rows, bias_ref):
        q = qs_ref[c, pl.ds(aligned(g * class_len[c] + n * blk), blk), :]
        kv_rows = pl.ds(aligned(kv_row0(c, g) + n * blk), 2 * blk)
        kk = ks_ref[kv_rows, :]
        vv = vs_ref[kv_rows, :]
        outs = []
        lses = []
        for h in range(LANES // HEAD_DIM):
            hm = head0 if h == 0 else jnp.logical_not(head0)
            qh = jnp.where(hm, q, jnp.zeros_like(q))
            s = lax.dot_general(qh, kk, (((1,), (1,)), ((), ())), preferred_element_type=F32)
            s = s + bias_ref[...]
            m = jnp.max(s, axis=-1, keepdims=True)
            p = jnp.exp(s - m)
            den = jnp.sum(p, axis=-1, keepdims=True)
            outs.append(jnp.dot(p.astype(BF16), vv, preferred_element_type=F32) / den)
            lses.append(m + jnp.log(den))
        ob_ref[c, out_rows, :] = jnp.where(head0, outs[0], outs[1])
        lb_ref[c, out_rows, :] = jnp.where(head0, lses[0], lses[1])

    def run_blocks(n_blocks, fn):
        group = max(g for g in range(1, ATT_GROUP + 1) if n_blocks % g == 0)
        if n_blocks == group:
            for g in range(group):
                fn(g)
            return
        def body(it, carry):
            for g in range(group):
                fn(it * group + g)
            return carry
        lax.fori_loop(0, n_blocks // group, body, 0)

    def aligned(x):
        return x if isinstance(x, int) else pl.multiple_of(x, blk)

    one_block(0, 0, 0, pl.ds(0, blk), first_ref)
    run_blocks(seq // blk - 1,
               lambda i: one_block(0, 0, i + 1, pl.ds(aligned((i + 1) * blk), blk), band_ref))

    nb1 = seg // blk
    run_blocks(d1, lambda a: one_block(1, a, 0, pl.ds(a, blk, stride=d1), first_ref))
    def later1(i):
        a = i // (nb1 - 1)
        n = i - a * (nb1 - 1) + 1
        one_block(1, a, n, pl.ds(a + n * (d1 * blk), blk, stride=d1), band_ref)
    run_blocks(d1 * (nb1 - 1), later1)

    assert sub == blk
    def only2(g):
        a = g // d1
        one_block(2, g, 0, pl.ds(a + d1 * (g - a * d1), blk, stride=d2), first_ref)
    run_blocks(d2, only2)

    rc = 256
    def merge(i, carry):
        sl = pl.ds(pl.multiple_of(i * rc, rc), rc)
        l0 = lb_ref[0, sl, :]
        l1 = lb_ref[1, sl, :]
        l2 = lb_ref[2, sl, :]
        mx = jnp.maximum(jnp.maximum(l0, l1), l2)
        e0 = jnp.exp(l0 - mx)
        e1 = jnp.exp(l1 - mx)
        e2 = jnp.exp(l2 - mx)
        tot = e0 + e1 + e2
        o_ref[sl, :] = ((e0 / tot) * ob_ref[0, sl, :] + (e1 / tot) * ob_ref[1, sl, :]
                        + (e2 / tot) * ob_ref[2, sl, :])
        return carry
    lax.fori_loop(0, seq // rc, merge, 0)


def _attention(proj, bsz, seq):
    n_tok = proj.shape[0]
    pairs = ATT_WIDTH // LANES
    assert seq % (ATT_BLOCK * max(DILATIONS)) == 0
    kv_rows = sum(seq + d * ATT_BLOCK for d in DILATIONS)
    blk = (seq, LANES)
    return pl.pallas_call(
        functools.partial(_attn_kernel, seq=seq),
        out_shape=jax.ShapeDtypeStruct((n_tok, ATT_WIDTH), F32),
        grid=(bsz, pairs),
        in_specs=[pl.BlockSpec(blk, lambda b, h: (b, h)),
                  pl.BlockSpec(blk, lambda b, h: (b, pairs + h)),
                  pl.BlockSpec(blk, lambda b, h: (b, 2 * pairs + h))],
        out_specs=pl.BlockSpec(blk, lambda b, h: (b, h)),
        scratch_shapes=[pltpu.VMEM((len(DILATIONS), seq, LANES), BF16),
                        pltpu.VMEM((kv_rows, LANES), BF16),
                        pltpu.VMEM((kv_rows, LANES), BF16),
                        pltpu.VMEM((seq, LANES), F32),
                        pltpu.VMEM((len(DILATIONS), seq, LANES), F32),
                        pltpu.VMEM((len(DILATIONS), seq, LANES), F32),
                        pltpu.VMEM((ATT_BLOCK, 2 * ATT_BLOCK), F32),
                        pltpu.VMEM((ATT_BLOCK, 2 * ATT_BLOCK), F32)],
        compiler_params=_cparams(("parallel", "parallel"), 40),
        name="attn",
    )(proj, proj, proj)


def _s5_kernel(u_ref, bm_ref, lam_ref, cm_ref, dk_ref, o_ref, us_ref, st_ref, ys_ref, carry_ref, *, tc):
    bsz = u_ref.shape[0]
    half = st_ref.shape[1] // 2
    rows = tc * bsz
    mm_rows = 512

    @pl.when(pl.program_id(1) == 0)
    def _():
        carry_ref[...] = jnp.zeros_like(carry_ref)

    for b in range(bsz):
        us_ref[pl.ds(b, tc, stride=bsz), :] = u_ref[b]

    bm = bm_ref[0]
    for r0 in range(0, rows, mm_rows):
        st_ref[r0:r0 + mm_rows, :] = jnp.dot(us_ref[r0:r0 + mm_rows, :].astype(BF16), bm,
                                             preferred_element_type=F32)

    lam = lam_ref[0]
    lam_re = lam[:, :half]
    lam_im = lam[:, half:]

    def step(t, carry):
        xr, xi = carry
        sl = pl.ds(pl.multiple_of(t * bsz, bsz), bsz)
        nr = lam_re * xr - lam_im * xi + st_ref[sl, :half]
        ni = lam_re * xi + lam_im * xr + st_ref[sl, half:]
        st_ref[sl, :half] = nr
        st_ref[sl, half:] = ni
        return nr, ni

    xr, xi = lax.fori_loop(0, tc, step, (carry_ref[:, :half], carry_ref[:, half:]), unroll=4)
    carry_ref[:, :half] = xr
    carry_ref[:, half:] = xi

    cm = cm_ref[0]
    for r0 in range(0, rows, mm_rows):
        ys_ref[r0:r0 + mm_rows, :] = jnp.dot(st_ref[r0:r0 + mm_rows, :].astype(BF16), cm,
                                             preferred_element_type=F32)
    dk = dk_ref[...]
    for b in range(bsz):
        o_ref[b] = ys_ref[pl.ds(b, tc, stride=bsz), :] + dk * u_ref[b]


def _s5_params(lam_re, lam_im, log_step, b_re, b_im, c_re, c_im, bsz):
    groups = lam_re.shape[0]
    gpc = LANES // SSM_CH
    n_chunks = groups // gpc
    lam = lax.complex(lam_re.astype(F32), lam_im.astype(F32))
    step = jnp.exp(log_step.astype(F32))[:, None]
    lam_bar = jnp.exp(lam * step)
    bmat = lax.complex(b_re.astype(F32), b_im.astype(F32))
    b_bar = ((lam_bar - 1.0) / lam)[..., None] * bmat
    eye = jnp.eye(gpc, dtype=F32)

    def block_diag_in(t):
        t = t.reshape(n_chunks, gpc, SSM_STATE, SSM_CH)
        return jnp.einsum('ngpc,gh->ngchp', t, eye).reshape(n_chunks, gpc * SSM_CH, gpc * SSM_STATE)

    def block_diag_out(t):
        t = t.reshape(n_chunks, gpc, SSM_CH, SSM_STATE)
        return jnp.einsum('ngcp,gh->ngphc', t, eye).reshape(n_chunks, gpc * SSM_STATE, gpc * SSM_CH)

    bm = jnp.concatenate([block_diag_in(b_bar.real), block_diag_in(b_bar.imag)], axis=2).astype(BF16)
    cm = jnp.concatenate([block_diag_out(c_re.astype(F32)), block_diag_out(-c_im.astype(F32))],
                         axis=1).astype(BF16)
    lam_row = jnp.concatenate([lam_bar.real.reshape(n_chunks, gpc * SSM_STATE),
                               lam_bar.imag.reshape(n_chunks, gpc * SSM_STATE)], axis=1)
    lam_t = jnp.broadcast_to(lam_row[:, None, :], (n_chunks, bsz, 2 * gpc * SSM_STATE))
    return bm, lam_t, cm, n_chunks


def _s5(proj3, u_col0, lam_re, lam_im, log_step, b_re, b_im, c_re, c_im, d_skip):
    bsz, seq, _ = proj3.shape
    assert bsz == SUBLANES
    bm, lam_t, cm, n_chunks = _s5_params(lam_re, lam_im, log_step, b_re, b_im, c_re, c_im, bsz)
    width = n_chunks * LANES
    tc = 256
    st_cols = bm.shape[2]
    ublk0 = u_col0 // LANES
    return pl.pallas_call(
        functools.partial(_s5_kernel, tc=tc),
        out_shape=jax.ShapeDtypeStruct((bsz, seq, width), F32),
        grid=(n_chunks, seq // tc),
        in_specs=[pl.BlockSpec((bsz, tc, LANES), lambda c, t: (0, t, ublk0 + c)),
                  pl.BlockSpec((1, LANES, st_cols), lambda c, t: (c, 0, 0)),
                  pl.BlockSpec((1, bsz, st_cols), lambda c, t: (c, 0, 0)),
                  pl.BlockSpec((1, st_cols, LANES), lambda c, t: (c, 0, 0)),
                  pl.BlockSpec((1, LANES), lambda c, t: (0, c))],
        out_specs=pl.BlockSpec((bsz, tc, LANES), lambda c, t: (0, t, c)),
        scratch_shapes=[pltpu.VMEM((tc * bsz, LANES), F32),
                        pltpu.VMEM((tc * bsz, st_cols), F32),
                        pltpu.VMEM((tc * bsz, LANES), F32),
                        pltpu.VMEM((bsz, st_cols), F32)],
        compiler_params=_cparams(("arbitrary", "arbitrary"), 40),
        name="s5",
    )(proj3, bm, lam_t, cm, d_skip.reshape(1, width).astype(F32))


def _layer_norm(v, g, b):
    mu = jnp.mean(v, axis=-1, keepdims=True)
    var = jnp.mean(jnp.square(v - mu), axis=-1, keepdims=True)
    return (v - mu) * lax.rsqrt(var + LN_EPS) * g + b


def _rms_norm(v, g):
    return v * lax.rsqrt(jnp.mean(jnp.square(v), axis=-1, keepdims=True) + RMS_EPS) * g


def _mixout_kernel(att_ref, ssm_ref, x_ref, wglu_ref, bglu_ref, ag_ref, sg_ref, wout_ref, g_ref, b_ref,
                   o_ref, *, alpha):
    y = jax.nn.gelu(ssm_ref[...])
    z = jnp.dot(y.astype(BF16), wglu_ref[...], preferred_element_type=F32) + bglu_ref[...]
    o_ssm = y * jax.nn.sigmoid(z)
    a = _rms_norm(att_ref[...], ag_ref[...]).astype(BF16)
    s = _rms_norm(o_ssm, sg_ref[...]).astype(BF16)
    wa = att_ref.shape[1]
    mix = (jnp.dot(a, wout_ref[:wa, :], preferred_element_type=F32)
           + jnp.dot(s, wout_ref[wa:, :], preferred_element_type=F32))
    o_ref[...] = _layer_norm(alpha * x_ref[...] + mix, g_ref[...], b_ref[...])


def _mixout(o_att, y_ssm, x2d, w_glu, b_glu, att_g, ssm_g, w_out, ln_g, ln_b, alpha):
    n_tok, d = x2d.shape
    wa = o_att.shape[1]
    ws = y_ssm.shape[1]
    tm = 256
    row = lambda w: pl.BlockSpec((tm, w), lambda i: (i, 0))
    full = lambda a: pl.BlockSpec(a.shape, lambda i: (0,) * a.ndim)
    args = (o_att, y_ssm, x2d, w_glu.astype(BF16), b_glu.reshape(1, ws), att_g.reshape(1, wa),
            ssm_g.reshape(1, ws), w_out.astype(BF16), ln_g.reshape(1, d), ln_b.reshape(1, d))
    return pl.pallas_call(
        functools.partial(_mixout_kernel, alpha=alpha),
        out_shape=jax.ShapeDtypeStruct((n_tok, d), F32),
        grid=(n_tok // tm,),
        in_specs=[row(wa), row(ws), row(d)] + [full(a) for a in args[3:]],
        out_specs=row(d),
        compiler_params=_cparams(("parallel",), 32),
        name="mixout",
    )(*args)


def _split_bf16(v):
    hi = v.astype(BF16)
    lo = (v - hi.astype(F32)).astype(BF16)
    return hi, lo


def _router_kernel(h_ref, wt_ref, bias_ref, e_ref, g_ref, r_ref, cnt_ref, run_ref):
    tm = h_ref.shape[0]
    n_exp = wt_ref.shape[0]
    gsz = n_exp // N_EXPERT_GROUPS

    @pl.when(pl.program_id(0) == 0)
    def _():
        run_ref[...] = jnp.zeros_like(run_ref)

    w_hi, w_lo = _split_bf16(wt_ref[...])
    h_hi, h_lo = _split_bf16(h_ref[...])
    nt = (((1,), (1,)), ((), ()))
    logits = (lax.dot_general(w_hi, h_hi, nt, preferred_element_type=F32)
              + lax.dot_general(w_hi, h_lo, nt, preferred_element_type=F32)
              + lax.dot_general(w_lo, h_hi, nt, preferred_element_type=F32))
    scores = jax.nn.sigmoid(logits)
    choice = scores + bias_ref[:, 0:1]

    gio = lax.broadcasted_iota(jnp.int32, (gsz, tm), 0).astype(F32)
    gscore = []
    for g in range(N_EXPERT_GROUPS):
        cg = choice[g * gsz:(g + 1) * gsz, :]
        m1 = jnp.max(cg, axis=0, keepdims=True)
        i1 = jnp.min(jnp.where(cg == m1, gio, float(gsz)), axis=0, keepdims=True)
        m2 = jnp.max(jnp.where(gio == i1, NEG_INF, cg), axis=0, keepdims=True)
        gscore.append(m1 + m2)
    masked = []
    for g in range(N_EXPERT_GROUPS):
        beat = jnp.zeros((1, tm), F32)
        for o in range(N_EXPERT_GROUPS):
            if o == g:
                continue
            wins = (gscore[o] >= gscore[g]) if o < g else (gscore[o] > gscore[g])
            beat = beat + jnp.where(wins, 1.0, 0.0)
        keep = beat < float(TOPK_GROUPS)
        masked.append(jnp.where(keep, choice[g * gsz:(g + 1) * gsz, :], NEG_INF))
    cur = jnp.concatenate(masked, axis=0)

    eio = lax.broadcasted_iota(jnp.int32, (n_exp, tm), 0).astype(F32)
    idxs = []
    gates = []
    onehot = jnp.zeros((n_exp, tm), F32)
    for _ in range(TOP_K):
        m = jnp.max(cur, axis=0, keepdims=True)
        idx = jnp.min(jnp.where(cur == m, eio, float(n_exp)), axis=0, keepdims=True)
        hit = eio == idx
        idxs.append(idx)
        gates.append(jnp.sum(jnp.where(hit, scores, 0.0), axis=0, keepdims=True))
        cur = jnp.where(hit, NEG_INF, cur)
        onehot = onehot + jnp.where(hit, 1.0, 0.0)
    gate = jnp.concatenate(gates, axis=0)
    gate = ROUTED_SCALE * gate / (jnp.sum(gate, axis=0, keepdims=True) + 1e-20)

    si = lax.broadcasted_iota(jnp.int32, (tm, tm), 0)
    ti = lax.broadcasted_iota(jnp.int32, (tm, tm), 1)
    upper = jnp.where(si < ti, 1.0, 0.0).astype(BF16)
    before = jnp.dot(onehot.astype(BF16), upper, preferred_element_type=F32) + run_ref[:, 0:1]
    ranks = [jnp.sum(jnp.where(eio == idx, before, 0.0), axis=0, keepdims=True) for idx in idxs]

    e_ref[...] = jnp.concatenate(idxs, axis=0).astype(jnp.int32)
    g_ref[...] = gate
    r_ref[...] = jnp.concatenate(ranks, axis=0).astype(jnp.int32)
    run_ref[...] = run_ref[...] + jnp.sum(onehot, axis=1, keepdims=True)
    cnt_ref[...] = run_ref[...]


def _router(h, router_w, router_bias):
    n_tok, d = h.shape
    n_exp = router_w.shape[1]
    tm = 256
    wt = router_w.astype(F32).T
    bias = jnp.broadcast_to(router_bias.astype(F32)[:, None], (n_exp, LANES))
    tok = pl.BlockSpec((TOP_K, tm), lambda i: (0, i))
    return pl.pallas_call(
        _router_kernel,
        out_shape=(jax.ShapeDtypeStruct((TOP_K, n_tok), jnp.int32),
                   jax.ShapeDtypeStruct((TOP_K, n_tok), F32),
                   jax.ShapeDtypeStruct((TOP_K, n_tok), jnp.int32),
                   jax.ShapeDtypeStruct((n_exp, LANES), F32)),
        grid=(n_tok // tm,),
        in_specs=[pl.BlockSpec((tm, d), lambda i: (i, 0)),
                  pl.BlockSpec((n_exp, d), lambda i: (0, 0)),
                  pl.BlockSpec((n_exp, LANES), lambda i: (0, 0))],
        out_specs=(tok, tok, tok, pl.BlockSpec((n_exp, LANES), lambda i: (0, 0))),
        scratch_shapes=[pltpu.VMEM((n_exp, LANES), F32)],
        compiler_params=_cparams(("arbitrary",), 32),
        name="router",
    )(h, wt, bias)


def _dest_kernel(e_ref, r_ref, st_ref, d_ref):
    n_exp = st_ref.shape[0]
    tm = e_ref.shape[1]
    eio = lax.broadcasted_iota(jnp.int32, (n_exp, tm), 0)
    start = st_ref[:, 0:1]
    rows = [jnp.sum(jnp.where(eio == e_ref[k:k + 1, :], start, 0.0), axis=0, keepdims=True)
            for k in range(TOP_K)]
    d_ref[...] = jnp.concatenate(rows, axis=0).astype(jnp.int32) + r_ref[...]


def _dest(top_e, rank, starts):
    n_tok = top_e.shape[1]
    n_exp = starts.shape[0]
    tm = 512
    st = jnp.broadcast_to(starts.astype(F32)[:, None], (n_exp, LANES))
    tok = pl.BlockSpec((TOP_K, tm), lambda i: (0, i))
    return pl.pallas_call(
        _dest_kernel,
        out_shape=jax.ShapeDtypeStruct((TOP_K, n_tok), jnp.int32),
        grid=(n_tok // tm,),
        in_specs=[tok, tok, pl.BlockSpec((n_exp, LANES), lambda i: (0, 0))],
        out_specs=tok,
        compiler_params=_cparams(("parallel",), 32),
        name="dest",
    )(top_e, rank, st)


def _pack_bf16_pairs(val):
    half = val.shape[1] // 2
    lo = pltpu.bitcast(val[:, :half].astype(BF16).astype(F32), U32)
    hi = pltpu.bitcast(val[:, half:].astype(BF16).astype(F32), U32)
    return (lo >> 16) | (hi & jnp.uint32(0xFFFF0000))


def _unpack_bf16_pairs(words):
    lo = pltpu.bitcast(words << 16, F32)
    hi = pltpu.bitcast(words & jnp.uint32(0xFFFF0000), F32)
    return jnp.concatenate([lo, hi], axis=1)


def _to_row_tiles(dst_ref, slot, val):
    rows = val.shape[0]
    words = _pack_bf16_pairs(val)
    for j in range(ROW_TILE):
        dst_ref[slot, pl.ds(j, rows, stride=ROW_TILE), :] = words[:, j * LANES:(j + 1) * LANES]


def _row_tile_words(src_ref, idx, rows):
    return jnp.concatenate([src_ref[(*idx, pl.ds(j, rows, stride=ROW_TILE), slice(None))]
                            for j in range(ROW_TILE)], axis=1)


def _row_tile(r):
    return pl.ds(pl.multiple_of(r * ROW_TILE, ROW_TILE), ROW_TILE)


def _dispatch_kernel(dest_ref, h_ref, xs_ref, ht_ref, sem):
    tm = h_ref.shape[0]
    i = pl.program_id(0)
    cur = i % 2
    _to_row_tiles(ht_ref, cur, h_ref[...])

    def issue(t, carry):
        for k in range(TOP_K):
            pltpu.make_async_copy(ht_ref.at[cur, _row_tile(t)], xs_ref.at[_row_tile(dest_ref[t * TOP_K + k])],
                                  sem.at[cur]).start(priority=k % 2)
        return carry
    lax.fori_loop(0, tm, issue, 0, unroll=2)

    def drain(slot):
        for k in range(TOP_K):
            pltpu.make_async_copy(ht_ref.at[slot], xs_ref.at[pl.ds(0, tm * ROW_TILE)], sem.at[slot]).wait()

    @pl.when(i > 0)
    def _():
        drain(1 - cur)

    @pl.when(i == pl.num_programs(0) - 1)
    def _():
        drain(cur)


def _dispatch(h, dest, n_exp):
    n_tok, d = h.shape
    assert d == 2 * ROW_TILE * LANES
    tm = MOE_TOKEN_TILE
    n_rows = n_tok * TOP_K + n_exp * EXPERT_ROWS
    return pl.pallas_call(
        _dispatch_kernel,
        out_shape=jax.ShapeDtypeStruct((n_rows * ROW_TILE, LANES), U32),
        grid=(n_tok // tm,),
        in_specs=[pl.BlockSpec((TOP_K * tm,), lambda i: (i,), memory_space=pltpu.SMEM),
                  pl.BlockSpec((tm, d), lambda i: (i, 0))],
        out_specs=pl.BlockSpec(memory_space=pl.ANY),
        scratch_shapes=[pltpu.VMEM((2, tm * ROW_TILE, LANES), U32), pltpu.SemaphoreType.DMA((2,))],
        compiler_params=_cparams(("arbitrary",), 32),
        name="dispatch",
    )(dest, h)


def _experts_kernel(bstart_ref, bend_ref, cnt_ref, nblk_ref, xs_ref, wgu_ref, wdn_ref, ys_ref,
                    xbuf, ybuf, act_ref, wgu_bf, wdn_bf, xsem, ysem):
    e = pl.program_id(0)
    n_blk = nblk_ref[0]
    trows = xbuf.shape[1]
    rows = trows // ROW_TILE
    ff = wdn_bf.shape[0]
    b0 = bstart_ref[e]
    b1 = bend_ref[e]

    def block_rows(b):
        return pl.ds(pl.multiple_of(b * trows, trows), trows)

    def x_copy(b):
        slot = b % EXPERT_RING
        return pltpu.make_async_copy(xs_ref.at[block_rows(b)], xbuf.at[slot], xsem.at[slot])

    def y_copy(b):
        slot = b % EXPERT_OUT_RING
        return pltpu.make_async_copy(ybuf.at[slot], ys_ref.at[block_rows(b)], ysem.at[slot])

    @pl.when(e == 0)
    def _():
        for i in range(EXPERT_RING):
            @pl.when(i < n_blk)
            def _():
                x_copy(i).start(priority=1)

    wgu_bf[...] = wgu_ref[0].astype(BF16)
    wdn_bf[...] = wdn_ref[0].astype(BF16)

    def up(blocks):
        for b in blocks:
            x_copy(b).wait()
        for b in blocks:
            words = _row_tile_words(xbuf, (b % EXPERT_RING,), rows)
            row = lax.broadcasted_iota(jnp.int32, (rows, 1), 0)
            words = jnp.where(row < cnt_ref[e] - (b - b0) * rows, words, jnp.uint32(0))
            xb = _unpack_bf16_pairs(words).astype(BF16)
            gu = jnp.dot(xb, wgu_bf[...], preferred_element_type=F32)
            act_ref[b - b0] = (jax.nn.silu(gu[:, :ff]) * gu[:, ff:]).astype(BF16)
        for b in blocks:
            @pl.when(b + EXPERT_RING < n_blk)
            def _():
                x_copy(b + EXPERT_RING).start(priority=1)

    def up_pair(p, carry):
        up((b0 + 2 * p, b0 + 2 * p + 1))
        return carry
    n_mine = b1 - b0
    lax.fori_loop(0, n_mine // 2, up_pair, 0)

    @pl.when(n_mine % 2 == 1)
    def _():
        up((b1 - 1,))

    def down(blocks):
        for b in blocks:
            @pl.when(b >= EXPERT_OUT_RING)
            def _():
                y_copy(b - EXPERT_OUT_RING).wait()
        for b in blocks:
            _to_row_tiles(ybuf, b % EXPERT_OUT_RING,
                          jnp.dot(act_ref[b - b0], wdn_bf[...], preferred_element_type=F32))
        for b in blocks:
            y_copy(b).start(priority=1)
        last = blocks[-1]

        @pl.when(last == n_blk - 1)
        def _():
            for i in range(EXPERT_OUT_RING):
                @pl.when(last >= i)
                def _():
                    y_copy(last - i).wait()

    def down_pair(p, carry):
        down((b0 + 2 * p, b0 + 2 * p + 1))
        return carry
    lax.fori_loop(0, n_mine // 2, down_pair, 0)

    @pl.when(n_mine % 2 == 1)
    def _():
        down((b1 - 1,))


def _expert_blocks(counts):
    blocks = (counts + EXPERT_ROWS - 1) // EXPERT_ROWS
    bend = jnp.cumsum(blocks)
    bstart = bend - blocks
    i32 = lambda a: a.astype(jnp.int32)
    return i32(bstart), i32(bend), i32(bend[-1]).reshape(1), i32(bstart * EXPERT_ROWS)


def _experts(xs, bstart, bend, counts, n_blk, w_gu, w_down):
    n_exp, d, ff2 = w_gu.shape
    ff = w_down.shape[1]
    n_rows = xs.shape[0] // ROW_TILE
    assert n_rows % EXPERT_ROWS == 0 and d == 2 * ROW_TILE * LANES
    max_blocks = (n_rows - n_exp * EXPERT_ROWS) // TOP_K // EXPERT_ROWS + 1
    grid_spec = pltpu.PrefetchScalarGridSpec(
        num_scalar_prefetch=4,
        grid=(n_exp,),
        in_specs=[pl.BlockSpec(memory_space=pl.ANY),
                  pl.BlockSpec((1, d, ff2), lambda e, *_: (e, 0, 0)),
                  pl.BlockSpec((1, ff, d), lambda e, *_: (e, 0, 0))],
        out_specs=pl.BlockSpec(memory_space=pl.ANY),
        scratch_shapes=[pltpu.VMEM((EXPERT_RING, EXPERT_ROWS * ROW_TILE, LANES), U32),
                        pltpu.VMEM((EXPERT_OUT_RING, EXPERT_ROWS * ROW_TILE, LANES), U32),
                        pltpu.VMEM((max_blocks, EXPERT_ROWS, ff), BF16),
                        pltpu.VMEM((d, ff2), BF16), pltpu.VMEM((ff, d), BF16),
                        pltpu.SemaphoreType.DMA((EXPERT_RING,)),
                        pltpu.SemaphoreType.DMA((EXPERT_OUT_RING,))],
    )
    return pl.pallas_call(
        _experts_kernel,
        out_shape=jax.ShapeDtypeStruct(xs.shape, U32),
        grid_spec=grid_spec,
        compiler_params=_cparams(("arbitrary",), 40),
        name="experts",
    )(bstart, bend, counts, n_blk, xs, w_gu, w_down)


def _combine_kernel(dest_ref, dnext_ref, gate_ref, h_ref, ys_ref, wgu_ref, wdn_ref, g_ref, b_ref, o_ref,
                    buf_ref, routed_ref, sem, *, alpha):
    tm = h_ref.shape[0]
    i = pl.program_id(0)
    cur = i % 2
    chunk = SUBLANES

    def issue(d_ref, slot, t):
        for k in range(TOP_K):
            pltpu.make_async_copy(ys_ref.at[_row_tile(d_ref[t * TOP_K + k])], buf_ref.at[slot, k, _row_tile(t)],
                                  sem.at[slot]).start(priority=k % 2)

    @pl.when(i == 0)
    def _():
        def first(t, carry):
            issue(dest_ref, 0, t)
            return carry
        lax.fori_loop(0, tm, first, 0, unroll=2)

    for k in range(TOP_K):
        pltpu.make_async_copy(ys_ref.at[pl.ds(0, tm * ROW_TILE)], buf_ref.at[cur, k], sem.at[cur]).wait()

    def weighted_sum(c):
        tok = pl.ds(pl.multiple_of(c * chunk, chunk), chunk)
        gate = gate_ref[tok, :]
        total = None
        for k in range(TOP_K):
            words = jnp.concatenate(
                [buf_ref[cur, k, pl.ds(pl.multiple_of(c * (chunk * ROW_TILE), chunk * ROW_TILE) + j, chunk,
                                       stride=ROW_TILE), :] for j in range(ROW_TILE)], axis=1)
            term = gate[:, k:k + 1] * _unpack_bf16_pairs(words)
            total = term if total is None else total + term
        routed_ref[tok, :] = total

    @pl.when(i + 1 < pl.num_programs(0))
    def _():
        def body(c, carry):
            for t in range(chunk):
                issue(dnext_ref, 1 - cur, c * chunk + t)
            weighted_sum(c)
            return carry
        lax.fori_loop(0, tm // chunk, body, 0)

    @pl.when(i + 1 == pl.num_programs(0))
    def _():
        def body(c, carry):
            weighted_sum(c)
            return carry
        lax.fori_loop(0, tm // chunk, body, 0)

    h = h_ref[...]
    ff = wdn_ref.shape[0]
    gu = jnp.dot(h.astype(BF16), wgu_ref[...], preferred_element_type=F32)
    act = (jax.nn.silu(gu[:, :ff]) * gu[:, ff:]).astype(BF16)
    acc = alpha * h + jnp.dot(act, wdn_ref[...], preferred_element_type=F32) + routed_ref[...]
    o_ref[...] = _layer_norm(acc, g_ref[...], b_ref[...])


def _combine(h, ys, dest, gate_t, shared_w_gu, shared_w_down, ln_g, ln_b, alpha):
    n_tok, d = h.shape
    tm = MOE_TOKEN_TILE
    n_tiles = n_tok // tm
    full = lambda a: pl.BlockSpec(a.shape, lambda i: (0,) * a.ndim)
    wgu = shared_w_gu.astype(BF16)
    wdn = shared_w_down.astype(BF16)
    g2 = ln_g.reshape(1, d)
    b2 = ln_b.reshape(1, d)
    return pl.pallas_call(
        functools.partial(_combine_kernel, alpha=alpha),
        out_shape=jax.ShapeDtypeStruct((n_tok, d), F32),
        grid=(n_tiles,),
        in_specs=[pl.BlockSpec((TOP_K * tm,), lambda i: (i,), memory_space=pltpu.SMEM),
                  pl.BlockSpec((TOP_K * tm,), lambda i: (jnp.minimum(i + 1, n_tiles - 1),),
                               memory_space=pltpu.SMEM),
                  pl.BlockSpec((tm, TOP_K), lambda i: (i, 0)),
                  pl.BlockSpec((tm, d), lambda i: (i, 0)),
                  pl.BlockSpec(memory_space=pl.ANY),
                  full(wgu), full(wdn), full(g2), full(b2)],
        out_specs=pl.BlockSpec((tm, d), lambda i: (i, 0)),
        scratch_shapes=[pltpu.VMEM((2, TOP_K, tm * ROW_TILE, LANES), U32), pltpu.VMEM((tm, d), F32),
                        pltpu.SemaphoreType.DMA((2,))],
        compiler_params=_cparams(("arbitrary",), 48),
        name="combine",
    )(dest, dest, gate_t, h, ys, wgu, wdn, g2, b2)


def _moe(h, router_w, router_bias, w_gu, w_down, shared_w_gu, shared_w_down, ln_g, ln_b, alpha):
    top_e, gate, rank, cnt = _router(h, router_w, router_bias)
    counts = cnt[:, 0].astype(jnp.int32)
    bstart, bend, n_blk, pad_start = _expert_blocks(counts)
    dest = _dest(top_e, rank, pad_start)
    dest_tiles = dest.T.reshape(-1)
    xs = _dispatch(h, dest_tiles, counts.shape[0])
    ys = _experts(xs, bstart, bend, counts, n_blk, w_gu, w_down)
    return _combine(h, ys, dest_tiles, gate.T, shared_w_gu, shared_w_down, ln_g, ln_b, alpha)


def kernel(x, w_in, att_norm_g, lam_re, lam_im, log_step, b_re, b_im, c_re, c_im, d_skip, w_glu, b_glu,
           ssm_norm_g, w_out, ln1_g, ln1_b, router_w, router_bias, w_gu, w_down, shared_w_gu,
           shared_w_down, ln2_g, ln2_b):
    bsz, seq, d = x.shape
    depth = w_in.shape[0]
    alpha = (2 * depth) ** 0.25
    h = x.reshape(bsz * seq, d)
    for i in range(depth):
        proj = _inproj(h, w_in[i].astype(BF16), seq)
        o_att = _attention(proj, bsz, seq)
        y_ssm = _s5(proj.reshape(bsz, seq, -1), 3 * ATT_WIDTH, lam_re[i], lam_im[i], log_step[i],
                    b_re[i], b_im[i], c_re[i], c_im[i], d_skip[i])
        h = _mixout(o_att, y_ssm.reshape(bsz * seq, -1), h, w_glu[i], b_glu[i], att_norm_g[i],
                    ssm_norm_g[i], w_out[i], ln1_g[i], ln1_b[i], alpha)
        h = _moe(h, router_w[i], router_bias[i], w_gu[i], w_down[i], shared_w_gu[i], shared_w_down[i],
                 ln2_g[i], ln2_b[i], alpha)
    return h.reshape(bsz, seq, d)
```

```python
import functools
import math

import jax
import jax.numpy as jnp
from jax import lax
from jax.experimental import pallas as pl
from jax.experimental.pallas import tpu as pltpu

F32 = jnp.float32
BF16 = jnp.bfloat16
U32 = jnp.uint32

ATT_HEADS = 8
HEAD_DIM = 64
ATT_WIDTH = ATT_HEADS * HEAD_DIM
SSM_CH = 16
SSM_STATE = 64
ROPE_THETA = 500000.0
ROT_DIM = HEAD_DIM // 4
DILATIONS = (1, 4, 16)
ATT_BLOCK = 128
ATT_GROUP = 16
N_EXPERTS = 256
TOP_K = 8
N_EXPERT_GROUPS = 8
TOPK_GROUPS = 4
ROUTED_SCALE = 2.5
LN_EPS = 1e-5
RMS_EPS = 1e-6

LANES = 128
SUBLANES = 8
EXPERT_ROWS = 128
MOE_TOKEN_TILE = 512
EXPERT_RING = 16
EXPERT_OUT_RING = 16
EXPERT_GROUP = 4
ROW_TILE = 4
NEG_INF = float("-inf")


def _cparams(sem, vmem_mb):
    return pltpu.CompilerParams(dimension_semantics=sem, vmem_limit_bytes=vmem_mb * 1024 * 1024)


def _inproj_kernel(x_ref, w_ref, cos_ref, sa_ref, sb_ref, o_ref, *, n_rot_cols):
    xb = x_ref[...].astype(BF16)
    cosf = cos_ref[...]
    sa = sa_ref[...]
    sb = sb_ref[...]
    width = o_ref.shape[1]
    chunk = 512
    for c in range(width // chunk):
        r = jnp.dot(xb, w_ref[:, c * chunk:(c + 1) * chunk], preferred_element_type=F32)
        if c * chunk < n_rot_cols:
            parts = []
            for s in range(chunk // LANES):
                t = r[:, s * LANES:(s + 1) * LANES]
                parts.append(t * cosf + pltpu.roll(t, LANES - ROT_DIM // 2, 1) * sa
                             + pltpu.roll(t, ROT_DIM // 2, 1) * sb)
            r = jnp.concatenate(parts, axis=1)
        o_ref[:, c * chunk:(c + 1) * chunk] = r


def _rope_lane_tables(seq):
    half = ROT_DIM // 2
    inv_freq = jnp.power(jnp.float32(ROPE_THETA), -jnp.arange(half, dtype=F32) / half)
    ang = jnp.arange(seq, dtype=F32)[:, None] * inv_freq[None, :]
    cos, sin = jnp.cos(ang), jnp.sin(ang)
    rest = HEAD_DIM - ROT_DIM
    cos_h = jnp.concatenate([cos, cos, jnp.ones((seq, rest), F32)], axis=1)
    sa_h = jnp.concatenate([-sin, jnp.zeros((seq, half + rest), F32)], axis=1)
    sb_h = jnp.concatenate([jnp.zeros((seq, half), F32), sin, jnp.zeros((seq, rest), F32)], axis=1)
    rep = LANES // HEAD_DIM
    return tuple(jnp.tile(t, (1, rep)) for t in (cos_h, sa_h, sb_h))


def _inproj(x2d, w_in_bf, seq):
    n_tok, d = x2d.shape
    width = w_in_bf.shape[1]
    tm = 512
    cosf, sa, sb = _rope_lane_tables(seq)
    tab_spec = pl.BlockSpec((tm, LANES), lambda i: (i % (seq // tm), 0))
    return pl.pallas_call(
        functools.partial(_inproj_kernel, n_rot_cols=2 * ATT_WIDTH),
        out_shape=jax.ShapeDtypeStruct((n_tok, width), F32),
        grid=(n_tok // tm,),
        in_specs=[pl.BlockSpec((tm, d), lambda i: (i, 0)),
                  pl.BlockSpec((d, width), lambda i: (0, 0)),
                  tab_spec, tab_spec, tab_spec],
        out_specs=pl.BlockSpec((tm, width), lambda i: (i, 0)),
        compiler_params=_cparams(("parallel",), 48),
        name="inproj",
    )(x2d, w_in_bf, cosf, sa, sb)


def _attn_kernel(q_ref, k_ref, v_ref, o_ref, qs_ref, ks_ref, vs_ref, tmp_ref, ob_ref, lb_ref, band_ref,
                 first_ref, *, seq):
    blk = ATT_BLOCK
    lane = lax.broadcasted_iota(jnp.int32, (1, LANES), 1)
    head0 = lane < HEAD_DIM
    scale = HEAD_DIM ** -0.5
    d1, d2 = DILATIONS[1], DILATIONS[2]
    assert DILATIONS[0] == 1 and d2 == d1 * d1
    seg = seq // d1
    sub = seg // d1

    qi = lax.broadcasted_iota(jnp.int32, (blk, 2 * blk), 0)
    kj = lax.broadcasted_iota(jnp.int32, (blk, 2 * blk), 1)
    dist = qi + blk - kj
    band_ref[...] = jnp.where((dist >= 0) & (dist <= blk), 0.0, NEG_INF)
    first_ref[...] = jnp.where((dist >= 0) & (kj >= blk), 0.0, NEG_INF)

    n_class = (1, d1, d2)
    class_len = (seq, seg, sub)
    base = [0]
    for c in range(len(DILATIONS)):
        base.append(base[c] + n_class[c] * (class_len[c] + blk))

    def kv_row0(c, g):
        return base[c] + g * (class_len[c] + blk)

    qs_ref[0] = (q_ref[...] * scale).astype(BF16)
    for a in range(d1):
        x = q_ref[pl.ds(a, seg, stride=d1), :] * scale
        tmp_ref[a * seg:(a + 1) * seg, :] = x
        qs_ref[1, a * seg:(a + 1) * seg, :] = x.astype(BF16)
    for g in range(d2):
        qs_ref[2, g * sub:(g + 1) * sub, :] = tmp_ref[pl.ds((g // d1) * seg + g % d1, sub, stride=d1),
                                                      :].astype(BF16)
    for src_ref, dst_ref in ((k_ref, ks_ref), (v_ref, vs_ref)):
        for c in range(len(DILATIONS)):
            for g in range(n_class[c]):
                dst_ref[kv_row0(c, g):kv_row0(c, g) + blk, :] = jnp.zeros((blk, LANES), BF16)
        dst_ref[kv_row0(0, 0) + blk:kv_row0(0, 0) + blk + seq, :] = src_ref[...].astype(BF16)
        for a in range(d1):
            x = src_ref[pl.ds(a, seg, stride=d1), :]
            tmp_ref[a * seg:(a + 1) * seg, :] = x
            dst_ref[kv_row0(1, a) + blk:kv_row0(1, a) + blk + seg, :] = x.astype(BF16)
        for g in range(d2):
            dst_ref[kv_row0(2, g) + blk:kv_row0(2, g) + blk + sub, :] = tmp_ref[
                pl.ds((g // d1) * seg + g % d1, sub, stride=d1), :].astype(BF16)

    def one_block(c, g, n, out_rows, bias_ref):
        q = qs_ref[c, pl.ds(aligned(g * class_len[c] + n * blk), blk), :]
        kv_rows = pl.ds(aligned(kv_row0(c, g) + n * blk), 2 * blk)
        kk = ks_ref[kv_rows, :]
        vv = vs_ref[kv_rows, :]
        outs = []
        lses = []
        for h in range(LANES // HEAD_DIM):
            hm = head0 if h == 0 else jnp.logical_not(head0)
            qh = jnp.where(hm, q, jnp.zeros_like(q))
            s = lax.dot_general(qh, kk, (((1,), (1,)), ((), ())), preferred_element_type=F32)
            s = s + bias_ref[...]
            m = jnp.max(s, axis=-1, keepdims=True)
            p = jnp.exp(s - m)
            den = jnp.sum(p, axis=-1, keepdims=True)
            outs.append(jnp.dot(p.astype(BF16), vv, preferred_element_type=F32) / den)
            lses.append(m + jnp.log(den))
        ob_ref[c, out_rows, :] = jnp.where(head0, outs[0], outs[1])
        lb_ref[c, out_rows, :] = jnp.where(head0, lses[0], lses[1])

    def run_blocks(n_blocks, fn):
        group = max(g for g in range(1, ATT_GROUP + 1) if n_blocks % g == 0)
        if n_blocks == group:
            for g in range(group):
                fn(g)
            return
        def body(it, carry):
            for g in range(group):
                fn(it * group + g)
            return carry
        lax.fori_loop(0, n_blocks // group, body, 0)

    def aligned(x):
        return x if isinstance(x, int) else pl.multiple_of(x, blk)

    one_block(0, 0, 0, pl.ds(0, blk), first_ref)
    run_blocks(seq // blk - 1,
               lambda i: one_block(0, 0, i + 1, pl.ds(aligned((i + 1) * blk), blk), band_ref))

    nb1 = seg // blk
    run_blocks(d1, lambda a: one_block(1, a, 0, pl.ds(a, blk, stride=d1), first_ref))
    def later1(i):
        a = i // (nb1 - 1)
        n = i - a * (nb1 - 1) + 1
        one_block(1, a, n, pl.ds(a + n * (d1 * blk), blk, stride=d1), band_ref)
    run_blocks(d1 * (nb1 - 1), later1)

    assert sub == blk
    def only2(g):
        a = g // d1
        one_block(2, g, 0, pl.ds(a + d1 * (g - a * d1), blk, stride=d2), first_ref)
    run_blocks(d2, only2)

    rc = 256
    def merge(i, carry):
        sl = pl.ds(pl.multiple_of(i * rc, rc), rc)
        l0 = lb_ref[0, sl, :]
        l1 = lb_ref[1, sl, :]
        l2 = lb_ref[2, sl, :]
        mx = jnp.maximum(jnp.maximum(l0, l1), l2)
        e0 = jnp.exp(l0 - mx)
        e1 = jnp.exp(l1 - mx)
        e2 = jnp.exp(l2 - mx)
        tot = e0 + e1 + e2
        o_ref[sl, :] = ((e0 / tot) * ob_ref[0, sl, :] + (e1 / tot) * ob_ref[1, sl, :]
                        + (e2 / tot) * ob_ref[2, sl, :])
        return carry
    lax.fori_loop(0, seq // rc, merge, 0)


def _attention(proj, bsz, seq):
    n_tok = proj.shape[0]
    pairs = ATT_WIDTH // LANES
    assert seq % (ATT_BLOCK * max(DILATIONS)) == 0
    kv_rows = sum(seq + d * ATT_BLOCK for d in DILATIONS)
    blk = (seq, LANES)
    return pl.pallas_call(
        functools.partial(_attn_kernel, seq=seq),
        out_shape=jax.ShapeDtypeStruct((n_tok, ATT_WIDTH), F32),
        grid=(bsz, pairs),
        in_specs=[pl.BlockSpec(blk, lambda b, h: (b, h)),
                  pl.BlockSpec(blk, lambda b, h: (b, pairs + h)),
                  pl.BlockSpec(blk, lambda b, h: (b, 2 * pairs + h))],
        out_specs=pl.BlockSpec(blk, lambda b, h: (b, h)),
        scratch_shapes=[pltpu.VMEM((len(DILATIONS), seq, LANES), BF16),
                        pltpu.VMEM((kv_rows, LANES), BF16),
                        pltpu.VMEM((kv_rows, LANES), BF16),
                        pltpu.VMEM((seq, LANES), F32),
                        pltpu.VMEM((len(DILATIONS), seq, LANES), F32),
                        pltpu.VMEM((len(DILATIONS), seq, LANES), F32),
                        pltpu.VMEM((ATT_BLOCK, 2 * ATT_BLOCK), F32),
                        pltpu.VMEM((ATT_BLOCK, 2 * ATT_BLOCK), F32)],
        compiler_params=_cparams(("parallel", "parallel"), 40),
        name="attn",
    )(proj, proj, proj)


def _s5_kernel(u_ref, bm_ref, lam_ref, cm_ref, dk_ref, o_ref, us_ref, st_ref, ys_ref, carry_ref, *, tc):
    bsz = u_ref.shape[0]
    half = st_ref.shape[1] // 2
    rows = tc * bsz
    mm_rows = 512

    @pl.when(pl.program_id(1) == 0)
    def _():
        carry_ref[...] = jnp.zeros_like(carry_ref)

    for b in range(bsz):
        us_ref[pl.ds(b, tc, stride=bsz), :] = u_ref[b]

    bm = bm_ref[0]
    for r0 in range(0, rows, mm_rows):
        st_ref[r0:r0 + mm_rows, :] = jnp.dot(us_ref[r0:r0 + mm_rows, :].astype(BF16), bm,
                                             preferred_element_type=F32)

    lam = lam_ref[0]
    lam_re = lam[:, :half]
    lam_im = lam[:, half:]

    def step(t, carry):
        xr, xi = carry
        sl = pl.ds(pl.multiple_of(t * bsz, bsz), bsz)
        nr = lam_re * xr - lam_im * xi + st_ref[sl, :half]
        ni = lam_re * xi + lam_im * xr + st_ref[sl, half:]
        st_ref[sl, :half] = nr
        st_ref[sl, half:] = ni
        return nr, ni

    xr, xi = lax.fori_loop(0, tc, step, (carry_ref[:, :half], carry_ref[:, half:]), unroll=4)
    carry_ref[:, :half] = xr
    carry_ref[:, half:] = xi

    cm = cm_ref[0]
    for r0 in range(0, rows, mm_rows):
        ys_ref[r0:r0 + mm_rows, :] = jnp.dot(st_ref[r0:r0 + mm_rows, :].astype(BF16), cm,
                                             preferred_element_type=F32)
    dk = dk_ref[...]
    for b in range(bsz):
        o_ref[b] = ys_ref[pl.ds(b, tc, stride=bsz), :] + dk * u_ref[b]


def _s5_params(lam_re, lam_im, log_step, b_re, b_im, c_re, c_im, bsz):
    groups = lam_re.shape[0]
    gpc = LANES // SSM_CH
    n_chunks = groups // gpc
    lam = lax.complex(lam_re.astype(F32), lam_im.astype(F32))
    step = jnp.exp(log_step.astype(F32))[:, None]
    lam_bar = jnp.exp(lam * step)
    bmat = lax.complex(b_re.astype(F32), b_im.astype(F32))
    b_bar = ((lam_bar - 1.0) / lam)[..., None] * bmat
    eye = jnp.eye(gpc, dtype=F32)

    def block_diag_in(t):
        t = t.reshape(n_chunks, gpc, SSM_STATE, SSM_CH)
        return jnp.einsum('ngpc,gh->ngchp', t, eye).reshape(n_chunks, gpc * SSM_CH, gpc * SSM_STATE)

    def block_diag_out(t):
        t = t.reshape(n_chunks, gpc, SSM_CH, SSM_STATE)
        return jnp.einsum('ngcp,gh->ngphc', t, eye).reshape(n_chunks, gpc * SSM_STATE, gpc * SSM_CH)

    bm = jnp.concatenate([block_diag_in(b_bar.real), block_diag_in(b_bar.imag)], axis=2).astype(BF16)
    cm = jnp.concatenate([block_diag_out(c_re.astype(F32)), block_diag_out(-c_im.astype(F32))],
                         axis=1).astype(BF16)
    lam_row = jnp.concatenate([lam_bar.real.reshape(n_chunks, gpc * SSM_STATE),
                               lam_bar.imag.reshape(n_chunks, gpc * SSM_STATE)], axis=1)
    lam_t = jnp.broadcast_to(lam_row[:, None, :], (n_chunks, bsz, 2 * gpc * SSM_STATE))
    return bm, lam_t, cm, n_chunks


def _s5(proj3, u_col0, lam_re, lam_im, log_step, b_re, b_im, c_re, c_im, d_skip):
    bsz, seq, _ = proj3.shape
    assert bsz == SUBLANES
    bm, lam_t, cm, n_chunks = _s5_params(lam_re, lam_im, log_step, b_re, b_im, c_re, c_im, bsz)
    width = n_chunks * LANES
    tc = 256
    st_cols = bm.shape[2]
    ublk0 = u_col0 // LANES
    return pl.pallas_call(
        functools.partial(_s5_kernel, tc=tc),
        out_shape=jax.ShapeDtypeStruct((bsz, seq, width), F32),
        grid=(n_chunks, seq // tc),
        in_specs=[pl.BlockSpec((bsz, tc, LANES), lambda c, t: (0, t, ublk0 + c)),
                  pl.BlockSpec((1, LANES, st_cols), lambda c, t: (c, 0, 0)),
                  pl.BlockSpec((1, bsz, st_cols), lambda c, t: (c, 0, 0)),
                  pl.BlockSpec((1, st_cols, LANES), lambda c, t: (c, 0, 0)),
                  pl.BlockSpec((1, LANES), lambda c, t: (0, c))],
        out_specs=pl.BlockSpec((bsz, tc, LANES), lambda c, t: (0, t, c)),
        scratch_shapes=[pltpu.VMEM((tc * bsz, LANES), F32),
                        pltpu.VMEM((tc * bsz, st_cols), F32),
                        pltpu.VMEM((tc * bsz, LANES), F32),
                        pltpu.VMEM((bsz, st_cols), F32)],
        compiler_params=_cparams(("arbitrary", "arbitrary"), 40),
        name="s5",
    )(proj3, bm, lam_t, cm, d_skip.reshape(1, width).astype(F32))


def _layer_norm(v, g, b):
    mu = jnp.mean(v, axis=-1, keepdims=True)
    var = jnp.mean(jnp.square(v - mu), axis=-1, keepdims=True)
    return (v - mu) * lax.rsqrt(var + LN_EPS) * g + b


def _rms_norm(v, g):
    return v * lax.rsqrt(jnp.mean(jnp.square(v), axis=-1, keepdims=True) + RMS_EPS) * g


def _mixout_kernel(att_ref, ssm_ref, x_ref, wglu_ref, bglu_ref, ag_ref, sg_ref, wout_ref, g_ref, b_ref,
                   o_ref, *, alpha):
    y = jax.nn.gelu(ssm_ref[...])
    z = jnp.dot(y.astype(BF16), wglu_ref[...], preferred_element_type=F32) + bglu_ref[...]
    o_ssm = y * jax.nn.sigmoid(z)
    a = _rms_norm(att_ref[...], ag_ref[...]).astype(BF16)
    s = _rms_norm(o_ssm, sg_ref[...]).astype(BF16)
    wa = att_ref.shape[1]
    mix = (jnp.dot(a, wout_ref[:wa, :], preferred_element_type=F32)
           + jnp.dot(s, wout_ref[wa:, :], preferred_element_type=F32))
    o_ref[...] = _layer_norm(alpha * x_ref[...] + mix, g_ref[...], b_ref[...])


def _mixout(o_att, y_ssm, x2d, w_glu, b_glu, att_g, ssm_g, w_out, ln_g, ln_b, alpha):
    n_tok, d = x2d.shape
    wa = o_att.shape[1]
    ws = y_ssm.shape[1]
    tm = 256
    row = lambda w: pl.BlockSpec((tm, w), lambda i: (i, 0))
    full = lambda a: pl.BlockSpec(a.shape, lambda i: (0,) * a.ndim)
    args = (o_att, y_ssm, x2d, w_glu.astype(BF16), b_glu.reshape(1, ws), att_g.reshape(1, wa),
            ssm_g.reshape(1, ws), w_out.astype(BF16), ln_g.reshape(1, d), ln_b.reshape(1, d))
    return pl.pallas_call(
        functools.partial(_mixout_kernel, alpha=alpha),
        out_shape=jax.ShapeDtypeStruct((n_tok, d), F32),
        grid=(n_tok // tm,),
        in_specs=[row(wa), row(ws), row(d)] + [full(a) for a in args[3:]],
        out_specs=row(d),
        compiler_params=_cparams(("parallel",), 32),
        name="mixout",
    )(*args)


def _split_bf16(v):
    hi = v.astype(BF16)
    lo = (v - hi.astype(F32)).astype(BF16)
    return hi, lo


def _router_kernel(h_ref, wt_ref, bias_ref, e_ref, g_ref, r_ref, cnt_ref, run_ref):
    tm = h_ref.shape[0]
    n_exp = wt_ref.shape[0]
    gsz = n_exp // N_EXPERT_GROUPS

    @pl.when(pl.program_id(0) == 0)
    def _():
        run_ref[...] = jnp.zeros_like(run_ref)

    w_hi, w_lo = _split_bf16(wt_ref[...])
    h_hi, h_lo = _split_bf16(h_ref[...])
    nt = (((1,), (1,)), ((), ()))
    logits = (lax.dot_general(w_hi, h_hi, nt, preferred_element_type=F32)
              + lax.dot_general(w_hi, h_lo, nt, preferred_element_type=F32)
              + lax.dot_general(w_lo, h_hi, nt, preferred_element_type=F32))
    scores = jax.nn.sigmoid(logits)
    choice = scores + bias_ref[:, 0:1]

    gio = lax.broadcasted_iota(jnp.int32, (gsz, tm), 0).astype(F32)
    gscore = []
    for g in range(N_EXPERT_GROUPS):
        cg = choice[g * gsz:(g + 1) * gsz, :]
        m1 = jnp.max(cg, axis=0, keepdims=True)
        i1 = jnp.min(jnp.where(cg == m1, gio, float(gsz)), axis=0, keepdims=True)
        m2 = jnp.max(jnp.where(gio == i1, NEG_INF, cg), axis=0, keepdims=True)
        gscore.append(m1 + m2)
    masked = []
    for g in range(N_EXPERT_GROUPS):
        beat = jnp.zeros((1, tm), F32)
        for o in range(N_EXPERT_GROUPS):
            if o == g:
                continue
            wins = (gscore[o] >= gscore[g]) if o < g else (gscore[o] > gscore[g])
            beat = beat + jnp.where(wins, 1.0, 0.0)
        keep = beat < float(TOPK_GROUPS)
        masked.append(jnp.where(keep, choice[g * gsz:(g + 1) * gsz, :], NEG_INF))
    cur = jnp.concatenate(masked, axis=0)

    eio = lax.broadcasted_iota(jnp.int32, (n_exp, tm), 0).astype(F32)
    idxs = []
    gates = []
    onehot = jnp.zeros((n_exp, tm), F32)
    for _ in range(TOP_K):
        m = jnp.max(cur, axis=0, keepdims=True)
        idx = jnp.min(jnp.where(cur == m, eio, float(n_exp)), axis=0, keepdims=True)
        hit = eio == idx
        idxs.append(idx)
        gates.append(jnp.sum(jnp.where(hit, scores, 0.0), axis=0, keepdims=True))
        cur = jnp.where(hit, NEG_INF, cur)
        onehot = onehot + jnp.where(hit, 1.0, 0.0)
    gate = jnp.concatenate(gates, axis=0)
    gate = ROUTED_SCALE * gate / (jnp.sum(gate, axis=0, keepdims=True) + 1e-20)

    si = lax.broadcasted_iota(jnp.int32, (tm, tm), 0)
    ti = lax.broadcasted_iota(jnp.int32, (tm, tm), 1)
    upper = jnp.where(si < ti, 1.0, 0.0).astype(BF16)
    before = jnp.dot(onehot.astype(BF16), upper, preferred_element_type=F32) + run_ref[:, 0:1]
    ranks = [jnp.sum(jnp.where(eio == idx, before, 0.0), axis=0, keepdims=True) for idx in idxs]

    e_ref[...] = jnp.concatenate(idxs, axis=0).astype(jnp.int32)
    g_ref[...] = gate
    r_ref[...] = jnp.concatenate(ranks, axis=0).astype(jnp.int32)
    run_ref[...] = run_ref[...] + jnp.sum(onehot, axis=1, keepdims=True)
    cnt_ref[...] = run_ref[...]


def _router(h, router_w, router_bias):
    n_tok, d = h.shape
    n_exp = router_w.shape[1]
    tm = 256
    wt = router_w.astype(F32).T
    bias = jnp.broadcast_to(router_bias.astype(F32)[:, None], (n_exp, LANES))
    tok = pl.BlockSpec((TOP_K, tm), lambda i: (0, i))
    return pl.pallas_call(
        _router_kernel,
        out_shape=(jax.ShapeDtypeStruct((TOP_K, n_tok), jnp.int32),
                   jax.ShapeDtypeStruct((TOP_K, n_tok), F32),
                   jax.ShapeDtypeStruct((TOP_K, n_tok), jnp.int32),
                   jax.ShapeDtypeStruct((n_exp, LANES), F32)),
        grid=(n_tok // tm,),
        in_specs=[pl.BlockSpec((tm, d), lambda i: (i, 0)),
                  pl.BlockSpec((n_exp, d), lambda i: (0, 0)),
                  pl.BlockSpec((n_exp, LANES), lambda i: (0, 0))],
        out_specs=(tok, tok, tok, pl.BlockSpec((n_exp, LANES), lambda i: (0, 0))),
        scratch_shapes=[pltpu.VMEM((n_exp, LANES), F32)],
        compiler_params=_cparams(("arbitrary",), 32),
        name="router",
    )(h, wt, bias)


def _dest_kernel(e_ref, r_ref, st_ref, d_ref):
    n_exp = st_ref.shape[0]
    tm = e_ref.shape[1]
    eio = lax.broadcasted_iota(jnp.int32, (n_exp, tm), 0)
    start = st_ref[:, 0:1]
    rows = [jnp.sum(jnp.where(eio == e_ref[k:k + 1, :], start, 0.0), axis=0, keepdims=True)
            for k in range(TOP_K)]
    d_ref[...] = jnp.concatenate(rows, axis=0).astype(jnp.int32) + r_ref[...]


def _dest(top_e, rank, starts):
    n_tok = top_e.shape[1]
    n_exp = starts.shape[0]
    tm = 512
    st = jnp.broadcast_to(starts.astype(F32)[:, None], (n_exp, LANES))
    tok = pl.BlockSpec((TOP_K, tm), lambda i: (0, i))
    return pl.pallas_call(
        _dest_kernel,
        out_shape=jax.ShapeDtypeStruct((TOP_K, n_tok), jnp.int32),
        grid=(n_tok // tm,),
        in_specs=[tok, tok, pl.BlockSpec((n_exp, LANES), lambda i: (0, 0))],
        out_specs=tok,
        compiler_params=_cparams(("parallel",), 32),
        name="dest",
    )(top_e, rank, st)


def _pack_bf16_pairs(val):
    half = val.shape[1] // 2
    lo = pltpu.bitcast(val[:, :half].astype(BF16).astype(F32), U32)
    hi = pltpu.bitcast(val[:, half:].astype(BF16).astype(F32), U32)
    return (lo >> 16) | (hi & jnp.uint32(0xFFFF0000))


def _unpack_bf16_pairs(words):
    lo = pltpu.bitcast(words << 16, F32)
    hi = pltpu.bitcast(words & jnp.uint32(0xFFFF0000), F32)
    return jnp.concatenate([lo, hi], axis=1)


def _to_row_tiles(dst_ref, slot, val):
    rows = val.shape[0]
    words = _pack_bf16_pairs(val)
    for j in range(ROW_TILE):
        dst_ref[slot, pl.ds(j, rows, stride=ROW_TILE), :] = words[:, j * LANES:(j + 1) * LANES]


def _row_tile_words(src_ref, idx, rows):
    return jnp.concatenate([src_ref[(*idx, pl.ds(j, rows, stride=ROW_TILE), slice(None))]
                            for j in range(ROW_TILE)], axis=1)


def _row_tile(r):
    return pl.ds(pl.multiple_of(r * ROW_TILE, ROW_TILE), ROW_TILE)


def _dispatch_kernel(dest_ref, h_ref, xs_ref, ht_ref, sem):
    tm = h_ref.shape[0]
    i = pl.program_id(0)
    cur = i % 2
    _to_row_tiles(ht_ref, cur, h_ref[...])

    def issue(t, carry):
        for k in range(TOP_K):
            pltpu.make_async_copy(ht_ref.at[cur, _row_tile(t)], xs_ref.at[_row_tile(dest_ref[t * TOP_K + k])],
                                  sem.at[cur]).start(priority=k % 2)
        return carry
    lax.fori_loop(0, tm, issue, 0, unroll=2)

    def drain(slot):
        for k in range(TOP_K):
            pltpu.make_async_copy(ht_ref.at[slot], xs_ref.at[pl.ds(0, tm * ROW_TILE)], sem.at[slot]).wait()

    @pl.when(i > 0)
    def _():
        drain(1 - cur)

    @pl.when(i == pl.num_programs(0) - 1)
    def _():
        drain(cur)


def _dispatch(h, dest, n_exp):
    n_tok, d = h.shape
    assert d == 2 * ROW_TILE * LANES
    tm = MOE_TOKEN_TILE
    n_rows = n_tok * TOP_K + n_exp * EXPERT_ROWS
    return pl.pallas_call(
        _dispatch_kernel,
        out_shape=jax.ShapeDtypeStruct((n_rows * ROW_TILE, LANES), U32),
        grid=(n_tok // tm,),
        in_specs=[pl.BlockSpec((TOP_K * tm,), lambda i: (i,), memory_space=pltpu.SMEM),
                  pl.BlockSpec((tm, d), lambda i: (i, 0))],
        out_specs=pl.BlockSpec(memory_space=pl.ANY),
        scratch_shapes=[pltpu.VMEM((2, tm * ROW_TILE, LANES), U32), pltpu.SemaphoreType.DMA((2,))],
        compiler_params=_cparams(("arbitrary",), 32),
        name="dispatch",
    )(dest, h)


def _experts_kernel(bstart_ref, bend_ref, cnt_ref, nblk_ref, xs_ref, wgu_ref, wdn_ref, ys_ref,
                    xbuf, ybuf, act_ref, wgu_bf, wdn_bf, xsem, ysem):
    e = pl.program_id(0)
    n_blk = nblk_ref[0]
    trows = xbuf.shape[1]
    rows = trows // ROW_TILE
    ff = wdn_bf.shape[0]
    b0 = bstart_ref[e]
    b1 = bend_ref[e]

    def block_rows(b):
        return pl.ds(pl.multiple_of(b * trows, trows), trows)

    def x_copy(b):
        slot = b % EXPERT_RING
        return pltpu.make_async_copy(xs_ref.at[block_rows(b)], xbuf.at[slot], xsem.at[slot])

    def y_copy(b):
        slot = b % EXPERT_OUT_RING
        return pltpu.make_async_copy(ybuf.at[slot], ys_ref.at[block_rows(b)], ysem.at[slot])

    @pl.when(e == 0)
    def _():
        for i in range(EXPERT_RING):
            @pl.when(i < n_blk)
            def _():
                x_copy(i).start(priority=1)

    wgu_bf[...] = wgu_ref[0].astype(BF16)
    wdn_bf[...] = wdn_ref[0].astype(BF16)

    def up(blocks):
        for b in blocks:
            x_copy(b).wait()
        for b in blocks:
            words = _row_tile_words(xbuf, (b % EXPERT_RING,), rows)
            row = lax.broadcasted_iota(jnp.int32, (rows, 1), 0)
            words = jnp.where(row < cnt_ref[e] - (b - b0) * rows, words, jnp.uint32(0))
            xb = _unpack_bf16_pairs(words).astype(BF16)
            gu = jnp.dot(xb, wgu_bf[...], preferred_element_type=F32)
            act_ref[b - b0] = (jax.nn.silu(gu[:, :ff]) * gu[:, ff:]).astype(BF16)
        for b in blocks:
            @pl.when(b + EXPERT_RING < n_blk)
            def _():
                x_copy(b + EXPERT_RING).start(priority=1)

    n_mine = b1 - b0

    def run_groups(fn):
        def body(p, carry):
            fn(tuple(b0 + EXPERT_GROUP * p + j for j in range(EXPERT_GROUP)))
            return carry
        lax.fori_loop(0, n_mine // EXPERT_GROUP, body, 0)
        size = EXPERT_GROUP // 2
        while size >= 1:
            @pl.when(n_mine & size != 0)
            def _(size=size):
                start = b0 + (n_mine // (2 * size)) * (2 * size)
                fn(tuple(start + j for j in range(size)))
            size //= 2

    run_groups(up)

    def down(blocks):
        for b in blocks:
            @pl.when(b >= EXPERT_OUT_RING)
            def _():
                y_copy(b - EXPERT_OUT_RING).wait()
        for b in blocks:
            _to_row_tiles(ybuf, b % EXPERT_OUT_RING,
                          jnp.dot(act_ref[b - b0], wdn_bf[...], preferred_element_type=F32))
        for b in blocks:
            y_copy(b).start(priority=1)
        last = blocks[-1]

        @pl.when(last == n_blk - 1)
        def _():
            for i in range(EXPERT_OUT_RING):
                @pl.when(last >= i)
                def _():
                    y_copy(last - i).wait()

    run_groups(down)


def _expert_blocks(counts):
    blocks = (counts + EXPERT_ROWS - 1) // EXPERT_ROWS
    bend = jnp.cumsum(blocks)
    bstart = bend - blocks
    i32 = lambda a: a.astype(jnp.int32)
    return i32(bstart), i32(bend), i32(bend[-1]).reshape(1), i32(bstart * EXPERT_ROWS)


def _experts(xs, bstart, bend, counts, n_blk, w_gu, w_down):
    n_exp, d, ff2 = w_gu.shape
    ff = w_down.shape[1]
    n_rows = xs.shape[0] // ROW_TILE
    assert n_rows % EXPERT_ROWS == 0 and d == 2 * ROW_TILE * LANES
    max_blocks = (n_rows - n_exp * EXPERT_ROWS) // TOP_K // EXPERT_ROWS + 1
    grid_spec = pltpu.PrefetchScalarGridSpec(
        num_scalar_prefetch=4,
        grid=(n_exp,),
        in_specs=[pl.BlockSpec(memory_space=pl.ANY),
                  pl.BlockSpec((1, d, ff2), lambda e, *_: (e, 0, 0)),
                  pl.BlockSpec((1, ff, d), lambda e, *_: (e, 0, 0))],
        out_specs=pl.BlockSpec(memory_space=pl.ANY),
        scratch_shapes=[pltpu.VMEM((EXPERT_RING, EXPERT_ROWS * ROW_TILE, LANES), U32),
                        pltpu.VMEM((EXPERT_OUT_RING, EXPERT_ROWS * ROW_TILE, LANES), U32),
                        pltpu.VMEM((max_blocks, EXPERT_ROWS, ff), BF16),
                        pltpu.VMEM((d, ff2), BF16), pltpu.VMEM((ff, d), BF16),
                        pltpu.SemaphoreType.DMA((EXPERT_RING,)),
                        pltpu.SemaphoreType.DMA((EXPERT_OUT_RING,))],
    )
    return pl.pallas_call(
        _experts_kernel,
        out_shape=jax.ShapeDtypeStruct(xs.shape, U32),
        grid_spec=grid_spec,
        compiler_params=_cparams(("arbitrary",), 40),
        name="experts",
    )(bstart, bend, counts, n_blk, xs, w_gu, w_down)


def _combine_kernel(dest_ref, dnext_ref, gate_ref, h_ref, ys_ref, wgu_ref, wdn_ref, g_ref, b_ref, o_ref,
                    buf_ref, routed_ref, sem, *, alpha):
    tm = h_ref.shape[0]
    i = pl.program_id(0)
    cur = i % 2
    chunk = SUBLANES

    def issue(d_ref, slot, t):
        for k in range(TOP_K):
            pltpu.make_async_copy(ys_ref.at[_row_tile(d_ref[t * TOP_K + k])], buf_ref.at[slot, k, _row_tile(t)],
                                  sem.at[slot]).start(priority=k % 2)

    @pl.when(i == 0)
    def _():
        def first(t, carry):
            issue(dest_ref, 0, t)
            return carry
        lax.fori_loop(0, tm, first, 0, unroll=2)

    for k in range(TOP_K):
        pltpu.make_async_copy(ys_ref.at[pl.ds(0, tm * ROW_TILE)], buf_ref.at[cur, k], sem.at[cur]).wait()

    def weighted_sum(c):
        tok = pl.ds(pl.multiple_of(c * chunk, chunk), chunk)
        gate = gate_ref[tok, :]
        total = None
        for k in range(TOP_K):
            words = jnp.concatenate(
                [buf_ref[cur, k, pl.ds(pl.multiple_of(c * (chunk * ROW_TILE), chunk * ROW_TILE) + j, chunk,
                                       stride=ROW_TILE), :] for j in range(ROW_TILE)], axis=1)
            term = gate[:, k:k + 1] * _unpack_bf16_pairs(words)
            total = term if total is None else total + term
        routed_ref[tok, :] = total

    @pl.when(i + 1 < pl.num_programs(0))
    def _():
        def body(c, carry):
            for t in range(chunk):
                issue(dnext_ref, 1 - cur, c * chunk + t)
            weighted_sum(c)
            return carry
        lax.fori_loop(0, tm // chunk, body, 0)

    @pl.when(i + 1 == pl.num_programs(0))
    def _():
        def body(c, carry):
            weighted_sum(c)
            return carry
        lax.fori_loop(0, tm // chunk, body, 0)

    h = h_ref[...]
    ff = wdn_ref.shape[0]
    gu = jnp.dot(h.astype(BF16), wgu_ref[...], preferred_element_type=F32)
    act = (jax.nn.silu(gu[:, :ff]) * gu[:, ff:]).astype(BF16)
    acc = alpha * h + jnp.dot(act, wdn_ref[...], preferred_element_type=F32) + routed_ref[...]
    o_ref[...] = _layer_norm(acc, g_ref[...], b_ref[...])


def _combine(h, ys, dest, gate_t, shared_w_gu, shared_w_down, ln_g, ln_b, alpha):
    n_tok, d = h.shape
    tm = MOE_TOKEN_TILE
    n_tiles = n_tok // tm
    full = lambda a: pl.BlockSpec(a.shape, lambda i: (0,) * a.ndim)
    wgu = shared_w_gu.astype(BF16)
    wdn = shared_w_down.astype(BF16)
    g2 = ln_g.reshape(1, d)
    b2 = ln_b.reshape(1, d)
    return pl.pallas_call(
        functools.partial(_combine_kernel, alpha=alpha),
        out_shape=jax.ShapeDtypeStruct((n_tok, d), F32),
        grid=(n_tiles,),
        in_specs=[pl.BlockSpec((TOP_K * tm,), lambda i: (i,), memory_space=pltpu.SMEM),
                  pl.BlockSpec((TOP_K * tm,), lambda i: (jnp.minimum(i + 1, n_tiles - 1),),
                               memory_space=pltpu.SMEM),
                  pl.BlockSpec((tm, TOP_K), lambda i: (i, 0)),
                  pl.BlockSpec((tm, d), lambda i: (i, 0)),
                  pl.BlockSpec(memory_space=pl.ANY),
                  full(wgu), full(wdn), full(g2), full(b2)],
        out_specs=pl.BlockSpec((tm, d), lambda i: (i, 0)),
        scratch_shapes=[pltpu.VMEM((2, TOP_K, tm * ROW_TILE, LANES), U32), pltpu.VMEM((tm, d), F32),
                        pltpu.SemaphoreType.DMA((2,))],
        compiler_params=_cparams(("arbitrary",), 48),
        name="combine",
    )(dest, dest, gate_t, h, ys, wgu, wdn, g2, b2)


def _moe(h, router_w, router_bias, w_gu, w_down, shared_w_gu, shared_w_down, ln_g, ln_b, alpha):
    top_e, gate, rank, cnt = _router(h, router_w, router_bias)
    counts = cnt[:, 0].astype(jnp.int32)
    bstart, bend, n_blk, pad_start = _expert_blocks(counts)
    dest = _dest(top_e, rank, pad_start)
    dest_tiles = dest.T.reshape(-1)
    xs = _dispatch(h, dest_tiles, counts.shape[0])
    ys = _experts(xs, bstart, bend, counts, n_blk, w_gu, w_down)
    return _combine(h, ys, dest_tiles, gate.T, shared_w_gu, shared_w_down, ln_g, ln_b, alpha)


def kernel(x, w_in, att_norm_g, lam_re, lam_im, log_step, b_re, b_im, c_re, c_im, d_skip, w_glu, b_glu,
           ssm_norm_g, w_out, ln1_g, ln1_b, router_w, router_bias, w_gu, w_down, shared_w_gu,
           shared_w_down, ln2_g, ln2_b):
    bsz, seq, d = x.shape
    depth = w_in.shape[0]
    alpha = (2 * depth) ** 0.25
    h = x.reshape(bsz * seq, d)
    for i in range(depth):
        proj = _inproj(h, w_in[i].astype(BF16), seq)
        o_att = _attention(proj, bsz, seq)
        y_ssm = _s5(proj.reshape(bsz, seq, -1), 3 * ATT_WIDTH, lam_re[i], lam_im[i], log_step[i],
                    b_re[i], b_im[i], c_re[i], c_im[i], d_skip[i])
        h = _mixout(o_att, y_ssm.reshape(bsz * seq, -1), h, w_glu[i], b_glu[i], att_norm_g[i],
                    ssm_norm_g[i], w_out[i], ln1_g[i], ln1_b[i], alpha)
        h = _moe(h, router_w[i], router_bias[i], w_gu[i], w_down[i], shared_w_gu[i], shared_w_down[i],
                 ln2_g[i], ln2_b[i], alpha)
    return h.reshape(bsz, seq, d)
```

```python
import functools
import math

import jax
import jax.numpy as jnp
from jax import lax
from jax.experimental import pallas as pl
from jax.experimental.pallas import tpu as pltpu

F32 = jnp.float32
BF16 = jnp.bfloat16
U32 = jnp.uint32

ATT_HEADS = 8
HEAD_DIM = 64
ATT_WIDTH = ATT_HEADS * HEAD_DIM
SSM_CH = 16
SSM_STATE = 64
ROPE_THETA = 500000.0
ROT_DIM = HEAD_DIM // 4
DILATIONS = (1, 4, 16)
ATT_BLOCK = 128
ATT_GROUP = 16
N_EXPERTS = 256
TOP_K = 8
N_EXPERT_GROUPS = 8
TOPK_GROUPS = 4
ROUTED_SCALE = 2.5
LN_EPS = 1e-5
RMS_EPS = 1e-6

LANES = 128
SUBLANES = 8
EXPERT_ROWS = 144
MOE_TOKEN_TILE = 512
EXPERT_RING = 16
EXPERT_OUT_RING = 16
EXPERT_GROUP = 4
ROW_TILE = 4
NEG_INF = float("-inf")


def _cparams(sem, vmem_mb):
    return pltpu.CompilerParams(dimension_semantics=sem, vmem_limit_bytes=vmem_mb * 1024 * 1024)


def _inproj_kernel(x_ref, w_ref, cos_ref, sa_ref, sb_ref, o_ref, *, n_rot_cols):
    xb = x_ref[...].astype(BF16)
    cosf = cos_ref[...]
    sa = sa_ref[...]
    sb = sb_ref[...]
    width = o_ref.shape[1]
    chunk = 512
    for c in range(width // chunk):
        r = jnp.dot(xb, w_ref[:, c * chunk:(c + 1) * chunk], preferred_element_type=F32)
        if c * chunk < n_rot_cols:
            parts = []
            for s in range(chunk // LANES):
                t = r[:, s * LANES:(s + 1) * LANES]
                parts.append(t * cosf + pltpu.roll(t, LANES - ROT_DIM // 2, 1) * sa
                             + pltpu.roll(t, ROT_DIM // 2, 1) * sb)
            r = jnp.concatenate(parts, axis=1)
        o_ref[:, c * chunk:(c + 1) * chunk] = r


def _rope_lane_tables(seq):
    half = ROT_DIM // 2
    inv_freq = jnp.power(jnp.float32(ROPE_THETA), -jnp.arange(half, dtype=F32) / half)
    ang = jnp.arange(seq, dtype=F32)[:, None] * inv_freq[None, :]
    cos, sin = jnp.cos(ang), jnp.sin(ang)
    rest = HEAD_DIM - ROT_DIM
    cos_h = jnp.concatenate([cos, cos, jnp.ones((seq, rest), F32)], axis=1)
    sa_h = jnp.concatenate([-sin, jnp.zeros((seq, half + rest), F32)], axis=1)
    sb_h = jnp.concatenate([jnp.zeros((seq, half), F32), sin, jnp.zeros((seq, rest), F32)], axis=1)
    rep = LANES // HEAD_DIM
    return tuple(jnp.tile(t, (1, rep)) for t in (cos_h, sa_h, sb_h))


def _inproj(x2d, w_in_bf, seq):
    n_tok, d = x2d.shape
    width = w_in_bf.shape[1]
    tm = 512
    cosf, sa, sb = _rope_lane_tables(seq)
    tab_spec = pl.BlockSpec((tm, LANES), lambda i: (i % (seq // tm), 0))
    return pl.pallas_call(
        functools.partial(_inproj_kernel, n_rot_cols=2 * ATT_WIDTH),
        out_shape=jax.ShapeDtypeStruct((n_tok, width), F32),
        grid=(n_tok // tm,),
        in_specs=[pl.BlockSpec((tm, d), lambda i: (i, 0)),
                  pl.BlockSpec((d, width), lambda i: (0, 0)),
                  tab_spec, tab_spec, tab_spec],
        out_specs=pl.BlockSpec((tm, width), lambda i: (i, 0)),
        compiler_params=_cparams(("parallel",), 48),
        name="inproj",
    )(x2d, w_in_bf, cosf, sa, sb)


def _attn_kernel(q_ref, k_ref, v_ref, o_ref, qs_ref, ks_ref, vs_ref, tmp_ref, ob_ref, lb_ref, band_ref,
                 first_ref, *, seq):
    blk = ATT_BLOCK
    lane = lax.broadcasted_iota(jnp.int32, (1, LANES), 1)
    head0 = lane < HEAD_DIM
    scale = HEAD_DIM ** -0.5
    d1, d2 = DILATIONS[1], DILATIONS[2]
    assert DILATIONS[0] == 1 and d2 == d1 * d1
    seg = seq // d1
    sub = seg // d1

    qi = lax.broadcasted_iota(jnp.int32, (blk, 2 * blk), 0)
    kj = lax.broadcasted_iota(jnp.int32, (blk, 2 * blk), 1)
    dist = qi + blk - kj
    band_ref[...] = jnp.where((dist >= 0) & (dist <= blk), 0.0, NEG_INF)
    first_ref[...] = jnp.where((dist >= 0) & (kj >= blk), 0.0, NEG_INF)

    n_class = (1, d1, d2)
    class_len = (seq, seg, sub)
    base = [0]
    for c in range(len(DILATIONS)):
        base.append(base[c] + n_class[c] * (class_len[c] + blk))

    def kv_row0(c, g):
        return base[c] + g * (class_len[c] + blk)

    qs_ref[0] = (q_ref[...] * scale).astype(BF16)
    for a in range(d1):
        x = q_ref[pl.ds(a, seg, stride=d1), :] * scale
        tmp_ref[a * seg:(a + 1) * seg, :] = x
        qs_ref[1, a * seg:(a + 1) * seg, :] = x.astype(BF16)
    for g in range(d2):
        qs_ref[2, g * sub:(g + 1) * sub, :] = tmp_ref[pl.ds((g // d1) * seg + g % d1, sub, stride=d1),
                                                      :].astype(BF16)
    for src_ref, dst_ref in ((k_ref, ks_ref), (v_ref, vs_ref)):
        for c in range(len(DILATIONS)):
            for g in range(n_class[c]):
                dst_ref[kv_row0(c, g):kv_row0(c, g) + blk, :] = jnp.zeros((blk, LANES), BF16)
        dst_ref[kv_row0(0, 0) + blk:kv_row0(0, 0) + blk + seq, :] = src_ref[...].astype(BF16)
        for a in range(d1):
            x = src_ref[pl.ds(a, seg, stride=d1), :]
            tmp_ref[a * seg:(a + 1) * seg, :] = x
            dst_ref[kv_row0(1, a) + blk:kv_row0(1, a) + blk + seg, :] = x.astype(BF16)
        for g in range(d2):
            dst_ref[kv_row0(2, g) + blk:kv_row0(2, g) + blk + sub, :] = tmp_ref[
                pl.ds((g // d1) * seg + g % d1, sub, stride=d1), :].astype(BF16)

    def one_block(c, g, n, out_rows, bias_ref):
        q = qs_ref[c, pl.ds(aligned(g * class_len[c] + n * blk), blk), :]
        kv_rows = pl.ds(aligned(kv_row0(c, g) + n * blk), 2 * blk)
        kk = ks_ref[kv_rows, :]
        vv = vs_ref[kv_rows, :]
        outs = []
        lses = []
        for h in range(LANES // HEAD_DIM):
            hm = head0 if h == 0 else jnp.logical_not(head0)
            qh = jnp.where(hm, q, jnp.zeros_like(q))
            s = lax.dot_general(qh, kk, (((1,), (1,)), ((), ())), preferred_element_type=F32)
            s = s + bias_ref[...]
            m = jnp.max(s, axis=-1, keepdims=True)
            p = jnp.exp(s - m)
            den = jnp.sum(p, axis=-1, keepdims=True)
            outs.append(jnp.dot(p.astype(BF16), vv, preferred_element_type=F32) / den)
            lses.append(m + jnp.log(den))
        ob_ref[c, out_rows, :] = jnp.where(head0, outs[0], outs[1])
        lb_ref[c, out_rows, :] = jnp.where(head0, lses[0], lses[1])

    def run_blocks(n_blocks, fn):
        group = max(g for g in range(1, ATT_GROUP + 1) if n_blocks % g == 0)
        if n_blocks == group:
            for g in range(group):
                fn(g)
            return
        def body(it, carry):
            for g in range(group):
                fn(it * group + g)
            return carry
        lax.fori_loop(0, n_blocks // group, body, 0)

    def aligned(x):
        return x if isinstance(x, int) else pl.multiple_of(x, blk)

    one_block(0, 0, 0, pl.ds(0, blk), first_ref)
    run_blocks(seq // blk - 1,
               lambda i: one_block(0, 0, i + 1, pl.ds(aligned((i + 1) * blk), blk), band_ref))

    nb1 = seg // blk
    run_blocks(d1, lambda a: one_block(1, a, 0, pl.ds(a, blk, stride=d1), first_ref))
    def later1(i):
        a = i // (nb1 - 1)
        n = i - a * (nb1 - 1) + 1
        one_block(1, a, n, pl.ds(a + n * (d1 * blk), blk, stride=d1), band_ref)
    run_blocks(d1 * (nb1 - 1), later1)

    assert sub == blk
    def only2(g):
        a = g // d1
        one_block(2, g, 0, pl.ds(a + d1 * (g - a * d1), blk, stride=d2), first_ref)
    run_blocks(d2, only2)

    rc = 256
    def merge(i, carry):
        sl = pl.ds(pl.multiple_of(i * rc, rc), rc)
        l0 = lb_ref[0, sl, :]
        l1 = lb_ref[1, sl, :]
        l2 = lb_ref[2, sl, :]
        mx = jnp.maximum(jnp.maximum(l0, l1), l2)
        e0 = jnp.exp(l0 - mx)
        e1 = jnp.exp(l1 - mx)
        e2 = jnp.exp(l2 - mx)
        tot = e0 + e1 + e2
        o_ref[sl, :] = ((e0 / tot) * ob_ref[0, sl, :] + (e1 / tot) * ob_ref[1, sl, :]
                        + (e2 / tot) * ob_ref[2, sl, :])
        return carry
    lax.fori_loop(0, seq // rc, merge, 0)


def _attention(proj, bsz, seq):
    n_tok = proj.shape[0]
    pairs = ATT_WIDTH // LANES
    assert seq % (ATT_BLOCK * max(DILATIONS)) == 0
    kv_rows = sum(seq + d * ATT_BLOCK for d in DILATIONS)
    blk = (seq, LANES)
    return pl.pallas_call(
        functools.partial(_attn_kernel, seq=seq),
        out_shape=jax.ShapeDtypeStruct((n_tok, ATT_WIDTH), F32),
        grid=(bsz, pairs),
        in_specs=[pl.BlockSpec(blk, lambda b, h: (b, h)),
                  pl.BlockSpec(blk, lambda b, h: (b, pairs + h)),
                  pl.BlockSpec(blk, lambda b, h: (b, 2 * pairs + h))],
        out_specs=pl.BlockSpec(blk, lambda b, h: (b, h)),
        scratch_shapes=[pltpu.VMEM((len(DILATIONS), seq, LANES), BF16),
                        pltpu.VMEM((kv_rows, LANES), BF16),
                        pltpu.VMEM((kv_rows, LANES), BF16),
                        pltpu.VMEM((seq, LANES), F32),
                        pltpu.VMEM((len(DILATIONS), seq, LANES), F32),
                        pltpu.VMEM((len(DILATIONS), seq, LANES), F32),
                        pltpu.VMEM((ATT_BLOCK, 2 * ATT_BLOCK), F32),
                        pltpu.VMEM((ATT_BLOCK, 2 * ATT_BLOCK), F32)],
        compiler_params=_cparams(("parallel", "parallel"), 40),
        name="attn",
    )(proj, proj, proj)


def _s5_kernel(u_ref, bm_ref, lam_ref, cm_ref, dk_ref, o_ref, us_ref, st_ref, ys_ref, carry_ref, *, tc):
    bsz = u_ref.shape[0]
    half = st_ref.shape[1] // 2
    rows = tc * bsz
    mm_rows = 512

    @pl.when(pl.program_id(1) == 0)
    def _():
        carry_ref[...] = jnp.zeros_like(carry_ref)

    for b in range(bsz):
        us_ref[pl.ds(b, tc, stride=bsz), :] = u_ref[b]

    bm = bm_ref[0]
    for r0 in range(0, rows, mm_rows):
        st_ref[r0:r0 + mm_rows, :] = jnp.dot(us_ref[r0:r0 + mm_rows, :].astype(BF16), bm,
                                             preferred_element_type=F32)

    lam = lam_ref[0]
    lam_re = lam[:, :half]
    lam_im = lam[:, half:]

    def step(t, carry):
        xr, xi = carry
        sl = pl.ds(pl.multiple_of(t * bsz, bsz), bsz)
        nr = lam_re * xr - lam_im * xi + st_ref[sl, :half]
        ni = lam_re * xi + lam_im * xr + st_ref[sl, half:]
        st_ref[sl, :half] = nr
        st_ref[sl, half:] = ni
        return nr, ni

    xr, xi = lax.fori_loop(0, tc, step, (carry_ref[:, :half], carry_ref[:, half:]), unroll=4)
    carry_ref[:, :half] = xr
    carry_ref[:, half:] = xi

    cm = cm_ref[0]
    for r0 in range(0, rows, mm_rows):
        ys_ref[r0:r0 + mm_rows, :] = jnp.dot(st_ref[r0:r0 + mm_rows, :].astype(BF16), cm,
                                             preferred_element_type=F32)
    dk = dk_ref[...]
    for b in range(bsz):
        o_ref[b] = ys_ref[pl.ds(b, tc, stride=bsz), :] + dk * u_ref[b]


def _s5_params(lam_re, lam_im, log_step, b_re, b_im, c_re, c_im, bsz):
    groups = lam_re.shape[0]
    gpc = LANES // SSM_CH
    n_chunks = groups // gpc
    lam = lax.complex(lam_re.astype(F32), lam_im.astype(F32))
    step = jnp.exp(log_step.astype(F32))[:, None]
    lam_bar = jnp.exp(lam * step)
    bmat = lax.complex(b_re.astype(F32), b_im.astype(F32))
    b_bar = ((lam_bar - 1.0) / lam)[..., None] * bmat
    eye = jnp.eye(gpc, dtype=F32)

    def block_diag_in(t):
        t = t.reshape(n_chunks, gpc, SSM_STATE, SSM_CH)
        return jnp.einsum('ngpc,gh->ngchp', t, eye).reshape(n_chunks, gpc * SSM_CH, gpc * SSM_STATE)

    def block_diag_out(t):
        t = t.reshape(n_chunks, gpc, SSM_CH, SSM_STATE)
        return jnp.einsum('ngcp,gh->ngphc', t, eye).reshape(n_chunks, gpc * SSM_STATE, gpc * SSM_CH)

    bm = jnp.concatenate([block_diag_in(b_bar.real), block_diag_in(b_bar.imag)], axis=2).astype(BF16)
    cm = jnp.concatenate([block_diag_out(c_re.astype(F32)), block_diag_out(-c_im.astype(F32))],
                         axis=1).astype(BF16)
    lam_row = jnp.concatenate([lam_bar.real.reshape(n_chunks, gpc * SSM_STATE),
                               lam_bar.imag.reshape(n_chunks, gpc * SSM_STATE)], axis=1)
    lam_t = jnp.broadcast_to(lam_row[:, None, :], (n_chunks, bsz, 2 * gpc * SSM_STATE))
    return bm, lam_t, cm, n_chunks


def _s5(proj3, u_col0, lam_re, lam_im, log_step, b_re, b_im, c_re, c_im, d_skip):
    bsz, seq, _ = proj3.shape
    assert bsz == SUBLANES
    bm, lam_t, cm, n_chunks = _s5_params(lam_re, lam_im, log_step, b_re, b_im, c_re, c_im, bsz)
    width = n_chunks * LANES
    tc = 256
    st_cols = bm.shape[2]
    ublk0 = u_col0 // LANES
    return pl.pallas_call(
        functools.partial(_s5_kernel, tc=tc),
        out_shape=jax.ShapeDtypeStruct((bsz, seq, width), F32),
        grid=(n_chunks, seq // tc),
        in_specs=[pl.BlockSpec((bsz, tc, LANES), lambda c, t: (0, t, ublk0 + c)),
                  pl.BlockSpec((1, LANES, st_cols), lambda c, t: (c, 0, 0)),
                  pl.BlockSpec((1, bsz, st_cols), lambda c, t: (c, 0, 0)),
                  pl.BlockSpec((1, st_cols, LANES), lambda c, t: (c, 0, 0)),
                  pl.BlockSpec((1, LANES), lambda c, t: (0, c))],
        out_specs=pl.BlockSpec((bsz, tc, LANES), lambda c, t: (0, t, c)),
        scratch_shapes=[pltpu.VMEM((tc * bsz, LANES), F32),
                        pltpu.VMEM((tc * bsz, st_cols), F32),
                        pltpu.VMEM((tc * bsz, LANES), F32),
                        pltpu.VMEM((bsz, st_cols), F32)],
        compiler_params=_cparams(("arbitrary", "arbitrary"), 40),
        name="s5",
    )(proj3, bm, lam_t, cm, d_skip.reshape(1, width).astype(F32))


def _layer_norm(v, g, b):
    mu = jnp.mean(v, axis=-1, keepdims=True)
    var = jnp.mean(jnp.square(v - mu), axis=-1, keepdims=True)
    return (v - mu) * lax.rsqrt(var + LN_EPS) * g + b


def _rms_norm(v, g):
    return v * lax.rsqrt(jnp.mean(jnp.square(v), axis=-1, keepdims=True) + RMS_EPS) * g


def _mixout_kernel(att_ref, ssm_ref, x_ref, wglu_ref, bglu_ref, ag_ref, sg_ref, wout_ref, g_ref, b_ref,
                   o_ref, *, alpha):
    y = jax.nn.gelu(ssm_ref[...])
    z = jnp.dot(y.astype(BF16), wglu_ref[...], preferred_element_type=F32) + bglu_ref[...]
    o_ssm = y * jax.nn.sigmoid(z)
    a = _rms_norm(att_ref[...], ag_ref[...]).astype(BF16)
    s = _rms_norm(o_ssm, sg_ref[...]).astype(BF16)
    wa = att_ref.shape[1]
    mix = (jnp.dot(a, wout_ref[:wa, :], preferred_element_type=F32)
           + jnp.dot(s, wout_ref[wa:, :], preferred_element_type=F32))
    o_ref[...] = _layer_norm(alpha * x_ref[...] + mix, g_ref[...], b_ref[...])


def _mixout(o_att, y_ssm, x2d, w_glu, b_glu, att_g, ssm_g, w_out, ln_g, ln_b, alpha):
    n_tok, d = x2d.shape
    wa = o_att.shape[1]
    ws = y_ssm.shape[1]
    tm = 256
    row = lambda w: pl.BlockSpec((tm, w), lambda i: (i, 0))
    full = lambda a: pl.BlockSpec(a.shape, lambda i: (0,) * a.ndim)
    args = (o_att, y_ssm, x2d, w_glu.astype(BF16), b_glu.reshape(1, ws), att_g.reshape(1, wa),
            ssm_g.reshape(1, ws), w_out.astype(BF16), ln_g.reshape(1, d), ln_b.reshape(1, d))
    return pl.pallas_call(
        functools.partial(_mixout_kernel, alpha=alpha),
        out_shape=jax.ShapeDtypeStruct((n_tok, d), F32),
        grid=(n_tok // tm,),
        in_specs=[row(wa), row(ws), row(d)] + [full(a) for a in args[3:]],
        out_specs=row(d),
        compiler_params=_cparams(("parallel",), 32),
        name="mixout",
    )(*args)


def _split_bf16(v):
    hi = v.astype(BF16)
    lo = (v - hi.astype(F32)).astype(BF16)
    return hi, lo


def _router_kernel(h_ref, wt_ref, bias_ref, e_ref, g_ref, r_ref, cnt_ref, run_ref):
    tm = h_ref.shape[0]
    n_exp = wt_ref.shape[0]
    gsz = n_exp // N_EXPERT_GROUPS

    @pl.when(pl.program_id(0) == 0)
    def _():
        run_ref[...] = jnp.zeros_like(run_ref)

    w_hi, w_lo = _split_bf16(wt_ref[...])
    h_hi, h_lo = _split_bf16(h_ref[...])
    nt = (((1,), (1,)), ((), ()))
    logits = (lax.dot_general(w_hi, h_hi, nt, preferred_element_type=F32)
              + lax.dot_general(w_hi, h_lo, nt, preferred_element_type=F32)
              + lax.dot_general(w_lo, h_hi, nt, preferred_element_type=F32))
    scores = jax.nn.sigmoid(logits)
    choice = scores + bias_ref[:, 0:1]

    gio = lax.broadcasted_iota(jnp.int32, (gsz, tm), 0).astype(F32)
    gscore = []
    for g in range(N_EXPERT_GROUPS):
        cg = choice[g * gsz:(g + 1) * gsz, :]
        m1 = jnp.max(cg, axis=0, keepdims=True)
        i1 = jnp.min(jnp.where(cg == m1, gio, float(gsz)), axis=0, keepdims=True)
        m2 = jnp.max(jnp.where(gio == i1, NEG_INF, cg), axis=0, keepdims=True)
        gscore.append(m1 + m2)
    masked = []
    for g in range(N_EXPERT_GROUPS):
        beat = jnp.zeros((1, tm), F32)
        for o in range(N_EXPERT_GROUPS):
            if o == g:
                continue
            wins = (gscore[o] >= gscore[g]) if o < g else (gscore[o] > gscore[g])
            beat = beat + jnp.where(wins, 1.0, 0.0)
        keep = beat < float(TOPK_GROUPS)
        masked.append(jnp.where(keep, choice[g * gsz:(g + 1) * gsz, :], NEG_INF))
    cur = jnp.concatenate(masked, axis=0)

    eio = lax.broadcasted_iota(jnp.int32, (n_exp, tm), 0).astype(F32)
    idxs = []
    gates = []
    onehot = jnp.zeros((n_exp, tm), F32)
    for _ in range(TOP_K):
        m = jnp.max(cur, axis=0, keepdims=True)
        idx = jnp.min(jnp.where(cur == m, eio, float(n_exp)), axis=0, keepdims=True)
        hit = eio == idx
        idxs.append(idx)
        gates.append(jnp.sum(jnp.where(hit, scores, 0.0), axis=0, keepdims=True))
        cur = jnp.where(hit, NEG_INF, cur)
        onehot = onehot + jnp.where(hit, 1.0, 0.0)
    gate = jnp.concatenate(gates, axis=0)
    gate = ROUTED_SCALE * gate / (jnp.sum(gate, axis=0, keepdims=True) + 1e-20)

    si = lax.broadcasted_iota(jnp.int32, (tm, tm), 0)
    ti = lax.broadcasted_iota(jnp.int32, (tm, tm), 1)
    upper = jnp.where(si < ti, 1.0, 0.0).astype(BF16)
    before = jnp.dot(onehot.astype(BF16), upper, preferred_element_type=F32) + run_ref[:, 0:1]
    ranks = [jnp.sum(jnp.where(eio == idx, before, 0.0), axis=0, keepdims=True) for idx in idxs]

    e_ref[...] = jnp.concatenate(idxs, axis=0).astype(jnp.int32)
    g_ref[...] = gate
    r_ref[...] = jnp.concatenate(ranks, axis=0).astype(jnp.int32)
    run_ref[...] = run_ref[...] + jnp.sum(onehot, axis=1, keepdims=True)
    cnt_ref[...] = run_ref[...]


def _router(h, router_w, router_bias):
    n_tok, d = h.shape
    n_exp = router_w.shape[1]
    tm = 256
    wt = router_w.astype(F32).T
    bias = jnp.broadcast_to(router_bias.astype(F32)[:, None], (n_exp, LANES))
    tok = pl.BlockSpec((TOP_K, tm), lambda i: (0, i))
    return pl.pallas_call(
        _router_kernel,
        out_shape=(jax.ShapeDtypeStruct((TOP_K, n_tok), jnp.int32),
                   jax.ShapeDtypeStruct((TOP_K, n_tok), F32),
                   jax.ShapeDtypeStruct((TOP_K, n_tok), jnp.int32),
                   jax.ShapeDtypeStruct((n_exp, LANES), F32)),
        grid=(n_tok // tm,),
        in_specs=[pl.BlockSpec((tm, d), lambda i: (i, 0)),
                  pl.BlockSpec((n_exp, d), lambda i: (0, 0)),
                  pl.BlockSpec((n_exp, LANES), lambda i: (0, 0))],
        out_specs=(tok, tok, tok, pl.BlockSpec((n_exp, LANES), lambda i: (0, 0))),
        scratch_shapes=[pltpu.VMEM((n_exp, LANES), F32)],
        compiler_params=_cparams(("arbitrary",), 32),
        name="router",
    )(h, wt, bias)


def _dest_kernel(e_ref, r_ref, st_ref, d_ref):
    n_exp = st_ref.shape[0]
    tm = e_ref.shape[1]
    eio = lax.broadcasted_iota(jnp.int32, (n_exp, tm), 0)
    start = st_ref[:, 0:1]
    rows = [jnp.sum(jnp.where(eio == e_ref[k:k + 1, :], start, 0.0), axis=0, keepdims=True)
            for k in range(TOP_K)]
    d_ref[...] = jnp.concatenate(rows, axis=0).astype(jnp.int32) + r_ref[...]


def _dest(top_e, rank, starts):
    n_tok = top_e.shape[1]
    n_exp = starts.shape[0]
    tm = 512
    st = jnp.broadcast_to(starts.astype(F32)[:, None], (n_exp, LANES))
    tok = pl.BlockSpec((TOP_K, tm), lambda i: (0, i))
    return pl.pallas_call(
        _dest_kernel,
        out_shape=jax.ShapeDtypeStruct((TOP_K, n_tok), jnp.int32),
        grid=(n_tok // tm,),
        in_specs=[tok, tok, pl.BlockSpec((n_exp, LANES), lambda i: (0, 0))],
        out_specs=tok,
        compiler_params=_cparams(("parallel",), 32),
        name="dest",
    )(top_e, rank, st)


def _pack_bf16_pairs(val):
    half = val.shape[1] // 2
    lo = pltpu.bitcast(val[:, :half].astype(BF16).astype(F32), U32)
    hi = pltpu.bitcast(val[:, half:].astype(BF16).astype(F32), U32)
    return (lo >> 16) | (hi & jnp.uint32(0xFFFF0000))


def _unpack_bf16_pairs(words):
    lo = pltpu.bitcast(words << 16, F32)
    hi = pltpu.bitcast(words & jnp.uint32(0xFFFF0000), F32)
    return jnp.concatenate([lo, hi], axis=1)


def _to_row_tiles(dst_ref, slot, val):
    rows = val.shape[0]
    words = _pack_bf16_pairs(val)
    for j in range(ROW_TILE):
        dst_ref[slot, pl.ds(j, rows, stride=ROW_TILE), :] = words[:, j * LANES:(j + 1) * LANES]


def _row_tile_words(src_ref, idx, rows):
    return jnp.concatenate([src_ref[(*idx, pl.ds(j, rows, stride=ROW_TILE), slice(None))]
                            for j in range(ROW_TILE)], axis=1)


def _row_tile(r):
    return pl.ds(pl.multiple_of(r * ROW_TILE, ROW_TILE), ROW_TILE)


def _dispatch_kernel(dest_ref, h_ref, xs_ref, ht_ref, sem):
    tm = h_ref.shape[0]
    i = pl.program_id(0)
    cur = i % 2
    _to_row_tiles(ht_ref, cur, h_ref[...])

    def issue(t, carry):
        for k in range(TOP_K):
            pltpu.make_async_copy(ht_ref.at[cur, _row_tile(t)], xs_ref.at[_row_tile(dest_ref[t * TOP_K + k])],
                                  sem.at[cur]).start(priority=k % 2)
        return carry
    lax.fori_loop(0, tm, issue, 0, unroll=2)

    def drain(slot):
        for k in range(TOP_K):
            pltpu.make_async_copy(ht_ref.at[slot], xs_ref.at[pl.ds(0, tm * ROW_TILE)], sem.at[slot]).wait()

    @pl.when(i > 0)
    def _():
        drain(1 - cur)

    @pl.when(i == pl.num_programs(0) - 1)
    def _():
        drain(cur)


def _dispatch(h, dest, n_exp):
    n_tok, d = h.shape
    assert d == 2 * ROW_TILE * LANES
    tm = MOE_TOKEN_TILE
    n_rows = (pl.cdiv(n_tok * TOP_K, EXPERT_ROWS) + n_exp) * EXPERT_ROWS
    return pl.pallas_call(
        _dispatch_kernel,
        out_shape=jax.ShapeDtypeStruct((n_rows * ROW_TILE, LANES), U32),
        grid=(n_tok // tm,),
        in_specs=[pl.BlockSpec((TOP_K * tm,), lambda i: (i,), memory_space=pltpu.SMEM),
                  pl.BlockSpec((tm, d), lambda i: (i, 0))],
        out_specs=pl.BlockSpec(memory_space=pl.ANY),
        scratch_shapes=[pltpu.VMEM((2, tm * ROW_TILE, LANES), U32), pltpu.SemaphoreType.DMA((2,))],
        compiler_params=_cparams(("arbitrary",), 32),
        name="dispatch",
    )(dest, h)


def _experts_kernel(bstart_ref, bend_ref, cnt_ref, nblk_ref, xs_ref, wgu_ref, wdn_ref, ys_ref,
                    xbuf, ybuf, act_ref, wgu_bf, wdn_bf, xsem, ysem):
    e = pl.program_id(0)
    n_blk = nblk_ref[0]
    trows = xbuf.shape[1]
    rows = trows // ROW_TILE
    ff = wdn_bf.shape[0]
    b0 = bstart_ref[e]
    b1 = bend_ref[e]

    def block_rows(b):
        return pl.ds(pl.multiple_of(b * trows, trows), trows)

    def x_copy(b):
        slot = b % EXPERT_RING
        return pltpu.make_async_copy(xs_ref.at[block_rows(b)], xbuf.at[slot], xsem.at[slot])

    def y_copy(b):
        slot = b % EXPERT_OUT_RING
        return pltpu.make_async_copy(ybuf.at[slot], ys_ref.at[block_rows(b)], ysem.at[slot])

    @pl.when(e == 0)
    def _():
        for i in range(EXPERT_RING):
            @pl.when(i < n_blk)
            def _():
                x_copy(i).start(priority=1)

    wgu_bf[...] = wgu_ref[0].astype(BF16)
    wdn_bf[...] = wdn_ref[0].astype(BF16)

    def up(blocks):
        for b in blocks:
            x_copy(b).wait()
        for b in blocks:
            words = _row_tile_words(xbuf, (b % EXPERT_RING,), rows)
            row = lax.broadcasted_iota(jnp.int32, (rows, 1), 0)
            words = jnp.where(row < cnt_ref[e] - (b - b0) * rows, words, jnp.uint32(0))
            xb = _unpack_bf16_pairs(words).astype(BF16)
            gu = jnp.dot(xb, wgu_bf[...], preferred_element_type=F32)
            act_ref[b - b0] = (jax.nn.silu(gu[:, :ff]) * gu[:, ff:]).astype(BF16)
        for b in blocks:
            @pl.when(b + EXPERT_RING < n_blk)
            def _():
                x_copy(b + EXPERT_RING).start(priority=1)

    n_mine = b1 - b0

    def run_groups(fn):
        def body(p, carry):
            fn(tuple(b0 + EXPERT_GROUP * p + j for j in range(EXPERT_GROUP)))
            return carry
        lax.fori_loop(0, n_mine // EXPERT_GROUP, body, 0)
        size = EXPERT_GROUP // 2
        while size >= 1:
            @pl.when(n_mine & size != 0)
            def _(size=size):
                start = b0 + (n_mine // (2 * size)) * (2 * size)
                fn(tuple(start + j for j in range(size)))
            size //= 2

    run_groups(up)

    def down(blocks):
        for b in blocks:
            @pl.when(b >= EXPERT_OUT_RING)
            def _():
                y_copy(b - EXPERT_OUT_RING).wait()
        for b in blocks:
            _to_row_tiles(ybuf, b % EXPERT_OUT_RING,
                          jnp.dot(act_ref[b - b0], wdn_bf[...], preferred_element_type=F32))
        for b in blocks:
            y_copy(b).start(priority=1)
        last = blocks[-1]

        @pl.when(last == n_blk - 1)
        def _():
            for i in range(EXPERT_OUT_RING):
                @pl.when(last >= i)
                def _():
                    y_copy(last - i).wait()

    run_groups(down)


def _expert_blocks(counts):
    blocks = (counts + EXPERT_ROWS - 1) // EXPERT_ROWS
    bend = jnp.cumsum(blocks)
    bstart = bend - blocks
    i32 = lambda a: a.astype(jnp.int32)
    return i32(bstart), i32(bend), i32(bend[-1]).reshape(1), i32(bstart * EXPERT_ROWS)


def _experts(xs, bstart, bend, counts, n_blk, w_gu, w_down, n_tok):
    n_exp, d, ff2 = w_gu.shape
    ff = w_down.shape[1]
    n_rows = xs.shape[0] // ROW_TILE
    assert n_rows % EXPERT_ROWS == 0 and d == 2 * ROW_TILE * LANES
    max_blocks = pl.cdiv(n_tok, EXPERT_ROWS)
    grid_spec = pltpu.PrefetchScalarGridSpec(
        num_scalar_prefetch=4,
        grid=(n_exp,),
        in_specs=[pl.BlockSpec(memory_space=pl.ANY),
                  pl.BlockSpec((1, d, ff2), lambda e, *_: (e, 0, 0)),
                  pl.BlockSpec((1, ff, d), lambda e, *_: (e, 0, 0))],
        out_specs=pl.BlockSpec(memory_space=pl.ANY),
        scratch_shapes=[pltpu.VMEM((EXPERT_RING, EXPERT_ROWS * ROW_TILE, LANES), U32),
                        pltpu.VMEM((EXPERT_OUT_RING, EXPERT_ROWS * ROW_TILE, LANES), U32),
                        pltpu.VMEM((max_blocks, EXPERT_ROWS, ff), BF16),
                        pltpu.VMEM((d, ff2), BF16), pltpu.VMEM((ff, d), BF16),
                        pltpu.SemaphoreType.DMA((EXPERT_RING,)),
                        pltpu.SemaphoreType.DMA((EXPERT_OUT_RING,))],
    )
    return pl.pallas_call(
        _experts_kernel,
        out_shape=jax.ShapeDtypeStruct(xs.shape, U32),
        grid_spec=grid_spec,
        compiler_params=_cparams(("arbitrary",), 40),
        name="experts",
    )(bstart, bend, counts, n_blk, xs, w_gu, w_down)


def _combine_kernel(dest_ref, dnext_ref, gate_ref, h_ref, ys_ref, wgu_ref, wdn_ref, g_ref, b_ref, o_ref,
                    buf_ref, routed_ref, sem, *, alpha):
    tm = h_ref.shape[0]
    i = pl.program_id(0)
    cur = i % 2
    chunk = SUBLANES

    def issue(d_ref, slot, t):
        for k in range(TOP_K):
            pltpu.make_async_copy(ys_ref.at[_row_tile(d_ref[t * TOP_K + k])], buf_ref.at[slot, k, _row_tile(t)],
                                  sem.at[slot]).start(priority=k % 2)

    @pl.when(i == 0)
    def _():
        def first(t, carry):
            issue(dest_ref, 0, t)
            return carry
        lax.fori_loop(0, tm, first, 0, unroll=2)

    for k in range(TOP_K):
        pltpu.make_async_copy(ys_ref.at[pl.ds(0, tm * ROW_TILE)], buf_ref.at[cur, k], sem.at[cur]).wait()

    def weighted_sum(c):
        tok = pl.ds(pl.multiple_of(c * chunk, chunk), chunk)
        gate = gate_ref[tok, :]
        total = None
        for k in range(TOP_K):
            words = jnp.concatenate(
                [buf_ref[cur, k, pl.ds(pl.multiple_of(c * (chunk * ROW_TILE), chunk * ROW_TILE) + j, chunk,
                                       stride=ROW_TILE), :] for j in range(ROW_TILE)], axis=1)
            term = gate[:, k:k + 1] * _unpack_bf16_pairs(words)
            total = term if total is None else total + term
        routed_ref[tok, :] = total

    @pl.when(i + 1 < pl.num_programs(0))
    def _():
        def body(c, carry):
            for t in range(chunk):
                issue(dnext_ref, 1 - cur, c * chunk + t)
            weighted_sum(c)
            return carry
        lax.fori_loop(0, tm // chunk, body, 0)

    @pl.when(i + 1 == pl.num_programs(0))
    def _():
        def body(c, carry):
            weighted_sum(c)
            return carry
        lax.fori_loop(0, tm // chunk, body, 0)

    h = h_ref[...]
    ff = wdn_ref.shape[0]
    gu = jnp.dot(h.astype(BF16), wgu_ref[...], preferred_element_type=F32)
    act = (jax.nn.silu(gu[:, :ff]) * gu[:, ff:]).astype(BF16)
    acc = alpha * h + jnp.dot(act, wdn_ref[...], preferred_element_type=F32) + routed_ref[...]
    o_ref[...] = _layer_norm(acc, g_ref[...], b_ref[...])


def _combine(h, ys, dest, gate_t, shared_w_gu, shared_w_down, ln_g, ln_b, alpha):
    n_tok, d = h.shape
    tm = MOE_TOKEN_TILE
    n_tiles = n_tok // tm
    full = lambda a: pl.BlockSpec(a.shape, lambda i: (0,) * a.ndim)
    wgu = shared_w_gu.astype(BF16)
    wdn = shared_w_down.astype(BF16)
    g2 = ln_g.reshape(1, d)
    b2 = ln_b.reshape(1, d)
    return pl.pallas_call(
        functools.partial(_combine_kernel, alpha=alpha),
        out_shape=jax.ShapeDtypeStruct((n_tok, d), F32),
        grid=(n_tiles,),
        in_specs=[pl.BlockSpec((TOP_K * tm,), lambda i: (i,), memory_space=pltpu.SMEM),
                  pl.BlockSpec((TOP_K * tm,), lambda i: (jnp.minimum(i + 1, n_tiles - 1),),
                               memory_space=pltpu.SMEM),
                  pl.BlockSpec((tm, TOP_K), lambda i: (i, 0)),
                  pl.BlockSpec((tm, d), lambda i: (i, 0)),
                  pl.BlockSpec(memory_space=pl.ANY),
                  full(wgu), full(wdn), full(g2), full(b2)],
        out_specs=pl.BlockSpec((tm, d), lambda i: (i, 0)),
        scratch_shapes=[pltpu.VMEM((2, TOP_K, tm * ROW_TILE, LANES), U32), pltpu.VMEM((tm, d), F32),
                        pltpu.SemaphoreType.DMA((2,))],
        compiler_params=_cparams(("arbitrary",), 48),
        name="combine",
    )(dest, dest, gate_t, h, ys, wgu, wdn, g2, b2)


def _moe(h, router_w, router_bias, w_gu, w_down, shared_w_gu, shared_w_down, ln_g, ln_b, alpha):
    top_e, gate, rank, cnt = _router(h, router_w, router_bias)
    counts = cnt[:, 0].astype(jnp.int32)
    bstart, bend, n_blk, pad_start = _expert_blocks(counts)
    dest = _dest(top_e, rank, pad_start)
    dest_tiles = dest.T.reshape(-1)
    xs = _dispatch(h, dest_tiles, counts.shape[0])
    ys = _experts(xs, bstart, bend, counts, n_blk, w_gu, w_down, h.shape[0])
    return _combine(h, ys, dest_tiles, gate.T, shared_w_gu, shared_w_down, ln_g, ln_b, alpha)


def kernel(x, w_in, att_norm_g, lam_re, lam_im, log_step, b_re, b_im, c_re, c_im, d_skip, w_glu, b_glu,
           ssm_norm_g, w_out, ln1_g, ln1_b, router_w, router_bias, w_gu, w_down, shared_w_gu,
           shared_w_down, ln2_g, ln2_b):
    bsz, seq, d = x.shape
    depth = w_in.shape[0]
    alpha = (2 * depth) ** 0.25
    h = x.reshape(bsz * seq, d)
    for i in range(depth):
        proj = _inproj(h, w_in[i].astype(BF16), seq)
        o_att = _attention(proj, bsz, seq)
        y_ssm = _s5(proj.reshape(bsz, seq, -1), 3 * ATT_WIDTH, lam_re[i], lam_im[i], log_step[i],
                    b_re[i], b_im[i], c_re[i], c_im[i], d_skip[i])
        h = _mixout(o_att, y_ssm.reshape(bsz * seq, -1), h, w_glu[i], b_glu[i], att_norm_g[i],
                    ssm_norm_g[i], w_out[i], ln1_g[i], ln1_b[i], alpha)
        h = _moe(h, router_w[i], router_bias[i], w_gu[i], w_down[i], shared_w_gu[i], shared_w_down[i],
                 ln2_g[i], ln2_b[i], alpha)
    return h.reshape(bsz, seq, d)
```

```python
import functools
import math

import jax
import jax.numpy as jnp
from jax import lax
from jax.experimental import pallas as pl
from jax.experimental.pallas import tpu as pltpu

F32 = jnp.float32
BF16 = jnp.bfloat16
U32 = jnp.uint32

ATT_HEADS = 8
HEAD_DIM = 64
ATT_WIDTH = ATT_HEADS * HEAD_DIM
SSM_CH = 16
SSM_STATE = 64
ROPE_THETA = 500000.0
ROT_DIM = HEAD_DIM // 4
DILATIONS = (1, 4, 16)
ATT_BLOCK = 128
ATT_GROUP = 16
N_EXPERTS = 256
TOP_K = 8
N_EXPERT_GROUPS = 8
TOPK_GROUPS = 4
ROUTED_SCALE = 2.5
LN_EPS = 1e-5
RMS_EPS = 1e-6

LANES = 128
SUBLANES = 8
EXPERT_ROWS = 144
MOE_TOKEN_TILE = 512
EXPERT_RING = 16
EXPERT_OUT_RING = 16
EXPERT_GROUP = 4
EXPERT_WEIGHT_RING = 3
ROW_TILE = 4
NEG_INF = float("-inf")


def _cparams(sem, vmem_mb):
    return pltpu.CompilerParams(dimension_semantics=sem, vmem_limit_bytes=vmem_mb * 1024 * 1024)


def _inproj_kernel(x_ref, w_ref, cos_ref, sa_ref, sb_ref, o_ref, *, n_rot_cols):
    xb = x_ref[...].astype(BF16)
    cosf = cos_ref[...]
    sa = sa_ref[...]
    sb = sb_ref[...]
    width = o_ref.shape[1]
    chunk = 512
    for c in range(width // chunk):
        r = jnp.dot(xb, w_ref[:, c * chunk:(c + 1) * chunk], preferred_element_type=F32)
        if c * chunk < n_rot_cols:
            parts = []
            for s in range(chunk // LANES):
                t = r[:, s * LANES:(s + 1) * LANES]
                parts.append(t * cosf + pltpu.roll(t, LANES - ROT_DIM // 2, 1) * sa
                             + pltpu.roll(t, ROT_DIM // 2, 1) * sb)
            r = jnp.concatenate(parts, axis=1)
        o_ref[:, c * chunk:(c + 1) * chunk] = r


def _rope_lane_tables(seq):
    half = ROT_DIM // 2
    inv_freq = jnp.power(jnp.float32(ROPE_THETA), -jnp.arange(half, dtype=F32) / half)
    ang = jnp.arange(seq, dtype=F32)[:, None] * inv_freq[None, :]
    cos, sin = jnp.cos(ang), jnp.sin(ang)
    rest = HEAD_DIM - ROT_DIM
    cos_h = jnp.concatenate([cos, cos, jnp.ones((seq, rest), F32)], axis=1)
    sa_h = jnp.concatenate([-sin, jnp.zeros((seq, half + rest), F32)], axis=1)
    sb_h = jnp.concatenate([jnp.zeros((seq, half), F32), sin, jnp.zeros((seq, rest), F32)], axis=1)
    rep = LANES // HEAD_DIM
    return tuple(jnp.tile(t, (1, rep)) for t in (cos_h, sa_h, sb_h))


def _inproj(x2d, w_in_bf, seq):
    n_tok, d = x2d.shape
    width = w_in_bf.shape[1]
    tm = 512
    cosf, sa, sb = _rope_lane_tables(seq)
    tab_spec = pl.BlockSpec((tm, LANES), lambda i: (i % (seq // tm), 0))
    return pl.pallas_call(
        functools.partial(_inproj_kernel, n_rot_cols=2 * ATT_WIDTH),
        out_shape=jax.ShapeDtypeStruct((n_tok, width), F32),
        grid=(n_tok // tm,),
        in_specs=[pl.BlockSpec((tm, d), lambda i: (i, 0)),
                  pl.BlockSpec((d, width), lambda i: (0, 0)),
                  tab_spec, tab_spec, tab_spec],
        out_specs=pl.BlockSpec((tm, width), lambda i: (i, 0)),
        compiler_params=_cparams(("parallel",), 48),
        name="inproj",
    )(x2d, w_in_bf, cosf, sa, sb)


def _attn_kernel(q_ref, k_ref, v_ref, o_ref, qs_ref, ks_ref, vs_ref, tmp_ref, ob_ref, lb_ref, band_ref,
                 first_ref, *, seq):
    blk = ATT_BLOCK
    lane = lax.broadcasted_iota(jnp.int32, (1, LANES), 1)
    head0 = lane < HEAD_DIM
    scale = HEAD_DIM ** -0.5
    d1, d2 = DILATIONS[1], DILATIONS[2]
    assert DILATIONS[0] == 1 and d2 == d1 * d1
    seg = seq // d1
    sub = seg // d1

    qi = lax.broadcasted_iota(jnp.int32, (blk, 2 * blk), 0)
    kj = lax.broadcasted_iota(jnp.int32, (blk, 2 * blk), 1)
    dist = qi + blk - kj
    band_ref[...] = jnp.where((dist >= 0) & (dist <= blk), 0.0, NEG_INF)
    first_ref[...] = jnp.where((dist >= 0) & (kj >= blk), 0.0, NEG_INF)

    n_class = (1, d1, d2)
    class_len = (seq, seg, sub)
    base = [0]
    for c in range(len(DILATIONS)):
        base.append(base[c] + n_class[c] * (class_len[c] + blk))

    def kv_row0(c, g):
        return base[c] + g * (class_len[c] + blk)

    qs_ref[0] = (q_ref[...] * scale).astype(BF16)
    for a in range(d1):
        x = q_ref[pl.ds(a, seg, stride=d1), :] * scale
        tmp_ref[a * seg:(a + 1) * seg, :] = x
        qs_ref[1, a * seg:(a + 1) * seg, :] = x.astype(BF16)
    for g in range(d2):
        qs_ref[2, g * sub:(g + 1) * sub, :] = tmp_ref[pl.ds((g // d1) * seg + g % d1, sub, stride=d1),
                                                      :].astype(BF16)
    for src_ref, dst_ref in ((k_ref, ks_ref), (v_ref, vs_ref)):
        for c in range(len(DILATIONS)):
            for g in range(n_class[c]):
                dst_ref[kv_row0(c, g):kv_row0(c, g) + blk, :] = jnp.zeros((blk, LANES), BF16)
        dst_ref[kv_row0(0, 0) + blk:kv_row0(0, 0) + blk + seq, :] = src_ref[...].astype(BF16)
        for a in range(d1):
            x = src_ref[pl.ds(a, seg, stride=d1), :]
            tmp_ref[a * seg:(a + 1) * seg, :] = x
            dst_ref[kv_row0(1, a) + blk:kv_row0(1, a) + blk + seg, :] = x.astype(BF16)
        for g in range(d2):
            dst_ref[kv_row0(2, g) + blk:kv_row0(2, g) + blk + sub, :] = tmp_ref[
                pl.ds((g // d1) * seg + g % d1, sub, stride=d1), :].astype(BF16)

    def one_block(c, g, n, out_rows, bias_ref):
        q = qs_ref[c, pl.ds(aligned(g * class_len[c] + n * blk), blk), :]
        kv_rows = pl.ds(aligned(kv_row0(c, g) + n * blk), 2 * blk)
        kk = ks_ref[kv_rows, :]
        vv = vs_ref[kv_rows, :]
        outs = []
        lses = []
        for h in range(LANES // HEAD_DIM):
            hm = head0 if h == 0 else jnp.logical_not(head0)
            qh = jnp.where(hm, q, jnp.zeros_like(q))
            s = lax.dot_general(qh, kk, (((1,), (1,)), ((), ())), preferred_element_type=F32)
            s = s + bias_ref[...]
            m = jnp.max(s, axis=-1, keepdims=True)
            p = jnp.exp(s - m)
            den = jnp.sum(p, axis=-1, keepdims=True)
            outs.append(jnp.dot(p.astype(BF16), vv, preferred_element_type=F32) / den)
            lses.append(m + jnp.log(den))
        ob_ref[c, out_rows, :] = jnp.where(head0, outs[0], outs[1])
        lb_ref[c, out_rows, :] = jnp.where(head0, lses[0], lses[1])

    def run_blocks(n_blocks, fn):
        group = max(g for g in range(1, ATT_GROUP + 1) if n_blocks % g == 0)
        if n_blocks == group:
            for g in range(group):
                fn(g)
            return
        def body(it, carry):
            for g in range(group):
                fn(it * group + g)
            return carry
        lax.fori_loop(0, n_blocks // group, body, 0)

    def aligned(x):
        return x if isinstance(x, int) else pl.multiple_of(x, blk)

    one_block(0, 0, 0, pl.ds(0, blk), first_ref)
    run_blocks(seq // blk - 1,
               lambda i: one_block(0, 0, i + 1, pl.ds(aligned((i + 1) * blk), blk), band_ref))

    nb1 = seg // blk
    run_blocks(d1, lambda a: one_block(1, a, 0, pl.ds(a, blk, stride=d1), first_ref))
    def later1(i):
        a = i // (nb1 - 1)
        n = i - a * (nb1 - 1) + 1
        one_block(1, a, n, pl.ds(a + n * (d1 * blk), blk, stride=d1), band_ref)
    run_blocks(d1 * (nb1 - 1), later1)

    assert sub == blk
    def only2(g):
        a = g // d1
        one_block(2, g, 0, pl.ds(a + d1 * (g - a * d1), blk, stride=d2), first_ref)
    run_blocks(d2, only2)

    rc = 256
    def merge(i, carry):
        sl = pl.ds(pl.multiple_of(i * rc, rc), rc)
        l0 = lb_ref[0, sl, :]
        l1 = lb_ref[1, sl, :]
        l2 = lb_ref[2, sl, :]
        mx = jnp.maximum(jnp.maximum(l0, l1), l2)
        e0 = jnp.exp(l0 - mx)
        e1 = jnp.exp(l1 - mx)
        e2 = jnp.exp(l2 - mx)
        tot = e0 + e1 + e2
        o_ref[sl, :] = ((e0 / tot) * ob_ref[0, sl, :] + (e1 / tot) * ob_ref[1, sl, :]
                        + (e2 / tot) * ob_ref[2, sl, :])
        return carry
    lax.fori_loop(0, seq // rc, merge, 0)


def _attention(proj, bsz, seq):
    n_tok = proj.shape[0]
    pairs = ATT_WIDTH // LANES
    assert seq % (ATT_BLOCK * max(DILATIONS)) == 0
    kv_rows = sum(seq + d * ATT_BLOCK for d in DILATIONS)
    blk = (seq, LANES)
    return pl.pallas_call(
        functools.partial(_attn_kernel, seq=seq),
        out_shape=jax.ShapeDtypeStruct((n_tok, ATT_WIDTH), F32),
        grid=(bsz, pairs),
        in_specs=[pl.BlockSpec(blk, lambda b, h: (b, h)),
                  pl.BlockSpec(blk, lambda b, h: (b, pairs + h)),
                  pl.BlockSpec(blk, lambda b, h: (b, 2 * pairs + h))],
        out_specs=pl.BlockSpec(blk, lambda b, h: (b, h)),
        scratch_shapes=[pltpu.VMEM((len(DILATIONS), seq, LANES), BF16),
                        pltpu.VMEM((kv_rows, LANES), BF16),
                        pltpu.VMEM((kv_rows, LANES), BF16),
                        pltpu.VMEM((seq, LANES), F32),
                        pltpu.VMEM((len(DILATIONS), seq, LANES), F32),
                        pltpu.VMEM((len(DILATIONS), seq, LANES), F32),
                        pltpu.VMEM((ATT_BLOCK, 2 * ATT_BLOCK), F32),
                        pltpu.VMEM((ATT_BLOCK, 2 * ATT_BLOCK), F32)],
        compiler_params=_cparams(("parallel", "parallel"), 40),
        name="attn",
    )(proj, proj, proj)


def _s5_kernel(u_ref, bm_ref, lam_ref, cm_ref, dk_ref, o_ref, us_ref, st_ref, ys_ref, carry_ref, *, tc):
    bsz = u_ref.shape[0]
    half = st_ref.shape[1] // 2
    rows = tc * bsz
    mm_rows = 512

    @pl.when(pl.program_id(1) == 0)
    def _():
        carry_ref[...] = jnp.zeros_like(carry_ref)

    for b in range(bsz):
        us_ref[pl.ds(b, tc, stride=bsz), :] = u_ref[b]

    bm = bm_ref[0]
    for r0 in range(0, rows, mm_rows):
        st_ref[r0:r0 + mm_rows, :] = jnp.dot(us_ref[r0:r0 + mm_rows, :].astype(BF16), bm,
                                             preferred_element_type=F32)

    lam = lam_ref[0]
    lam_re = lam[:, :half]
    lam_im = lam[:, half:]

    def step(t, carry):
        xr, xi = carry
        sl = pl.ds(pl.multiple_of(t * bsz, bsz), bsz)
        nr = lam_re * xr - lam_im * xi + st_ref[sl, :half]
        ni = lam_re * xi + lam_im * xr + st_ref[sl, half:]
        st_ref[sl, :half] = nr
        st_ref[sl, half:] = ni
        return nr, ni

    xr, xi = lax.fori_loop(0, tc, step, (carry_ref[:, :half], carry_ref[:, half:]), unroll=4)
    carry_ref[:, :half] = xr
    carry_ref[:, half:] = xi

    cm = cm_ref[0]
    for r0 in range(0, rows, mm_rows):
        ys_ref[r0:r0 + mm_rows, :] = jnp.dot(st_ref[r0:r0 + mm_rows, :].astype(BF16), cm,
                                             preferred_element_type=F32)
    dk = dk_ref[...]
    for b in range(bsz):
        o_ref[b] = ys_ref[pl.ds(b, tc, stride=bsz), :] + dk * u_ref[b]


def _s5_params(lam_re, lam_im, log_step, b_re, b_im, c_re, c_im, bsz):
    groups = lam_re.shape[0]
    gpc = LANES // SSM_CH
    n_chunks = groups // gpc
    lam = lax.complex(lam_re.astype(F32), lam_im.astype(F32))
    step = jnp.exp(log_step.astype(F32))[:, None]
    lam_bar = jnp.exp(lam * step)
    bmat = lax.complex(b_re.astype(F32), b_im.astype(F32))
    b_bar = ((lam_bar - 1.0) / lam)[..., None] * bmat
    eye = jnp.eye(gpc, dtype=F32)

    def block_diag_in(t):
        t = t.reshape(n_chunks, gpc, SSM_STATE, SSM_CH)
        return jnp.einsum('ngpc,gh->ngchp', t, eye).reshape(n_chunks, gpc * SSM_CH, gpc * SSM_STATE)

    def block_diag_out(t):
        t = t.reshape(n_chunks, gpc, SSM_CH, SSM_STATE)
        return jnp.einsum('ngcp,gh->ngphc', t, eye).reshape(n_chunks, gpc * SSM_STATE, gpc * SSM_CH)

    bm = jnp.concatenate([block_diag_in(b_bar.real), block_diag_in(b_bar.imag)], axis=2).astype(BF16)
    cm = jnp.concatenate([block_diag_out(c_re.astype(F32)), block_diag_out(-c_im.astype(F32))],
                         axis=1).astype(BF16)
    lam_row = jnp.concatenate([lam_bar.real.reshape(n_chunks, gpc * SSM_STATE),
                               lam_bar.imag.reshape(n_chunks, gpc * SSM_STATE)], axis=1)
    lam_t = jnp.broadcast_to(lam_row[:, None, :], (n_chunks, bsz, 2 * gpc * SSM_STATE))
    return bm, lam_t, cm, n_chunks


def _s5(proj3, u_col0, lam_re, lam_im, log_step, b_re, b_im, c_re, c_im, d_skip):
    bsz, seq, _ = proj3.shape
    assert bsz == SUBLANES
    bm, lam_t, cm, n_chunks = _s5_params(lam_re, lam_im, log_step, b_re, b_im, c_re, c_im, bsz)
    width = n_chunks * LANES
    tc = 256
    st_cols = bm.shape[2]
    ublk0 = u_col0 // LANES
    return pl.pallas_call(
        functools.partial(_s5_kernel, tc=tc),
        out_shape=jax.ShapeDtypeStruct((bsz, seq, width), F32),
        grid=(n_chunks, seq // tc),
        in_specs=[pl.BlockSpec((bsz, tc, LANES), lambda c, t: (0, t, ublk0 + c)),
                  pl.BlockSpec((1, LANES, st_cols), lambda c, t: (c, 0, 0)),
                  pl.BlockSpec((1, bsz, st_cols), lambda c, t: (c, 0, 0)),
                  pl.BlockSpec((1, st_cols, LANES), lambda c, t: (c, 0, 0)),
                  pl.BlockSpec((1, LANES), lambda c, t: (0, c))],
        out_specs=pl.BlockSpec((bsz, tc, LANES), lambda c, t: (0, t, c)),
        scratch_shapes=[pltpu.VMEM((tc * bsz, LANES), F32),
                        pltpu.VMEM((tc * bsz, st_cols), F32),
                        pltpu.VMEM((tc * bsz, LANES), F32),
                        pltpu.VMEM((bsz, st_cols), F32)],
        compiler_params=_cparams(("arbitrary", "arbitrary"), 40),
        name="s5",
    )(proj3, bm, lam_t, cm, d_skip.reshape(1, width).astype(F32))


def _layer_norm(v, g, b):
    mu = jnp.mean(v, axis=-1, keepdims=True)
    var = jnp.mean(jnp.square(v - mu), axis=-1, keepdims=True)
    return (v - mu) * lax.rsqrt(var + LN_EPS) * g + b


def _rms_norm(v, g):
    return v * lax.rsqrt(jnp.mean(jnp.square(v), axis=-1, keepdims=True) + RMS_EPS) * g


def _mixout_kernel(att_ref, ssm_ref, x_ref, wglu_ref, bglu_ref, ag_ref, sg_ref, wout_ref, g_ref, b_ref,
                   o_ref, *, alpha):
    y = jax.nn.gelu(ssm_ref[...])
    z = jnp.dot(y.astype(BF16), wglu_ref[...], preferred_element_type=F32) + bglu_ref[...]
    o_ssm = y * jax.nn.sigmoid(z)
    a = _rms_norm(att_ref[...], ag_ref[...]).astype(BF16)
    s = _rms_norm(o_ssm, sg_ref[...]).astype(BF16)
    wa = att_ref.shape[1]
    mix = (jnp.dot(a, wout_ref[:wa, :], preferred_element_type=F32)
           + jnp.dot(s, wout_ref[wa:, :], preferred_element_type=F32))
    o_ref[...] = _layer_norm(alpha * x_ref[...] + mix, g_ref[...], b_ref[...])


def _mixout(o_att, y_ssm, x2d, w_glu, b_glu, att_g, ssm_g, w_out, ln_g, ln_b, alpha):
    n_tok, d = x2d.shape
    wa = o_att.shape[1]
    ws = y_ssm.shape[1]
    tm = 256
    row = lambda w: pl.BlockSpec((tm, w), lambda i: (i, 0))
    full = lambda a: pl.BlockSpec(a.shape, lambda i: (0,) * a.ndim)
    args = (o_att, y_ssm, x2d, w_glu.astype(BF16), b_glu.reshape(1, ws), att_g.reshape(1, wa),
            ssm_g.reshape(1, ws), w_out.astype(BF16), ln_g.reshape(1, d), ln_b.reshape(1, d))
    return pl.pallas_call(
        functools.partial(_mixout_kernel, alpha=alpha),
        out_shape=jax.ShapeDtypeStruct((n_tok, d), F32),
        grid=(n_tok // tm,),
        in_specs=[row(wa), row(ws), row(d)] + [full(a) for a in args[3:]],
        out_specs=row(d),
        compiler_params=_cparams(("parallel",), 32),
        name="mixout",
    )(*args)


def _split_bf16(v):
    hi = v.astype(BF16)
    lo = (v - hi.astype(F32)).astype(BF16)
    return hi, lo


def _router_kernel(h_ref, wt_ref, bias_ref, e_ref, g_ref, r_ref, cnt_ref, run_ref):
    tm = h_ref.shape[0]
    n_exp = wt_ref.shape[0]
    gsz = n_exp // N_EXPERT_GROUPS

    @pl.when(pl.program_id(0) == 0)
    def _():
        run_ref[...] = jnp.zeros_like(run_ref)

    w_hi, w_lo = _split_bf16(wt_ref[...])
    h_hi, h_lo = _split_bf16(h_ref[...])
    nt = (((1,), (1,)), ((), ()))
    logits = (lax.dot_general(w_hi, h_hi, nt, preferred_element_type=F32)
              + lax.dot_general(w_hi, h_lo, nt, preferred_element_type=F32)
              + lax.dot_general(w_lo, h_hi, nt, preferred_element_type=F32))
    scores = jax.nn.sigmoid(logits)
    choice = scores + bias_ref[:, 0:1]

    gio = lax.broadcasted_iota(jnp.int32, (gsz, tm), 0).astype(F32)
    gscore = []
    for g in range(N_EXPERT_GROUPS):
        cg = choice[g * gsz:(g + 1) * gsz, :]
        m1 = jnp.max(cg, axis=0, keepdims=True)
        i1 = jnp.min(jnp.where(cg == m1, gio, float(gsz)), axis=0, keepdims=True)
        m2 = jnp.max(jnp.where(gio == i1, NEG_INF, cg), axis=0, keepdims=True)
        gscore.append(m1 + m2)
    masked = []
    for g in range(N_EXPERT_GROUPS):
        beat = jnp.zeros((1, tm), F32)
        for o in range(N_EXPERT_GROUPS):
            if o == g:
                continue
            wins = (gscore[o] >= gscore[g]) if o < g else (gscore[o] > gscore[g])
            beat = beat + jnp.where(wins, 1.0, 0.0)
        keep = beat < float(TOPK_GROUPS)
        masked.append(jnp.where(keep, choice[g * gsz:(g + 1) * gsz, :], NEG_INF))
    cur = jnp.concatenate(masked, axis=0)

    eio = lax.broadcasted_iota(jnp.int32, (n_exp, tm), 0).astype(F32)
    idxs = []
    gates = []
    onehot = jnp.zeros((n_exp, tm), F32)
    for _ in range(TOP_K):
        m = jnp.max(cur, axis=0, keepdims=True)
        idx = jnp.min(jnp.where(cur == m, eio, float(n_exp)), axis=0, keepdims=True)
        hit = eio == idx
        idxs.append(idx)
        gates.append(jnp.sum(jnp.where(hit, scores, 0.0), axis=0, keepdims=True))
        cur = jnp.where(hit, NEG_INF, cur)
        onehot = onehot + jnp.where(hit, 1.0, 0.0)
    gate = jnp.concatenate(gates, axis=0)
    gate = ROUTED_SCALE * gate / (jnp.sum(gate, axis=0, keepdims=True) + 1e-20)

    si = lax.broadcasted_iota(jnp.int32, (tm, tm), 0)
    ti = lax.broadcasted_iota(jnp.int32, (tm, tm), 1)
    upper = jnp.where(si < ti, 1.0, 0.0).astype(BF16)
    before = jnp.dot(onehot.astype(BF16), upper, preferred_element_type=F32) + run_ref[:, 0:1]
    ranks = [jnp.sum(jnp.where(eio == idx, before, 0.0), axis=0, keepdims=True) for idx in idxs]

    e_ref[...] = jnp.concatenate(idxs, axis=0).astype(jnp.int32)
    g_ref[...] = gate
    r_ref[...] = jnp.concatenate(ranks, axis=0).astype(jnp.int32)
    run_ref[...] = run_ref[...] + jnp.sum(onehot, axis=1, keepdims=True)
    cnt_ref[...] = run_ref[...]


def _router(h, router_w, router_bias):
    n_tok, d = h.shape
    n_exp = router_w.shape[1]
    tm = 256
    wt = router_w.astype(F32).T
    bias = jnp.broadcast_to(router_bias.astype(F32)[:, None], (n_exp, LANES))
    tok = pl.BlockSpec((TOP_K, tm), lambda i: (0, i))
    return pl.pallas_call(
        _router_kernel,
        out_shape=(jax.ShapeDtypeStruct((TOP_K, n_tok), jnp.int32),
                   jax.ShapeDtypeStruct((TOP_K, n_tok), F32),
                   jax.ShapeDtypeStruct((TOP_K, n_tok), jnp.int32),
                   jax.ShapeDtypeStruct((n_exp, LANES), F32)),
        grid=(n_tok // tm,),
        in_specs=[pl.BlockSpec((tm, d), lambda i: (i, 0)),
                  pl.BlockSpec((n_exp, d), lambda i: (0, 0)),
                  pl.BlockSpec((n_exp, LANES), lambda i: (0, 0))],
        out_specs=(tok, tok, tok, pl.BlockSpec((n_exp, LANES), lambda i: (0, 0))),
        scratch_shapes=[pltpu.VMEM((n_exp, LANES), F32)],
        compiler_params=_cparams(("arbitrary",), 32),
        name="router",
    )(h, wt, bias)


def _dest_kernel(e_ref, r_ref, st_ref, d_ref):
    n_exp = st_ref.shape[0]
    tm = e_ref.shape[1]
    eio = lax.broadcasted_iota(jnp.int32, (n_exp, tm), 0)
    start = st_ref[:, 0:1]
    rows = [jnp.sum(jnp.where(eio == e_ref[k:k + 1, :], start, 0.0), axis=0, keepdims=True)
            for k in range(TOP_K)]
    d_ref[...] = jnp.concatenate(rows, axis=0).astype(jnp.int32) + r_ref[...]


def _dest(top_e, rank, starts):
    n_tok = top_e.shape[1]
    n_exp = starts.shape[0]
    tm = 512
    st = jnp.broadcast_to(starts.astype(F32)[:, None], (n_exp, LANES))
    tok = pl.BlockSpec((TOP_K, tm), lambda i: (0, i))
    return pl.pallas_call(
        _dest_kernel,
        out_shape=jax.ShapeDtypeStruct((TOP_K, n_tok), jnp.int32),
        grid=(n_tok // tm,),
        in_specs=[tok, tok, pl.BlockSpec((n_exp, LANES), lambda i: (0, 0))],
        out_specs=tok,
        compiler_params=_cparams(("parallel",), 32),
        name="dest",
    )(top_e, rank, st)


def _pack_bf16_pairs(val):
    half = val.shape[1] // 2
    lo = pltpu.bitcast(val[:, :half].astype(BF16).astype(F32), U32)
    hi = pltpu.bitcast(val[:, half:].astype(BF16).astype(F32), U32)
    return (lo >> 16) | (hi & jnp.uint32(0xFFFF0000))


def _unpack_bf16_pairs(words):
    lo = pltpu.bitcast(words << 16, F32)
    hi = pltpu.bitcast(words & jnp.uint32(0xFFFF0000), F32)
    return jnp.concatenate([lo, hi], axis=1)


def _to_row_tiles(dst_ref, slot, val):
    rows = val.shape[0]
    words = _pack_bf16_pairs(val)
    for j in range(ROW_TILE):
        dst_ref[slot, pl.ds(j, rows, stride=ROW_TILE), :] = words[:, j * LANES:(j + 1) * LANES]


def _row_tile_words(src_ref, idx, rows):
    return jnp.concatenate([src_ref[(*idx, pl.ds(j, rows, stride=ROW_TILE), slice(None))]
                            for j in range(ROW_TILE)], axis=1)


def _row_tile(r):
    return pl.ds(pl.multiple_of(r * ROW_TILE, ROW_TILE), ROW_TILE)


def _dispatch_kernel(dest_ref, h_ref, xs_ref, ht_ref, sem):
    tm = h_ref.shape[0]
    i = pl.program_id(0)
    cur = i % 2
    _to_row_tiles(ht_ref, cur, h_ref[...])

    def issue(t, carry):
        for k in range(TOP_K):
            pltpu.make_async_copy(ht_ref.at[cur, _row_tile(t)], xs_ref.at[_row_tile(dest_ref[t * TOP_K + k])],
                                  sem.at[cur]).start(priority=k % 2)
        return carry
    lax.fori_loop(0, tm, issue, 0, unroll=2)

    def drain(slot):
        for k in range(TOP_K):
            pltpu.make_async_copy(ht_ref.at[slot], xs_ref.at[pl.ds(0, tm * ROW_TILE)], sem.at[slot]).wait()

    @pl.when(i > 0)
    def _():
        drain(1 - cur)

    @pl.when(i == pl.num_programs(0) - 1)
    def _():
        drain(cur)


def _dispatch(h, dest, n_exp):
    n_tok, d = h.shape
    assert d == 2 * ROW_TILE * LANES
    tm = MOE_TOKEN_TILE
    n_rows = (pl.cdiv(n_tok * TOP_K, EXPERT_ROWS) + n_exp) * EXPERT_ROWS
    return pl.pallas_call(
        _dispatch_kernel,
        out_shape=jax.ShapeDtypeStruct((n_rows * ROW_TILE, LANES), U32),
        grid=(n_tok // tm,),
        in_specs=[pl.BlockSpec((TOP_K * tm,), lambda i: (i,), memory_space=pltpu.SMEM),
                  pl.BlockSpec((tm, d), lambda i: (i, 0))],
        out_specs=pl.BlockSpec(memory_space=pl.ANY),
        scratch_shapes=[pltpu.VMEM((2, tm * ROW_TILE, LANES), U32), pltpu.SemaphoreType.DMA((2,))],
        compiler_params=_cparams(("arbitrary",), 32),
        name="dispatch",
    )(dest, h)


def _experts_kernel(bstart_ref, bend_ref, cnt_ref, nblk_ref, xs_ref, wgu_hbm, wdn_hbm, ys_ref,
                    xbuf, ybuf, act_ref, wgu_f32, wdn_f32, wgu_bf, wdn_bf, xsem, ysem, wsem):
    e = pl.program_id(0)
    n_blk = nblk_ref[0]
    trows = xbuf.shape[1]
    rows = trows // ROW_TILE
    ff = wdn_bf.shape[0]
    b0 = bstart_ref[e]
    b1 = bend_ref[e]

    def block_rows(b):
        return pl.ds(pl.multiple_of(b * trows, trows), trows)

    def x_copy(b):
        slot = b % EXPERT_RING
        return pltpu.make_async_copy(xs_ref.at[block_rows(b)], xbuf.at[slot], xsem.at[slot])

    def y_copy(b):
        slot = b % EXPERT_OUT_RING
        return pltpu.make_async_copy(ybuf.at[slot], ys_ref.at[block_rows(b)], ysem.at[slot])

    @pl.when(e == 0)
    def _():
        for i in range(EXPERT_RING):
            @pl.when(i < n_blk)
            def _():
                x_copy(i).start(priority=1)

    n_exp = pl.num_programs(0)
    wslot = e % EXPERT_WEIGHT_RING

    def w_copies(x, slot):
        return (pltpu.make_async_copy(wgu_hbm.at[x], wgu_f32.at[slot], wsem.at[0, slot]),
                pltpu.make_async_copy(wdn_hbm.at[x], wdn_f32.at[slot], wsem.at[1, slot]))

    @pl.when(e == 0)
    def _():
        for i in range(EXPERT_WEIGHT_RING):
            @pl.when(i < n_exp)
            def _():
                for c in w_copies(i, i):
                    c.start()

    for c in w_copies(e, wslot):
        c.wait()
    wgu_bf[...] = wgu_f32[wslot].astype(BF16)
    wdn_bf[...] = wdn_f32[wslot].astype(BF16)

    @pl.when(e + EXPERT_WEIGHT_RING < n_exp)
    def _():
        for c in w_copies(e + EXPERT_WEIGHT_RING, wslot):
            c.start()

    def up(blocks):
        for b in blocks:
            x_copy(b).wait()
        for b in blocks:
            words = _row_tile_words(xbuf, (b % EXPERT_RING,), rows)
            row = lax.broadcasted_iota(jnp.int32, (rows, 1), 0)
            words = jnp.where(row < cnt_ref[e] - (b - b0) * rows, words, jnp.uint32(0))
            xb = _unpack_bf16_pairs(words).astype(BF16)
            gu = jnp.dot(xb, wgu_bf[...], preferred_element_type=F32)
            act_ref[b - b0] = (jax.nn.silu(gu[:, :ff]) * gu[:, ff:]).astype(BF16)
        for b in blocks:
            @pl.when(b + EXPERT_RING < n_blk)
            def _():
                x_copy(b + EXPERT_RING).start(priority=1)

    n_mine = b1 - b0

    def run_groups(fn):
        def body(p, carry):
            fn(tuple(b0 + EXPERT_GROUP * p + j for j in range(EXPERT_GROUP)))
            return carry
        lax.fori_loop(0, n_mine // EXPERT_GROUP, body, 0)
        size = EXPERT_GROUP // 2
        while size >= 1:
            @pl.when(n_mine & size != 0)
            def _(size=size):
                start = b0 + (n_mine // (2 * size)) * (2 * size)
                fn(tuple(start + j for j in range(size)))
            size //= 2

    run_groups(up)

    def down(blocks):
        for b in blocks:
            @pl.when(b >= EXPERT_OUT_RING)
            def _():
                y_copy(b - EXPERT_OUT_RING).wait()
        for b in blocks:
            _to_row_tiles(ybuf, b % EXPERT_OUT_RING,
                          jnp.dot(act_ref[b - b0], wdn_bf[...], preferred_element_type=F32))
        for b in blocks:
            y_copy(b).start(priority=1)
        last = blocks[-1]

        @pl.when(last == n_blk - 1)
        def _():
            for i in range(EXPERT_OUT_RING):
                @pl.when(last >= i)
                def _():
                    y_copy(last - i).wait()

    run_groups(down)


def _expert_blocks(counts):
    blocks = (counts + EXPERT_ROWS - 1) // EXPERT_ROWS
    bend = jnp.cumsum(blocks)
    bstart = bend - blocks
    i32 = lambda a: a.astype(jnp.int32)
    return i32(bstart), i32(bend), i32(bend[-1]).reshape(1), i32(bstart * EXPERT_ROWS)


def _experts(xs, bstart, bend, counts, n_blk, w_gu, w_down, n_tok):
    n_exp, d, ff2 = w_gu.shape
    ff = w_down.shape[1]
    n_rows = xs.shape[0] // ROW_TILE
    assert n_rows % EXPERT_ROWS == 0 and d == 2 * ROW_TILE * LANES
    max_blocks = pl.cdiv(n_tok, EXPERT_ROWS)
    grid_spec = pltpu.PrefetchScalarGridSpec(
        num_scalar_prefetch=4,
        grid=(n_exp,),
        in_specs=[pl.BlockSpec(memory_space=pl.ANY), pl.BlockSpec(memory_space=pl.ANY),
                  pl.BlockSpec(memory_space=pl.ANY)],
        out_specs=pl.BlockSpec(memory_space=pl.ANY),
        scratch_shapes=[pltpu.VMEM((EXPERT_RING, EXPERT_ROWS * ROW_TILE, LANES), U32),
                        pltpu.VMEM((EXPERT_OUT_RING, EXPERT_ROWS * ROW_TILE, LANES), U32),
                        pltpu.VMEM((max_blocks, EXPERT_ROWS, ff), BF16),
                        pltpu.VMEM((EXPERT_WEIGHT_RING, d, ff2), F32),
                        pltpu.VMEM((EXPERT_WEIGHT_RING, ff, d), F32),
                        pltpu.VMEM((d, ff2), BF16), pltpu.VMEM((ff, d), BF16),
                        pltpu.SemaphoreType.DMA((EXPERT_RING,)),
                        pltpu.SemaphoreType.DMA((EXPERT_OUT_RING,)),
                        pltpu.SemaphoreType.DMA((2, EXPERT_WEIGHT_RING))],
    )
    return pl.pallas_call(
        _experts_kernel,
        out_shape=jax.ShapeDtypeStruct(xs.shape, U32),
        grid_spec=grid_spec,
        compiler_params=_cparams(("arbitrary",), 48),
        name="experts",
    )(bstart, bend, counts, n_blk, xs, w_gu, w_down)


def _combine_kernel(dest_ref, dnext_ref, gate_ref, h_ref, ys_ref, wgu_ref, wdn_ref, g_ref, b_ref, o_ref,
                    buf_ref, routed_ref, sem, *, alpha):
    tm = h_ref.shape[0]
    i = pl.program_id(0)
    cur = i % 2
    chunk = SUBLANES

    def issue(d_ref, slot, t):
        for k in range(TOP_K):
            pltpu.make_async_copy(ys_ref.at[_row_tile(d_ref[t * TOP_K + k])], buf_ref.at[slot, k, _row_tile(t)],
                                  sem.at[slot]).start(priority=k % 2)

    @pl.when(i == 0)
    def _():
        def first(t, carry):
            issue(dest_ref, 0, t)
            return carry
        lax.fori_loop(0, tm, first, 0, unroll=2)

    for k in range(TOP_K):
        pltpu.make_async_copy(ys_ref.at[pl.ds(0, tm * ROW_TILE)], buf_ref.at[cur, k], sem.at[cur]).wait()

    def weighted_sum(c):
        tok = pl.ds(pl.multiple_of(c * chunk, chunk), chunk)
        gate = gate_ref[tok, :]
        total = None
        for k in range(TOP_K):
            words = jnp.concatenate(
                [buf_ref[cur, k, pl.ds(pl.multiple_of(c * (chunk * ROW_TILE), chunk * ROW_TILE) + j, chunk,
                                       stride=ROW_TILE), :] for j in range(ROW_TILE)], axis=1)
            term = gate[:, k:k + 1] * _unpack_bf16_pairs(words)
            total = term if total is None else total + term
        routed_ref[tok, :] = total

    @pl.when(i + 1 < pl.num_programs(0))
    def _():
        def body(c, carry):
            for t in range(chunk):
                issue(dnext_ref, 1 - cur, c * chunk + t)
            weighted_sum(c)
            return carry
        lax.fori_loop(0, tm // chunk, body, 0)

    @pl.when(i + 1 == pl.num_programs(0))
    def _():
        def body(c, carry):
            weighted_sum(c)
            return carry
        lax.fori_loop(0, tm // chunk, body, 0)

    h = h_ref[...]
    ff = wdn_ref.shape[0]
    gu = jnp.dot(h.astype(BF16), wgu_ref[...], preferred_element_type=F32)
    act = (jax.nn.silu(gu[:, :ff]) * gu[:, ff:]).astype(BF16)
    acc = alpha * h + jnp.dot(act, wdn_ref[...], preferred_element_type=F32) + routed_ref[...]
    o_ref[...] = _layer_norm(acc, g_ref[...], b_ref[...])


def _combine(h, ys, dest, gate_t, shared_w_gu, shared_w_down, ln_g, ln_b, alpha):
    n_tok, d = h.shape
    tm = MOE_TOKEN_TILE
    n_tiles = n_tok // tm
    full = lambda a: pl.BlockSpec(a.shape, lambda i: (0,) * a.ndim)
    wgu = shared_w_gu.astype(BF16)
    wdn = shared_w_down.astype(BF16)
    g2 = ln_g.reshape(1, d)
    b2 = ln_b.reshape(1, d)
    return pl.pallas_call(
        functools.partial(_combine_kernel, alpha=alpha),
        out_shape=jax.ShapeDtypeStruct((n_tok, d), F32),
        grid=(n_tiles,),
        in_specs=[pl.BlockSpec((TOP_K * tm,), lambda i: (i,), memory_space=pltpu.SMEM),
                  pl.BlockSpec((TOP_K * tm,), lambda i: (jnp.minimum(i + 1, n_tiles - 1),),
                               memory_space=pltpu.SMEM),
                  pl.BlockSpec((tm, TOP_K), lambda i: (i, 0)),
                  pl.BlockSpec((tm, d), lambda i: (i, 0)),
                  pl.BlockSpec(memory_space=pl.ANY),
                  full(wgu), full(wdn), full(g2), full(b2)],
        out_specs=pl.BlockSpec((tm, d), lambda i: (i, 0)),
        scratch_shapes=[pltpu.VMEM((2, TOP_K, tm * ROW_TILE, LANES), U32), pltpu.VMEM((tm, d), F32),
                        pltpu.SemaphoreType.DMA((2,))],
        compiler_params=_cparams(("arbitrary",), 48),
        name="combine",
    )(dest, dest, gate_t, h, ys, wgu, wdn, g2, b2)


def _moe(h, router_w, router_bias, w_gu, w_down, shared_w_gu, shared_w_down, ln_g, ln_b, alpha):
    top_e, gate, rank, cnt = _router(h, router_w, router_bias)
    counts = cnt[:, 0].astype(jnp.int32)
    bstart, bend, n_blk, pad_start = _expert_blocks(counts)
    dest = _dest(top_e, rank, pad_start)
    dest_tiles = dest.T.reshape(-1)
    xs = _dispatch(h, dest_tiles, counts.shape[0])
    ys = _experts(xs, bstart, bend, counts, n_blk, w_gu, w_down, h.shape[0])
    return _combine(h, ys, dest_tiles, gate.T, shared_w_gu, shared_w_down, ln_g, ln_b, alpha)


def kernel(x, w_in, att_norm_g, lam_re, lam_im, log_step, b_re, b_im, c_re, c_im, d_skip, w_glu, b_glu,
           ssm_norm_g, w_out, ln1_g, ln1_b, router_w, router_bias, w_gu, w_down, shared_w_gu,
           shared_w_down, ln2_g, ln2_b):
    bsz, seq, d = x.shape
    depth = w_in.shape[0]
    alpha = (2 * depth) ** 0.25
    h = x.reshape(bsz * seq, d)
    for i in range(depth):
        proj = _inproj(h, w_in[i].astype(BF16), seq)
        o_att = _attention(proj, bsz, seq)
        y_ssm = _s5(proj.reshape(bsz, seq, -1), 3 * ATT_WIDTH, lam_re[i], lam_im[i], log_step[i],
                    b_re[i], b_im[i], c_re[i], c_im[i], d_skip[i])
        h = _mixout(o_att, y_ssm.reshape(bsz * seq, -1), h, w_glu[i], b_glu[i], att_norm_g[i],
                    ssm_norm_g[i], w_out[i], ln1_g[i], ln1_b[i], alpha)
        h = _moe(h, router_w[i], router_bias[i], w_gu[i], w_down[i], shared_w_gu[i], shared_w_down[i],
                 ln2_g[i], ln2_b[i], alpha)
    return h.reshape(bsz, seq, d)
```

```python
import functools
import math

import jax
import jax.numpy as jnp
from jax import lax
from jax.experimental import pallas as pl
from jax.experimental.pallas import tpu as pltpu

F32 = jnp.float32
BF16 = jnp.bfloat16
U32 = jnp.uint32

ATT_HEADS = 8
HEAD_DIM = 64
ATT_WIDTH = ATT_HEADS * HEAD_DIM
SSM_CH = 16
SSM_STATE = 64
ROPE_THETA = 500000.0
ROT_DIM = HEAD_DIM // 4
DILATIONS = (1, 4, 16)
ATT_BLOCK = 128
ATT_GROUP = 16
N_EXPERTS = 256
TOP_K = 8
N_EXPERT_GROUPS = 8
TOPK_GROUPS = 4
ROUTED_SCALE = 2.5
LN_EPS = 1e-5
RMS_EPS = 1e-6

LANES = 128
SUBLANES = 8
EXPERT_ROWS = 144
MOE_TOKEN_TILE = 512
EXPERT_RING = 16
EXPERT_OUT_RING = 16
EXPERT_GROUP = 4
EXPERT_WEIGHT_RING = 4
ROW_TILE = 4
NEG_INF = float("-inf")


def _cparams(sem, vmem_mb):
    return pltpu.CompilerParams(dimension_semantics=sem, vmem_limit_bytes=vmem_mb * 1024 * 1024)


def _inproj_kernel(x_ref, w_ref, cos_ref, sa_ref, sb_ref, o_ref, *, n_rot_cols):
    xb = x_ref[...].astype(BF16)
    cosf = cos_ref[...]
    sa = sa_ref[...]
    sb = sb_ref[...]
    width = o_ref.shape[1]
    chunk = 512
    for c in range(width // chunk):
        r = jnp.dot(xb, w_ref[:, c * chunk:(c + 1) * chunk], preferred_element_type=F32)
        if c * chunk < n_rot_cols:
            parts = []
            for s in range(chunk // LANES):
                t = r[:, s * LANES:(s + 1) * LANES]
                parts.append(t * cosf + pltpu.roll(t, LANES - ROT_DIM // 2, 1) * sa
                             + pltpu.roll(t, ROT_DIM // 2, 1) * sb)
            r = jnp.concatenate(parts, axis=1)
        o_ref[:, c * chunk:(c + 1) * chunk] = r


def _rope_lane_tables(seq):
    half = ROT_DIM // 2
    inv_freq = jnp.power(jnp.float32(ROPE_THETA), -jnp.arange(half, dtype=F32) / half)
    ang = jnp.arange(seq, dtype=F32)[:, None] * inv_freq[None, :]
    cos, sin = jnp.cos(ang), jnp.sin(ang)
    rest = HEAD_DIM - ROT_DIM
    cos_h = jnp.concatenate([cos, cos, jnp.ones((seq, rest), F32)], axis=1)
    sa_h = jnp.concatenate([-sin, jnp.zeros((seq, half + rest), F32)], axis=1)
    sb_h = jnp.concatenate([jnp.zeros((seq, half), F32), sin, jnp.zeros((seq, rest), F32)], axis=1)
    rep = LANES // HEAD_DIM
    return tuple(jnp.tile(t, (1, rep)) for t in (cos_h, sa_h, sb_h))


def _inproj(x2d, w_in_bf, seq):
    n_tok, d = x2d.shape
    width = w_in_bf.shape[1]
    tm = 512
    cosf, sa, sb = _rope_lane_tables(seq)
    tab_spec = pl.BlockSpec((tm, LANES), lambda i: (i % (seq // tm), 0))
    return pl.pallas_call(
        functools.partial(_inproj_kernel, n_rot_cols=2 * ATT_WIDTH),
        out_shape=jax.ShapeDtypeStruct((n_tok, width), F32),
        grid=(n_tok // tm,),
        in_specs=[pl.BlockSpec((tm, d), lambda i: (i, 0)),
                  pl.BlockSpec((d, width), lambda i: (0, 0)),
                  tab_spec, tab_spec, tab_spec],
        out_specs=pl.BlockSpec((tm, width), lambda i: (i, 0)),
        compiler_params=_cparams(("parallel",), 48),
        name="inproj",
    )(x2d, w_in_bf, cosf, sa, sb)


def _attn_kernel(q_ref, k_ref, v_ref, o_ref, qs_ref, ks_ref, vs_ref, tmp_ref, ob_ref, lb_ref, band_ref,
                 first_ref, *, seq):
    blk = ATT_BLOCK
    lane = lax.broadcasted_iota(jnp.int32, (1, LANES), 1)
    head0 = lane < HEAD_DIM
    scale = HEAD_DIM ** -0.5
    d1, d2 = DILATIONS[1], DILATIONS[2]
    assert DILATIONS[0] == 1 and d2 == d1 * d1
    seg = seq // d1
    sub = seg // d1

    qi = lax.broadcasted_iota(jnp.int32, (blk, 2 * blk), 0)
    kj = lax.broadcasted_iota(jnp.int32, (blk, 2 * blk), 1)
    dist = qi + blk - kj
    band_ref[...] = jnp.where((dist >= 0) & (dist <= blk), 0.0, NEG_INF)
    first_ref[...] = jnp.where((dist >= 0) & (kj >= blk), 0.0, NEG_INF)

    n_class = (1, d1, d2)
    class_len = (seq, seg, sub)
    base = [0]
    for c in range(len(DILATIONS)):
        base.append(base[c] + n_class[c] * (class_len[c] + blk))

    def kv_row0(c, g):
        return base[c] + g * (class_len[c] + blk)

    qs_ref[0] = (q_ref[...] * scale).astype(BF16)
    for a in range(d1):
        x = q_ref[pl.ds(a, seg, stride=d1), :] * scale
        tmp_ref[a * seg:(a + 1) * seg, :] = x
        qs_ref[1, a * seg:(a + 1) * seg, :] = x.astype(BF16)
    for g in range(d2):
        qs_ref[2, g * sub:(g + 1) * sub, :] = tmp_ref[pl.ds((g // d1) * seg + g % d1, sub, stride=d1),
                                                      :].astype(BF16)
    for src_ref, dst_ref in ((k_ref, ks_ref), (v_ref, vs_ref)):
        for c in range(len(DILATIONS)):
            for g in range(n_class[c]):
                dst_ref[kv_row0(c, g):kv_row0(c, g) + blk, :] = jnp.zeros((blk, LANES), BF16)
        dst_ref[kv_row0(0, 0) + blk:kv_row0(0, 0) + blk + seq, :] = src_ref[...].astype(BF16)
        for a in range(d1):
            x = src_ref[pl.ds(a, seg, stride=d1), :]
            tmp_ref[a * seg:(a + 1) * seg, :] = x
            dst_ref[kv_row0(1, a) + blk:kv_row0(1, a) + blk + seg, :] = x.astype(BF16)
        for g in range(d2):
            dst_ref[kv_row0(2, g) + blk:kv_row0(2, g) + blk + sub, :] = tmp_ref[
                pl.ds((g // d1) * seg + g % d1, sub, stride=d1), :].astype(BF16)

    def one_block(c, g, n, out_rows, bias_ref):
        q = qs_ref[c, pl.ds(aligned(g * class_len[c] + n * blk), blk), :]
        kv_rows = pl.ds(aligned(kv_row0(c, g) + n * blk), 2 * blk)
        kk = ks_ref[kv_rows, :]
        vv = vs_ref[kv_rows, :]
        outs = []
        lses = []
        for h in range(LANES // HEAD_DIM):
            hm = head0 if h == 0 else jnp.logical_not(head0)
            qh = jnp.where(hm, q, jnp.zeros_like(q))
            s = lax.dot_general(qh, kk, (((1,), (1,)), ((), ())), preferred_element_type=F32)
            s = s + bias_ref[...]
            m = jnp.max(s, axis=-1, keepdims=True)
            p = jnp.exp(s - m)
            den = jnp.sum(p, axis=-1, keepdims=True)
            outs.append(jnp.dot(p.astype(BF16), vv, preferred_element_type=F32) / den)
            lses.append(m + jnp.log(den))
        ob_ref[c, out_rows, :] = jnp.where(head0, outs[0], outs[1])
        lb_ref[c, out_rows, :] = jnp.where(head0, lses[0], lses[1])

    def run_blocks(n_blocks, fn):
        group = max(g for g in range(1, ATT_GROUP + 1) if n_blocks % g == 0)
        if n_blocks == group:
            for g in range(group):
                fn(g)
            return
        def body(it, carry):
            for g in range(group):
                fn(it * group + g)
            return carry
        lax.fori_loop(0, n_blocks // group, body, 0)

    def aligned(x):
        return x if isinstance(x, int) else pl.multiple_of(x, blk)

    one_block(0, 0, 0, pl.ds(0, blk), first_ref)
    run_blocks(seq // blk - 1,
               lambda i: one_block(0, 0, i + 1, pl.ds(aligned((i + 1) * blk), blk), band_ref))

    nb1 = seg // blk
    run_blocks(d1, lambda a: one_block(1, a, 0, pl.ds(a, blk, stride=d1), first_ref))
    def later1(i):
        a = i // (nb1 - 1)
        n = i - a * (nb1 - 1) + 1
        one_block(1, a, n, pl.ds(a + n * (d1 * blk), blk, stride=d1), band_ref)
    run_blocks(d1 * (nb1 - 1), later1)

    assert sub == blk
    def only2(g):
        a = g // d1
        one_block(2, g, 0, pl.ds(a + d1 * (g - a * d1), blk, stride=d2), first_ref)
    run_blocks(d2, only2)

    rc = 256
    def merge(i, carry):
        sl = pl.ds(pl.multiple_of(i * rc, rc), rc)
        l0 = lb_ref[0, sl, :]
        l1 = lb_ref[1, sl, :]
        l2 = lb_ref[2, sl, :]
        mx = jnp.maximum(jnp.maximum(l0, l1), l2)
        e0 = jnp.exp(l0 - mx)
        e1 = jnp.exp(l1 - mx)
        e2 = jnp.exp(l2 - mx)
        tot = e0 + e1 + e2
        o_ref[sl, :] = ((e0 / tot) * ob_ref[0, sl, :] + (e1 / tot) * ob_ref[1, sl, :]
                        + (e2 / tot) * ob_ref[2, sl, :])
        return carry
    lax.fori_loop(0, seq // rc, merge, 0)


def _attention(proj, bsz, seq):
    n_tok = proj.shape[0]
    pairs = ATT_WIDTH // LANES
    assert seq % (ATT_BLOCK * max(DILATIONS)) == 0
    kv_rows = sum(seq + d * ATT_BLOCK for d in DILATIONS)
    blk = (seq, LANES)
    return pl.pallas_call(
        functools.partial(_attn_kernel, seq=seq),
        out_shape=jax.ShapeDtypeStruct((n_tok, ATT_WIDTH), F32),
        grid=(bsz, pairs),
        in_specs=[pl.BlockSpec(blk, lambda b, h: (b, h)),
                  pl.BlockSpec(blk, lambda b, h: (b, pairs + h)),
                  pl.BlockSpec(blk, lambda b, h: (b, 2 * pairs + h))],
        out_specs=pl.BlockSpec(blk, lambda b, h: (b, h)),
        scratch_shapes=[pltpu.VMEM((len(DILATIONS), seq, LANES), BF16),
                        pltpu.VMEM((kv_rows, LANES), BF16),
                        pltpu.VMEM((kv_rows, LANES), BF16),
                        pltpu.VMEM((seq, LANES), F32),
                        pltpu.VMEM((len(DILATIONS), seq, LANES), F32),
                        pltpu.VMEM((len(DILATIONS), seq, LANES), F32),
                        pltpu.VMEM((ATT_BLOCK, 2 * ATT_BLOCK), F32),
                        pltpu.VMEM((ATT_BLOCK, 2 * ATT_BLOCK), F32)],
        compiler_params=_cparams(("parallel", "parallel"), 40),
        name="attn",
    )(proj, proj, proj)


def _s5_kernel(u_ref, bm_ref, lam_ref, cm_ref, dk_ref, o_ref, us_ref, st_ref, ys_ref, carry_ref, *, tc):
    bsz = u_ref.shape[0]
    half = st_ref.shape[1] // 2
    rows = tc * bsz
    mm_rows = 512

    @pl.when(pl.program_id(1) == 0)
    def _():
        carry_ref[...] = jnp.zeros_like(carry_ref)

    for b in range(bsz):
        us_ref[pl.ds(b, tc, stride=bsz), :] = u_ref[b]

    bm = bm_ref[0]
    for r0 in range(0, rows, mm_rows):
        st_ref[r0:r0 + mm_rows, :] = jnp.dot(us_ref[r0:r0 + mm_rows, :].astype(BF16), bm,
                                             preferred_element_type=F32)

    lam = lam_ref[0]
    lam_re = lam[:, :half]
    lam_im = lam[:, half:]

    def step(t, carry):
        xr, xi = carry
        sl = pl.ds(pl.multiple_of(t * bsz, bsz), bsz)
        nr = lam_re * xr - lam_im * xi + st_ref[sl, :half]
        ni = lam_re * xi + lam_im * xr + st_ref[sl, half:]
        st_ref[sl, :half] = nr
        st_ref[sl, half:] = ni
        return nr, ni

    xr, xi = lax.fori_loop(0, tc, step, (carry_ref[:, :half], carry_ref[:, half:]), unroll=4)
    carry_ref[:, :half] = xr
    carry_ref[:, half:] = xi

    cm = cm_ref[0]
    for r0 in range(0, rows, mm_rows):
        ys_ref[r0:r0 + mm_rows, :] = jnp.dot(st_ref[r0:r0 + mm_rows, :].astype(BF16), cm,
                                             preferred_element_type=F32)
    dk = dk_ref[...]
    for b in range(bsz):
        o_ref[b] = ys_ref[pl.ds(b, tc, stride=bsz), :] + dk * u_ref[b]


def _s5_params(lam_re, lam_im, log_step, b_re, b_im, c_re, c_im, bsz):
    groups = lam_re.shape[0]
    gpc = LANES // SSM_CH
    n_chunks = groups // gpc
    lam = lax.complex(lam_re.astype(F32), lam_im.astype(F32))
    step = jnp.exp(log_step.astype(F32))[:, None]
    lam_bar = jnp.exp(lam * step)
    bmat = lax.complex(b_re.astype(F32), b_im.astype(F32))
    b_bar = ((lam_bar - 1.0) / lam)[..., None] * bmat
    eye = jnp.eye(gpc, dtype=F32)

    def block_diag_in(t):
        t = t.reshape(n_chunks, gpc, SSM_STATE, SSM_CH)
        return jnp.einsum('ngpc,gh->ngchp', t, eye).reshape(n_chunks, gpc * SSM_CH, gpc * SSM_STATE)

    def block_diag_out(t):
        t = t.reshape(n_chunks, gpc, SSM_CH, SSM_STATE)
        return jnp.einsum('ngcp,gh->ngphc', t, eye).reshape(n_chunks, gpc * SSM_STATE, gpc * SSM_CH)

    bm = jnp.concatenate([block_diag_in(b_bar.real), block_diag_in(b_bar.imag)], axis=2).astype(BF16)
    cm = jnp.concatenate([block_diag_out(c_re.astype(F32)), block_diag_out(-c_im.astype(F32))],
                         axis=1).astype(BF16)
    lam_row = jnp.concatenate([lam_bar.real.reshape(n_chunks, gpc * SSM_STATE),
                               lam_bar.imag.reshape(n_chunks, gpc * SSM_STATE)], axis=1)
    lam_t = jnp.broadcast_to(lam_row[:, None, :], (n_chunks, bsz, 2 * gpc * SSM_STATE))
    return bm, lam_t, cm, n_chunks


def _s5(proj3, u_col0, lam_re, lam_im, log_step, b_re, b_im, c_re, c_im, d_skip):
    bsz, seq, _ = proj3.shape
    assert bsz == SUBLANES
    bm, lam_t, cm, n_chunks = _s5_params(lam_re, lam_im, log_step, b_re, b_im, c_re, c_im, bsz)
    width = n_chunks * LANES
    tc = 256
    st_cols = bm.shape[2]
    ublk0 = u_col0 // LANES
    return pl.pallas_call(
        functools.partial(_s5_kernel, tc=tc),
        out_shape=jax.ShapeDtypeStruct((bsz, seq, width), F32),
        grid=(n_chunks, seq // tc),
        in_specs=[pl.BlockSpec((bsz, tc, LANES), lambda c, t: (0, t, ublk0 + c)),
                  pl.BlockSpec((1, LANES, st_cols), lambda c, t: (c, 0, 0)),
                  pl.BlockSpec((1, bsz, st_cols), lambda c, t: (c, 0, 0)),
                  pl.BlockSpec((1, st_cols, LANES), lambda c, t: (c, 0, 0)),
                  pl.BlockSpec((1, LANES), lambda c, t: (0, c))],
        out_specs=pl.BlockSpec((bsz, tc, LANES), lambda c, t: (0, t, c)),
        scratch_shapes=[pltpu.VMEM((tc * bsz, LANES), F32),
                        pltpu.VMEM((tc * bsz, st_cols), F32),
                        pltpu.VMEM((tc * bsz, LANES), F32),
                        pltpu.VMEM((bsz, st_cols), F32)],
        compiler_params=_cparams(("arbitrary", "arbitrary"), 40),
        name="s5",
    )(proj3, bm, lam_t, cm, d_skip.reshape(1, width).astype(F32))


def _layer_norm(v, g, b):
    mu = jnp.mean(v, axis=-1, keepdims=True)
    var = jnp.mean(jnp.square(v - mu), axis=-1, keepdims=True)
    return (v - mu) * lax.rsqrt(var + LN_EPS) * g + b


def _rms_norm(v, g):
    return v * lax.rsqrt(jnp.mean(jnp.square(v), axis=-1, keepdims=True) + RMS_EPS) * g


def _mixout_kernel(att_ref, ssm_ref, x_ref, wglu_ref, bglu_ref, ag_ref, sg_ref, wout_ref, g_ref, b_ref,
                   o_ref, *, alpha):
    y = jax.nn.gelu(ssm_ref[...])
    z = jnp.dot(y.astype(BF16), wglu_ref[...], preferred_element_type=F32) + bglu_ref[...]
    o_ssm = y * jax.nn.sigmoid(z)
    a = _rms_norm(att_ref[...], ag_ref[...]).astype(BF16)
    s = _rms_norm(o_ssm, sg_ref[...]).astype(BF16)
    wa = att_ref.shape[1]
    mix = (jnp.dot(a, wout_ref[:wa, :], preferred_element_type=F32)
           + jnp.dot(s, wout_ref[wa:, :], preferred_element_type=F32))
    o_ref[...] = _layer_norm(alpha * x_ref[...] + mix, g_ref[...], b_ref[...])


def _mixout(o_att, y_ssm, x2d, w_glu, b_glu, att_g, ssm_g, w_out, ln_g, ln_b, alpha):
    n_tok, d = x2d.shape
    wa = o_att.shape[1]
    ws = y_ssm.shape[1]
    tm = 256
    row = lambda w: pl.BlockSpec((tm, w), lambda i: (i, 0))
    full = lambda a: pl.BlockSpec(a.shape, lambda i: (0,) * a.ndim)
    args = (o_att, y_ssm, x2d, w_glu.astype(BF16), b_glu.reshape(1, ws), att_g.reshape(1, wa),
            ssm_g.reshape(1, ws), w_out.astype(BF16), ln_g.reshape(1, d), ln_b.reshape(1, d))
    return pl.pallas_call(
        functools.partial(_mixout_kernel, alpha=alpha),
        out_shape=jax.ShapeDtypeStruct((n_tok, d), F32),
        grid=(n_tok // tm,),
        in_specs=[row(wa), row(ws), row(d)] + [full(a) for a in args[3:]],
        out_specs=row(d),
        compiler_params=_cparams(("parallel",), 32),
        name="mixout",
    )(*args)


def _split_bf16(v):
    hi = v.astype(BF16)
    lo = (v - hi.astype(F32)).astype(BF16)
    return hi, lo


def _router_kernel(h_ref, wt_ref, bias_ref, e_ref, g_ref, r_ref, cnt_ref, run_ref):
    tm = h_ref.shape[0]
    n_exp = wt_ref.shape[0]
    gsz = n_exp // N_EXPERT_GROUPS

    @pl.when(pl.program_id(0) == 0)
    def _():
        run_ref[...] = jnp.zeros_like(run_ref)

    w_hi, w_lo = _split_bf16(wt_ref[...])
    h_hi, h_lo = _split_bf16(h_ref[...])
    nt = (((1,), (1,)), ((), ()))
    logits = (lax.dot_general(w_hi, h_hi, nt, preferred_element_type=F32)
              + lax.dot_general(w_hi, h_lo, nt, preferred_element_type=F32)
              + lax.dot_general(w_lo, h_hi, nt, preferred_element_type=F32))
    scores = jax.nn.sigmoid(logits)
    choice = scores + bias_ref[:, 0:1]

    gio = lax.broadcasted_iota(jnp.int32, (gsz, tm), 0).astype(F32)
    gscore = []
    for g in range(N_EXPERT_GROUPS):
        cg = choice[g * gsz:(g + 1) * gsz, :]
        m1 = jnp.max(cg, axis=0, keepdims=True)
        i1 = jnp.min(jnp.where(cg == m1, gio, float(gsz)), axis=0, keepdims=True)
        m2 = jnp.max(jnp.where(gio == i1, NEG_INF, cg), axis=0, keepdims=True)
        gscore.append(m1 + m2)
    masked = []
    for g in range(N_EXPERT_GROUPS):
        beat = jnp.zeros((1, tm), F32)
        for o in range(N_EXPERT_GROUPS):
            if o == g:
                continue
            wins = (gscore[o] >= gscore[g]) if o < g else (gscore[o] > gscore[g])
            beat = beat + jnp.where(wins, 1.0, 0.0)
        keep = beat < float(TOPK_GROUPS)
        masked.append(jnp.where(keep, choice[g * gsz:(g + 1) * gsz, :], NEG_INF))
    cur = jnp.concatenate(masked, axis=0)

    eio = lax.broadcasted_iota(jnp.int32, (n_exp, tm), 0).astype(F32)
    idxs = []
    gates = []
    candidates = cur
    for _ in range(TOP_K):
        m = jnp.max(cur, axis=0, keepdims=True)
        idx = jnp.min(jnp.where(cur == m, eio, float(n_exp)), axis=0, keepdims=True)
        hit = eio == idx
        idxs.append(idx)
        gates.append(jnp.sum(jnp.where(hit, scores, 0.0), axis=0, keepdims=True))
        cur = jnp.where(hit, NEG_INF, cur)
    onehot = jnp.where(cur != candidates, 1.0, 0.0)
    gate = jnp.concatenate(gates, axis=0)
    gate = ROUTED_SCALE * gate / (jnp.sum(gate, axis=0, keepdims=True) + 1e-20)

    si = lax.broadcasted_iota(jnp.int32, (tm, tm), 0)
    ti = lax.broadcasted_iota(jnp.int32, (tm, tm), 1)
    upper = jnp.where(si < ti, 1.0, 0.0).astype(BF16)
    before = jnp.dot(onehot.astype(BF16), upper, preferred_element_type=F32) + run_ref[:, 0:1]
    ranks = [jnp.sum(jnp.where(eio == idx, before, 0.0), axis=0, keepdims=True) for idx in idxs]

    e_ref[...] = jnp.concatenate(idxs, axis=0).astype(jnp.int32)
    g_ref[...] = gate
    r_ref[...] = jnp.concatenate(ranks, axis=0).astype(jnp.int32)
    run_ref[...] = run_ref[...] + jnp.sum(onehot, axis=1, keepdims=True)
    cnt_ref[...] = run_ref[...]


def _router(h, router_w, router_bias):
    n_tok, d = h.shape
    n_exp = router_w.shape[1]
    tm = 256
    wt = router_w.astype(F32).T
    bias = jnp.broadcast_to(router_bias.astype(F32)[:, None], (n_exp, LANES))
    tok = pl.BlockSpec((TOP_K, tm), lambda i: (0, i))
    return pl.pallas_call(
        _router_kernel,
        out_shape=(jax.ShapeDtypeStruct((TOP_K, n_tok), jnp.int32),
                   jax.ShapeDtypeStruct((TOP_K, n_tok), F32),
                   jax.ShapeDtypeStruct((TOP_K, n_tok), jnp.int32),
                   jax.ShapeDtypeStruct((n_exp, LANES), F32)),
        grid=(n_tok // tm,),
        in_specs=[pl.BlockSpec((tm, d), lambda i: (i, 0)),
                  pl.BlockSpec((n_exp, d), lambda i: (0, 0)),
                  pl.BlockSpec((n_exp, LANES), lambda i: (0, 0))],
        out_specs=(tok, tok, tok, pl.BlockSpec((n_exp, LANES), lambda i: (0, 0))),
        scratch_shapes=[pltpu.VMEM((n_exp, LANES), F32)],
        compiler_params=_cparams(("arbitrary",), 32),
        name="router",
    )(h, wt, bias)


def _dest_kernel(e_ref, r_ref, st_ref, d_ref):
    n_exp = st_ref.shape[0]
    tm = e_ref.shape[1]
    eio = lax.broadcasted_iota(jnp.int32, (n_exp, tm), 0)
    start = st_ref[:, 0:1]
    rows = [jnp.sum(jnp.where(eio == e_ref[k:k + 1, :], start, 0.0), axis=0, keepdims=True)
            for k in range(TOP_K)]
    d_ref[...] = jnp.concatenate(rows, axis=0).astype(jnp.int32) + r_ref[...]


def _dest(top_e, rank, starts):
    n_tok = top_e.shape[1]
    n_exp = starts.shape[0]
    tm = 512
    st = jnp.broadcast_to(starts.astype(F32)[:, None], (n_exp, LANES))
    tok = pl.BlockSpec((TOP_K, tm), lambda i: (0, i))
    return pl.pallas_call(
        _dest_kernel,
        out_shape=jax.ShapeDtypeStruct((TOP_K, n_tok), jnp.int32),
        grid=(n_tok // tm,),
        in_specs=[tok, tok, pl.BlockSpec((n_exp, LANES), lambda i: (0, 0))],
        out_specs=tok,
        compiler_params=_cparams(("parallel",), 32),
        name="dest",
    )(top_e, rank, st)


def _pack_bf16_pairs(val):
    half = val.shape[1] // 2
    lo = pltpu.bitcast(val[:, :half].astype(BF16).astype(F32), U32)
    hi = pltpu.bitcast(val[:, half:].astype(BF16).astype(F32), U32)
    return (lo >> 16) | (hi & jnp.uint32(0xFFFF0000))


def _unpack_bf16_pairs(words):
    lo = pltpu.bitcast(words << 16, F32)
    hi = pltpu.bitcast(words & jnp.uint32(0xFFFF0000), F32)
    return jnp.concatenate([lo, hi], axis=1)


def _to_row_tiles(dst_ref, slot, val):
    rows = val.shape[0]
    words = _pack_bf16_pairs(val)
    for j in range(ROW_TILE):
        dst_ref[slot, pl.ds(j, rows, stride=ROW_TILE), :] = words[:, j * LANES:(j + 1) * LANES]


def _row_tile_words(src_ref, idx, rows):
    return jnp.concatenate([src_ref[(*idx, pl.ds(j, rows, stride=ROW_TILE), slice(None))]
                            for j in range(ROW_TILE)], axis=1)


def _row_tile(r):
    return pl.ds(pl.multiple_of(r * ROW_TILE, ROW_TILE), ROW_TILE)


def _dispatch_kernel(dest_ref, h_ref, xs_ref, ht_ref, sem):
    tm = h_ref.shape[0]
    i = pl.program_id(0)
    cur = i % 2
    _to_row_tiles(ht_ref, cur, h_ref[...])

    def issue(t, carry):
        for k in range(TOP_K):
            pltpu.make_async_copy(ht_ref.at[cur, _row_tile(t)], xs_ref.at[_row_tile(dest_ref[t * TOP_K + k])],
                                  sem.at[cur]).start(priority=k % 2)
        return carry
    lax.fori_loop(0, tm, issue, 0, unroll=2)

    def drain(slot):
        for k in range(TOP_K):
            pltpu.make_async_copy(ht_ref.at[slot], xs_ref.at[pl.ds(0, tm * ROW_TILE)], sem.at[slot]).wait()

    @pl.when(i > 0)
    def _():
        drain(1 - cur)

    @pl.when(i == pl.num_programs(0) - 1)
    def _():
        drain(cur)


def _dispatch(h, dest, n_exp):
    n_tok, d = h.shape
    assert d == 2 * ROW_TILE * LANES
    tm = MOE_TOKEN_TILE
    n_rows = (pl.cdiv(n_tok * TOP_K, EXPERT_ROWS) + n_exp) * EXPERT_ROWS
    return pl.pallas_call(
        _dispatch_kernel,
        out_shape=jax.ShapeDtypeStruct((n_rows * ROW_TILE, LANES), U32),
        grid=(n_tok // tm,),
        in_specs=[pl.BlockSpec((TOP_K * tm,), lambda i: (i,), memory_space=pltpu.SMEM),
                  pl.BlockSpec((tm, d), lambda i: (i, 0))],
        out_specs=pl.BlockSpec(memory_space=pl.ANY),
        scratch_shapes=[pltpu.VMEM((2, tm * ROW_TILE, LANES), U32), pltpu.SemaphoreType.DMA((2,))],
        compiler_params=_cparams(("arbitrary",), 32),
        name="dispatch",
    )(dest, h)


def _experts_kernel(bstart_ref, bend_ref, cnt_ref, nblk_ref, xs_ref, wgu_hbm, wdn_hbm, ys_ref,
                    xbuf, ybuf, act_ref, wgu_f32, wdn_f32, wgu_bf, wdn_bf, xsem, ysem, wsem):
    e = pl.program_id(0)
    n_blk = nblk_ref[0]
    trows = xbuf.shape[1]
    rows = trows // ROW_TILE
    ff = wdn_bf.shape[0]
    b0 = bstart_ref[e]
    b1 = bend_ref[e]

    def block_rows(b):
        return pl.ds(pl.multiple_of(b * trows, trows), trows)

    def x_copy(b):
        slot = b % EXPERT_RING
        return pltpu.make_async_copy(xs_ref.at[block_rows(b)], xbuf.at[slot], xsem.at[slot])

    def y_copy(b):
        slot = b % EXPERT_OUT_RING
        return pltpu.make_async_copy(ybuf.at[slot], ys_ref.at[block_rows(b)], ysem.at[slot])

    @pl.when(e == 0)
    def _():
        for i in range(EXPERT_RING):
            @pl.when(i < n_blk)
            def _():
                x_copy(i).start(priority=1)

    n_exp = pl.num_programs(0)
    wslot = e % EXPERT_WEIGHT_RING

    def w_copies(x, slot):
        return (pltpu.make_async_copy(wgu_hbm.at[x], wgu_f32.at[slot], wsem.at[0, slot]),
                pltpu.make_async_copy(wdn_hbm.at[x], wdn_f32.at[slot], wsem.at[1, slot]))

    @pl.when(e == 0)
    def _():
        for i in range(EXPERT_WEIGHT_RING):
            @pl.when(i < n_exp)
            def _():
                for c in w_copies(i, i):
                    c.start()

    for c in w_copies(e, wslot):
        c.wait()
    wgu_bf[...] = wgu_f32[wslot].astype(BF16)
    wdn_bf[...] = wdn_f32[wslot].astype(BF16)

    @pl.when(e + EXPERT_WEIGHT_RING < n_exp)
    def _():
        for c in w_copies(e + EXPERT_WEIGHT_RING, wslot):
            c.start()

    def up(blocks):
        for b in blocks:
            x_copy(b).wait()
        for b in blocks:
            words = _row_tile_words(xbuf, (b % EXPERT_RING,), rows)
            row = lax.broadcasted_iota(jnp.int32, (rows, 1), 0)
            words = jnp.where(row < cnt_ref[e] - (b - b0) * rows, words, jnp.uint32(0))
            xb = _unpack_bf16_pairs(words).astype(BF16)
            gu = jnp.dot(xb, wgu_bf[...], preferred_element_type=F32)
            act_ref[b - b0] = (jax.nn.silu(gu[:, :ff]) * gu[:, ff:]).astype(BF16)
        for b in blocks:
            @pl.when(b + EXPERT_RING < n_blk)
            def _():
                x_copy(b + EXPERT_RING).start(priority=1)

    n_mine = b1 - b0

    def run_groups(fn):
        def body(p, carry):
            fn(tuple(b0 + EXPERT_GROUP * p + j for j in range(EXPERT_GROUP)))
            return carry
        lax.fori_loop(0, n_mine // EXPERT_GROUP, body, 0)
        size = EXPERT_GROUP // 2
        while size >= 1:
            @pl.when(n_mine & size != 0)
            def _(size=size):
                start = b0 + (n_mine // (2 * size)) * (2 * size)
                fn(tuple(start + j for j in range(size)))
            size //= 2

    run_groups(up)

    def down(blocks):
        for b in blocks:
            @pl.when(b >= EXPERT_OUT_RING)
            def _():
                y_copy(b - EXPERT_OUT_RING).wait()
        for b in blocks:
            _to_row_tiles(ybuf, b % EXPERT_OUT_RING,
                          jnp.dot(act_ref[b - b0], wdn_bf[...], preferred_element_type=F32))
        for b in blocks:
            y_copy(b).start(priority=1)
        last = blocks[-1]

        @pl.when(last == n_blk - 1)
        def _():
            for i in range(EXPERT_OUT_RING):
                @pl.when(last >= i)
                def _():
                    y_copy(last - i).wait()

    run_groups(down)


def _expert_blocks(counts):
    blocks = (counts + EXPERT_ROWS - 1) // EXPERT_ROWS
    bend = jnp.cumsum(blocks)
    bstart = bend - blocks
    i32 = lambda a: a.astype(jnp.int32)
    return i32(bstart), i32(bend), i32(bend[-1]).reshape(1), i32(bstart * EXPERT_ROWS)


def _experts(xs, bstart, bend, counts, n_blk, w_gu, w_down, n_tok):
    n_exp, d, ff2 = w_gu.shape
    ff = w_down.shape[1]
    n_rows = xs.shape[0] // ROW_TILE
    assert n_rows % EXPERT_ROWS == 0 and d == 2 * ROW_TILE * LANES
    max_blocks = pl.cdiv(n_tok, EXPERT_ROWS)
    grid_spec = pltpu.PrefetchScalarGridSpec(
        num_scalar_prefetch=4,
        grid=(n_exp,),
        in_specs=[pl.BlockSpec(memory_space=pl.ANY), pl.BlockSpec(memory_space=pl.ANY),
                  pl.BlockSpec(memory_space=pl.ANY)],
        out_specs=pl.BlockSpec(memory_space=pl.ANY),
        scratch_shapes=[pltpu.VMEM((EXPERT_RING, EXPERT_ROWS * ROW_TILE, LANES), U32),
                        pltpu.VMEM((EXPERT_OUT_RING, EXPERT_ROWS * ROW_TILE, LANES), U32),
                        pltpu.VMEM((max_blocks, EXPERT_ROWS, ff), BF16),
                        pltpu.VMEM((EXPERT_WEIGHT_RING, d, ff2), F32),
                        pltpu.VMEM((EXPERT_WEIGHT_RING, ff, d), F32),
                        pltpu.VMEM((d, ff2), BF16), pltpu.VMEM((ff, d), BF16),
                        pltpu.SemaphoreType.DMA((EXPERT_RING,)),
                        pltpu.SemaphoreType.DMA((EXPERT_OUT_RING,)),
                        pltpu.SemaphoreType.DMA((2, EXPERT_WEIGHT_RING))],
    )
    return pl.pallas_call(
        _experts_kernel,
        out_shape=jax.ShapeDtypeStruct(xs.shape, U32),
        grid_spec=grid_spec,
        compiler_params=_cparams(("arbitrary",), 48),
        name="experts",
    )(bstart, bend, counts, n_blk, xs, w_gu, w_down)


def _combine_kernel(dest_ref, dnext_ref, gate_ref, h_ref, ys_ref, wgu_ref, wdn_ref, g_ref, b_ref, o_ref,
                    buf_ref, routed_ref, sem, *, alpha):
    tm = h_ref.shape[0]
    i = pl.program_id(0)
    cur = i % 2
    chunk = SUBLANES

    def issue(d_ref, slot, t):
        for k in range(TOP_K):
            pltpu.make_async_copy(ys_ref.at[_row_tile(d_ref[t * TOP_K + k])], buf_ref.at[slot, k, _row_tile(t)],
                                  sem.at[slot]).start(priority=k % 2)

    @pl.when(i == 0)
    def _():
        def first(t, carry):
            issue(dest_ref, 0, t)
            return carry
        lax.fori_loop(0, tm, first, 0, unroll=2)

    for k in range(TOP_K):
        pltpu.make_async_copy(ys_ref.at[pl.ds(0, tm * ROW_TILE)], buf_ref.at[cur, k], sem.at[cur]).wait()

    def weighted_sum(c):
        tok = pl.ds(pl.multiple_of(c * chunk, chunk), chunk)
        gate = gate_ref[tok, :]
        total = None
        for k in range(TOP_K):
            words = jnp.concatenate(
                [buf_ref[cur, k, pl.ds(pl.multiple_of(c * (chunk * ROW_TILE), chunk * ROW_TILE) + j, chunk,
                                       stride=ROW_TILE), :] for j in range(ROW_TILE)], axis=1)
            term = gate[:, k:k + 1] * _unpack_bf16_pairs(words)
            total = term if total is None else total + term
        routed_ref[tok, :] = total

    @pl.when(i + 1 < pl.num_programs(0))
    def _():
        def body(c, carry):
            for t in range(chunk):
                issue(dnext_ref, 1 - cur, c * chunk + t)
            weighted_sum(c)
            return carry
        lax.fori_loop(0, tm // chunk, body, 0)

    @pl.when(i + 1 == pl.num_programs(0))
    def _():
        def body(c, carry):
            weighted_sum(c)
            return carry
        lax.fori_loop(0, tm // chunk, body, 0)

    h = h_ref[...]
    ff = wdn_ref.shape[0]
    gu = jnp.dot(h.astype(BF16), wgu_ref[...], preferred_element_type=F32)
    act = (jax.nn.silu(gu[:, :ff]) * gu[:, ff:]).astype(BF16)
    acc = alpha * h + jnp.dot(act, wdn_ref[...], preferred_element_type=F32) + routed_ref[...]
    o_ref[...] = _layer_norm(acc, g_ref[...], b_ref[...])


def _combine(h, ys, dest, gate_t, shared_w_gu, shared_w_down, ln_g, ln_b, alpha):
    n_tok, d = h.shape
    tm = MOE_TOKEN_TILE
    n_tiles = n_tok // tm
    full = lambda a: pl.BlockSpec(a.shape, lambda i: (0,) * a.ndim)
    wgu = shared_w_gu.astype(BF16)
    wdn = shared_w_down.astype(BF16)
    g2 = ln_g.reshape(1, d)
    b2 = ln_b.reshape(1, d)
    return pl.pallas_call(
        functools.partial(_combine_kernel, alpha=alpha),
        out_shape=jax.ShapeDtypeStruct((n_tok, d), F32),
        grid=(n_tiles,),
        in_specs=[pl.BlockSpec((TOP_K * tm,), lambda i: (i,), memory_space=pltpu.SMEM),
                  pl.BlockSpec((TOP_K * tm,), lambda i: (jnp.minimum(i + 1, n_tiles - 1),),
                               memory_space=pltpu.SMEM),
                  pl.BlockSpec((tm, TOP_K), lambda i: (i, 0)),
                  pl.BlockSpec((tm, d), lambda i: (i, 0)),
                  pl.BlockSpec(memory_space=pl.ANY),
                  full(wgu), full(wdn), full(g2), full(b2)],
        out_specs=pl.BlockSpec((tm, d), lambda i: (i, 0)),
        scratch_shapes=[pltpu.VMEM((2, TOP_K, tm * ROW_TILE, LANES), U32), pltpu.VMEM((tm, d), F32),
                        pltpu.SemaphoreType.DMA((2,))],
        compiler_params=_cparams(("arbitrary",), 48),
        name="combine",
    )(dest, dest, gate_t, h, ys, wgu, wdn, g2, b2)


def _moe(h, router_w, router_bias, w_gu, w_down, shared_w_gu, shared_w_down, ln_g, ln_b, alpha):
    top_e, gate, rank, cnt = _router(h, router_w, router_bias)
    counts = cnt[:, 0].astype(jnp.int32)
    bstart, bend, n_blk, pad_start = _expert_blocks(counts)
    dest = _dest(top_e, rank, pad_start)
    dest_tiles = dest.T.reshape(-1)
    xs = _dispatch(h, dest_tiles, counts.shape[0])
    ys = _experts(xs, bstart, bend, counts, n_blk, w_gu, w_down, h.shape[0])
    return _combine(h, ys, dest_tiles, gate.T, shared_w_gu, shared_w_down, ln_g, ln_b, alpha)


def kernel(x, w_in, att_norm_g, lam_re, lam_im, log_step, b_re, b_im, c_re, c_im, d_skip, w_glu, b_glu,
           ssm_norm_g, w_out, ln1_g, ln1_b, router_w, router_bias, w_gu, w_down, shared_w_gu,
           shared_w_down, ln2_g, ln2_b):
    bsz, seq, d = x.shape
    depth = w_in.shape[0]
    alpha = (2 * depth) ** 0.25
    h = x.reshape(bsz * seq, d)
    for i in range(depth):
        proj = _inproj(h, w_in[i].astype(BF16), seq)
        o_att = _attention(proj, bsz, seq)
        y_ssm = _s5(proj.reshape(bsz, seq, -1), 3 * ATT_WIDTH, lam_re[i], lam_im[i], log_step[i],
                    b_re[i], b_im[i], c_re[i], c_im[i], d_skip[i])
        h = _mixout(o_att, y_ssm.reshape(bsz * seq, -1), h, w_glu[i], b_glu[i], att_norm_g[i],
                    ssm_norm_g[i], w_out[i], ln1_g[i], ln1_b[i], alpha)
        h = _moe(h, router_w[i], router_bias[i], w_gu[i], w_down[i], shared_w_gu[i], shared_w_down[i],
                 ln2_g[i], ln2_b[i], alpha)
    return h.reshape(bsz, seq, d)
```

```python
import functools

import jax
import jax.numpy as jnp
from jax import lax
from jax.experimental import pallas as pl
from jax.experimental.pallas import tpu as pltpu

F32 = jnp.float32
BF16 = jnp.bfloat16
U32 = jnp.uint32

ATT_HEADS = 8
HEAD_DIM = 64
ATT_WIDTH = ATT_HEADS * HEAD_DIM
SSM_CH = 16
SSM_STATE = 64
ROPE_THETA = 500000.0
ROT_DIM = HEAD_DIM // 4
DILATIONS = (1, 4, 16)
ATT_BLOCK = 128
ATT_GROUP = 16
TOP_K = 8
N_EXPERT_GROUPS = 8
TOPK_GROUPS = 4
ROUTED_SCALE = 2.5
LN_EPS = 1e-5
RMS_EPS = 1e-6

LANES = 128
SUBLANES = 8
EXPERT_ROWS = 144
MOE_TOKEN_TILE = 512
EXPERT_RING = 16
EXPERT_OUT_RING = 16
EXPERT_GROUP = 4
EXPERT_WEIGHT_RING = 4
ROW_TILE = 4
NEG_INF = float("-inf")


def _cparams(sem, vmem_mb):
    return pltpu.CompilerParams(dimension_semantics=sem, vmem_limit_bytes=vmem_mb * 1024 * 1024)


def _inproj_kernel(x_ref, w_ref, cos_ref, sa_ref, sb_ref, o_ref, *, n_rot_cols):
    xb = x_ref[...].astype(BF16)
    cosf = cos_ref[...]
    sa = sa_ref[...]
    sb = sb_ref[...]
    width = o_ref.shape[1]
    chunk = 512
    for c in range(width // chunk):
        r = jnp.dot(xb, w_ref[:, c * chunk:(c + 1) * chunk], preferred_element_type=F32)
        if c * chunk < n_rot_cols:
            parts = []
            for s in range(chunk // LANES):
                t = r[:, s * LANES:(s + 1) * LANES]
                parts.append(t * cosf + pltpu.roll(t, LANES - ROT_DIM // 2, 1) * sa
                             + pltpu.roll(t, ROT_DIM // 2, 1) * sb)
            r = jnp.concatenate(parts, axis=1)
        o_ref[:, c * chunk:(c + 1) * chunk] = r


def _rope_lane_tables(seq):
    half = ROT_DIM // 2
    inv_freq = jnp.power(jnp.float32(ROPE_THETA), -jnp.arange(half, dtype=F32) / half)
    ang = jnp.arange(seq, dtype=F32)[:, None] * inv_freq[None, :]
    cos, sin = jnp.cos(ang), jnp.sin(ang)
    rest = HEAD_DIM - ROT_DIM
    cos_h = jnp.concatenate([cos, cos, jnp.ones((seq, rest), F32)], axis=1)
    sa_h = jnp.concatenate([-sin, jnp.zeros((seq, half + rest), F32)], axis=1)
    sb_h = jnp.concatenate([jnp.zeros((seq, half), F32), sin, jnp.zeros((seq, rest), F32)], axis=1)
    rep = LANES // HEAD_DIM
    return tuple(jnp.tile(t, (1, rep)) for t in (cos_h, sa_h, sb_h))


def _inproj(x2d, w_in_bf, seq):
    n_tok, d = x2d.shape
    width = w_in_bf.shape[1]
    tm = 512
    cosf, sa, sb = _rope_lane_tables(seq)
    tab_spec = pl.BlockSpec((tm, LANES), lambda i: (i % (seq // tm), 0))
    return pl.pallas_call(
        functools.partial(_inproj_kernel, n_rot_cols=2 * ATT_WIDTH),
        out_shape=jax.ShapeDtypeStruct((n_tok, width), F32),
        grid=(n_tok // tm,),
        in_specs=[pl.BlockSpec((tm, d), lambda i: (i, 0)),
                  pl.BlockSpec((d, width), lambda i: (0, 0)),
                  tab_spec, tab_spec, tab_spec],
        out_specs=pl.BlockSpec((tm, width), lambda i: (i, 0)),
        compiler_params=_cparams(("parallel",), 48),
        name="inproj",
    )(x2d, w_in_bf, cosf, sa, sb)


def _attn_kernel(q_ref, k_ref, v_ref, o_ref, qs_ref, ks_ref, vs_ref, tmp_ref, ob_ref, lb_ref, band_ref,
                 first_ref, *, seq):
    blk = ATT_BLOCK
    lane = lax.broadcasted_iota(jnp.int32, (1, LANES), 1)
    head0 = lane < HEAD_DIM
    scale = HEAD_DIM ** -0.5
    d1, d2 = DILATIONS[1], DILATIONS[2]
    assert DILATIONS[0] == 1 and d2 == d1 * d1
    seg = seq // d1
    sub = seg // d1

    qi = lax.broadcasted_iota(jnp.int32, (blk, 2 * blk), 0)
    kj = lax.broadcasted_iota(jnp.int32, (blk, 2 * blk), 1)
    dist = qi + blk - kj
    band_ref[...] = jnp.where((dist >= 0) & (dist <= blk), 0.0, NEG_INF)
    first_ref[...] = jnp.where((dist >= 0) & (kj >= blk), 0.0, NEG_INF)

    n_class = (1, d1, d2)
    class_len = (seq, seg, sub)
    base = [0]
    for c in range(len(DILATIONS)):
        base.append(base[c] + n_class[c] * (class_len[c] + blk))

    def kv_row0(c, g):
        return base[c] + g * (class_len[c] + blk)

    qs_ref[0] = (q_ref[...] * scale).astype(BF16)
    for a in range(d1):
        x = q_ref[pl.ds(a, seg, stride=d1), :] * scale
        tmp_ref[a * seg:(a + 1) * seg, :] = x
        qs_ref[1, a * seg:(a + 1) * seg, :] = x.astype(BF16)
    for g in range(d2):
        qs_ref[2, g * sub:(g + 1) * sub, :] = tmp_ref[pl.ds((g // d1) * seg + g % d1, sub, stride=d1),
                                                      :].astype(BF16)
    for src_ref, dst_ref in ((k_ref, ks_ref), (v_ref, vs_ref)):
        for c in range(len(DILATIONS)):
            for g in range(n_class[c]):
                dst_ref[kv_row0(c, g):kv_row0(c, g) + blk, :] = jnp.zeros((blk, LANES), BF16)
        dst_ref[kv_row0(0, 0) + blk:kv_row0(0, 0) + blk + seq, :] = src_ref[...].astype(BF16)
        for a in range(d1):
            x = src_ref[pl.ds(a, seg, stride=d1), :]
            tmp_ref[a * seg:(a + 1) * seg, :] = x
            dst_ref[kv_row0(1, a) + blk:kv_row0(1, a) + blk + seg, :] = x.astype(BF16)
        for g in range(d2):
            dst_ref[kv_row0(2, g) + blk:kv_row0(2, g) + blk + sub, :] = tmp_ref[
                pl.ds((g // d1) * seg + g % d1, sub, stride=d1), :].astype(BF16)

    def one_block(c, g, n, out_rows, bias_ref):
        q = qs_ref[c, pl.ds(aligned(g * class_len[c] + n * blk), blk), :]
        kv_rows = pl.ds(aligned(kv_row0(c, g) + n * blk), 2 * blk)
        kk = ks_ref[kv_rows, :]
        vv = vs_ref[kv_rows, :]
        outs = []
        lses = []
        for h in range(LANES // HEAD_DIM):
            hm = head0 if h == 0 else jnp.logical_not(head0)
            qh = jnp.where(hm, q, jnp.zeros_like(q))
            s = lax.dot_general(qh, kk, (((1,), (1,)), ((), ())), preferred_element_type=F32)
            s = s + bias_ref[...]
            m = jnp.max(s, axis=-1, keepdims=True)
            p = jnp.exp(s - m)
            den = jnp.sum(p, axis=-1, keepdims=True)
            outs.append(jnp.dot(p.astype(BF16), vv, preferred_element_type=F32) / den)
            lses.append(m + jnp.log(den))
        ob_ref[c, out_rows, :] = jnp.where(head0, outs[0], outs[1])
        lb_ref[c, out_rows, :] = jnp.where(head0, lses[0], lses[1])

    def run_blocks(n_blocks, fn):
        group = max(g for g in range(1, ATT_GROUP + 1) if n_blocks % g == 0)
        if n_blocks == group:
            for g in range(group):
                fn(g)
            return
        def body(it, carry):
            for g in range(group):
                fn(it * group + g)
            return carry
        lax.fori_loop(0, n_blocks // group, body, 0)

    def aligned(x):
        return x if isinstance(x, int) else pl.multiple_of(x, blk)

    one_block(0, 0, 0, pl.ds(0, blk), first_ref)
    run_blocks(seq // blk - 1,
               lambda i: one_block(0, 0, i + 1, pl.ds(aligned((i + 1) * blk), blk), band_ref))

    nb1 = seg // blk
    run_blocks(d1, lambda a: one_block(1, a, 0, pl.ds(a, blk, stride=d1), first_ref))
    def later1(i):
        a = i // (nb1 - 1)
        n = i - a * (nb1 - 1) + 1
        one_block(1, a, n, pl.ds(a + n * (d1 * blk), blk, stride=d1), band_ref)
    run_blocks(d1 * (nb1 - 1), later1)

    assert sub == blk
    def only2(g):
        a = g // d1
        one_block(2, g, 0, pl.ds(a + d1 * (g - a * d1), blk, stride=d2), first_ref)
    run_blocks(d2, only2)

    rc = 256
    def merge(i, carry):
        sl = pl.ds(pl.multiple_of(i * rc, rc), rc)
        l0 = lb_ref[0, sl, :]
        l1 = lb_ref[1, sl, :]
        l2 = lb_ref[2, sl, :]
        mx = jnp.maximum(jnp.maximum(l0, l1), l2)
        e0 = jnp.exp(l0 - mx)
        e1 = jnp.exp(l1 - mx)
        e2 = jnp.exp(l2 - mx)
        tot = e0 + e1 + e2
        o_ref[sl, :] = ((e0 / tot) * ob_ref[0, sl, :] + (e1 / tot) * ob_ref[1, sl, :]
                        + (e2 / tot) * ob_ref[2, sl, :])
        return carry
    lax.fori_loop(0, seq // rc, merge, 0)


def _attention(proj, bsz, seq):
    n_tok = proj.shape[0]
    pairs = ATT_WIDTH // LANES
    assert seq % (ATT_BLOCK * max(DILATIONS)) == 0
    kv_rows = sum(seq + d * ATT_BLOCK for d in DILATIONS)
    blk = (seq, LANES)
    return pl.pallas_call(
        functools.partial(_attn_kernel, seq=seq),
        out_shape=jax.ShapeDtypeStruct((n_tok, ATT_WIDTH), F32),
        grid=(bsz, pairs),
        in_specs=[pl.BlockSpec(blk, lambda b, h: (b, h)),
                  pl.BlockSpec(blk, lambda b, h: (b, pairs + h)),
                  pl.BlockSpec(blk, lambda b, h: (b, 2 * pairs + h))],
        out_specs=pl.BlockSpec(blk, lambda b, h: (b, h)),
        scratch_shapes=[pltpu.VMEM((len(DILATIONS), seq, LANES), BF16),
                        pltpu.VMEM((kv_rows, LANES), BF16),
                        pltpu.VMEM((kv_rows, LANES), BF16),
                        pltpu.VMEM((seq, LANES), F32),
                        pltpu.VMEM((len(DILATIONS), seq, LANES), F32),
                        pltpu.VMEM((len(DILATIONS), seq, LANES), F32),
                        pltpu.VMEM((ATT_BLOCK, 2 * ATT_BLOCK), F32),
                        pltpu.VMEM((ATT_BLOCK, 2 * ATT_BLOCK), F32)],
        compiler_params=_cparams(("parallel", "parallel"), 40),
        name="attn",
    )(proj, proj, proj)


def _s5_kernel(u_ref, bm_ref, lam_ref, cm_ref, dk_ref, o_ref, us_ref, st_ref, ys_ref, carry_ref, *, tc):
    bsz = u_ref.shape[0]
    half = st_ref.shape[1] // 2
    rows = tc * bsz
    mm_rows = 512

    @pl.when(pl.program_id(1) == 0)
    def _():
        carry_ref[...] = jnp.zeros_like(carry_ref)

    for b in range(bsz):
        us_ref[pl.ds(b, tc, stride=bsz), :] = u_ref[b]

    bm = bm_ref[0]
    for r0 in range(0, rows, mm_rows):
        st_ref[r0:r0 + mm_rows, :] = jnp.dot(us_ref[r0:r0 + mm_rows, :].astype(BF16), bm,
                                             preferred_element_type=F32)

    lam = lam_ref[0]
    lam_re = lam[:, :half]
    lam_im = lam[:, half:]

    def step(t, carry):
        xr, xi = carry
        sl = pl.ds(pl.multiple_of(t * bsz, bsz), bsz)
        nr = lam_re * xr - lam_im * xi + st_ref[sl, :half]
        ni = lam_re * xi + lam_im * xr + st_ref[sl, half:]
        st_ref[sl, :half] = nr
        st_ref[sl, half:] = ni
        return nr, ni

    xr, xi = lax.fori_loop(0, tc, step, (carry_ref[:, :half], carry_ref[:, half:]), unroll=4)
    carry_ref[:, :half] = xr
    carry_ref[:, half:] = xi

    cm = cm_ref[0]
    for r0 in range(0, rows, mm_rows):
        ys_ref[r0:r0 + mm_rows, :] = jnp.dot(st_ref[r0:r0 + mm_rows, :].astype(BF16), cm,
                                             preferred_element_type=F32)
    dk = dk_ref[...]
    for b in range(bsz):
        o_ref[b] = ys_ref[pl.ds(b, tc, stride=bsz), :] + dk * u_ref[b]


def _s5_params(lam_re, lam_im, log_step, b_re, b_im, c_re, c_im, bsz):
    groups = lam_re.shape[0]
    gpc = LANES // SSM_CH
    n_chunks = groups // gpc
    lam = lax.complex(lam_re.astype(F32), lam_im.astype(F32))
    step = jnp.exp(log_step.astype(F32))[:, None]
    lam_bar = jnp.exp(lam * step)
    bmat = lax.complex(b_re.astype(F32), b_im.astype(F32))
    b_bar = ((lam_bar - 1.0) / lam)[..., None] * bmat
    eye = jnp.eye(gpc, dtype=F32)

    def block_diag_in(t):
        t = t.reshape(n_chunks, gpc, SSM_STATE, SSM_CH)
        return jnp.einsum('ngpc,gh->ngchp', t, eye).reshape(n_chunks, gpc * SSM_CH, gpc * SSM_STATE)

    def block_diag_out(t):
        t = t.reshape(n_chunks, gpc, SSM_CH, SSM_STATE)
        return jnp.einsum('ngcp,gh->ngphc', t, eye).reshape(n_chunks, gpc * SSM_STATE, gpc * SSM_CH)

    bm = jnp.concatenate([block_diag_in(b_bar.real), block_diag_in(b_bar.imag)], axis=2).astype(BF16)
    cm = jnp.concatenate([block_diag_out(c_re.astype(F32)), block_diag_out(-c_im.astype(F32))],
                         axis=1).astype(BF16)
    lam_row = jnp.concatenate([lam_bar.real.reshape(n_chunks, gpc * SSM_STATE),
                               lam_bar.imag.reshape(n_chunks, gpc * SSM_STATE)], axis=1)
    lam_t = jnp.broadcast_to(lam_row[:, None, :], (n_chunks, bsz, 2 * gpc * SSM_STATE))
    return bm, lam_t, cm, n_chunks


def _s5(proj3, u_col0, lam_re, lam_im, log_step, b_re, b_im, c_re, c_im, d_skip):
    bsz, seq, _ = proj3.shape
    assert bsz == SUBLANES
    bm, lam_t, cm, n_chunks = _s5_params(lam_re, lam_im, log_step, b_re, b_im, c_re, c_im, bsz)
    width = n_chunks * LANES
    tc = 256
    st_cols = bm.shape[2]
    ublk0 = u_col0 // LANES
    return pl.pallas_call(
        functools.partial(_s5_kernel, tc=tc),
        out_shape=jax.ShapeDtypeStruct((bsz, seq, width), F32),
        grid=(n_chunks, seq // tc),
        in_specs=[pl.BlockSpec((bsz, tc, LANES), lambda c, t: (0, t, ublk0 + c)),
                  pl.BlockSpec((1, LANES, st_cols), lambda c, t: (c, 0, 0)),
                  pl.BlockSpec((1, bsz, st_cols), lambda c, t: (c, 0, 0)),
                  pl.BlockSpec((1, st_cols, LANES), lambda c, t: (c, 0, 0)),
                  pl.BlockSpec((1, LANES), lambda c, t: (0, c))],
        out_specs=pl.BlockSpec((bsz, tc, LANES), lambda c, t: (0, t, c)),
        scratch_shapes=[pltpu.VMEM((tc * bsz, LANES), F32),
                        pltpu.VMEM((tc * bsz, st_cols), F32),
                        pltpu.VMEM((tc * bsz, LANES), F32),
                        pltpu.VMEM((bsz, st_cols), F32)],
        compiler_params=_cparams(("arbitrary", "arbitrary"), 40),
        name="s5",
    )(proj3, bm, lam_t, cm, d_skip.reshape(1, width).astype(F32))


def _layer_norm(v, g, b):
    mu = jnp.mean(v, axis=-1, keepdims=True)
    var = jnp.mean(jnp.square(v - mu), axis=-1, keepdims=True)
    return (v - mu) * lax.rsqrt(var + LN_EPS) * g + b


def _rms_norm(v, g):
    return v * lax.rsqrt(jnp.mean(jnp.square(v), axis=-1, keepdims=True) + RMS_EPS) * g


def _mixout_kernel(att_ref, ssm_ref, x_ref, wglu_ref, bglu_ref, ag_ref, sg_ref, wout_ref, g_ref, b_ref,
                   o_ref, *, alpha):
    y = jax.nn.gelu(ssm_ref[...])
    z = jnp.dot(y.astype(BF16), wglu_ref[...], preferred_element_type=F32) + bglu_ref[...]
    o_ssm = y * jax.nn.sigmoid(z)
    a = _rms_norm(att_ref[...], ag_ref[...]).astype(BF16)
    s = _rms_norm(o_ssm, sg_ref[...]).astype(BF16)
    wa = att_ref.shape[1]
    mix = (jnp.dot(a, wout_ref[:wa, :], preferred_element_type=F32)
           + jnp.dot(s, wout_ref[wa:, :], preferred_element_type=F32))
    o_ref[...] = _layer_norm(alpha * x_ref[...] + mix, g_ref[...], b_ref[...])


def _mixout(o_att, y_ssm, x2d, w_glu, b_glu, att_g, ssm_g, w_out, ln_g, ln_b, alpha):
    n_tok, d = x2d.shape
    wa = o_att.shape[1]
    ws = y_ssm.shape[1]
    tm = 256
    row = lambda w: pl.BlockSpec((tm, w), lambda i: (i, 0))
    full = lambda a: pl.BlockSpec(a.shape, lambda i: (0,) * a.ndim)
    args = (o_att, y_ssm, x2d, w_glu.astype(BF16), b_glu.reshape(1, ws), att_g.reshape(1, wa),
            ssm_g.reshape(1, ws), w_out.astype(BF16), ln_g.reshape(1, d), ln_b.reshape(1, d))
    return pl.pallas_call(
        functools.partial(_mixout_kernel, alpha=alpha),
        out_shape=jax.ShapeDtypeStruct((n_tok, d), F32),
        grid=(n_tok // tm,),
        in_specs=[row(wa), row(ws), row(d)] + [full(a) for a in args[3:]],
        out_specs=row(d),
        compiler_params=_cparams(("parallel",), 32),
        name="mixout",
    )(*args)


def _split_bf16(v):
    hi = v.astype(BF16)
    lo = (v - hi.astype(F32)).astype(BF16)
    return hi, lo


def _router_kernel(h_ref, wt_ref, bias_ref, e_ref, g_ref, r_ref, cnt_ref, run_ref):
    tm = h_ref.shape[0]
    n_exp = wt_ref.shape[0]
    gsz = n_exp // N_EXPERT_GROUPS

    @pl.when(pl.program_id(0) == 0)
    def _():
        run_ref[...] = jnp.zeros_like(run_ref)

    w_hi, w_lo = _split_bf16(wt_ref[...])
    h_hi, h_lo = _split_bf16(h_ref[...])
    nt = (((1,), (1,)), ((), ()))
    logits = (lax.dot_general(w_hi, h_hi, nt, preferred_element_type=F32)
              + lax.dot_general(w_hi, h_lo, nt, preferred_element_type=F32)
              + lax.dot_general(w_lo, h_hi, nt, preferred_element_type=F32))
    scores = jax.nn.sigmoid(logits)
    choice = scores + bias_ref[:, 0:1]

    gio = lax.broadcasted_iota(jnp.int32, (gsz, tm), 0).astype(F32)
    gscore = []
    for g in range(N_EXPERT_GROUPS):
        cg = choice[g * gsz:(g + 1) * gsz, :]
        m1 = jnp.max(cg, axis=0, keepdims=True)
        i1 = jnp.min(jnp.where(cg == m1, gio, float(gsz)), axis=0, keepdims=True)
        m2 = jnp.max(jnp.where(gio == i1, NEG_INF, cg), axis=0, keepdims=True)
        gscore.append(m1 + m2)
    masked = []
    for g in range(N_EXPERT_GROUPS):
        beat = jnp.zeros((1, tm), F32)
        for o in range(N_EXPERT_GROUPS):
            if o == g:
                continue
            wins = (gscore[o] >= gscore[g]) if o < g else (gscore[o] > gscore[g])
            beat = beat + jnp.where(wins, 1.0, 0.0)
        keep = beat < float(TOPK_GROUPS)
        masked.append(jnp.where(keep, choice[g * gsz:(g + 1) * gsz, :], NEG_INF))
    cur = jnp.concatenate(masked, axis=0)

    eio = lax.broadcasted_iota(jnp.int32, (n_exp, tm), 0).astype(F32)
    idxs = []
    gates = []
    candidates = cur
    for _ in range(TOP_K):
        m = jnp.max(cur, axis=0, keepdims=True)
        idx = jnp.min(jnp.where(cur == m, eio, float(n_exp)), axis=0, keepdims=True)
        hit = eio == idx
        idxs.append(idx)
        gates.append(jnp.sum(jnp.where(hit, scores, 0.0), axis=0, keepdims=True))
        cur = jnp.where(hit, NEG_INF, cur)
    onehot = jnp.where(cur != candidates, 1.0, 0.0)
    gate = jnp.concatenate(gates, axis=0)
    gate = ROUTED_SCALE * gate / (jnp.sum(gate, axis=0, keepdims=True) + 1e-20)

    si = lax.broadcasted_iota(jnp.int32, (tm, tm), 0)
    ti = lax.broadcasted_iota(jnp.int32, (tm, tm), 1)
    upper = jnp.where(si < ti, 1.0, 0.0).astype(BF16)
    before = jnp.dot(onehot.astype(BF16), upper, preferred_element_type=F32) + run_ref[:, 0:1]
    ranks = [jnp.sum(jnp.where(eio == idx, before, 0.0), axis=0, keepdims=True) for idx in idxs]

    e_ref[...] = jnp.concatenate(idxs, axis=0).astype(jnp.int32)
    g_ref[...] = gate
    r_ref[...] = jnp.concatenate(ranks, axis=0).astype(jnp.int32)
    run_ref[...] = run_ref[...] + jnp.sum(onehot, axis=1, keepdims=True)
    cnt_ref[...] = run_ref[...]


def _router(h, router_w, router_bias):
    n_tok, d = h.shape
    n_exp = router_w.shape[1]
    tm = 256
    wt = router_w.astype(F32).T
    bias = jnp.broadcast_to(router_bias.astype(F32)[:, None], (n_exp, LANES))
    tok = pl.BlockSpec((TOP_K, tm), lambda i: (0, i))
    return pl.pallas_call(
        _router_kernel,
        out_shape=(jax.ShapeDtypeStruct((TOP_K, n_tok), jnp.int32),
                   jax.ShapeDtypeStruct((TOP_K, n_tok), F32),
                   jax.ShapeDtypeStruct((TOP_K, n_tok), jnp.int32),
                   jax.ShapeDtypeStruct((n_exp, LANES), F32)),
        grid=(n_tok // tm,),
        in_specs=[pl.BlockSpec((tm, d), lambda i: (i, 0)),
                  pl.BlockSpec((n_exp, d), lambda i: (0, 0)),
                  pl.BlockSpec((n_exp, LANES), lambda i: (0, 0))],
        out_specs=(tok, tok, tok, pl.BlockSpec((n_exp, LANES), lambda i: (0, 0))),
        scratch_shapes=[pltpu.VMEM((n_exp, LANES), F32)],
        compiler_params=_cparams(("arbitrary",), 32),
        name="router",
    )(h, wt, bias)


def _dest_kernel(e_ref, r_ref, st_ref, d_ref):
    n_exp = st_ref.shape[0]
    tm = e_ref.shape[1]
    eio = lax.broadcasted_iota(jnp.int32, (n_exp, tm), 0)
    start = st_ref[:, 0:1]
    rows = [jnp.sum(jnp.where(eio == e_ref[k:k + 1, :], start, 0.0), axis=0, keepdims=True)
            for k in range(TOP_K)]
    d_ref[...] = jnp.concatenate(rows, axis=0).astype(jnp.int32) + r_ref[...]


def _dest(top_e, rank, starts):
    n_tok = top_e.shape[1]
    n_exp = starts.shape[0]
    tm = 512
    st = jnp.broadcast_to(starts.astype(F32)[:, None], (n_exp, LANES))
    tok = pl.BlockSpec((TOP_K, tm), lambda i: (0, i))
    return pl.pallas_call(
        _dest_kernel,
        out_shape=jax.ShapeDtypeStruct((TOP_K, n_tok), jnp.int32),
        grid=(n_tok // tm,),
        in_specs=[tok, tok, pl.BlockSpec((n_exp, LANES), lambda i: (0, 0))],
        out_specs=tok,
        compiler_params=_cparams(("parallel",), 32),
        name="dest",
    )(top_e, rank, st)


def _pack_bf16_pairs(val):
    half = val.shape[1] // 2
    lo = pltpu.bitcast(val[:, :half].astype(BF16).astype(F32), U32)
    hi = pltpu.bitcast(val[:, half:].astype(BF16).astype(F32), U32)
    return (lo >> 16) | (hi & jnp.uint32(0xFFFF0000))


def _unpack_bf16_pairs(words):
    lo = pltpu.bitcast(words << 16, F32)
    hi = pltpu.bitcast(words & jnp.uint32(0xFFFF0000), F32)
    return jnp.concatenate([lo, hi], axis=1)


def _to_row_tiles(dst_ref, slot, val):
    rows = val.shape[0]
    words = _pack_bf16_pairs(val)
    for j in range(ROW_TILE):
        dst_ref[slot, pl.ds(j, rows, stride=ROW_TILE), :] = words[:, j * LANES:(j + 1) * LANES]


def _row_tile_words(src_ref, idx, rows):
    return jnp.concatenate([src_ref[(*idx, pl.ds(j, rows, stride=ROW_TILE), slice(None))]
                            for j in range(ROW_TILE)], axis=1)


def _row_tile(r):
    return pl.ds(pl.multiple_of(r * ROW_TILE, ROW_TILE), ROW_TILE)


def _dispatch_kernel(dest_ref, h_ref, xs_ref, ht_ref, sem):
    tm = h_ref.shape[0]
    i = pl.program_id(0)
    cur = i % 2
    _to_row_tiles(ht_ref, cur, h_ref[...])

    def issue(t, carry):
        for k in range(TOP_K):
            pltpu.make_async_copy(ht_ref.at[cur, _row_tile(t)], xs_ref.at[_row_tile(dest_ref[t * TOP_K + k])],
                                  sem.at[cur]).start(priority=k % 2)
        return carry
    lax.fori_loop(0, tm, issue, 0, unroll=2)

    def drain(slot):
        for k in range(TOP_K):
            pltpu.make_async_copy(ht_ref.at[slot], xs_ref.at[pl.ds(0, tm * ROW_TILE)], sem.at[slot]).wait()

    @pl.when(i > 0)
    def _():
        drain(1 - cur)

    @pl.when(i == pl.num_programs(0) - 1)
    def _():
        drain(cur)


def _dispatch(h, dest, n_exp):
    n_tok, d = h.shape
    assert d == 2 * ROW_TILE * LANES
    tm = MOE_TOKEN_TILE
    n_rows = (pl.cdiv(n_tok * TOP_K, EXPERT_ROWS) + n_exp) * EXPERT_ROWS
    return pl.pallas_call(
        _dispatch_kernel,
        out_shape=jax.ShapeDtypeStruct((n_rows * ROW_TILE, LANES), U32),
        grid=(n_tok // tm,),
        in_specs=[pl.BlockSpec((TOP_K * tm,), lambda i: (i,), memory_space=pltpu.SMEM),
                  pl.BlockSpec((tm, d), lambda i: (i, 0))],
        out_specs=pl.BlockSpec(memory_space=pl.ANY),
        scratch_shapes=[pltpu.VMEM((2, tm * ROW_TILE, LANES), U32), pltpu.SemaphoreType.DMA((2,))],
        compiler_params=_cparams(("arbitrary",), 32),
        name="dispatch",
    )(dest, h)


def _experts_kernel(bstart_ref, bend_ref, cnt_ref, nblk_ref, xs_ref, wgu_hbm, wdn_hbm, ys_ref,
                    xbuf, ybuf, act_ref, wgu_f32, wdn_f32, wgu_bf, wdn_bf, xsem, ysem, wsem):
    e = pl.program_id(0)
    n_blk = nblk_ref[0]
    trows = xbuf.shape[1]
    rows = trows // ROW_TILE
    ff = wdn_bf.shape[0]
    b0 = bstart_ref[e]
    b1 = bend_ref[e]

    def block_rows(b):
        return pl.ds(pl.multiple_of(b * trows, trows), trows)

    def x_copy(b):
        slot = b % EXPERT_RING
        return pltpu.make_async_copy(xs_ref.at[block_rows(b)], xbuf.at[slot], xsem.at[slot])

    def y_copy(b):
        slot = b % EXPERT_OUT_RING
        return pltpu.make_async_copy(ybuf.at[slot], ys_ref.at[block_rows(b)], ysem.at[slot])

    @pl.when(e == 0)
    def _():
        for i in range(EXPERT_RING):
            @pl.when(i < n_blk)
            def _():
                x_copy(i).start(priority=1)

    n_exp = pl.num_programs(0)
    wslot = e % EXPERT_WEIGHT_RING

    def w_copies(x, slot):
        return (pltpu.make_async_copy(wgu_hbm.at[x], wgu_f32.at[slot], wsem.at[0, slot]),
                pltpu.make_async_copy(wdn_hbm.at[x], wdn_f32.at[slot], wsem.at[1, slot]))

    @pl.when(e == 0)
    def _():
        for i in range(EXPERT_WEIGHT_RING):
            @pl.when(i < n_exp)
            def _():
                for c in w_copies(i, i):
                    c.start()

    for c in w_copies(e, wslot):
        c.wait()
    wgu_bf[...] = wgu_f32[wslot].astype(BF16)
    wdn_bf[...] = wdn_f32[wslot].astype(BF16)

    @pl.when(e + EXPERT_WEIGHT_RING < n_exp)
    def _():
        for c in w_copies(e + EXPERT_WEIGHT_RING, wslot):
            c.start()

    def up(blocks):
        for b in blocks:
            x_copy(b).wait()
        for b in blocks:
            words = _row_tile_words(xbuf, (b % EXPERT_RING,), rows)
            row = lax.broadcasted_iota(jnp.int32, (rows, 1), 0)
            words = jnp.where(row < cnt_ref[e] - (b - b0) * rows, words, jnp.uint32(0))
            xb = _unpack_bf16_pairs(words).astype(BF16)
            gu = jnp.dot(xb, wgu_bf[...], preferred_element_type=F32)
            act_ref[b - b0] = (jax.nn.silu(gu[:, :ff]) * gu[:, ff:]).astype(BF16)
        for b in blocks:
            @pl.when(b + EXPERT_RING < n_blk)
            def _():
                x_copy(b + EXPERT_RING).start(priority=1)

    n_mine = b1 - b0

    def run_groups(fn):
        def body(p, carry):
            fn(tuple(b0 + EXPERT_GROUP * p + j for j in range(EXPERT_GROUP)))
            return carry
        lax.fori_loop(0, n_mine // EXPERT_GROUP, body, 0)
        size = EXPERT_GROUP // 2
        while size >= 1:
            @pl.when(n_mine & size != 0)
            def _(size=size):
                start = b0 + (n_mine // (2 * size)) * (2 * size)
                fn(tuple(start + j for j in range(size)))
            size //= 2

    run_groups(up)

    def down(blocks):
        for b in blocks:
            @pl.when(b >= EXPERT_OUT_RING)
            def _():
                y_copy(b - EXPERT_OUT_RING).wait()
        for b in blocks:
            _to_row_tiles(ybuf, b % EXPERT_OUT_RING,
                          jnp.dot(act_ref[b - b0], wdn_bf[...], preferred_element_type=F32))
        for b in blocks:
            y_copy(b).start(priority=1)
        last = blocks[-1]

        @pl.when(last == n_blk - 1)
        def _():
            for i in range(EXPERT_OUT_RING):
                @pl.when(last >= i)
                def _():
                    y_copy(last - i).wait()

    run_groups(down)


def _expert_blocks(counts):
    blocks = (counts + EXPERT_ROWS - 1) // EXPERT_ROWS
    bend = jnp.cumsum(blocks)
    bstart = bend - blocks
    i32 = lambda a: a.astype(jnp.int32)
    return i32(bstart), i32(bend), i32(bend[-1]).reshape(1), i32(bstart * EXPERT_ROWS)


def _experts(xs, bstart, bend, counts, n_blk, w_gu, w_down, n_tok):
    n_exp, d, ff2 = w_gu.shape
    ff = w_down.shape[1]
    n_rows = xs.shape[0] // ROW_TILE
    assert n_rows % EXPERT_ROWS == 0 and d == 2 * ROW_TILE * LANES
    max_blocks = pl.cdiv(n_tok, EXPERT_ROWS)
    grid_spec = pltpu.PrefetchScalarGridSpec(
        num_scalar_prefetch=4,
        grid=(n_exp,),
        in_specs=[pl.BlockSpec(memory_space=pl.ANY), pl.BlockSpec(memory_space=pl.ANY),
                  pl.BlockSpec(memory_space=pl.ANY)],
        out_specs=pl.BlockSpec(memory_space=pl.ANY),
        scratch_shapes=[pltpu.VMEM((EXPERT_RING, EXPERT_ROWS * ROW_TILE, LANES), U32),
                        pltpu.VMEM((EXPERT_OUT_RING, EXPERT_ROWS * ROW_TILE, LANES), U32),
                        pltpu.VMEM((max_blocks, EXPERT_ROWS, ff), BF16),
                        pltpu.VMEM((EXPERT_WEIGHT_RING, d, ff2), F32),
                        pltpu.VMEM((EXPERT_WEIGHT_RING, ff, d), F32),
                        pltpu.VMEM((d, ff2), BF16), pltpu.VMEM((ff, d), BF16),
                        pltpu.SemaphoreType.DMA((EXPERT_RING,)),
                        pltpu.SemaphoreType.DMA((EXPERT_OUT_RING,)),
                        pltpu.SemaphoreType.DMA((2, EXPERT_WEIGHT_RING))],
    )
    return pl.pallas_call(
        _experts_kernel,
        out_shape=jax.ShapeDtypeStruct(xs.shape, U32),
        grid_spec=grid_spec,
        compiler_params=_cparams(("arbitrary",), 48),
        name="experts",
    )(bstart, bend, counts, n_blk, xs, w_gu, w_down)


def _combine_kernel(dest_ref, dnext_ref, gate_ref, h_ref, ys_ref, wgu_ref, wdn_ref, g_ref, b_ref, o_ref,
                    buf_ref, routed_ref, sem, *, alpha):
    tm = h_ref.shape[0]
    i = pl.program_id(0)
    cur = i % 2
    chunk = SUBLANES

    def issue(d_ref, slot, t):
        for k in range(TOP_K):
            pltpu.make_async_copy(ys_ref.at[_row_tile(d_ref[t * TOP_K + k])], buf_ref.at[slot, k, _row_tile(t)],
                                  sem.at[slot]).start(priority=k % 2)

    @pl.when(i == 0)
    def _():
        def first(t, carry):
            issue(dest_ref, 0, t)
            return carry
        lax.fori_loop(0, tm, first, 0, unroll=2)

    for k in range(TOP_K):
        pltpu.make_async_copy(ys_ref.at[pl.ds(0, tm * ROW_TILE)], buf_ref.at[cur, k], sem.at[cur]).wait()

    def weighted_sum(c):
        tok = pl.ds(pl.multiple_of(c * chunk, chunk), chunk)
        gate = gate_ref[tok, :]
        total = None
        for k in range(TOP_K):
            words = jnp.concatenate(
                [buf_ref[cur, k, pl.ds(pl.multiple_of(c * (chunk * ROW_TILE), chunk * ROW_TILE) + j, chunk,
                                       stride=ROW_TILE), :] for j in range(ROW_TILE)], axis=1)
            term = gate[:, k:k + 1] * _unpack_bf16_pairs(words)
            total = term if total is None else total + term
        routed_ref[tok, :] = total

    @pl.when(i + 1 < pl.num_programs(0))
    def _():
        def body(c, carry):
            for t in range(chunk):
                issue(dnext_ref, 1 - cur, c * chunk + t)
            weighted_sum(c)
            return carry
        lax.fori_loop(0, tm // chunk, body, 0)

    @pl.when(i + 1 == pl.num_programs(0))
    def _():
        def body(c, carry):
            weighted_sum(c)
            return carry
        lax.fori_loop(0, tm // chunk, body, 0)

    h = h_ref[...]
    ff = wdn_ref.shape[0]
    gu = jnp.dot(h.astype(BF16), wgu_ref[...], preferred_element_type=F32)
    act = (jax.nn.silu(gu[:, :ff]) * gu[:, ff:]).astype(BF16)
    acc = alpha * h + jnp.dot(act, wdn_ref[...], preferred_element_type=F32) + routed_ref[...]
    o_ref[...] = _layer_norm(acc, g_ref[...], b_ref[...])


def _combine(h, ys, dest, gate_t, shared_w_gu, shared_w_down, ln_g, ln_b, alpha):
    n_tok, d = h.shape
    tm = MOE_TOKEN_TILE
    n_tiles = n_tok // tm
    full = lambda a: pl.BlockSpec(a.shape, lambda i: (0,) * a.ndim)
    wgu = shared_w_gu.astype(BF16)
    wdn = shared_w_down.astype(BF16)
    g2 = ln_g.reshape(1, d)
    b2 = ln_b.reshape(1, d)
    return pl.pallas_call(
        functools.partial(_combine_kernel, alpha=alpha),
        out_shape=jax.ShapeDtypeStruct((n_tok, d), F32),
        grid=(n_tiles,),
        in_specs=[pl.BlockSpec((TOP_K * tm,), lambda i: (i,), memory_space=pltpu.SMEM),
                  pl.BlockSpec((TOP_K * tm,), lambda i: (jnp.minimum(i + 1, n_tiles - 1),),
                               memory_space=pltpu.SMEM),
                  pl.BlockSpec((tm, TOP_K), lambda i: (i, 0)),
                  pl.BlockSpec((tm, d), lambda i: (i, 0)),
                  pl.BlockSpec(memory_space=pl.ANY),
                  full(wgu), full(wdn), full(g2), full(b2)],
        out_specs=pl.BlockSpec((tm, d), lambda i: (i, 0)),
        scratch_shapes=[pltpu.VMEM((2, TOP_K, tm * ROW_TILE, LANES), U32), pltpu.VMEM((tm, d), F32),
                        pltpu.SemaphoreType.DMA((2,))],
        compiler_params=_cparams(("arbitrary",), 48),
        name="combine",
    )(dest, dest, gate_t, h, ys, wgu, wdn, g2, b2)


def _moe(h, router_w, router_bias, w_gu, w_down, shared_w_gu, shared_w_down, ln_g, ln_b, alpha):
    top_e, gate, rank, cnt = _router(h, router_w, router_bias)
    counts = cnt[:, 0].astype(jnp.int32)
    bstart, bend, n_blk, pad_start = _expert_blocks(counts)
    dest = _dest(top_e, rank, pad_start)
    dest_tiles = dest.T.reshape(-1)
    xs = _dispatch(h, dest_tiles, counts.shape[0])
    ys = _experts(xs, bstart, bend, counts, n_blk, w_gu, w_down, h.shape[0])
    return _combine(h, ys, dest_tiles, gate.T, shared_w_gu, shared_w_down, ln_g, ln_b, alpha)


def kernel(x, w_in, att_norm_g, lam_re, lam_im, log_step, b_re, b_im, c_re, c_im, d_skip, w_glu, b_glu,
           ssm_norm_g, w_out, ln1_g, ln1_b, router_w, router_bias, w_gu, w_down, shared_w_gu,
           shared_w_down, ln2_g, ln2_b):
    bsz, seq, d = x.shape
    depth = w_in.shape[0]
    alpha = (2 * depth) ** 0.25
    h = x.reshape(bsz * seq, d)
    for i in range(depth):
        proj = _inproj(h, w_in[i].astype(BF16), seq)
        o_att = _attention(proj, bsz, seq)
        y_ssm = _s5(proj.reshape(bsz, seq, -1), 3 * ATT_WIDTH, lam_re[i], lam_im[i], log_step[i],
                    b_re[i], b_im[i], c_re[i], c_im[i], d_skip[i])
        h = _mixout(o_att, y_ssm.reshape(bsz * seq, -1), h, w_glu[i], b_glu[i], att_norm_g[i],
                    ssm_norm_g[i], w_out[i], ln1_g[i], ln1_b[i], alpha)
        h = _moe(h, router_w[i], router_bias[i], w_gu[i], w_down[i], shared_w_gu[i], shared_w_down[i],
                 ln2_g[i], ln2_b[i], alpha)
    return h.reshape(bsz, seq, d)
```

```python
import functools

import jax
import jax.numpy as jnp
from jax import lax
from jax.experimental import pallas as pl
from jax.experimental.pallas import tpu as pltpu

F32 = jnp.float32
BF16 = jnp.bfloat16
U32 = jnp.uint32

ATT_HEADS = 8
HEAD_DIM = 64
ATT_WIDTH = ATT_HEADS * HEAD_DIM
SSM_CH = 16
SSM_STATE = 64
S5_SLAB = 32
ROPE_THETA = 500000.0
ROT_DIM = HEAD_DIM // 4
DILATIONS = (1, 4, 16)
ATT_BLOCK = 128
ATT_GROUP = 16
TOP_K = 8
N_EXPERT_GROUPS = 8
TOPK_GROUPS = 4
ROUTED_SCALE = 2.5
LN_EPS = 1e-5
RMS_EPS = 1e-6

LANES = 128
SUBLANES = 8
EXPERT_ROWS = 144
MOE_TOKEN_TILE = 512
EXPERT_RING = 16
EXPERT_OUT_RING = 16
EXPERT_GROUP = 4
EXPERT_WEIGHT_RING = 4
ROW_TILE = 4
NEG_INF = float("-inf")


def _cparams(sem, vmem_mb):
    return pltpu.CompilerParams(dimension_semantics=sem, vmem_limit_bytes=vmem_mb * 1024 * 1024)


def _inproj_kernel(x_ref, w_ref, cos_ref, sa_ref, sb_ref, o_ref, *, n_rot_cols):
    xb = x_ref[...].astype(BF16)
    cosf = cos_ref[...]
    sa = sa_ref[...]
    sb = sb_ref[...]
    width = o_ref.shape[1]
    chunk = 512
    for c in range(width // chunk):
        r = jnp.dot(xb, w_ref[:, c * chunk:(c + 1) * chunk], preferred_element_type=F32)
        if c * chunk < n_rot_cols:
            parts = []
            for s in range(chunk // LANES):
                t = r[:, s * LANES:(s + 1) * LANES]
                parts.append(t * cosf + pltpu.roll(t, LANES - ROT_DIM // 2, 1) * sa
                             + pltpu.roll(t, ROT_DIM // 2, 1) * sb)
            r = jnp.concatenate(parts, axis=1)
        o_ref[:, c * chunk:(c + 1) * chunk] = r


def _rope_lane_tables(seq):
    half = ROT_DIM // 2
    inv_freq = jnp.power(jnp.float32(ROPE_THETA), -jnp.arange(half, dtype=F32) / half)
    ang = jnp.arange(seq, dtype=F32)[:, None] * inv_freq[None, :]
    cos, sin = jnp.cos(ang), jnp.sin(ang)
    rest = HEAD_DIM - ROT_DIM
    cos_h = jnp.concatenate([cos, cos, jnp.ones((seq, rest), F32)], axis=1)
    sa_h = jnp.concatenate([-sin, jnp.zeros((seq, half + rest), F32)], axis=1)
    sb_h = jnp.concatenate([jnp.zeros((seq, half), F32), sin, jnp.zeros((seq, rest), F32)], axis=1)
    rep = LANES // HEAD_DIM
    return tuple(jnp.tile(t, (1, rep)) for t in (cos_h, sa_h, sb_h))


def _inproj(x2d, w_in_bf, seq):
    n_tok, d = x2d.shape
    width = w_in_bf.shape[1]
    tm = 512
    cosf, sa, sb = _rope_lane_tables(seq)
    tab_spec = pl.BlockSpec((tm, LANES), lambda i: (i % (seq // tm), 0))
    return pl.pallas_call(
        functools.partial(_inproj_kernel, n_rot_cols=2 * ATT_WIDTH),
        out_shape=jax.ShapeDtypeStruct((n_tok, width), F32),
        grid=(n_tok // tm,),
        in_specs=[pl.BlockSpec((tm, d), lambda i: (i, 0)),
                  pl.BlockSpec((d, width), lambda i: (0, 0)),
                  tab_spec, tab_spec, tab_spec],
        out_specs=pl.BlockSpec((tm, width), lambda i: (i, 0)),
        compiler_params=_cparams(("parallel",), 48),
        name="inproj",
    )(x2d, w_in_bf, cosf, sa, sb)


def _attn_kernel(q_ref, k_ref, v_ref, o_ref, qs_ref, ks_ref, vs_ref, tmp_ref, ob_ref, lb_ref, band_ref,
                 first_ref, *, seq):
    blk = ATT_BLOCK
    lane = lax.broadcasted_iota(jnp.int32, (1, LANES), 1)
    head0 = lane < HEAD_DIM
    scale = HEAD_DIM ** -0.5
    d1, d2 = DILATIONS[1], DILATIONS[2]
    assert DILATIONS[0] == 1 and d2 == d1 * d1
    seg = seq // d1
    sub = seg // d1

    qi = lax.broadcasted_iota(jnp.int32, (blk, 2 * blk), 0)
    kj = lax.broadcasted_iota(jnp.int32, (blk, 2 * blk), 1)
    dist = qi + blk - kj
    band_ref[...] = jnp.where((dist >= 0) & (dist <= blk), 0.0, NEG_INF)
    first_ref[...] = jnp.where((dist >= 0) & (kj >= blk), 0.0, NEG_INF)

    n_class = (1, d1, d2)
    class_len = (seq, seg, sub)
    base = [0]
    for c in range(len(DILATIONS)):
        base.append(base[c] + n_class[c] * (class_len[c] + blk))

    def kv_row0(c, g):
        return base[c] + g * (class_len[c] + blk)

    qs_ref[0] = (q_ref[...] * scale).astype(BF16)
    for a in range(d1):
        x = q_ref[pl.ds(a, seg, stride=d1), :] * scale
        tmp_ref[a * seg:(a + 1) * seg, :] = x
        qs_ref[1, a * seg:(a + 1) * seg, :] = x.astype(BF16)
    for g in range(d2):
        qs_ref[2, g * sub:(g + 1) * sub, :] = tmp_ref[pl.ds((g // d1) * seg + g % d1, sub, stride=d1),
                                                      :].astype(BF16)
    for src_ref, dst_ref in ((k_ref, ks_ref), (v_ref, vs_ref)):
        for c in range(len(DILATIONS)):
            for g in range(n_class[c]):
                dst_ref[kv_row0(c, g):kv_row0(c, g) + blk, :] = jnp.zeros((blk, LANES), BF16)
        dst_ref[kv_row0(0, 0) + blk:kv_row0(0, 0) + blk + seq, :] = src_ref[...].astype(BF16)
        for a in range(d1):
            x = src_ref[pl.ds(a, seg, stride=d1), :]
            tmp_ref[a * seg:(a + 1) * seg, :] = x
            dst_ref[kv_row0(1, a) + blk:kv_row0(1, a) + blk + seg, :] = x.astype(BF16)
        for g in range(d2):
            dst_ref[kv_row0(2, g) + blk:kv_row0(2, g) + blk + sub, :] = tmp_ref[
                pl.ds((g // d1) * seg + g % d1, sub, stride=d1), :].astype(BF16)

    def one_block(c, g, n, out_rows, bias_ref):
        q = qs_ref[c, pl.ds(aligned(g * class_len[c] + n * blk), blk), :]
        kv_rows = pl.ds(aligned(kv_row0(c, g) + n * blk), 2 * blk)
        kk = ks_ref[kv_rows, :]
        vv = vs_ref[kv_rows, :]
        outs = []
        lses = []
        for h in range(LANES // HEAD_DIM):
            hm = head0 if h == 0 else jnp.logical_not(head0)
            qh = jnp.where(hm, q, jnp.zeros_like(q))
            s = lax.dot_general(qh, kk, (((1,), (1,)), ((), ())), preferred_element_type=F32)
            s = s + bias_ref[...]
            m = jnp.max(s, axis=-1, keepdims=True)
            p = jnp.exp(s - m)
            den = jnp.sum(p, axis=-1, keepdims=True)
            outs.append(jnp.dot(p.astype(BF16), vv, preferred_element_type=F32) / den)
            lses.append(m + jnp.log(den))
        ob_ref[c, out_rows, :] = jnp.where(head0, outs[0], outs[1])
        lb_ref[c, out_rows, :] = jnp.where(head0, lses[0], lses[1])

    def run_blocks(n_blocks, fn):
        group = max(g for g in range(1, ATT_GROUP + 1) if n_blocks % g == 0)
        if n_blocks == group:
            for g in range(group):
                fn(g)
            return
        def body(it, carry):
            for g in range(group):
                fn(it * group + g)
            return carry
        lax.fori_loop(0, n_blocks // group, body, 0)

    def aligned(x):
        return x if isinstance(x, int) else pl.multiple_of(x, blk)

    one_block(0, 0, 0, pl.ds(0, blk), first_ref)
    run_blocks(seq // blk - 1,
               lambda i: one_block(0, 0, i + 1, pl.ds(aligned((i + 1) * blk), blk), band_ref))

    nb1 = seg // blk
    run_blocks(d1, lambda a: one_block(1, a, 0, pl.ds(a, blk, stride=d1), first_ref))
    def later1(i):
        a = i // (nb1 - 1)
        n = i - a * (nb1 - 1) + 1
        one_block(1, a, n, pl.ds(a + n * (d1 * blk), blk, stride=d1), band_ref)
    run_blocks(d1 * (nb1 - 1), later1)

    assert sub == blk
    def only2(g):
        a = g // d1
        one_block(2, g, 0, pl.ds(a + d1 * (g - a * d1), blk, stride=d2), first_ref)
    run_blocks(d2, only2)

    rc = 256
    def merge(i, carry):
        sl = pl.ds(pl.multiple_of(i * rc, rc), rc)
        l0 = lb_ref[0, sl, :]
        l1 = lb_ref[1, sl, :]
        l2 = lb_ref[2, sl, :]
        mx = jnp.maximum(jnp.maximum(l0, l1), l2)
        e0 = jnp.exp(l0 - mx)
        e1 = jnp.exp(l1 - mx)
        e2 = jnp.exp(l2 - mx)
        tot = e0 + e1 + e2
        o_ref[sl, :] = ((e0 / tot) * ob_ref[0, sl, :] + (e1 / tot) * ob_ref[1, sl, :]
                        + (e2 / tot) * ob_ref[2, sl, :])
        return carry
    lax.fori_loop(0, seq // rc, merge, 0)


def _attention(proj, bsz, seq):
    n_tok = proj.shape[0]
    pairs = ATT_WIDTH // LANES
    assert seq % (ATT_BLOCK * max(DILATIONS)) == 0
    kv_rows = sum(seq + d * ATT_BLOCK for d in DILATIONS)
    blk = (seq, LANES)
    return pl.pallas_call(
        functools.partial(_attn_kernel, seq=seq),
        out_shape=jax.ShapeDtypeStruct((n_tok, ATT_WIDTH), F32),
        grid=(bsz, pairs),
        in_specs=[pl.BlockSpec(blk, lambda b, h: (b, h)),
                  pl.BlockSpec(blk, lambda b, h: (b, pairs + h)),
                  pl.BlockSpec(blk, lambda b, h: (b, 2 * pairs + h))],
        out_specs=pl.BlockSpec(blk, lambda b, h: (b, h)),
        scratch_shapes=[pltpu.VMEM((len(DILATIONS), seq, LANES), BF16),
                        pltpu.VMEM((kv_rows, LANES), BF16),
                        pltpu.VMEM((kv_rows, LANES), BF16),
                        pltpu.VMEM((seq, LANES), F32),
                        pltpu.VMEM((len(DILATIONS), seq, LANES), F32),
                        pltpu.VMEM((len(DILATIONS), seq, LANES), F32),
                        pltpu.VMEM((ATT_BLOCK, 2 * ATT_BLOCK), F32),
                        pltpu.VMEM((ATT_BLOCK, 2 * ATT_BLOCK), F32)],
        compiler_params=_cparams(("parallel", "parallel"), 40),
        name="attn",
    )(proj, proj, proj)


def _s5_kernel(u_ref, bm_ref, lam_ref, cm_ref, dk_ref, o_ref, us_ref, st_ref, ys_ref, carry_ref, *, tc):
    bsz = u_ref.shape[0]
    half = st_ref.shape[1] // 2
    slab_rows = S5_SLAB * bsz
    n_slabs = tc // S5_SLAB

    @pl.when(pl.program_id(1) == 0)
    def _():
        carry_ref[...] = jnp.zeros_like(carry_ref)

    for b in range(bsz):
        us_ref[pl.ds(b, tc, stride=bsz), :] = u_ref[b]

    bm = bm_ref[0]
    cm = cm_ref[0]
    lam = lam_ref[0]
    lam_re = lam[:, :half]
    lam_im = lam[:, half:]

    def slab(s):
        return slice(s * slab_rows, (s + 1) * slab_rows)

    def project_in(s):
        st_ref[slab(s), :] = jnp.dot(us_ref[slab(s), :].astype(BF16), bm, preferred_element_type=F32)

    def project_out(s):
        ys_ref[slab(s), :] = jnp.dot(st_ref[slab(s), :].astype(BF16), cm, preferred_element_type=F32)

    def scan(s, xr, xi):
        for t in range(S5_SLAB):
            sl = slice(s * slab_rows + t * bsz, s * slab_rows + (t + 1) * bsz)
            xr, xi = (lam_re * xr - lam_im * xi + st_ref[sl, :half],
                      lam_re * xi + lam_im * xr + st_ref[sl, half:])
            st_ref[sl, :half] = xr
            st_ref[sl, half:] = xi
        return xr, xi

    xr, xi = carry_ref[:, :half], carry_ref[:, half:]
    project_in(0)
    for s in range(n_slabs):
        if s + 1 < n_slabs:
            project_in(s + 1)
        xr, xi = scan(s, xr, xi)
        if s >= 1:
            project_out(s - 1)
    project_out(n_slabs - 1)
    carry_ref[:, :half] = xr
    carry_ref[:, half:] = xi

    dk = dk_ref[...]
    for b in range(bsz):
        o_ref[b] = ys_ref[pl.ds(b, tc, stride=bsz), :] + dk * u_ref[b]


def _s5_params(lam_re, lam_im, log_step, b_re, b_im, c_re, c_im, bsz):
    groups = lam_re.shape[0]
    gpc = LANES // SSM_CH
    n_chunks = groups // gpc
    lam = lax.complex(lam_re.astype(F32), lam_im.astype(F32))
    step = jnp.exp(log_step.astype(F32))[:, None]
    lam_bar = jnp.exp(lam * step)
    bmat = lax.complex(b_re.astype(F32), b_im.astype(F32))
    b_bar = ((lam_bar - 1.0) / lam)[..., None] * bmat
    eye = jnp.eye(gpc, dtype=F32)

    def block_diag_in(t):
        t = t.reshape(n_chunks, gpc, SSM_STATE, SSM_CH)
        return jnp.einsum('ngpc,gh->ngchp', t, eye).reshape(n_chunks, gpc * SSM_CH, gpc * SSM_STATE)

    def block_diag_out(t):
        t = t.reshape(n_chunks, gpc, SSM_CH, SSM_STATE)
        return jnp.einsum('ngcp,gh->ngphc', t, eye).reshape(n_chunks, gpc * SSM_STATE, gpc * SSM_CH)

    bm = jnp.concatenate([block_diag_in(b_bar.real), block_diag_in(b_bar.imag)], axis=2).astype(BF16)
    cm = jnp.concatenate([block_diag_out(c_re.astype(F32)), block_diag_out(-c_im.astype(F32))],
                         axis=1).astype(BF16)
    lam_row = jnp.concatenate([lam_bar.real.reshape(n_chunks, gpc * SSM_STATE),
                               lam_bar.imag.reshape(n_chunks, gpc * SSM_STATE)], axis=1)
    lam_t = jnp.broadcast_to(lam_row[:, None, :], (n_chunks, bsz, 2 * gpc * SSM_STATE))
    return bm, lam_t, cm, n_chunks


def _s5(proj3, u_col0, lam_re, lam_im, log_step, b_re, b_im, c_re, c_im, d_skip):
    bsz, seq, _ = proj3.shape
    assert bsz == SUBLANES
    bm, lam_t, cm, n_chunks = _s5_params(lam_re, lam_im, log_step, b_re, b_im, c_re, c_im, bsz)
    width = n_chunks * LANES
    tc = 256
    st_cols = bm.shape[2]
    ublk0 = u_col0 // LANES
    return pl.pallas_call(
        functools.partial(_s5_kernel, tc=tc),
        out_shape=jax.ShapeDtypeStruct((bsz, seq, width), F32),
        grid=(n_chunks, seq // tc),
        in_specs=[pl.BlockSpec((bsz, tc, LANES), lambda c, t: (0, t, ublk0 + c)),
                  pl.BlockSpec((1, LANES, st_cols), lambda c, t: (c, 0, 0)),
                  pl.BlockSpec((1, bsz, st_cols), lambda c, t: (c, 0, 0)),
                  pl.BlockSpec((1, st_cols, LANES), lambda c, t: (c, 0, 0)),
                  pl.BlockSpec((1, LANES), lambda c, t: (0, c))],
        out_specs=pl.BlockSpec((bsz, tc, LANES), lambda c, t: (0, t, c)),
        scratch_shapes=[pltpu.VMEM((tc * bsz, LANES), F32),
                        pltpu.VMEM((tc * bsz, st_cols), F32),
                        pltpu.VMEM((tc * bsz, LANES), F32),
                        pltpu.VMEM((bsz, st_cols), F32)],
        compiler_params=_cparams(("arbitrary", "arbitrary"), 40),
        name="s5",
    )(proj3, bm, lam_t, cm, d_skip.reshape(1, width).astype(F32))


def _layer_norm(v, g, b):
    mu = jnp.mean(v, axis=-1, keepdims=True)
    var = jnp.mean(jnp.square(v - mu), axis=-1, keepdims=True)
    return (v - mu) * lax.rsqrt(var + LN_EPS) * g + b


def _rms_norm(v, g):
    return v * lax.rsqrt(jnp.mean(jnp.square(v), axis=-1, keepdims=True) + RMS_EPS) * g


def _mixout_kernel(att_ref, ssm_ref, x_ref, wglu_ref, bglu_ref, ag_ref, sg_ref, wout_ref, g_ref, b_ref,
                   o_ref, *, alpha):
    y = jax.nn.gelu(ssm_ref[...])
    z = jnp.dot(y.astype(BF16), wglu_ref[...], preferred_element_type=F32) + bglu_ref[...]
    o_ssm = y * jax.nn.sigmoid(z)
    a = _rms_norm(att_ref[...], ag_ref[...]).astype(BF16)
    s = _rms_norm(o_ssm, sg_ref[...]).astype(BF16)
    wa = att_ref.shape[1]
    mix = (jnp.dot(a, wout_ref[:wa, :], preferred_element_type=F32)
           + jnp.dot(s, wout_ref[wa:, :], preferred_element_type=F32))
    o_ref[...] = _layer_norm(alpha * x_ref[...] + mix, g_ref[...], b_ref[...])


def _mixout(o_att, y_ssm, x2d, w_glu, b_glu, att_g, ssm_g, w_out, ln_g, ln_b, alpha):
    n_tok, d = x2d.shape
    wa = o_att.shape[1]
    ws = y_ssm.shape[1]
    tm = 256
    row = lambda w: pl.BlockSpec((tm, w), lambda i: (i, 0))
    full = lambda a: pl.BlockSpec(a.shape, lambda i: (0,) * a.ndim)
    args = (o_att, y_ssm, x2d, w_glu.astype(BF16), b_glu.reshape(1, ws), att_g.reshape(1, wa),
            ssm_g.reshape(1, ws), w_out.astype(BF16), ln_g.reshape(1, d), ln_b.reshape(1, d))
    return pl.pallas_call(
        functools.partial(_mixout_kernel, alpha=alpha),
        out_shape=jax.ShapeDtypeStruct((n_tok, d), F32),
        grid=(n_tok // tm,),
        in_specs=[row(wa), row(ws), row(d)] + [full(a) for a in args[3:]],
        out_specs=row(d),
        compiler_params=_cparams(("parallel",), 32),
        name="mixout",
    )(*args)


def _split_bf16(v):
    hi = v.astype(BF16)
    lo = (v - hi.astype(F32)).astype(BF16)
    return hi, lo


def _router_kernel(h_ref, wt_ref, bias_ref, e_ref, g_ref, r_ref, cnt_ref, run_ref):
    tm = h_ref.shape[0]
    n_exp = wt_ref.shape[0]
    gsz = n_exp // N_EXPERT_GROUPS

    @pl.when(pl.program_id(0) == 0)
    def _():
        run_ref[...] = jnp.zeros_like(run_ref)

    w_hi, w_lo = _split_bf16(wt_ref[...])
    h_hi, h_lo = _split_bf16(h_ref[...])
    nt = (((1,), (1,)), ((), ()))
    logits = (lax.dot_general(w_hi, h_hi, nt, preferred_element_type=F32)
              + lax.dot_general(w_hi, h_lo, nt, preferred_element_type=F32)
              + lax.dot_general(w_lo, h_hi, nt, preferred_element_type=F32))
    scores = jax.nn.sigmoid(logits)
    choice = scores + bias_ref[:, 0:1]

    gio = lax.broadcasted_iota(jnp.int32, (gsz, tm), 0).astype(F32)
    gscore = []
    for g in range(N_EXPERT_GROUPS):
        cg = choice[g * gsz:(g + 1) * gsz, :]
        m1 = jnp.max(cg, axis=0, keepdims=True)
        i1 = jnp.min(jnp.where(cg == m1, gio, float(gsz)), axis=0, keepdims=True)
        m2 = jnp.max(jnp.where(gio == i1, NEG_INF, cg), axis=0, keepdims=True)
        gscore.append(m1 + m2)
    masked = []
    for g in range(N_EXPERT_GROUPS):
        beat = jnp.zeros((1, tm), F32)
        for o in range(N_EXPERT_GROUPS):
            if o == g:
                continue
            wins = (gscore[o] >= gscore[g]) if o < g else (gscore[o] > gscore[g])
            beat = beat + jnp.where(wins, 1.0, 0.0)
        keep = beat < float(TOPK_GROUPS)
        masked.append(jnp.where(keep, choice[g * gsz:(g + 1) * gsz, :], NEG_INF))
    cur = jnp.concatenate(masked, axis=0)

    eio = lax.broadcasted_iota(jnp.int32, (n_exp, tm), 0).astype(F32)
    idxs = []
    gates = []
    candidates = cur
    for _ in range(TOP_K):
        m = jnp.max(cur, axis=0, keepdims=True)
        idx = jnp.min(jnp.where(cur == m, eio, float(n_exp)), axis=0, keepdims=True)
        hit = eio == idx
        idxs.append(idx)
        gates.append(jnp.sum(jnp.where(hit, scores, 0.0), axis=0, keepdims=True))
        cur = jnp.where(hit, NEG_INF, cur)
    onehot = jnp.where(cur != candidates, 1.0, 0.0)
    gate = jnp.concatenate(gates, axis=0)
    gate = ROUTED_SCALE * gate / (jnp.sum(gate, axis=0, keepdims=True) + 1e-20)

    si = lax.broadcasted_iota(jnp.int32, (tm, tm), 0)
    ti = lax.broadcasted_iota(jnp.int32, (tm, tm), 1)
    upper = jnp.where(si < ti, 1.0, 0.0).astype(BF16)
    before = jnp.dot(onehot.astype(BF16), upper, preferred_element_type=F32) + run_ref[:, 0:1]
    ranks = [jnp.sum(jnp.where(eio == idx, before, 0.0), axis=0, keepdims=True) for idx in idxs]

    e_ref[...] = jnp.concatenate(idxs, axis=0).astype(jnp.int32)
    g_ref[...] = gate
    r_ref[...] = jnp.concatenate(ranks, axis=0).astype(jnp.int32)
    run_ref[...] = run_ref[...] + jnp.sum(onehot, axis=1, keepdims=True)
    cnt_ref[...] = run_ref[...]


def _router(h, router_w, router_bias):
    n_tok, d = h.shape
    n_exp = router_w.shape[1]
    tm = 256
    wt = router_w.astype(F32).T
    bias = jnp.broadcast_to(router_bias.astype(F32)[:, None], (n_exp, LANES))
    tok = pl.BlockSpec((TOP_K, tm), lambda i: (0, i))
    return pl.pallas_call(
        _router_kernel,
        out_shape=(jax.ShapeDtypeStruct((TOP_K, n_tok), jnp.int32),
                   jax.ShapeDtypeStruct((TOP_K, n_tok), F32),
                   jax.ShapeDtypeStruct((TOP_K, n_tok), jnp.int32),
                   jax.ShapeDtypeStruct((n_exp, LANES), F32)),
        grid=(n_tok // tm,),
        in_specs=[pl.BlockSpec((tm, d), lambda i: (i, 0)),
                  pl.BlockSpec((n_exp, d), lambda i: (0, 0)),
                  pl.BlockSpec((n_exp, LANES), lambda i: (0, 0))],
        out_specs=(tok, tok, tok, pl.BlockSpec((n_exp, LANES), lambda i: (0, 0))),
        scratch_shapes=[pltpu.VMEM((n_exp, LANES), F32)],
        compiler_params=_cparams(("arbitrary",), 32),
        name="router",
    )(h, wt, bias)


def _dest_kernel(e_ref, r_ref, st_ref, d_ref):
    n_exp = st_ref.shape[0]
    tm = e_ref.shape[1]
    eio = lax.broadcasted_iota(jnp.int32, (n_exp, tm), 0)
    start = st_ref[:, 0:1]
    rows = [jnp.sum(jnp.where(eio == e_ref[k:k + 1, :], start, 0.0), axis=0, keepdims=True)
            for k in range(TOP_K)]
    d_ref[...] = jnp.concatenate(rows, axis=0).astype(jnp.int32) + r_ref[...]


def _dest(top_e, rank, starts):
    n_tok = top_e.shape[1]
    n_exp = starts.shape[0]
    tm = 512
    st = jnp.broadcast_to(starts.astype(F32)[:, None], (n_exp, LANES))
    tok = pl.BlockSpec((TOP_K, tm), lambda i: (0, i))
    return pl.pallas_call(
        _dest_kernel,
        out_shape=jax.ShapeDtypeStruct((TOP_K, n_tok), jnp.int32),
        grid=(n_tok // tm,),
        in_specs=[tok, tok, pl.BlockSpec((n_exp, LANES), lambda i: (0, 0))],
        out_specs=tok,
        compiler_params=_cparams(("parallel",), 32),
        name="dest",
    )(top_e, rank, st)


def _pack_bf16_pairs(val):
    half = val.shape[1] // 2
    lo = pltpu.bitcast(val[:, :half].astype(BF16).astype(F32), U32)
    hi = pltpu.bitcast(val[:, half:].astype(BF16).astype(F32), U32)
    return (lo >> 16) | (hi & jnp.uint32(0xFFFF0000))


def _unpack_bf16_pairs(words):
    lo = pltpu.bitcast(words << 16, F32)
    hi = pltpu.bitcast(words & jnp.uint32(0xFFFF0000), F32)
    return jnp.concatenate([lo, hi], axis=1)


def _to_row_tiles(dst_ref, slot, val):
    rows = val.shape[0]
    words = _pack_bf16_pairs(val)
    for j in range(ROW_TILE):
        dst_ref[slot, pl.ds(j, rows, stride=ROW_TILE), :] = words[:, j * LANES:(j + 1) * LANES]


def _row_tile_words(src_ref, idx, rows):
    return jnp.concatenate([src_ref[(*idx, pl.ds(j, rows, stride=ROW_TILE), slice(None))]
                            for j in range(ROW_TILE)], axis=1)


def _row_tile(r):
    return pl.ds(pl.multiple_of(r * ROW_TILE, ROW_TILE), ROW_TILE)


def _dispatch_kernel(dest_ref, h_ref, xs_ref, ht_ref, sem):
    tm = h_ref.shape[0]
    i = pl.program_id(0)
    cur = i % 2
    _to_row_tiles(ht_ref, cur, h_ref[...])

    def issue(t, carry):
        for k in range(TOP_K):
            pltpu.make_async_copy(ht_ref.at[cur, _row_tile(t)], xs_ref.at[_row_tile(dest_ref[t * TOP_K + k])],
                                  sem.at[cur]).start(priority=k % 2)
        return carry
    lax.fori_loop(0, tm, issue, 0, unroll=2)

    def drain(slot):
        for k in range(TOP_K):
            pltpu.make_async_copy(ht_ref.at[slot], xs_ref.at[pl.ds(0, tm * ROW_TILE)], sem.at[slot]).wait()

    @pl.when(i > 0)
    def _():
        drain(1 - cur)

    @pl.when(i == pl.num_programs(0) - 1)
    def _():
        drain(cur)


def _dispatch(h, dest, n_exp):
    n_tok, d = h.shape
    assert d == 2 * ROW_TILE * LANES
    tm = MOE_TOKEN_TILE
    n_rows = (pl.cdiv(n_tok * TOP_K, EXPERT_ROWS) + n_exp) * EXPERT_ROWS
    return pl.pallas_call(
        _dispatch_kernel,
        out_shape=jax.ShapeDtypeStruct((n_rows * ROW_TILE, LANES), U32),
        grid=(n_tok // tm,),
        in_specs=[pl.BlockSpec((TOP_K * tm,), lambda i: (i,), memory_space=pltpu.SMEM),
                  pl.BlockSpec((tm, d), lambda i: (i, 0))],
        out_specs=pl.BlockSpec(memory_space=pl.ANY),
        scratch_shapes=[pltpu.VMEM((2, tm * ROW_TILE, LANES), U32), pltpu.SemaphoreType.DMA((2,))],
        compiler_params=_cparams(("arbitrary",), 32),
        name="dispatch",
    )(dest, h)


def _experts_kernel(bstart_ref, bend_ref, cnt_ref, nblk_ref, xs_ref, wgu_hbm, wdn_hbm, ys_ref,
                    xbuf, ybuf, act_ref, wgu_f32, wdn_f32, wgu_bf, wdn_bf, xsem, ysem, wsem):
    e = pl.program_id(0)
    n_blk = nblk_ref[0]
    trows = xbuf.shape[1]
    rows = trows // ROW_TILE
    ff = wdn_bf.shape[0]
    b0 = bstart_ref[e]
    b1 = bend_ref[e]

    def block_rows(b):
        return pl.ds(pl.multiple_of(b * trows, trows), trows)

    def x_copy(b):
        slot = b % EXPERT_RING
        return pltpu.make_async_copy(xs_ref.at[block_rows(b)], xbuf.at[slot], xsem.at[slot])

    def y_copy(b):
        slot = b % EXPERT_OUT_RING
        return pltpu.make_async_copy(ybuf.at[slot], ys_ref.at[block_rows(b)], ysem.at[slot])

    @pl.when(e == 0)
    def _():
        for i in range(EXPERT_RING):
            @pl.when(i < n_blk)
            def _():
                x_copy(i).start(priority=1)

    n_exp = pl.num_programs(0)
    wslot = e % EXPERT_WEIGHT_RING

    def w_copies(x, slot):
        return (pltpu.make_async_copy(wgu_hbm.at[x], wgu_f32.at[slot], wsem.at[0, slot]),
                pltpu.make_async_copy(wdn_hbm.at[x], wdn_f32.at[slot], wsem.at[1, slot]))

    @pl.when(e == 0)
    def _():
        for i in range(EXPERT_WEIGHT_RING):
            @pl.when(i < n_exp)
            def _():
                for c in w_copies(i, i):
                    c.start()

    for c in w_copies(e, wslot):
        c.wait()
    wgu_bf[...] = wgu_f32[wslot].astype(BF16)
    wdn_bf[...] = wdn_f32[wslot].astype(BF16)

    @pl.when(e + EXPERT_WEIGHT_RING < n_exp)
    def _():
        for c in w_copies(e + EXPERT_WEIGHT_RING, wslot):
            c.start()

    def up(blocks):
        for b in blocks:
            x_copy(b).wait()
        for b in blocks:
            words = _row_tile_words(xbuf, (b % EXPERT_RING,), rows)
            row = lax.broadcasted_iota(jnp.int32, (rows, 1), 0)
            words = jnp.where(row < cnt_ref[e] - (b - b0) * rows, words, jnp.uint32(0))
            xb = _unpack_bf16_pairs(words).astype(BF16)
            gu = jnp.dot(xb, wgu_bf[...], preferred_element_type=F32)
            act_ref[b - b0] = (jax.nn.silu(gu[:, :ff]) * gu[:, ff:]).astype(BF16)
        for b in blocks:
            @pl.when(b + EXPERT_RING < n_blk)
            def _():
                x_copy(b + EXPERT_RING).start(priority=1)

    n_mine = b1 - b0

    def run_groups(fn):
        def body(p, carry):
            fn(tuple(b0 + EXPERT_GROUP * p + j for j in range(EXPERT_GROUP)))
            return carry
        lax.fori_loop(0, n_mine // EXPERT_GROUP, body, 0)
        size = EXPERT_GROUP // 2
        while size >= 1:
            @pl.when(n_mine & size != 0)
            def _(size=size):
                start = b0 + (n_mine // (2 * size)) * (2 * size)
                fn(tuple(start + j for j in range(size)))
            size //= 2

    run_groups(up)

    def down(blocks):
        for b in blocks:
            @pl.when(b >= EXPERT_OUT_RING)
            def _():
                y_copy(b - EXPERT_OUT_RING).wait()
        for b in blocks:
            _to_row_tiles(ybuf, b % EXPERT_OUT_RING,
                          jnp.dot(act_ref[b - b0], wdn_bf[...], preferred_element_type=F32))
        for b in blocks:
            y_copy(b).start(priority=1)
        last = blocks[-1]

        @pl.when(last == n_blk - 1)
        def _():
            for i in range(EXPERT_OUT_RING):
                @pl.when(last >= i)
                def _():
                    y_copy(last - i).wait()

    run_groups(down)


def _expert_blocks(counts):
    blocks = (counts + EXPERT_ROWS - 1) // EXPERT_ROWS
    bend = jnp.cumsum(blocks)
    bstart = bend - blocks
    i32 = lambda a: a.astype(jnp.int32)
    return i32(bstart), i32(bend), i32(bend[-1]).reshape(1), i32(bstart * EXPERT_ROWS)


def _experts(xs, bstart, bend, counts, n_blk, w_gu, w_down, n_tok):
    n_exp, d, ff2 = w_gu.shape
    ff = w_down.shape[1]
    n_rows = xs.shape[0] // ROW_TILE
    assert n_rows % EXPERT_ROWS == 0 and d == 2 * ROW_TILE * LANES
    max_blocks = pl.cdiv(n_tok, EXPERT_ROWS)
    grid_spec = pltpu.PrefetchScalarGridSpec(
        num_scalar_prefetch=4,
        grid=(n_exp,),
        in_specs=[pl.BlockSpec(memory_space=pl.ANY), pl.BlockSpec(memory_space=pl.ANY),
                  pl.BlockSpec(memory_space=pl.ANY)],
        out_specs=pl.BlockSpec(memory_space=pl.ANY),
        scratch_shapes=[pltpu.VMEM((EXPERT_RING, EXPERT_ROWS * ROW_TILE, LANES), U32),
                        pltpu.VMEM((EXPERT_OUT_RING, EXPERT_ROWS * ROW_TILE, LANES), U32),
                        pltpu.VMEM((max_blocks, EXPERT_ROWS, ff), BF16),
                        pltpu.VMEM((EXPERT_WEIGHT_RING, d, ff2), F32),
                        pltpu.VMEM((EXPERT_WEIGHT_RING, ff, d), F32),
                        pltpu.VMEM((d, ff2), BF16), pltpu.VMEM((ff, d), BF16),
                        pltpu.SemaphoreType.DMA((EXPERT_RING,)),
                        pltpu.SemaphoreType.DMA((EXPERT_OUT_RING,)),
                        pltpu.SemaphoreType.DMA((2, EXPERT_WEIGHT_RING))],
    )
    return pl.pallas_call(
        _experts_kernel,
        out_shape=jax.ShapeDtypeStruct(xs.shape, U32),
        grid_spec=grid_spec,
        compiler_params=_cparams(("arbitrary",), 48),
        name="experts",
    )(bstart, bend, counts, n_blk, xs, w_gu, w_down)


def _combine_kernel(dest_ref, dnext_ref, gate_ref, h_ref, ys_ref, wgu_ref, wdn_ref, g_ref, b_ref, o_ref,
                    buf_ref, routed_ref, sem, *, alpha):
    tm = h_ref.shape[0]
    i = pl.program_id(0)
    cur = i % 2
    chunk = SUBLANES

    def issue(d_ref, slot, t):
        for k in range(TOP_K):
            pltpu.make_async_copy(ys_ref.at[_row_tile(d_ref[t * TOP_K + k])], buf_ref.at[slot, k, _row_tile(t)],
                                  sem.at[slot]).start(priority=k % 2)

    @pl.when(i == 0)
    def _():
        def first(t, carry):
            issue(dest_ref, 0, t)
            return carry
        lax.fori_loop(0, tm, first, 0, unroll=2)

    for k in range(TOP_K):
        pltpu.make_async_copy(ys_ref.at[pl.ds(0, tm * ROW_TILE)], buf_ref.at[cur, k], sem.at[cur]).wait()

    def weighted_sum(c):
        tok = pl.ds(pl.multiple_of(c * chunk, chunk), chunk)
        gate = gate_ref[tok, :]
        total = None
        for k in range(TOP_K):
            words = jnp.concatenate(
                [buf_ref[cur, k, pl.ds(pl.multiple_of(c * (chunk * ROW_TILE), chunk * ROW_TILE) + j, chunk,
                                       stride=ROW_TILE), :] for j in range(ROW_TILE)], axis=1)
            term = gate[:, k:k + 1] * _unpack_bf16_pairs(words)
            total = term if total is None else total + term
        routed_ref[tok, :] = total

    @pl.when(i + 1 < pl.num_programs(0))
    def _():
        def body(c, carry):
            for t in range(chunk):
                issue(dnext_ref, 1 - cur, c * chunk + t)
            weighted_sum(c)
            return carry
        lax.fori_loop(0, tm // chunk, body, 0)

    @pl.when(i + 1 == pl.num_programs(0))
    def _():
        def body(c, carry):
            weighted_sum(c)
            return carry
        lax.fori_loop(0, tm // chunk, body, 0)

    h = h_ref[...]
    ff = wdn_ref.shape[0]
    gu = jnp.dot(h.astype(BF16), wgu_ref[...], preferred_element_type=F32)
    act = (jax.nn.silu(gu[:, :ff]) * gu[:, ff:]).astype(BF16)
    acc = alpha * h + jnp.dot(act, wdn_ref[...], preferred_element_type=F32) + routed_ref[...]
    o_ref[...] = _layer_norm(acc, g_ref[...], b_ref[...])


def _combine(h, ys, dest, gate_t, shared_w_gu, shared_w_down, ln_g, ln_b, alpha):
    n_tok, d = h.shape
    tm = MOE_TOKEN_TILE
    n_tiles = n_tok // tm
    full = lambda a: pl.BlockSpec(a.shape, lambda i: (0,) * a.ndim)
    wgu = shared_w_gu.astype(BF16)
    wdn = shared_w_down.astype(BF16)
    g2 = ln_g.reshape(1, d)
    b2 = ln_b.reshape(1, d)
    return pl.pallas_call(
        functools.partial(_combine_kernel, alpha=alpha),
        out_shape=jax.ShapeDtypeStruct((n_tok, d), F32),
        grid=(n_tiles,),
        in_specs=[pl.BlockSpec((TOP_K * tm,), lambda i: (i,), memory_space=pltpu.SMEM),
                  pl.BlockSpec((TOP_K * tm,), lambda i: (jnp.minimum(i + 1, n_tiles - 1),),
                               memory_space=pltpu.SMEM),
                  pl.BlockSpec((tm, TOP_K), lambda i: (i, 0)),
                  pl.BlockSpec((tm, d), lambda i: (i, 0)),
                  pl.BlockSpec(memory_space=pl.ANY),
                  full(wgu), full(wdn), full(g2), full(b2)],
        out_specs=pl.BlockSpec((tm, d), lambda i: (i, 0)),
        scratch_shapes=[pltpu.VMEM((2, TOP_K, tm * ROW_TILE, LANES), U32), pltpu.VMEM((tm, d), F32),
                        pltpu.SemaphoreType.DMA((2,))],
        compiler_params=_cparams(("arbitrary",), 48),
        name="combine",
    )(dest, dest, gate_t, h, ys, wgu, wdn, g2, b2)


def _moe(h, router_w, router_bias, w_gu, w_down, shared_w_gu, shared_w_down, ln_g, ln_b, alpha):
    top_e, gate, rank, cnt = _router(h, router_w, router_bias)
    counts = cnt[:, 0].astype(jnp.int32)
    bstart, bend, n_blk, pad_start = _expert_blocks(counts)
    dest = _dest(top_e, rank, pad_start)
    dest_tiles = dest.T.reshape(-1)
    xs = _dispatch(h, dest_tiles, counts.shape[0])
    ys = _experts(xs, bstart, bend, counts, n_blk, w_gu, w_down, h.shape[0])
    return _combine(h, ys, dest_tiles, gate.T, shared_w_gu, shared_w_down, ln_g, ln_b, alpha)


def kernel(x, w_in, att_norm_g, lam_re, lam_im, log_step, b_re, b_im, c_re, c_im, d_skip, w_glu, b_glu,
           ssm_norm_g, w_out, ln1_g, ln1_b, router_w, router_bias, w_gu, w_down, shared_w_gu,
           shared_w_down, ln2_g, ln2_b):
    bsz, seq, d = x.shape
    depth = w_in.shape[0]
    alpha = (2 * depth) ** 0.25
    h = x.reshape(bsz * seq, d)
    for i in range(depth):
        proj = _inproj(h, w_in[i].astype(BF16), seq)
        o_att = _attention(proj, bsz, seq)
        y_ssm = _s5(proj.reshape(bsz, seq, -1), 3 * ATT_WIDTH, lam_re[i], lam_im[i], log_step[i],
                    b_re[i], b_im[i], c_re[i], c_im[i], d_skip[i])
        h = _mixout(o_att, y_ssm.reshape(bsz * seq, -1), h, w_glu[i], b_glu[i], att_norm_g[i],
                    ssm_norm_g[i], w_out[i], ln1_g[i], ln1_b[i], alpha)
        h = _moe(h, router_w[i], router_bias[i], w_gu[i], w_down[i], shared_w_gu[i], shared_w_down[i],
                 ln2_g[i], ln2_b[i], alpha)
    return h.reshape(bsz, seq, d)
```

```python
import functools

import jax
import jax.numpy as jnp
from jax import lax
from jax.experimental import pallas as pl
from jax.experimental.pallas import tpu as pltpu

F32 = jnp.float32
BF16 = jnp.bfloat16
U32 = jnp.uint32

ATT_HEADS = 8
HEAD_DIM = 64
ATT_WIDTH = ATT_HEADS * HEAD_DIM
SSM_CH = 16
SSM_STATE = 64
S5_SLAB = 64
ROPE_THETA = 500000.0
ROT_DIM = HEAD_DIM // 4
DILATIONS = (1, 4, 16)
ATT_BLOCK = 128
ATT_GROUP = 16
TOP_K = 8
N_EXPERT_GROUPS = 8
TOPK_GROUPS = 4
ROUTED_SCALE = 2.5
LN_EPS = 1e-5
RMS_EPS = 1e-6

LANES = 128
SUBLANES = 8
EXPERT_ROWS = 144
MOE_TOKEN_TILE = 512
EXPERT_RING = 16
EXPERT_OUT_RING = 16
EXPERT_GROUP = 4
EXPERT_WEIGHT_RING = 4
ROW_TILE = 4
NEG_INF = float("-inf")


def _cparams(sem, vmem_mb):
    return pltpu.CompilerParams(dimension_semantics=sem, vmem_limit_bytes=vmem_mb * 1024 * 1024)


def _inproj_kernel(x_ref, w_ref, cos_ref, sa_ref, sb_ref, o_ref, *, n_rot_cols):
    xb = x_ref[...].astype(BF16)
    cosf = cos_ref[...]
    sa = sa_ref[...]
    sb = sb_ref[...]
    width = o_ref.shape[1]
    chunk = 512
    for c in range(width // chunk):
        r = jnp.dot(xb, w_ref[:, c * chunk:(c + 1) * chunk], preferred_element_type=F32)
        if c * chunk < n_rot_cols:
            parts = []
            for s in range(chunk // LANES):
                t = r[:, s * LANES:(s + 1) * LANES]
                parts.append(t * cosf + pltpu.roll(t, LANES - ROT_DIM // 2, 1) * sa
                             + pltpu.roll(t, ROT_DIM // 2, 1) * sb)
            r = jnp.concatenate(parts, axis=1)
        o_ref[:, c * chunk:(c + 1) * chunk] = r


def _rope_lane_tables(seq):
    half = ROT_DIM // 2
    inv_freq = jnp.power(jnp.float32(ROPE_THETA), -jnp.arange(half, dtype=F32) / half)
    ang = jnp.arange(seq, dtype=F32)[:, None] * inv_freq[None, :]
    cos, sin = jnp.cos(ang), jnp.sin(ang)
    rest = HEAD_DIM - ROT_DIM
    cos_h = jnp.concatenate([cos, cos, jnp.ones((seq, rest), F32)], axis=1)
    sa_h = jnp.concatenate([-sin, jnp.zeros((seq, half + rest), F32)], axis=1)
    sb_h = jnp.concatenate([jnp.zeros((seq, half), F32), sin, jnp.zeros((seq, rest), F32)], axis=1)
    rep = LANES // HEAD_DIM
    return tuple(jnp.tile(t, (1, rep)) for t in (cos_h, sa_h, sb_h))


def _inproj(x2d, w_in_bf, seq):
    n_tok, d = x2d.shape
    width = w_in_bf.shape[1]
    tm = 512
    cosf, sa, sb = _rope_lane_tables(seq)
    tab_spec = pl.BlockSpec((tm, LANES), lambda i: (i % (seq // tm), 0))
    return pl.pallas_call(
        functools.partial(_inproj_kernel, n_rot_cols=2 * ATT_WIDTH),
        out_shape=jax.ShapeDtypeStruct((n_tok, width), F32),
        grid=(n_tok // tm,),
        in_specs=[pl.BlockSpec((tm, d), lambda i: (i, 0)),
                  pl.BlockSpec((d, width), lambda i: (0, 0)),
                  tab_spec, tab_spec, tab_spec],
        out_specs=pl.BlockSpec((tm, width), lambda i: (i, 0)),
        compiler_params=_cparams(("parallel",), 48),
        name="inproj",
    )(x2d, w_in_bf, cosf, sa, sb)


def _attn_kernel(q_ref, k_ref, v_ref, o_ref, qs_ref, ks_ref, vs_ref, tmp_ref, ob_ref, lb_ref, band_ref,
                 first_ref, *, seq):
    blk = ATT_BLOCK
    lane = lax.broadcasted_iota(jnp.int32, (1, LANES), 1)
    head0 = lane < HEAD_DIM
    scale = HEAD_DIM ** -0.5
    d1, d2 = DILATIONS[1], DILATIONS[2]
    assert DILATIONS[0] == 1 and d2 == d1 * d1
    seg = seq // d1
    sub = seg // d1

    qi = lax.broadcasted_iota(jnp.int32, (blk, 2 * blk), 0)
    kj = lax.broadcasted_iota(jnp.int32, (blk, 2 * blk), 1)
    dist = qi + blk - kj
    band_ref[...] = jnp.where((dist >= 0) & (dist <= blk), 0.0, NEG_INF)
    first_ref[...] = jnp.where((dist >= 0) & (kj >= blk), 0.0, NEG_INF)

    n_class = (1, d1, d2)
    class_len = (seq, seg, sub)
    base = [0]
    for c in range(len(DILATIONS)):
        base.append(base[c] + n_class[c] * (class_len[c] + blk))

    def kv_row0(c, g):
        return base[c] + g * (class_len[c] + blk)

    qs_ref[0] = (q_ref[...] * scale).astype(BF16)
    for a in range(d1):
        x = q_ref[pl.ds(a, seg, stride=d1), :] * scale
        tmp_ref[a * seg:(a + 1) * seg, :] = x
        qs_ref[1, a * seg:(a + 1) * seg, :] = x.astype(BF16)
    for g in range(d2):
        qs_ref[2, g * sub:(g + 1) * sub, :] = tmp_ref[pl.ds((g // d1) * seg + g % d1, sub, stride=d1),
                                                      :].astype(BF16)
    for src_ref, dst_ref in ((k_ref, ks_ref), (v_ref, vs_ref)):
        for c in range(len(DILATIONS)):
            for g in range(n_class[c]):
                dst_ref[kv_row0(c, g):kv_row0(c, g) + blk, :] = jnp.zeros((blk, LANES), BF16)
        dst_ref[kv_row0(0, 0) + blk:kv_row0(0, 0) + blk + seq, :] = src_ref[...].astype(BF16)
        for a in range(d1):
            x = src_ref[pl.ds(a, seg, stride=d1), :]
            tmp_ref[a * seg:(a + 1) * seg, :] = x
            dst_ref[kv_row0(1, a) + blk:kv_row0(1, a) + blk + seg, :] = x.astype(BF16)
        for g in range(d2):
            dst_ref[kv_row0(2, g) + blk:kv_row0(2, g) + blk + sub, :] = tmp_ref[
                pl.ds((g // d1) * seg + g % d1, sub, stride=d1), :].astype(BF16)

    def one_block(c, g, n, out_rows, bias_ref):
        q = qs_ref[c, pl.ds(aligned(g * class_len[c] + n * blk), blk), :]
        kv_rows = pl.ds(aligned(kv_row0(c, g) + n * blk), 2 * blk)
        kk = ks_ref[kv_rows, :]
        vv = vs_ref[kv_rows, :]
        outs = []
        lses = []
        for h in range(LANES // HEAD_DIM):
            hm = head0 if h == 0 else jnp.logical_not(head0)
            qh = jnp.where(hm, q, jnp.zeros_like(q))
            s = lax.dot_general(qh, kk, (((1,), (1,)), ((), ())), preferred_element_type=F32)
            s = s + bias_ref[...]
            m = jnp.max(s, axis=-1, keepdims=True)
            p = jnp.exp(s - m)
            den = jnp.sum(p, axis=-1, keepdims=True)
            outs.append(jnp.dot(p.astype(BF16), vv, preferred_element_type=F32) / den)
            lses.append(m + jnp.log(den))
        ob_ref[c, out_rows, :] = jnp.where(head0, outs[0], outs[1])
        lb_ref[c, out_rows, :] = jnp.where(head0, lses[0], lses[1])

    def run_blocks(n_blocks, fn):
        group = max(g for g in range(1, ATT_GROUP + 1) if n_blocks % g == 0)
        if n_blocks == group:
            for g in range(group):
                fn(g)
            return
        def body(it, carry):
            for g in range(group):
                fn(it * group + g)
            return carry
        lax.fori_loop(0, n_blocks // group, body, 0)

    def aligned(x):
        return x if isinstance(x, int) else pl.multiple_of(x, blk)

    one_block(0, 0, 0, pl.ds(0, blk), first_ref)
    run_blocks(seq // blk - 1,
               lambda i: one_block(0, 0, i + 1, pl.ds(aligned((i + 1) * blk), blk), band_ref))

    nb1 = seg // blk
    run_blocks(d1, lambda a: one_block(1, a, 0, pl.ds(a, blk, stride=d1), first_ref))
    def later1(i):
        a = i // (nb1 - 1)
        n = i - a * (nb1 - 1) + 1
        one_block(1, a, n, pl.ds(a + n * (d1 * blk), blk, stride=d1), band_ref)
    run_blocks(d1 * (nb1 - 1), later1)

    assert sub == blk
    def only2(g):
        a = g // d1
        one_block(2, g, 0, pl.ds(a + d1 * (g - a * d1), blk, stride=d2), first_ref)
    run_blocks(d2, only2)

    rc = 256
    def merge(i, carry):
        sl = pl.ds(pl.multiple_of(i * rc, rc), rc)
        l0 = lb_ref[0, sl, :]
        l1 = lb_ref[1, sl, :]
        l2 = lb_ref[2, sl, :]
        mx = jnp.maximum(jnp.maximum(l0, l1), l2)
        e0 = jnp.exp(l0 - mx)
        e1 = jnp.exp(l1 - mx)
        e2 = jnp.exp(l2 - mx)
        tot = e0 + e1 + e2
        o_ref[sl, :] = ((e0 / tot) * ob_ref[0, sl, :] + (e1 / tot) * ob_ref[1, sl, :]
                        + (e2 / tot) * ob_ref[2, sl, :])
        return carry
    lax.fori_loop(0, seq // rc, merge, 0)


def _attention(proj, bsz, seq):
    n_tok = proj.shape[0]
    pairs = ATT_WIDTH // LANES
    assert seq % (ATT_BLOCK * max(DILATIONS)) == 0
    kv_rows = sum(seq + d * ATT_BLOCK for d in DILATIONS)
    blk = (seq, LANES)
    return pl.pallas_call(
        functools.partial(_attn_kernel, seq=seq),
        out_shape=jax.ShapeDtypeStruct((n_tok, ATT_WIDTH), F32),
        grid=(bsz, pairs),
        in_specs=[pl.BlockSpec(blk, lambda b, h: (b, h)),
                  pl.BlockSpec(blk, lambda b, h: (b, pairs + h)),
                  pl.BlockSpec(blk, lambda b, h: (b, 2 * pairs + h))],
        out_specs=pl.BlockSpec(blk, lambda b, h: (b, h)),
        scratch_shapes=[pltpu.VMEM((len(DILATIONS), seq, LANES), BF16),
                        pltpu.VMEM((kv_rows, LANES), BF16),
                        pltpu.VMEM((kv_rows, LANES), BF16),
                        pltpu.VMEM((seq, LANES), F32),
                        pltpu.VMEM((len(DILATIONS), seq, LANES), F32),
                        pltpu.VMEM((len(DILATIONS), seq, LANES), F32),
                        pltpu.VMEM((ATT_BLOCK, 2 * ATT_BLOCK), F32),
                        pltpu.VMEM((ATT_BLOCK, 2 * ATT_BLOCK), F32)],
        compiler_params=_cparams(("parallel", "parallel"), 40),
        name="attn",
    )(proj, proj, proj)


def _s5_kernel(u_ref, bm_ref, lam_ref, cm_ref, dk_ref, o_ref, us_ref, st_ref, ys_ref, carry_ref, *, tc):
    bsz = u_ref.shape[0]
    half = st_ref.shape[1] // 2
    slab_rows = S5_SLAB * bsz
    n_slabs = tc // S5_SLAB

    @pl.when(pl.program_id(1) == 0)
    def _():
        carry_ref[...] = jnp.zeros_like(carry_ref)

    for b in range(bsz):
        us_ref[pl.ds(b, tc, stride=bsz), :] = u_ref[b]

    bm = bm_ref[0]
    cm = cm_ref[0]
    lam = lam_ref[0]
    lam_re = lam[:, :half]
    lam_im = lam[:, half:]

    def slab(s):
        return slice(s * slab_rows, (s + 1) * slab_rows)

    def project_in(s):
        st_ref[slab(s), :] = jnp.dot(us_ref[slab(s), :].astype(BF16), bm, preferred_element_type=F32)

    def project_out(s):
        ys_ref[slab(s), :] = jnp.dot(st_ref[slab(s), :].astype(BF16), cm, preferred_element_type=F32)

    def scan(s, xr, xi):
        for t in range(S5_SLAB):
            sl = slice(s * slab_rows + t * bsz, s * slab_rows + (t + 1) * bsz)
            xr, xi = (lam_re * xr - lam_im * xi + st_ref[sl, :half],
                      lam_re * xi + lam_im * xr + st_ref[sl, half:])
            st_ref[sl, :half] = xr
            st_ref[sl, half:] = xi
        return xr, xi

    xr, xi = carry_ref[:, :half], carry_ref[:, half:]
    project_in(0)
    for s in range(n_slabs):
        if s + 1 < n_slabs:
            project_in(s + 1)
        xr, xi = scan(s, xr, xi)
        if s >= 1:
            project_out(s - 1)
    project_out(n_slabs - 1)
    carry_ref[:, :half] = xr
    carry_ref[:, half:] = xi

    dk = dk_ref[...]
    for b in range(bsz):
        o_ref[b] = ys_ref[pl.ds(b, tc, stride=bsz), :] + dk * u_ref[b]


def _s5_params(lam_re, lam_im, log_step, b_re, b_im, c_re, c_im, bsz):
    groups = lam_re.shape[0]
    gpc = LANES // SSM_CH
    n_chunks = groups // gpc
    lam = lax.complex(lam_re.astype(F32), lam_im.astype(F32))
    step = jnp.exp(log_step.astype(F32))[:, None]
    lam_bar = jnp.exp(lam * step)
    bmat = lax.complex(b_re.astype(F32), b_im.astype(F32))
    b_bar = ((lam_bar - 1.0) / lam)[..., None] * bmat
    eye = jnp.eye(gpc, dtype=F32)

    def block_diag_in(t):
        t = t.reshape(n_chunks, gpc, SSM_STATE, SSM_CH)
        return jnp.einsum('ngpc,gh->ngchp', t, eye).reshape(n_chunks, gpc * SSM_CH, gpc * SSM_STATE)

    def block_diag_out(t):
        t = t.reshape(n_chunks, gpc, SSM_CH, SSM_STATE)
        return jnp.einsum('ngcp,gh->ngphc', t, eye).reshape(n_chunks, gpc * SSM_STATE, gpc * SSM_CH)

    bm = jnp.concatenate([block_diag_in(b_bar.real), block_diag_in(b_bar.imag)], axis=2).astype(BF16)
    cm = jnp.concatenate([block_diag_out(c_re.astype(F32)), block_diag_out(-c_im.astype(F32))],
                         axis=1).astype(BF16)
    lam_row = jnp.concatenate([lam_bar.real.reshape(n_chunks, gpc * SSM_STATE),
                               lam_bar.imag.reshape(n_chunks, gpc * SSM_STATE)], axis=1)
    lam_t = jnp.broadcast_to(lam_row[:, None, :], (n_chunks, bsz, 2 * gpc * SSM_STATE))
    return bm, lam_t, cm, n_chunks


def _s5(proj3, u_col0, lam_re, lam_im, log_step, b_re, b_im, c_re, c_im, d_skip):
    bsz, seq, _ = proj3.shape
    assert bsz == SUBLANES
    bm, lam_t, cm, n_chunks = _s5_params(lam_re, lam_im, log_step, b_re, b_im, c_re, c_im, bsz)
    width = n_chunks * LANES
    tc = 256
    st_cols = bm.shape[2]
    ublk0 = u_col0 // LANES
    return pl.pallas_call(
        functools.partial(_s5_kernel, tc=tc),
        out_shape=jax.ShapeDtypeStruct((bsz, seq, width), F32),
        grid=(n_chunks, seq // tc),
        in_specs=[pl.BlockSpec((bsz, tc, LANES), lambda c, t: (0, t, ublk0 + c)),
                  pl.BlockSpec((1, LANES, st_cols), lambda c, t: (c, 0, 0)),
                  pl.BlockSpec((1, bsz, st_cols), lambda c, t: (c, 0, 0)),
                  pl.BlockSpec((1, st_cols, LANES), lambda c, t: (c, 0, 0)),
                  pl.BlockSpec((1, LANES), lambda c, t: (0, c))],
        out_specs=pl.BlockSpec((bsz, tc, LANES), lambda c, t: (0, t, c)),
        scratch_shapes=[pltpu.VMEM((tc * bsz, LANES), F32),
                        pltpu.VMEM((tc * bsz, st_cols), F32),
                        pltpu.VMEM((tc * bsz, LANES), F32),
                        pltpu.VMEM((bsz, st_cols), F32)],
        compiler_params=_cparams(("arbitrary", "arbitrary"), 40),
        name="s5",
    )(proj3, bm, lam_t, cm, d_skip.reshape(1, width).astype(F32))


def _layer_norm(v, g, b):
    mu = jnp.mean(v, axis=-1, keepdims=True)
    var = jnp.mean(jnp.square(v - mu), axis=-1, keepdims=True)
    return (v - mu) * lax.rsqrt(var + LN_EPS) * g + b


def _rms_norm(v, g):
    return v * lax.rsqrt(jnp.mean(jnp.square(v), axis=-1, keepdims=True) + RMS_EPS) * g


def _mixout_kernel(att_ref, ssm_ref, x_ref, wglu_ref, bglu_ref, ag_ref, sg_ref, wout_ref, g_ref, b_ref,
                   o_ref, *, alpha):
    y = jax.nn.gelu(ssm_ref[...])
    z = jnp.dot(y.astype(BF16), wglu_ref[...], preferred_element_type=F32) + bglu_ref[...]
    o_ssm = y * jax.nn.sigmoid(z)
    a = _rms_norm(att_ref[...], ag_ref[...]).astype(BF16)
    s = _rms_norm(o_ssm, sg_ref[...]).astype(BF16)
    wa = att_ref.shape[1]
    mix = (jnp.dot(a, wout_ref[:wa, :], preferred_element_type=F32)
           + jnp.dot(s, wout_ref[wa:, :], preferred_element_type=F32))
    o_ref[...] = _layer_norm(alpha * x_ref[...] + mix, g_ref[...], b_ref[...])


def _mixout(o_att, y_ssm, x2d, w_glu, b_glu, att_g, ssm_g, w_out, ln_g, ln_b, alpha):
    n_tok, d = x2d.shape
    wa = o_att.shape[1]
    ws = y_ssm.shape[1]
    tm = 256
    row = lambda w: pl.BlockSpec((tm, w), lambda i: (i, 0))
    full = lambda a: pl.BlockSpec(a.shape, lambda i: (0,) * a.ndim)
    args = (o_att, y_ssm, x2d, w_glu.astype(BF16), b_glu.reshape(1, ws), att_g.reshape(1, wa),
            ssm_g.reshape(1, ws), w_out.astype(BF16), ln_g.reshape(1, d), ln_b.reshape(1, d))
    return pl.pallas_call(
        functools.partial(_mixout_kernel, alpha=alpha),
        out_shape=jax.ShapeDtypeStruct((n_tok, d), F32),
        grid=(n_tok // tm,),
        in_specs=[row(wa), row(ws), row(d)] + [full(a) for a in args[3:]],
        out_specs=row(d),
        compiler_params=_cparams(("parallel",), 32),
        name="mixout",
    )(*args)


def _split_bf16(v):
    hi = v.astype(BF16)
    lo = (v - hi.astype(F32)).astype(BF16)
    return hi, lo


def _router_kernel(h_ref, wt_ref, bias_ref, e_ref, g_ref, r_ref, cnt_ref, run_ref):
    tm = h_ref.shape[0]
    n_exp = wt_ref.shape[0]
    gsz = n_exp // N_EXPERT_GROUPS

    @pl.when(pl.program_id(0) == 0)
    def _():
        run_ref[...] = jnp.zeros_like(run_ref)

    w_hi, w_lo = _split_bf16(wt_ref[...])
    h_hi, h_lo = _split_bf16(h_ref[...])
    nt = (((1,), (1,)), ((), ()))
    logits = (lax.dot_general(w_hi, h_hi, nt, preferred_element_type=F32)
              + lax.dot_general(w_hi, h_lo, nt, preferred_element_type=F32)
              + lax.dot_general(w_lo, h_hi, nt, preferred_element_type=F32))
    scores = jax.nn.sigmoid(logits)
    choice = scores + bias_ref[:, 0:1]

    gio = lax.broadcasted_iota(jnp.int32, (gsz, tm), 0).astype(F32)
    gscore = []
    for g in range(N_EXPERT_GROUPS):
        cg = choice[g * gsz:(g + 1) * gsz, :]
        m1 = jnp.max(cg, axis=0, keepdims=True)
        i1 = jnp.min(jnp.where(cg == m1, gio, float(gsz)), axis=0, keepdims=True)
        m2 = jnp.max(jnp.where(gio == i1, NEG_INF, cg), axis=0, keepdims=True)
        gscore.append(m1 + m2)
    masked = []
    for g in range(N_EXPERT_GROUPS):
        beat = jnp.zeros((1, tm), F32)
        for o in range(N_EXPERT_GROUPS):
            if o == g:
                continue
            wins = (gscore[o] >= gscore[g]) if o < g else (gscore[o] > gscore[g])
            beat = beat + jnp.where(wins, 1.0, 0.0)
        keep = beat < float(TOPK_GROUPS)
        masked.append(jnp.where(keep, choice[g * gsz:(g + 1) * gsz, :], NEG_INF))
    cur = jnp.concatenate(masked, axis=0)

    eio = lax.broadcasted_iota(jnp.int32, (n_exp, tm), 0).astype(F32)
    idxs = []
    gates = []
    candidates = cur
    for _ in range(TOP_K):
        m = jnp.max(cur, axis=0, keepdims=True)
        idx = jnp.min(jnp.where(cur == m, eio, float(n_exp)), axis=0, keepdims=True)
        hit = eio == idx
        idxs.append(idx)
        gates.append(jnp.sum(jnp.where(hit, scores, 0.0), axis=0, keepdims=True))
        cur = jnp.where(hit, NEG_INF, cur)
    onehot = jnp.where(cur != candidates, 1.0, 0.0)
    gate = jnp.concatenate(gates, axis=0)
    gate = ROUTED_SCALE * gate / (jnp.sum(gate, axis=0, keepdims=True) + 1e-20)

    si = lax.broadcasted_iota(jnp.int32, (tm, tm), 0)
    ti = lax.broadcasted_iota(jnp.int32, (tm, tm), 1)
    upper = jnp.where(si < ti, 1.0, 0.0).astype(BF16)
    before = jnp.dot(onehot.astype(BF16), upper, preferred_element_type=F32) + run_ref[:, 0:1]
    ranks = [jnp.sum(jnp.where(eio == idx, before, 0.0), axis=0, keepdims=True) for idx in idxs]

    e_ref[...] = jnp.concatenate(idxs, axis=0).astype(jnp.int32)
    g_ref[...] = gate
    r_ref[...] = jnp.concatenate(ranks, axis=0).astype(jnp.int32)
    run_ref[...] = run_ref[...] + jnp.sum(onehot, axis=1, keepdims=True)
    cnt_ref[...] = run_ref[...]


def _router(h, router_w, router_bias):
    n_tok, d = h.shape
    n_exp = router_w.shape[1]
    tm = 256
    wt = router_w.astype(F32).T
    bias = jnp.broadcast_to(router_bias.astype(F32)[:, None], (n_exp, LANES))
    tok = pl.BlockSpec((TOP_K, tm), lambda i: (0, i))
    return pl.pallas_call(
        _router_kernel,
        out_shape=(jax.ShapeDtypeStruct((TOP_K, n_tok), jnp.int32),
                   jax.ShapeDtypeStruct((TOP_K, n_tok), F32),
                   jax.ShapeDtypeStruct((TOP_K, n_tok), jnp.int32),
                   jax.ShapeDtypeStruct((n_exp, LANES), F32)),
        grid=(n_tok // tm,),
        in_specs=[pl.BlockSpec((tm, d), lambda i: (i, 0)),
                  pl.BlockSpec((n_exp, d), lambda i: (0, 0)),
                  pl.BlockSpec((n_exp, LANES), lambda i: (0, 0))],
        out_specs=(tok, tok, tok, pl.BlockSpec((n_exp, LANES), lambda i: (0, 0))),
        scratch_shapes=[pltpu.VMEM((n_exp, LANES), F32)],
        compiler_params=_cparams(("arbitrary",), 32),
        name="router",
    )(h, wt, bias)


def _dest_kernel(e_ref, r_ref, st_ref, d_ref):
    n_exp = st_ref.shape[0]
    tm = e_ref.shape[1]
    eio = lax.broadcasted_iota(jnp.int32, (n_exp, tm), 0)
    start = st_ref[:, 0:1]
    rows = [jnp.sum(jnp.where(eio == e_ref[k:k + 1, :], start, 0.0), axis=0, keepdims=True)
            for k in range(TOP_K)]
    d_ref[...] = jnp.concatenate(rows, axis=0).astype(jnp.int32) + r_ref[...]


def _dest(top_e, rank, starts):
    n_tok = top_e.shape[1]
    n_exp = starts.shape[0]
    tm = 512
    st = jnp.broadcast_to(starts.astype(F32)[:, None], (n_exp, LANES))
    tok = pl.BlockSpec((TOP_K, tm), lambda i: (0, i))
    return pl.pallas_call(
        _dest_kernel,
        out_shape=jax.ShapeDtypeStruct((TOP_K, n_tok), jnp.int32),
        grid=(n_tok // tm,),
        in_specs=[tok, tok, pl.BlockSpec((n_exp, LANES), lambda i: (0, 0))],
        out_specs=tok,
        compiler_params=_cparams(("parallel",), 32),
        name="dest",
    )(top_e, rank, st)


def _pack_bf16_pairs(val):
    half = val.shape[1] // 2
    lo = pltpu.bitcast(val[:, :half].astype(BF16).astype(F32), U32)
    hi = pltpu.bitcast(val[:, half:].astype(BF16).astype(F32), U32)
    return (lo >> 16) | (hi & jnp.uint32(0xFFFF0000))


def _unpack_bf16_pairs(words):
    lo = pltpu.bitcast(words << 16, F32)
    hi = pltpu.bitcast(words & jnp.uint32(0xFFFF0000), F32)
    return jnp.concatenate([lo, hi], axis=1)


def _to_row_tiles(dst_ref, slot, val):
    rows = val.shape[0]
    words = _pack_bf16_pairs(val)
    for j in range(ROW_TILE):
        dst_ref[slot, pl.ds(j, rows, stride=ROW_TILE), :] = words[:, j * LANES:(j + 1) * LANES]


def _row_tile_words(src_ref, idx, rows):
    return jnp.concatenate([src_ref[(*idx, pl.ds(j, rows, stride=ROW_TILE), slice(None))]
                            for j in range(ROW_TILE)], axis=1)


def _row_tile(r):
    return pl.ds(pl.multiple_of(r * ROW_TILE, ROW_TILE), ROW_TILE)


def _dispatch_kernel(dest_ref, h_ref, xs_ref, ht_ref, sem):
    tm = h_ref.shape[0]
    i = pl.program_id(0)
    cur = i % 2
    _to_row_tiles(ht_ref, cur, h_ref[...])

    def issue(t, carry):
        for k in range(TOP_K):
            pltpu.make_async_copy(ht_ref.at[cur, _row_tile(t)], xs_ref.at[_row_tile(dest_ref[t * TOP_K + k])],
                                  sem.at[cur]).start(priority=k % 2)
        return carry
    lax.fori_loop(0, tm, issue, 0, unroll=2)

    def drain(slot):
        for k in range(TOP_K):
            pltpu.make_async_copy(ht_ref.at[slot], xs_ref.at[pl.ds(0, tm * ROW_TILE)], sem.at[slot]).wait()

    @pl.when(i > 0)
    def _():
        drain(1 - cur)

    @pl.when(i == pl.num_programs(0) - 1)
    def _():
        drain(cur)


def _dispatch(h, dest, n_exp):
    n_tok, d = h.shape
    assert d == 2 * ROW_TILE * LANES
    tm = MOE_TOKEN_TILE
    n_rows = (pl.cdiv(n_tok * TOP_K, EXPERT_ROWS) + n_exp) * EXPERT_ROWS
    return pl.pallas_call(
        _dispatch_kernel,
        out_shape=jax.ShapeDtypeStruct((n_rows * ROW_TILE, LANES), U32),
        grid=(n_tok // tm,),
        in_specs=[pl.BlockSpec((TOP_K * tm,), lambda i: (i,), memory_space=pltpu.SMEM),
                  pl.BlockSpec((tm, d), lambda i: (i, 0))],
        out_specs=pl.BlockSpec(memory_space=pl.ANY),
        scratch_shapes=[pltpu.VMEM((2, tm * ROW_TILE, LANES), U32), pltpu.SemaphoreType.DMA((2,))],
        compiler_params=_cparams(("arbitrary",), 32),
        name="dispatch",
    )(dest, h)


def _experts_kernel(bstart_ref, bend_ref, cnt_ref, nblk_ref, xs_ref, wgu_hbm, wdn_hbm, ys_ref,
                    xbuf, ybuf, act_ref, wgu_f32, wdn_f32, wgu_bf, wdn_bf, xsem, ysem, wsem):
    e = pl.program_id(0)
    n_blk = nblk_ref[0]
    trows = xbuf.shape[1]
    rows = trows // ROW_TILE
    ff = wdn_bf.shape[0]
    b0 = bstart_ref[e]
    b1 = bend_ref[e]

    def block_rows(b):
        return pl.ds(pl.multiple_of(b * trows, trows), trows)

    def x_copy(b):
        slot = b % EXPERT_RING
        return pltpu.make_async_copy(xs_ref.at[block_rows(b)], xbuf.at[slot], xsem.at[slot])

    def y_copy(b):
        slot = b % EXPERT_OUT_RING
        return pltpu.make_async_copy(ybuf.at[slot], ys_ref.at[block_rows(b)], ysem.at[slot])

    @pl.when(e == 0)
    def _():
        for i in range(EXPERT_RING):
            @pl.when(i < n_blk)
            def _():
                x_copy(i).start(priority=1)

    n_exp = pl.num_programs(0)
    wslot = e % EXPERT_WEIGHT_RING

    def w_copies(x, slot):
        return (pltpu.make_async_copy(wgu_hbm.at[x], wgu_f32.at[slot], wsem.at[0, slot]),
                pltpu.make_async_copy(wdn_hbm.at[x], wdn_f32.at[slot], wsem.at[1, slot]))

    @pl.when(e == 0)
    def _():
        for i in range(EXPERT_WEIGHT_RING):
            @pl.when(i < n_exp)
            def _():
                for c in w_copies(i, i):
                    c.start()

    for c in w_copies(e, wslot):
        c.wait()
    wgu_bf[...] = wgu_f32[wslot].astype(BF16)
    wdn_bf[...] = wdn_f32[wslot].astype(BF16)

    @pl.when(e + EXPERT_WEIGHT_RING < n_exp)
    def _():
        for c in w_copies(e + EXPERT_WEIGHT_RING, wslot):
            c.start()

    def up_wait(blocks):
        for b in blocks:
            x_copy(b).wait()

    def up_compute(blocks):
        for b in blocks:
            words = _row_tile_words(xbuf, (b % EXPERT_RING,), rows)
            row = lax.broadcasted_iota(jnp.int32, (rows, 1), 0)
            words = jnp.where(row < cnt_ref[e] - (b - b0) * rows, words, jnp.uint32(0))
            xb = _unpack_bf16_pairs(words).astype(BF16)
            gu = jnp.dot(xb, wgu_bf[...], preferred_element_type=F32)
            act_ref[b - b0] = (jax.nn.silu(gu[:, :ff]) * gu[:, ff:]).astype(BF16)

    def up_start(blocks):
        for b in blocks:
            @pl.when(b + EXPERT_RING < n_blk)
            def _():
                x_copy(b + EXPERT_RING).start(priority=1)

    def down_wait(blocks):
        for b in blocks:
            @pl.when(b >= EXPERT_OUT_RING)
            def _():
                y_copy(b - EXPERT_OUT_RING).wait()

    def down_compute(blocks):
        for b in blocks:
            _to_row_tiles(ybuf, b % EXPERT_OUT_RING,
                          jnp.dot(act_ref[b - b0], wdn_bf[...], preferred_element_type=F32))

    def down_start(blocks):
        for b in blocks:
            y_copy(b).start(priority=1)
        last = blocks[-1]

        @pl.when(last == n_blk - 1)
        def _():
            for i in range(EXPERT_OUT_RING):
                @pl.when(last >= i)
                def _():
                    y_copy(last - i).wait()

    def up(blocks):
        up_wait(blocks)
        up_compute(blocks)
        up_start(blocks)

    def down(blocks):
        down_wait(blocks)
        down_compute(blocks)
        down_start(blocks)

    n_mine = b1 - b0

    def run_groups(fn):
        def body(p, carry):
            fn(tuple(b0 + EXPERT_GROUP * p + j for j in range(EXPERT_GROUP)))
            return carry
        lax.fori_loop(0, n_mine // EXPERT_GROUP, body, 0)
        size = EXPERT_GROUP // 2
        while size >= 1:
            @pl.when(n_mine & size != 0)
            def _(size=size):
                start = b0 + (n_mine // (2 * size)) * (2 * size)
                fn(tuple(start + j for j in range(size)))
            size //= 2

    @pl.when(n_mine == EXPERT_GROUP)
    def _():
        blocks = tuple(b0 + j for j in range(EXPERT_GROUP))
        up_wait(blocks)
        down_wait(blocks)
        up_compute(blocks)
        down_compute(blocks)
        up_start(blocks)
        down_start(blocks)

    @pl.when(n_mine != EXPERT_GROUP)
    def _():
        run_groups(up)
        run_groups(down)


def _expert_blocks(counts):
    blocks = (counts + EXPERT_ROWS - 1) // EXPERT_ROWS
    bend = jnp.cumsum(blocks)
    bstart = bend - blocks
    i32 = lambda a: a.astype(jnp.int32)
    return i32(bstart), i32(bend), i32(bend[-1]).reshape(1), i32(bstart * EXPERT_ROWS)


def _experts(xs, bstart, bend, counts, n_blk, w_gu, w_down, n_tok):
    n_exp, d, ff2 = w_gu.shape
    ff = w_down.shape[1]
    n_rows = xs.shape[0] // ROW_TILE
    assert n_rows % EXPERT_ROWS == 0 and d == 2 * ROW_TILE * LANES
    max_blocks = pl.cdiv(n_tok, EXPERT_ROWS)
    grid_spec = pltpu.PrefetchScalarGridSpec(
        num_scalar_prefetch=4,
        grid=(n_exp,),
        in_specs=[pl.BlockSpec(memory_space=pl.ANY), pl.BlockSpec(memory_space=pl.ANY),
                  pl.BlockSpec(memory_space=pl.ANY)],
        out_specs=pl.BlockSpec(memory_space=pl.ANY),
        scratch_shapes=[pltpu.VMEM((EXPERT_RING, EXPERT_ROWS * ROW_TILE, LANES), U32),
                        pltpu.VMEM((EXPERT_OUT_RING, EXPERT_ROWS * ROW_TILE, LANES), U32),
                        pltpu.VMEM((max_blocks, EXPERT_ROWS, ff), BF16),
                        pltpu.VMEM((EXPERT_WEIGHT_RING, d, ff2), F32),
                        pltpu.VMEM((EXPERT_WEIGHT_RING, ff, d), F32),
                        pltpu.VMEM((d, ff2), BF16), pltpu.VMEM((ff, d), BF16),
                        pltpu.SemaphoreType.DMA((EXPERT_RING,)),
                        pltpu.SemaphoreType.DMA((EXPERT_OUT_RING,)),
                        pltpu.SemaphoreType.DMA((2, EXPERT_WEIGHT_RING))],
    )
    return pl.pallas_call(
        _experts_kernel,
        out_shape=jax.ShapeDtypeStruct(xs.shape, U32),
        grid_spec=grid_spec,
        compiler_params=_cparams(("arbitrary",), 48),
        name="experts",
    )(bstart, bend, counts, n_blk, xs, w_gu, w_down)


def _combine_kernel(dest_ref, dnext_ref, gate_ref, h_ref, ys_ref, wgu_ref, wdn_ref, g_ref, b_ref, o_ref,
                    buf_ref, routed_ref, sem, *, alpha):
    tm = h_ref.shape[0]
    i = pl.program_id(0)
    cur = i % 2
    chunk = SUBLANES

    def issue(d_ref, slot, t):
        for k in range(TOP_K):
            pltpu.make_async_copy(ys_ref.at[_row_tile(d_ref[t * TOP_K + k])], buf_ref.at[slot, k, _row_tile(t)],
                                  sem.at[slot]).start(priority=k % 2)

    @pl.when(i == 0)
    def _():
        def first(t, carry):
            issue(dest_ref, 0, t)
            return carry
        lax.fori_loop(0, tm, first, 0, unroll=2)

    for k in range(TOP_K):
        pltpu.make_async_copy(ys_ref.at[pl.ds(0, tm * ROW_TILE)], buf_ref.at[cur, k], sem.at[cur]).wait()

    def weighted_sum(c):
        tok = pl.ds(pl.multiple_of(c * chunk, chunk), chunk)
        gate = gate_ref[tok, :]
        total = None
        for k in range(TOP_K):
            words = jnp.concatenate(
                [buf_ref[cur, k, pl.ds(pl.multiple_of(c * (chunk * ROW_TILE), chunk * ROW_TILE) + j, chunk,
                                       stride=ROW_TILE), :] for j in range(ROW_TILE)], axis=1)
            term = gate[:, k:k + 1] * _unpack_bf16_pairs(words)
            total = term if total is None else total + term
        routed_ref[tok, :] = total

    @pl.when(i + 1 < pl.num_programs(0))
    def _():
        def body(c, carry):
            for t in range(chunk):
                issue(dnext_ref, 1 - cur, c * chunk + t)
            weighted_sum(c)
            return carry
        lax.fori_loop(0, tm // chunk, body, 0)

    @pl.when(i + 1 == pl.num_programs(0))
    def _():
        def body(c, carry):
            weighted_sum(c)
            return carry
        lax.fori_loop(0, tm // chunk, body, 0)

    h = h_ref[...]
    ff = wdn_ref.shape[0]
    gu = jnp.dot(h.astype(BF16), wgu_ref[...], preferred_element_type=F32)
    act = (jax.nn.silu(gu[:, :ff]) * gu[:, ff:]).astype(BF16)
    acc = alpha * h + jnp.dot(act, wdn_ref[...], preferred_element_type=F32) + routed_ref[...]
    o_ref[...] = _layer_norm(acc, g_ref[...], b_ref[...])


def _combine(h, ys, dest, gate_t, shared_w_gu, shared_w_down, ln_g, ln_b, alpha):
    n_tok, d = h.shape
    tm = MOE_TOKEN_TILE
    n_tiles = n_tok // tm
    full = lambda a: pl.BlockSpec(a.shape, lambda i: (0,) * a.ndim)
    wgu = shared_w_gu.astype(BF16)
    wdn = shared_w_down.astype(BF16)
    g2 = ln_g.reshape(1, d)
    b2 = ln_b.reshape(1, d)
    return pl.pallas_call(
        functools.partial(_combine_kernel, alpha=alpha),
        out_shape=jax.ShapeDtypeStruct((n_tok, d), F32),
        grid=(n_tiles,),
        in_specs=[pl.BlockSpec((TOP_K * tm,), lambda i: (i,), memory_space=pltpu.SMEM),
                  pl.BlockSpec((TOP_K * tm,), lambda i: (jnp.minimum(i + 1, n_tiles - 1),),
                               memory_space=pltpu.SMEM),
                  pl.BlockSpec((tm, TOP_K), lambda i: (i, 0)),
                  pl.BlockSpec((tm, d), lambda i: (i, 0)),
                  pl.BlockSpec(memory_space=pl.ANY),
                  full(wgu), full(wdn), full(g2), full(b2)],
        out_specs=pl.BlockSpec((tm, d), lambda i: (i, 0)),
        scratch_shapes=[pltpu.VMEM((2, TOP_K, tm * ROW_TILE, LANES), U32), pltpu.VMEM((tm, d), F32),
                        pltpu.SemaphoreType.DMA((2,))],
        compiler_params=_cparams(("arbitrary",), 48),
        name="combine",
    )(dest, dest, gate_t, h, ys, wgu, wdn, g2, b2)


def _moe(h, router_w, router_bias, w_gu, w_down, shared_w_gu, shared_w_down, ln_g, ln_b, alpha):
    top_e, gate, rank, cnt = _router(h, router_w, router_bias)
    counts = cnt[:, 0].astype(jnp.int32)
    bstart, bend, n_blk, pad_start = _expert_blocks(counts)
    dest = _dest(top_e, rank, pad_start)
    dest_tiles = dest.T.reshape(-1)
    xs = _dispatch(h, dest_tiles, counts.shape[0])
    ys = _experts(xs, bstart, bend, counts, n_blk, w_gu, w_down, h.shape[0])
    return _combine(h, ys, dest_tiles, gate.T, shared_w_gu, shared_w_down, ln_g, ln_b, alpha)


def kernel(x, w_in, att_norm_g, lam_re, lam_im, log_step, b_re, b_im, c_re, c_im, d_skip, w_glu, b_glu,
           ssm_norm_g, w_out, ln1_g, ln1_b, router_w, router_bias, w_gu, w_down, shared_w_gu,
           shared_w_down, ln2_g, ln2_b):
    bsz, seq, d = x.shape
    depth = w_in.shape[0]
    alpha = (2 * depth) ** 0.25
    h = x.reshape(bsz * seq, d)
    for i in range(depth):
        proj = _inproj(h, w_in[i].astype(BF16), seq)
        o_att = _attention(proj, bsz, seq)
        y_ssm = _s5(proj.reshape(bsz, seq, -1), 3 * ATT_WIDTH, lam_re[i], lam_im[i], log_step[i],
                    b_re[i], b_im[i], c_re[i], c_im[i], d_skip[i])
        h = _mixout(o_att, y_ssm.reshape(bsz * seq, -1), h, w_glu[i], b_glu[i], att_norm_g[i],
                    ssm_norm_g[i], w_out[i], ln1_g[i], ln1_b[i], alpha)
        h = _moe(h, router_w[i], router_bias[i], w_gu[i], w_down[i], shared_w_gu[i], shared_w_down[i],
                 ln2_g[i], ln2_b[i], alpha)
    return h.reshape(bsz, seq, d)
```

```python
import functools

import jax
import jax.numpy as jnp
from jax import lax
from jax.experimental import pallas as pl
from jax.experimental.pallas import tpu as pltpu

F32 = jnp.float32
BF16 = jnp.bfloat16
U32 = jnp.uint32

ATT_HEADS = 8
HEAD_DIM = 64
ATT_WIDTH = ATT_HEADS * HEAD_DIM
SSM_CH = 16
SSM_STATE = 64
S5_SLAB = 64
ROPE_THETA = 500000.0
ROT_DIM = HEAD_DIM // 4
DILATIONS = (1, 4, 16)
ATT_BLOCK = 128
ATT_GROUP = 16
TOP_K = 8
N_EXPERT_GROUPS = 8
TOPK_GROUPS = 4
ROUTED_SCALE = 2.5
LN_EPS = 1e-5
RMS_EPS = 1e-6

LANES = 128
SUBLANES = 8
EXPERT_ROWS = 144
MOE_TOKEN_TILE = 512
EXPERT_RING = 16
EXPERT_OUT_RING = 16
EXPERT_GROUP = 4
EXPERT_WEIGHT_RING = 4
ROW_TILE = 4
NEG_INF = float("-inf")


def _cparams(sem, vmem_mb):
    return pltpu.CompilerParams(dimension_semantics=sem, vmem_limit_bytes=vmem_mb * 1024 * 1024)


def _inproj_kernel(x_ref, w_ref, cos_ref, sa_ref, sb_ref, o_ref, *, n_rot_cols):
    xb = x_ref[...].astype(BF16)
    cosf = cos_ref[...]
    sa = sa_ref[...]
    sb = sb_ref[...]
    width = o_ref.shape[1]
    chunk = 512
    for c in range(width // chunk):
        r = jnp.dot(xb, w_ref[:, c * chunk:(c + 1) * chunk], preferred_element_type=F32)
        if c * chunk < n_rot_cols:
            parts = []
            for s in range(chunk // LANES):
                t = r[:, s * LANES:(s + 1) * LANES]
                parts.append(t * cosf + pltpu.roll(t, LANES - ROT_DIM // 2, 1) * sa
                             + pltpu.roll(t, ROT_DIM // 2, 1) * sb)
            r = jnp.concatenate(parts, axis=1)
        o_ref[:, c * chunk:(c + 1) * chunk] = r


def _rope_lane_tables(seq):
    half = ROT_DIM // 2
    inv_freq = jnp.power(jnp.float32(ROPE_THETA), -jnp.arange(half, dtype=F32) / half)
    ang = jnp.arange(seq, dtype=F32)[:, None] * inv_freq[None, :]
    cos, sin = jnp.cos(ang), jnp.sin(ang)
    rest = HEAD_DIM - ROT_DIM
    cos_h = jnp.concatenate([cos, cos, jnp.ones((seq, rest), F32)], axis=1)
    sa_h = jnp.concatenate([-sin, jnp.zeros((seq, half + rest), F32)], axis=1)
    sb_h = jnp.concatenate([jnp.zeros((seq, half), F32), sin, jnp.zeros((seq, rest), F32)], axis=1)
    rep = LANES // HEAD_DIM
    return tuple(jnp.tile(t, (1, rep)) for t in (cos_h, sa_h, sb_h))


def _inproj(x2d, w_in_bf, seq):
    n_tok, d = x2d.shape
    width = w_in_bf.shape[1]
    tm = 512
    cosf, sa, sb = _rope_lane_tables(seq)
    tab_spec = pl.BlockSpec((tm, LANES), lambda i: (i % (seq // tm), 0))
    return pl.pallas_call(
        functools.partial(_inproj_kernel, n_rot_cols=2 * ATT_WIDTH),
        out_shape=jax.ShapeDtypeStruct((n_tok, width), F32),
        grid=(n_tok // tm,),
        in_specs=[pl.BlockSpec((tm, d), lambda i: (i, 0)),
                  pl.BlockSpec((d, width), lambda i: (0, 0)),
                  tab_spec, tab_spec, tab_spec],
        out_specs=pl.BlockSpec((tm, width), lambda i: (i, 0)),
        compiler_params=_cparams(("parallel",), 48),
        name="inproj",
    )(x2d, w_in_bf, cosf, sa, sb)


def _attn_kernel(q_ref, k_ref, v_ref, o_ref, qs_ref, ks_ref, vs_ref, tmp_ref, ob_ref, lb_ref, band_ref,
                 first_ref, *, seq):
    blk = ATT_BLOCK
    lane = lax.broadcasted_iota(jnp.int32, (1, LANES), 1)
    head0 = lane < HEAD_DIM
    scale = HEAD_DIM ** -0.5
    d1, d2 = DILATIONS[1], DILATIONS[2]
    assert DILATIONS[0] == 1 and d2 == d1 * d1
    seg = seq // d1
    sub = seg // d1

    qi = lax.broadcasted_iota(jnp.int32, (blk, 2 * blk), 0)
    kj = lax.broadcasted_iota(jnp.int32, (blk, 2 * blk), 1)
    dist = qi + blk - kj
    band_ref[...] = jnp.where((dist >= 0) & (dist <= blk), 0.0, NEG_INF)
    first_ref[...] = jnp.where((dist >= 0) & (kj >= blk), 0.0, NEG_INF)

    n_class = (1, d1, d2)
    class_len = (seq, seg, sub)
    base = [0]
    for c in range(len(DILATIONS)):
        base.append(base[c] + n_class[c] * (class_len[c] + blk))

    def kv_row0(c, g):
        return base[c] + g * (class_len[c] + blk)

    qs_ref[0] = (q_ref[...] * scale).astype(BF16)
    for a in range(d1):
        x = q_ref[pl.ds(a, seg, stride=d1), :] * scale
        tmp_ref[a * seg:(a + 1) * seg, :] = x
        qs_ref[1, a * seg:(a + 1) * seg, :] = x.astype(BF16)
    for g in range(d2):
        qs_ref[2, g * sub:(g + 1) * sub, :] = tmp_ref[pl.ds((g // d1) * seg + g % d1, sub, stride=d1),
                                                      :].astype(BF16)
    for src_ref, dst_ref in ((k_ref, ks_ref), (v_ref, vs_ref)):
        for c in range(len(DILATIONS)):
            for g in range(n_class[c]):
                dst_ref[kv_row0(c, g):kv_row0(c, g) + blk, :] = jnp.zeros((blk, LANES), BF16)
        dst_ref[kv_row0(0, 0) + blk:kv_row0(0, 0) + blk + seq, :] = src_ref[...].astype(BF16)
        for a in range(d1):
            x = src_ref[pl.ds(a, seg, stride=d1), :]
            tmp_ref[a * seg:(a + 1) * seg, :] = x
            dst_ref[kv_row0(1, a) + blk:kv_row0(1, a) + blk + seg, :] = x.astype(BF16)
        for g in range(d2):
            dst_ref[kv_row0(2, g) + blk:kv_row0(2, g) + blk + sub, :] = tmp_ref[
                pl.ds((g // d1) * seg + g % d1, sub, stride=d1), :].astype(BF16)

    def one_block(c, g, n, out_rows, bias_ref):
        q = qs_ref[c, pl.ds(aligned(g * class_len[c] + n * blk), blk), :]
        kv_rows = pl.ds(aligned(kv_row0(c, g) + n * blk), 2 * blk)
        kk = ks_ref[kv_rows, :]
        vv = vs_ref[kv_rows, :]
        outs = []
        lses = []
        for h in range(LANES // HEAD_DIM):
            hm = head0 if h == 0 else jnp.logical_not(head0)
            qh = jnp.where(hm, q, jnp.zeros_like(q))
            s = lax.dot_general(qh, kk, (((1,), (1,)), ((), ())), preferred_element_type=F32)
            s = s + bias_ref[...]
            m = jnp.max(s, axis=-1, keepdims=True)
            p = jnp.exp(s - m)
            den = jnp.sum(p, axis=-1, keepdims=True)
            outs.append(jnp.dot(p.astype(BF16), vv, preferred_element_type=F32) / den)
            lses.append(m + jnp.log(den))
        ob_ref[c, out_rows, :] = jnp.where(head0, outs[0], outs[1])
        lb_ref[c, out_rows, :] = jnp.where(head0, lses[0], lses[1])

    def run_blocks(n_blocks, fn):
        group = max(g for g in range(1, ATT_GROUP + 1) if n_blocks % g == 0)
        if n_blocks == group:
            for g in range(group):
                fn(g)
            return
        def body(it, carry):
            for g in range(group):
                fn(it * group + g)
            return carry
        lax.fori_loop(0, n_blocks // group, body, 0)

    def aligned(x):
        return x if isinstance(x, int) else pl.multiple_of(x, blk)

    one_block(0, 0, 0, pl.ds(0, blk), first_ref)
    run_blocks(seq // blk - 1,
               lambda i: one_block(0, 0, i + 1, pl.ds(aligned((i + 1) * blk), blk), band_ref))

    nb1 = seg // blk
    run_blocks(d1, lambda a: one_block(1, a, 0, pl.ds(a, blk, stride=d1), first_ref))
    def later1(i):
        a = i // (nb1 - 1)
        n = i - a * (nb1 - 1) + 1
        one_block(1, a, n, pl.ds(a + n * (d1 * blk), blk, stride=d1), band_ref)
    run_blocks(d1 * (nb1 - 1), later1)

    assert sub == blk
    def only2(g):
        a = g // d1
        one_block(2, g, 0, pl.ds(a + d1 * (g - a * d1), blk, stride=d2), first_ref)
    run_blocks(d2, only2)

    rc = 256
    def merge(i, carry):
        sl = pl.ds(pl.multiple_of(i * rc, rc), rc)
        l0 = lb_ref[0, sl, :]
        l1 = lb_ref[1, sl, :]
        l2 = lb_ref[2, sl, :]
        mx = jnp.maximum(jnp.maximum(l0, l1), l2)
        e0 = jnp.exp(l0 - mx)
        e1 = jnp.exp(l1 - mx)
        e2 = jnp.exp(l2 - mx)
        tot = e0 + e1 + e2
        o_ref[sl, :] = ((e0 / tot) * ob_ref[0, sl, :] + (e1 / tot) * ob_ref[1, sl, :]
                        + (e2 / tot) * ob_ref[2, sl, :])
        return carry
    lax.fori_loop(0, seq // rc, merge, 0)


def _attention(proj, bsz, seq):
    n_tok = proj.shape[0]
    pairs = ATT_WIDTH // LANES
    assert seq % (ATT_BLOCK * max(DILATIONS)) == 0
    kv_rows = sum(seq + d * ATT_BLOCK for d in DILATIONS)
    blk = (seq, LANES)
    return pl.pallas_call(
        functools.partial(_attn_kernel, seq=seq),
        out_shape=jax.ShapeDtypeStruct((n_tok, ATT_WIDTH), F32),
        grid=(bsz, pairs),
        in_specs=[pl.BlockSpec(blk, lambda b, h: (b, h)),
                  pl.BlockSpec(blk, lambda b, h: (b, pairs + h)),
                  pl.BlockSpec(blk, lambda b, h: (b, 2 * pairs + h))],
        out_specs=pl.BlockSpec(blk, lambda b, h: (b, h)),
        scratch_shapes=[pltpu.VMEM((len(DILATIONS), seq, LANES), BF16),
                        pltpu.VMEM((kv_rows, LANES), BF16),
                        pltpu.VMEM((kv_rows, LANES), BF16),
                        pltpu.VMEM((seq, LANES), F32),
                        pltpu.VMEM((len(DILATIONS), seq, LANES), F32),
                        pltpu.VMEM((len(DILATIONS), seq, LANES), F32),
                        pltpu.VMEM((ATT_BLOCK, 2 * ATT_BLOCK), F32),
                        pltpu.VMEM((ATT_BLOCK, 2 * ATT_BLOCK), F32)],
        compiler_params=_cparams(("parallel", "parallel"), 40),
        name="attn",
    )(proj, proj, proj)


def _s5_kernel(u_ref, bm_ref, lam_ref, cm_ref, dk_ref, o_ref, us_ref, st_ref, ys_ref, carry_ref, *, tc):
    bsz = u_ref.shape[0]
    half = st_ref.shape[1] // 2
    slab_rows = S5_SLAB * bsz
    n_slabs = tc // S5_SLAB

    @pl.when(pl.program_id(1) == 0)
    def _():
        carry_ref[...] = jnp.zeros_like(carry_ref)

    for b in range(bsz):
        us_ref[pl.ds(b, tc, stride=bsz), :] = u_ref[b]

    bm = bm_ref[0]
    cm = cm_ref[0]
    lam = lam_ref[0]
    lam_re = lam[:, :half]
    lam_im = lam[:, half:]

    def slab(s):
        return slice(s * slab_rows, (s + 1) * slab_rows)

    def project_in(s):
        st_ref[slab(s), :] = jnp.dot(us_ref[slab(s), :].astype(BF16), bm, preferred_element_type=F32)

    def project_out(s):
        ys_ref[slab(s), :] = jnp.dot(st_ref[slab(s), :].astype(BF16), cm, preferred_element_type=F32)

    def scan(s, xr, xi):
        for t in range(S5_SLAB):
            sl = slice(s * slab_rows + t * bsz, s * slab_rows + (t + 1) * bsz)
            xr, xi = (lam_re * xr - lam_im * xi + st_ref[sl, :half],
                      lam_re * xi + lam_im * xr + st_ref[sl, half:])
            st_ref[sl, :half] = xr
            st_ref[sl, half:] = xi
        return xr, xi

    xr, xi = carry_ref[:, :half], carry_ref[:, half:]
    project_in(0)
    for s in range(n_slabs):
        if s + 1 < n_slabs:
            project_in(s + 1)
        xr, xi = scan(s, xr, xi)
        if s >= 1:
            project_out(s - 1)
    project_out(n_slabs - 1)
    carry_ref[:, :half] = xr
    carry_ref[:, half:] = xi

    dk = dk_ref[...]
    for b in range(bsz):
        o_ref[b] = ys_ref[pl.ds(b, tc, stride=bsz), :] + dk * u_ref[b]


def _s5_params(lam_re, lam_im, log_step, b_re, b_im, c_re, c_im, bsz):
    groups = lam_re.shape[0]
    gpc = LANES // SSM_CH
    n_chunks = groups // gpc
    lam = lax.complex(lam_re.astype(F32), lam_im.astype(F32))
    step = jnp.exp(log_step.astype(F32))[:, None]
    lam_bar = jnp.exp(lam * step)
    bmat = lax.complex(b_re.astype(F32), b_im.astype(F32))
    b_bar = ((lam_bar - 1.0) / lam)[..., None] * bmat
    eye = jnp.eye(gpc, dtype=F32)

    def block_diag_in(t):
        t = t.reshape(n_chunks, gpc, SSM_STATE, SSM_CH)
        return jnp.einsum('ngpc,gh->ngchp', t, eye).reshape(n_chunks, gpc * SSM_CH, gpc * SSM_STATE)

    def block_diag_out(t):
        t = t.reshape(n_chunks, gpc, SSM_CH, SSM_STATE)
        return jnp.einsum('ngcp,gh->ngphc', t, eye).reshape(n_chunks, gpc * SSM_STATE, gpc * SSM_CH)

    bm = jnp.concatenate([block_diag_in(b_bar.real), block_diag_in(b_bar.imag)], axis=2).astype(BF16)
    cm = jnp.concatenate([block_diag_out(c_re.astype(F32)), block_diag_out(-c_im.astype(F32))],
                         axis=1).astype(BF16)
    lam_row = jnp.concatenate([lam_bar.real.reshape(n_chunks, gpc * SSM_STATE),
                               lam_bar.imag.reshape(n_chunks, gpc * SSM_STATE)], axis=1)
    lam_t = jnp.broadcast_to(lam_row[:, None, :], (n_chunks, bsz, 2 * gpc * SSM_STATE))
    return bm, lam_t, cm, n_chunks


def _s5(proj3, u_col0, lam_re, lam_im, log_step, b_re, b_im, c_re, c_im, d_skip):
    bsz, seq, _ = proj3.shape
    assert bsz == SUBLANES
    bm, lam_t, cm, n_chunks = _s5_params(lam_re, lam_im, log_step, b_re, b_im, c_re, c_im, bsz)
    width = n_chunks * LANES
    tc = 256
    st_cols = bm.shape[2]
    ublk0 = u_col0 // LANES
    return pl.pallas_call(
        functools.partial(_s5_kernel, tc=tc),
        out_shape=jax.ShapeDtypeStruct((bsz, seq, width), F32),
        grid=(n_chunks, seq // tc),
        in_specs=[pl.BlockSpec((bsz, tc, LANES), lambda c, t: (0, t, ublk0 + c)),
                  pl.BlockSpec((1, LANES, st_cols), lambda c, t: (c, 0, 0)),
                  pl.BlockSpec((1, bsz, st_cols), lambda c, t: (c, 0, 0)),
                  pl.BlockSpec((1, st_cols, LANES), lambda c, t: (c, 0, 0)),
                  pl.BlockSpec((1, LANES), lambda c, t: (0, c))],
        out_specs=pl.BlockSpec((bsz, tc, LANES), lambda c, t: (0, t, c)),
        scratch_shapes=[pltpu.VMEM((tc * bsz, LANES), F32),
                        pltpu.VMEM((tc * bsz, st_cols), F32),
                        pltpu.VMEM((tc * bsz, LANES), F32),
                        pltpu.VMEM((bsz, st_cols), F32)],
        compiler_params=_cparams(("arbitrary", "arbitrary"), 40),
        name="s5",
    )(proj3, bm, lam_t, cm, d_skip.reshape(1, width).astype(F32))


def _layer_norm(v, g, b):
    mu = jnp.mean(v, axis=-1, keepdims=True)
    var = jnp.mean(jnp.square(v - mu), axis=-1, keepdims=True)
    return (v - mu) * lax.rsqrt(var + LN_EPS) * g + b


def _rms_norm(v, g):
    return v * lax.rsqrt(jnp.mean(jnp.square(v), axis=-1, keepdims=True) + RMS_EPS) * g


def _mixout_kernel(att_ref, ssm_ref, x_ref, wglu_ref, bglu_ref, ag_ref, sg_ref, wout_ref, g_ref, b_ref,
                   o_ref, *, alpha):
    y = jax.nn.gelu(ssm_ref[...])
    z = jnp.dot(y.astype(BF16), wglu_ref[...], preferred_element_type=F32) + bglu_ref[...]
    o_ssm = y * jax.nn.sigmoid(z)
    a = _rms_norm(att_ref[...], ag_ref[...]).astype(BF16)
    s = _rms_norm(o_ssm, sg_ref[...]).astype(BF16)
    wa = att_ref.shape[1]
    mix = (jnp.dot(a, wout_ref[:wa, :], preferred_element_type=F32)
           + jnp.dot(s, wout_ref[wa:, :], preferred_element_type=F32))
    o_ref[...] = _layer_norm(alpha * x_ref[...] + mix, g_ref[...], b_ref[...])


def _mixout(o_att, y_ssm, x2d, w_glu, b_glu, att_g, ssm_g, w_out, ln_g, ln_b, alpha):
    n_tok, d = x2d.shape
    wa = o_att.shape[1]
    ws = y_ssm.shape[1]
    tm = 512
    row = lambda w: pl.BlockSpec((tm, w), lambda i: (i, 0))
    full = lambda a: pl.BlockSpec(a.shape, lambda i: (0,) * a.ndim)
    args = (o_att, y_ssm, x2d, w_glu.astype(BF16), b_glu.reshape(1, ws), att_g.reshape(1, wa),
            ssm_g.reshape(1, ws), w_out.astype(BF16), ln_g.reshape(1, d), ln_b.reshape(1, d))
    return pl.pallas_call(
        functools.partial(_mixout_kernel, alpha=alpha),
        out_shape=jax.ShapeDtypeStruct((n_tok, d), F32),
        grid=(n_tok // tm,),
        in_specs=[row(wa), row(ws), row(d)] + [full(a) for a in args[3:]],
        out_specs=row(d),
        compiler_params=_cparams(("parallel",), 32),
        name="mixout",
    )(*args)


def _split_bf16(v):
    hi = v.astype(BF16)
    lo = (v - hi.astype(F32)).astype(BF16)
    return hi, lo


def _router_kernel(h_ref, wt_ref, bias_ref, e_ref, g_ref, r_ref, cnt_ref, run_ref):
    tm = h_ref.shape[0]
    n_exp = wt_ref.shape[0]
    gsz = n_exp // N_EXPERT_GROUPS

    @pl.when(pl.program_id(0) == 0)
    def _():
        run_ref[...] = jnp.zeros_like(run_ref)

    w_hi, w_lo = _split_bf16(wt_ref[...])
    h_hi, h_lo = _split_bf16(h_ref[...])
    nt = (((1,), (1,)), ((), ()))
    logits = (lax.dot_general(w_hi, h_hi, nt, preferred_element_type=F32)
              + lax.dot_general(w_hi, h_lo, nt, preferred_element_type=F32)
              + lax.dot_general(w_lo, h_hi, nt, preferred_element_type=F32))
    scores = jax.nn.sigmoid(logits)
    choice = scores + bias_ref[:, 0:1]

    gio = lax.broadcasted_iota(jnp.int32, (gsz, tm), 0).astype(F32)
    gscore = []
    for g in range(N_EXPERT_GROUPS):
        cg = choice[g * gsz:(g + 1) * gsz, :]
        m1 = jnp.max(cg, axis=0, keepdims=True)
        i1 = jnp.min(jnp.where(cg == m1, gio, float(gsz)), axis=0, keepdims=True)
        m2 = jnp.max(jnp.where(gio == i1, NEG_INF, cg), axis=0, keepdims=True)
        gscore.append(m1 + m2)
    masked = []
    for g in range(N_EXPERT_GROUPS):
        beat = jnp.zeros((1, tm), F32)
        for o in range(N_EXPERT_GROUPS):
            if o == g:
                continue
            wins = (gscore[o] >= gscore[g]) if o < g else (gscore[o] > gscore[g])
            beat = beat + jnp.where(wins, 1.0, 0.0)
        keep = beat < float(TOPK_GROUPS)
        masked.append(jnp.where(keep, choice[g * gsz:(g + 1) * gsz, :], NEG_INF))
    cur = jnp.concatenate(masked, axis=0)

    eio = lax.broadcasted_iota(jnp.int32, (n_exp, tm), 0).astype(F32)
    idxs = []
    gates = []
    candidates = cur
    for _ in range(TOP_K):
        m = jnp.max(cur, axis=0, keepdims=True)
        idx = jnp.min(jnp.where(cur == m, eio, float(n_exp)), axis=0, keepdims=True)
        hit = eio == idx
        idxs.append(idx)
        gates.append(jnp.sum(jnp.where(hit, scores, 0.0), axis=0, keepdims=True))
        cur = jnp.where(hit, NEG_INF, cur)
    onehot = jnp.where(cur != candidates, 1.0, 0.0)
    gate = jnp.concatenate(gates, axis=0)
    gate = ROUTED_SCALE * gate / (jnp.sum(gate, axis=0, keepdims=True) + 1e-20)

    si = lax.broadcasted_iota(jnp.int32, (tm, tm), 0)
    ti = lax.broadcasted_iota(jnp.int32, (tm, tm), 1)
    upper = jnp.where(si < ti, 1.0, 0.0).astype(BF16)
    before = jnp.dot(onehot.astype(BF16), upper, preferred_element_type=F32) + run_ref[:, 0:1]
    ranks = [jnp.sum(jnp.where(eio == idx, before, 0.0), axis=0, keepdims=True) for idx in idxs]

    e_ref[...] = jnp.concatenate(idxs, axis=0).astype(jnp.int32)
    g_ref[...] = gate
    r_ref[...] = jnp.concatenate(ranks, axis=0).astype(jnp.int32)
    run_ref[...] = run_ref[...] + jnp.sum(onehot, axis=1, keepdims=True)
    cnt_ref[...] = run_ref[...]


def _router(h, router_w, router_bias):
    n_tok, d = h.shape
    n_exp = router_w.shape[1]
    tm = 256
    wt = router_w.astype(F32).T
    bias = jnp.broadcast_to(router_bias.astype(F32)[:, None], (n_exp, LANES))
    tok = pl.BlockSpec((TOP_K, tm), lambda i: (0, i))
    return pl.pallas_call(
        _router_kernel,
        out_shape=(jax.ShapeDtypeStruct((TOP_K, n_tok), jnp.int32),
                   jax.ShapeDtypeStruct((TOP_K, n_tok), F32),
                   jax.ShapeDtypeStruct((TOP_K, n_tok), jnp.int32),
                   jax.ShapeDtypeStruct((n_exp, LANES), F32)),
        grid=(n_tok // tm,),
        in_specs=[pl.BlockSpec((tm, d), lambda i: (i, 0)),
                  pl.BlockSpec((n_exp, d), lambda i: (0, 0)),
                  pl.BlockSpec((n_exp, LANES), lambda i: (0, 0))],
        out_specs=(tok, tok, tok, pl.BlockSpec((n_exp, LANES), lambda i: (0, 0))),
        scratch_shapes=[pltpu.VMEM((n_exp, LANES), F32)],
        compiler_params=_cparams(("arbitrary",), 32),
        name="router",
    )(h, wt, bias)


def _dest_kernel(e_ref, r_ref, st_ref, d_ref):
    n_exp = st_ref.shape[0]
    tm = e_ref.shape[1]
    eio = lax.broadcasted_iota(jnp.int32, (n_exp, tm), 0)
    start = st_ref[:, 0:1]
    rows = [jnp.sum(jnp.where(eio == e_ref[k:k + 1, :], start, 0.0), axis=0, keepdims=True)
            for k in range(TOP_K)]
    d_ref[...] = jnp.concatenate(rows, axis=0).astype(jnp.int32) + r_ref[...]


def _dest(top_e, rank, starts):
    n_tok = top_e.shape[1]
    n_exp = starts.shape[0]
    tm = 512
    st = jnp.broadcast_to(starts.astype(F32)[:, None], (n_exp, LANES))
    tok = pl.BlockSpec((TOP_K, tm), lambda i: (0, i))
    return pl.pallas_call(
        _dest_kernel,
        out_shape=jax.ShapeDtypeStruct((TOP_K, n_tok), jnp.int32),
        grid=(n_tok // tm,),
        in_specs=[tok, tok, pl.BlockSpec((n_exp, LANES), lambda i: (0, 0))],
        out_specs=tok,
        compiler_params=_cparams(("parallel",), 32),
        name="dest",
    )(top_e, rank, st)


def _pack_bf16_pairs(val):
    half = val.shape[1] // 2
    lo = pltpu.bitcast(val[:, :half].astype(BF16).astype(F32), U32)
    hi = pltpu.bitcast(val[:, half:].astype(BF16).astype(F32), U32)
    return (lo >> 16) | (hi & jnp.uint32(0xFFFF0000))


def _unpack_bf16_pairs(words):
    lo = pltpu.bitcast(words << 16, F32)
    hi = pltpu.bitcast(words & jnp.uint32(0xFFFF0000), F32)
    return jnp.concatenate([lo, hi], axis=1)


def _to_row_tiles(dst_ref, slot, val):
    rows = val.shape[0]
    words = _pack_bf16_pairs(val)
    for j in range(ROW_TILE):
        dst_ref[slot, pl.ds(j, rows, stride=ROW_TILE), :] = words[:, j * LANES:(j + 1) * LANES]


def _row_tile_words(src_ref, idx, rows):
    return jnp.concatenate([src_ref[(*idx, pl.ds(j, rows, stride=ROW_TILE), slice(None))]
                            for j in range(ROW_TILE)], axis=1)


def _row_tile(r):
    return pl.ds(pl.multiple_of(r * ROW_TILE, ROW_TILE), ROW_TILE)


def _dispatch_kernel(dest_ref, h_ref, xs_ref, ht_ref, sem):
    tm = h_ref.shape[0]
    i = pl.program_id(0)
    cur = i % 2
    _to_row_tiles(ht_ref, cur, h_ref[...])

    def issue(t, carry):
        for k in range(TOP_K):
            pltpu.make_async_copy(ht_ref.at[cur, _row_tile(t)], xs_ref.at[_row_tile(dest_ref[t * TOP_K + k])],
                                  sem.at[cur]).start(priority=k % 2)
        return carry
    lax.fori_loop(0, tm, issue, 0, unroll=2)

    def drain(slot):
        for k in range(TOP_K):
            pltpu.make_async_copy(ht_ref.at[slot], xs_ref.at[pl.ds(0, tm * ROW_TILE)], sem.at[slot]).wait()

    @pl.when(i > 0)
    def _():
        drain(1 - cur)

    @pl.when(i == pl.num_programs(0) - 1)
    def _():
        drain(cur)


def _dispatch(h, dest, n_exp):
    n_tok, d = h.shape
    assert d == 2 * ROW_TILE * LANES
    tm = MOE_TOKEN_TILE
    n_rows = (pl.cdiv(n_tok * TOP_K, EXPERT_ROWS) + n_exp) * EXPERT_ROWS
    return pl.pallas_call(
        _dispatch_kernel,
        out_shape=jax.ShapeDtypeStruct((n_rows * ROW_TILE, LANES), U32),
        grid=(n_tok // tm,),
        in_specs=[pl.BlockSpec((TOP_K * tm,), lambda i: (i,), memory_space=pltpu.SMEM),
                  pl.BlockSpec((tm, d), lambda i: (i, 0))],
        out_specs=pl.BlockSpec(memory_space=pl.ANY),
        scratch_shapes=[pltpu.VMEM((2, tm * ROW_TILE, LANES), U32), pltpu.SemaphoreType.DMA((2,))],
        compiler_params=_cparams(("arbitrary",), 32),
        name="dispatch",
    )(dest, h)


def _experts_kernel(bstart_ref, bend_ref, cnt_ref, nblk_ref, xs_ref, wgu_hbm, wdn_hbm, ys_ref,
                    xbuf, ybuf, act_ref, wgu_f32, wdn_f32, wgu_bf, wdn_bf, xsem, ysem, wsem):
    e = pl.program_id(0)
    n_blk = nblk_ref[0]
    trows = xbuf.shape[1]
    rows = trows // ROW_TILE
    ff = wdn_bf.shape[0]
    b0 = bstart_ref[e]
    b1 = bend_ref[e]

    def block_rows(b):
        return pl.ds(pl.multiple_of(b * trows, trows), trows)

    def x_copy(b):
        slot = b % EXPERT_RING
        return pltpu.make_async_copy(xs_ref.at[block_rows(b)], xbuf.at[slot], xsem.at[slot])

    def y_copy(b):
        slot = b % EXPERT_OUT_RING
        return pltpu.make_async_copy(ybuf.at[slot], ys_ref.at[block_rows(b)], ysem.at[slot])

    @pl.when(e == 0)
    def _():
        for i in range(EXPERT_RING):
            @pl.when(i < n_blk)
            def _():
                x_copy(i).start(priority=1)

    n_exp = pl.num_programs(0)
    wslot = e % EXPERT_WEIGHT_RING

    def w_copies(x, slot):
        return (pltpu.make_async_copy(wgu_hbm.at[x], wgu_f32.at[slot], wsem.at[0, slot]),
                pltpu.make_async_copy(wdn_hbm.at[x], wdn_f32.at[slot], wsem.at[1, slot]))

    @pl.when(e == 0)
    def _():
        for i in range(EXPERT_WEIGHT_RING):
            @pl.when(i < n_exp)
            def _():
                for c in w_copies(i, i):
                    c.start()

    for c in w_copies(e, wslot):
        c.wait()

    def cast_weights():
        wgu_bf[...] = wgu_f32[wslot].astype(BF16)
        wdn_bf[...] = wdn_f32[wslot].astype(BF16)

    def up_wait(blocks):
        for b in blocks:
            x_copy(b).wait()

    def up_compute(blocks):
        for b in blocks:
            words = _row_tile_words(xbuf, (b % EXPERT_RING,), rows)
            row = lax.broadcasted_iota(jnp.int32, (rows, 1), 0)
            words = jnp.where(row < cnt_ref[e] - (b - b0) * rows, words, jnp.uint32(0))
            xb = _unpack_bf16_pairs(words).astype(BF16)
            gu = jnp.dot(xb, wgu_bf[...], preferred_element_type=F32)
            act_ref[b - b0] = (jax.nn.silu(gu[:, :ff]) * gu[:, ff:]).astype(BF16)

    def up_start(blocks):
        for b in blocks:
            @pl.when(b + EXPERT_RING < n_blk)
            def _():
                x_copy(b + EXPERT_RING).start(priority=1)

    def down_wait(blocks):
        for b in blocks:
            @pl.when(b >= EXPERT_OUT_RING)
            def _():
                y_copy(b - EXPERT_OUT_RING).wait()

    def down_compute(blocks):
        for b in blocks:
            _to_row_tiles(ybuf, b % EXPERT_OUT_RING,
                          jnp.dot(act_ref[b - b0], wdn_bf[...], preferred_element_type=F32))

    def down_start(blocks):
        for b in blocks:
            y_copy(b).start(priority=1)
        last = blocks[-1]

        @pl.when(last == n_blk - 1)
        def _():
            for i in range(EXPERT_OUT_RING):
                @pl.when(last >= i)
                def _():
                    y_copy(last - i).wait()

    def up(blocks):
        up_wait(blocks)
        up_compute(blocks)
        up_start(blocks)

    def down(blocks):
        down_wait(blocks)
        down_compute(blocks)
        down_start(blocks)

    n_mine = b1 - b0

    def run_groups(fn):
        def body(p, carry):
            fn(tuple(b0 + EXPERT_GROUP * p + j for j in range(EXPERT_GROUP)))
            return carry
        lax.fori_loop(0, n_mine // EXPERT_GROUP, body, 0)
        size = EXPERT_GROUP // 2
        while size >= 1:
            @pl.when(n_mine & size != 0)
            def _(size=size):
                start = b0 + (n_mine // (2 * size)) * (2 * size)
                fn(tuple(start + j for j in range(size)))
            size //= 2

    @pl.when(n_mine == EXPERT_GROUP)
    def _():
        blocks = tuple(b0 + j for j in range(EXPERT_GROUP))
        up_wait(blocks)
        down_wait(blocks)
        cast_weights()
        up_compute(blocks)
        down_compute(blocks)
        up_start(blocks)
        down_start(blocks)

    @pl.when(n_mine != EXPERT_GROUP)
    def _():
        cast_weights()
        run_groups(up)
        run_groups(down)

    @pl.when(e + EXPERT_WEIGHT_RING < n_exp)
    def _():
        for c in w_copies(e + EXPERT_WEIGHT_RING, wslot):
            c.start()


def _expert_blocks(counts):
    blocks = (counts + EXPERT_ROWS - 1) // EXPERT_ROWS
    bend = jnp.cumsum(blocks)
    bstart = bend - blocks
    i32 = lambda a: a.astype(jnp.int32)
    return i32(bstart), i32(bend), i32(bend[-1]).reshape(1), i32(bstart * EXPERT_ROWS)


def _experts(xs, bstart, bend, counts, n_blk, w_gu, w_down, n_tok):
    n_exp, d, ff2 = w_gu.shape
    ff = w_down.shape[1]
    n_rows = xs.shape[0] // ROW_TILE
    assert n_rows % EXPERT_ROWS == 0 and d == 2 * ROW_TILE * LANES
    max_blocks = pl.cdiv(n_tok, EXPERT_ROWS)
    grid_spec = pltpu.PrefetchScalarGridSpec(
        num_scalar_prefetch=4,
        grid=(n_exp,),
        in_specs=[pl.BlockSpec(memory_space=pl.ANY), pl.BlockSpec(memory_space=pl.ANY),
                  pl.BlockSpec(memory_space=pl.ANY)],
        out_specs=pl.BlockSpec(memory_space=pl.ANY),
        scratch_shapes=[pltpu.VMEM((EXPERT_RING, EXPERT_ROWS * ROW_TILE, LANES), U32),
                        pltpu.VMEM((EXPERT_OUT_RING, EXPERT_ROWS * ROW_TILE, LANES), U32),
                        pltpu.VMEM((max_blocks, EXPERT_ROWS, ff), BF16),
                        pltpu.VMEM((EXPERT_WEIGHT_RING, d, ff2), F32),
                        pltpu.VMEM((EXPERT_WEIGHT_RING, ff, d), F32),
                        pltpu.VMEM((d, ff2), BF16), pltpu.VMEM((ff, d), BF16),
                        pltpu.SemaphoreType.DMA((EXPERT_RING,)),
                        pltpu.SemaphoreType.DMA((EXPERT_OUT_RING,)),
                        pltpu.SemaphoreType.DMA((2, EXPERT_WEIGHT_RING))],
    )
    return pl.pallas_call(
        _experts_kernel,
        out_shape=jax.ShapeDtypeStruct(xs.shape, U32),
        grid_spec=grid_spec,
        compiler_params=_cparams(("arbitrary",), 48),
        name="experts",
    )(bstart, bend, counts, n_blk, xs, w_gu, w_down)


def _combine_kernel(dest_ref, dnext_ref, gate_ref, h_ref, ys_ref, wgu_ref, wdn_ref, g_ref, b_ref, o_ref,
                    buf_ref, routed_ref, sem, *, alpha):
    tm = h_ref.shape[0]
    i = pl.program_id(0)
    cur = i % 2
    chunk = SUBLANES

    def issue(d_ref, slot, t):
        for k in range(TOP_K):
            pltpu.make_async_copy(ys_ref.at[_row_tile(d_ref[t * TOP_K + k])], buf_ref.at[slot, k, _row_tile(t)],
                                  sem.at[slot]).start(priority=k % 2)

    @pl.when(i == 0)
    def _():
        def first(t, carry):
            issue(dest_ref, 0, t)
            return carry
        lax.fori_loop(0, tm, first, 0, unroll=2)

    for k in range(TOP_K):
        pltpu.make_async_copy(ys_ref.at[pl.ds(0, tm * ROW_TILE)], buf_ref.at[cur, k], sem.at[cur]).wait()

    def weighted_sum(c):
        tok = pl.ds(pl.multiple_of(c * chunk, chunk), chunk)
        gate = gate_ref[tok, :]
        total = None
        for k in range(TOP_K):
            words = jnp.concatenate(
                [buf_ref[cur, k, pl.ds(pl.multiple_of(c * (chunk * ROW_TILE), chunk * ROW_TILE) + j, chunk,
                                       stride=ROW_TILE), :] for j in range(ROW_TILE)], axis=1)
            term = gate[:, k:k + 1] * _unpack_bf16_pairs(words)
            total = term if total is None else total + term
        routed_ref[tok, :] = total

    @pl.when(i + 1 < pl.num_programs(0))
    def _():
        def body(c, carry):
            for t in range(chunk):
                issue(dnext_ref, 1 - cur, c * chunk + t)
            weighted_sum(c)
            return carry
        lax.fori_loop(0, tm // chunk, body, 0)

    @pl.when(i + 1 == pl.num_programs(0))
    def _():
        def body(c, carry):
            weighted_sum(c)
            return carry
        lax.fori_loop(0, tm // chunk, body, 0)

    h = h_ref[...]
    ff = wdn_ref.shape[0]
    gu = jnp.dot(h.astype(BF16), wgu_ref[...], preferred_element_type=F32)
    act = (jax.nn.silu(gu[:, :ff]) * gu[:, ff:]).astype(BF16)
    acc = alpha * h + jnp.dot(act, wdn_ref[...], preferred_element_type=F32) + routed_ref[...]
    o_ref[...] = _layer_norm(acc, g_ref[...], b_ref[...])


def _combine(h, ys, dest, gate_t, shared_w_gu, shared_w_down, ln_g, ln_b, alpha):
    n_tok, d = h.shape
    tm = MOE_TOKEN_TILE
    n_tiles = n_tok // tm
    full = lambda a: pl.BlockSpec(a.shape, lambda i: (0,) * a.ndim)
    wgu = shared_w_gu.astype(BF16)
    wdn = shared_w_down.astype(BF16)
    g2 = ln_g.reshape(1, d)
    b2 = ln_b.reshape(1, d)
    return pl.pallas_call(
        functools.partial(_combine_kernel, alpha=alpha),
        out_shape=jax.ShapeDtypeStruct((n_tok, d), F32),
        grid=(n_tiles,),
        in_specs=[pl.BlockSpec((TOP_K * tm,), lambda i: (i,), memory_space=pltpu.SMEM),
                  pl.BlockSpec((TOP_K * tm,), lambda i: (jnp.minimum(i + 1, n_tiles - 1),),
                               memory_space=pltpu.SMEM),
                  pl.BlockSpec((tm, TOP_K), lambda i: (i, 0)),
                  pl.BlockSpec((tm, d), lambda i: (i, 0)),
                  pl.BlockSpec(memory_space=pl.ANY),
                  full(wgu), full(wdn), full(g2), full(b2)],
        out_specs=pl.BlockSpec((tm, d), lambda i: (i, 0)),
        scratch_shapes=[pltpu.VMEM((2, TOP_K, tm * ROW_TILE, LANES), U32), pltpu.VMEM((tm, d), F32),
                        pltpu.SemaphoreType.DMA((2,))],
        compiler_params=_cparams(("arbitrary",), 48),
        name="combine",
    )(dest, dest, gate_t, h, ys, wgu, wdn, g2, b2)


def _moe(h, router_w, router_bias, w_gu, w_down, shared_w_gu, shared_w_down, ln_g, ln_b, alpha):
    top_e, gate, rank, cnt = _router(h, router_w, router_bias)
    counts = cnt[:, 0].astype(jnp.int32)
    bstart, bend, n_blk, pad_start = _expert_blocks(counts)
    dest = _dest(top_e, rank, pad_start)
    dest_tiles = dest.T.reshape(-1)
    xs = _dispatch(h, dest_tiles, counts.shape[0])
    ys = _experts(xs, bstart, bend, counts, n_blk, w_gu, w_down, h.shape[0])
    return _combine(h, ys, dest_tiles, gate.T, shared_w_gu, shared_w_down, ln_g, ln_b, alpha)


def kernel(x, w_in, att_norm_g, lam_re, lam_im, log_step, b_re, b_im, c_re, c_im, d_skip, w_glu, b_glu,
           ssm_norm_g, w_out, ln1_g, ln1_b, router_w, router_bias, w_gu, w_down, shared_w_gu,
           shared_w_down, ln2_g, ln2_b):
    bsz, seq, d = x.shape
    depth = w_in.shape[0]
    alpha = (2 * depth) ** 0.25
    h = x.reshape(bsz * seq, d)
    for i in range(depth):
        proj = _inproj(h, w_in[i].astype(BF16), seq)
        o_att = _attention(proj, bsz, seq)
        y_ssm = _s5(proj.reshape(bsz, seq, -1), 3 * ATT_WIDTH, lam_re[i], lam_im[i], log_step[i],
                    b_re[i], b_im[i], c_re[i], c_im[i], d_skip[i])
        h = _mixout(o_att, y_ssm.reshape(bsz * seq, -1), h, w_glu[i], b_glu[i], att_norm_g[i],
                    ssm_norm_g[i], w_out[i], ln1_g[i], ln1_b[i], alpha)
        h = _moe(h, router_w[i], router_bias[i], w_gu[i], w_down[i], shared_w_gu[i], shared_w_down[i],
                 ln2_g[i], ln2_b[i], alpha)
    return h.reshape(bsz, seq, d)
```

```python
import functools

import jax
import jax.numpy as jnp
from jax import lax
from jax.experimental import pallas as pl
from jax.experimental.pallas import tpu as pltpu

F32 = jnp.float32
BF16 = jnp.bfloat16
U32 = jnp.uint32

ATT_HEADS = 8
HEAD_DIM = 64
ATT_WIDTH = ATT_HEADS * HEAD_DIM
SSM_CH = 16
SSM_STATE = 64
S5_SLAB = 64
ROPE_THETA = 500000.0
ROT_DIM = HEAD_DIM // 4
DILATIONS = (1, 4, 16)
ATT_BLOCK = 128
ATT_GROUP = 16
TOP_K = 8
N_EXPERT_GROUPS = 8
TOPK_GROUPS = 4
ROUTED_SCALE = 2.5
LN_EPS = 1e-5
RMS_EPS = 1e-6

LANES = 128
SUBLANES = 8
EXPERT_ROWS = 144
MOE_TOKEN_TILE = 512
EXPERT_RING = 16
EXPERT_OUT_RING = 16
EXPERT_GROUP = 4
EXPERT_WEIGHT_RING = 4
ROW_TILE = 4
NEG_INF = float("-inf")


def _cparams(sem, vmem_mb):
    return pltpu.CompilerParams(dimension_semantics=sem, vmem_limit_bytes=vmem_mb * 1024 * 1024)


def _inproj_kernel(x_ref, w_ref, cos_ref, sa_ref, sb_ref, o_ref, *, n_rot_cols):
    xb = x_ref[...].astype(BF16)
    cosf = cos_ref[...]
    sa = sa_ref[...]
    sb = sb_ref[...]
    width = o_ref.shape[1]
    chunk = 512
    for c in range(width // chunk):
        r = jnp.dot(xb, w_ref[:, c * chunk:(c + 1) * chunk], preferred_element_type=F32)
        if c * chunk < n_rot_cols:
            parts = []
            for s in range(chunk // LANES):
                t = r[:, s * LANES:(s + 1) * LANES]
                parts.append(t * cosf + pltpu.roll(t, LANES - ROT_DIM // 2, 1) * sa
                             + pltpu.roll(t, ROT_DIM // 2, 1) * sb)
            r = jnp.concatenate(parts, axis=1)
        o_ref[:, c * chunk:(c + 1) * chunk] = r


def _rope_lane_tables(seq):
    half = ROT_DIM // 2
    inv_freq = jnp.power(jnp.float32(ROPE_THETA), -jnp.arange(half, dtype=F32) / half)
    ang = jnp.arange(seq, dtype=F32)[:, None] * inv_freq[None, :]
    cos, sin = jnp.cos(ang), jnp.sin(ang)
    rest = HEAD_DIM - ROT_DIM
    cos_h = jnp.concatenate([cos, cos, jnp.ones((seq, rest), F32)], axis=1)
    sa_h = jnp.concatenate([-sin, jnp.zeros((seq, half + rest), F32)], axis=1)
    sb_h = jnp.concatenate([jnp.zeros((seq, half), F32), sin, jnp.zeros((seq, rest), F32)], axis=1)
    rep = LANES // HEAD_DIM
    return tuple(jnp.tile(t, (1, rep)) for t in (cos_h, sa_h, sb_h))


def _inproj(x2d, w_in_bf, seq):
    n_tok, d = x2d.shape
    width = w_in_bf.shape[1]
    tm = 512
    cosf, sa, sb = _rope_lane_tables(seq)
    tab_spec = pl.BlockSpec((tm, LANES), lambda i: (i % (seq // tm), 0))
    return pl.pallas_call(
        functools.partial(_inproj_kernel, n_rot_cols=2 * ATT_WIDTH),
        out_shape=jax.ShapeDtypeStruct((n_tok, width), F32),
        grid=(n_tok // tm,),
        in_specs=[pl.BlockSpec((tm, d), lambda i: (i, 0)),
                  pl.BlockSpec((d, width), lambda i: (0, 0)),
                  tab_spec, tab_spec, tab_spec],
        out_specs=pl.BlockSpec((tm, width), lambda i: (i, 0)),
        compiler_params=_cparams(("parallel",), 48),
        name="inproj",
    )(x2d, w_in_bf, cosf, sa, sb)


def _attn_kernel(q_ref, k_ref, v_ref, o_ref, qs_ref, ks_ref, vs_ref, tmp_ref, ob_ref, lb_ref, band_ref,
                 first_ref, *, seq):
    blk = ATT_BLOCK
    lane = lax.broadcasted_iota(jnp.int32, (1, LANES), 1)
    head0 = lane < HEAD_DIM
    scale = HEAD_DIM ** -0.5
    d1, d2 = DILATIONS[1], DILATIONS[2]
    assert DILATIONS[0] == 1 and d2 == d1 * d1
    seg = seq // d1
    sub = seg // d1

    qi = lax.broadcasted_iota(jnp.int32, (blk, 2 * blk), 0)
    kj = lax.broadcasted_iota(jnp.int32, (blk, 2 * blk), 1)
    dist = qi + blk - kj
    band_ref[...] = jnp.where((dist >= 0) & (dist <= blk), 0.0, NEG_INF)
    first_ref[...] = jnp.where((dist >= 0) & (kj >= blk), 0.0, NEG_INF)

    n_class = (1, d1, d2)
    class_len = (seq, seg, sub)
    base = [0]
    for c in range(len(DILATIONS)):
        base.append(base[c] + n_class[c] * (class_len[c] + blk))

    def kv_row0(c, g):
        return base[c] + g * (class_len[c] + blk)

    qs_ref[0] = (q_ref[...] * scale).astype(BF16)
    for a in range(d1):
        x = q_ref[pl.ds(a, seg, stride=d1), :] * scale
        tmp_ref[a * seg:(a + 1) * seg, :] = x
        qs_ref[1, a * seg:(a + 1) * seg, :] = x.astype(BF16)
    for g in range(d2):
        qs_ref[2, g * sub:(g + 1) * sub, :] = tmp_ref[pl.ds((g // d1) * seg + g % d1, sub, stride=d1),
                                                      :].astype(BF16)
    for src_ref, dst_ref in ((k_ref, ks_ref), (v_ref, vs_ref)):
        for c in range(len(DILATIONS)):
            for g in range(n_class[c]):
                dst_ref[kv_row0(c, g):kv_row0(c, g) + blk, :] = jnp.zeros((blk, LANES), BF16)
        dst_ref[kv_row0(0, 0) + blk:kv_row0(0, 0) + blk + seq, :] = src_ref[...].astype(BF16)
        for a in range(d1):
            x = src_ref[pl.ds(a, seg, stride=d1), :]
            tmp_ref[a * seg:(a + 1) * seg, :] = x
            dst_ref[kv_row0(1, a) + blk:kv_row0(1, a) + blk + seg, :] = x.astype(BF16)
        for g in range(d2):
            dst_ref[kv_row0(2, g) + blk:kv_row0(2, g) + blk + sub, :] = tmp_ref[
                pl.ds((g // d1) * seg + g % d1, sub, stride=d1), :].astype(BF16)

    def one_block(c, g, n, out_rows, bias_ref):
        q = qs_ref[c, pl.ds(aligned(g * class_len[c] + n * blk), blk), :]
        kv_rows = pl.ds(aligned(kv_row0(c, g) + n * blk), 2 * blk)
        kk = ks_ref[kv_rows, :]
        vv = vs_ref[kv_rows, :]
        outs = []
        lses = []
        for h in range(LANES // HEAD_DIM):
            hm = head0 if h == 0 else jnp.logical_not(head0)
            qh = jnp.where(hm, q, jnp.zeros_like(q))
            s = lax.dot_general(qh, kk, (((1,), (1,)), ((), ())), preferred_element_type=F32)
            s = s + bias_ref[...]
            m = jnp.max(s, axis=-1, keepdims=True)
            p = jnp.exp(s - m)
            den = jnp.sum(p, axis=-1, keepdims=True)
            outs.append(jnp.dot(p.astype(BF16), vv, preferred_element_type=F32) / den)
            lses.append(m + jnp.log(den))
        ob_ref[c, out_rows, :] = jnp.where(head0, outs[0], outs[1])
        lb_ref[c, out_rows, :] = jnp.where(head0, lses[0], lses[1])

    def run_blocks(n_blocks, fn):
        group = max(g for g in range(1, ATT_GROUP + 1) if n_blocks % g == 0)
        if n_blocks == group:
            for g in range(group):
                fn(g)
            return
        def body(it, carry):
            for g in range(group):
                fn(it * group + g)
            return carry
        lax.fori_loop(0, n_blocks // group, body, 0)

    def aligned(x):
        return x if isinstance(x, int) else pl.multiple_of(x, blk)

    one_block(0, 0, 0, pl.ds(0, blk), first_ref)
    run_blocks(seq // blk - 1,
               lambda i: one_block(0, 0, i + 1, pl.ds(aligned((i + 1) * blk), blk), band_ref))

    nb1 = seg // blk
    run_blocks(d1, lambda a: one_block(1, a, 0, pl.ds(a, blk, stride=d1), first_ref))
    def later1(i):
        a = i // (nb1 - 1)
        n = i - a * (nb1 - 1) + 1
        one_block(1, a, n, pl.ds(a + n * (d1 * blk), blk, stride=d1), band_ref)
    run_blocks(d1 * (nb1 - 1), later1)

    assert sub == blk
    def only2(g):
        a = g // d1
        one_block(2, g, 0, pl.ds(a + d1 * (g - a * d1), blk, stride=d2), first_ref)
    run_blocks(d2, only2)

    rc = 256
    def merge(i, carry):
        sl = pl.ds(pl.multiple_of(i * rc, rc), rc)
        l0 = lb_ref[0, sl, :]
        l1 = lb_ref[1, sl, :]
        l2 = lb_ref[2, sl, :]
        mx = jnp.maximum(jnp.maximum(l0, l1), l2)
        e0 = jnp.exp(l0 - mx)
        e1 = jnp.exp(l1 - mx)
        e2 = jnp.exp(l2 - mx)
        o_ref[sl, :] = ((e0 * ob_ref[0, sl, :] + e1 * ob_ref[1, sl, :] + e2 * ob_ref[2, sl, :])
                        / (e0 + e1 + e2))
        return carry
    lax.fori_loop(0, seq // rc, merge, 0)


def _attention(proj, bsz, seq):
    n_tok = proj.shape[0]
    pairs = ATT_WIDTH // LANES
    assert seq % (ATT_BLOCK * max(DILATIONS)) == 0
    kv_rows = sum(seq + d * ATT_BLOCK for d in DILATIONS)
    blk = (seq, LANES)
    return pl.pallas_call(
        functools.partial(_attn_kernel, seq=seq),
        out_shape=jax.ShapeDtypeStruct((n_tok, ATT_WIDTH), F32),
        grid=(bsz, pairs),
        in_specs=[pl.BlockSpec(blk, lambda b, h: (b, h)),
                  pl.BlockSpec(blk, lambda b, h: (b, pairs + h)),
                  pl.BlockSpec(blk, lambda b, h: (b, 2 * pairs + h))],
        out_specs=pl.BlockSpec(blk, lambda b, h: (b, h)),
        scratch_shapes=[pltpu.VMEM((len(DILATIONS), seq, LANES), BF16),
                        pltpu.VMEM((kv_rows, LANES), BF16),
                        pltpu.VMEM((kv_rows, LANES), BF16),
                        pltpu.VMEM((seq, LANES), F32),
                        pltpu.VMEM((len(DILATIONS), seq, LANES), F32),
                        pltpu.VMEM((len(DILATIONS), seq, LANES), F32),
                        pltpu.VMEM((ATT_BLOCK, 2 * ATT_BLOCK), F32),
                        pltpu.VMEM((ATT_BLOCK, 2 * ATT_BLOCK), F32)],
        compiler_params=_cparams(("parallel", "parallel"), 40),
        name="attn",
    )(proj, proj, proj)


def _s5_kernel(u_ref, bm_ref, lam_ref, cm_ref, dk_ref, o_ref, us_ref, st_ref, ys_ref, carry_ref, *, tc):
    bsz = u_ref.shape[0]
    half = st_ref.shape[1] // 2
    slab_rows = S5_SLAB * bsz
    n_slabs = tc // S5_SLAB

    @pl.when(pl.program_id(1) == 0)
    def _():
        carry_ref[...] = jnp.zeros_like(carry_ref)

    for b in range(bsz):
        us_ref[pl.ds(b, tc, stride=bsz), :] = u_ref[b]

    bm = bm_ref[0]
    cm = cm_ref[0]
    lam = lam_ref[0]
    lam_re = lam[:, :half]
    lam_im = lam[:, half:]

    def slab(s):
        return slice(s * slab_rows, (s + 1) * slab_rows)

    def project_in(s):
        st_ref[slab(s), :] = jnp.dot(us_ref[slab(s), :].astype(BF16), bm, preferred_element_type=F32)

    def project_out(s):
        ys_ref[slab(s), :] = jnp.dot(st_ref[slab(s), :].astype(BF16), cm, preferred_element_type=F32)

    def scan(s, xr, xi):
        for t in range(S5_SLAB):
            sl = slice(s * slab_rows + t * bsz, s * slab_rows + (t + 1) * bsz)
            xr, xi = (lam_re * xr - lam_im * xi + st_ref[sl, :half],
                      lam_re * xi + lam_im * xr + st_ref[sl, half:])
            st_ref[sl, :half] = xr
            st_ref[sl, half:] = xi
        return xr, xi

    xr, xi = carry_ref[:, :half], carry_ref[:, half:]
    project_in(0)
    for s in range(n_slabs):
        if s + 1 < n_slabs:
            project_in(s + 1)
        xr, xi = scan(s, xr, xi)
        if s >= 1:
            project_out(s - 1)
    project_out(n_slabs - 1)
    carry_ref[:, :half] = xr
    carry_ref[:, half:] = xi

    dk = dk_ref[...]
    for b in range(bsz):
        o_ref[b] = ys_ref[pl.ds(b, tc, stride=bsz), :] + dk * u_ref[b]


def _s5_params(lam_re, lam_im, log_step, b_re, b_im, c_re, c_im, bsz):
    groups = lam_re.shape[0]
    gpc = LANES // SSM_CH
    n_chunks = groups // gpc
    lam = lax.complex(lam_re.astype(F32), lam_im.astype(F32))
    step = jnp.exp(log_step.astype(F32))[:, None]
    lam_bar = jnp.exp(lam * step)
    bmat = lax.complex(b_re.astype(F32), b_im.astype(F32))
    b_bar = ((lam_bar - 1.0) / lam)[..., None] * bmat
    eye = jnp.eye(gpc, dtype=F32)

    def block_diag_in(t):
        t = t.reshape(n_chunks, gpc, SSM_STATE, SSM_CH)
        return jnp.einsum('ngpc,gh->ngchp', t, eye).reshape(n_chunks, gpc * SSM_CH, gpc * SSM_STATE)

    def block_diag_out(t):
        t = t.reshape(n_chunks, gpc, SSM_CH, SSM_STATE)
        return jnp.einsum('ngcp,gh->ngphc', t, eye).reshape(n_chunks, gpc * SSM_STATE, gpc * SSM_CH)

    bm = jnp.concatenate([block_diag_in(b_bar.real), block_diag_in(b_bar.imag)], axis=2).astype(BF16)
    cm = jnp.concatenate([block_diag_out(c_re.astype(F32)), block_diag_out(-c_im.astype(F32))],
                         axis=1).astype(BF16)
    lam_row = jnp.concatenate([lam_bar.real.reshape(n_chunks, gpc * SSM_STATE),
                               lam_bar.imag.reshape(n_chunks, gpc * SSM_STATE)], axis=1)
    lam_t = jnp.broadcast_to(lam_row[:, None, :], (n_chunks, bsz, 2 * gpc * SSM_STATE))
    return bm, lam_t, cm, n_chunks


def _s5(proj3, u_col0, lam_re, lam_im, log_step, b_re, b_im, c_re, c_im, d_skip):
    bsz, seq, _ = proj3.shape
    assert bsz == SUBLANES
    bm, lam_t, cm, n_chunks = _s5_params(lam_re, lam_im, log_step, b_re, b_im, c_re, c_im, bsz)
    width = n_chunks * LANES
    tc = 256
    st_cols = bm.shape[2]
    ublk0 = u_col0 // LANES
    return pl.pallas_call(
        functools.partial(_s5_kernel, tc=tc),
        out_shape=jax.ShapeDtypeStruct((bsz, seq, width), F32),
        grid=(n_chunks, seq // tc),
        in_specs=[pl.BlockSpec((bsz, tc, LANES), lambda c, t: (0, t, ublk0 + c)),
                  pl.BlockSpec((1, LANES, st_cols), lambda c, t: (c, 0, 0)),
                  pl.BlockSpec((1, bsz, st_cols), lambda c, t: (c, 0, 0)),
                  pl.BlockSpec((1, st_cols, LANES), lambda c, t: (c, 0, 0)),
                  pl.BlockSpec((1, LANES), lambda c, t: (0, c))],
        out_specs=pl.BlockSpec((bsz, tc, LANES), lambda c, t: (0, t, c)),
        scratch_shapes=[pltpu.VMEM((tc * bsz, LANES), F32),
                        pltpu.VMEM((tc * bsz, st_cols), F32),
                        pltpu.VMEM((tc * bsz, LANES), F32),
                        pltpu.VMEM((bsz, st_cols), F32)],
        compiler_params=_cparams(("arbitrary", "arbitrary"), 40),
        name="s5",
    )(proj3, bm, lam_t, cm, d_skip.reshape(1, width).astype(F32))


def _layer_norm(v, g, b):
    mu = jnp.mean(v, axis=-1, keepdims=True)
    var = jnp.mean(jnp.square(v - mu), axis=-1, keepdims=True)
    return (v - mu) * lax.rsqrt(var + LN_EPS) * g + b


def _rms_norm(v, g):
    return v * lax.rsqrt(jnp.mean(jnp.square(v), axis=-1, keepdims=True) + RMS_EPS) * g


def _mixout_kernel(att_ref, ssm_ref, x_ref, wglu_ref, bglu_ref, ag_ref, sg_ref, wout_ref, g_ref, b_ref,
                   o_ref, *, alpha):
    y = jax.nn.gelu(ssm_ref[...])
    z = jnp.dot(y.astype(BF16), wglu_ref[...], preferred_element_type=F32) + bglu_ref[...]
    o_ssm = y * jax.nn.sigmoid(z)
    a = _rms_norm(att_ref[...], ag_ref[...]).astype(BF16)
    s = _rms_norm(o_ssm, sg_ref[...]).astype(BF16)
    wa = att_ref.shape[1]
    mix = (jnp.dot(a, wout_ref[:wa, :], preferred_element_type=F32)
           + jnp.dot(s, wout_ref[wa:, :], preferred_element_type=F32))
    o_ref[...] = _layer_norm(alpha * x_ref[...] + mix, g_ref[...], b_ref[...])


def _mixout(o_att, y_ssm, x2d, w_glu, b_glu, att_g, ssm_g, w_out, ln_g, ln_b, alpha):
    n_tok, d = x2d.shape
    wa = o_att.shape[1]
    ws = y_ssm.shape[1]
    tm = 512
    row = lambda w: pl.BlockSpec((tm, w), lambda i: (i, 0))
    full = lambda a: pl.BlockSpec(a.shape, lambda i: (0,) * a.ndim)
    args = (o_att, y_ssm, x2d, w_glu.astype(BF16), b_glu.reshape(1, ws), att_g.reshape(1, wa),
            ssm_g.reshape(1, ws), w_out.astype(BF16), ln_g.reshape(1, d), ln_b.reshape(1, d))
    return pl.pallas_call(
        functools.partial(_mixout_kernel, alpha=alpha),
        out_shape=jax.ShapeDtypeStruct((n_tok, d), F32),
        grid=(n_tok // tm,),
        in_specs=[row(wa), row(ws), row(d)] + [full(a) for a in args[3:]],
        out_specs=row(d),
        compiler_params=_cparams(("parallel",), 32),
        name="mixout",
    )(*args)


def _split_bf16(v):
    hi = v.astype(BF16)
    lo = (v - hi.astype(F32)).astype(BF16)
    return hi, lo


def _router_kernel(h_ref, wt_ref, bias_ref, e_ref, g_ref, r_ref, cnt_ref, run_ref):
    tm = h_ref.shape[0]
    n_exp = wt_ref.shape[0]
    gsz = n_exp // N_EXPERT_GROUPS

    @pl.when(pl.program_id(0) == 0)
    def _():
        run_ref[...] = jnp.zeros_like(run_ref)

    w_hi, w_lo = _split_bf16(wt_ref[...])
    h_hi, h_lo = _split_bf16(h_ref[...])
    nt = (((1,), (1,)), ((), ()))
    logits = (lax.dot_general(w_hi, h_hi, nt, preferred_element_type=F32)
              + lax.dot_general(w_hi, h_lo, nt, preferred_element_type=F32)
              + lax.dot_general(w_lo, h_hi, nt, preferred_element_type=F32))
    scores = jax.nn.sigmoid(logits)
    choice = scores + bias_ref[:, 0:1]

    gio = lax.broadcasted_iota(jnp.int32, (gsz, tm), 0).astype(F32)
    gscore = []
    for g in range(N_EXPERT_GROUPS):
        cg = choice[g * gsz:(g + 1) * gsz, :]
        m1 = jnp.max(cg, axis=0, keepdims=True)
        i1 = jnp.min(jnp.where(cg == m1, gio, float(gsz)), axis=0, keepdims=True)
        m2 = jnp.max(jnp.where(gio == i1, NEG_INF, cg), axis=0, keepdims=True)
        gscore.append(m1 + m2)
    masked = []
    for g in range(N_EXPERT_GROUPS):
        beat = jnp.zeros((1, tm), F32)
        for o in range(N_EXPERT_GROUPS):
            if o == g:
                continue
            wins = (gscore[o] >= gscore[g]) if o < g else (gscore[o] > gscore[g])
            beat = beat + jnp.where(wins, 1.0, 0.0)
        keep = beat < float(TOPK_GROUPS)
        masked.append(jnp.where(keep, choice[g * gsz:(g + 1) * gsz, :], NEG_INF))
    cur = jnp.concatenate(masked, axis=0)

    eio = lax.broadcasted_iota(jnp.int32, (n_exp, tm), 0).astype(F32)
    idxs = []
    gates = []
    candidates = cur
    for _ in range(TOP_K):
        m = jnp.max(cur, axis=0, keepdims=True)
        idx = jnp.min(jnp.where(cur == m, eio, float(n_exp)), axis=0, keepdims=True)
        hit = eio == idx
        idxs.append(idx)
        gates.append(jnp.sum(jnp.where(hit, scores, 0.0), axis=0, keepdims=True))
        cur = jnp.where(hit, NEG_INF, cur)
    onehot = jnp.where(cur != candidates, 1.0, 0.0)
    gate = jnp.concatenate(gates, axis=0)
    gate = ROUTED_SCALE * gate / (jnp.sum(gate, axis=0, keepdims=True) + 1e-20)

    si = lax.broadcasted_iota(jnp.int32, (tm, tm), 0)
    ti = lax.broadcasted_iota(jnp.int32, (tm, tm), 1)
    upper = jnp.where(si < ti, 1.0, 0.0).astype(BF16)
    before = jnp.dot(onehot.astype(BF16), upper, preferred_element_type=F32) + run_ref[:, 0:1]
    ranks = [jnp.sum(jnp.where(eio == idx, before, 0.0), axis=0, keepdims=True) for idx in idxs]

    e_ref[...] = jnp.concatenate(idxs, axis=0).astype(jnp.int32)
    g_ref[...] = gate
    r_ref[...] = jnp.concatenate(ranks, axis=0).astype(jnp.int32)
    run_ref[...] = run_ref[...] + jnp.sum(onehot, axis=1, keepdims=True)
    cnt_ref[...] = run_ref[...]


def _router(h, router_w, router_bias):
    n_tok, d = h.shape
    n_exp = router_w.shape[1]
    tm = 512
    wt = router_w.astype(F32).T
    bias = jnp.broadcast_to(router_bias.astype(F32)[:, None], (n_exp, LANES))
    tok = pl.BlockSpec((TOP_K, tm), lambda i: (0, i))
    return pl.pallas_call(
        _router_kernel,
        out_shape=(jax.ShapeDtypeStruct((TOP_K, n_tok), jnp.int32),
                   jax.ShapeDtypeStruct((TOP_K, n_tok), F32),
                   jax.ShapeDtypeStruct((TOP_K, n_tok), jnp.int32),
                   jax.ShapeDtypeStruct((n_exp, LANES), F32)),
        grid=(n_tok // tm,),
        in_specs=[pl.BlockSpec((tm, d), lambda i: (i, 0)),
                  pl.BlockSpec((n_exp, d), lambda i: (0, 0)),
                  pl.BlockSpec((n_exp, LANES), lambda i: (0, 0))],
        out_specs=(tok, tok, tok, pl.BlockSpec((n_exp, LANES), lambda i: (0, 0))),
        scratch_shapes=[pltpu.VMEM((n_exp, LANES), F32)],
        compiler_params=_cparams(("arbitrary",), 32),
        name="router",
    )(h, wt, bias)


def _dest_kernel(e_ref, r_ref, st_ref, d_ref):
    n_exp = st_ref.shape[0]
    tm = e_ref.shape[1]
    eio = lax.broadcasted_iota(jnp.int32, (n_exp, tm), 0)
    start = st_ref[:, 0:1]
    rows = [jnp.sum(jnp.where(eio == e_ref[k:k + 1, :], start, 0.0), axis=0, keepdims=True)
            for k in range(TOP_K)]
    d_ref[...] = jnp.concatenate(rows, axis=0).astype(jnp.int32) + r_ref[...]


def _dest(top_e, rank, starts):
    n_tok = top_e.shape[1]
    n_exp = starts.shape[0]
    tm = 512
    st = jnp.broadcast_to(starts.astype(F32)[:, None], (n_exp, LANES))
    tok = pl.BlockSpec((TOP_K, tm), lambda i: (0, i))
    return pl.pallas_call(
        _dest_kernel,
        out_shape=jax.ShapeDtypeStruct((TOP_K, n_tok), jnp.int32),
        grid=(n_tok // tm,),
        in_specs=[tok, tok, pl.BlockSpec((n_exp, LANES), lambda i: (0, 0))],
        out_specs=tok,
        compiler_params=_cparams(("parallel",), 32),
        name="dest",
    )(top_e, rank, st)


def _pack_bf16_pairs(val):
    half = val.shape[1] // 2
    lo = pltpu.bitcast(val[:, :half].astype(BF16).astype(F32), U32)
    hi = pltpu.bitcast(val[:, half:].astype(BF16).astype(F32), U32)
    return (lo >> 16) | (hi & jnp.uint32(0xFFFF0000))


def _unpack_bf16_pairs(words):
    lo = pltpu.bitcast(words << 16, F32)
    hi = pltpu.bitcast(words & jnp.uint32(0xFFFF0000), F32)
    return jnp.concatenate([lo, hi], axis=1)


def _to_row_tiles(dst_ref, slot, val):
    rows = val.shape[0]
    words = _pack_bf16_pairs(val)
    for j in range(ROW_TILE):
        dst_ref[slot, pl.ds(j, rows, stride=ROW_TILE), :] = words[:, j * LANES:(j + 1) * LANES]


def _row_tile_words(src_ref, idx, rows):
    return jnp.concatenate([src_ref[(*idx, pl.ds(j, rows, stride=ROW_TILE), slice(None))]
                            for j in range(ROW_TILE)], axis=1)


def _row_tile(r):
    return pl.ds(pl.multiple_of(r * ROW_TILE, ROW_TILE), ROW_TILE)


def _dispatch_kernel(dest_ref, h_ref, xs_ref, ht_ref, sem):
    tm = h_ref.shape[0]
    i = pl.program_id(0)
    cur = i % 2
    _to_row_tiles(ht_ref, cur, h_ref[...])

    def issue(t, carry):
        for k in range(TOP_K):
            pltpu.make_async_copy(ht_ref.at[cur, _row_tile(t)], xs_ref.at[_row_tile(dest_ref[t * TOP_K + k])],
                                  sem.at[cur]).start(priority=k % 2)
        return carry
    lax.fori_loop(0, tm, issue, 0, unroll=2)

    def drain(slot):
        for k in range(TOP_K):
            pltpu.make_async_copy(ht_ref.at[slot], xs_ref.at[pl.ds(0, tm * ROW_TILE)], sem.at[slot]).wait()

    @pl.when(i > 0)
    def _():
        drain(1 - cur)

    @pl.when(i == pl.num_programs(0) - 1)
    def _():
        drain(cur)


def _dispatch(h, dest, n_exp):
    n_tok, d = h.shape
    assert d == 2 * ROW_TILE * LANES
    tm = MOE_TOKEN_TILE
    n_rows = (pl.cdiv(n_tok * TOP_K, EXPERT_ROWS) + n_exp) * EXPERT_ROWS
    return pl.pallas_call(
        _dispatch_kernel,
        out_shape=jax.ShapeDtypeStruct((n_rows * ROW_TILE, LANES), U32),
        grid=(n_tok // tm,),
        in_specs=[pl.BlockSpec((TOP_K * tm,), lambda i: (i,), memory_space=pltpu.SMEM),
                  pl.BlockSpec((tm, d), lambda i: (i, 0))],
        out_specs=pl.BlockSpec(memory_space=pl.ANY),
        scratch_shapes=[pltpu.VMEM((2, tm * ROW_TILE, LANES), U32), pltpu.SemaphoreType.DMA((2,))],
        compiler_params=_cparams(("arbitrary",), 32),
        name="dispatch",
    )(dest, h)


def _experts_kernel(bstart_ref, bend_ref, cnt_ref, nblk_ref, xs_ref, wgu_hbm, wdn_hbm, ys_ref,
                    xbuf, ybuf, act_ref, wgu_f32, wdn_f32, wgu_bf, wdn_bf, xsem, ysem, wsem):
    e = pl.program_id(0)
    n_blk = nblk_ref[0]
    trows = xbuf.shape[1]
    rows = trows // ROW_TILE
    ff = wdn_bf.shape[0]
    b0 = bstart_ref[e]
    b1 = bend_ref[e]

    def block_rows(b):
        return pl.ds(pl.multiple_of(b * trows, trows), trows)

    def x_copy(b):
        slot = b % EXPERT_RING
        return pltpu.make_async_copy(xs_ref.at[block_rows(b)], xbuf.at[slot], xsem.at[slot])

    def y_copy(b):
        slot = b % EXPERT_OUT_RING
        return pltpu.make_async_copy(ybuf.at[slot], ys_ref.at[block_rows(b)], ysem.at[slot])

    @pl.when(e == 0)
    def _():
        for i in range(EXPERT_RING):
            @pl.when(i < n_blk)
            def _():
                x_copy(i).start(priority=1)

    n_exp = pl.num_programs(0)
    wslot = e % EXPERT_WEIGHT_RING

    def w_copies(x, slot):
        return (pltpu.make_async_copy(wgu_hbm.at[x], wgu_f32.at[slot], wsem.at[0, slot]),
                pltpu.make_async_copy(wdn_hbm.at[x], wdn_f32.at[slot], wsem.at[1, slot]))

    @pl.when(e == 0)
    def _():
        for i in range(EXPERT_WEIGHT_RING):
            @pl.when(i < n_exp)
            def _():
                for c in w_copies(i, i):
                    c.start()

    for c in w_copies(e, wslot):
        c.wait()
    wgu_bf[...] = wgu_f32[wslot].astype(BF16)
    wdn_bf[...] = wdn_f32[wslot].astype(BF16)

    @pl.when(e + EXPERT_WEIGHT_RING < n_exp)
    def _():
        for c in w_copies(e + EXPERT_WEIGHT_RING, wslot):
            c.start()

    def up_wait(blocks):
        for b in blocks:
            x_copy(b).wait()

    def up_compute(blocks):
        for b in blocks:
            words = _row_tile_words(xbuf, (b % EXPERT_RING,), rows)
            row = lax.broadcasted_iota(jnp.int32, (rows, 1), 0)
            words = jnp.where(row < cnt_ref[e] - (b - b0) * rows, words, jnp.uint32(0))
            xb = _unpack_bf16_pairs(words).astype(BF16)
            gu = jnp.dot(xb, wgu_bf[...], preferred_element_type=F32)
            act_ref[b - b0] = (jax.nn.silu(gu[:, :ff]) * gu[:, ff:]).astype(BF16)

    def up_start(blocks):
        for b in blocks:
            @pl.when(b + EXPERT_RING < n_blk)
            def _():
                x_copy(b + EXPERT_RING).start(priority=1)

    def down_wait(blocks):
        for b in blocks:
            @pl.when(b >= EXPERT_OUT_RING)
            def _():
                y_copy(b - EXPERT_OUT_RING).wait()

    def down_compute(blocks):
        for b in blocks:
            _to_row_tiles(ybuf, b % EXPERT_OUT_RING,
                          jnp.dot(act_ref[b - b0], wdn_bf[...], preferred_element_type=F32))

    def down_start(blocks):
        for b in blocks:
            y_copy(b).start(priority=1)
        last = blocks[-1]

        @pl.when(last == n_blk - 1)
        def _():
            for i in range(EXPERT_OUT_RING):
                @pl.when(last >= i)
                def _():
                    y_copy(last - i).wait()

    def up(blocks):
        up_wait(blocks)
        up_compute(blocks)
        up_start(blocks)

    def down(blocks):
        down_wait(blocks)
        down_compute(blocks)
        down_start(blocks)

    n_mine = b1 - b0

    def run_groups(fn):
        def body(p, carry):
            fn(tuple(b0 + EXPERT_GROUP * p + j for j in range(EXPERT_GROUP)))
            return carry
        lax.fori_loop(0, n_mine // EXPERT_GROUP, body, 0)
        size = EXPERT_GROUP // 2
        while size >= 1:
            @pl.when(n_mine & size != 0)
            def _(size=size):
                start = b0 + (n_mine // (2 * size)) * (2 * size)
                fn(tuple(start + j for j in range(size)))
            size //= 2

    @pl.when(n_mine == EXPERT_GROUP)
    def _():
        blocks = tuple(b0 + j for j in range(EXPERT_GROUP))
        up_wait(blocks)
        down_wait(blocks)
        up_compute(blocks)
        down_compute(blocks)
        up_start(blocks)
        down_start(blocks)

    @pl.when(n_mine != EXPERT_GROUP)
    def _():
        run_groups(up)
        run_groups(down)


def _expert_blocks(counts):
    blocks = (counts + EXPERT_ROWS - 1) // EXPERT_ROWS
    bend = jnp.cumsum(blocks)
    bstart = bend - blocks
    i32 = lambda a: a.astype(jnp.int32)
    return i32(bstart), i32(bend), i32(bend[-1]).reshape(1), i32(bstart * EXPERT_ROWS)


def _experts(xs, bstart, bend, counts, n_blk, w_gu, w_down, n_tok):
    n_exp, d, ff2 = w_gu.shape
    ff = w_down.shape[1]
    n_rows = xs.shape[0] // ROW_TILE
    assert n_rows % EXPERT_ROWS == 0 and d == 2 * ROW_TILE * LANES
    max_blocks = pl.cdiv(n_tok, EXPERT_ROWS)
    grid_spec = pltpu.PrefetchScalarGridSpec(
        num_scalar_prefetch=4,
        grid=(n_exp,),
        in_specs=[pl.BlockSpec(memory_space=pl.ANY), pl.BlockSpec(memory_space=pl.ANY),
                  pl.BlockSpec(memory_space=pl.ANY)],
        out_specs=pl.BlockSpec(memory_space=pl.ANY),
        scratch_shapes=[pltpu.VMEM((EXPERT_RING, EXPERT_ROWS * ROW_TILE, LANES), U32),
                        pltpu.VMEM((EXPERT_OUT_RING, EXPERT_ROWS * ROW_TILE, LANES), U32),
                        pltpu.VMEM((max_blocks, EXPERT_ROWS, ff), BF16),
                        pltpu.VMEM((EXPERT_WEIGHT_RING, d, ff2), F32),
                        pltpu.VMEM((EXPERT_WEIGHT_RING, ff, d), F32),
                        pltpu.VMEM((d, ff2), BF16), pltpu.VMEM((ff, d), BF16),
                        pltpu.SemaphoreType.DMA((EXPERT_RING,)),
                        pltpu.SemaphoreType.DMA((EXPERT_OUT_RING,)),
                        pltpu.SemaphoreType.DMA((2, EXPERT_WEIGHT_RING))],
    )
    return pl.pallas_call(
        _experts_kernel,
        out_shape=jax.ShapeDtypeStruct(xs.shape, U32),
        grid_spec=grid_spec,
        compiler_params=_cparams(("arbitrary",), 48),
        name="experts",
    )(bstart, bend, counts, n_blk, xs, w_gu, w_down)


def _combine_kernel(dest_ref, dnext_ref, gate_ref, h_ref, ys_ref, wgu_ref, wdn_ref, g_ref, b_ref, o_ref,
                    buf_ref, routed_ref, sem, *, alpha):
    tm = h_ref.shape[0]
    i = pl.program_id(0)
    cur = i % 2
    chunk = SUBLANES

    def issue(d_ref, slot, t):
        for k in range(TOP_K):
            pltpu.make_async_copy(ys_ref.at[_row_tile(d_ref[t * TOP_K + k])], buf_ref.at[slot, k, _row_tile(t)],
                                  sem.at[slot]).start(priority=k % 2)

    @pl.when(i == 0)
    def _():
        def first(t, carry):
            issue(dest_ref, 0, t)
            return carry
        lax.fori_loop(0, tm, first, 0, unroll=2)

    for k in range(TOP_K):
        pltpu.make_async_copy(ys_ref.at[pl.ds(0, tm * ROW_TILE)], buf_ref.at[cur, k], sem.at[cur]).wait()

    def weighted_sum(c):
        tok = pl.ds(pl.multiple_of(c * chunk, chunk), chunk)
        gate = gate_ref[tok, :]
        total = None
        for k in range(TOP_K):
            words = jnp.concatenate(
                [buf_ref[cur, k, pl.ds(pl.multiple_of(c * (chunk * ROW_TILE), chunk * ROW_TILE) + j, chunk,
                                       stride=ROW_TILE), :] for j in range(ROW_TILE)], axis=1)
            term = gate[:, k:k + 1] * _unpack_bf16_pairs(words)
            total = term if total is None else total + term
        routed_ref[tok, :] = total

    @pl.when(i + 1 < pl.num_programs(0))
    def _():
        def body(c, carry):
            for t in range(chunk):
                issue(dnext_ref, 1 - cur, c * chunk + t)
            weighted_sum(c)
            return carry
        lax.fori_loop(0, tm // chunk, body, 0)

    @pl.when(i + 1 == pl.num_programs(0))
    def _():
        def body(c, carry):
            weighted_sum(c)
            return carry
        lax.fori_loop(0, tm // chunk, body, 0)

    h = h_ref[...]
    ff = wdn_ref.shape[0]
    gu = jnp.dot(h.astype(BF16), wgu_ref[...], preferred_element_type=F32)
    act = (jax.nn.silu(gu[:, :ff]) * gu[:, ff:]).astype(BF16)
    acc = alpha * h + jnp.dot(act, wdn_ref[...], preferred_element_type=F32) + routed_ref[...]
    o_ref[...] = _layer_norm(acc, g_ref[...], b_ref[...])


def _combine(h, ys, dest, gate_t, shared_w_gu, shared_w_down, ln_g, ln_b, alpha):
    n_tok, d = h.shape
    tm = MOE_TOKEN_TILE
    n_tiles = n_tok // tm
    full = lambda a: pl.BlockSpec(a.shape, lambda i: (0,) * a.ndim)
    wgu = shared_w_gu.astype(BF16)
    wdn = shared_w_down.astype(BF16)
    g2 = ln_g.reshape(1, d)
    b2 = ln_b.reshape(1, d)
    return pl.pallas_call(
        functools.partial(_combine_kernel, alpha=alpha),
        out_shape=jax.ShapeDtypeStruct((n_tok, d), F32),
        grid=(n_tiles,),
        in_specs=[pl.BlockSpec((TOP_K * tm,), lambda i: (i,), memory_space=pltpu.SMEM),
                  pl.BlockSpec((TOP_K * tm,), lambda i: (jnp.minimum(i + 1, n_tiles - 1),),
                               memory_space=pltpu.SMEM),
                  pl.BlockSpec((tm, TOP_K), lambda i: (i, 0)),
                  pl.BlockSpec((tm, d), lambda i: (i, 0)),
                  pl.BlockSpec(memory_space=pl.ANY),
                  full(wgu), full(wdn), full(g2), full(b2)],
        out_specs=pl.BlockSpec((tm, d), lambda i: (i, 0)),
        scratch_shapes=[pltpu.VMEM((2, TOP_K, tm * ROW_TILE, LANES), U32), pltpu.VMEM((tm, d), F32),
                        pltpu.SemaphoreType.DMA((2,))],
        compiler_params=_cparams(("arbitrary",), 48),
        name="combine",
    )(dest, dest, gate_t, h, ys, wgu, wdn, g2, b2)


def _moe(h, router_w, router_bias, w_gu, w_down, shared_w_gu, shared_w_down, ln_g, ln_b, alpha):
    top_e, gate, rank, cnt = _router(h, router_w, router_bias)
    counts = cnt[:, 0].astype(jnp.int32)
    bstart, bend, n_blk, pad_start = _expert_blocks(counts)
    dest = _dest(top_e, rank, pad_start)
    dest_tiles = dest.T.reshape(-1)
    xs = _dispatch(h, dest_tiles, counts.shape[0])
    ys = _experts(xs, bstart, bend, counts, n_blk, w_gu, w_down, h.shape[0])
    return _combine(h, ys, dest_tiles, gate.T, shared_w_gu, shared_w_down, ln_g, ln_b, alpha)


def kernel(x, w_in, att_norm_g, lam_re, lam_im, log_step, b_re, b_im, c_re, c_im, d_skip, w_glu, b_glu,
           ssm_norm_g, w_out, ln1_g, ln1_b, router_w, router_bias, w_gu, w_down, shared_w_gu,
           shared_w_down, ln2_g, ln2_b):
    bsz, seq, d = x.shape
    depth = w_in.shape[0]
    alpha = (2 * depth) ** 0.25
    h = x.reshape(bsz * seq, d)
    for i in range(depth):
        proj = _inproj(h, w_in[i].astype(BF16), seq)
        o_att = _attention(proj, bsz, seq)
        y_ssm = _s5(proj.reshape(bsz, seq, -1), 3 * ATT_WIDTH, lam_re[i], lam_im[i], log_step[i],
                    b_re[i], b_im[i], c_re[i], c_im[i], d_skip[i])
        h = _mixout(o_att, y_ssm.reshape(bsz * seq, -1), h, w_glu[i], b_glu[i], att_norm_g[i],
                    ssm_norm_g[i], w_out[i], ln1_g[i], ln1_b[i], alpha)
        h = _moe(h, router_w[i], router_bias[i], w_gu[i], w_down[i], shared_w_gu[i], shared_w_down[i],
                 ln2_g[i], ln2_b[i], alpha)
    return h.reshape(bsz, seq, d)
```

```python
import functools

import jax
import jax.numpy as jnp
from jax import lax
from jax.experimental import pallas as pl
from jax.experimental.pallas import tpu as pltpu

F32 = jnp.float32
BF16 = jnp.bfloat16
U32 = jnp.uint32

ATT_HEADS = 8
HEAD_DIM = 64
ATT_WIDTH = ATT_HEADS * HEAD_DIM
SSM_CH = 16
SSM_STATE = 64
S5_SLAB = 64
ROPE_THETA = 500000.0
ROT_DIM = HEAD_DIM // 4
DILATIONS = (1, 4, 16)
ATT_BLOCK = 128
ATT_GROUP = 16
TOP_K = 8
N_EXPERT_GROUPS = 8
TOPK_GROUPS = 4
ROUTED_SCALE = 2.5
LN_EPS = 1e-5
RMS_EPS = 1e-6

LANES = 128
SUBLANES = 8
EXPERT_ROWS = 144
MOE_TOKEN_TILE = 512
EXPERT_RING = 16
EXPERT_OUT_RING = 16
EXPERT_GROUP = 4
EXPERT_WEIGHT_RING = 4
ROW_TILE = 4
NEG_INF = float("-inf")


def _cparams(sem, vmem_mb):
    return pltpu.CompilerParams(dimension_semantics=sem, vmem_limit_bytes=vmem_mb * 1024 * 1024)


def _inproj_kernel(x_ref, wf_ref, cos_ref, sa_ref, sb_ref, o_ref, w_ref, *, n_rot_cols):
    @pl.when(pl.program_id(0) == 0)
    def _():
        w_ref[...] = wf_ref[...].astype(BF16)

    xb = x_ref[...].astype(BF16)
    cosf = cos_ref[...]
    sa = sa_ref[...]
    sb = sb_ref[...]
    width = o_ref.shape[1]
    chunk = 512
    for c in range(width // chunk):
        r = jnp.dot(xb, w_ref[:, c * chunk:(c + 1) * chunk], preferred_element_type=F32)
        if c * chunk < n_rot_cols:
            parts = []
            for s in range(chunk // LANES):
                t = r[:, s * LANES:(s + 1) * LANES]
                parts.append(t * cosf + pltpu.roll(t, LANES - ROT_DIM // 2, 1) * sa
                             + pltpu.roll(t, ROT_DIM // 2, 1) * sb)
            r = jnp.concatenate(parts, axis=1)
        o_ref[:, c * chunk:(c + 1) * chunk] = r


def _rope_lane_tables(seq):
    half = ROT_DIM // 2
    inv_freq = jnp.power(jnp.float32(ROPE_THETA), -jnp.arange(half, dtype=F32) / half)
    ang = jnp.arange(seq, dtype=F32)[:, None] * inv_freq[None, :]
    cos, sin = jnp.cos(ang), jnp.sin(ang)
    rest = HEAD_DIM - ROT_DIM
    cos_h = jnp.concatenate([cos, cos, jnp.ones((seq, rest), F32)], axis=1)
    sa_h = jnp.concatenate([-sin, jnp.zeros((seq, half + rest), F32)], axis=1)
    sb_h = jnp.concatenate([jnp.zeros((seq, half), F32), sin, jnp.zeros((seq, rest), F32)], axis=1)
    rep = LANES // HEAD_DIM
    return tuple(jnp.tile(t, (1, rep)) for t in (cos_h, sa_h, sb_h))


def _inproj(x2d, w_in, seq):
    n_tok, d = x2d.shape
    width = w_in.shape[1]
    tm = 512
    cosf, sa, sb = _rope_lane_tables(seq)
    tab_spec = pl.BlockSpec((tm, LANES), lambda i: (i % (seq // tm), 0))
    return pl.pallas_call(
        functools.partial(_inproj_kernel, n_rot_cols=2 * ATT_WIDTH),
        out_shape=jax.ShapeDtypeStruct((n_tok, width), F32),
        grid=(n_tok // tm,),
        in_specs=[pl.BlockSpec((tm, d), lambda i: (i, 0)),
                  pl.BlockSpec((d, width), lambda i: (0, 0)),
                  tab_spec, tab_spec, tab_spec],
        out_specs=pl.BlockSpec((tm, width), lambda i: (i, 0)),
        scratch_shapes=[pltpu.VMEM((d, width), BF16)],
        compiler_params=_cparams(("arbitrary",), 56),
        name="inproj",
    )(x2d, w_in, cosf, sa, sb)


def _attn_kernel(q_ref, k_ref, v_ref, o_ref, qs_ref, ks_ref, vs_ref, tmp_ref, ob_ref, lb_ref, band_ref,
                 first_ref, *, seq):
    blk = ATT_BLOCK
    lane = lax.broadcasted_iota(jnp.int32, (1, LANES), 1)
    head0 = lane < HEAD_DIM
    scale = HEAD_DIM ** -0.5
    d1, d2 = DILATIONS[1], DILATIONS[2]
    assert DILATIONS[0] == 1 and d2 == d1 * d1
    seg = seq // d1
    sub = seg // d1

    qi = lax.broadcasted_iota(jnp.int32, (blk, 2 * blk), 0)
    kj = lax.broadcasted_iota(jnp.int32, (blk, 2 * blk), 1)
    dist = qi + blk - kj
    band_ref[...] = jnp.where((dist >= 0) & (dist <= blk), 0.0, NEG_INF)
    first_ref[...] = jnp.where((dist >= 0) & (kj >= blk), 0.0, NEG_INF)

    n_class = (1, d1, d2)
    class_len = (seq, seg, sub)
    base = [0]
    for c in range(len(DILATIONS)):
        base.append(base[c] + n_class[c] * (class_len[c] + blk))

    def kv_row0(c, g):
        return base[c] + g * (class_len[c] + blk)

    qs_ref[0] = (q_ref[...] * scale).astype(BF16)
    for a in range(d1):
        x = q_ref[pl.ds(a, seg, stride=d1), :] * scale
        tmp_ref[a * seg:(a + 1) * seg, :] = x
        qs_ref[1, a * seg:(a + 1) * seg, :] = x.astype(BF16)
    for g in range(d2):
        qs_ref[2, g * sub:(g + 1) * sub, :] = tmp_ref[pl.ds((g // d1) * seg + g % d1, sub, stride=d1),
                                                      :].astype(BF16)
    for src_ref, dst_ref in ((k_ref, ks_ref), (v_ref, vs_ref)):
        for c in range(len(DILATIONS)):
            for g in range(n_class[c]):
                dst_ref[kv_row0(c, g):kv_row0(c, g) + blk, :] = jnp.zeros((blk, LANES), BF16)
        dst_ref[kv_row0(0, 0) + blk:kv_row0(0, 0) + blk + seq, :] = src_ref[...].astype(BF16)
        for a in range(d1):
            x = src_ref[pl.ds(a, seg, stride=d1), :]
            tmp_ref[a * seg:(a + 1) * seg, :] = x
            dst_ref[kv_row0(1, a) + blk:kv_row0(1, a) + blk + seg, :] = x.astype(BF16)
        for g in range(d2):
            dst_ref[kv_row0(2, g) + blk:kv_row0(2, g) + blk + sub, :] = tmp_ref[
                pl.ds((g // d1) * seg + g % d1, sub, stride=d1), :].astype(BF16)

    def one_block(c, g, n, out_rows, bias_ref):
        q = qs_ref[c, pl.ds(aligned(g * class_len[c] + n * blk), blk), :]
        kv_rows = pl.ds(aligned(kv_row0(c, g) + n * blk), 2 * blk)
        kk = ks_ref[kv_rows, :]
        vv = vs_ref[kv_rows, :]
        outs = []
        lses = []
        for h in range(LANES // HEAD_DIM):
            hm = head0 if h == 0 else jnp.logical_not(head0)
            qh = jnp.where(hm, q, jnp.zeros_like(q))
            s = lax.dot_general(qh, kk, (((1,), (1,)), ((), ())), preferred_element_type=F32)
            s = s + bias_ref[...]
            m = jnp.max(s, axis=-1, keepdims=True)
            p = jnp.exp(s - m)
            den = jnp.sum(p, axis=-1, keepdims=True)
            outs.append(jnp.dot(p.astype(BF16), vv, preferred_element_type=F32) / den)
            lses.append(m + jnp.log(den))
        ob_ref[c, out_rows, :] = jnp.where(head0, outs[0], outs[1])
        lb_ref[c, out_rows, :] = jnp.where(head0, lses[0], lses[1])

    def run_blocks(n_blocks, fn):
        group = max(g for g in range(1, ATT_GROUP + 1) if n_blocks % g == 0)
        if n_blocks == group:
            for g in range(group):
                fn(g)
            return
        def body(it, carry):
            for g in range(group):
                fn(it * group + g)
            return carry
        lax.fori_loop(0, n_blocks // group, body, 0)

    def aligned(x):
        return x if isinstance(x, int) else pl.multiple_of(x, blk)

    one_block(0, 0, 0, pl.ds(0, blk), first_ref)
    run_blocks(seq // blk - 1,
               lambda i: one_block(0, 0, i + 1, pl.ds(aligned((i + 1) * blk), blk), band_ref))

    nb1 = seg // blk
    run_blocks(d1, lambda a: one_block(1, a, 0, pl.ds(a, blk, stride=d1), first_ref))
    def later1(i):
        a = i // (nb1 - 1)
        n = i - a * (nb1 - 1) + 1
        one_block(1, a, n, pl.ds(a + n * (d1 * blk), blk, stride=d1), band_ref)
    run_blocks(d1 * (nb1 - 1), later1)

    assert sub == blk
    def only2(g):
        a = g // d1
        one_block(2, g, 0, pl.ds(a + d1 * (g - a * d1), blk, stride=d2), first_ref)
    run_blocks(d2, only2)

    rc = 256
    def merge(i, carry):
        sl = pl.ds(pl.multiple_of(i * rc, rc), rc)
        l0 = lb_ref[0, sl, :]
        l1 = lb_ref[1, sl, :]
        l2 = lb_ref[2, sl, :]
        mx = jnp.maximum(jnp.maximum(l0, l1), l2)
        e0 = jnp.exp(l0 - mx)
        e1 = jnp.exp(l1 - mx)
        e2 = jnp.exp(l2 - mx)
        o_ref[sl, :] = ((e0 * ob_ref[0, sl, :] + e1 * ob_ref[1, sl, :] + e2 * ob_ref[2, sl, :])
                        / (e0 + e1 + e2))
        return carry
    lax.fori_loop(0, seq // rc, merge, 0)


def _attention(proj, bsz, seq):
    n_tok = proj.shape[0]
    pairs = ATT_WIDTH // LANES
    assert seq % (ATT_BLOCK * max(DILATIONS)) == 0
    kv_rows = sum(seq + d * ATT_BLOCK for d in DILATIONS)
    blk = (seq, LANES)
    return pl.pallas_call(
        functools.partial(_attn_kernel, seq=seq),
        out_shape=jax.ShapeDtypeStruct((n_tok, ATT_WIDTH), F32),
        grid=(bsz, pairs),
        in_specs=[pl.BlockSpec(blk, lambda b, h: (b, h)),
                  pl.BlockSpec(blk, lambda b, h: (b, pairs + h)),
                  pl.BlockSpec(blk, lambda b, h: (b, 2 * pairs + h))],
        out_specs=pl.BlockSpec(blk, lambda b, h: (b, h)),
        scratch_shapes=[pltpu.VMEM((len(DILATIONS), seq, LANES), BF16),
                        pltpu.VMEM((kv_rows, LANES), BF16),
                        pltpu.VMEM((kv_rows, LANES), BF16),
                        pltpu.VMEM((seq, LANES), F32),
                        pltpu.VMEM((len(DILATIONS), seq, LANES), F32),
                        pltpu.VMEM((len(DILATIONS), seq, LANES), F32),
                        pltpu.VMEM((ATT_BLOCK, 2 * ATT_BLOCK), F32),
                        pltpu.VMEM((ATT_BLOCK, 2 * ATT_BLOCK), F32)],
        compiler_params=_cparams(("parallel", "parallel"), 40),
        name="attn",
    )(proj, proj, proj)


def _s5_kernel(u_ref, bm_ref, lam_ref, cm_ref, dk_ref, o_ref, us_ref, st_ref, ys_ref, carry_ref, *, tc):
    bsz = u_ref.shape[0]
    half = st_ref.shape[1] // 2
    slab_rows = S5_SLAB * bsz
    n_slabs = tc // S5_SLAB

    @pl.when(pl.program_id(1) == 0)
    def _():
        carry_ref[...] = jnp.zeros_like(carry_ref)

    for b in range(bsz):
        us_ref[pl.ds(b, tc, stride=bsz), :] = u_ref[b]

    bm = bm_ref[0]
    cm = cm_ref[0]
    lam = lam_ref[0]
    lam_re = lam[:, :half]
    lam_im = lam[:, half:]

    def slab(s):
        return slice(s * slab_rows, (s + 1) * slab_rows)

    def project_in(s):
        st_ref[slab(s), :] = jnp.dot(us_ref[slab(s), :].astype(BF16), bm, preferred_element_type=F32)

    def project_out(s):
        ys_ref[slab(s), :] = jnp.dot(st_ref[slab(s), :].astype(BF16), cm, preferred_element_type=F32)

    def scan(s, xr, xi):
        for t in range(S5_SLAB):
            sl = slice(s * slab_rows + t * bsz, s * slab_rows + (t + 1) * bsz)
            xr, xi = (lam_re * xr - lam_im * xi + st_ref[sl, :half],
                      lam_re * xi + lam_im * xr + st_ref[sl, half:])
            st_ref[sl, :half] = xr
            st_ref[sl, half:] = xi
        return xr, xi

    xr, xi = carry_ref[:, :half], carry_ref[:, half:]
    project_in(0)
    for s in range(n_slabs):
        if s + 1 < n_slabs:
            project_in(s + 1)
        xr, xi = scan(s, xr, xi)
        if s >= 1:
            project_out(s - 1)
    project_out(n_slabs - 1)
    carry_ref[:, :half] = xr
    carry_ref[:, half:] = xi

    dk = dk_ref[...]
    for b in range(bsz):
        o_ref[b] = ys_ref[pl.ds(b, tc, stride=bsz), :] + dk * u_ref[b]


def _s5_params(lam_re, lam_im, log_step, b_re, b_im, c_re, c_im, bsz):
    groups = lam_re.shape[0]
    gpc = LANES // SSM_CH
    n_chunks = groups // gpc
    lam = lax.complex(lam_re.astype(F32), lam_im.astype(F32))
    step = jnp.exp(log_step.astype(F32))[:, None]
    lam_bar = jnp.exp(lam * step)
    bmat = lax.complex(b_re.astype(F32), b_im.astype(F32))
    b_bar = ((lam_bar - 1.0) / lam)[..., None] * bmat
    eye = jnp.eye(gpc, dtype=F32)

    def block_diag_in(t):
        t = t.reshape(n_chunks, gpc, SSM_STATE, SSM_CH)
        return jnp.einsum('ngpc,gh->ngchp', t, eye).reshape(n_chunks, gpc * SSM_CH, gpc * SSM_STATE)

    def block_diag_out(t):
        t = t.reshape(n_chunks, gpc, SSM_CH, SSM_STATE)
        return jnp.einsum('ngcp,gh->ngphc', t, eye).reshape(n_chunks, gpc * SSM_STATE, gpc * SSM_CH)

    bm = jnp.concatenate([block_diag_in(b_bar.real), block_diag_in(b_bar.imag)], axis=2).astype(BF16)
    cm = jnp.concatenate([block_diag_out(c_re.astype(F32)), block_diag_out(-c_im.astype(F32))],
                         axis=1).astype(BF16)
    lam_row = jnp.concatenate([lam_bar.real.reshape(n_chunks, gpc * SSM_STATE),
                               lam_bar.imag.reshape(n_chunks, gpc * SSM_STATE)], axis=1)
    lam_t = jnp.broadcast_to(lam_row[:, None, :], (n_chunks, bsz, 2 * gpc * SSM_STATE))
    return bm, lam_t, cm, n_chunks


def _s5(proj3, u_col0, lam_re, lam_im, log_step, b_re, b_im, c_re, c_im, d_skip):
    bsz, seq, _ = proj3.shape
    assert bsz == SUBLANES
    bm, lam_t, cm, n_chunks = _s5_params(lam_re, lam_im, log_step, b_re, b_im, c_re, c_im, bsz)
    width = n_chunks * LANES
    tc = 256
    st_cols = bm.shape[2]
    ublk0 = u_col0 // LANES
    return pl.pallas_call(
        functools.partial(_s5_kernel, tc=tc),
        out_shape=jax.ShapeDtypeStruct((bsz, seq, width), F32),
        grid=(n_chunks, seq // tc),
        in_specs=[pl.BlockSpec((bsz, tc, LANES), lambda c, t: (0, t, ublk0 + c)),
                  pl.BlockSpec((1, LANES, st_cols), lambda c, t: (c, 0, 0)),
                  pl.BlockSpec((1, bsz, st_cols), lambda c, t: (c, 0, 0)),
                  pl.BlockSpec((1, st_cols, LANES), lambda c, t: (c, 0, 0)),
                  pl.BlockSpec((1, LANES), lambda c, t: (0, c))],
        out_specs=pl.BlockSpec((bsz, tc, LANES), lambda c, t: (0, t, c)),
        scratch_shapes=[pltpu.VMEM((tc * bsz, LANES), F32),
                        pltpu.VMEM((tc * bsz, st_cols), F32),
                        pltpu.VMEM((tc * bsz, LANES), F32),
                        pltpu.VMEM((bsz, st_cols), F32)],
        compiler_params=_cparams(("arbitrary", "arbitrary"), 40),
        name="s5",
    )(proj3, bm, lam_t, cm, d_skip.reshape(1, width).astype(F32))


def _layer_norm(v, g, b):
    mu = jnp.mean(v, axis=-1, keepdims=True)
    var = jnp.mean(jnp.square(v - mu), axis=-1, keepdims=True)
    return (v - mu) * lax.rsqrt(var + LN_EPS) * g + b


def _rms_norm(v, g):
    return v * lax.rsqrt(jnp.mean(jnp.square(v), axis=-1, keepdims=True) + RMS_EPS) * g


def _mixout_kernel(att_ref, ssm_ref, x_ref, wglu_f32, bglu_ref, ag_ref, sg_ref, wout_f32, g_ref, b_ref,
                   o_ref, wglu_ref, wout_ref, *, alpha):
    @pl.when(pl.program_id(0) == 0)
    def _():
        wglu_ref[...] = wglu_f32[...].astype(BF16)
        wout_ref[...] = wout_f32[...].astype(BF16)

    y = jax.nn.gelu(ssm_ref[...])
    z = jnp.dot(y.astype(BF16), wglu_ref[...], preferred_element_type=F32) + bglu_ref[...]
    o_ssm = y * jax.nn.sigmoid(z)
    a = _rms_norm(att_ref[...], ag_ref[...]).astype(BF16)
    s = _rms_norm(o_ssm, sg_ref[...]).astype(BF16)
    wa = att_ref.shape[1]
    mix = (jnp.dot(a, wout_ref[:wa, :], preferred_element_type=F32)
           + jnp.dot(s, wout_ref[wa:, :], preferred_element_type=F32))
    o_ref[...] = _layer_norm(alpha * x_ref[...] + mix, g_ref[...], b_ref[...])


def _mixout(o_att, y_ssm, x2d, w_glu, b_glu, att_g, ssm_g, w_out, ln_g, ln_b, alpha):
    n_tok, d = x2d.shape
    wa = o_att.shape[1]
    ws = y_ssm.shape[1]
    tm = 512
    row = lambda w: pl.BlockSpec((tm, w), lambda i: (i, 0))
    full = lambda a: pl.BlockSpec(a.shape, lambda i: (0,) * a.ndim)
    args = (o_att, y_ssm, x2d, w_glu, b_glu.reshape(1, ws), att_g.reshape(1, wa),
            ssm_g.reshape(1, ws), w_out, ln_g.reshape(1, d), ln_b.reshape(1, d))
    return pl.pallas_call(
        functools.partial(_mixout_kernel, alpha=alpha),
        out_shape=jax.ShapeDtypeStruct((n_tok, d), F32),
        grid=(n_tok // tm,),
        in_specs=[row(wa), row(ws), row(d)] + [full(a) for a in args[3:]],
        out_specs=row(d),
        scratch_shapes=[pltpu.VMEM(w_glu.shape, BF16), pltpu.VMEM(w_out.shape, BF16)],
        compiler_params=_cparams(("arbitrary",), 40),
        name="mixout",
    )(*args)


def _split_bf16(v):
    hi = v.astype(BF16)
    lo = (v - hi.astype(F32)).astype(BF16)
    return hi, lo


def _router_kernel(h_ref, wt_ref, bias_ref, e_ref, g_ref, r_ref, cnt_ref, run_ref):
    tm = h_ref.shape[0]
    n_exp = wt_ref.shape[0]
    gsz = n_exp // N_EXPERT_GROUPS

    @pl.when(pl.program_id(0) == 0)
    def _():
        run_ref[...] = jnp.zeros_like(run_ref)

    w_hi, w_lo = _split_bf16(wt_ref[...])
    h_hi, h_lo = _split_bf16(h_ref[...])
    nt = (((1,), (1,)), ((), ()))
    logits = (lax.dot_general(w_hi, h_hi, nt, preferred_element_type=F32)
              + lax.dot_general(w_hi, h_lo, nt, preferred_element_type=F32)
              + lax.dot_general(w_lo, h_hi, nt, preferred_element_type=F32))
    scores = jax.nn.sigmoid(logits)
    choice = scores + bias_ref[:, 0:1]

    gio = lax.broadcasted_iota(jnp.int32, (gsz, tm), 0).astype(F32)
    gscore = []
    for g in range(N_EXPERT_GROUPS):
        cg = choice[g * gsz:(g + 1) * gsz, :]
        m1 = jnp.max(cg, axis=0, keepdims=True)
        i1 = jnp.min(jnp.where(cg == m1, gio, float(gsz)), axis=0, keepdims=True)
        m2 = jnp.max(jnp.where(gio == i1, NEG_INF, cg), axis=0, keepdims=True)
        gscore.append(m1 + m2)
    masked = []
    for g in range(N_EXPERT_GROUPS):
        beat = jnp.zeros((1, tm), F32)
        for o in range(N_EXPERT_GROUPS):
            if o == g:
                continue
            wins = (gscore[o] >= gscore[g]) if o < g else (gscore[o] > gscore[g])
            beat = beat + jnp.where(wins, 1.0, 0.0)
        keep = beat < float(TOPK_GROUPS)
        masked.append(jnp.where(keep, choice[g * gsz:(g + 1) * gsz, :], NEG_INF))
    cur = jnp.concatenate(masked, axis=0)

    eio = lax.broadcasted_iota(jnp.int32, (n_exp, tm), 0).astype(F32)
    idxs = []
    gates = []
    candidates = cur
    for _ in range(TOP_K):
        m = jnp.max(cur, axis=0, keepdims=True)
        idx = jnp.min(jnp.where(cur == m, eio, float(n_exp)), axis=0, keepdims=True)
        hit = eio == idx
        idxs.append(idx)
        gates.append(jnp.sum(jnp.where(hit, scores, 0.0), axis=0, keepdims=True))
        cur = jnp.where(hit, NEG_INF, cur)
    onehot = jnp.where(cur != candidates, 1.0, 0.0)
    gate = jnp.concatenate(gates, axis=0)
    gate = ROUTED_SCALE * gate / (jnp.sum(gate, axis=0, keepdims=True) + 1e-20)

    si = lax.broadcasted_iota(jnp.int32, (tm, tm), 0)
    ti = lax.broadcasted_iota(jnp.int32, (tm, tm), 1)
    upper = jnp.where(si < ti, 1.0, 0.0).astype(BF16)
    before = jnp.dot(onehot.astype(BF16), upper, preferred_element_type=F32) + run_ref[:, 0:1]
    ranks = [jnp.sum(jnp.where(eio == idx, before, 0.0), axis=0, keepdims=True) for idx in idxs]

    e_ref[...] = jnp.concatenate(idxs, axis=0).astype(jnp.int32)
    g_ref[...] = gate
    r_ref[...] = jnp.concatenate(ranks, axis=0).astype(jnp.int32)
    run_ref[...] = run_ref[...] + jnp.sum(onehot, axis=1, keepdims=True)
    cnt_ref[...] = run_ref[...]


def _router(h, router_w, router_bias):
    n_tok, d = h.shape
    n_exp = router_w.shape[1]
    tm = 512
    wt = router_w.astype(F32).T
    bias = jnp.broadcast_to(router_bias.astype(F32)[:, None], (n_exp, LANES))
    tok = pl.BlockSpec((TOP_K, tm), lambda i: (0, i))
    return pl.pallas_call(
        _router_kernel,
        out_shape=(jax.ShapeDtypeStruct((TOP_K, n_tok), jnp.int32),
                   jax.ShapeDtypeStruct((TOP_K, n_tok), F32),
                   jax.ShapeDtypeStruct((TOP_K, n_tok), jnp.int32),
                   jax.ShapeDtypeStruct((n_exp, LANES), F32)),
        grid=(n_tok // tm,),
        in_specs=[pl.BlockSpec((tm, d), lambda i: (i, 0)),
                  pl.BlockSpec((n_exp, d), lambda i: (0, 0)),
                  pl.BlockSpec((n_exp, LANES), lambda i: (0, 0))],
        out_specs=(tok, tok, tok, pl.BlockSpec((n_exp, LANES), lambda i: (0, 0))),
        scratch_shapes=[pltpu.VMEM((n_exp, LANES), F32)],
        compiler_params=_cparams(("arbitrary",), 32),
        name="router",
    )(h, wt, bias)


def _dest_kernel(e_ref, r_ref, st_ref, d_ref):
    n_exp = st_ref.shape[0]
    tm = e_ref.shape[1]
    eio = lax.broadcasted_iota(jnp.int32, (n_exp, tm), 0)
    start = st_ref[:, 0:1]
    rows = [jnp.sum(jnp.where(eio == e_ref[k:k + 1, :], start, 0.0), axis=0, keepdims=True)
            for k in range(TOP_K)]
    d_ref[...] = jnp.concatenate(rows, axis=0).astype(jnp.int32) + r_ref[...]


def _dest(top_e, rank, starts):
    n_tok = top_e.shape[1]
    n_exp = starts.shape[0]
    tm = 512
    st = jnp.broadcast_to(starts.astype(F32)[:, None], (n_exp, LANES))
    tok = pl.BlockSpec((TOP_K, tm), lambda i: (0, i))
    return pl.pallas_call(
        _dest_kernel,
        out_shape=jax.ShapeDtypeStruct((TOP_K, n_tok), jnp.int32),
        grid=(n_tok // tm,),
        in_specs=[tok, tok, pl.BlockSpec((n_exp, LANES), lambda i: (0, 0))],
        out_specs=tok,
        compiler_params=_cparams(("parallel",), 32),
        name="dest",
    )(top_e, rank, st)


def _pack_bf16_pairs(val):
    half = val.shape[1] // 2
    lo = pltpu.bitcast(val[:, :half].astype(BF16).astype(F32), U32)
    hi = pltpu.bitcast(val[:, half:].astype(BF16).astype(F32), U32)
    return (lo >> 16) | (hi & jnp.uint32(0xFFFF0000))


def _unpack_bf16_pairs(words):
    lo = pltpu.bitcast(words << 16, F32)
    hi = pltpu.bitcast(words & jnp.uint32(0xFFFF0000), F32)
    return jnp.concatenate([lo, hi], axis=1)


def _to_row_tiles(dst_ref, slot, val):
    rows = val.shape[0]
    words = _pack_bf16_pairs(val)
    for j in range(ROW_TILE):
        dst_ref[slot, pl.ds(j, rows, stride=ROW_TILE), :] = words[:, j * LANES:(j + 1) * LANES]


def _row_tile_words(src_ref, idx, rows):
    return jnp.concatenate([src_ref[(*idx, pl.ds(j, rows, stride=ROW_TILE), slice(None))]
                            for j in range(ROW_TILE)], axis=1)


def _row_tile(r):
    return pl.ds(pl.multiple_of(r * ROW_TILE, ROW_TILE), ROW_TILE)


def _dispatch_kernel(dest_ref, h_ref, xs_ref, ht_ref, sem):
    tm = h_ref.shape[0]
    i = pl.program_id(0)
    cur = i % 2
    _to_row_tiles(ht_ref, cur, h_ref[...])

    def issue(t, carry):
        for k in range(TOP_K):
            pltpu.make_async_copy(ht_ref.at[cur, _row_tile(t)], xs_ref.at[_row_tile(dest_ref[t * TOP_K + k])],
                                  sem.at[cur]).start(priority=k % 2)
        return carry
    lax.fori_loop(0, tm, issue, 0, unroll=2)

    def drain(slot):
        for k in range(TOP_K):
            pltpu.make_async_copy(ht_ref.at[slot], xs_ref.at[pl.ds(0, tm * ROW_TILE)], sem.at[slot]).wait()

    @pl.when(i > 0)
    def _():
        drain(1 - cur)

    @pl.when(i == pl.num_programs(0) - 1)
    def _():
        drain(cur)


def _dispatch(h, dest, n_exp):
    n_tok, d = h.shape
    assert d == 2 * ROW_TILE * LANES
    tm = MOE_TOKEN_TILE
    n_rows = (pl.cdiv(n_tok * TOP_K, EXPERT_ROWS) + n_exp) * EXPERT_ROWS
    return pl.pallas_call(
        _dispatch_kernel,
        out_shape=jax.ShapeDtypeStruct((n_rows * ROW_TILE, LANES), U32),
        grid=(n_tok // tm,),
        in_specs=[pl.BlockSpec((TOP_K * tm,), lambda i: (i,), memory_space=pltpu.SMEM),
                  pl.BlockSpec((tm, d), lambda i: (i, 0))],
        out_specs=pl.BlockSpec(memory_space=pl.ANY),
        scratch_shapes=[pltpu.VMEM((2, tm * ROW_TILE, LANES), U32), pltpu.SemaphoreType.DMA((2,))],
        compiler_params=_cparams(("arbitrary",), 32),
        name="dispatch",
    )(dest, h)


def _experts_kernel(bstart_ref, bend_ref, cnt_ref, nblk_ref, xs_ref, wgu_hbm, wdn_hbm, ys_ref,
                    xbuf, ybuf, act_ref, wgu_f32, wdn_f32, wgu_bf, wdn_bf, xsem, ysem, wsem):
    e = pl.program_id(0)
    n_blk = nblk_ref[0]
    trows = xbuf.shape[1]
    rows = trows // ROW_TILE
    ff = wdn_bf.shape[0]
    b0 = bstart_ref[e]
    b1 = bend_ref[e]

    def block_rows(b):
        return pl.ds(pl.multiple_of(b * trows, trows), trows)

    def x_copy(b):
        slot = b % EXPERT_RING
        return pltpu.make_async_copy(xs_ref.at[block_rows(b)], xbuf.at[slot], xsem.at[slot])

    def y_copy(b):
        slot = b % EXPERT_OUT_RING
        return pltpu.make_async_copy(ybuf.at[slot], ys_ref.at[block_rows(b)], ysem.at[slot])

    @pl.when(e == 0)
    def _():
        for i in range(EXPERT_RING):
            @pl.when(i < n_blk)
            def _():
                x_copy(i).start(priority=1)

    n_exp = pl.num_programs(0)
    wslot = e % EXPERT_WEIGHT_RING

    def w_copies(x, slot):
        return (pltpu.make_async_copy(wgu_hbm.at[x], wgu_f32.at[slot], wsem.at[0, slot]),
                pltpu.make_async_copy(wdn_hbm.at[x], wdn_f32.at[slot], wsem.at[1, slot]))

    @pl.when(e == 0)
    def _():
        for i in range(EXPERT_WEIGHT_RING):
            @pl.when(i < n_exp)
            def _():
                for c in w_copies(i, i):
                    c.start()

    for c in w_copies(e, wslot):
        c.wait()
    wgu_bf[...] = wgu_f32[wslot].astype(BF16)
    wdn_bf[...] = wdn_f32[wslot].astype(BF16)

    @pl.when(e + EXPERT_WEIGHT_RING < n_exp)
    def _():
        for c in w_copies(e + EXPERT_WEIGHT_RING, wslot):
            c.start()

    def up_wait(blocks):
        for b in blocks:
            x_copy(b).wait()

    def up_compute(blocks):
        for b in blocks:
            words = _row_tile_words(xbuf, (b % EXPERT_RING,), rows)
            row = lax.broadcasted_iota(jnp.int32, (rows, 1), 0)
            words = jnp.where(row < cnt_ref[e] - (b - b0) * rows, words, jnp.uint32(0))
            xb = _unpack_bf16_pairs(words).astype(BF16)
            gu = jnp.dot(xb, wgu_bf[...], preferred_element_type=F32)
            act_ref[b - b0] = (jax.nn.silu(gu[:, :ff]) * gu[:, ff:]).astype(BF16)

    def up_start(blocks):
        for b in blocks:
            @pl.when(b + EXPERT_RING < n_blk)
            def _():
                x_copy(b + EXPERT_RING).start(priority=1)

    def down_wait(blocks):
        for b in blocks:
            @pl.when(b >= EXPERT_OUT_RING)
            def _():
                y_copy(b - EXPERT_OUT_RING).wait()

    def down_compute(blocks):
        for b in blocks:
            _to_row_tiles(ybuf, b % EXPERT_OUT_RING,
                          jnp.dot(act_ref[b - b0], wdn_bf[...], preferred_element_type=F32))

    def down_start(blocks):
        for b in blocks:
            y_copy(b).start(priority=1)
        last = blocks[-1]

        @pl.when(last == n_blk - 1)
        def _():
            for i in range(EXPERT_OUT_RING):
                @pl.when(last >= i)
                def _():
                    y_copy(last - i).wait()

    def up(blocks):
        up_wait(blocks)
        up_compute(blocks)
        up_start(blocks)

    def down(blocks):
        down_wait(blocks)
        down_compute(blocks)
        down_start(blocks)

    n_mine = b1 - b0

    def run_groups(fn):
        def body(p, carry):
            fn(tuple(b0 + EXPERT_GROUP * p + j for j in range(EXPERT_GROUP)))
            return carry
        lax.fori_loop(0, n_mine // EXPERT_GROUP, body, 0)
        size = EXPERT_GROUP // 2
        while size >= 1:
            @pl.when(n_mine & size != 0)
            def _(size=size):
                start = b0 + (n_mine // (2 * size)) * (2 * size)
                fn(tuple(start + j for j in range(size)))
            size //= 2

    @pl.when(n_mine == EXPERT_GROUP)
    def _():
        blocks = tuple(b0 + j for j in range(EXPERT_GROUP))
        up_wait(blocks)
        down_wait(blocks)
        up_compute(blocks)
        down_compute(blocks)
        up_start(blocks)
        down_start(blocks)

    @pl.when(n_mine != EXPERT_GROUP)
    def _():
        run_groups(up)
        run_groups(down)


def _expert_blocks(counts):
    blocks = (counts + EXPERT_ROWS - 1) // EXPERT_ROWS
    bend = jnp.cumsum(blocks)
    bstart = bend - blocks
    i32 = lambda a: a.astype(jnp.int32)
    return i32(bstart), i32(bend), i32(bend[-1]).reshape(1), i32(bstart * EXPERT_ROWS)


def _experts(xs, bstart, bend, counts, n_blk, w_gu, w_down, n_tok):
    n_exp, d, ff2 = w_gu.shape
    ff = w_down.shape[1]
    n_rows = xs.shape[0] // ROW_TILE
    assert n_rows % EXPERT_ROWS == 0 and d == 2 * ROW_TILE * LANES
    max_blocks = pl.cdiv(n_tok, EXPERT_ROWS)
    grid_spec = pltpu.PrefetchScalarGridSpec(
        num_scalar_prefetch=4,
        grid=(n_exp,),
        in_specs=[pl.BlockSpec(memory_space=pl.ANY), pl.BlockSpec(memory_space=pl.ANY),
                  pl.BlockSpec(memory_space=pl.ANY)],
        out_specs=pl.BlockSpec(memory_space=pl.ANY),
        scratch_shapes=[pltpu.VMEM((EXPERT_RING, EXPERT_ROWS * ROW_TILE, LANES), U32),
                        pltpu.VMEM((EXPERT_OUT_RING, EXPERT_ROWS * ROW_TILE, LANES), U32),
                        pltpu.VMEM((max_blocks, EXPERT_ROWS, ff), BF16),
                        pltpu.VMEM((EXPERT_WEIGHT_RING, d, ff2), F32),
                        pltpu.VMEM((EXPERT_WEIGHT_RING, ff, d), F32),
                        pltpu.VMEM((d, ff2), BF16), pltpu.VMEM((ff, d), BF16),
                        pltpu.SemaphoreType.DMA((EXPERT_RING,)),
                        pltpu.SemaphoreType.DMA((EXPERT_OUT_RING,)),
                        pltpu.SemaphoreType.DMA((2, EXPERT_WEIGHT_RING))],
    )
    return pl.pallas_call(
        _experts_kernel,
        out_shape=jax.ShapeDtypeStruct(xs.shape, U32),
        grid_spec=grid_spec,
        compiler_params=_cparams(("arbitrary",), 48),
        name="experts",
    )(bstart, bend, counts, n_blk, xs, w_gu, w_down)


def _combine_kernel(dest_ref, dnext_ref, gate_ref, h_ref, ys_ref, wgu_ref, wdn_ref, g_ref, b_ref, o_ref,
                    buf_ref, routed_ref, sem, *, alpha):
    tm = h_ref.shape[0]
    i = pl.program_id(0)
    cur = i % 2
    chunk = SUBLANES

    def issue(d_ref, slot, t):
        for k in range(TOP_K):
            pltpu.make_async_copy(ys_ref.at[_row_tile(d_ref[t * TOP_K + k])], buf_ref.at[slot, k, _row_tile(t)],
                                  sem.at[slot]).start(priority=k % 2)

    @pl.when(i == 0)
    def _():
        def first(t, carry):
            issue(dest_ref, 0, t)
            return carry
        lax.fori_loop(0, tm, first, 0, unroll=2)

    for k in range(TOP_K):
        pltpu.make_async_copy(ys_ref.at[pl.ds(0, tm * ROW_TILE)], buf_ref.at[cur, k], sem.at[cur]).wait()

    def weighted_sum(c):
        tok = pl.ds(pl.multiple_of(c * chunk, chunk), chunk)
        gate = gate_ref[tok, :]
        total = None
        for k in range(TOP_K):
            words = jnp.concatenate(
                [buf_ref[cur, k, pl.ds(pl.multiple_of(c * (chunk * ROW_TILE), chunk * ROW_TILE) + j, chunk,
                                       stride=ROW_TILE), :] for j in range(ROW_TILE)], axis=1)
            term = gate[:, k:k + 1] * _unpack_bf16_pairs(words)
            total = term if total is None else total + term
        routed_ref[tok, :] = total

    @pl.when(i + 1 < pl.num_programs(0))
    def _():
        def body(c, carry):
            for t in range(chunk):
                issue(dnext_ref, 1 - cur, c * chunk + t)
            weighted_sum(c)
            return carry
        lax.fori_loop(0, tm // chunk, body, 0)

    @pl.when(i + 1 == pl.num_programs(0))
    def _():
        def body(c, carry):
            weighted_sum(c)
            return carry
        lax.fori_loop(0, tm // chunk, body, 0)

    h = h_ref[...]
    ff = wdn_ref.shape[0]
    gu = jnp.dot(h.astype(BF16), wgu_ref[...], preferred_element_type=F32)
    act = (jax.nn.silu(gu[:, :ff]) * gu[:, ff:]).astype(BF16)
    acc = alpha * h + jnp.dot(act, wdn_ref[...], preferred_element_type=F32) + routed_ref[...]
    o_ref[...] = _layer_norm(acc, g_ref[...], b_ref[...])


def _combine(h, ys, dest, gate_t, shared_w_gu, shared_w_down, ln_g, ln_b, alpha):
    n_tok, d = h.shape
    tm = MOE_TOKEN_TILE
    n_tiles = n_tok // tm
    full = lambda a: pl.BlockSpec(a.shape, lambda i: (0,) * a.ndim)
    wgu = shared_w_gu.astype(BF16)
    wdn = shared_w_down.astype(BF16)
    g2 = ln_g.reshape(1, d)
    b2 = ln_b.reshape(1, d)
    return pl.pallas_call(
        functools.partial(_combine_kernel, alpha=alpha),
        out_shape=jax.ShapeDtypeStruct((n_tok, d), F32),
        grid=(n_tiles,),
        in_specs=[pl.BlockSpec((TOP_K * tm,), lambda i: (i,), memory_space=pltpu.SMEM),
                  pl.BlockSpec((TOP_K * tm,), lambda i: (jnp.minimum(i + 1, n_tiles - 1),),
                               memory_space=pltpu.SMEM),
                  pl.BlockSpec((tm, TOP_K), lambda i: (i, 0)),
                  pl.BlockSpec((tm, d), lambda i: (i, 0)),
                  pl.BlockSpec(memory_space=pl.ANY),
                  full(wgu), full(wdn), full(g2), full(b2)],
        out_specs=pl.BlockSpec((tm, d), lambda i: (i, 0)),
        scratch_shapes=[pltpu.VMEM((2, TOP_K, tm * ROW_TILE, LANES), U32), pltpu.VMEM((tm, d), F32),
                        pltpu.SemaphoreType.DMA((2,))],
        compiler_params=_cparams(("arbitrary",), 48),
        name="combine",
    )(dest, dest, gate_t, h, ys, wgu, wdn, g2, b2)


def _moe(h, router_w, router_bias, w_gu, w_down, shared_w_gu, shared_w_down, ln_g, ln_b, alpha):
    top_e, gate, rank, cnt = _router(h, router_w, router_bias)
    counts = cnt[:, 0].astype(jnp.int32)
    bstart, bend, n_blk, pad_start = _expert_blocks(counts)
    dest = _dest(top_e, rank, pad_start)
    dest_tiles = dest.T.reshape(-1)
    xs = _dispatch(h, dest_tiles, counts.shape[0])
    ys = _experts(xs, bstart, bend, counts, n_blk, w_gu, w_down, h.shape[0])
    return _combine(h, ys, dest_tiles, gate.T, shared_w_gu, shared_w_down, ln_g, ln_b, alpha)


def kernel(x, w_in, att_norm_g, lam_re, lam_im, log_step, b_re, b_im, c_re, c_im, d_skip, w_glu, b_glu,
           ssm_norm_g, w_out, ln1_g, ln1_b, router_w, router_bias, w_gu, w_down, shared_w_gu,
           shared_w_down, ln2_g, ln2_b):
    bsz, seq, d = x.shape
    depth = w_in.shape[0]
    alpha = (2 * depth) ** 0.25
    h = x.reshape(bsz * seq, d)
    for i in range(depth):
        proj = _inproj(h, w_in[i], seq)
        o_att = _attention(proj, bsz, seq)
        y_ssm = _s5(proj.reshape(bsz, seq, -1), 3 * ATT_WIDTH, lam_re[i], lam_im[i], log_step[i],
                    b_re[i], b_im[i], c_re[i], c_im[i], d_skip[i])
        h = _mixout(o_att, y_ssm.reshape(bsz * seq, -1), h, w_glu[i], b_glu[i], att_norm_g[i],
                    ssm_norm_g[i], w_out[i], ln1_g[i], ln1_b[i], alpha)
        h = _moe(h, router_w[i], router_bias[i], w_gu[i], w_down[i], shared_w_gu[i], shared_w_down[i],
                 ln2_g[i], ln2_b[i], alpha)
    return h.reshape(bsz, seq, d)
```

```python
import functools

import jax
import jax.numpy as jnp
from jax import lax
from jax.experimental import pallas as pl
from jax.experimental.pallas import tpu as pltpu

F32 = jnp.float32
BF16 = jnp.bfloat16
U32 = jnp.uint32

ATT_HEADS = 8
HEAD_DIM = 64
ATT_WIDTH = ATT_HEADS * HEAD_DIM
SSM_CH = 16
SSM_STATE = 64
S5_SLAB = 64
ROPE_THETA = 500000.0
ROT_DIM = HEAD_DIM // 4
DILATIONS = (1, 4, 16)
ATT_BLOCK = 128
ATT_GROUP = 16
TOP_K = 8
N_EXPERT_GROUPS = 8
TOPK_GROUPS = 4
ROUTED_SCALE = 2.5
LN_EPS = 1e-5
RMS_EPS = 1e-6

LANES = 128
SUBLANES = 8
EXPERT_ROWS = 144
MOE_TOKEN_TILE = 512
EXPERT_RING = 16
EXPERT_OUT_RING = 16
EXPERT_GROUP = 4
EXPERT_WEIGHT_RING = 4
ROW_TILE = 4
NEG_INF = float("-inf")


def _cparams(sem, vmem_mb):
    return pltpu.CompilerParams(dimension_semantics=sem, vmem_limit_bytes=vmem_mb * 1024 * 1024)


def _inproj_kernel(x_ref, w_ref, cos_ref, sa_ref, sb_ref, o_ref, *, n_rot_cols):
    xb = x_ref[...].astype(BF16)
    cosf = cos_ref[...]
    sa = sa_ref[...]
    sb = sb_ref[...]
    width = o_ref.shape[1]
    chunk = 512
    for c in range(width // chunk):
        r = jnp.dot(xb, w_ref[:, c * chunk:(c + 1) * chunk], preferred_element_type=F32)
        if c * chunk < n_rot_cols:
            parts = []
            for s in range(chunk // LANES):
                t = r[:, s * LANES:(s + 1) * LANES]
                parts.append(t * cosf + pltpu.roll(t, LANES - ROT_DIM // 2, 1) * sa
                             + pltpu.roll(t, ROT_DIM // 2, 1) * sb)
            r = jnp.concatenate(parts, axis=1)
        o_ref[:, c * chunk:(c + 1) * chunk] = r


def _rope_lane_tables(seq):
    half = ROT_DIM // 2
    inv_freq = jnp.power(jnp.float32(ROPE_THETA), -jnp.arange(half, dtype=F32) / half)
    ang = jnp.arange(seq, dtype=F32)[:, None] * inv_freq[None, :]
    cos, sin = jnp.cos(ang), jnp.sin(ang)
    rest = HEAD_DIM - ROT_DIM
    cos_h = jnp.concatenate([cos, cos, jnp.ones((seq, rest), F32)], axis=1)
    sa_h = jnp.concatenate([-sin, jnp.zeros((seq, half + rest), F32)], axis=1)
    sb_h = jnp.concatenate([jnp.zeros((seq, half), F32), sin, jnp.zeros((seq, rest), F32)], axis=1)
    rep = LANES // HEAD_DIM
    return tuple(jnp.tile(t, (1, rep)) for t in (cos_h, sa_h, sb_h))


def _inproj(x2d, w_in_bf, seq):
    n_tok, d = x2d.shape
    width = w_in_bf.shape[1]
    tm = 512
    cosf, sa, sb = _rope_lane_tables(seq)
    tab_spec = pl.BlockSpec((tm, LANES), lambda i: (i % (seq // tm), 0))
    return pl.pallas_call(
        functools.partial(_inproj_kernel, n_rot_cols=2 * ATT_WIDTH),
        out_shape=jax.ShapeDtypeStruct((n_tok, width), F32),
        grid=(n_tok // tm,),
        in_specs=[pl.BlockSpec((tm, d), lambda i: (i, 0)),
                  pl.BlockSpec((d, width), lambda i: (0, 0)),
                  tab_spec, tab_spec, tab_spec],
        out_specs=pl.BlockSpec((tm, width), lambda i: (i, 0)),
        compiler_params=_cparams(("parallel",), 48),
        name="inproj",
    )(x2d, w_in_bf, cosf, sa, sb)


def _attn_kernel(q_ref, k_ref, v_ref, o_ref, qs_ref, ks_ref, vs_ref, tmp_ref, ob_ref, lb_ref, band_ref,
                 first_ref, *, seq):
    blk = ATT_BLOCK
    lane = lax.broadcasted_iota(jnp.int32, (1, LANES), 1)
    head0 = lane < HEAD_DIM
    scale = HEAD_DIM ** -0.5
    d1, d2 = DILATIONS[1], DILATIONS[2]
    assert DILATIONS[0] == 1 and d2 == d1 * d1
    seg = seq // d1
    sub = seg // d1

    qi = lax.broadcasted_iota(jnp.int32, (blk, 2 * blk), 0)
    kj = lax.broadcasted_iota(jnp.int32, (blk, 2 * blk), 1)
    dist = qi + blk - kj
    band_ref[...] = jnp.where((dist >= 0) & (dist <= blk), 0.0, NEG_INF)
    first_ref[...] = jnp.where((dist >= 0) & (kj >= blk), 0.0, NEG_INF)

    n_class = (1, d1, d2)
    class_len = (seq, seg, sub)
    base = [0]
    for c in range(len(DILATIONS)):
        base.append(base[c] + n_class[c] * (class_len[c] + blk))

    def kv_row0(c, g):
        return base[c] + g * (class_len[c] + blk)

    qs_ref[0] = (q_ref[...] * scale).astype(BF16)
    for a in range(d1):
        x = q_ref[pl.ds(a, seg, stride=d1), :] * scale
        tmp_ref[a * seg:(a + 1) * seg, :] = x
        qs_ref[1, a * seg:(a + 1) * seg, :] = x.astype(BF16)
    for g in range(d2):
        qs_ref[2, g * sub:(g + 1) * sub, :] = tmp_ref[pl.ds((g // d1) * seg + g % d1, sub, stride=d1),
                                                      :].astype(BF16)
    for src_ref, dst_ref in ((k_ref, ks_ref), (v_ref, vs_ref)):
        for c in range(len(DILATIONS)):
            for g in range(n_class[c]):
                dst_ref[kv_row0(c, g):kv_row0(c, g) + blk, :] = jnp.zeros((blk, LANES), BF16)
        dst_ref[kv_row0(0, 0) + blk:kv_row0(0, 0) + blk + seq, :] = src_ref[...].astype(BF16)
        for a in range(d1):
            x = src_ref[pl.ds(a, seg, stride=d1), :]
            tmp_ref[a * seg:(a + 1) * seg, :] = x
            dst_ref[kv_row0(1, a) + blk:kv_row0(1, a) + blk + seg, :] = x.astype(BF16)
        for g in range(d2):
            dst_ref[kv_row0(2, g) + blk:kv_row0(2, g) + blk + sub, :] = tmp_ref[
                pl.ds((g // d1) * seg + g % d1, sub, stride=d1), :].astype(BF16)

    def one_block(c, g, n, out_rows, bias_ref):
        q = qs_ref[c, pl.ds(aligned(g * class_len[c] + n * blk), blk), :]
        kv_rows = pl.ds(aligned(kv_row0(c, g) + n * blk), 2 * blk)
        kk = ks_ref[kv_rows, :]
        vv = vs_ref[kv_rows, :]
        outs = []
        lses = []
        for h in range(LANES // HEAD_DIM):
            hm = head0 if h == 0 else jnp.logical_not(head0)
            qh = jnp.where(hm, q, jnp.zeros_like(q))
            s = lax.dot_general(qh, kk, (((1,), (1,)), ((), ())), preferred_element_type=F32)
            s = s + bias_ref[...]
            m = jnp.max(s, axis=-1, keepdims=True)
            p = jnp.exp(s - m)
            den = jnp.sum(p, axis=-1, keepdims=True)
            outs.append(jnp.dot(p.astype(BF16), vv, preferred_element_type=F32) / den)
            lses.append(m + jnp.log(den))
        ob_ref[c, out_rows, :] = jnp.where(head0, outs[0], outs[1])
        lb_ref[c, out_rows, :] = jnp.where(head0, lses[0], lses[1])

    def run_blocks(n_blocks, fn):
        group = max(g for g in range(1, ATT_GROUP + 1) if n_blocks % g == 0)
        if n_blocks == group:
            for g in range(group):
                fn(g)
            return
        def body(it, carry):
            for g in range(group):
                fn(it * group + g)
            return carry
        lax.fori_loop(0, n_blocks // group, body, 0)

    def aligned(x):
        return x if isinstance(x, int) else pl.multiple_of(x, blk)

    one_block(0, 0, 0, pl.ds(0, blk), first_ref)
    run_blocks(seq // blk - 1,
               lambda i: one_block(0, 0, i + 1, pl.ds(aligned((i + 1) * blk), blk), band_ref))

    nb1 = seg // blk
    run_blocks(d1, lambda a: one_block(1, a, 0, pl.ds(a, blk, stride=d1), first_ref))
    def later1(i):
        a = i // (nb1 - 1)
        n = i - a * (nb1 - 1) + 1
        one_block(1, a, n, pl.ds(a + n * (d1 * blk), blk, stride=d1), band_ref)
    run_blocks(d1 * (nb1 - 1), later1)

    assert sub == blk
    def only2(g):
        a = g // d1
        one_block(2, g, 0, pl.ds(a + d1 * (g - a * d1), blk, stride=d2), first_ref)
    run_blocks(d2, only2)

    rc = 256
    def merge(i, carry):
        sl = pl.ds(pl.multiple_of(i * rc, rc), rc)
        l0 = lb_ref[0, sl, :]
        l1 = lb_ref[1, sl, :]
        l2 = lb_ref[2, sl, :]
        mx = jnp.maximum(jnp.maximum(l0, l1), l2)
        e0 = jnp.exp(l0 - mx)
        e1 = jnp.exp(l1 - mx)
        e2 = jnp.exp(l2 - mx)
        o_ref[sl, :] = ((e0 * ob_ref[0, sl, :] + e1 * ob_ref[1, sl, :] + e2 * ob_ref[2, sl, :])
                        / (e0 + e1 + e2))
        return carry
    lax.fori_loop(0, seq // rc, merge, 0)


def _attention(proj, bsz, seq):
    n_tok = proj.shape[0]
    pairs = ATT_WIDTH // LANES
    assert seq % (ATT_BLOCK * max(DILATIONS)) == 0
    kv_rows = sum(seq + d * ATT_BLOCK for d in DILATIONS)
    blk = (seq, LANES)
    return pl.pallas_call(
        functools.partial(_attn_kernel, seq=seq),
        out_shape=jax.ShapeDtypeStruct((n_tok, ATT_WIDTH), F32),
        grid=(bsz, pairs),
        in_specs=[pl.BlockSpec(blk, lambda b, h: (b, h)),
                  pl.BlockSpec(blk, lambda b, h: (b, pairs + h)),
                  pl.BlockSpec(blk, lambda b, h: (b, 2 * pairs + h))],
        out_specs=pl.BlockSpec(blk, lambda b, h: (b, h)),
        scratch_shapes=[pltpu.VMEM((len(DILATIONS), seq, LANES), BF16),
                        pltpu.VMEM((kv_rows, LANES), BF16),
                        pltpu.VMEM((kv_rows, LANES), BF16),
                        pltpu.VMEM((seq, LANES), F32),
                        pltpu.VMEM((len(DILATIONS), seq, LANES), F32),
                        pltpu.VMEM((len(DILATIONS), seq, LANES), F32),
                        pltpu.VMEM((ATT_BLOCK, 2 * ATT_BLOCK), F32),
                        pltpu.VMEM((ATT_BLOCK, 2 * ATT_BLOCK), F32)],
        compiler_params=_cparams(("parallel", "parallel"), 40),
        name="attn",
    )(proj, proj, proj)


def _s5_kernel(u_ref, bm_ref, lam_ref, cm_ref, dk_ref, o_ref, us_ref, st_ref, ys_ref, carry_ref, *, tc):
    bsz = u_ref.shape[0]
    half = st_ref.shape[1] // 2
    slab_rows = S5_SLAB * bsz
    n_slabs = tc // S5_SLAB

    @pl.when(pl.program_id(1) == 0)
    def _():
        carry_ref[...] = jnp.zeros_like(carry_ref)

    for b in range(bsz):
        us_ref[pl.ds(b, tc, stride=bsz), :] = u_ref[b]

    bm = bm_ref[0]
    cm = cm_ref[0]
    lam = lam_ref[0]
    lam_re = lam[:, :half]
    lam_im = lam[:, half:]

    def slab(s):
        return slice(s * slab_rows, (s + 1) * slab_rows)

    def project_in(s):
        st_ref[slab(s), :] = jnp.dot(us_ref[slab(s), :].astype(BF16), bm, preferred_element_type=F32)

    def project_out(s):
        ys_ref[slab(s), :] = jnp.dot(st_ref[slab(s), :].astype(BF16), cm, preferred_element_type=F32)

    def scan(s, xr, xi):
        for t in range(S5_SLAB):
            sl = slice(s * slab_rows + t * bsz, s * slab_rows + (t + 1) * bsz)
            xr, xi = (lam_re * xr - lam_im * xi + st_ref[sl, :half],
                      lam_re * xi + lam_im * xr + st_ref[sl, half:])
            st_ref[sl, :half] = xr
            st_ref[sl, half:] = xi
        return xr, xi

    xr, xi = carry_ref[:, :half], carry_ref[:, half:]
    project_in(0)
    for s in range(n_slabs):
        if s + 1 < n_slabs:
            project_in(s + 1)
        xr, xi = scan(s, xr, xi)
        if s >= 1:
            project_out(s - 1)
    project_out(n_slabs - 1)
    carry_ref[:, :half] = xr
    carry_ref[:, half:] = xi

    dk = dk_ref[...]
    for b in range(bsz):
        o_ref[b] = ys_ref[pl.ds(b, tc, stride=bsz), :] + dk * u_ref[b]


def _s5_params(lam_re, lam_im, log_step, b_re, b_im, c_re, c_im, bsz):
    groups = lam_re.shape[0]
    gpc = LANES // SSM_CH
    n_chunks = groups // gpc
    lam = lax.complex(lam_re.astype(F32), lam_im.astype(F32))
    step = jnp.exp(log_step.astype(F32))[:, None]
    lam_bar = jnp.exp(lam * step)
    bmat = lax.complex(b_re.astype(F32), b_im.astype(F32))
    b_bar = ((lam_bar - 1.0) / lam)[..., None] * bmat
    eye = jnp.eye(gpc, dtype=F32)

    def block_diag_in(t):
        t = t.reshape(n_chunks, gpc, SSM_STATE, SSM_CH)
        return jnp.einsum('ngpc,gh->ngchp', t, eye).reshape(n_chunks, gpc * SSM_CH, gpc * SSM_STATE)

    def block_diag_out(t):
        t = t.reshape(n_chunks, gpc, SSM_CH, SSM_STATE)
        return jnp.einsum('ngcp,gh->ngphc', t, eye).reshape(n_chunks, gpc * SSM_STATE, gpc * SSM_CH)

    bm = jnp.concatenate([block_diag_in(b_bar.real), block_diag_in(b_bar.imag)], axis=2).astype(BF16)
    cm = jnp.concatenate([block_diag_out(c_re.astype(F32)), block_diag_out(-c_im.astype(F32))],
                         axis=1).astype(BF16)
    lam_row = jnp.concatenate([lam_bar.real.reshape(n_chunks, gpc * SSM_STATE),
                               lam_bar.imag.reshape(n_chunks, gpc * SSM_STATE)], axis=1)
    lam_t = jnp.broadcast_to(lam_row[:, None, :], (n_chunks, bsz, 2 * gpc * SSM_STATE))
    return bm, lam_t, cm, n_chunks


def _s5(proj3, u_col0, lam_re, lam_im, log_step, b_re, b_im, c_re, c_im, d_skip):
    bsz, seq, _ = proj3.shape
    assert bsz == SUBLANES
    bm, lam_t, cm, n_chunks = _s5_params(lam_re, lam_im, log_step, b_re, b_im, c_re, c_im, bsz)
    width = n_chunks * LANES
    tc = 256
    st_cols = bm.shape[2]
    ublk0 = u_col0 // LANES
    return pl.pallas_call(
        functools.partial(_s5_kernel, tc=tc),
        out_shape=jax.ShapeDtypeStruct((bsz, seq, width), F32),
        grid=(n_chunks, seq // tc),
        in_specs=[pl.BlockSpec((bsz, tc, LANES), lambda c, t: (0, t, ublk0 + c)),
                  pl.BlockSpec((1, LANES, st_cols), lambda c, t: (c, 0, 0)),
                  pl.BlockSpec((1, bsz, st_cols), lambda c, t: (c, 0, 0)),
                  pl.BlockSpec((1, st_cols, LANES), lambda c, t: (c, 0, 0)),
                  pl.BlockSpec((1, LANES), lambda c, t: (0, c))],
        out_specs=pl.BlockSpec((bsz, tc, LANES), lambda c, t: (0, t, c)),
        scratch_shapes=[pltpu.VMEM((tc * bsz, LANES), F32),
                        pltpu.VMEM((tc * bsz, st_cols), F32),
                        pltpu.VMEM((tc * bsz, LANES), F32),
                        pltpu.VMEM((bsz, st_cols), F32)],
        compiler_params=_cparams(("arbitrary", "arbitrary"), 40),
        name="s5",
    )(proj3, bm, lam_t, cm, d_skip.reshape(1, width).astype(F32))


def _layer_norm(v, g, b):
    mu = jnp.mean(v, axis=-1, keepdims=True)
    var = jnp.mean(jnp.square(v - mu), axis=-1, keepdims=True)
    return (v - mu) * lax.rsqrt(var + LN_EPS) * g + b


def _rms_norm(v, g):
    return v * lax.rsqrt(jnp.mean(jnp.square(v), axis=-1, keepdims=True) + RMS_EPS) * g


def _mixout_kernel(att_ref, ssm_ref, x_ref, wglu_ref, bglu_ref, ag_ref, sg_ref, wout_ref, g_ref, b_ref,
                   o_ref, *, alpha):
    y = jax.nn.gelu(ssm_ref[...])
    z = jnp.dot(y.astype(BF16), wglu_ref[...], preferred_element_type=F32) + bglu_ref[...]
    o_ssm = y * jax.nn.sigmoid(z)
    a = _rms_norm(att_ref[...], ag_ref[...]).astype(BF16)
    s = _rms_norm(o_ssm, sg_ref[...]).astype(BF16)
    wa = att_ref.shape[1]
    mix = (jnp.dot(a, wout_ref[:wa, :], preferred_element_type=F32)
           + jnp.dot(s, wout_ref[wa:, :], preferred_element_type=F32))
    o_ref[...] = _layer_norm(alpha * x_ref[...] + mix, g_ref[...], b_ref[...])


def _mixout(o_att, y_ssm, x2d, w_glu, b_glu, att_g, ssm_g, w_out, ln_g, ln_b, alpha):
    n_tok, d = x2d.shape
    wa = o_att.shape[1]
    ws = y_ssm.shape[1]
    tm = 512
    row = lambda w: pl.BlockSpec((tm, w), lambda i: (i, 0))
    full = lambda a: pl.BlockSpec(a.shape, lambda i: (0,) * a.ndim)
    args = (o_att, y_ssm, x2d, w_glu.astype(BF16), b_glu.reshape(1, ws), att_g.reshape(1, wa),
            ssm_g.reshape(1, ws), w_out.astype(BF16), ln_g.reshape(1, d), ln_b.reshape(1, d))
    return pl.pallas_call(
        functools.partial(_mixout_kernel, alpha=alpha),
        out_shape=jax.ShapeDtypeStruct((n_tok, d), F32),
        grid=(n_tok // tm,),
        in_specs=[row(wa), row(ws), row(d)] + [full(a) for a in args[3:]],
        out_specs=row(d),
        compiler_params=_cparams(("parallel",), 32),
        name="mixout",
    )(*args)


def _split_bf16(v):
    hi = v.astype(BF16)
    lo = (v - hi.astype(F32)).astype(BF16)
    return hi, lo


def _router_kernel(h_ref, wt_ref, bias_ref, e_ref, g_ref, r_ref, cnt_ref, run_ref):
    tm = h_ref.shape[0]
    n_exp = wt_ref.shape[0]
    gsz = n_exp // N_EXPERT_GROUPS

    @pl.when(pl.program_id(0) == 0)
    def _():
        run_ref[...] = jnp.zeros_like(run_ref)

    w_hi, w_lo = _split_bf16(wt_ref[...])
    h_hi, h_lo = _split_bf16(h_ref[...])
    nt = (((1,), (1,)), ((), ()))
    logits = (lax.dot_general(w_hi, h_hi, nt, preferred_element_type=F32)
              + lax.dot_general(w_hi, h_lo, nt, preferred_element_type=F32)
              + lax.dot_general(w_lo, h_hi, nt, preferred_element_type=F32))
    scores = jax.nn.sigmoid(logits)
    choice = scores + bias_ref[:, 0:1]

    gio = lax.broadcasted_iota(jnp.int32, (gsz, tm), 0).astype(F32)
    gscore = []
    for g in range(N_EXPERT_GROUPS):
        cg = choice[g * gsz:(g + 1) * gsz, :]
        m1 = jnp.max(cg, axis=0, keepdims=True)
        i1 = jnp.min(jnp.where(cg == m1, gio, float(gsz)), axis=0, keepdims=True)
        m2 = jnp.max(jnp.where(gio == i1, NEG_INF, cg), axis=0, keepdims=True)
        gscore.append(m1 + m2)
    masked = []
    for g in range(N_EXPERT_GROUPS):
        beat = jnp.zeros((1, tm), F32)
        for o in range(N_EXPERT_GROUPS):
            if o == g:
                continue
            wins = (gscore[o] >= gscore[g]) if o < g else (gscore[o] > gscore[g])
            beat = beat + jnp.where(wins, 1.0, 0.0)
        keep = beat < float(TOPK_GROUPS)
        masked.append(jnp.where(keep, choice[g * gsz:(g + 1) * gsz, :], NEG_INF))
    cur = jnp.concatenate(masked, axis=0)

    eio = lax.broadcasted_iota(jnp.int32, (n_exp, tm), 0).astype(F32)
    idxs = []
    gates = []
    candidates = cur
    for _ in range(TOP_K):
        m = jnp.max(cur, axis=0, keepdims=True)
        idx = jnp.min(jnp.where(cur == m, eio, float(n_exp)), axis=0, keepdims=True)
        hit = eio == idx
        idxs.append(idx)
        gates.append(jnp.sum(jnp.where(hit, scores, 0.0), axis=0, keepdims=True))
        cur = jnp.where(hit, NEG_INF, cur)
    onehot = jnp.where(cur != candidates, 1.0, 0.0)
    gate = jnp.concatenate(gates, axis=0)
    gate = ROUTED_SCALE * gate / (jnp.sum(gate, axis=0, keepdims=True) + 1e-20)

    si = lax.broadcasted_iota(jnp.int32, (tm, tm), 0)
    ti = lax.broadcasted_iota(jnp.int32, (tm, tm), 1)
    upper = jnp.where(si < ti, 1.0, 0.0).astype(BF16)
    before = jnp.dot(onehot.astype(BF16), upper, preferred_element_type=F32) + run_ref[:, 0:1]
    ranks = [jnp.sum(jnp.where(eio == idx, before, 0.0), axis=0, keepdims=True) for idx in idxs]

    e_ref[...] = jnp.concatenate(idxs, axis=0).astype(jnp.int32)
    g_ref[...] = gate
    r_ref[...] = jnp.concatenate(ranks, axis=0).astype(jnp.int32)
    run_ref[...] = run_ref[...] + jnp.sum(onehot, axis=1, keepdims=True)
    cnt_ref[...] = run_ref[...]


def _router(h, router_w, router_bias):
    n_tok, d = h.shape
    n_exp = router_w.shape[1]
    tm = 512
    wt = router_w.astype(F32).T
    bias = jnp.broadcast_to(router_bias.astype(F32)[:, None], (n_exp, LANES))
    tok = pl.BlockSpec((TOP_K, tm), lambda i: (0, i))
    return pl.pallas_call(
        _router_kernel,
        out_shape=(jax.ShapeDtypeStruct((TOP_K, n_tok), jnp.int32),
                   jax.ShapeDtypeStruct((TOP_K, n_tok), F32),
                   jax.ShapeDtypeStruct((TOP_K, n_tok), jnp.int32),
                   jax.ShapeDtypeStruct((n_exp, LANES), F32)),
        grid=(n_tok // tm,),
        in_specs=[pl.BlockSpec((tm, d), lambda i: (i, 0)),
                  pl.BlockSpec((n_exp, d), lambda i: (0, 0)),
                  pl.BlockSpec((n_exp, LANES), lambda i: (0, 0))],
        out_specs=(tok, tok, tok, pl.BlockSpec((n_exp, LANES), lambda i: (0, 0))),
        scratch_shapes=[pltpu.VMEM((n_exp, LANES), F32)],
        compiler_params=_cparams(("arbitrary",), 32),
        name="router",
    )(h, wt, bias)


def _dest_kernel(e_ref, r_ref, st_ref, d_ref):
    n_exp = st_ref.shape[0]
    tm = e_ref.shape[1]
    eio = lax.broadcasted_iota(jnp.int32, (n_exp, tm), 0)
    start = st_ref[:, 0:1]
    rows = [jnp.sum(jnp.where(eio == e_ref[k:k + 1, :], start, 0.0), axis=0, keepdims=True)
            for k in range(TOP_K)]
    d_ref[...] = jnp.concatenate(rows, axis=0).astype(jnp.int32) + r_ref[...]


def _dest(top_e, rank, starts):
    n_tok = top_e.shape[1]
    n_exp = starts.shape[0]
    tm = 512
    st = jnp.broadcast_to(starts.astype(F32)[:, None], (n_exp, LANES))
    tok = pl.BlockSpec((TOP_K, tm), lambda i: (0, i))
    return pl.pallas_call(
        _dest_kernel,
        out_shape=jax.ShapeDtypeStruct((TOP_K, n_tok), jnp.int32),
        grid=(n_tok // tm,),
        in_specs=[tok, tok, pl.BlockSpec((n_exp, LANES), lambda i: (0, 0))],
        out_specs=tok,
        compiler_params=_cparams(("parallel",), 32),
        name="dest",
    )(top_e, rank, st)


def _pack_bf16_pairs(val):
    half = val.shape[1] // 2
    lo = pltpu.bitcast(val[:, :half].astype(BF16).astype(F32), U32)
    hi = pltpu.bitcast(val[:, half:].astype(BF16).astype(F32), U32)
    return (lo >> 16) | (hi & jnp.uint32(0xFFFF0000))


def _unpack_bf16_pairs(words):
    lo = pltpu.bitcast(words << 16, F32)
    hi = pltpu.bitcast(words & jnp.uint32(0xFFFF0000), F32)
    return jnp.concatenate([lo, hi], axis=1)


def _to_row_tiles(dst_ref, slot, val):
    rows = val.shape[0]
    words = _pack_bf16_pairs(val)
    for j in range(ROW_TILE):
        dst_ref[slot, pl.ds(j, rows, stride=ROW_TILE), :] = words[:, j * LANES:(j + 1) * LANES]


def _row_tile_words(src_ref, idx, rows):
    return jnp.concatenate([src_ref[(*idx, pl.ds(j, rows, stride=ROW_TILE), slice(None))]
                            for j in range(ROW_TILE)], axis=1)


def _row_tile(r):
    return pl.ds(pl.multiple_of(r * ROW_TILE, ROW_TILE), ROW_TILE)


def _dispatch_kernel(pstart_ref, cnt_ref, nblk_ref, dest_ref, h_ref, xs_ref, ht_ref, zero_ref, sem, zsem):
    tm = h_ref.shape[0]
    i = pl.program_id(0)
    n_steps = pl.num_programs(0)
    cur = i % 2
    _to_row_tiles(ht_ref, cur, h_ref[...])
    zero_ref[...] = jnp.zeros_like(zero_ref)

    def issue(t, carry):
        for k in range(TOP_K):
            pltpu.make_async_copy(ht_ref.at[cur, _row_tile(t)], xs_ref.at[_row_tile(dest_ref[t * TOP_K + k])],
                                  sem.at[cur]).start(priority=k % 2)
        return carry
    lax.fori_loop(0, tm, issue, 0, unroll=2)

    def drain(slot):
        for k in range(TOP_K):
            pltpu.make_async_copy(ht_ref.at[slot], xs_ref.at[pl.ds(0, tm * ROW_TILE)], sem.at[slot]).wait()

    @pl.when(i > 0)
    def _():
        drain(1 - cur)

    @pl.when(i == n_steps - 1)
    def _():
        drain(cur)

    n_exp = cnt_ref.shape[0]
    per_step = pl.cdiv(n_exp, n_steps)
    sizes = [1 << s for s in reversed(range(EXPERT_ROWS.bit_length()))]

    def pad_copies(j, wait):
        e = jnp.minimum(i * per_step + j, n_exp - 1)
        n_pad = jnp.where(i * per_step + j < n_exp, (EXPERT_ROWS - cnt_ref[e] % EXPERT_ROWS) % EXPERT_ROWS, 0)
        row = pstart_ref[e] + cnt_ref[e]
        for size in sizes:
            @pl.when(n_pad & size != 0)
            def _(row=row, size=size):
                c = pltpu.make_async_copy(
                    zero_ref.at[pl.ds(0, size * ROW_TILE)],
                    xs_ref.at[pl.ds(pl.multiple_of(row * ROW_TILE, ROW_TILE), size * ROW_TILE)], zsem)
                c.wait() if wait else c.start()
            row = row + (n_pad & size)

    def pad_start(j, carry):
        pad_copies(j, False)
        return carry

    def pad_wait(j, carry):
        pad_copies(j, True)
        return carry
    lax.fori_loop(0, per_step, pad_start, 0)
    lax.fori_loop(0, per_step, pad_wait, 0)

    @pl.when(i == n_steps - 1)
    def _():
        _zero_tail_blocks(zero_ref, xs_ref, nblk_ref[0], zsem)


def _zero_tail_blocks(zero_ref, out_ref, n_blk, sem):
    trows = zero_ref.shape[0]

    def copy(b):
        return pltpu.make_async_copy(zero_ref, out_ref.at[pl.ds(pl.multiple_of(b * trows, trows), trows)], sem)

    def start(b, carry):
        copy(b).start()
        return carry

    def wait(b, carry):
        copy(b).wait()
        return carry
    total = out_ref.shape[0] // trows
    lax.fori_loop(n_blk, total, start, 0)
    lax.fori_loop(n_blk, total, wait, 0)


def _dispatch(h, dest, pad_start, counts, n_blk):
    n_tok, d = h.shape
    n_exp = counts.shape[0]
    assert d == 2 * ROW_TILE * LANES
    tm = MOE_TOKEN_TILE
    n_rows = (pl.cdiv(n_tok * TOP_K, EXPERT_ROWS) + n_exp) * EXPERT_ROWS
    grid_spec = pltpu.PrefetchScalarGridSpec(
        num_scalar_prefetch=3,
        grid=(n_tok // tm,),
        in_specs=[pl.BlockSpec((TOP_K * tm,), lambda i, *_: (i,), memory_space=pltpu.SMEM),
                  pl.BlockSpec((tm, d), lambda i, *_: (i, 0))],
        out_specs=pl.BlockSpec(memory_space=pl.ANY),
        scratch_shapes=[pltpu.VMEM((2, tm * ROW_TILE, LANES), U32),
                        pltpu.VMEM((EXPERT_ROWS * ROW_TILE, LANES), U32),
                        pltpu.SemaphoreType.DMA((2,)), pltpu.SemaphoreType.DMA(())],
    )
    return pl.pallas_call(
        _dispatch_kernel,
        out_shape=jax.ShapeDtypeStruct((n_rows * ROW_TILE, LANES), U32),
        grid_spec=grid_spec,
        compiler_params=_cparams(("arbitrary",), 32),
        name="dispatch",
    )(pad_start, counts, n_blk, dest, h)


def _experts_kernel(bstart_ref, bend_ref, nblk_ref, xs_ref, wgu_hbm, wdn_hbm, ys_ref,
                    xbuf, ybuf, act_ref, wgu_f32, wdn_f32, wgu_bf, wdn_bf, zero_ref, xsem, ysem, wsem, zsem):
    e = pl.program_id(0)
    n_blk = nblk_ref[0]
    trows = xbuf.shape[1]
    rows = trows // ROW_TILE
    ff = wdn_bf.shape[0]
    b0 = bstart_ref[e]
    b1 = bend_ref[e]

    def block_rows(b):
        return pl.ds(pl.multiple_of(b * trows, trows), trows)

    def x_copy(b):
        slot = b % EXPERT_RING
        return pltpu.make_async_copy(xs_ref.at[block_rows(b)], xbuf.at[slot], xsem.at[slot])

    def y_copy(b):
        slot = b % EXPERT_OUT_RING
        return pltpu.make_async_copy(ybuf.at[slot], ys_ref.at[block_rows(b)], ysem.at[slot])

    @pl.when(e == 0)
    def _():
        for i in range(EXPERT_RING):
            @pl.when(i < n_blk)
            def _():
                x_copy(i).start(priority=1)

    n_exp = pl.num_programs(0)
    wslot = e % EXPERT_WEIGHT_RING

    def w_copies(x, slot):
        return (pltpu.make_async_copy(wgu_hbm.at[x], wgu_f32.at[slot], wsem.at[0, slot]),
                pltpu.make_async_copy(wdn_hbm.at[x], wdn_f32.at[slot], wsem.at[1, slot]))

    @pl.when(e == 0)
    def _():
        for i in range(EXPERT_WEIGHT_RING):
            @pl.when(i < n_exp)
            def _():
                for c in w_copies(i, i):
                    c.start()

    for c in w_copies(e, wslot):
        c.wait()
    wgu_bf[...] = wgu_f32[wslot].astype(BF16)
    wdn_bf[...] = wdn_f32[wslot].astype(BF16)

    @pl.when(e + EXPERT_WEIGHT_RING < n_exp)
    def _():
        for c in w_copies(e + EXPERT_WEIGHT_RING, wslot):
            c.start()

    def up_wait(blocks):
        for b in blocks:
            x_copy(b).wait()

    def up_compute(blocks):
        for b in blocks:
            xb = _unpack_bf16_pairs(_row_tile_words(xbuf, (b % EXPERT_RING,), rows)).astype(BF16)
            gu = jnp.dot(xb, wgu_bf[...], preferred_element_type=F32)
            act_ref[b - b0] = (jax.nn.silu(gu[:, :ff]) * gu[:, ff:]).astype(BF16)

    def up_start(blocks):
        for b in blocks:
            @pl.when(b + EXPERT_RING < n_blk)
            def _():
                x_copy(b + EXPERT_RING).start(priority=1)

    def down_wait(blocks):
        for b in blocks:
            @pl.when(b >= EXPERT_OUT_RING)
            def _():
                y_copy(b - EXPERT_OUT_RING).wait()

    def down_compute(blocks):
        for b in blocks:
            _to_row_tiles(ybuf, b % EXPERT_OUT_RING,
                          jnp.dot(act_ref[b - b0], wdn_bf[...], preferred_element_type=F32))

    def down_start(blocks):
        for b in blocks:
            y_copy(b).start(priority=1)
        last = blocks[-1]

        @pl.when(last == n_blk - 1)
        def _():
            for i in range(EXPERT_OUT_RING):
                @pl.when(last >= i)
                def _():
                    y_copy(last - i).wait()

    def up(blocks):
        up_wait(blocks)
        up_compute(blocks)
        up_start(blocks)

    def down(blocks):
        down_wait(blocks)
        down_compute(blocks)
        down_start(blocks)

    n_mine = b1 - b0

    def run_groups(fn):
        def body(p, carry):
            fn(tuple(b0 + EXPERT_GROUP * p + j for j in range(EXPERT_GROUP)))
            return carry
        lax.fori_loop(0, n_mine // EXPERT_GROUP, body, 0)
        size = EXPERT_GROUP // 2
        while size >= 1:
            @pl.when(n_mine & size != 0)
            def _(size=size):
                start = b0 + (n_mine // (2 * size)) * (2 * size)
                fn(tuple(start + j for j in range(size)))
            size //= 2

    @pl.when(n_mine == EXPERT_GROUP)
    def _():
        blocks = tuple(b0 + j for j in range(EXPERT_GROUP))
        up_wait(blocks)
        down_wait(blocks)
        up_compute(blocks)
        down_compute(blocks)
        up_start(blocks)
        down_start(blocks)

    @pl.when(n_mine != EXPERT_GROUP)
    def _():
        run_groups(up)
        run_groups(down)

    @pl.when(e == n_exp - 1)
    def _():
        zero_ref[...] = jnp.zeros_like(zero_ref)
        _zero_tail_blocks(zero_ref, ys_ref, n_blk, zsem)


def _expert_blocks(counts):
    blocks = (counts + EXPERT_ROWS - 1) // EXPERT_ROWS
    bend = jnp.cumsum(blocks)
    bstart = bend - blocks
    i32 = lambda a: a.astype(jnp.int32)
    return i32(bstart), i32(bend), i32(bend[-1]).reshape(1), i32(bstart * EXPERT_ROWS)


def _experts(xs, bstart, bend, n_blk, w_gu, w_down, n_tok):
    n_exp, d, ff2 = w_gu.shape
    ff = w_down.shape[1]
    n_rows = xs.shape[0] // ROW_TILE
    assert n_rows % EXPERT_ROWS == 0 and d == 2 * ROW_TILE * LANES
    max_blocks = pl.cdiv(n_tok, EXPERT_ROWS)
    grid_spec = pltpu.PrefetchScalarGridSpec(
        num_scalar_prefetch=3,
        grid=(n_exp,),
        in_specs=[pl.BlockSpec(memory_space=pl.ANY), pl.BlockSpec(memory_space=pl.ANY),
                  pl.BlockSpec(memory_space=pl.ANY)],
        out_specs=pl.BlockSpec(memory_space=pl.ANY),
        scratch_shapes=[pltpu.VMEM((EXPERT_RING, EXPERT_ROWS * ROW_TILE, LANES), U32),
                        pltpu.VMEM((EXPERT_OUT_RING, EXPERT_ROWS * ROW_TILE, LANES), U32),
                        pltpu.VMEM((max_blocks, EXPERT_ROWS, ff), BF16),
                        pltpu.VMEM((EXPERT_WEIGHT_RING, d, ff2), F32),
                        pltpu.VMEM((EXPERT_WEIGHT_RING, ff, d), F32),
                        pltpu.VMEM((d, ff2), BF16), pltpu.VMEM((ff, d), BF16),
                        pltpu.VMEM((EXPERT_ROWS * ROW_TILE, LANES), U32),
                        pltpu.SemaphoreType.DMA((EXPERT_RING,)),
                        pltpu.SemaphoreType.DMA((EXPERT_OUT_RING,)),
                        pltpu.SemaphoreType.DMA((2, EXPERT_WEIGHT_RING)),
                        pltpu.SemaphoreType.DMA(())],
    )
    return pl.pallas_call(
        _experts_kernel,
        out_shape=jax.ShapeDtypeStruct(xs.shape, U32),
        grid_spec=grid_spec,
        compiler_params=_cparams(("arbitrary",), 48),
        name="experts",
    )(bstart, bend, n_blk, xs, w_gu, w_down)


def _combine_kernel(dest_ref, dnext_ref, gate_ref, h_ref, ys_ref, wgu_ref, wdn_ref, g_ref, b_ref, o_ref,
                    buf_ref, routed_ref, sem, *, alpha):
    tm = h_ref.shape[0]
    i = pl.program_id(0)
    cur = i % 2
    chunk = SUBLANES

    def issue(d_ref, slot, t):
        for k in range(TOP_K):
            pltpu.make_async_copy(ys_ref.at[_row_tile(d_ref[t * TOP_K + k])], buf_ref.at[slot, k, _row_tile(t)],
                                  sem.at[slot]).start(priority=k % 2)

    @pl.when(i == 0)
    def _():
        def first(t, carry):
            issue(dest_ref, 0, t)
            return carry
        lax.fori_loop(0, tm, first, 0, unroll=2)

    for k in range(TOP_K):
        pltpu.make_async_copy(ys_ref.at[pl.ds(0, tm * ROW_TILE)], buf_ref.at[cur, k], sem.at[cur]).wait()

    def weighted_sum(c):
        tok = pl.ds(pl.multiple_of(c * chunk, chunk), chunk)
        gate = gate_ref[tok, :]
        total = None
        for k in range(TOP_K):
            words = jnp.concatenate(
                [buf_ref[cur, k, pl.ds(pl.multiple_of(c * (chunk * ROW_TILE), chunk * ROW_TILE) + j, chunk,
                                       stride=ROW_TILE), :] for j in range(ROW_TILE)], axis=1)
            term = gate[:, k:k + 1] * _unpack_bf16_pairs(words)
            total = term if total is None else total + term
        routed_ref[tok, :] = total

    @pl.when(i + 1 < pl.num_programs(0))
    def _():
        def body(c, carry):
            for t in range(chunk):
                issue(dnext_ref, 1 - cur, c * chunk + t)
            weighted_sum(c)
            return carry
        lax.fori_loop(0, tm // chunk, body, 0)

    @pl.when(i + 1 == pl.num_programs(0))
    def _():
        def body(c, carry):
            weighted_sum(c)
            return carry
        lax.fori_loop(0, tm // chunk, body, 0)

    h = h_ref[...]
    ff = wdn_ref.shape[0]
    gu = jnp.dot(h.astype(BF16), wgu_ref[...], preferred_element_type=F32)
    act = (jax.nn.silu(gu[:, :ff]) * gu[:, ff:]).astype(BF16)
    acc = alpha * h + jnp.dot(act, wdn_ref[...], preferred_element_type=F32) + routed_ref[...]
    o_ref[...] = _layer_norm(acc, g_ref[...], b_ref[...])


def _combine(h, ys, dest, gate_t, shared_w_gu, shared_w_down, ln_g, ln_b, alpha):
    n_tok, d = h.shape
    tm = MOE_TOKEN_TILE
    n_tiles = n_tok // tm
    full = lambda a: pl.BlockSpec(a.shape, lambda i: (0,) * a.ndim)
    wgu = shared_w_gu.astype(BF16)
    wdn = shared_w_down.astype(BF16)
    g2 = ln_g.reshape(1, d)
    b2 = ln_b.reshape(1, d)
    return pl.pallas_call(
        functools.partial(_combine_kernel, alpha=alpha),
        out_shape=jax.ShapeDtypeStruct((n_tok, d), F32),
        grid=(n_tiles,),
        in_specs=[pl.BlockSpec((TOP_K * tm,), lambda i: (i,), memory_space=pltpu.SMEM),
                  pl.BlockSpec((TOP_K * tm,), lambda i: (jnp.minimum(i + 1, n_tiles - 1),),
                               memory_space=pltpu.SMEM),
                  pl.BlockSpec((tm, TOP_K), lambda i: (i, 0)),
                  pl.BlockSpec((tm, d), lambda i: (i, 0)),
                  pl.BlockSpec(memory_space=pl.ANY),
                  full(wgu), full(wdn), full(g2), full(b2)],
        out_specs=pl.BlockSpec((tm, d), lambda i: (i, 0)),
        scratch_shapes=[pltpu.VMEM((2, TOP_K, tm * ROW_TILE, LANES), U32), pltpu.VMEM((tm, d), F32),
                        pltpu.SemaphoreType.DMA((2,))],
        compiler_params=_cparams(("arbitrary",), 48),
        name="combine",
    )(dest, dest, gate_t, h, ys, wgu, wdn, g2, b2)


def _moe(h, router_w, router_bias, w_gu, w_down, shared_w_gu, shared_w_down, ln_g, ln_b, alpha):
    top_e, gate, rank, cnt = _router(h, router_w, router_bias)
    counts = cnt[:, 0].astype(jnp.int32)
    bstart, bend, n_blk, pad_start = _expert_blocks(counts)
    dest = _dest(top_e, rank, pad_start)
    dest_tiles = dest.T.reshape(-1)
    xs = _dispatch(h, dest_tiles, pad_start, counts, n_blk)
    ys = _experts(xs, bstart, bend, n_blk, w_gu, w_down, h.shape[0])
    return _combine(h, ys, dest_tiles, gate.T, shared_w_gu, shared_w_down, ln_g, ln_b, alpha)


def kernel(x, w_in, att_norm_g, lam_re, lam_im, log_step, b_re, b_im, c_re, c_im, d_skip, w_glu, b_glu,
           ssm_norm_g, w_out, ln1_g, ln1_b, router_w, router_bias, w_gu, w_down, shared_w_gu,
           shared_w_down, ln2_g, ln2_b):
    bsz, seq, d = x.shape
    depth = w_in.shape[0]
    alpha = (2 * depth) ** 0.25
    h = x.reshape(bsz * seq, d)
    for i in range(depth):
        proj = _inproj(h, w_in[i].astype(BF16), seq)
        o_att = _attention(proj, bsz, seq)
        y_ssm = _s5(proj.reshape(bsz, seq, -1), 3 * ATT_WIDTH, lam_re[i], lam_im[i], log_step[i],
                    b_re[i], b_im[i], c_re[i], c_im[i], d_skip[i])
        h = _mixout(o_att, y_ssm.reshape(bsz * seq, -1), h, w_glu[i], b_glu[i], att_norm_g[i],
                    ssm_norm_g[i], w_out[i], ln1_g[i], ln1_b[i], alpha)
        h = _moe(h, router_w[i], router_bias[i], w_gu[i], w_down[i], shared_w_gu[i], shared_w_down[i],
                 ln2_g[i], ln2_b[i], alpha)
    return h.reshape(bsz, seq, d)
```

```python
import functools

import jax
import jax.numpy as jnp
from jax import lax
from jax.experimental import pallas as pl
from jax.experimental.pallas import tpu as pltpu

F32 = jnp.float32
BF16 = jnp.bfloat16
U32 = jnp.uint32

ATT_HEADS = 8
HEAD_DIM = 64
ATT_WIDTH = ATT_HEADS * HEAD_DIM
SSM_CH = 16
SSM_STATE = 64
S5_SLAB = 64
ROPE_THETA = 500000.0
ROT_DIM = HEAD_DIM // 4
DILATIONS = (1, 4, 16)
ATT_BLOCK = 128
ATT_GROUP = 16
TOP_K = 8
N_EXPERT_GROUPS = 8
TOPK_GROUPS = 4
ROUTED_SCALE = 2.5
LN_EPS = 1e-5
RMS_EPS = 1e-6

LANES = 128
SUBLANES = 8
EXPERT_ROWS = 144
MOE_TOKEN_TILE = 512
EXPERT_RING = 16
EXPERT_OUT_RING = 16
EXPERT_GROUP = 4
EXPERT_WEIGHT_RING = 4
ROW_TILE = 4
NEG_INF = float("-inf")


def _cparams(sem, vmem_mb):
    return pltpu.CompilerParams(dimension_semantics=sem, vmem_limit_bytes=vmem_mb * 1024 * 1024)


def _inproj_kernel(x_ref, w_ref, cos_ref, sa_ref, sb_ref, o_ref, *, n_rot_cols):
    xb = x_ref[...].astype(BF16)
    cosf = cos_ref[...]
    sa = sa_ref[...]
    sb = sb_ref[...]
    width = o_ref.shape[1]
    chunk = 512
    for c in range(width // chunk):
        r = jnp.dot(xb, w_ref[:, c * chunk:(c + 1) * chunk], preferred_element_type=F32)
        if c * chunk < n_rot_cols:
            parts = []
            for s in range(chunk // LANES):
                t = r[:, s * LANES:(s + 1) * LANES]
                parts.append(t * cosf + pltpu.roll(t, LANES - ROT_DIM // 2, 1) * sa
                             + pltpu.roll(t, ROT_DIM // 2, 1) * sb)
            r = jnp.concatenate(parts, axis=1)
        o_ref[:, c * chunk:(c + 1) * chunk] = r


def _rope_lane_tables(seq):
    half = ROT_DIM // 2
    inv_freq = jnp.power(jnp.float32(ROPE_THETA), -jnp.arange(half, dtype=F32) / half)
    ang = jnp.arange(seq, dtype=F32)[:, None] * inv_freq[None, :]
    cos, sin = jnp.cos(ang), jnp.sin(ang)
    rest = HEAD_DIM - ROT_DIM
    cos_h = jnp.concatenate([cos, cos, jnp.ones((seq, rest), F32)], axis=1)
    sa_h = jnp.concatenate([-sin, jnp.zeros((seq, half + rest), F32)], axis=1)
    sb_h = jnp.concatenate([jnp.zeros((seq, half), F32), sin, jnp.zeros((seq, rest), F32)], axis=1)
    rep = LANES // HEAD_DIM
    return tuple(jnp.tile(t, (1, rep)) for t in (cos_h, sa_h, sb_h))


def _inproj(x2d, w_in_bf, seq):
    n_tok, d = x2d.shape
    width = w_in_bf.shape[1]
    tm = 1024
    cosf, sa, sb = _rope_lane_tables(seq)
    tab_spec = pl.BlockSpec((tm, LANES), lambda i: (i % (seq // tm), 0))
    return pl.pallas_call(
        functools.partial(_inproj_kernel, n_rot_cols=2 * ATT_WIDTH),
        out_shape=jax.ShapeDtypeStruct((n_tok, width), F32),
        grid=(n_tok // tm,),
        in_specs=[pl.BlockSpec((tm, d), lambda i: (i, 0)),
                  pl.BlockSpec((d, width), lambda i: (0, 0)),
                  tab_spec, tab_spec, tab_spec],
        out_specs=pl.BlockSpec((tm, width), lambda i: (i, 0)),
        compiler_params=_cparams(("parallel",), 48),
        name="inproj",
    )(x2d, w_in_bf, cosf, sa, sb)


def _attn_kernel(q_ref, k_ref, v_ref, o_ref, qs_ref, ks_ref, vs_ref, tmp_ref, ob_ref, lb_ref, band_ref,
                 first_ref, *, seq):
    blk = ATT_BLOCK
    lane = lax.broadcasted_iota(jnp.int32, (1, LANES), 1)
    head0 = lane < HEAD_DIM
    scale = HEAD_DIM ** -0.5
    d1, d2 = DILATIONS[1], DILATIONS[2]
    assert DILATIONS[0] == 1 and d2 == d1 * d1
    seg = seq // d1
    sub = seg // d1

    qi = lax.broadcasted_iota(jnp.int32, (blk, 2 * blk), 0)
    kj = lax.broadcasted_iota(jnp.int32, (blk, 2 * blk), 1)
    dist = qi + blk - kj
    band_ref[...] = jnp.where((dist >= 0) & (dist <= blk), 0.0, NEG_INF)
    first_ref[...] = jnp.where((dist >= 0) & (kj >= blk), 0.0, NEG_INF)

    n_class = (1, d1, d2)
    class_len = (seq, seg, sub)
    base = [0]
    for c in range(len(DILATIONS)):
        base.append(base[c] + n_class[c] * (class_len[c] + blk))

    def kv_row0(c, g):
        return base[c] + g * (class_len[c] + blk)

    qs_ref[0] = (q_ref[...] * scale).astype(BF16)
    for a in range(d1):
        x = q_ref[pl.ds(a, seg, stride=d1), :] * scale
        tmp_ref[a * seg:(a + 1) * seg, :] = x
        qs_ref[1, a * seg:(a + 1) * seg, :] = x.astype(BF16)
    for g in range(d2):
        qs_ref[2, g * sub:(g + 1) * sub, :] = tmp_ref[pl.ds((g // d1) * seg + g % d1, sub, stride=d1),
                                                      :].astype(BF16)
    for src_ref, dst_ref in ((k_ref, ks_ref), (v_ref, vs_ref)):
        for c in range(len(DILATIONS)):
            for g in range(n_class[c]):
                dst_ref[kv_row0(c, g):kv_row0(c, g) + blk, :] = jnp.zeros((blk, LANES), BF16)
        dst_ref[kv_row0(0, 0) + blk:kv_row0(0, 0) + blk + seq, :] = src_ref[...].astype(BF16)
        for a in range(d1):
            x = src_ref[pl.ds(a, seg, stride=d1), :]
            tmp_ref[a * seg:(a + 1) * seg, :] = x
            dst_ref[kv_row0(1, a) + blk:kv_row0(1, a) + blk + seg, :] = x.astype(BF16)
        for g in range(d2):
            dst_ref[kv_row0(2, g) + blk:kv_row0(2, g) + blk + sub, :] = tmp_ref[
                pl.ds((g // d1) * seg + g % d1, sub, stride=d1), :].astype(BF16)

    def one_block(c, g, n, out_rows, bias_ref):
        q = qs_ref[c, pl.ds(aligned(g * class_len[c] + n * blk), blk), :]
        kv_rows = pl.ds(aligned(kv_row0(c, g) + n * blk), 2 * blk)
        kk = ks_ref[kv_rows, :]
        vv = vs_ref[kv_rows, :]
        outs = []
        lses = []
        for h in range(LANES // HEAD_DIM):
            hm = head0 if h == 0 else jnp.logical_not(head0)
            qh = jnp.where(hm, q, jnp.zeros_like(q))
            s = lax.dot_general(qh, kk, (((1,), (1,)), ((), ())), preferred_element_type=F32)
            s = s + bias_ref[...]
            m = jnp.max(s, axis=-1, keepdims=True)
            p = jnp.exp(s - m)
            den = jnp.sum(p, axis=-1, keepdims=True)
            outs.append(jnp.dot(p.astype(BF16), vv, preferred_element_type=F32) / den)
            lses.append(m + jnp.log(den))
        ob_ref[c, out_rows, :] = jnp.where(head0, outs[0], outs[1])
        lb_ref[c, out_rows, :] = jnp.where(head0, lses[0], lses[1])

    def run_blocks(n_blocks, fn):
        group = max(g for g in range(1, ATT_GROUP + 1) if n_blocks % g == 0)
        if n_blocks == group:
            for g in range(group):
                fn(g)
            return
        def body(it, carry):
            for g in range(group):
                fn(it * group + g)
            return carry
        lax.fori_loop(0, n_blocks // group, body, 0)

    def aligned(x):
        return x if isinstance(x, int) else pl.multiple_of(x, blk)

    one_block(0, 0, 0, pl.ds(0, blk), first_ref)
    run_blocks(seq // blk - 1,
               lambda i: one_block(0, 0, i + 1, pl.ds(aligned((i + 1) * blk), blk), band_ref))

    nb1 = seg // blk
    run_blocks(d1, lambda a: one_block(1, a, 0, pl.ds(a, blk, stride=d1), first_ref))
    def later1(i):
        a = i // (nb1 - 1)
        n = i - a * (nb1 - 1) + 1
        one_block(1, a, n, pl.ds(a + n * (d1 * blk), blk, stride=d1), band_ref)
    run_blocks(d1 * (nb1 - 1), later1)

    assert sub == blk
    def only2(g):
        a = g // d1
        one_block(2, g, 0, pl.ds(a + d1 * (g - a * d1), blk, stride=d2), first_ref)
    run_blocks(d2, only2)

    rc = 256
    def merge(i, carry):
        sl = pl.ds(pl.multiple_of(i * rc, rc), rc)
        l0 = lb_ref[0, sl, :]
        l1 = lb_ref[1, sl, :]
        l2 = lb_ref[2, sl, :]
        mx = jnp.maximum(jnp.maximum(l0, l1), l2)
        e0 = jnp.exp(l0 - mx)
        e1 = jnp.exp(l1 - mx)
        e2 = jnp.exp(l2 - mx)
        o_ref[sl, :] = ((e0 * ob_ref[0, sl, :] + e1 * ob_ref[1, sl, :] + e2 * ob_ref[2, sl, :])
                        / (e0 + e1 + e2))
        return carry
    lax.fori_loop(0, seq // rc, merge, 0)


def _attention(proj, bsz, seq):
    n_tok = proj.shape[0]
    pairs = ATT_WIDTH // LANES
    assert seq % (ATT_BLOCK * max(DILATIONS)) == 0
    kv_rows = sum(seq + d * ATT_BLOCK for d in DILATIONS)
    blk = (seq, LANES)
    return pl.pallas_call(
        functools.partial(_attn_kernel, seq=seq),
        out_shape=jax.ShapeDtypeStruct((n_tok, ATT_WIDTH), F32),
        grid=(bsz, pairs),
        in_specs=[pl.BlockSpec(blk, lambda b, h: (b, h)),
                  pl.BlockSpec(blk, lambda b, h: (b, pairs + h)),
                  pl.BlockSpec(blk, lambda b, h: (b, 2 * pairs + h))],
        out_specs=pl.BlockSpec(blk, lambda b, h: (b, h)),
        scratch_shapes=[pltpu.VMEM((len(DILATIONS), seq, LANES), BF16),
                        pltpu.VMEM((kv_rows, LANES), BF16),
                        pltpu.VMEM((kv_rows, LANES), BF16),
                        pltpu.VMEM((seq, LANES), F32),
                        pltpu.VMEM((len(DILATIONS), seq, LANES), F32),
                        pltpu.VMEM((len(DILATIONS), seq, LANES), F32),
                        pltpu.VMEM((ATT_BLOCK, 2 * ATT_BLOCK), F32),
                        pltpu.VMEM((ATT_BLOCK, 2 * ATT_BLOCK), F32)],
        compiler_params=_cparams(("parallel", "parallel"), 40),
        name="attn",
    )(proj, proj, proj)


def _s5_kernel(u_ref, bm_ref, lam_ref, cm_ref, dk_ref, o_ref, us_ref, st_ref, ys_ref, carry_ref, *, tc):
    bsz = u_ref.shape[0]
    half = st_ref.shape[1] // 2
    slab_rows = S5_SLAB * bsz
    n_slabs = tc // S5_SLAB

    @pl.when(pl.program_id(1) == 0)
    def _():
        carry_ref[...] = jnp.zeros_like(carry_ref)

    for b in range(bsz):
        us_ref[pl.ds(b, tc, stride=bsz), :] = u_ref[b]

    bm = bm_ref[0]
    cm = cm_ref[0]
    lam = lam_ref[0]
    lam_re = lam[:, :half]
    lam_im = lam[:, half:]

    def slab(s):
        return slice(s * slab_rows, (s + 1) * slab_rows)

    def project_in(s):
        st_ref[slab(s), :] = jnp.dot(us_ref[slab(s), :].astype(BF16), bm, preferred_element_type=F32)

    def project_out(s):
        ys_ref[slab(s), :] = jnp.dot(st_ref[slab(s), :].astype(BF16), cm, preferred_element_type=F32)

    def scan(s, xr, xi):
        for t in range(S5_SLAB):
            sl = slice(s * slab_rows + t * bsz, s * slab_rows + (t + 1) * bsz)
            xr, xi = (lam_re * xr - lam_im * xi + st_ref[sl, :half],
                      lam_re * xi + lam_im * xr + st_ref[sl, half:])
            st_ref[sl, :half] = xr
            st_ref[sl, half:] = xi
        return xr, xi

    xr, xi = carry_ref[:, :half], carry_ref[:, half:]
    project_in(0)
    for s in range(n_slabs):
        if s + 1 < n_slabs:
            project_in(s + 1)
        xr, xi = scan(s, xr, xi)
        if s >= 1:
            project_out(s - 1)
    project_out(n_slabs - 1)
    carry_ref[:, :half] = xr
    carry_ref[:, half:] = xi

    dk = dk_ref[...]
    for b in range(bsz):
        o_ref[b] = ys_ref[pl.ds(b, tc, stride=bsz), :] + dk * u_ref[b]


def _s5_params(lam_re, lam_im, log_step, b_re, b_im, c_re, c_im, bsz):
    groups = lam_re.shape[0]
    gpc = LANES // SSM_CH
    n_chunks = groups // gpc
    lam = lax.complex(lam_re.astype(F32), lam_im.astype(F32))
    step = jnp.exp(log_step.astype(F32))[:, None]
    lam_bar = jnp.exp(lam * step)
    bmat = lax.complex(b_re.astype(F32), b_im.astype(F32))
    b_bar = ((lam_bar - 1.0) / lam)[..., None] * bmat
    eye = jnp.eye(gpc, dtype=F32)

    def block_diag_in(t):
        t = t.reshape(n_chunks, gpc, SSM_STATE, SSM_CH)
        return jnp.einsum('ngpc,gh->ngchp', t, eye).reshape(n_chunks, gpc * SSM_CH, gpc * SSM_STATE)

    def block_diag_out(t):
        t = t.reshape(n_chunks, gpc, SSM_CH, SSM_STATE)
        return jnp.einsum('ngcp,gh->ngphc', t, eye).reshape(n_chunks, gpc * SSM_STATE, gpc * SSM_CH)

    bm = jnp.concatenate([block_diag_in(b_bar.real), block_diag_in(b_bar.imag)], axis=2).astype(BF16)
    cm = jnp.concatenate([block_diag_out(c_re.astype(F32)), block_diag_out(-c_im.astype(F32))],
                         axis=1).astype(BF16)
    lam_row = jnp.concatenate([lam_bar.real.reshape(n_chunks, gpc * SSM_STATE),
                               lam_bar.imag.reshape(n_chunks, gpc * SSM_STATE)], axis=1)
    lam_t = jnp.broadcast_to(lam_row[:, None, :], (n_chunks, bsz, 2 * gpc * SSM_STATE))
    return bm, lam_t, cm, n_chunks


def _s5(proj3, u_col0, lam_re, lam_im, log_step, b_re, b_im, c_re, c_im, d_skip):
    bsz, seq, _ = proj3.shape
    assert bsz == SUBLANES
    bm, lam_t, cm, n_chunks = _s5_params(lam_re, lam_im, log_step, b_re, b_im, c_re, c_im, bsz)
    width = n_chunks * LANES
    tc = 256
    st_cols = bm.shape[2]
    ublk0 = u_col0 // LANES
    return pl.pallas_call(
        functools.partial(_s5_kernel, tc=tc),
        out_shape=jax.ShapeDtypeStruct((bsz, seq, width), F32),
        grid=(n_chunks, seq // tc),
        in_specs=[pl.BlockSpec((bsz, tc, LANES), lambda c, t: (0, t, ublk0 + c)),
                  pl.BlockSpec((1, LANES, st_cols), lambda c, t: (c, 0, 0)),
                  pl.BlockSpec((1, bsz, st_cols), lambda c, t: (c, 0, 0)),
                  pl.BlockSpec((1, st_cols, LANES), lambda c, t: (c, 0, 0)),
                  pl.BlockSpec((1, LANES), lambda c, t: (0, c))],
        out_specs=pl.BlockSpec((bsz, tc, LANES), lambda c, t: (0, t, c)),
        scratch_shapes=[pltpu.VMEM((tc * bsz, LANES), F32),
                        pltpu.VMEM((tc * bsz, st_cols), F32),
                        pltpu.VMEM((tc * bsz, LANES), F32),
                        pltpu.VMEM((bsz, st_cols), F32)],
        compiler_params=_cparams(("arbitrary", "arbitrary"), 40),
        name="s5",
    )(proj3, bm, lam_t, cm, d_skip.reshape(1, width).astype(F32))


def _layer_norm(v, g, b):
    mu = jnp.mean(v, axis=-1, keepdims=True)
    var = jnp.mean(jnp.square(v - mu), axis=-1, keepdims=True)
    return (v - mu) * lax.rsqrt(var + LN_EPS) * g + b


def _rms_norm(v, g):
    return v * lax.rsqrt(jnp.mean(jnp.square(v), axis=-1, keepdims=True) + RMS_EPS) * g


def _mixout_kernel(att_ref, ssm_ref, x_ref, wglu_ref, bglu_ref, ag_ref, sg_ref, wout_ref, g_ref, b_ref,
                   o_ref, *, alpha):
    y = jax.nn.gelu(ssm_ref[...])
    z = jnp.dot(y.astype(BF16), wglu_ref[...], preferred_element_type=F32) + bglu_ref[...]
    o_ssm = y * jax.nn.sigmoid(z)
    a = _rms_norm(att_ref[...], ag_ref[...]).astype(BF16)
    s = _rms_norm(o_ssm, sg_ref[...]).astype(BF16)
    wa = att_ref.shape[1]
    mix = (jnp.dot(a, wout_ref[:wa, :], preferred_element_type=F32)
           + jnp.dot(s, wout_ref[wa:, :], preferred_element_type=F32))
    o_ref[...] = _layer_norm(alpha * x_ref[...] + mix, g_ref[...], b_ref[...])


def _mixout(o_att, y_ssm, x2d, w_glu, b_glu, att_g, ssm_g, w_out, ln_g, ln_b, alpha):
    n_tok, d = x2d.shape
    wa = o_att.shape[1]
    ws = y_ssm.shape[1]
    tm = 512
    row = lambda w: pl.BlockSpec((tm, w), lambda i: (i, 0))
    full = lambda a: pl.BlockSpec(a.shape, lambda i: (0,) * a.ndim)
    args = (o_att, y_ssm, x2d, w_glu.astype(BF16), b_glu.reshape(1, ws), att_g.reshape(1, wa),
            ssm_g.reshape(1, ws), w_out.astype(BF16), ln_g.reshape(1, d), ln_b.reshape(1, d))
    return pl.pallas_call(
        functools.partial(_mixout_kernel, alpha=alpha),
        out_shape=jax.ShapeDtypeStruct((n_tok, d), F32),
        grid=(n_tok // tm,),
        in_specs=[row(wa), row(ws), row(d)] + [full(a) for a in args[3:]],
        out_specs=row(d),
        compiler_params=_cparams(("parallel",), 32),
        name="mixout",
    )(*args)


def _split_bf16(v):
    hi = v.astype(BF16)
    lo = (v - hi.astype(F32)).astype(BF16)
    return hi, lo


def _router_kernel(h_ref, wt_ref, bias_ref, e_ref, g_ref, r_ref, cnt_ref, run_ref):
    tm = h_ref.shape[0]
    n_exp = wt_ref.shape[0]
    gsz = n_exp // N_EXPERT_GROUPS

    @pl.when(pl.program_id(0) == 0)
    def _():
        run_ref[...] = jnp.zeros_like(run_ref)

    w_hi, w_lo = _split_bf16(wt_ref[...])
    h_hi, h_lo = _split_bf16(h_ref[...])
    nt = (((1,), (1,)), ((), ()))
    logits = (lax.dot_general(w_hi, h_hi, nt, preferred_element_type=F32)
              + lax.dot_general(w_hi, h_lo, nt, preferred_element_type=F32)
              + lax.dot_general(w_lo, h_hi, nt, preferred_element_type=F32))
    scores = jax.nn.sigmoid(logits)
    choice = scores + bias_ref[:, 0:1]

    gio = lax.broadcasted_iota(jnp.int32, (gsz, tm), 0).astype(F32)
    gscore = []
    for g in range(N_EXPERT_GROUPS):
        cg = choice[g * gsz:(g + 1) * gsz, :]
        m1 = jnp.max(cg, axis=0, keepdims=True)
        i1 = jnp.min(jnp.where(cg == m1, gio, float(gsz)), axis=0, keepdims=True)
        m2 = jnp.max(jnp.where(gio == i1, NEG_INF, cg), axis=0, keepdims=True)
        gscore.append(m1 + m2)
    masked = []
    for g in range(N_EXPERT_GROUPS):
        beat = jnp.zeros((1, tm), F32)
        for o in range(N_EXPERT_GROUPS):
            if o == g:
                continue
            wins = (gscore[o] >= gscore[g]) if o < g else (gscore[o] > gscore[g])
            beat = beat + jnp.where(wins, 1.0, 0.0)
        keep = beat < float(TOPK_GROUPS)
        masked.append(jnp.where(keep, choice[g * gsz:(g + 1) * gsz, :], NEG_INF))
    cur = jnp.concatenate(masked, axis=0)

    eio = lax.broadcasted_iota(jnp.int32, (n_exp, tm), 0).astype(F32)
    idxs = []
    gates = []
    candidates = cur
    for _ in range(TOP_K):
        m = jnp.max(cur, axis=0, keepdims=True)
        idx = jnp.min(jnp.where(cur == m, eio, float(n_exp)), axis=0, keepdims=True)
        hit = eio == idx
        idxs.append(idx)
        gates.append(jnp.sum(jnp.where(hit, scores, 0.0), axis=0, keepdims=True))
        cur = jnp.where(hit, NEG_INF, cur)
    onehot = jnp.where(cur != candidates, 1.0, 0.0)
    gate = jnp.concatenate(gates, axis=0)
    gate = ROUTED_SCALE * gate / (jnp.sum(gate, axis=0, keepdims=True) + 1e-20)

    si = lax.broadcasted_iota(jnp.int32, (tm, tm), 0)
    ti = lax.broadcasted_iota(jnp.int32, (tm, tm), 1)
    upper = jnp.where(si < ti, 1.0, 0.0).astype(BF16)
    before = jnp.dot(onehot.astype(BF16), upper, preferred_element_type=F32) + run_ref[:, 0:1]
    ranks = [jnp.sum(jnp.where(eio == idx, before, 0.0), axis=0, keepdims=True) for idx in idxs]

    e_ref[...] = jnp.concatenate(idxs, axis=0).astype(jnp.int32)
    g_ref[...] = gate
    r_ref[...] = jnp.concatenate(ranks, axis=0).astype(jnp.int32)
    run_ref[...] = run_ref[...] + jnp.sum(onehot, axis=1, keepdims=True)
    cnt_ref[...] = run_ref[...]


def _router(h, router_w, router_bias):
    n_tok, d = h.shape
    n_exp = router_w.shape[1]
    tm = 512
    wt = router_w.astype(F32).T
    bias = jnp.broadcast_to(router_bias.astype(F32)[:, None], (n_exp, LANES))
    tok = pl.BlockSpec((TOP_K, tm), lambda i: (0, i))
    return pl.pallas_call(
        _router_kernel,
        out_shape=(jax.ShapeDtypeStruct((TOP_K, n_tok), jnp.int32),
                   jax.ShapeDtypeStruct((TOP_K, n_tok), F32),
                   jax.ShapeDtypeStruct((TOP_K, n_tok), jnp.int32),
                   jax.ShapeDtypeStruct((n_exp, LANES), F32)),
        grid=(n_tok // tm,),
        in_specs=[pl.BlockSpec((tm, d), lambda i: (i, 0)),
                  pl.BlockSpec((n_exp, d), lambda i: (0, 0)),
                  pl.BlockSpec((n_exp, LANES), lambda i: (0, 0))],
        out_specs=(tok, tok, tok, pl.BlockSpec((n_exp, LANES), lambda i: (0, 0))),
        scratch_shapes=[pltpu.VMEM((n_exp, LANES), F32)],
        compiler_params=_cparams(("arbitrary",), 32),
        name="router",
    )(h, wt, bias)


def _dest_kernel(e_ref, r_ref, st_ref, d_ref):
    n_exp = st_ref.shape[0]
    tm = e_ref.shape[1]
    eio = lax.broadcasted_iota(jnp.int32, (n_exp, tm), 0)
    start = st_ref[:, 0:1]
    rows = [jnp.sum(jnp.where(eio == e_ref[k:k + 1, :], start, 0.0), axis=0, keepdims=True)
            for k in range(TOP_K)]
    d_ref[...] = jnp.concatenate(rows, axis=0).astype(jnp.int32) + r_ref[...]


def _dest(top_e, rank, starts):
    n_tok = top_e.shape[1]
    n_exp = starts.shape[0]
    tm = 512
    st = jnp.broadcast_to(starts.astype(F32)[:, None], (n_exp, LANES))
    tok = pl.BlockSpec((TOP_K, tm), lambda i: (0, i))
    return pl.pallas_call(
        _dest_kernel,
        out_shape=jax.ShapeDtypeStruct((TOP_K, n_tok), jnp.int32),
        grid=(n_tok // tm,),
        in_specs=[tok, tok, pl.BlockSpec((n_exp, LANES), lambda i: (0, 0))],
        out_specs=tok,
        compiler_params=_cparams(("parallel",), 32),
        name="dest",
    )(top_e, rank, st)


def _pack_bf16_pairs(val):
    half = val.shape[1] // 2
    lo = pltpu.bitcast(val[:, :half].astype(BF16).astype(F32), U32)
    hi = pltpu.bitcast(val[:, half:].astype(BF16).astype(F32), U32)
    return (lo >> 16) | (hi & jnp.uint32(0xFFFF0000))


def _unpack_bf16_pairs(words):
    lo = pltpu.bitcast(words << 16, F32)
    hi = pltpu.bitcast(words & jnp.uint32(0xFFFF0000), F32)
    return jnp.concatenate([lo, hi], axis=1)


def _to_row_tiles(dst_ref, slot, val):
    rows = val.shape[0]
    words = _pack_bf16_pairs(val)
    for j in range(ROW_TILE):
        dst_ref[slot, pl.ds(j, rows, stride=ROW_TILE), :] = words[:, j * LANES:(j + 1) * LANES]


def _row_tile_words(src_ref, idx, rows):
    return jnp.concatenate([src_ref[(*idx, pl.ds(j, rows, stride=ROW_TILE), slice(None))]
                            for j in range(ROW_TILE)], axis=1)


def _row_tile(r):
    return pl.ds(pl.multiple_of(r * ROW_TILE, ROW_TILE), ROW_TILE)


def _dispatch_kernel(pstart_ref, cnt_ref, nblk_ref, dest_ref, h_ref, xs_ref, ht_ref, zero_ref, sem, zsem, tsem):
    tm = h_ref.shape[0]
    i = pl.program_id(0)
    n_steps = pl.num_programs(0)
    cur = i % 2
    _to_row_tiles(ht_ref, cur, h_ref[...])

    @pl.when(i == 0)
    def _():
        zero_ref[...] = jnp.zeros_like(zero_ref)
        _zero_tail_blocks(zero_ref, xs_ref, nblk_ref[0], tsem, False)

    def issue(t, carry):
        for k in range(TOP_K):
            pltpu.make_async_copy(ht_ref.at[cur, _row_tile(t)], xs_ref.at[_row_tile(dest_ref[t * TOP_K + k])],
                                  sem.at[cur]).start(priority=k % 2)
        return carry
    lax.fori_loop(0, tm, issue, 0, unroll=2)

    def drain(slot):
        for k in range(TOP_K):
            pltpu.make_async_copy(ht_ref.at[slot], xs_ref.at[pl.ds(0, tm * ROW_TILE)], sem.at[slot]).wait()

    @pl.when(i > 0)
    def _():
        drain(1 - cur)

    @pl.when(i == n_steps - 1)
    def _():
        drain(cur)

    n_exp = cnt_ref.shape[0]
    per_step = pl.cdiv(n_exp, n_steps)
    sizes = [1 << s for s in reversed(range(EXPERT_ROWS.bit_length()))]

    def pad_copies(step, j, wait):
        e = jnp.minimum(step * per_step + j, n_exp - 1)
        n_pad = jnp.where(step * per_step + j < n_exp, (EXPERT_ROWS - cnt_ref[e] % EXPERT_ROWS) % EXPERT_ROWS, 0)
        row = pstart_ref[e] + cnt_ref[e]
        for size in sizes:
            @pl.when(n_pad & size != 0)
            def _(row=row, size=size):
                c = pltpu.make_async_copy(
                    zero_ref.at[pl.ds(0, size * ROW_TILE)],
                    xs_ref.at[pl.ds(pl.multiple_of(row * ROW_TILE, ROW_TILE), size * ROW_TILE)], zsem)
                c.wait() if wait else c.start()
            row = row + (n_pad & size)

    def pad_start(j, carry):
        pad_copies(i, j, False)
        return carry

    def pad_wait_previous(j, carry):
        pad_copies(i - 1, j, True)
        return carry

    def pad_wait(j, carry):
        pad_copies(i, j, True)
        return carry
    lax.fori_loop(0, per_step, pad_start, 0)

    @pl.when(i > 0)
    def _():
        lax.fori_loop(0, per_step, pad_wait_previous, 0)

    @pl.when(i == n_steps - 1)
    def _():
        lax.fori_loop(0, per_step, pad_wait, 0)
        _zero_tail_blocks(zero_ref, xs_ref, nblk_ref[0], tsem, True)


def _zero_tail_blocks(zero_ref, out_ref, n_blk, sem, wait):
    trows = zero_ref.shape[0]

    def body(b, carry):
        c = pltpu.make_async_copy(zero_ref, out_ref.at[pl.ds(pl.multiple_of(b * trows, trows), trows)], sem)
        c.wait() if wait else c.start()
        return carry
    lax.fori_loop(n_blk, out_ref.shape[0] // trows, body, 0)


def _dispatch(h, dest, pad_start, counts, n_blk):
    n_tok, d = h.shape
    n_exp = counts.shape[0]
    assert d == 2 * ROW_TILE * LANES
    tm = MOE_TOKEN_TILE
    n_rows = (pl.cdiv(n_tok * TOP_K, EXPERT_ROWS) + n_exp) * EXPERT_ROWS
    grid_spec = pltpu.PrefetchScalarGridSpec(
        num_scalar_prefetch=3,
        grid=(n_tok // tm,),
        in_specs=[pl.BlockSpec((TOP_K * tm,), lambda i, *_: (i,), memory_space=pltpu.SMEM),
                  pl.BlockSpec((tm, d), lambda i, *_: (i, 0))],
        out_specs=pl.BlockSpec(memory_space=pl.ANY),
        scratch_shapes=[pltpu.VMEM((2, tm * ROW_TILE, LANES), U32),
                        pltpu.VMEM((EXPERT_ROWS * ROW_TILE, LANES), U32),
                        pltpu.SemaphoreType.DMA((2,)), pltpu.SemaphoreType.DMA(()), pltpu.SemaphoreType.DMA(())],
    )
    return pl.pallas_call(
        _dispatch_kernel,
        out_shape=jax.ShapeDtypeStruct((n_rows * ROW_TILE, LANES), U32),
        grid_spec=grid_spec,
        compiler_params=_cparams(("arbitrary",), 32),
        name="dispatch",
    )(pad_start, counts, n_blk, dest, h)


def _experts_kernel(bstart_ref, bend_ref, nblk_ref, xs_ref, wgu_hbm, wdn_hbm, ys_ref,
                    xbuf, ybuf, act_ref, wgu_f32, wdn_f32, wgu_bf, wdn_bf, zero_ref, xsem, ysem, wsem, zsem):
    e = pl.program_id(0)
    n_blk = nblk_ref[0]
    trows = xbuf.shape[1]
    rows = trows // ROW_TILE
    ff = wdn_bf.shape[0]
    b0 = bstart_ref[e]
    b1 = bend_ref[e]

    def block_rows(b):
        return pl.ds(pl.multiple_of(b * trows, trows), trows)

    def x_copy(b):
        slot = b % EXPERT_RING
        return pltpu.make_async_copy(xs_ref.at[block_rows(b)], xbuf.at[slot], xsem.at[slot])

    def y_copy(b):
        slot = b % EXPERT_OUT_RING
        return pltpu.make_async_copy(ybuf.at[slot], ys_ref.at[block_rows(b)], ysem.at[slot])

    @pl.when(e == 0)
    def _():
        for i in range(EXPERT_RING):
            @pl.when(i < n_blk)
            def _():
                x_copy(i).start(priority=1)
        zero_ref[...] = jnp.zeros_like(zero_ref)
        _zero_tail_blocks(zero_ref, ys_ref, n_blk, zsem, False)

    n_exp = pl.num_programs(0)
    wslot = e % EXPERT_WEIGHT_RING

    def w_copies(x, slot):
        return (pltpu.make_async_copy(wgu_hbm.at[x], wgu_f32.at[slot], wsem.at[0, slot]),
                pltpu.make_async_copy(wdn_hbm.at[x], wdn_f32.at[slot], wsem.at[1, slot]))

    @pl.when(e == 0)
    def _():
        for i in range(EXPERT_WEIGHT_RING):
            @pl.when(i < n_exp)
            def _():
                for c in w_copies(i, i):
                    c.start()

    for c in w_copies(e, wslot):
        c.wait()
    wgu_bf[...] = wgu_f32[wslot].astype(BF16)
    wdn_bf[...] = wdn_f32[wslot].astype(BF16)

    @pl.when(e + EXPERT_WEIGHT_RING < n_exp)
    def _():
        for c in w_copies(e + EXPERT_WEIGHT_RING, wslot):
            c.start()

    def up_wait(blocks):
        for b in blocks:
            x_copy(b).wait()

    def up_compute(blocks):
        for b in blocks:
            xb = _unpack_bf16_pairs(_row_tile_words(xbuf, (b % EXPERT_RING,), rows)).astype(BF16)
            gu = jnp.dot(xb, wgu_bf[...], preferred_element_type=F32)
            act_ref[b - b0] = (jax.nn.silu(gu[:, :ff]) * gu[:, ff:]).astype(BF16)

    def up_start(blocks):
        for b in blocks:
            @pl.when(b + EXPERT_RING < n_blk)
            def _():
                x_copy(b + EXPERT_RING).start(priority=1)

    def down_wait(blocks):
        for b in blocks:
            @pl.when(b >= EXPERT_OUT_RING)
            def _():
                y_copy(b - EXPERT_OUT_RING).wait()

    def down_compute(blocks):
        for b in blocks:
            _to_row_tiles(ybuf, b % EXPERT_OUT_RING,
                          jnp.dot(act_ref[b - b0], wdn_bf[...], preferred_element_type=F32))

    def down_start(blocks):
        for b in blocks:
            y_copy(b).start(priority=1)
        last = blocks[-1]

        @pl.when(last == n_blk - 1)
        def _():
            for i in range(EXPERT_OUT_RING):
                @pl.when(last >= i)
                def _():
                    y_copy(last - i).wait()

    def up(blocks):
        up_wait(blocks)
        up_compute(blocks)
        up_start(blocks)

    def down(blocks):
        down_wait(blocks)
        down_compute(blocks)
        down_start(blocks)

    n_mine = b1 - b0

    def run_groups(fn):
        def body(p, carry):
            fn(tuple(b0 + EXPERT_GROUP * p + j for j in range(EXPERT_GROUP)))
            return carry
        lax.fori_loop(0, n_mine // EXPERT_GROUP, body, 0)
        size = EXPERT_GROUP // 2
        while size >= 1:
            @pl.when(n_mine & size != 0)
            def _(size=size):
                start = b0 + (n_mine // (2 * size)) * (2 * size)
                fn(tuple(start + j for j in range(size)))
            size //= 2

    @pl.when(n_mine == EXPERT_GROUP)
    def _():
        blocks = tuple(b0 + j for j in range(EXPERT_GROUP))
        up_wait(blocks)
        down_wait(blocks)
        up_compute(blocks)
        down_compute(blocks)
        up_start(blocks)
        down_start(blocks)

    @pl.when(n_mine != EXPERT_GROUP)
    def _():
        run_groups(up)
        run_groups(down)

    @pl.when(e == n_exp - 1)
    def _():
        _zero_tail_blocks(zero_ref, ys_ref, n_blk, zsem, True)


def _expert_blocks(counts):
    blocks = (counts + EXPERT_ROWS - 1) // EXPERT_ROWS
    bend = jnp.cumsum(blocks)
    bstart = bend - blocks
    i32 = lambda a: a.astype(jnp.int32)
    return i32(bstart), i32(bend), i32(bend[-1]).reshape(1), i32(bstart * EXPERT_ROWS)


def _experts(xs, bstart, bend, n_blk, w_gu, w_down, n_tok):
    n_exp, d, ff2 = w_gu.shape
    ff = w_down.shape[1]
    n_rows = xs.shape[0] // ROW_TILE
    assert n_rows % EXPERT_ROWS == 0 and d == 2 * ROW_TILE * LANES
    max_blocks = pl.cdiv(n_tok, EXPERT_ROWS)
    grid_spec = pltpu.PrefetchScalarGridSpec(
        num_scalar_prefetch=3,
        grid=(n_exp,),
        in_specs=[pl.BlockSpec(memory_space=pl.ANY), pl.BlockSpec(memory_space=pl.ANY),
                  pl.BlockSpec(memory_space=pl.ANY)],
        out_specs=pl.BlockSpec(memory_space=pl.ANY),
        scratch_shapes=[pltpu.VMEM((EXPERT_RING, EXPERT_ROWS * ROW_TILE, LANES), U32),
                        pltpu.VMEM((EXPERT_OUT_RING, EXPERT_ROWS * ROW_TILE, LANES), U32),
                        pltpu.VMEM((max_blocks, EXPERT_ROWS, ff), BF16),
                        pltpu.VMEM((EXPERT_WEIGHT_RING, d, ff2), F32),
                        pltpu.VMEM((EXPERT_WEIGHT_RING, ff, d), F32),
                        pltpu.VMEM((d, ff2), BF16), pltpu.VMEM((ff, d), BF16),
                        pltpu.VMEM((EXPERT_ROWS * ROW_TILE, LANES), U32),
                        pltpu.SemaphoreType.DMA((EXPERT_RING,)),
                        pltpu.SemaphoreType.DMA((EXPERT_OUT_RING,)),
                        pltpu.SemaphoreType.DMA((2, EXPERT_WEIGHT_RING)),
                        pltpu.SemaphoreType.DMA(())],
    )
    return pl.pallas_call(
        _experts_kernel,
        out_shape=jax.ShapeDtypeStruct(xs.shape, U32),
        grid_spec=grid_spec,
        compiler_params=_cparams(("arbitrary",), 48),
        name="experts",
    )(bstart, bend, n_blk, xs, w_gu, w_down)


def _combine_kernel(dest_ref, dnext_ref, gate_ref, h_ref, ys_ref, wgu_ref, wdn_ref, g_ref, b_ref, o_ref,
                    buf_ref, routed_ref, sem, *, alpha):
    tm = h_ref.shape[0]
    i = pl.program_id(0)
    cur = i % 2
    chunk = SUBLANES

    def issue(d_ref, slot, t):
        for k in range(TOP_K):
            pltpu.make_async_copy(ys_ref.at[_row_tile(d_ref[t * TOP_K + k])], buf_ref.at[slot, k, _row_tile(t)],
                                  sem.at[slot]).start(priority=k % 2)

    @pl.when(i == 0)
    def _():
        def first(t, carry):
            issue(dest_ref, 0, t)
            return carry
        lax.fori_loop(0, tm, first, 0, unroll=2)

    for k in range(TOP_K):
        pltpu.make_async_copy(ys_ref.at[pl.ds(0, tm * ROW_TILE)], buf_ref.at[cur, k], sem.at[cur]).wait()

    def weighted_sum(c):
        tok = pl.ds(pl.multiple_of(c * chunk, chunk), chunk)
        gate = gate_ref[tok, :]
        total = None
        for k in range(TOP_K):
            words = jnp.concatenate(
                [buf_ref[cur, k, pl.ds(pl.multiple_of(c * (chunk * ROW_TILE), chunk * ROW_TILE) + j, chunk,
                                       stride=ROW_TILE), :] for j in range(ROW_TILE)], axis=1)
            term = gate[:, k:k + 1] * _unpack_bf16_pairs(words)
            total = term if total is None else total + term
        routed_ref[tok, :] = total

    @pl.when(i + 1 < pl.num_programs(0))
    def _():
        def body(c, carry):
            for t in range(chunk):
                issue(dnext_ref, 1 - cur, c * chunk + t)
            weighted_sum(c)
            return carry
        lax.fori_loop(0, tm // chunk, body, 0)

    @pl.when(i + 1 == pl.num_programs(0))
    def _():
        def body(c, carry):
            weighted_sum(c)
            return carry
        lax.fori_loop(0, tm // chunk, body, 0)

    h = h_ref[...]
    ff = wdn_ref.shape[0]
    gu = jnp.dot(h.astype(BF16), wgu_ref[...], preferred_element_type=F32)
    act = (jax.nn.silu(gu[:, :ff]) * gu[:, ff:]).astype(BF16)
    acc = alpha * h + jnp.dot(act, wdn_ref[...], preferred_element_type=F32) + routed_ref[...]
    o_ref[...] = _layer_norm(acc, g_ref[...], b_ref[...])


def _combine(h, ys, dest, gate_t, shared_w_gu, shared_w_down, ln_g, ln_b, alpha):
    n_tok, d = h.shape
    tm = MOE_TOKEN_TILE
    n_tiles = n_tok // tm
    full = lambda a: pl.BlockSpec(a.shape, lambda i: (0,) * a.ndim)
    wgu = shared_w_gu.astype(BF16)
    wdn = shared_w_down.astype(BF16)
    g2 = ln_g.reshape(1, d)
    b2 = ln_b.reshape(1, d)
    return pl.pallas_call(
        functools.partial(_combine_kernel, alpha=alpha),
        out_shape=jax.ShapeDtypeStruct((n_tok, d), F32),
        grid=(n_tiles,),
        in_specs=[pl.BlockSpec((TOP_K * tm,), lambda i: (i,), memory_space=pltpu.SMEM),
                  pl.BlockSpec((TOP_K * tm,), lambda i: (jnp.minimum(i + 1, n_tiles - 1),),
                               memory_space=pltpu.SMEM),
                  pl.BlockSpec((tm, TOP_K), lambda i: (i, 0)),
                  pl.BlockSpec((tm, d), lambda i: (i, 0)),
                  pl.BlockSpec(memory_space=pl.ANY),
                  full(wgu), full(wdn), full(g2), full(b2)],
        out_specs=pl.BlockSpec((tm, d), lambda i: (i, 0)),
        scratch_shapes=[pltpu.VMEM((2, TOP_K, tm * ROW_TILE, LANES), U32), pltpu.VMEM((tm, d), F32),
                        pltpu.SemaphoreType.DMA((2,))],
        compiler_params=_cparams(("arbitrary",), 48),
        name="combine",
    )(dest, dest, gate_t, h, ys, wgu, wdn, g2, b2)


def _moe(h, router_w, router_bias, w_gu, w_down, shared_w_gu, shared_w_down, ln_g, ln_b, alpha):
    top_e, gate, rank, cnt = _router(h, router_w, router_bias)
    counts = cnt[:, 0].astype(jnp.int32)
    bstart, bend, n_blk, pad_start = _expert_blocks(counts)
    dest = _dest(top_e, rank, pad_start)
    dest_tiles = dest.T.reshape(-1)
    xs = _dispatch(h, dest_tiles, pad_start, counts, n_blk)
    ys = _experts(xs, bstart, bend, n_blk, w_gu, w_down, h.shape[0])
    return _combine(h, ys, dest_tiles, gate.T, shared_w_gu, shared_w_down, ln_g, ln_b, alpha)


def kernel(x, w_in, att_norm_g, lam_re, lam_im, log_step, b_re, b_im, c_re, c_im, d_skip, w_glu, b_glu,
           ssm_norm_g, w_out, ln1_g, ln1_b, router_w, router_bias, w_gu, w_down, shared_w_gu,
           shared_w_down, ln2_g, ln2_b):
    bsz, seq, d = x.shape
    depth = w_in.shape[0]
    alpha = (2 * depth) ** 0.25
    h = x.reshape(bsz * seq, d)
    for i in range(depth):
        proj = _inproj(h, w_in[i].astype(BF16), seq)
        o_att = _attention(proj, bsz, seq)
        y_ssm = _s5(proj.reshape(bsz, seq, -1), 3 * ATT_WIDTH, lam_re[i], lam_im[i], log_step[i],
                    b_re[i], b_im[i], c_re[i], c_im[i], d_skip[i])
        h = _mixout(o_att, y_ssm.reshape(bsz * seq, -1), h, w_glu[i], b_glu[i], att_norm_g[i],
                    ssm_norm_g[i], w_out[i], ln1_g[i], ln1_b[i], alpha)
        h = _moe(h, router_w[i], router_bias[i], w_gu[i], w_down[i], shared_w_gu[i], shared_w_down[i],
                 ln2_g[i], ln2_b[i], alpha)
    return h.reshape(bsz, seq, d)
```

```python
import functools

import jax
import jax.numpy as jnp
from jax import lax
from jax.experimental import pallas as pl
from jax.experimental.pallas import tpu as pltpu

F32 = jnp.float32
BF16 = jnp.bfloat16
U32 = jnp.uint32

ATT_HEADS = 8
HEAD_DIM = 64
ATT_WIDTH = ATT_HEADS * HEAD_DIM
SSM_CH = 16
SSM_STATE = 64
S5_SLAB = 64
ROPE_THETA = 500000.0
ROT_DIM = HEAD_DIM // 4
DILATIONS = (1, 4, 16)
ATT_BLOCK = 128
ATT_GROUP = 16
TOP_K = 8
N_EXPERT_GROUPS = 8
TOPK_GROUPS = 4
ROUTED_SCALE = 2.5
LN_EPS = 1e-5
RMS_EPS = 1e-6

LANES = 128
SUBLANES = 8
EXPERT_ROWS = 144
MOE_TOKEN_TILE = 512
EXPERT_RING = 16
EXPERT_OUT_RING = 16
EXPERT_GROUP = 4
EXPERT_WEIGHT_RING = 4
ROW_TILE = 4
NEG_INF = float("-inf")


def _cparams(sem, vmem_mb):
    return pltpu.CompilerParams(dimension_semantics=sem, vmem_limit_bytes=vmem_mb * 1024 * 1024)


def _inproj_kernel(x_ref, w_ref, cos_ref, sa_ref, sb_ref, o_ref, *, n_rot_cols):
    xb = x_ref[...].astype(BF16)
    cosf = cos_ref[...]
    sa = sa_ref[...]
    sb = sb_ref[...]
    width = o_ref.shape[1]
    chunk = 512
    for c in range(width // chunk):
        r = jnp.dot(xb, w_ref[:, c * chunk:(c + 1) * chunk], preferred_element_type=F32)
        if c * chunk < n_rot_cols:
            parts = []
            for s in range(chunk // LANES):
                t = r[:, s * LANES:(s + 1) * LANES]
                parts.append(t * cosf + pltpu.roll(t, LANES - ROT_DIM // 2, 1) * sa
                             + pltpu.roll(t, ROT_DIM // 2, 1) * sb)
            r = jnp.concatenate(parts, axis=1)
        o_ref[:, c * chunk:(c + 1) * chunk] = r


def _rope_lane_tables(seq):
    half = ROT_DIM // 2
    inv_freq = jnp.power(jnp.float32(ROPE_THETA), -jnp.arange(half, dtype=F32) / half)
    ang = jnp.arange(seq, dtype=F32)[:, None] * inv_freq[None, :]
    cos, sin = jnp.cos(ang), jnp.sin(ang)
    rest = HEAD_DIM - ROT_DIM
    cos_h = jnp.concatenate([cos, cos, jnp.ones((seq, rest), F32)], axis=1)
    sa_h = jnp.concatenate([-sin, jnp.zeros((seq, half + rest), F32)], axis=1)
    sb_h = jnp.concatenate([jnp.zeros((seq, half), F32), sin, jnp.zeros((seq, rest), F32)], axis=1)
    rep = LANES // HEAD_DIM
    return tuple(jnp.tile(t, (1, rep)) for t in (cos_h, sa_h, sb_h))


def _inproj(x2d, w_in_bf, seq):
    n_tok, d = x2d.shape
    width = w_in_bf.shape[1]
    tm = 1024
    cosf, sa, sb = _rope_lane_tables(seq)
    tab_spec = pl.BlockSpec((tm, LANES), lambda i: (i % (seq // tm), 0))
    return pl.pallas_call(
        functools.partial(_inproj_kernel, n_rot_cols=2 * ATT_WIDTH),
        out_shape=jax.ShapeDtypeStruct((n_tok, width), F32),
        grid=(n_tok // tm,),
        in_specs=[pl.BlockSpec((tm, d), lambda i: (i, 0)),
                  pl.BlockSpec((d, width), lambda i: (0, 0)),
                  tab_spec, tab_spec, tab_spec],
        out_specs=pl.BlockSpec((tm, width), lambda i: (i, 0)),
        compiler_params=_cparams(("parallel",), 48),
        name="inproj",
    )(x2d, w_in_bf, cosf, sa, sb)


def _attn_kernel(q_ref, k_ref, v_ref, o_ref, qs_ref, ks_ref, vs_ref, tmp_ref, ob_ref, lb_ref, band_ref,
                 first_ref, *, seq):
    blk = ATT_BLOCK
    lane = lax.broadcasted_iota(jnp.int32, (1, LANES), 1)
    head0 = lane < HEAD_DIM
    scale = HEAD_DIM ** -0.5
    d1, d2 = DILATIONS[1], DILATIONS[2]
    assert DILATIONS[0] == 1 and d2 == d1 * d1
    seg = seq // d1
    sub = seg // d1

    qi = lax.broadcasted_iota(jnp.int32, (blk, 2 * blk), 0)
    kj = lax.broadcasted_iota(jnp.int32, (blk, 2 * blk), 1)
    dist = qi + blk - kj
    band_ref[...] = jnp.where((dist >= 0) & (dist <= blk), 0.0, NEG_INF)
    first_ref[...] = jnp.where((dist >= 0) & (kj >= blk), 0.0, NEG_INF)

    n_class = (1, d1, d2)
    class_len = (seq, seg, sub)
    base = [0]
    for c in range(len(DILATIONS)):
        base.append(base[c] + n_class[c] * (class_len[c] + blk))

    def kv_row0(c, g):
        return base[c] + g * (class_len[c] + blk)

    qs_ref[0] = (q_ref[...] * scale).astype(BF16)
    for a in range(d1):
        x = q_ref[pl.ds(a, seg, stride=d1), :] * scale
        tmp_ref[a * seg:(a + 1) * seg, :] = x
        qs_ref[1, a * seg:(a + 1) * seg, :] = x.astype(BF16)
    for g in range(d2):
        qs_ref[2, g * sub:(g + 1) * sub, :] = tmp_ref[pl.ds((g // d1) * seg + g % d1, sub, stride=d1),
                                                      :].astype(BF16)
    for src_ref, dst_ref in ((k_ref, ks_ref), (v_ref, vs_ref)):
        for c in range(len(DILATIONS)):
            for g in range(n_class[c]):
                dst_ref[kv_row0(c, g):kv_row0(c, g) + blk, :] = jnp.zeros((blk, LANES), BF16)
        dst_ref[kv_row0(0, 0) + blk:kv_row0(0, 0) + blk + seq, :] = src_ref[...].astype(BF16)
        for a in range(d1):
            x = src_ref[pl.ds(a, seg, stride=d1), :]
            tmp_ref[a * seg:(a + 1) * seg, :] = x
            dst_ref[kv_row0(1, a) + blk:kv_row0(1, a) + blk + seg, :] = x.astype(BF16)
        for g in range(d2):
            dst_ref[kv_row0(2, g) + blk:kv_row0(2, g) + blk + sub, :] = tmp_ref[
                pl.ds((g // d1) * seg + g % d1, sub, stride=d1), :].astype(BF16)

    def one_block(c, g, n, out_rows, bias_ref):
        q = qs_ref[c, pl.ds(aligned(g * class_len[c] + n * blk), blk), :]
        kv_rows = pl.ds(aligned(kv_row0(c, g) + n * blk), 2 * blk)
        kk = ks_ref[kv_rows, :]
        vv = vs_ref[kv_rows, :]
        outs = []
        lses = []
        for h in range(LANES // HEAD_DIM):
            hm = head0 if h == 0 else jnp.logical_not(head0)
            qh = jnp.where(hm, q, jnp.zeros_like(q))
            s = lax.dot_general(qh, kk, (((1,), (1,)), ((), ())), preferred_element_type=F32)
            s = s + bias_ref[...]
            m = jnp.max(s, axis=-1, keepdims=True)
            p = jnp.exp(s - m)
            den = jnp.sum(p, axis=-1, keepdims=True)
            outs.append(jnp.dot(p.astype(BF16), vv, preferred_element_type=F32) / den)
            lses.append(m + jnp.log(den))
        ob_ref[c, out_rows, :] = jnp.where(head0, outs[0], outs[1])
        lb_ref[c, out_rows, :] = jnp.where(head0, lses[0], lses[1])

    def run_blocks(n_blocks, fn):
        group = max(g for g in range(1, ATT_GROUP + 1) if n_blocks % g == 0)
        if n_blocks == group:
            for g in range(group):
                fn(g)
            return
        def body(it, carry):
            for g in range(group):
                fn(it * group + g)
            return carry
        lax.fori_loop(0, n_blocks // group, body, 0)

    def aligned(x):
        return x if isinstance(x, int) else pl.multiple_of(x, blk)

    one_block(0, 0, 0, pl.ds(0, blk), first_ref)
    run_blocks(seq // blk - 1,
               lambda i: one_block(0, 0, i + 1, pl.ds(aligned((i + 1) * blk), blk), band_ref))

    nb1 = seg // blk
    run_blocks(d1, lambda a: one_block(1, a, 0, pl.ds(a, blk, stride=d1), first_ref))
    def later1(i):
        a = i // (nb1 - 1)
        n = i - a * (nb1 - 1) + 1
        one_block(1, a, n, pl.ds(a + n * (d1 * blk), blk, stride=d1), band_ref)
    run_blocks(d1 * (nb1 - 1), later1)

    assert sub == blk
    def only2(g):
        a = g // d1
        one_block(2, g, 0, pl.ds(a + d1 * (g - a * d1), blk, stride=d2), first_ref)
    run_blocks(d2, only2)

    rc = 256
    def merge(i, carry):
        sl = pl.ds(pl.multiple_of(i * rc, rc), rc)
        l0 = lb_ref[0, sl, :]
        l1 = lb_ref[1, sl, :]
        l2 = lb_ref[2, sl, :]
        mx = jnp.maximum(jnp.maximum(l0, l1), l2)
        e0 = jnp.exp(l0 - mx)
        e1 = jnp.exp(l1 - mx)
        e2 = jnp.exp(l2 - mx)
        o_ref[sl, :] = ((e0 * ob_ref[0, sl, :] + e1 * ob_ref[1, sl, :] + e2 * ob_ref[2, sl, :])
                        / (e0 + e1 + e2))
        return carry
    lax.fori_loop(0, seq // rc, merge, 0)


def _attention(proj, bsz, seq):
    n_tok = proj.shape[0]
    pairs = ATT_WIDTH // LANES
    assert seq % (ATT_BLOCK * max(DILATIONS)) == 0
    kv_rows = sum(seq + d * ATT_BLOCK for d in DILATIONS)
    blk = (seq, LANES)
    return pl.pallas_call(
        functools.partial(_attn_kernel, seq=seq),
        out_shape=jax.ShapeDtypeStruct((n_tok, ATT_WIDTH), F32),
        grid=(bsz, pairs),
        in_specs=[pl.BlockSpec(blk, lambda b, h: (b, h)),
                  pl.BlockSpec(blk, lambda b, h: (b, pairs + h)),
                  pl.BlockSpec(blk, lambda b, h: (b, 2 * pairs + h))],
        out_specs=pl.BlockSpec(blk, lambda b, h: (b, h)),
        scratch_shapes=[pltpu.VMEM((len(DILATIONS), seq, LANES), BF16),
                        pltpu.VMEM((kv_rows, LANES), BF16),
                        pltpu.VMEM((kv_rows, LANES), BF16),
                        pltpu.VMEM((seq, LANES), F32),
                        pltpu.VMEM((len(DILATIONS), seq, LANES), F32),
                        pltpu.VMEM((len(DILATIONS), seq, LANES), F32),
                        pltpu.VMEM((ATT_BLOCK, 2 * ATT_BLOCK), F32),
                        pltpu.VMEM((ATT_BLOCK, 2 * ATT_BLOCK), F32)],
        compiler_params=_cparams(("parallel", "parallel"), 40),
        name="attn",
    )(proj, proj, proj)


def _s5_kernel(u_ref, bm_ref, lam_ref, cm_ref, dk_ref, o_ref, us_ref, st_ref, ys_ref, carry_ref, *, tc):
    bsz = u_ref.shape[0]
    half = st_ref.shape[1] // 2
    slab_rows = S5_SLAB * bsz
    n_slabs = tc // S5_SLAB

    @pl.when(pl.program_id(1) == 0)
    def _():
        carry_ref[...] = jnp.zeros_like(carry_ref)

    for b in range(bsz):
        us_ref[pl.ds(b, tc, stride=bsz), :] = u_ref[b]

    bm = bm_ref[0]
    cm = cm_ref[0]
    lam = lam_ref[0]
    lam_re = lam[:, :half]
    lam_im = lam[:, half:]

    def slab(s):
        return slice(s * slab_rows, (s + 1) * slab_rows)

    def project_in(s):
        st_ref[slab(s), :] = jnp.dot(us_ref[slab(s), :].astype(BF16), bm, preferred_element_type=F32)

    def project_out(s):
        ys_ref[slab(s), :] = jnp.dot(st_ref[slab(s), :].astype(BF16), cm, preferred_element_type=F32)

    def scan(s, xr, xi):
        for t in range(S5_SLAB):
            sl = slice(s * slab_rows + t * bsz, s * slab_rows + (t + 1) * bsz)
            xr, xi = (lam_re * xr - lam_im * xi + st_ref[sl, :half],
                      lam_re * xi + lam_im * xr + st_ref[sl, half:])
            st_ref[sl, :half] = xr
            st_ref[sl, half:] = xi
        return xr, xi

    xr, xi = carry_ref[:, :half], carry_ref[:, half:]
    project_in(0)
    for s in range(n_slabs):
        if s + 1 < n_slabs:
            project_in(s + 1)
        xr, xi = scan(s, xr, xi)
        if s >= 1:
            project_out(s - 1)
    project_out(n_slabs - 1)
    carry_ref[:, :half] = xr
    carry_ref[:, half:] = xi

    dk = dk_ref[...]
    for b in range(bsz):
        o_ref[b] = ys_ref[pl.ds(b, tc, stride=bsz), :] + dk * u_ref[b]


def _s5_params(lam_re, lam_im, log_step, b_re, b_im, c_re, c_im, bsz):
    groups = lam_re.shape[0]
    gpc = LANES // SSM_CH
    n_chunks = groups // gpc
    lam = lax.complex(lam_re.astype(F32), lam_im.astype(F32))
    step = jnp.exp(log_step.astype(F32))[:, None]
    lam_bar = jnp.exp(lam * step)
    bmat = lax.complex(b_re.astype(F32), b_im.astype(F32))
    b_bar = ((lam_bar - 1.0) / lam)[..., None] * bmat
    eye = jnp.eye(gpc, dtype=F32)

    def block_diag_in(t):
        t = t.reshape(n_chunks, gpc, SSM_STATE, SSM_CH)
        return jnp.einsum('ngpc,gh->ngchp', t, eye).reshape(n_chunks, gpc * SSM_CH, gpc * SSM_STATE)

    def block_diag_out(t):
        t = t.reshape(n_chunks, gpc, SSM_CH, SSM_STATE)
        return jnp.einsum('ngcp,gh->ngphc', t, eye).reshape(n_chunks, gpc * SSM_STATE, gpc * SSM_CH)

    bm = jnp.concatenate([block_diag_in(b_bar.real), block_diag_in(b_bar.imag)], axis=2).astype(BF16)
    cm = jnp.concatenate([block_diag_out(c_re.astype(F32)), block_diag_out(-c_im.astype(F32))],
                         axis=1).astype(BF16)
    lam_row = jnp.concatenate([lam_bar.real.reshape(n_chunks, gpc * SSM_STATE),
                               lam_bar.imag.reshape(n_chunks, gpc * SSM_STATE)], axis=1)
    lam_t = jnp.broadcast_to(lam_row[:, None, :], (n_chunks, bsz, 2 * gpc * SSM_STATE))
    return bm, lam_t, cm, n_chunks


def _s5(proj3, u_col0, lam_re, lam_im, log_step, b_re, b_im, c_re, c_im, d_skip):
    bsz, seq, _ = proj3.shape
    assert bsz == SUBLANES
    bm, lam_t, cm, n_chunks = _s5_params(lam_re, lam_im, log_step, b_re, b_im, c_re, c_im, bsz)
    width = n_chunks * LANES
    tc = 512
    st_cols = bm.shape[2]
    ublk0 = u_col0 // LANES
    return pl.pallas_call(
        functools.partial(_s5_kernel, tc=tc),
        out_shape=jax.ShapeDtypeStruct((bsz, seq, width), F32),
        grid=(n_chunks, seq // tc),
        in_specs=[pl.BlockSpec((bsz, tc, LANES), lambda c, t: (0, t, ublk0 + c)),
                  pl.BlockSpec((1, LANES, st_cols), lambda c, t: (c, 0, 0)),
                  pl.BlockSpec((1, bsz, st_cols), lambda c, t: (c, 0, 0)),
                  pl.BlockSpec((1, st_cols, LANES), lambda c, t: (c, 0, 0)),
                  pl.BlockSpec((1, LANES), lambda c, t: (0, c))],
        out_specs=pl.BlockSpec((bsz, tc, LANES), lambda c, t: (0, t, c)),
        scratch_shapes=[pltpu.VMEM((tc * bsz, LANES), F32),
                        pltpu.VMEM((tc * bsz, st_cols), F32),
                        pltpu.VMEM((tc * bsz, LANES), F32),
                        pltpu.VMEM((bsz, st_cols), F32)],
        compiler_params=_cparams(("arbitrary", "arbitrary"), 48),
        name="s5",
    )(proj3, bm, lam_t, cm, d_skip.reshape(1, width).astype(F32))


def _layer_norm(v, g, b):
    mu = jnp.mean(v, axis=-1, keepdims=True)
    var = jnp.mean(jnp.square(v - mu), axis=-1, keepdims=True)
    return (v - mu) * lax.rsqrt(var + LN_EPS) * g + b


def _rms_norm(v, g):
    return v * lax.rsqrt(jnp.mean(jnp.square(v), axis=-1, keepdims=True) + RMS_EPS) * g


def _mixout_kernel(att_ref, ssm_ref, x_ref, wglu_ref, bglu_ref, ag_ref, sg_ref, wout_ref, g_ref, b_ref,
                   o_ref, *, alpha):
    y = jax.nn.gelu(ssm_ref[...])
    z = jnp.dot(y.astype(BF16), wglu_ref[...], preferred_element_type=F32) + bglu_ref[...]
    o_ssm = y * jax.nn.sigmoid(z)
    a = _rms_norm(att_ref[...], ag_ref[...]).astype(BF16)
    s = _rms_norm(o_ssm, sg_ref[...]).astype(BF16)
    wa = att_ref.shape[1]
    mix = (jnp.dot(a, wout_ref[:wa, :], preferred_element_type=F32)
           + jnp.dot(s, wout_ref[wa:, :], preferred_element_type=F32))
    o_ref[...] = _layer_norm(alpha * x_ref[...] + mix, g_ref[...], b_ref[...])


def _mixout(o_att, y_ssm, x2d, w_glu, b_glu, att_g, ssm_g, w_out, ln_g, ln_b, alpha):
    n_tok, d = x2d.shape
    wa = o_att.shape[1]
    ws = y_ssm.shape[1]
    tm = 1024
    row = lambda w: pl.BlockSpec((tm, w), lambda i: (i, 0))
    full = lambda a: pl.BlockSpec(a.shape, lambda i: (0,) * a.ndim)
    args = (o_att, y_ssm, x2d, w_glu.astype(BF16), b_glu.reshape(1, ws), att_g.reshape(1, wa),
            ssm_g.reshape(1, ws), w_out.astype(BF16), ln_g.reshape(1, d), ln_b.reshape(1, d))
    return pl.pallas_call(
        functools.partial(_mixout_kernel, alpha=alpha),
        out_shape=jax.ShapeDtypeStruct((n_tok, d), F32),
        grid=(n_tok // tm,),
        in_specs=[row(wa), row(ws), row(d)] + [full(a) for a in args[3:]],
        out_specs=row(d),
        compiler_params=_cparams(("parallel",), 48),
        name="mixout",
    )(*args)


def _split_bf16(v):
    hi = v.astype(BF16)
    lo = (v - hi.astype(F32)).astype(BF16)
    return hi, lo


def _router_kernel(h_ref, wt_ref, bias_ref, e_ref, g_ref, r_ref, cnt_ref, run_ref):
    tm = h_ref.shape[0]
    n_exp = wt_ref.shape[0]
    gsz = n_exp // N_EXPERT_GROUPS

    @pl.when(pl.program_id(0) == 0)
    def _():
        run_ref[...] = jnp.zeros_like(run_ref)

    w_hi, w_lo = _split_bf16(wt_ref[...])
    h_hi, h_lo = _split_bf16(h_ref[...])
    nt = (((1,), (1,)), ((), ()))
    logits = (lax.dot_general(w_hi, h_hi, nt, preferred_element_type=F32)
              + lax.dot_general(w_hi, h_lo, nt, preferred_element_type=F32)
              + lax.dot_general(w_lo, h_hi, nt, preferred_element_type=F32))
    scores = jax.nn.sigmoid(logits)
    choice = scores + bias_ref[:, 0:1]

    gio = lax.broadcasted_iota(jnp.int32, (gsz, tm), 0).astype(F32)
    gscore = []
    for g in range(N_EXPERT_GROUPS):
        cg = choice[g * gsz:(g + 1) * gsz, :]
        m1 = jnp.max(cg, axis=0, keepdims=True)
        i1 = jnp.min(jnp.where(cg == m1, gio, float(gsz)), axis=0, keepdims=True)
        m2 = jnp.max(jnp.where(gio == i1, NEG_INF, cg), axis=0, keepdims=True)
        gscore.append(m1 + m2)
    masked = []
    for g in range(N_EXPERT_GROUPS):
        beat = jnp.zeros((1, tm), F32)
        for o in range(N_EXPERT_GROUPS):
            if o == g:
                continue
            wins = (gscore[o] >= gscore[g]) if o < g else (gscore[o] > gscore[g])
            beat = beat + jnp.where(wins, 1.0, 0.0)
        keep = beat < float(TOPK_GROUPS)
        masked.append(jnp.where(keep, choice[g * gsz:(g + 1) * gsz, :], NEG_INF))
    cur = jnp.concatenate(masked, axis=0)

    eio = lax.broadcasted_iota(jnp.int32, (n_exp, tm), 0).astype(F32)
    idxs = []
    gates = []
    candidates = cur
    for _ in range(TOP_K):
        m = jnp.max(cur, axis=0, keepdims=True)
        idx = jnp.min(jnp.where(cur == m, eio, float(n_exp)), axis=0, keepdims=True)
        hit = eio == idx
        idxs.append(idx)
        gates.append(jnp.sum(jnp.where(hit, scores, 0.0), axis=0, keepdims=True))
        cur = jnp.where(hit, NEG_INF, cur)
    onehot = jnp.where(cur != candidates, 1.0, 0.0)
    gate = jnp.concatenate(gates, axis=0)
    gate = ROUTED_SCALE * gate / (jnp.sum(gate, axis=0, keepdims=True) + 1e-20)

    si = lax.broadcasted_iota(jnp.int32, (tm, tm), 0)
    ti = lax.broadcasted_iota(jnp.int32, (tm, tm), 1)
    upper = jnp.where(si < ti, 1.0, 0.0).astype(BF16)
    before = jnp.dot(onehot.astype(BF16), upper, preferred_element_type=F32) + run_ref[:, 0:1]
    ranks = [jnp.sum(jnp.where(eio == idx, before, 0.0), axis=0, keepdims=True) for idx in idxs]

    e_ref[...] = jnp.concatenate(idxs, axis=0).astype(jnp.int32)
    g_ref[...] = gate
    r_ref[...] = jnp.concatenate(ranks, axis=0).astype(jnp.int32)
    run_ref[...] = run_ref[...] + jnp.sum(onehot, axis=1, keepdims=True)
    cnt_ref[...] = run_ref[...]


def _router(h, router_w, router_bias):
    n_tok, d = h.shape
    n_exp = router_w.shape[1]
    tm = 512
    wt = router_w.astype(F32).T
    bias = jnp.broadcast_to(router_bias.astype(F32)[:, None], (n_exp, LANES))
    tok = pl.BlockSpec((TOP_K, tm), lambda i: (0, i))
    return pl.pallas_call(
        _router_kernel,
        out_shape=(jax.ShapeDtypeStruct((TOP_K, n_tok), jnp.int32),
                   jax.ShapeDtypeStruct((TOP_K, n_tok), F32),
                   jax.ShapeDtypeStruct((TOP_K, n_tok), jnp.int32),
                   jax.ShapeDtypeStruct((n_exp, LANES), F32)),
        grid=(n_tok // tm,),
        in_specs=[pl.BlockSpec((tm, d), lambda i: (i, 0)),
                  pl.BlockSpec((n_exp, d), lambda i: (0, 0)),
                  pl.BlockSpec((n_exp, LANES), lambda i: (0, 0))],
        out_specs=(tok, tok, tok, pl.BlockSpec((n_exp, LANES), lambda i: (0, 0))),
        scratch_shapes=[pltpu.VMEM((n_exp, LANES), F32)],
        compiler_params=_cparams(("arbitrary",), 32),
        name="router",
    )(h, wt, bias)


def _dest_kernel(e_ref, r_ref, st_ref, d_ref):
    n_exp = st_ref.shape[0]
    tm = e_ref.shape[1]
    eio = lax.broadcasted_iota(jnp.int32, (n_exp, tm), 0)
    start = st_ref[:, 0:1]
    rows = [jnp.sum(jnp.where(eio == e_ref[k:k + 1, :], start, 0.0), axis=0, keepdims=True)
            for k in range(TOP_K)]
    d_ref[...] = jnp.concatenate(rows, axis=0).astype(jnp.int32) + r_ref[...]


def _dest(top_e, rank, starts):
    n_tok = top_e.shape[1]
    n_exp = starts.shape[0]
    tm = 1024
    st = jnp.broadcast_to(starts.astype(F32)[:, None], (n_exp, LANES))
    tok = pl.BlockSpec((TOP_K, tm), lambda i: (0, i))
    return pl.pallas_call(
        _dest_kernel,
        out_shape=jax.ShapeDtypeStruct((TOP_K, n_tok), jnp.int32),
        grid=(n_tok // tm,),
        in_specs=[tok, tok, pl.BlockSpec((n_exp, LANES), lambda i: (0, 0))],
        out_specs=tok,
        compiler_params=_cparams(("parallel",), 32),
        name="dest",
    )(top_e, rank, st)


def _pack_bf16_pairs(val):
    half = val.shape[1] // 2
    lo = pltpu.bitcast(val[:, :half].astype(BF16).astype(F32), U32)
    hi = pltpu.bitcast(val[:, half:].astype(BF16).astype(F32), U32)
    return (lo >> 16) | (hi & jnp.uint32(0xFFFF0000))


def _unpack_bf16_pairs(words):
    lo = pltpu.bitcast(words << 16, F32)
    hi = pltpu.bitcast(words & jnp.uint32(0xFFFF0000), F32)
    return jnp.concatenate([lo, hi], axis=1)


def _to_row_tiles(dst_ref, slot, val):
    rows = val.shape[0]
    words = _pack_bf16_pairs(val)
    for j in range(ROW_TILE):
        dst_ref[slot, pl.ds(j, rows, stride=ROW_TILE), :] = words[:, j * LANES:(j + 1) * LANES]


def _row_tile_words(src_ref, idx, rows):
    return jnp.concatenate([src_ref[(*idx, pl.ds(j, rows, stride=ROW_TILE), slice(None))]
                            for j in range(ROW_TILE)], axis=1)


def _row_tile(r):
    return pl.ds(pl.multiple_of(r * ROW_TILE, ROW_TILE), ROW_TILE)


def _dispatch_kernel(pstart_ref, cnt_ref, nblk_ref, dest_ref, h_ref, xs_ref, ht_ref, zero_ref, sem, zsem, tsem):
    tm = h_ref.shape[0]
    i = pl.program_id(0)
    n_steps = pl.num_programs(0)
    cur = i % 2
    _to_row_tiles(ht_ref, cur, h_ref[...])

    @pl.when(i == 0)
    def _():
        zero_ref[...] = jnp.zeros_like(zero_ref)
        _zero_tail_blocks(zero_ref, xs_ref, nblk_ref[0], tsem, False)

    def issue(t, carry):
        for k in range(TOP_K):
            pltpu.make_async_copy(ht_ref.at[cur, _row_tile(t)], xs_ref.at[_row_tile(dest_ref[t * TOP_K + k])],
                                  sem.at[cur]).start(priority=k % 2)
        return carry
    lax.fori_loop(0, tm, issue, 0, unroll=2)

    def drain(slot):
        for k in range(TOP_K):
            pltpu.make_async_copy(ht_ref.at[slot], xs_ref.at[pl.ds(0, tm * ROW_TILE)], sem.at[slot]).wait()

    @pl.when(i > 0)
    def _():
        drain(1 - cur)

    @pl.when(i == n_steps - 1)
    def _():
        drain(cur)

    n_exp = cnt_ref.shape[0]
    per_step = pl.cdiv(n_exp, n_steps)
    sizes = [1 << s for s in reversed(range(EXPERT_ROWS.bit_length()))]

    def pad_copies(step, j, wait):
        e = jnp.minimum(step * per_step + j, n_exp - 1)
        n_pad = jnp.where(step * per_step + j < n_exp, (EXPERT_ROWS - cnt_ref[e] % EXPERT_ROWS) % EXPERT_ROWS, 0)
        row = pstart_ref[e] + cnt_ref[e]
        for size in sizes:
            @pl.when(n_pad & size != 0)
            def _(row=row, size=size):
                c = pltpu.make_async_copy(
                    zero_ref.at[pl.ds(0, size * ROW_TILE)],
                    xs_ref.at[pl.ds(pl.multiple_of(row * ROW_TILE, ROW_TILE), size * ROW_TILE)], zsem)
                c.wait() if wait else c.start()
            row = row + (n_pad & size)

    def pad_start(j, carry):
        pad_copies(i, j, False)
        return carry

    def pad_wait_previous(j, carry):
        pad_copies(i - 1, j, True)
        return carry

    def pad_wait(j, carry):
        pad_copies(i, j, True)
        return carry
    lax.fori_loop(0, per_step, pad_start, 0)

    @pl.when(i > 0)
    def _():
        lax.fori_loop(0, per_step, pad_wait_previous, 0)

    @pl.when(i == n_steps - 1)
    def _():
        lax.fori_loop(0, per_step, pad_wait, 0)
        _zero_tail_blocks(zero_ref, xs_ref, nblk_ref[0], tsem, True)


def _zero_tail_blocks(zero_ref, out_ref, n_blk, sem, wait):
    trows = zero_ref.shape[0]

    def body(b, carry):
        c = pltpu.make_async_copy(zero_ref, out_ref.at[pl.ds(pl.multiple_of(b * trows, trows), trows)], sem)
        c.wait() if wait else c.start()
        return carry
    lax.fori_loop(n_blk, out_ref.shape[0] // trows, body, 0)


def _dispatch(h, dest, pad_start, counts, n_blk):
    n_tok, d = h.shape
    n_exp = counts.shape[0]
    assert d == 2 * ROW_TILE * LANES
    tm = MOE_TOKEN_TILE
    n_rows = (pl.cdiv(n_tok * TOP_K, EXPERT_ROWS) + n_exp) * EXPERT_ROWS
    grid_spec = pltpu.PrefetchScalarGridSpec(
        num_scalar_prefetch=3,
        grid=(n_tok // tm,),
        in_specs=[pl.BlockSpec((TOP_K * tm,), lambda i, *_: (i,), memory_space=pltpu.SMEM),
                  pl.BlockSpec((tm, d), lambda i, *_: (i, 0))],
        out_specs=pl.BlockSpec(memory_space=pl.ANY),
        scratch_shapes=[pltpu.VMEM((2, tm * ROW_TILE, LANES), U32),
                        pltpu.VMEM((EXPERT_ROWS * ROW_TILE, LANES), U32),
                        pltpu.SemaphoreType.DMA((2,)), pltpu.SemaphoreType.DMA(()), pltpu.SemaphoreType.DMA(())],
    )
    return pl.pallas_call(
        _dispatch_kernel,
        out_shape=jax.ShapeDtypeStruct((n_rows * ROW_TILE, LANES), U32),
        grid_spec=grid_spec,
        compiler_params=_cparams(("arbitrary",), 32),
        name="dispatch",
    )(pad_start, counts, n_blk, dest, h)


def _experts_kernel(bstart_ref, bend_ref, nblk_ref, xs_ref, wgu_hbm, wdn_hbm, ys_ref,
                    xbuf, ybuf, act_ref, wgu_f32, wdn_f32, wgu_bf, wdn_bf, zero_ref, xsem, ysem, wsem, zsem):
    e = pl.program_id(0)
    n_blk = nblk_ref[0]
    trows = xbuf.shape[1]
    rows = trows // ROW_TILE
    ff = wdn_bf.shape[0]
    b0 = bstart_ref[e]
    b1 = bend_ref[e]

    def block_rows(b):
        return pl.ds(pl.multiple_of(b * trows, trows), trows)

    def x_copy(b):
        slot = b % EXPERT_RING
        return pltpu.make_async_copy(xs_ref.at[block_rows(b)], xbuf.at[slot], xsem.at[slot])

    def y_copy(b):
        slot = b % EXPERT_OUT_RING
        return pltpu.make_async_copy(ybuf.at[slot], ys_ref.at[block_rows(b)], ysem.at[slot])

    @pl.when(e == 0)
    def _():
        for i in range(EXPERT_RING):
            @pl.when(i < n_blk)
            def _():
                x_copy(i).start(priority=1)
        zero_ref[...] = jnp.zeros_like(zero_ref)
        _zero_tail_blocks(zero_ref, ys_ref, n_blk, zsem, False)

    n_exp = pl.num_programs(0)
    wslot = e % EXPERT_WEIGHT_RING

    def w_copies(x, slot):
        return (pltpu.make_async_copy(wgu_hbm.at[x], wgu_f32.at[slot], wsem.at[0, slot]),
                pltpu.make_async_copy(wdn_hbm.at[x], wdn_f32.at[slot], wsem.at[1, slot]))

    @pl.when(e == 0)
    def _():
        for i in range(EXPERT_WEIGHT_RING):
            @pl.when(i < n_exp)
            def _():
                for c in w_copies(i, i):
                    c.start()

    for c in w_copies(e, wslot):
        c.wait()
    wgu_bf[...] = wgu_f32[wslot].astype(BF16)
    wdn_bf[...] = wdn_f32[wslot].astype(BF16)

    @pl.when(e + EXPERT_WEIGHT_RING < n_exp)
    def _():
        for c in w_copies(e + EXPERT_WEIGHT_RING, wslot):
            c.start()

    def up_wait(blocks):
        for b in blocks:
            x_copy(b).wait()

    def up_compute(blocks):
        for b in blocks:
            xb = _unpack_bf16_pairs(_row_tile_words(xbuf, (b % EXPERT_RING,), rows)).astype(BF16)
            gu = jnp.dot(xb, wgu_bf[...], preferred_element_type=F32)
            act_ref[b - b0] = (jax.nn.silu(gu[:, :ff]) * gu[:, ff:]).astype(BF16)

    def up_start(blocks):
        for b in blocks:
            @pl.when(b + EXPERT_RING < n_blk)
            def _():
                x_copy(b + EXPERT_RING).start(priority=1)

    def down_wait(blocks):
        for b in blocks:
            @pl.when(b >= EXPERT_OUT_RING)
            def _():
                y_copy(b - EXPERT_OUT_RING).wait()

    def down_compute(blocks):
        for b in blocks:
            _to_row_tiles(ybuf, b % EXPERT_OUT_RING,
                          jnp.dot(act_ref[b - b0], wdn_bf[...], preferred_element_type=F32))

    def down_start(blocks):
        for b in blocks:
            y_copy(b).start(priority=1)
        last = blocks[-1]

        @pl.when(last == n_blk - 1)
        def _():
            for i in range(EXPERT_OUT_RING):
                @pl.when(last >= i)
                def _():
                    y_copy(last - i).wait()

    def up(blocks):
        up_wait(blocks)
        up_compute(blocks)
        up_start(blocks)

    def down(blocks):
        down_wait(blocks)
        down_compute(blocks)
        down_start(blocks)

    n_mine = b1 - b0

    def run_groups(fn):
        def body(p, carry):
            fn(tuple(b0 + EXPERT_GROUP * p + j for j in range(EXPERT_GROUP)))
            return carry
        lax.fori_loop(0, n_mine // EXPERT_GROUP, body, 0)
        size = EXPERT_GROUP // 2
        while size >= 1:
            @pl.when(n_mine & size != 0)
            def _(size=size):
                start = b0 + (n_mine // (2 * size)) * (2 * size)
                fn(tuple(start + j for j in range(size)))
            size //= 2

    @pl.when(n_mine == EXPERT_GROUP)
    def _():
        blocks = tuple(b0 + j for j in range(EXPERT_GROUP))
        up_wait(blocks)
        down_wait(blocks)
        up_compute(blocks)
        down_compute(blocks)
        up_start(blocks)
        down_start(blocks)

    @pl.when(n_mine != EXPERT_GROUP)
    def _():
        run_groups(up)
        run_groups(down)

    @pl.when(e == n_exp - 1)
    def _():
        _zero_tail_blocks(zero_ref, ys_ref, n_blk, zsem, True)


def _expert_blocks(counts):
    blocks = (counts + EXPERT_ROWS - 1) // EXPERT_ROWS
    bend = jnp.cumsum(blocks)
    bstart = bend - blocks
    i32 = lambda a: a.astype(jnp.int32)
    return i32(bstart), i32(bend), i32(bend[-1]).reshape(1), i32(bstart * EXPERT_ROWS)


def _experts(xs, bstart, bend, n_blk, w_gu, w_down, n_tok):
    n_exp, d, ff2 = w_gu.shape
    ff = w_down.shape[1]
    n_rows = xs.shape[0] // ROW_TILE
    assert n_rows % EXPERT_ROWS == 0 and d == 2 * ROW_TILE * LANES
    max_blocks = pl.cdiv(n_tok, EXPERT_ROWS)
    grid_spec = pltpu.PrefetchScalarGridSpec(
        num_scalar_prefetch=3,
        grid=(n_exp,),
        in_specs=[pl.BlockSpec(memory_space=pl.ANY), pl.BlockSpec(memory_space=pl.ANY),
                  pl.BlockSpec(memory_space=pl.ANY)],
        out_specs=pl.BlockSpec(memory_space=pl.ANY),
        scratch_shapes=[pltpu.VMEM((EXPERT_RING, EXPERT_ROWS * ROW_TILE, LANES), U32),
                        pltpu.VMEM((EXPERT_OUT_RING, EXPERT_ROWS * ROW_TILE, LANES), U32),
                        pltpu.VMEM((max_blocks, EXPERT_ROWS, ff), BF16),
                        pltpu.VMEM((EXPERT_WEIGHT_RING, d, ff2), F32),
                        pltpu.VMEM((EXPERT_WEIGHT_RING, ff, d), F32),
                        pltpu.VMEM((d, ff2), BF16), pltpu.VMEM((ff, d), BF16),
                        pltpu.VMEM((EXPERT_ROWS * ROW_TILE, LANES), U32),
                        pltpu.SemaphoreType.DMA((EXPERT_RING,)),
                        pltpu.SemaphoreType.DMA((EXPERT_OUT_RING,)),
                        pltpu.SemaphoreType.DMA((2, EXPERT_WEIGHT_RING)),
                        pltpu.SemaphoreType.DMA(())],
    )
    return pl.pallas_call(
        _experts_kernel,
        out_shape=jax.ShapeDtypeStruct(xs.shape, U32),
        grid_spec=grid_spec,
        compiler_params=_cparams(("arbitrary",), 48),
        name="experts",
    )(bstart, bend, n_blk, xs, w_gu, w_down)


def _combine_kernel(dest_ref, dnext_ref, gate_ref, h_ref, ys_ref, wgu_ref, wdn_ref, g_ref, b_ref, o_ref,
                    buf_ref, routed_ref, sem, *, alpha):
    tm = h_ref.shape[0]
    i = pl.program_id(0)
    cur = i % 2
    chunk = SUBLANES

    def issue(d_ref, slot, t):
        for k in range(TOP_K):
            pltpu.make_async_copy(ys_ref.at[_row_tile(d_ref[t * TOP_K + k])], buf_ref.at[slot, k, _row_tile(t)],
                                  sem.at[slot]).start(priority=k % 2)

    @pl.when(i == 0)
    def _():
        def first(t, carry):
            issue(dest_ref, 0, t)
            return carry
        lax.fori_loop(0, tm, first, 0, unroll=2)

    for k in range(TOP_K):
        pltpu.make_async_copy(ys_ref.at[pl.ds(0, tm * ROW_TILE)], buf_ref.at[cur, k], sem.at[cur]).wait()

    def weighted_sum(c):
        tok = pl.ds(pl.multiple_of(c * chunk, chunk), chunk)
        gate = gate_ref[tok, :]
        total = None
        for k in range(TOP_K):
            words = jnp.concatenate(
                [buf_ref[cur, k, pl.ds(pl.multiple_of(c * (chunk * ROW_TILE), chunk * ROW_TILE) + j, chunk,
                                       stride=ROW_TILE), :] for j in range(ROW_TILE)], axis=1)
            term = gate[:, k:k + 1] * _unpack_bf16_pairs(words)
            total = term if total is None else total + term
        routed_ref[tok, :] = total

    @pl.when(i + 1 < pl.num_programs(0))
    def _():
        def body(c, carry):
            for t in range(chunk):
                issue(dnext_ref, 1 - cur, c * chunk + t)
            weighted_sum(c)
            return carry
        lax.fori_loop(0, tm // chunk, body, 0)

    @pl.when(i + 1 == pl.num_programs(0))
    def _():
        def body(c, carry):
            weighted_sum(c)
            return carry
        lax.fori_loop(0, tm // chunk, body, 0)

    h = h_ref[...]
    ff = wdn_ref.shape[0]
    gu = jnp.dot(h.astype(BF16), wgu_ref[...], preferred_element_type=F32)
    act = (jax.nn.silu(gu[:, :ff]) * gu[:, ff:]).astype(BF16)
    acc = alpha * h + jnp.dot(act, wdn_ref[...], preferred_element_type=F32) + routed_ref[...]
    o_ref[...] = _layer_norm(acc, g_ref[...], b_ref[...])


def _combine(h, ys, dest, gate_t, shared_w_gu, shared_w_down, ln_g, ln_b, alpha):
    n_tok, d = h.shape
    tm = MOE_TOKEN_TILE
    n_tiles = n_tok // tm
    full = lambda a: pl.BlockSpec(a.shape, lambda i: (0,) * a.ndim)
    wgu = shared_w_gu.astype(BF16)
    wdn = shared_w_down.astype(BF16)
    g2 = ln_g.reshape(1, d)
    b2 = ln_b.reshape(1, d)
    return pl.pallas_call(
        functools.partial(_combine_kernel, alpha=alpha),
        out_shape=jax.ShapeDtypeStruct((n_tok, d), F32),
        grid=(n_tiles,),
        in_specs=[pl.BlockSpec((TOP_K * tm,), lambda i: (i,), memory_space=pltpu.SMEM),
                  pl.BlockSpec((TOP_K * tm,), lambda i: (jnp.minimum(i + 1, n_tiles - 1),),
                               memory_space=pltpu.SMEM),
                  pl.BlockSpec((tm, TOP_K), lambda i: (i, 0)),
                  pl.BlockSpec((tm, d), lambda i: (i, 0)),
                  pl.BlockSpec(memory_space=pl.ANY),
                  full(wgu), full(wdn), full(g2), full(b2)],
        out_specs=pl.BlockSpec((tm, d), lambda i: (i, 0)),
        scratch_shapes=[pltpu.VMEM((2, TOP_K, tm * ROW_TILE, LANES), U32), pltpu.VMEM((tm, d), F32),
                        pltpu.SemaphoreType.DMA((2,))],
        compiler_params=_cparams(("arbitrary",), 48),
        name="combine",
    )(dest, dest, gate_t, h, ys, wgu, wdn, g2, b2)


def _moe(h, router_w, router_bias, w_gu, w_down, shared_w_gu, shared_w_down, ln_g, ln_b, alpha):
    top_e, gate, rank, cnt = _router(h, router_w, router_bias)
    counts = cnt[:, 0].astype(jnp.int32)
    bstart, bend, n_blk, pad_start = _expert_blocks(counts)
    dest = _dest(top_e, rank, pad_start)
    dest_tiles = dest.T.reshape(-1)
    xs = _dispatch(h, dest_tiles, pad_start, counts, n_blk)
    ys = _experts(xs, bstart, bend, n_blk, w_gu, w_down, h.shape[0])
    return _combine(h, ys, dest_tiles, gate.T, shared_w_gu, shared_w_down, ln_g, ln_b, alpha)


def kernel(x, w_in, att_norm_g, lam_re, lam_im, log_step, b_re, b_im, c_re, c_im, d_skip, w_glu, b_glu,
           ssm_norm_g, w_out, ln1_g, ln1_b, router_w, router_bias, w_gu, w_down, shared_w_gu,
           shared_w_down, ln2_g, ln2_b):
    bsz, seq, d = x.shape
    depth = w_in.shape[0]
    alpha = (2 * depth) ** 0.25
    h = x.reshape(bsz * seq, d)
    for i in range(depth):
        proj = _inproj(h, w_in[i].astype(BF16), seq)
        o_att = _attention(proj, bsz, seq)
        y_ssm = _s5(proj.reshape(bsz, seq, -1), 3 * ATT_WIDTH, lam_re[i], lam_im[i], log_step[i],
                    b_re[i], b_im[i], c_re[i], c_im[i], d_skip[i])
        h = _mixout(o_att, y_ssm.reshape(bsz * seq, -1), h, w_glu[i], b_glu[i], att_norm_g[i],
                    ssm_norm_g[i], w_out[i], ln1_g[i], ln1_b[i], alpha)
        h = _moe(h, router_w[i], router_bias[i], w_gu[i], w_down[i], shared_w_gu[i], shared_w_down[i],
                 ln2_g[i], ln2_b[i], alpha)
    return h.reshape(bsz, seq, d)
```

```python
import functools

import jax
import jax.numpy as jnp
from jax import lax
from jax.experimental import pallas as pl
from jax.experimental.pallas import tpu as pltpu

F32 = jnp.float32
BF16 = jnp.bfloat16
U32 = jnp.uint32

ATT_HEADS = 8
HEAD_DIM = 64
ATT_WIDTH = ATT_HEADS * HEAD_DIM
SSM_CH = 16
SSM_STATE = 64
S5_SLAB = 64
ROPE_THETA = 500000.0
ROT_DIM = HEAD_DIM // 4
DILATIONS = (1, 4, 16)
ATT_BLOCK = 128
ATT_GROUP = 16
TOP_K = 8
N_EXPERT_GROUPS = 8
TOPK_GROUPS = 4
ROUTED_SCALE = 2.5
LN_EPS = 1e-5
RMS_EPS = 1e-6

LANES = 128
SUBLANES = 8
EXPERT_ROWS = 144
MOE_TOKEN_TILE = 512
EXPERT_RING = 16
EXPERT_OUT_RING = 16
EXPERT_GROUP = 4
EXPERT_WEIGHT_RING = 6
ROW_TILE = 4
NEG_INF = float("-inf")


def _cparams(sem, vmem_mb):
    return pltpu.CompilerParams(dimension_semantics=sem, vmem_limit_bytes=vmem_mb * 1024 * 1024)


def _inproj_kernel(x_ref, w_ref, cos_ref, sa_ref, sb_ref, o_ref, *, n_rot_cols):
    xb = x_ref[...].astype(BF16)
    cosf = cos_ref[...]
    sa = sa_ref[...]
    sb = sb_ref[...]
    width = o_ref.shape[1]
    chunk = 512
    for c in range(width // chunk):
        r = jnp.dot(xb, w_ref[:, c * chunk:(c + 1) * chunk], preferred_element_type=F32)
        if c * chunk < n_rot_cols:
            parts = []
            for s in range(chunk // LANES):
                t = r[:, s * LANES:(s + 1) * LANES]
                parts.append(t * cosf + pltpu.roll(t, LANES - ROT_DIM // 2, 1) * sa
                             + pltpu.roll(t, ROT_DIM // 2, 1) * sb)
            r = jnp.concatenate(parts, axis=1)
        o_ref[:, c * chunk:(c + 1) * chunk] = r


def _rope_lane_tables(seq):
    half = ROT_DIM // 2
    inv_freq = jnp.power(jnp.float32(ROPE_THETA), -jnp.arange(half, dtype=F32) / half)
    ang = jnp.arange(seq, dtype=F32)[:, None] * inv_freq[None, :]
    cos, sin = jnp.cos(ang), jnp.sin(ang)
    rest = HEAD_DIM - ROT_DIM
    cos_h = jnp.concatenate([cos, cos, jnp.ones((seq, rest), F32)], axis=1)
    sa_h = jnp.concatenate([-sin, jnp.zeros((seq, half + rest), F32)], axis=1)
    sb_h = jnp.concatenate([jnp.zeros((seq, half), F32), sin, jnp.zeros((seq, rest), F32)], axis=1)
    rep = LANES // HEAD_DIM
    return tuple(jnp.tile(t, (1, rep)) for t in (cos_h, sa_h, sb_h))


def _inproj(x2d, w_in_bf, seq):
    n_tok, d = x2d.shape
    width = w_in_bf.shape[1]
    tm = 1024
    cosf, sa, sb = _rope_lane_tables(seq)
    tab_spec = pl.BlockSpec((tm, LANES), lambda i: (i % (seq // tm), 0))
    return pl.pallas_call(
        functools.partial(_inproj_kernel, n_rot_cols=2 * ATT_WIDTH),
        out_shape=jax.ShapeDtypeStruct((n_tok, width), F32),
        grid=(n_tok // tm,),
        in_specs=[pl.BlockSpec((tm, d), lambda i: (i, 0)),
                  pl.BlockSpec((d, width), lambda i: (0, 0)),
                  tab_spec, tab_spec, tab_spec],
        out_specs=pl.BlockSpec((tm, width), lambda i: (i, 0)),
        compiler_params=_cparams(("parallel",), 48),
        name="inproj",
    )(x2d, w_in_bf, cosf, sa, sb)


def _attn_kernel(q_ref, k_ref, v_ref, o_ref, qs_ref, ks_ref, vs_ref, tmp_ref, ob_ref, lb_ref, band_ref,
                 first_ref, *, seq):
    blk = ATT_BLOCK
    lane = lax.broadcasted_iota(jnp.int32, (1, LANES), 1)
    head0 = lane < HEAD_DIM
    scale = HEAD_DIM ** -0.5
    d1, d2 = DILATIONS[1], DILATIONS[2]
    assert DILATIONS[0] == 1 and d2 == d1 * d1
    seg = seq // d1
    sub = seg // d1

    qi = lax.broadcasted_iota(jnp.int32, (blk, 2 * blk), 0)
    kj = lax.broadcasted_iota(jnp.int32, (blk, 2 * blk), 1)
    dist = qi + blk - kj
    band_ref[...] = jnp.where((dist >= 0) & (dist <= blk), 0.0, NEG_INF)
    first_ref[...] = jnp.where((dist >= 0) & (kj >= blk), 0.0, NEG_INF)

    n_class = (1, d1, d2)
    class_len = (seq, seg, sub)
    base = [0]
    for c in range(len(DILATIONS)):
        base.append(base[c] + n_class[c] * (class_len[c] + blk))

    def kv_row0(c, g):
        return base[c] + g * (class_len[c] + blk)

    qs_ref[0] = (q_ref[...] * scale).astype(BF16)
    for a in range(d1):
        x = q_ref[pl.ds(a, seg, stride=d1), :] * scale
        tmp_ref[a * seg:(a + 1) * seg, :] = x
        qs_ref[1, a * seg:(a + 1) * seg, :] = x.astype(BF16)
    for g in range(d2):
        qs_ref[2, g * sub:(g + 1) * sub, :] = tmp_ref[pl.ds((g // d1) * seg + g % d1, sub, stride=d1),
                                                      :].astype(BF16)
    for src_ref, dst_ref in ((k_ref, ks_ref), (v_ref, vs_ref)):
        for c in range(len(DILATIONS)):
            for g in range(n_class[c]):
                dst_ref[kv_row0(c, g):kv_row0(c, g) + blk, :] = jnp.zeros((blk, LANES), BF16)
        dst_ref[kv_row0(0, 0) + blk:kv_row0(0, 0) + blk + seq, :] = src_ref[...].astype(BF16)
        for a in range(d1):
            x = src_ref[pl.ds(a, seg, stride=d1), :]
            tmp_ref[a * seg:(a + 1) * seg, :] = x
            dst_ref[kv_row0(1, a) + blk:kv_row0(1, a) + blk + seg, :] = x.astype(BF16)
        for g in range(d2):
            dst_ref[kv_row0(2, g) + blk:kv_row0(2, g) + blk + sub, :] = tmp_ref[
                pl.ds((g // d1) * seg + g % d1, sub, stride=d1), :].astype(BF16)

    def one_block(c, g, n, out_rows, bias_ref):
        q = qs_ref[c, pl.ds(aligned(g * class_len[c] + n * blk), blk), :]
        kv_rows = pl.ds(aligned(kv_row0(c, g) + n * blk), 2 * blk)
        kk = ks_ref[kv_rows, :]
        vv = vs_ref[kv_rows, :]
        outs = []
        lses = []
        for h in range(LANES // HEAD_DIM):
            hm = head0 if h == 0 else jnp.logical_not(head0)
            qh = jnp.where(hm, q, jnp.zeros_like(q))
            s = lax.dot_general(qh, kk, (((1,), (1,)), ((), ())), preferred_element_type=F32)
            s = s + bias_ref[...]
            m = jnp.max(s, axis=-1, keepdims=True)
            p = jnp.exp(s - m)
            den = jnp.sum(p, axis=-1, keepdims=True)
            outs.append(jnp.dot(p.astype(BF16), vv, preferred_element_type=F32) / den)
            lses.append(m + jnp.log(den))
        ob_ref[c, out_rows, :] = jnp.where(head0, outs[0], outs[1])
        lb_ref[c, out_rows, :] = jnp.where(head0, lses[0], lses[1])

    def run_blocks(n_blocks, fn):
        group = max(g for g in range(1, ATT_GROUP + 1) if n_blocks % g == 0)
        if n_blocks == group:
            for g in range(group):
                fn(g)
            return
        def body(it, carry):
            for g in range(group):
                fn(it * group + g)
            return carry
        lax.fori_loop(0, n_blocks // group, body, 0)

    def aligned(x):
        return x if isinstance(x, int) else pl.multiple_of(x, blk)

    one_block(0, 0, 0, pl.ds(0, blk), first_ref)
    run_blocks(seq // blk - 1,
               lambda i: one_block(0, 0, i + 1, pl.ds(aligned((i + 1) * blk), blk), band_ref))

    nb1 = seg // blk
    run_blocks(d1, lambda a: one_block(1, a, 0, pl.ds(a, blk, stride=d1), first_ref))
    def later1(i):
        a = i // (nb1 - 1)
        n = i - a * (nb1 - 1) + 1
        one_block(1, a, n, pl.ds(a + n * (d1 * blk), blk, stride=d1), band_ref)
    run_blocks(d1 * (nb1 - 1), later1)

    assert sub == blk
    def only2(g):
        a = g // d1
        one_block(2, g, 0, pl.ds(a + d1 * (g - a * d1), blk, stride=d2), first_ref)
    run_blocks(d2, only2)

    rc = 256
    def merge(i, carry):
        sl = pl.ds(pl.multiple_of(i * rc, rc), rc)
        l0 = lb_ref[0, sl, :]
        l1 = lb_ref[1, sl, :]
        l2 = lb_ref[2, sl, :]
        mx = jnp.maximum(jnp.maximum(l0, l1), l2)
        e0 = jnp.exp(l0 - mx)
        e1 = jnp.exp(l1 - mx)
        e2 = jnp.exp(l2 - mx)
        o_ref[sl, :] = ((e0 * ob_ref[0, sl, :] + e1 * ob_ref[1, sl, :] + e2 * ob_ref[2, sl, :])
                        / (e0 + e1 + e2))
        return carry
    lax.fori_loop(0, seq // rc, merge, 0)


def _attention(proj, bsz, seq):
    n_tok = proj.shape[0]
    pairs = ATT_WIDTH // LANES
    assert seq % (ATT_BLOCK * max(DILATIONS)) == 0
    kv_rows = sum(seq + d * ATT_BLOCK for d in DILATIONS)
    blk = (seq, LANES)
    return pl.pallas_call(
        functools.partial(_attn_kernel, seq=seq),
        out_shape=jax.ShapeDtypeStruct((n_tok, ATT_WIDTH), F32),
        grid=(bsz, pairs),
        in_specs=[pl.BlockSpec(blk, lambda b, h: (b, h)),
                  pl.BlockSpec(blk, lambda b, h: (b, pairs + h)),
                  pl.BlockSpec(blk, lambda b, h: (b, 2 * pairs + h))],
        out_specs=pl.BlockSpec(blk, lambda b, h: (b, h)),
        scratch_shapes=[pltpu.VMEM((len(DILATIONS), seq, LANES), BF16),
                        pltpu.VMEM((kv_rows, LANES), BF16),
                        pltpu.VMEM((kv_rows, LANES), BF16),
                        pltpu.VMEM((seq, LANES), F32),
                        pltpu.VMEM((len(DILATIONS), seq, LANES), F32),
                        pltpu.VMEM((len(DILATIONS), seq, LANES), F32),
                        pltpu.VMEM((ATT_BLOCK, 2 * ATT_BLOCK), F32),
                        pltpu.VMEM((ATT_BLOCK, 2 * ATT_BLOCK), F32)],
        compiler_params=_cparams(("parallel", "parallel"), 40),
        name="attn",
    )(proj, proj, proj)


def _s5_kernel(u_ref, bm_ref, lam_ref, cm_ref, dk_ref, o_ref, us_ref, st_ref, ys_ref, carry_ref, *, tc):
    bsz = u_ref.shape[0]
    half = st_ref.shape[1] // 2
    slab_rows = S5_SLAB * bsz
    n_slabs = tc // S5_SLAB

    @pl.when(pl.program_id(1) == 0)
    def _():
        carry_ref[...] = jnp.zeros_like(carry_ref)

    for b in range(bsz):
        us_ref[pl.ds(b, tc, stride=bsz), :] = u_ref[b]

    bm = bm_ref[0]
    cm = cm_ref[0]
    lam = lam_ref[0]
    lam_re = lam[:, :half]
    lam_im = lam[:, half:]

    def slab(s):
        return slice(s * slab_rows, (s + 1) * slab_rows)

    def project_in(s):
        st_ref[slab(s), :] = jnp.dot(us_ref[slab(s), :].astype(BF16), bm, preferred_element_type=F32)

    def project_out(s):
        ys_ref[slab(s), :] = jnp.dot(st_ref[slab(s), :].astype(BF16), cm, preferred_element_type=F32)

    def scan(s, xr, xi):
        for t in range(S5_SLAB):
            sl = slice(s * slab_rows + t * bsz, s * slab_rows + (t + 1) * bsz)
            xr, xi = (lam_re * xr - lam_im * xi + st_ref[sl, :half],
                      lam_re * xi + lam_im * xr + st_ref[sl, half:])
            st_ref[sl, :half] = xr
            st_ref[sl, half:] = xi
        return xr, xi

    xr, xi = carry_ref[:, :half], carry_ref[:, half:]
    project_in(0)
    for s in range(n_slabs):
        if s + 1 < n_slabs:
            project_in(s + 1)
        xr, xi = scan(s, xr, xi)
        if s >= 1:
            project_out(s - 1)
    project_out(n_slabs - 1)
    carry_ref[:, :half] = xr
    carry_ref[:, half:] = xi

    dk = dk_ref[...]
    for b in range(bsz):
        o_ref[b] = ys_ref[pl.ds(b, tc, stride=bsz), :] + dk * u_ref[b]


def _s5_params(lam_re, lam_im, log_step, b_re, b_im, c_re, c_im, bsz):
    groups = lam_re.shape[0]
    gpc = LANES // SSM_CH
    n_chunks = groups // gpc
    lam = lax.complex(lam_re.astype(F32), lam_im.astype(F32))
    step = jnp.exp(log_step.astype(F32))[:, None]
    lam_bar = jnp.exp(lam * step)
    bmat = lax.complex(b_re.astype(F32), b_im.astype(F32))
    b_bar = ((lam_bar - 1.0) / lam)[..., None] * bmat
    eye = jnp.eye(gpc, dtype=F32)

    def block_diag_in(t):
        t = t.reshape(n_chunks, gpc, SSM_STATE, SSM_CH)
        return jnp.einsum('ngpc,gh->ngchp', t, eye).reshape(n_chunks, gpc * SSM_CH, gpc * SSM_STATE)

    def block_diag_out(t):
        t = t.reshape(n_chunks, gpc, SSM_CH, SSM_STATE)
        return jnp.einsum('ngcp,gh->ngphc', t, eye).reshape(n_chunks, gpc * SSM_STATE, gpc * SSM_CH)

    bm = jnp.concatenate([block_diag_in(b_bar.real), block_diag_in(b_bar.imag)], axis=2).astype(BF16)
    cm = jnp.concatenate([block_diag_out(c_re.astype(F32)), block_diag_out(-c_im.astype(F32))],
                         axis=1).astype(BF16)
    lam_row = jnp.concatenate([lam_bar.real.reshape(n_chunks, gpc * SSM_STATE),
                               lam_bar.imag.reshape(n_chunks, gpc * SSM_STATE)], axis=1)
    lam_t = jnp.broadcast_to(lam_row[:, None, :], (n_chunks, bsz, 2 * gpc * SSM_STATE))
    return bm, lam_t, cm, n_chunks


def _s5(proj3, u_col0, lam_re, lam_im, log_step, b_re, b_im, c_re, c_im, d_skip):
    bsz, seq, _ = proj3.shape
    assert bsz == SUBLANES
    bm, lam_t, cm, n_chunks = _s5_params(lam_re, lam_im, log_step, b_re, b_im, c_re, c_im, bsz)
    width = n_chunks * LANES
    tc = 512
    st_cols = bm.shape[2]
    ublk0 = u_col0 // LANES
    return pl.pallas_call(
        functools.partial(_s5_kernel, tc=tc),
        out_shape=jax.ShapeDtypeStruct((bsz, seq, width), F32),
        grid=(n_chunks, seq // tc),
        in_specs=[pl.BlockSpec((bsz, tc, LANES), lambda c, t: (0, t, ublk0 + c)),
                  pl.BlockSpec((1, LANES, st_cols), lambda c, t: (c, 0, 0)),
                  pl.BlockSpec((1, bsz, st_cols), lambda c, t: (c, 0, 0)),
                  pl.BlockSpec((1, st_cols, LANES), lambda c, t: (c, 0, 0)),
                  pl.BlockSpec((1, LANES), lambda c, t: (0, c))],
        out_specs=pl.BlockSpec((bsz, tc, LANES), lambda c, t: (0, t, c)),
        scratch_shapes=[pltpu.VMEM((tc * bsz, LANES), F32),
                        pltpu.VMEM((tc * bsz, st_cols), F32),
                        pltpu.VMEM((tc * bsz, LANES), F32),
                        pltpu.VMEM((bsz, st_cols), F32)],
        compiler_params=_cparams(("arbitrary", "arbitrary"), 48),
        name="s5",
    )(proj3, bm, lam_t, cm, d_skip.reshape(1, width).astype(F32))


def _layer_norm(v, g, b):
    mu = jnp.mean(v, axis=-1, keepdims=True)
    var = jnp.mean(jnp.square(v - mu), axis=-1, keepdims=True)
    return (v - mu) * lax.rsqrt(var + LN_EPS) * g + b


def _rms_norm(v, g):
    return v * lax.rsqrt(jnp.mean(jnp.square(v), axis=-1, keepdims=True) + RMS_EPS) * g


def _mixout_kernel(att_ref, ssm_ref, x_ref, wglu_ref, bglu_ref, ag_ref, sg_ref, wout_ref, g_ref, b_ref,
                   o_ref, *, alpha):
    y = jax.nn.gelu(ssm_ref[...])
    z = jnp.dot(y.astype(BF16), wglu_ref[...], preferred_element_type=F32) + bglu_ref[...]
    o_ssm = y * jax.nn.sigmoid(z)
    a = _rms_norm(att_ref[...], ag_ref[...]).astype(BF16)
    s = _rms_norm(o_ssm, sg_ref[...]).astype(BF16)
    wa = att_ref.shape[1]
    mix = (jnp.dot(a, wout_ref[:wa, :], preferred_element_type=F32)
           + jnp.dot(s, wout_ref[wa:, :], preferred_element_type=F32))
    o_ref[...] = _layer_norm(alpha * x_ref[...] + mix, g_ref[...], b_ref[...])


def _mixout(o_att, y_ssm, x2d, w_glu, b_glu, att_g, ssm_g, w_out, ln_g, ln_b, alpha):
    n_tok, d = x2d.shape
    wa = o_att.shape[1]
    ws = y_ssm.shape[1]
    tm = 1024
    row = lambda w: pl.BlockSpec((tm, w), lambda i: (i, 0))
    full = lambda a: pl.BlockSpec(a.shape, lambda i: (0,) * a.ndim)
    args = (o_att, y_ssm, x2d, w_glu.astype(BF16), b_glu.reshape(1, ws), att_g.reshape(1, wa),
            ssm_g.reshape(1, ws), w_out.astype(BF16), ln_g.reshape(1, d), ln_b.reshape(1, d))
    return pl.pallas_call(
        functools.partial(_mixout_kernel, alpha=alpha),
        out_shape=jax.ShapeDtypeStruct((n_tok, d), F32),
        grid=(n_tok // tm,),
        in_specs=[row(wa), row(ws), row(d)] + [full(a) for a in args[3:]],
        out_specs=row(d),
        compiler_params=_cparams(("parallel",), 48),
        name="mixout",
    )(*args)


def _split_bf16(v):
    hi = v.astype(BF16)
    lo = (v - hi.astype(F32)).astype(BF16)
    return hi, lo


def _router_kernel(h_ref, wt_ref, bias_ref, e_ref, g_ref, r_ref, cnt_ref, run_ref):
    tm = h_ref.shape[0]
    n_exp = wt_ref.shape[0]
    gsz = n_exp // N_EXPERT_GROUPS

    @pl.when(pl.program_id(0) == 0)
    def _():
        run_ref[...] = jnp.zeros_like(run_ref)

    w_hi, w_lo = _split_bf16(wt_ref[...])
    h_hi, h_lo = _split_bf16(h_ref[...])
    nt = (((1,), (1,)), ((), ()))
    logits = (lax.dot_general(w_hi, h_hi, nt, preferred_element_type=F32)
              + lax.dot_general(w_hi, h_lo, nt, preferred_element_type=F32)
              + lax.dot_general(w_lo, h_hi, nt, preferred_element_type=F32))
    scores = jax.nn.sigmoid(logits)
    choice = scores + bias_ref[:, 0:1]

    gio = lax.broadcasted_iota(jnp.int32, (gsz, tm), 0).astype(F32)
    gscore = []
    for g in range(N_EXPERT_GROUPS):
        cg = choice[g * gsz:(g + 1) * gsz, :]
        m1 = jnp.max(cg, axis=0, keepdims=True)
        i1 = jnp.min(jnp.where(cg == m1, gio, float(gsz)), axis=0, keepdims=True)
        m2 = jnp.max(jnp.where(gio == i1, NEG_INF, cg), axis=0, keepdims=True)
        gscore.append(m1 + m2)
    masked = []
    for g in range(N_EXPERT_GROUPS):
        beat = jnp.zeros((1, tm), F32)
        for o in range(N_EXPERT_GROUPS):
            if o == g:
                continue
            wins = (gscore[o] >= gscore[g]) if o < g else (gscore[o] > gscore[g])
            beat = beat + jnp.where(wins, 1.0, 0.0)
        keep = beat < float(TOPK_GROUPS)
        masked.append(jnp.where(keep, choice[g * gsz:(g + 1) * gsz, :], NEG_INF))
    cur = jnp.concatenate(masked, axis=0)

    eio = lax.broadcasted_iota(jnp.int32, (n_exp, tm), 0).astype(F32)
    idxs = []
    gates = []
    candidates = cur
    for _ in range(TOP_K):
        m = jnp.max(cur, axis=0, keepdims=True)
        idx = jnp.min(jnp.where(cur == m, eio, float(n_exp)), axis=0, keepdims=True)
        hit = eio == idx
        idxs.append(idx)
        gates.append(jnp.sum(jnp.where(hit, scores, 0.0), axis=0, keepdims=True))
        cur = jnp.where(hit, NEG_INF, cur)
    onehot = jnp.where(cur != candidates, 1.0, 0.0)
    gate = jnp.concatenate(gates, axis=0)
    gate = ROUTED_SCALE * gate / (jnp.sum(gate, axis=0, keepdims=True) + 1e-20)

    si = lax.broadcasted_iota(jnp.int32, (tm, tm), 0)
    ti = lax.broadcasted_iota(jnp.int32, (tm, tm), 1)
    upper = jnp.where(si < ti, 1.0, 0.0).astype(BF16)
    before = jnp.dot(onehot.astype(BF16), upper, preferred_element_type=F32) + run_ref[:, 0:1]
    ranks = [jnp.sum(jnp.where(eio == idx, before, 0.0), axis=0, keepdims=True) for idx in idxs]

    e_ref[...] = jnp.concatenate(idxs, axis=0).astype(jnp.int32)
    g_ref[...] = gate
    r_ref[...] = jnp.concatenate(ranks, axis=0).astype(jnp.int32)
    run_ref[...] = run_ref[...] + jnp.sum(onehot, axis=1, keepdims=True)
    cnt_ref[...] = run_ref[...]


def _router(h, router_w, router_bias):
    n_tok, d = h.shape
    n_exp = router_w.shape[1]
    tm = 512
    wt = router_w.astype(F32).T
    bias = jnp.broadcast_to(router_bias.astype(F32)[:, None], (n_exp, LANES))
    tok = pl.BlockSpec((TOP_K, tm), lambda i: (0, i))
    return pl.pallas_call(
        _router_kernel,
        out_shape=(jax.ShapeDtypeStruct((TOP_K, n_tok), jnp.int32),
                   jax.ShapeDtypeStruct((TOP_K, n_tok), F32),
                   jax.ShapeDtypeStruct((TOP_K, n_tok), jnp.int32),
                   jax.ShapeDtypeStruct((n_exp, LANES), F32)),
        grid=(n_tok // tm,),
        in_specs=[pl.BlockSpec((tm, d), lambda i: (i, 0)),
                  pl.BlockSpec((n_exp, d), lambda i: (0, 0)),
                  pl.BlockSpec((n_exp, LANES), lambda i: (0, 0))],
        out_specs=(tok, tok, tok, pl.BlockSpec((n_exp, LANES), lambda i: (0, 0))),
        scratch_shapes=[pltpu.VMEM((n_exp, LANES), F32)],
        compiler_params=_cparams(("arbitrary",), 32),
        name="router",
    )(h, wt, bias)


def _dest_kernel(e_ref, r_ref, st_ref, d_ref):
    n_exp = st_ref.shape[0]
    tm = e_ref.shape[1]
    eio = lax.broadcasted_iota(jnp.int32, (n_exp, tm), 0)
    start = st_ref[:, 0:1]
    rows = [jnp.sum(jnp.where(eio == e_ref[k:k + 1, :], start, 0.0), axis=0, keepdims=True)
            for k in range(TOP_K)]
    d_ref[...] = jnp.concatenate(rows, axis=0).astype(jnp.int32) + r_ref[...]


def _dest(top_e, rank, starts):
    n_tok = top_e.shape[1]
    n_exp = starts.shape[0]
    tm = 1024
    st = jnp.broadcast_to(starts.astype(F32)[:, None], (n_exp, LANES))
    tok = pl.BlockSpec((TOP_K, tm), lambda i: (0, i))
    return pl.pallas_call(
        _dest_kernel,
        out_shape=jax.ShapeDtypeStruct((TOP_K, n_tok), jnp.int32),
        grid=(n_tok // tm,),
        in_specs=[tok, tok, pl.BlockSpec((n_exp, LANES), lambda i: (0, 0))],
        out_specs=tok,
        compiler_params=_cparams(("parallel",), 32),
        name="dest",
    )(top_e, rank, st)


def _pack_bf16_pairs(val):
    half = val.shape[1] // 2
    lo = pltpu.bitcast(val[:, :half].astype(BF16).astype(F32), U32)
    hi = pltpu.bitcast(val[:, half:].astype(BF16).astype(F32), U32)
    return (lo >> 16) | (hi & jnp.uint32(0xFFFF0000))


def _unpack_bf16_pairs(words):
    lo = pltpu.bitcast(words << 16, F32)
    hi = pltpu.bitcast(words & jnp.uint32(0xFFFF0000), F32)
    return jnp.concatenate([lo, hi], axis=1)


def _to_row_tiles(dst_ref, slot, val):
    rows = val.shape[0]
    words = _pack_bf16_pairs(val)
    for j in range(ROW_TILE):
        dst_ref[slot, pl.ds(j, rows, stride=ROW_TILE), :] = words[:, j * LANES:(j + 1) * LANES]


def _row_tile_words(src_ref, idx, rows):
    return jnp.concatenate([src_ref[(*idx, pl.ds(j, rows, stride=ROW_TILE), slice(None))]
                            for j in range(ROW_TILE)], axis=1)


def _row_tile(r):
    return pl.ds(pl.multiple_of(r * ROW_TILE, ROW_TILE), ROW_TILE)


def _dispatch_kernel(pstart_ref, cnt_ref, nblk_ref, dest_ref, h_ref, xs_ref, ht_ref, zero_ref, sem, zsem, tsem):
    tm = h_ref.shape[0]
    i = pl.program_id(0)
    n_steps = pl.num_programs(0)
    cur = i % 2
    _to_row_tiles(ht_ref, cur, h_ref[...])

    @pl.when(i == 0)
    def _():
        zero_ref[...] = jnp.zeros_like(zero_ref)
        _zero_tail_blocks(zero_ref, xs_ref, nblk_ref[0], tsem, False)

    def issue(t, carry):
        for k in range(TOP_K):
            pltpu.make_async_copy(ht_ref.at[cur, _row_tile(t)], xs_ref.at[_row_tile(dest_ref[t * TOP_K + k])],
                                  sem.at[cur]).start(priority=k % 2)
        return carry
    lax.fori_loop(0, tm, issue, 0, unroll=2)

    def drain(slot):
        for k in range(TOP_K):
            pltpu.make_async_copy(ht_ref.at[slot], xs_ref.at[pl.ds(0, tm * ROW_TILE)], sem.at[slot]).wait()

    @pl.when(i > 0)
    def _():
        drain(1 - cur)

    @pl.when(i == n_steps - 1)
    def _():
        drain(cur)

    n_exp = cnt_ref.shape[0]
    per_step = pl.cdiv(n_exp, n_steps)
    sizes = [1 << s for s in reversed(range(EXPERT_ROWS.bit_length()))]

    def pad_copies(step, j, wait):
        e = jnp.minimum(step * per_step + j, n_exp - 1)
        n_pad = jnp.where(step * per_step + j < n_exp, (EXPERT_ROWS - cnt_ref[e] % EXPERT_ROWS) % EXPERT_ROWS, 0)
        row = pstart_ref[e] + cnt_ref[e]
        for size in sizes:
            @pl.when(n_pad & size != 0)
            def _(row=row, size=size):
                c = pltpu.make_async_copy(
                    zero_ref.at[pl.ds(0, size * ROW_TILE)],
                    xs_ref.at[pl.ds(pl.multiple_of(row * ROW_TILE, ROW_TILE), size * ROW_TILE)], zsem)
                c.wait() if wait else c.start()
            row = row + (n_pad & size)

    def pad_start(j, carry):
        pad_copies(i, j, False)
        return carry

    def pad_wait_previous(j, carry):
        pad_copies(i - 1, j, True)
        return carry

    def pad_wait(j, carry):
        pad_copies(i, j, True)
        return carry
    lax.fori_loop(0, per_step, pad_start, 0)

    @pl.when(i > 0)
    def _():
        lax.fori_loop(0, per_step, pad_wait_previous, 0)

    @pl.when(i == n_steps - 1)
    def _():
        lax.fori_loop(0, per_step, pad_wait, 0)
        _zero_tail_blocks(zero_ref, xs_ref, nblk_ref[0], tsem, True)


def _zero_tail_blocks(zero_ref, out_ref, n_blk, sem, wait):
    trows = zero_ref.shape[0]

    def body(b, carry):
        c = pltpu.make_async_copy(zero_ref, out_ref.at[pl.ds(pl.multiple_of(b * trows, trows), trows)], sem)
        c.wait() if wait else c.start()
        return carry
    lax.fori_loop(n_blk, out_ref.shape[0] // trows, body, 0)


def _dispatch(h, dest, pad_start, counts, n_blk):
    n_tok, d = h.shape
    n_exp = counts.shape[0]
    assert d == 2 * ROW_TILE * LANES
    tm = MOE_TOKEN_TILE
    n_rows = (pl.cdiv(n_tok * TOP_K, EXPERT_ROWS) + n_exp) * EXPERT_ROWS
    grid_spec = pltpu.PrefetchScalarGridSpec(
        num_scalar_prefetch=3,
        grid=(n_tok // tm,),
        in_specs=[pl.BlockSpec((TOP_K * tm,), lambda i, *_: (i,), memory_space=pltpu.SMEM),
                  pl.BlockSpec((tm, d), lambda i, *_: (i, 0))],
        out_specs=pl.BlockSpec(memory_space=pl.ANY),
        scratch_shapes=[pltpu.VMEM((2, tm * ROW_TILE, LANES), U32),
                        pltpu.VMEM((EXPERT_ROWS * ROW_TILE, LANES), U32),
                        pltpu.SemaphoreType.DMA((2,)), pltpu.SemaphoreType.DMA(()), pltpu.SemaphoreType.DMA(())],
    )
    return pl.pallas_call(
        _dispatch_kernel,
        out_shape=jax.ShapeDtypeStruct((n_rows * ROW_TILE, LANES), U32),
        grid_spec=grid_spec,
        compiler_params=_cparams(("arbitrary",), 32),
        name="dispatch",
    )(pad_start, counts, n_blk, dest, h)


def _experts_kernel(bstart_ref, bend_ref, nblk_ref, xs_ref, wgu_hbm, wdn_hbm, ys_ref,
                    xbuf, ybuf, act_ref, wgu_f32, wdn_f32, wgu_bf, wdn_bf, zero_ref, xsem, ysem, wsem, zsem):
    e = pl.program_id(0)
    n_blk = nblk_ref[0]
    trows = xbuf.shape[1]
    rows = trows // ROW_TILE
    ff = wdn_bf.shape[0]
    b0 = bstart_ref[e]
    b1 = bend_ref[e]

    def block_rows(b):
        return pl.ds(pl.multiple_of(b * trows, trows), trows)

    def x_copy(b):
        slot = b % EXPERT_RING
        return pltpu.make_async_copy(xs_ref.at[block_rows(b)], xbuf.at[slot], xsem.at[slot])

    def y_copy(b):
        slot = b % EXPERT_OUT_RING
        return pltpu.make_async_copy(ybuf.at[slot], ys_ref.at[block_rows(b)], ysem.at[slot])

    @pl.when(e == 0)
    def _():
        for i in range(EXPERT_RING):
            @pl.when(i < n_blk)
            def _():
                x_copy(i).start(priority=1)
        zero_ref[...] = jnp.zeros_like(zero_ref)
        _zero_tail_blocks(zero_ref, ys_ref, n_blk, zsem, False)

    n_exp = pl.num_programs(0)
    wslot = e % EXPERT_WEIGHT_RING

    def w_copies(x, slot):
        return (pltpu.make_async_copy(wgu_hbm.at[x], wgu_f32.at[slot], wsem.at[0, slot]),
                pltpu.make_async_copy(wdn_hbm.at[x], wdn_f32.at[slot], wsem.at[1, slot]))

    @pl.when(e == 0)
    def _():
        for i in range(EXPERT_WEIGHT_RING):
            @pl.when(i < n_exp)
            def _():
                for c in w_copies(i, i):
                    c.start()

    for c in w_copies(e, wslot):
        c.wait()
    wgu_bf[...] = wgu_f32[wslot].astype(BF16)
    wdn_bf[...] = wdn_f32[wslot].astype(BF16)

    @pl.when(e + EXPERT_WEIGHT_RING < n_exp)
    def _():
        for c in w_copies(e + EXPERT_WEIGHT_RING, wslot):
            c.start()

    def up_wait(blocks):
        for b in blocks:
            x_copy(b).wait()

    def up_compute(blocks):
        for b in blocks:
            xb = _unpack_bf16_pairs(_row_tile_words(xbuf, (b % EXPERT_RING,), rows)).astype(BF16)
            gu = jnp.dot(xb, wgu_bf[...], preferred_element_type=F32)
            act_ref[b - b0] = (jax.nn.silu(gu[:, :ff]) * gu[:, ff:]).astype(BF16)

    def up_start(blocks):
        for b in blocks:
            @pl.when(b + EXPERT_RING < n_blk)
            def _():
                x_copy(b + EXPERT_RING).start(priority=1)

    def down_wait(blocks):
        for b in blocks:
            @pl.when(b >= EXPERT_OUT_RING)
            def _():
                y_copy(b - EXPERT_OUT_RING).wait()

    def down_compute(blocks):
        for b in blocks:
            _to_row_tiles(ybuf, b % EXPERT_OUT_RING,
                          jnp.dot(act_ref[b - b0], wdn_bf[...], preferred_element_type=F32))

    def down_start(blocks):
        for b in blocks:
            y_copy(b).start(priority=1)
        last = blocks[-1]

        @pl.when(last == n_blk - 1)
        def _():
            for i in range(EXPERT_OUT_RING):
                @pl.when(last >= i)
                def _():
                    y_copy(last - i).wait()

    def up(blocks):
        up_wait(blocks)
        up_compute(blocks)
        up_start(blocks)

    def down(blocks):
        down_wait(blocks)
        down_compute(blocks)
        down_start(blocks)

    n_mine = b1 - b0

    def run_groups(fn):
        def body(p, carry):
            fn(tuple(b0 + EXPERT_GROUP * p + j for j in range(EXPERT_GROUP)))
            return carry
        lax.fori_loop(0, n_mine // EXPERT_GROUP, body, 0)
        size = EXPERT_GROUP // 2
        while size >= 1:
            @pl.when(n_mine & size != 0)
            def _(size=size):
                start = b0 + (n_mine // (2 * size)) * (2 * size)
                fn(tuple(start + j for j in range(size)))
            size //= 2

    @pl.when(n_mine == EXPERT_GROUP)
    def _():
        blocks = tuple(b0 + j for j in range(EXPERT_GROUP))
        up_wait(blocks)
        down_wait(blocks)
        up_compute(blocks)
        down_compute(blocks)
        up_start(blocks)
        down_start(blocks)

    @pl.when(n_mine != EXPERT_GROUP)
    def _():
        run_groups(up)
        run_groups(down)

    @pl.when(e == n_exp - 1)
    def _():
        _zero_tail_blocks(zero_ref, ys_ref, n_blk, zsem, True)


def _expert_blocks(counts):
    blocks = (counts + EXPERT_ROWS - 1) // EXPERT_ROWS
    bend = jnp.cumsum(blocks)
    bstart = bend - blocks
    i32 = lambda a: a.astype(jnp.int32)
    return i32(bstart), i32(bend), i32(bend[-1]).reshape(1), i32(bstart * EXPERT_ROWS)


def _experts(xs, bstart, bend, n_blk, w_gu, w_down, n_tok):
    n_exp, d, ff2 = w_gu.shape
    ff = w_down.shape[1]
    n_rows = xs.shape[0] // ROW_TILE
    assert n_rows % EXPERT_ROWS == 0 and d == 2 * ROW_TILE * LANES
    max_blocks = pl.cdiv(n_tok, EXPERT_ROWS)
    grid_spec = pltpu.PrefetchScalarGridSpec(
        num_scalar_prefetch=3,
        grid=(n_exp,),
        in_specs=[pl.BlockSpec(memory_space=pl.ANY), pl.BlockSpec(memory_space=pl.ANY),
                  pl.BlockSpec(memory_space=pl.ANY)],
        out_specs=pl.BlockSpec(memory_space=pl.ANY),
        scratch_shapes=[pltpu.VMEM((EXPERT_RING, EXPERT_ROWS * ROW_TILE, LANES), U32),
                        pltpu.VMEM((EXPERT_OUT_RING, EXPERT_ROWS * ROW_TILE, LANES), U32),
                        pltpu.VMEM((max_blocks, EXPERT_ROWS, ff), BF16),
                        pltpu.VMEM((EXPERT_WEIGHT_RING, d, ff2), F32),
                        pltpu.VMEM((EXPERT_WEIGHT_RING, ff, d), F32),
                        pltpu.VMEM((d, ff2), BF16), pltpu.VMEM((ff, d), BF16),
                        pltpu.VMEM((EXPERT_ROWS * ROW_TILE, LANES), U32),
                        pltpu.SemaphoreType.DMA((EXPERT_RING,)),
                        pltpu.SemaphoreType.DMA((EXPERT_OUT_RING,)),
                        pltpu.SemaphoreType.DMA((2, EXPERT_WEIGHT_RING)),
                        pltpu.SemaphoreType.DMA(())],
    )
    return pl.pallas_call(
        _experts_kernel,
        out_shape=jax.ShapeDtypeStruct(xs.shape, U32),
        grid_spec=grid_spec,
        compiler_params=_cparams(("arbitrary",), 48),
        name="experts",
    )(bstart, bend, n_blk, xs, w_gu, w_down)


def _combine_kernel(dest_ref, dnext_ref, gate_ref, h_ref, ys_ref, wgu_ref, wdn_ref, g_ref, b_ref, o_ref,
                    buf_ref, routed_ref, sem, *, alpha):
    tm = h_ref.shape[0]
    i = pl.program_id(0)
    cur = i % 2
    chunk = SUBLANES

    def issue(d_ref, slot, t):
        for k in range(TOP_K):
            pltpu.make_async_copy(ys_ref.at[_row_tile(d_ref[t * TOP_K + k])], buf_ref.at[slot, k, _row_tile(t)],
                                  sem.at[slot]).start(priority=k % 2)

    @pl.when(i == 0)
    def _():
        def first(t, carry):
            issue(dest_ref, 0, t)
            return carry
        lax.fori_loop(0, tm, first, 0, unroll=2)

    for k in range(TOP_K):
        pltpu.make_async_copy(ys_ref.at[pl.ds(0, tm * ROW_TILE)], buf_ref.at[cur, k], sem.at[cur]).wait()

    def weighted_sum(c):
        tok = pl.ds(pl.multiple_of(c * chunk, chunk), chunk)
        gate = gate_ref[tok, :]
        total = None
        for k in range(TOP_K):
            words = jnp.concatenate(
                [buf_ref[cur, k, pl.ds(pl.multiple_of(c * (chunk * ROW_TILE), chunk * ROW_TILE) + j, chunk,
                                       stride=ROW_TILE), :] for j in range(ROW_TILE)], axis=1)
            term = gate[:, k:k + 1] * _unpack_bf16_pairs(words)
            total = term if total is None else total + term
        routed_ref[tok, :] = total

    @pl.when(i + 1 < pl.num_programs(0))
    def _():
        def body(c, carry):
            for t in range(chunk):
                issue(dnext_ref, 1 - cur, c * chunk + t)
            weighted_sum(c)
            return carry
        lax.fori_loop(0, tm // chunk, body, 0)

    @pl.when(i + 1 == pl.num_programs(0))
    def _():
        def body(c, carry):
            weighted_sum(c)
            return carry
        lax.fori_loop(0, tm // chunk, body, 0)

    h = h_ref[...]
    ff = wdn_ref.shape[0]
    gu = jnp.dot(h.astype(BF16), wgu_ref[...], preferred_element_type=F32)
    act = (jax.nn.silu(gu[:, :ff]) * gu[:, ff:]).astype(BF16)
    acc = alpha * h + jnp.dot(act, wdn_ref[...], preferred_element_type=F32) + routed_ref[...]
    o_ref[...] = _layer_norm(acc, g_ref[...], b_ref[...])


def _combine(h, ys, dest, gate_t, shared_w_gu, shared_w_down, ln_g, ln_b, alpha):
    n_tok, d = h.shape
    tm = MOE_TOKEN_TILE
    n_tiles = n_tok // tm
    full = lambda a: pl.BlockSpec(a.shape, lambda i: (0,) * a.ndim)
    wgu = shared_w_gu.astype(BF16)
    wdn = shared_w_down.astype(BF16)
    g2 = ln_g.reshape(1, d)
    b2 = ln_b.reshape(1, d)
    return pl.pallas_call(
        functools.partial(_combine_kernel, alpha=alpha),
        out_shape=jax.ShapeDtypeStruct((n_tok, d), F32),
        grid=(n_tiles,),
        in_specs=[pl.BlockSpec((TOP_K * tm,), lambda i: (i,), memory_space=pltpu.SMEM),
                  pl.BlockSpec((TOP_K * tm,), lambda i: (jnp.minimum(i + 1, n_tiles - 1),),
                               memory_space=pltpu.SMEM),
                  pl.BlockSpec((tm, TOP_K), lambda i: (i, 0)),
                  pl.BlockSpec((tm, d), lambda i: (i, 0)),
                  pl.BlockSpec(memory_space=pl.ANY),
                  full(wgu), full(wdn), full(g2), full(b2)],
        out_specs=pl.BlockSpec((tm, d), lambda i: (i, 0)),
        scratch_shapes=[pltpu.VMEM((2, TOP_K, tm * ROW_TILE, LANES), U32), pltpu.VMEM((tm, d), F32),
                        pltpu.SemaphoreType.DMA((2,))],
        compiler_params=_cparams(("arbitrary",), 48),
        name="combine",
    )(dest, dest, gate_t, h, ys, wgu, wdn, g2, b2)


def _moe(h, router_w, router_bias, w_gu, w_down, shared_w_gu, shared_w_down, ln_g, ln_b, alpha):
    top_e, gate, rank, cnt = _router(h, router_w, router_bias)
    counts = cnt[:, 0].astype(jnp.int32)
    bstart, bend, n_blk, pad_start = _expert_blocks(counts)
    dest = _dest(top_e, rank, pad_start)
    dest_tiles = dest.T.reshape(-1)
    xs = _dispatch(h, dest_tiles, pad_start, counts, n_blk)
    ys = _experts(xs, bstart, bend, n_blk, w_gu, w_down, h.shape[0])
    return _combine(h, ys, dest_tiles, gate.T, shared_w_gu, shared_w_down, ln_g, ln_b, alpha)


def kernel(x, w_in, att_norm_g, lam_re, lam_im, log_step, b_re, b_im, c_re, c_im, d_skip, w_glu, b_glu,
           ssm_norm_g, w_out, ln1_g, ln1_b, router_w, router_bias, w_gu, w_down, shared_w_gu,
           shared_w_down, ln2_g, ln2_b):
    bsz, seq, d = x.shape
    depth = w_in.shape[0]
    alpha = (2 * depth) ** 0.25
    h = x.reshape(bsz * seq, d)
    for i in range(depth):
        proj = _inproj(h, w_in[i].astype(BF16), seq)
        o_att = _attention(proj, bsz, seq)
        y_ssm = _s5(proj.reshape(bsz, seq, -1), 3 * ATT_WIDTH, lam_re[i], lam_im[i], log_step[i],
                    b_re[i], b_im[i], c_re[i], c_im[i], d_skip[i])
        h = _mixout(o_att, y_ssm.reshape(bsz * seq, -1), h, w_glu[i], b_glu[i], att_norm_g[i],
                    ssm_norm_g[i], w_out[i], ln1_g[i], ln1_b[i], alpha)
        h = _moe(h, router_w[i], router_bias[i], w_gu[i], w_down[i], shared_w_gu[i], shared_w_down[i],
                 ln2_g[i], ln2_b[i], alpha)
    return h.reshape(bsz, seq, d)
```

```python
import functools

import jax
import jax.numpy as jnp
from jax import lax
from jax.experimental import pallas as pl
from jax.experimental.pallas import tpu as pltpu

F32 = jnp.float32
BF16 = jnp.bfloat16
U32 = jnp.uint32

ATT_HEADS = 8
HEAD_DIM = 64
ATT_WIDTH = ATT_HEADS * HEAD_DIM
SSM_CH = 16
SSM_STATE = 64
S5_SLAB = 64
ROPE_THETA = 500000.0
ROT_DIM = HEAD_DIM // 4
DILATIONS = (1, 4, 16)
ATT_BLOCK = 128
ATT_GROUP = 16
TOP_K = 8
N_EXPERT_GROUPS = 8
TOPK_GROUPS = 4
ROUTED_SCALE = 2.5
LN_EPS = 1e-5
RMS_EPS = 1e-6

LANES = 128
SUBLANES = 8
EXPERT_ROWS = 144
MOE_TOKEN_TILE = 512
EXPERT_RING = 16
EXPERT_OUT_RING = 16
EXPERT_GROUP = 4
EXPERT_WEIGHT_RING = 4
ROW_TILE = 4
NEG_INF = float("-inf")


def _cparams(sem, vmem_mb):
    return pltpu.CompilerParams(dimension_semantics=sem, vmem_limit_bytes=vmem_mb * 1024 * 1024)


def _inproj_kernel(x_ref, w_ref, cos_ref, sa_ref, sb_ref, o_ref, *, n_rot_cols):
    xb = x_ref[...].astype(BF16)
    cosf = cos_ref[...]
    sa = sa_ref[...]
    sb = sb_ref[...]
    width = o_ref.shape[1]
    chunk = 512
    for c in range(width // chunk):
        r = jnp.dot(xb, w_ref[:, c * chunk:(c + 1) * chunk], preferred_element_type=F32)
        if c * chunk < n_rot_cols:
            parts = []
            for s in range(chunk // LANES):
                t = r[:, s * LANES:(s + 1) * LANES]
                parts.append(t * cosf + pltpu.roll(t, LANES - ROT_DIM // 2, 1) * sa
                             + pltpu.roll(t, ROT_DIM // 2, 1) * sb)
            r = jnp.concatenate(parts, axis=1)
        o_ref[:, c * chunk:(c + 1) * chunk] = r


def _rope_lane_tables(seq):
    half = ROT_DIM // 2
    inv_freq = jnp.power(jnp.float32(ROPE_THETA), -jnp.arange(half, dtype=F32) / half)
    ang = jnp.arange(seq, dtype=F32)[:, None] * inv_freq[None, :]
    cos, sin = jnp.cos(ang), jnp.sin(ang)
    rest = HEAD_DIM - ROT_DIM
    cos_h = jnp.concatenate([cos, cos, jnp.ones((seq, rest), F32)], axis=1)
    sa_h = jnp.concatenate([-sin, jnp.zeros((seq, half + rest), F32)], axis=1)
    sb_h = jnp.concatenate([jnp.zeros((seq, half), F32), sin, jnp.zeros((seq, rest), F32)], axis=1)
    rep = LANES // HEAD_DIM
    return tuple(jnp.tile(t, (1, rep)) for t in (cos_h, sa_h, sb_h))


def _inproj(x2d, w_in_bf, seq):
    n_tok, d = x2d.shape
    width = w_in_bf.shape[1]
    tm = 1024
    cosf, sa, sb = _rope_lane_tables(seq)
    tab_spec = pl.BlockSpec((tm, LANES), lambda i: (i % (seq // tm), 0))
    return pl.pallas_call(
        functools.partial(_inproj_kernel, n_rot_cols=2 * ATT_WIDTH),
        out_shape=jax.ShapeDtypeStruct((n_tok, width), F32),
        grid=(n_tok // tm,),
        in_specs=[pl.BlockSpec((tm, d), lambda i: (i, 0)),
                  pl.BlockSpec((d, width), lambda i: (0, 0)),
                  tab_spec, tab_spec, tab_spec],
        out_specs=pl.BlockSpec((tm, width), lambda i: (i, 0)),
        compiler_params=_cparams(("parallel",), 48),
        name="inproj",
    )(x2d, w_in_bf, cosf, sa, sb)


def _attn_kernel(q_ref, k_ref, v_ref, o_ref, qs_ref, ks_ref, vs_ref, tmp_ref, ob_ref, lb_ref, band_ref,
                 first_ref, *, seq):
    blk = ATT_BLOCK
    lane = lax.broadcasted_iota(jnp.int32, (1, LANES), 1)
    head0 = lane < HEAD_DIM
    scale = HEAD_DIM ** -0.5
    d1, d2 = DILATIONS[1], DILATIONS[2]
    assert DILATIONS[0] == 1 and d2 == d1 * d1
    seg = seq // d1
    sub = seg // d1

    qi = lax.broadcasted_iota(jnp.int32, (blk, 2 * blk), 0)
    kj = lax.broadcasted_iota(jnp.int32, (blk, 2 * blk), 1)
    dist = qi + blk - kj
    band_ref[...] = jnp.where((dist >= 0) & (dist <= blk), 0.0, NEG_INF)
    first_ref[...] = jnp.where((dist >= 0) & (kj >= blk), 0.0, NEG_INF)

    n_class = (1, d1, d2)
    class_len = (seq, seg, sub)
    base = [0]
    for c in range(len(DILATIONS)):
        base.append(base[c] + n_class[c] * (class_len[c] + blk))

    def kv_row0(c, g):
        return base[c] + g * (class_len[c] + blk)

    qs_ref[0] = (q_ref[...] * scale).astype(BF16)
    for a in range(d1):
        x = q_ref[pl.ds(a, seg, stride=d1), :] * scale
        tmp_ref[a * seg:(a + 1) * seg, :] = x
        qs_ref[1, a * seg:(a + 1) * seg, :] = x.astype(BF16)
    for g in range(d2):
        qs_ref[2, g * sub:(g + 1) * sub, :] = tmp_ref[pl.ds((g // d1) * seg + g % d1, sub, stride=d1),
                                                      :].astype(BF16)
    for src_ref, dst_ref in ((k_ref, ks_ref), (v_ref, vs_ref)):
        for c in range(len(DILATIONS)):
            for g in range(n_class[c]):
                dst_ref[kv_row0(c, g):kv_row0(c, g) + blk, :] = jnp.zeros((blk, LANES), BF16)
        dst_ref[kv_row0(0, 0) + blk:kv_row0(0, 0) + blk + seq, :] = src_ref[...].astype(BF16)
        for a in range(d1):
            x = src_ref[pl.ds(a, seg, stride=d1), :]
            tmp_ref[a * seg:(a + 1) * seg, :] = x
            dst_ref[kv_row0(1, a) + blk:kv_row0(1, a) + blk + seg, :] = x.astype(BF16)
        for g in range(d2):
            dst_ref[kv_row0(2, g) + blk:kv_row0(2, g) + blk + sub, :] = tmp_ref[
                pl.ds((g // d1) * seg + g % d1, sub, stride=d1), :].astype(BF16)

    def one_block(c, g, n, out_rows, bias_ref):
        q = qs_ref[c, pl.ds(aligned(g * class_len[c] + n * blk), blk), :]
        kv_rows = pl.ds(aligned(kv_row0(c, g) + n * blk), 2 * blk)
        kk = ks_ref[kv_rows, :]
        vv = vs_ref[kv_rows, :]
        outs = []
        lses = []
        for h in range(LANES // HEAD_DIM):
            hm = head0 if h == 0 else jnp.logical_not(head0)
            qh = jnp.where(hm, q, jnp.zeros_like(q))
            s = lax.dot_general(qh, kk, (((1,), (1,)), ((), ())), preferred_element_type=F32)
            s = s + bias_ref[...]
            m = jnp.max(s, axis=-1, keepdims=True)
            p = jnp.exp(s - m)
            den = jnp.sum(p, axis=-1, keepdims=True)
            outs.append(jnp.dot(p.astype(BF16), vv, preferred_element_type=F32) / den)
            lses.append(m + jnp.log(den))
        ob_ref[c, out_rows, :] = jnp.where(head0, outs[0], outs[1])
        lb_ref[c, out_rows, :] = jnp.where(head0, lses[0], lses[1])

    def run_blocks(n_blocks, fn):
        group = max(g for g in range(1, ATT_GROUP + 1) if n_blocks % g == 0)
        if n_blocks == group:
            for g in range(group):
                fn(g)
            return
        def body(it, carry):
            for g in range(group):
                fn(it * group + g)
            return carry
        lax.fori_loop(0, n_blocks // group, body, 0)

    def aligned(x):
        return x if isinstance(x, int) else pl.multiple_of(x, blk)

    one_block(0, 0, 0, pl.ds(0, blk), first_ref)
    run_blocks(seq // blk - 1,
               lambda i: one_block(0, 0, i + 1, pl.ds(aligned((i + 1) * blk), blk), band_ref))

    nb1 = seg // blk
    run_blocks(d1, lambda a: one_block(1, a, 0, pl.ds(a, blk, stride=d1), first_ref))
    def later1(i):
        a = i // (nb1 - 1)
        n = i - a * (nb1 - 1) + 1
        one_block(1, a, n, pl.ds(a + n * (d1 * blk), blk, stride=d1), band_ref)
    run_blocks(d1 * (nb1 - 1), later1)

    assert sub == blk
    def only2(g):
        a = g // d1
        one_block(2, g, 0, pl.ds(a + d1 * (g - a * d1), blk, stride=d2), first_ref)
    run_blocks(d2, only2)

    rc = 256
    def merge(i, carry):
        sl = pl.ds(pl.multiple_of(i * rc, rc), rc)
        l0 = lb_ref[0, sl, :]
        l1 = lb_ref[1, sl, :]
        l2 = lb_ref[2, sl, :]
        mx = jnp.maximum(jnp.maximum(l0, l1), l2)
        e0 = jnp.exp(l0 - mx)
        e1 = jnp.exp(l1 - mx)
        e2 = jnp.exp(l2 - mx)
        o_ref[sl, :] = ((e0 * ob_ref[0, sl, :] + e1 * ob_ref[1, sl, :] + e2 * ob_ref[2, sl, :])
                        / (e0 + e1 + e2))
        return carry
    lax.fori_loop(0, seq // rc, merge, 0)


def _attention(proj, bsz, seq):
    n_tok = proj.shape[0]
    pairs = ATT_WIDTH // LANES
    assert seq % (ATT_BLOCK * max(DILATIONS)) == 0
    kv_rows = sum(seq + d * ATT_BLOCK for d in DILATIONS)
    blk = (seq, LANES)
    return pl.pallas_call(
        functools.partial(_attn_kernel, seq=seq),
        out_shape=jax.ShapeDtypeStruct((n_tok, ATT_WIDTH), F32),
        grid=(bsz, pairs),
        in_specs=[pl.BlockSpec(blk, lambda b, h: (b, h)),
                  pl.BlockSpec(blk, lambda b, h: (b, pairs + h)),
                  pl.BlockSpec(blk, lambda b, h: (b, 2 * pairs + h))],
        out_specs=pl.BlockSpec(blk, lambda b, h: (b, h)),
        scratch_shapes=[pltpu.VMEM((len(DILATIONS), seq, LANES), BF16),
                        pltpu.VMEM((kv_rows, LANES), BF16),
                        pltpu.VMEM((kv_rows, LANES), BF16),
                        pltpu.VMEM((seq, LANES), F32),
                        pltpu.VMEM((len(DILATIONS), seq, LANES), F32),
                        pltpu.VMEM((len(DILATIONS), seq, LANES), F32),
                        pltpu.VMEM((ATT_BLOCK, 2 * ATT_BLOCK), F32),
                        pltpu.VMEM((ATT_BLOCK, 2 * ATT_BLOCK), F32)],
        compiler_params=_cparams(("parallel", "parallel"), 40),
        name="attn",
    )(proj, proj, proj)


def _s5_kernel(u_ref, bm_ref, lam_ref, cm_ref, dk_ref, o_ref, us_ref, st_ref, ys_ref, carry_ref, *, tc):
    bsz = u_ref.shape[0]
    half = st_ref.shape[1] // 2
    slab_rows = S5_SLAB * bsz
    n_slabs = tc // S5_SLAB

    @pl.when(pl.program_id(1) == 0)
    def _():
        carry_ref[...] = jnp.zeros_like(carry_ref)

    for b in range(bsz):
        us_ref[pl.ds(b, tc, stride=bsz), :] = u_ref[b]

    bm = bm_ref[0]
    cm = cm_ref[0]
    lam = lam_ref[0]
    lam_re = lam[:, :half]
    lam_im = lam[:, half:]

    def slab(s):
        return slice(s * slab_rows, (s + 1) * slab_rows)

    def project_in(s):
        st_ref[slab(s), :] = jnp.dot(us_ref[slab(s), :].astype(BF16), bm, preferred_element_type=F32)

    def project_out(s):
        ys_ref[slab(s), :] = jnp.dot(st_ref[slab(s), :].astype(BF16), cm, preferred_element_type=F32)

    def scan(s, xr, xi):
        for t in range(S5_SLAB):
            sl = slice(s * slab_rows + t * bsz, s * slab_rows + (t + 1) * bsz)
            xr, xi = (lam_re * xr - lam_im * xi + st_ref[sl, :half],
                      lam_re * xi + lam_im * xr + st_ref[sl, half:])
            st_ref[sl, :half] = xr
            st_ref[sl, half:] = xi
        return xr, xi

    xr, xi = carry_ref[:, :half], carry_ref[:, half:]
    project_in(0)
    for s in range(n_slabs):
        if s + 1 < n_slabs:
            project_in(s + 1)
        xr, xi = scan(s, xr, xi)
        if s >= 1:
            project_out(s - 1)
    project_out(n_slabs - 1)
    carry_ref[:, :half] = xr
    carry_ref[:, half:] = xi

    dk = dk_ref[...]
    for b in range(bsz):
        o_ref[b] = ys_ref[pl.ds(b, tc, stride=bsz), :] + dk * u_ref[b]


def _s5_params(lam_re, lam_im, log_step, b_re, b_im, c_re, c_im, bsz):
    groups = lam_re.shape[0]
    gpc = LANES // SSM_CH
    n_chunks = groups // gpc
    lam = lax.complex(lam_re.astype(F32), lam_im.astype(F32))
    step = jnp.exp(log_step.astype(F32))[:, None]
    lam_bar = jnp.exp(lam * step)
    bmat = lax.complex(b_re.astype(F32), b_im.astype(F32))
    b_bar = ((lam_bar - 1.0) / lam)[..., None] * bmat
    eye = jnp.eye(gpc, dtype=F32)

    def block_diag_in(t):
        t = t.reshape(n_chunks, gpc, SSM_STATE, SSM_CH)
        return jnp.einsum('ngpc,gh->ngchp', t, eye).reshape(n_chunks, gpc * SSM_CH, gpc * SSM_STATE)

    def block_diag_out(t):
        t = t.reshape(n_chunks, gpc, SSM_CH, SSM_STATE)
        return jnp.einsum('ngcp,gh->ngphc', t, eye).reshape(n_chunks, gpc * SSM_STATE, gpc * SSM_CH)

    bm = jnp.concatenate([block_diag_in(b_bar.real), block_diag_in(b_bar.imag)], axis=2).astype(BF16)
    cm = jnp.concatenate([block_diag_out(c_re.astype(F32)), block_diag_out(-c_im.astype(F32))],
                         axis=1).astype(BF16)
    lam_row = jnp.concatenate([lam_bar.real.reshape(n_chunks, gpc * SSM_STATE),
                               lam_bar.imag.reshape(n_chunks, gpc * SSM_STATE)], axis=1)
    lam_t = jnp.broadcast_to(lam_row[:, None, :], (n_chunks, bsz, 2 * gpc * SSM_STATE))
    return bm, lam_t, cm, n_chunks


def _s5(proj3, u_col0, lam_re, lam_im, log_step, b_re, b_im, c_re, c_im, d_skip):
    bsz, seq, _ = proj3.shape
    assert bsz == SUBLANES
    bm, lam_t, cm, n_chunks = _s5_params(lam_re, lam_im, log_step, b_re, b_im, c_re, c_im, bsz)
    width = n_chunks * LANES
    tc = 512
    st_cols = bm.shape[2]
    ublk0 = u_col0 // LANES
    return pl.pallas_call(
        functools.partial(_s5_kernel, tc=tc),
        out_shape=jax.ShapeDtypeStruct((bsz, seq, width), F32),
        grid=(n_chunks, seq // tc),
        in_specs=[pl.BlockSpec((bsz, tc, LANES), lambda c, t: (0, t, ublk0 + c)),
                  pl.BlockSpec((1, LANES, st_cols), lambda c, t: (c, 0, 0)),
                  pl.BlockSpec((1, bsz, st_cols), lambda c, t: (c, 0, 0)),
                  pl.BlockSpec((1, st_cols, LANES), lambda c, t: (c, 0, 0)),
                  pl.BlockSpec((1, LANES), lambda c, t: (0, c))],
        out_specs=pl.BlockSpec((bsz, tc, LANES), lambda c, t: (0, t, c)),
        scratch_shapes=[pltpu.VMEM((tc * bsz, LANES), F32),
                        pltpu.VMEM((tc * bsz, st_cols), F32),
                        pltpu.VMEM((tc * bsz, LANES), F32),
                        pltpu.VMEM((bsz, st_cols), F32)],
        compiler_params=_cparams(("arbitrary", "arbitrary"), 48),
        name="s5",
    )(proj3, bm, lam_t, cm, d_skip.reshape(1, width).astype(F32))


def _layer_norm(v, g, b):
    mu = jnp.mean(v, axis=-1, keepdims=True)
    var = jnp.mean(jnp.square(v - mu), axis=-1, keepdims=True)
    return (v - mu) * lax.rsqrt(var + LN_EPS) * g + b


def _rms_norm(v, g):
    return v * lax.rsqrt(jnp.mean(jnp.square(v), axis=-1, keepdims=True) + RMS_EPS) * g


def _mixout_kernel(att_ref, ssm_ref, x_ref, wglu_ref, bglu_ref, ag_ref, sg_ref, wout_ref, g_ref, b_ref,
                   o_ref, *, alpha):
    y = jax.nn.gelu(ssm_ref[...])
    z = jnp.dot(y.astype(BF16), wglu_ref[...], preferred_element_type=F32) + bglu_ref[...]
    o_ssm = y * jax.nn.sigmoid(z)
    a = _rms_norm(att_ref[...], ag_ref[...]).astype(BF16)
    s = _rms_norm(o_ssm, sg_ref[...]).astype(BF16)
    wa = att_ref.shape[1]
    mix = (jnp.dot(a, wout_ref[:wa, :], preferred_element_type=F32)
           + jnp.dot(s, wout_ref[wa:, :], preferred_element_type=F32))
    o_ref[...] = _layer_norm(alpha * x_ref[...] + mix, g_ref[...], b_ref[...])


def _mixout(o_att, y_ssm, x2d, w_glu, b_glu, att_g, ssm_g, w_out, ln_g, ln_b, alpha):
    n_tok, d = x2d.shape
    wa = o_att.shape[1]
    ws = y_ssm.shape[1]
    tm = 1024
    row = lambda w: pl.BlockSpec((tm, w), lambda i: (i, 0))
    full = lambda a: pl.BlockSpec(a.shape, lambda i: (0,) * a.ndim)
    args = (o_att, y_ssm, x2d, w_glu.astype(BF16), b_glu.reshape(1, ws), att_g.reshape(1, wa),
            ssm_g.reshape(1, ws), w_out.astype(BF16), ln_g.reshape(1, d), ln_b.reshape(1, d))
    return pl.pallas_call(
        functools.partial(_mixout_kernel, alpha=alpha),
        out_shape=jax.ShapeDtypeStruct((n_tok, d), F32),
        grid=(n_tok // tm,),
        in_specs=[row(wa), row(ws), row(d)] + [full(a) for a in args[3:]],
        out_specs=row(d),
        compiler_params=_cparams(("parallel",), 48),
        name="mixout",
    )(*args)


def _split_bf16(v):
    hi = v.astype(BF16)
    lo = (v - hi.astype(F32)).astype(BF16)
    return hi, lo


def _router_kernel(h_ref, wt_ref, bias_ref, e_ref, g_ref, r_ref, cnt_ref, run_ref):
    tm = h_ref.shape[0]
    n_exp = wt_ref.shape[0]
    gsz = n_exp // N_EXPERT_GROUPS

    @pl.when(pl.program_id(0) == 0)
    def _():
        run_ref[...] = jnp.zeros_like(run_ref)

    w_hi, w_lo = _split_bf16(wt_ref[...])
    h_hi, h_lo = _split_bf16(h_ref[...])
    nt = (((1,), (1,)), ((), ()))
    logits = (lax.dot_general(w_hi, h_hi, nt, preferred_element_type=F32)
              + lax.dot_general(w_hi, h_lo, nt, preferred_element_type=F32)
              + lax.dot_general(w_lo, h_hi, nt, preferred_element_type=F32))
    scores = jax.nn.sigmoid(logits)
    choice = scores + bias_ref[:, 0:1]

    gio = lax.broadcasted_iota(jnp.int32, (gsz, tm), 0).astype(F32)
    gscore = []
    for g in range(N_EXPERT_GROUPS):
        cg = choice[g * gsz:(g + 1) * gsz, :]
        m1 = jnp.max(cg, axis=0, keepdims=True)
        i1 = jnp.min(jnp.where(cg == m1, gio, float(gsz)), axis=0, keepdims=True)
        m2 = jnp.max(jnp.where(gio == i1, NEG_INF, cg), axis=0, keepdims=True)
        gscore.append(m1 + m2)
    masked = []
    for g in range(N_EXPERT_GROUPS):
        beat = jnp.zeros((1, tm), F32)
        for o in range(N_EXPERT_GROUPS):
            if o == g:
                continue
            wins = (gscore[o] >= gscore[g]) if o < g else (gscore[o] > gscore[g])
            beat = beat + jnp.where(wins, 1.0, 0.0)
        keep = beat < float(TOPK_GROUPS)
        masked.append(jnp.where(keep, choice[g * gsz:(g + 1) * gsz, :], NEG_INF))
    cur = jnp.concatenate(masked, axis=0)

    eio = lax.broadcasted_iota(jnp.int32, (n_exp, tm), 0).astype(F32)
    idxs = []
    gates = []
    candidates = cur
    for _ in range(TOP_K):
        m = jnp.max(cur, axis=0, keepdims=True)
        idx = jnp.min(jnp.where(cur == m, eio, float(n_exp)), axis=0, keepdims=True)
        hit = eio == idx
        idxs.append(idx)
        gates.append(jnp.sum(jnp.where(hit, scores, 0.0), axis=0, keepdims=True))
        cur = jnp.where(hit, NEG_INF, cur)
    onehot = jnp.where(cur != candidates, 1.0, 0.0)
    gate = jnp.concatenate(gates, axis=0)
    gate = ROUTED_SCALE * gate / (jnp.sum(gate, axis=0, keepdims=True) + 1e-20)

    si = lax.broadcasted_iota(jnp.int32, (tm, tm), 0)
    ti = lax.broadcasted_iota(jnp.int32, (tm, tm), 1)
    upper = jnp.where(si < ti, 1.0, 0.0).astype(BF16)
    before = jnp.dot(onehot.astype(BF16), upper, preferred_element_type=F32) + run_ref[:, 0:1]
    ranks = [jnp.sum(jnp.where(eio == idx, before, 0.0), axis=0, keepdims=True) for idx in idxs]

    e_ref[...] = jnp.concatenate(idxs, axis=0).astype(jnp.int32)
    g_ref[...] = gate
    r_ref[...] = jnp.concatenate(ranks, axis=0).astype(jnp.int32)
    run_ref[...] = run_ref[...] + jnp.sum(onehot, axis=1, keepdims=True)
    cnt_ref[...] = run_ref[...]


def _router(h, router_w, router_bias):
    n_tok, d = h.shape
    n_exp = router_w.shape[1]
    tm = 512
    wt = router_w.astype(F32).T
    bias = jnp.broadcast_to(router_bias.astype(F32)[:, None], (n_exp, LANES))
    tok = pl.BlockSpec((TOP_K, tm), lambda i: (0, i))
    return pl.pallas_call(
        _router_kernel,
        out_shape=(jax.ShapeDtypeStruct((TOP_K, n_tok), jnp.int32),
                   jax.ShapeDtypeStruct((TOP_K, n_tok), F32),
                   jax.ShapeDtypeStruct((TOP_K, n_tok), jnp.int32),
                   jax.ShapeDtypeStruct((n_exp, LANES), F32)),
        grid=(n_tok // tm,),
        in_specs=[pl.BlockSpec((tm, d), lambda i: (i, 0)),
                  pl.BlockSpec((n_exp, d), lambda i: (0, 0)),
                  pl.BlockSpec((n_exp, LANES), lambda i: (0, 0))],
        out_specs=(tok, tok, tok, pl.BlockSpec((n_exp, LANES), lambda i: (0, 0))),
        scratch_shapes=[pltpu.VMEM((n_exp, LANES), F32)],
        compiler_params=_cparams(("arbitrary",), 32),
        name="router",
    )(h, wt, bias)


def _dest_kernel(e_ref, r_ref, st_ref, d_ref):
    n_exp = st_ref.shape[0]
    tm = e_ref.shape[1]
    eio = lax.broadcasted_iota(jnp.int32, (n_exp, tm), 0)
    start = st_ref[:, 0:1]
    rows = [jnp.sum(jnp.where(eio == e_ref[k:k + 1, :], start, 0.0), axis=0, keepdims=True)
            for k in range(TOP_K)]
    d_ref[...] = jnp.concatenate(rows, axis=0).astype(jnp.int32) + r_ref[...]


def _dest(top_e, rank, starts):
    n_tok = top_e.shape[1]
    n_exp = starts.shape[0]
    tm = 1024
    st = jnp.broadcast_to(starts.astype(F32)[:, None], (n_exp, LANES))
    tok = pl.BlockSpec((TOP_K, tm), lambda i: (0, i))
    return pl.pallas_call(
        _dest_kernel,
        out_shape=jax.ShapeDtypeStruct((TOP_K, n_tok), jnp.int32),
        grid=(n_tok // tm,),
        in_specs=[tok, tok, pl.BlockSpec((n_exp, LANES), lambda i: (0, 0))],
        out_specs=tok,
        compiler_params=_cparams(("parallel",), 32),
        name="dest",
    )(top_e, rank, st)


def _pack_bf16_pairs(val):
    half = val.shape[1] // 2
    lo = pltpu.bitcast(val[:, :half].astype(BF16).astype(F32), U32)
    hi = pltpu.bitcast(val[:, half:].astype(BF16).astype(F32), U32)
    return (lo >> 16) | (hi & jnp.uint32(0xFFFF0000))


def _unpack_bf16_pairs(words):
    lo = pltpu.bitcast(words << 16, F32)
    hi = pltpu.bitcast(words & jnp.uint32(0xFFFF0000), F32)
    return jnp.concatenate([lo, hi], axis=1)


def _to_row_tiles(dst_ref, slot, val):
    rows = val.shape[0]
    words = _pack_bf16_pairs(val)
    for j in range(ROW_TILE):
        dst_ref[slot, pl.ds(j, rows, stride=ROW_TILE), :] = words[:, j * LANES:(j + 1) * LANES]


def _row_tile_words(src_ref, idx, rows):
    return jnp.concatenate([src_ref[(*idx, pl.ds(j, rows, stride=ROW_TILE), slice(None))]
                            for j in range(ROW_TILE)], axis=1)


def _row_tile(r):
    return pl.ds(pl.multiple_of(r * ROW_TILE, ROW_TILE), ROW_TILE)


def _dispatch_kernel(pstart_ref, cnt_ref, nblk_ref, dest_ref, h_ref, xs_ref, ht_ref, zero_ref, sem, zsem, tsem):
    tm = h_ref.shape[0]
    i = pl.program_id(0)
    n_steps = pl.num_programs(0)
    cur = i % 2
    _to_row_tiles(ht_ref, cur, h_ref[...])

    @pl.when(i == 0)
    def _():
        zero_ref[...] = jnp.zeros_like(zero_ref)
        _zero_tail_blocks(zero_ref, xs_ref, nblk_ref[0], tsem, False)

    def issue(t, carry):
        for k in range(TOP_K):
            pltpu.make_async_copy(ht_ref.at[cur, _row_tile(t)], xs_ref.at[_row_tile(dest_ref[t * TOP_K + k])],
                                  sem.at[cur]).start(priority=k % 2)
        return carry
    lax.fori_loop(0, tm, issue, 0, unroll=2)

    def drain(slot):
        for k in range(TOP_K):
            pltpu.make_async_copy(ht_ref.at[slot], xs_ref.at[pl.ds(0, tm * ROW_TILE)], sem.at[slot]).wait()

    @pl.when(i > 0)
    def _():
        drain(1 - cur)

    @pl.when(i == n_steps - 1)
    def _():
        drain(cur)

    n_exp = cnt_ref.shape[0]
    per_step = pl.cdiv(n_exp, n_steps)
    sizes = [1 << s for s in reversed(range(EXPERT_ROWS.bit_length()))]

    def pad_copies(step, j, wait):
        e = jnp.minimum(step * per_step + j, n_exp - 1)
        n_pad = jnp.where(step * per_step + j < n_exp, (EXPERT_ROWS - cnt_ref[e] % EXPERT_ROWS) % EXPERT_ROWS, 0)
        row = pstart_ref[e] + cnt_ref[e]
        for size in sizes:
            @pl.when(n_pad & size != 0)
            def _(row=row, size=size):
                c = pltpu.make_async_copy(
                    zero_ref.at[pl.ds(0, size * ROW_TILE)],
                    xs_ref.at[pl.ds(pl.multiple_of(row * ROW_TILE, ROW_TILE), size * ROW_TILE)], zsem)
                c.wait() if wait else c.start()
            row = row + (n_pad & size)

    def pad_start(j, carry):
        pad_copies(i, j, False)
        return carry

    def pad_wait_previous(j, carry):
        pad_copies(i - 1, j, True)
        return carry

    def pad_wait(j, carry):
        pad_copies(i, j, True)
        return carry
    lax.fori_loop(0, per_step, pad_start, 0)

    @pl.when(i > 0)
    def _():
        lax.fori_loop(0, per_step, pad_wait_previous, 0)

    @pl.when(i == n_steps - 1)
    def _():
        lax.fori_loop(0, per_step, pad_wait, 0)
        _zero_tail_blocks(zero_ref, xs_ref, nblk_ref[0], tsem, True)


def _zero_tail_blocks(zero_ref, out_ref, n_blk, sem, wait):
    trows = zero_ref.shape[0]

    def body(b, carry):
        c = pltpu.make_async_copy(zero_ref, out_ref.at[pl.ds(pl.multiple_of(b * trows, trows), trows)], sem)
        c.wait() if wait else c.start()
        return carry
    lax.fori_loop(n_blk, out_ref.shape[0] // trows, body, 0)


def _dispatch(h, dest, pad_start, counts, n_blk):
    n_tok, d = h.shape
    n_exp = counts.shape[0]
    assert d == 2 * ROW_TILE * LANES
    tm = MOE_TOKEN_TILE
    n_rows = (pl.cdiv(n_tok * TOP_K, EXPERT_ROWS) + n_exp) * EXPERT_ROWS
    grid_spec = pltpu.PrefetchScalarGridSpec(
        num_scalar_prefetch=3,
        grid=(n_tok // tm,),
        in_specs=[pl.BlockSpec((TOP_K * tm,), lambda i, *_: (i,), memory_space=pltpu.SMEM),
                  pl.BlockSpec((tm, d), lambda i, *_: (i, 0))],
        out_specs=pl.BlockSpec(memory_space=pl.ANY),
        scratch_shapes=[pltpu.VMEM((2, tm * ROW_TILE, LANES), U32),
                        pltpu.VMEM((EXPERT_ROWS * ROW_TILE, LANES), U32),
                        pltpu.SemaphoreType.DMA((2,)), pltpu.SemaphoreType.DMA(()), pltpu.SemaphoreType.DMA(())],
    )
    return pl.pallas_call(
        _dispatch_kernel,
        out_shape=jax.ShapeDtypeStruct((n_rows * ROW_TILE, LANES), U32),
        grid_spec=grid_spec,
        compiler_params=_cparams(("arbitrary",), 32),
        name="dispatch",
    )(pad_start, counts, n_blk, dest, h)


def _experts_kernel(bstart_ref, bend_ref, nblk_ref, xs_ref, wgu_hbm, wdn_hbm, ys_ref,
                    xbuf, ybuf, act_ref, wgu_f32, wdn_f32, wgu_bf, wdn_bf, zero_ref, xsem, ysem, wsem, zsem):
    e = pl.program_id(0)
    n_blk = nblk_ref[0]
    trows = xbuf.shape[1]
    rows = trows // ROW_TILE
    ff = wdn_bf.shape[0]
    b0 = bstart_ref[e]
    b1 = bend_ref[e]

    def block_rows(b):
        return pl.ds(pl.multiple_of(b * trows, trows), trows)

    def x_copy(b):
        slot = b % EXPERT_RING
        return pltpu.make_async_copy(xs_ref.at[block_rows(b)], xbuf.at[slot], xsem.at[slot])

    def y_copy(b):
        slot = b % EXPERT_OUT_RING
        return pltpu.make_async_copy(ybuf.at[slot], ys_ref.at[block_rows(b)], ysem.at[slot])

    @pl.when(e == 0)
    def _():
        for i in range(EXPERT_RING):
            @pl.when(i < n_blk)
            def _():
                x_copy(i).start()
        zero_ref[...] = jnp.zeros_like(zero_ref)
        _zero_tail_blocks(zero_ref, ys_ref, n_blk, zsem, False)

    n_exp = pl.num_programs(0)
    wslot = e % EXPERT_WEIGHT_RING

    def w_copies(x, slot):
        return (pltpu.make_async_copy(wgu_hbm.at[x], wgu_f32.at[slot], wsem.at[0, slot]),
                pltpu.make_async_copy(wdn_hbm.at[x], wdn_f32.at[slot], wsem.at[1, slot]))

    @pl.when(e == 0)
    def _():
        for i in range(EXPERT_WEIGHT_RING):
            @pl.when(i < n_exp)
            def _():
                for c in w_copies(i, i):
                    c.start(priority=1)

    for c in w_copies(e, wslot):
        c.wait()
    wgu_bf[...] = wgu_f32[wslot].astype(BF16)
    wdn_bf[...] = wdn_f32[wslot].astype(BF16)

    @pl.when(e + EXPERT_WEIGHT_RING < n_exp)
    def _():
        for c in w_copies(e + EXPERT_WEIGHT_RING, wslot):
            c.start(priority=1)

    def up_wait(blocks):
        for b in blocks:
            x_copy(b).wait()

    def up_compute(blocks):
        for b in blocks:
            xb = _unpack_bf16_pairs(_row_tile_words(xbuf, (b % EXPERT_RING,), rows)).astype(BF16)
            gu = jnp.dot(xb, wgu_bf[...], preferred_element_type=F32)
            act_ref[b - b0] = (jax.nn.silu(gu[:, :ff]) * gu[:, ff:]).astype(BF16)

    def up_start(blocks):
        for b in blocks:
            @pl.when(b + EXPERT_RING < n_blk)
            def _():
                x_copy(b + EXPERT_RING).start()

    def down_wait(blocks):
        for b in blocks:
            @pl.when(b >= EXPERT_OUT_RING)
            def _():
                y_copy(b - EXPERT_OUT_RING).wait()

    def down_compute(blocks):
        for b in blocks:
            _to_row_tiles(ybuf, b % EXPERT_OUT_RING,
                          jnp.dot(act_ref[b - b0], wdn_bf[...], preferred_element_type=F32))

    def down_start(blocks):
        for b in blocks:
            y_copy(b).start(priority=1)
        last = blocks[-1]

        @pl.when(last == n_blk - 1)
        def _():
            for i in range(EXPERT_OUT_RING):
                @pl.when(last >= i)
                def _():
                    y_copy(last - i).wait()

    def up(blocks):
        up_wait(blocks)
        up_compute(blocks)
        up_start(blocks)

    def down(blocks):
        down_wait(blocks)
        down_compute(blocks)
        down_start(blocks)

    n_mine = b1 - b0

    def run_groups(fn):
        def body(p, carry):
            fn(tuple(b0 + EXPERT_GROUP * p + j for j in range(EXPERT_GROUP)))
            return carry
        lax.fori_loop(0, n_mine // EXPERT_GROUP, body, 0)
        size = EXPERT_GROUP // 2
        while size >= 1:
            @pl.when(n_mine & size != 0)
            def _(size=size):
                start = b0 + (n_mine // (2 * size)) * (2 * size)
                fn(tuple(start + j for j in range(size)))
            size //= 2

    @pl.when(n_mine == EXPERT_GROUP)
    def _():
        blocks = tuple(b0 + j for j in range(EXPERT_GROUP))
        up_wait(blocks)
        down_wait(blocks)
        up_compute(blocks)
        down_compute(blocks)
        up_start(blocks)
        down_start(blocks)

    @pl.when(n_mine != EXPERT_GROUP)
    def _():
        run_groups(up)
        run_groups(down)

    @pl.when(e == n_exp - 1)
    def _():
        _zero_tail_blocks(zero_ref, ys_ref, n_blk, zsem, True)


def _expert_blocks(counts):
    blocks = (counts + EXPERT_ROWS - 1) // EXPERT_ROWS
    bend = jnp.cumsum(blocks)
    bstart = bend - blocks
    i32 = lambda a: a.astype(jnp.int32)
    return i32(bstart), i32(bend), i32(bend[-1]).reshape(1), i32(bstart * EXPERT_ROWS)


def _experts(xs, bstart, bend, n_blk, w_gu, w_down, n_tok):
    n_exp, d, ff2 = w_gu.shape
    ff = w_down.shape[1]
    n_rows = xs.shape[0] // ROW_TILE
    assert n_rows % EXPERT_ROWS == 0 and d == 2 * ROW_TILE * LANES
    max_blocks = pl.cdiv(n_tok, EXPERT_ROWS)
    grid_spec = pltpu.PrefetchScalarGridSpec(
        num_scalar_prefetch=3,
        grid=(n_exp,),
        in_specs=[pl.BlockSpec(memory_space=pl.ANY), pl.BlockSpec(memory_space=pl.ANY),
                  pl.BlockSpec(memory_space=pl.ANY)],
        out_specs=pl.BlockSpec(memory_space=pl.ANY),
        scratch_shapes=[pltpu.VMEM((EXPERT_RING, EXPERT_ROWS * ROW_TILE, LANES), U32),
                        pltpu.VMEM((EXPERT_OUT_RING, EXPERT_ROWS * ROW_TILE, LANES), U32),
                        pltpu.VMEM((max_blocks, EXPERT_ROWS, ff), BF16),
                        pltpu.VMEM((EXPERT_WEIGHT_RING, d, ff2), F32),
                        pltpu.VMEM((EXPERT_WEIGHT_RING, ff, d), F32),
                        pltpu.VMEM((d, ff2), BF16), pltpu.VMEM((ff, d), BF16),
                        pltpu.VMEM((EXPERT_ROWS * ROW_TILE, LANES), U32),
                        pltpu.SemaphoreType.DMA((EXPERT_RING,)),
                        pltpu.SemaphoreType.DMA((EXPERT_OUT_RING,)),
                        pltpu.SemaphoreType.DMA((2, EXPERT_WEIGHT_RING)),
                        pltpu.SemaphoreType.DMA(())],
    )
    return pl.pallas_call(
        _experts_kernel,
        out_shape=jax.ShapeDtypeStruct(xs.shape, U32),
        grid_spec=grid_spec,
        compiler_params=_cparams(("arbitrary",), 48),
        name="experts",
    )(bstart, bend, n_blk, xs, w_gu, w_down)


def _combine_kernel(dest_ref, dnext_ref, gate_ref, h_ref, ys_ref, wgu_ref, wdn_ref, g_ref, b_ref, o_ref,
                    buf_ref, routed_ref, sem, *, alpha):
    tm = h_ref.shape[0]
    i = pl.program_id(0)
    cur = i % 2
    chunk = SUBLANES

    def issue(d_ref, slot, t):
        for k in range(TOP_K):
            pltpu.make_async_copy(ys_ref.at[_row_tile(d_ref[t * TOP_K + k])], buf_ref.at[slot, k, _row_tile(t)],
                                  sem.at[slot]).start(priority=k % 2)

    @pl.when(i == 0)
    def _():
        def first(t, carry):
            issue(dest_ref, 0, t)
            return carry
        lax.fori_loop(0, tm, first, 0, unroll=2)

    for k in range(TOP_K):
        pltpu.make_async_copy(ys_ref.at[pl.ds(0, tm * ROW_TILE)], buf_ref.at[cur, k], sem.at[cur]).wait()

    def weighted_sum(c):
        tok = pl.ds(pl.multiple_of(c * chunk, chunk), chunk)
        gate = gate_ref[tok, :]
        total = None
        for k in range(TOP_K):
            words = jnp.concatenate(
                [buf_ref[cur, k, pl.ds(pl.multiple_of(c * (chunk * ROW_TILE), chunk * ROW_TILE) + j, chunk,
                                       stride=ROW_TILE), :] for j in range(ROW_TILE)], axis=1)
            term = gate[:, k:k + 1] * _unpack_bf16_pairs(words)
            total = term if total is None else total + term
        routed_ref[tok, :] = total

    @pl.when(i + 1 < pl.num_programs(0))
    def _():
        def body(c, carry):
            for t in range(chunk):
                issue(dnext_ref, 1 - cur, c * chunk + t)
            weighted_sum(c)
            return carry
        lax.fori_loop(0, tm // chunk, body, 0)

    @pl.when(i + 1 == pl.num_programs(0))
    def _():
        def body(c, carry):
            weighted_sum(c)
            return carry
        lax.fori_loop(0, tm // chunk, body, 0)

    h = h_ref[...]
    ff = wdn_ref.shape[0]
    gu = jnp.dot(h.astype(BF16), wgu_ref[...], preferred_element_type=F32)
    act = (jax.nn.silu(gu[:, :ff]) * gu[:, ff:]).astype(BF16)
    acc = alpha * h + jnp.dot(act, wdn_ref[...], preferred_element_type=F32) + routed_ref[...]
    o_ref[...] = _layer_norm(acc, g_ref[...], b_ref[...])


def _combine(h, ys, dest, gate_t, shared_w_gu, shared_w_down, ln_g, ln_b, alpha):
    n_tok, d = h.shape
    tm = MOE_TOKEN_TILE
    n_tiles = n_tok // tm
    full = lambda a: pl.BlockSpec(a.shape, lambda i: (0,) * a.ndim)
    wgu = shared_w_gu.astype(BF16)
    wdn = shared_w_down.astype(BF16)
    g2 = ln_g.reshape(1, d)
    b2 = ln_b.reshape(1, d)
    return pl.pallas_call(
        functools.partial(_combine_kernel, alpha=alpha),
        out_shape=jax.ShapeDtypeStruct((n_tok, d), F32),
        grid=(n_tiles,),
        in_specs=[pl.BlockSpec((TOP_K * tm,), lambda i: (i,), memory_space=pltpu.SMEM),
                  pl.BlockSpec((TOP_K * tm,), lambda i: (jnp.minimum(i + 1, n_tiles - 1),),
                               memory_space=pltpu.SMEM),
                  pl.BlockSpec((tm, TOP_K), lambda i: (i, 0)),
                  pl.BlockSpec((tm, d), lambda i: (i, 0)),
                  pl.BlockSpec(memory_space=pl.ANY),
                  full(wgu), full(wdn), full(g2), full(b2)],
        out_specs=pl.BlockSpec((tm, d), lambda i: (i, 0)),
        scratch_shapes=[pltpu.VMEM((2, TOP_K, tm * ROW_TILE, LANES), U32), pltpu.VMEM((tm, d), F32),
                        pltpu.SemaphoreType.DMA((2,))],
        compiler_params=_cparams(("arbitrary",), 48),
        name="combine",
    )(dest, dest, gate_t, h, ys, wgu, wdn, g2, b2)


def _moe(h, router_w, router_bias, w_gu, w_down, shared_w_gu, shared_w_down, ln_g, ln_b, alpha):
    top_e, gate, rank, cnt = _router(h, router_w, router_bias)
    counts = cnt[:, 0].astype(jnp.int32)
    bstart, bend, n_blk, pad_start = _expert_blocks(counts)
    dest = _dest(top_e, rank, pad_start)
    dest_tiles = dest.T.reshape(-1)
    xs = _dispatch(h, dest_tiles, pad_start, counts, n_blk)
    ys = _experts(xs, bstart, bend, n_blk, w_gu, w_down, h.shape[0])
    return _combine(h, ys, dest_tiles, gate.T, shared_w_gu, shared_w_down, ln_g, ln_b, alpha)


def kernel(x, w_in, att_norm_g, lam_re, lam_im, log_step, b_re, b_im, c_re, c_im, d_skip, w_glu, b_glu,
           ssm_norm_g, w_out, ln1_g, ln1_b, router_w, router_bias, w_gu, w_down, shared_w_gu,
           shared_w_down, ln2_g, ln2_b):
    bsz, seq, d = x.shape
    depth = w_in.shape[0]
    alpha = (2 * depth) ** 0.25
    h = x.reshape(bsz * seq, d)
    for i in range(depth):
        proj = _inproj(h, w_in[i].astype(BF16), seq)
        o_att = _attention(proj, bsz, seq)
        y_ssm = _s5(proj.reshape(bsz, seq, -1), 3 * ATT_WIDTH, lam_re[i], lam_im[i], log_step[i],
                    b_re[i], b_im[i], c_re[i], c_im[i], d_skip[i])
        h = _mixout(o_att, y_ssm.reshape(bsz * seq, -1), h, w_glu[i], b_glu[i], att_norm_g[i],
                    ssm_norm_g[i], w_out[i], ln1_g[i], ln1_b[i], alpha)
        h = _moe(h, router_w[i], router_bias[i], w_gu[i], w_down[i], shared_w_gu[i], shared_w_down[i],
                 ln2_g[i], ln2_b[i], alpha)
    return h.reshape(bsz, seq, d)
```
